```python
import math
import jax
import jax.numpy as jnp
from jax import lax
import numpy as np

D_MODEL = 1024
BATCH = 8
SEQ = 4096
DEPTH = 4

N_MIXERS = 3
Q_BLOCK = 128
NORM_EPS = 1e-6
FFN_HIDDEN = -(-(8 * D_MODEL) // (3 * 256)) * 256

DN_HEAD_DIM = 128
DN_HEADS = D_MODEL // DN_HEAD_DIM
DN_WIDTH = DN_HEADS * DN_HEAD_DIM
DN_CONV = 4
DN_CHUNK = 64

SB_HEAD_DIM = 128
SB_HEADS = D_MODEL // SB_HEAD_DIM

MLA_HEADS = D_MODEL // 128
MLA_NOPE = 128
MLA_ROPE = 64
MLA_QK = MLA_NOPE + MLA_ROPE
MLA_V = 128
MLA_Q_RANK = 256
MLA_KV_RANK = 128
ROPE_THETA = 10000.0

kernel_name = 'hybrid_deltanet_stickbreak_mla_trunk'


def _rmsnorm(x, g):
    xf = x.astype(jnp.float32)
    y = xf * lax.rsqrt(jnp.mean(xf * xf, axis=-1, keepdims=True) + NORM_EPS)
    return (y * g.astype(jnp.float32)).astype(x.dtype)


def _l2norm(x):
    return x * lax.rsqrt(jnp.sum(x * x, axis=-1, keepdims=True) + NORM_EPS)


def _heads_first(x):
    return x.transpose(0, 2, 1, 3)


def _swiglu(h, w_gate_up, w_down):
    gate, up = jnp.split(h @ w_gate_up, 2, axis=-1)
    return (jax.nn.silu(gate) * up) @ w_down


def _causal_dwconv(u, w):
    k_width, t = w.shape[0], u.shape[1]
    up = jnp.pad(u, ((0, 0), (k_width - 1, 0), (0, 0)))
    y = up[:, 0:t] * w[0]
    for j in range(1, k_width):
        y = y + up[:, j:j + t] * w[j]
    return y


def _to_chunks(x):
    b, t, h = x.shape[:3]
    return jnp.moveaxis(x.reshape(b, t // DN_CHUNK, DN_CHUNK, h, *x.shape[3:]), 3, 1)


def _chunk_gated_delta_rule(q, k, v, g, beta):
    b, t, h, dk = q.shape
    dv = v.shape[-1]
    qc, kc, vc = _to_chunks(q), _to_chunks(k), _to_chunks(v)
    gc = jnp.cumsum(_to_chunks(g), axis=-1)
    bc = _to_chunks(beta)[..., None]
    idx = jnp.arange(DN_CHUNK)
    causal = idx[:, None] >= idx[None, :]
    strict = idx[:, None] > idx[None, :]
    diff = gc[..., :, None] - gc[..., None, :]
    decay = jnp.where(causal, jnp.exp(jnp.where(causal, diff, 0.0)), 0.0)
    k_beta = kc * bc
    lower = jnp.where(strict, jnp.einsum('bhncd,bhnsd->bhncs', k_beta, kc) * decay, 0.0)
    tmat = lower + jnp.eye(DN_CHUNK, dtype=lower.dtype)
    u = lax.linalg.triangular_solve(tmat, vc * bc, left_side=True, lower=True, unit_diagonal=True)
    w = lax.linalg.triangular_solve(tmat, k_beta * jnp.exp(gc)[..., None], left_side=True, lower=True, unit_diagonal=True)
    attn = jnp.einsum('bhncd,bhnsd->bhncs', qc, kc) * decay
    q_dec = qc * jnp.exp(gc)[..., None]
    k_dec = kc * jnp.exp(gc[..., -1:] - gc)[..., None]
    chunk_decay = jnp.exp(gc[..., -1])[..., None, None]

    def step(state, inp):
        u_i, w_i, q_i, k_i, a_i, d_i = inp
        v_new = u_i - jnp.einsum('bhcd,bhde->bhce', w_i, state)
        o_i = jnp.einsum('bhcd,bhde->bhce', q_i, state) + jnp.einsum('bhcs,bhse->bhce', a_i, v_new)
        state = state * d_i + jnp.einsum('bhcd,bhce->bhde', k_i, v_new)
        return state, o_i

    xs = tuple(jnp.moveaxis(a, 2, 0) for a in (u, w, q_dec, k_dec, attn, chunk_decay))
    state0 = jnp.zeros((b, h, dk, dv), q.dtype)
    _, o = lax.scan(step, state0, xs)
    return jnp.moveaxis(o, 0, 2).reshape(b, h, t, dv).transpose(0, 2, 1, 3)


def _gated_deltanet_mixer(h, w_in, conv_w, a_log, dt_bias, out_norm, w_out):
    b, t, _ = h.shape
    proj = h @ w_in
    qkv = jax.nn.silu(_causal_dwconv(proj[..., :3 * DN_WIDTH], conv_w))
    z = proj[..., 3 * DN_WIDTH:4 * DN_WIDTH].reshape(b, t, DN_HEADS, DN_HEAD_DIM).astype(jnp.float32)
    a = proj[..., 4 * DN_WIDTH:4 * DN_WIDTH + DN_HEADS].astype(jnp.float32)
    bl = proj[..., 4 * DN_WIDTH + DN_HEADS:].astype(jnp.float32)
    q, k, v = (y.reshape(b, t, DN_HEADS, DN_HEAD_DIM).astype(jnp.float32) for y in jnp.split(qkv, 3, axis=-1))
    q = _l2norm(q) * (DN_HEAD_DIM ** -0.5)
    k = _l2norm(k)
    g = -jnp.exp(a_log.astype(jnp.float32)) * jax.nn.softplus(a + dt_bias.astype(jnp.float32))
    beta = jax.nn.sigmoid(bl)
    o = _chunk_gated_delta_rule(q, k, v, g, beta)
    o = _rmsnorm(o, out_norm) * jax.nn.silu(z)
    return o.reshape(b, t, DN_WIDTH).astype(h.dtype) @ w_out


def _stick_breaking_weights(z, tpos, spos):
    past = spos < tpos
    log_stay = jnp.where(past, jax.nn.log_sigmoid(-z), 0.0)
    log_after = lax.cumsum(log_stay, axis=z.ndim - 1, reverse=True) - log_stay
    return jnp.where(past, jnp.exp(jax.nn.log_sigmoid(z) + log_after), 0.0)


def _softmax_weights(z, tpos, spos):
    return jax.nn.softmax(jnp.where(spos <= tpos, z, -jnp.inf), axis=-1)


def _causal_block_sweep(q, k, v, weight_fn):
    t = q.shape[2]
    local = jnp.arange(Q_BLOCK)
    outs = []
    for blk in range(t // Q_BLOCK):
        start, end = blk * Q_BLOCK, (blk + 1) * Q_BLOCK
        z = jnp.einsum('bhqd,bhkd->bhqk', q[:, :, start:end], k[:, :, :end])
        tpos = (start + local)[:, None]
        spos = jnp.arange(end)[None, :]
        wts = weight_fn(z, tpos, spos)
        outs.append(jnp.einsum('bhqk,bhkd->bhqd', wts, v[:, :, :end]))
    return jnp.concatenate(outs, axis=2)


def _stick_breaking_mixer(h, w_qkv, q_norm, k_norm, w_out):
    b, t, _ = h.shape
    q, k, v = (y.reshape(b, t, SB_HEADS, SB_HEAD_DIM) for y in jnp.split(h @ w_qkv, 3, axis=-1))
    q = _rmsnorm(q, q_norm).astype(jnp.float32) * (SB_HEAD_DIM ** -0.5)
    k = _rmsnorm(k, k_norm).astype(jnp.float32)
    o = _causal_block_sweep(_heads_first(q), _heads_first(k), _heads_first(v.astype(jnp.float32)), _stick_breaking_weights)
    return _heads_first(o).reshape(b, t, SB_HEADS * SB_HEAD_DIM).astype(h.dtype) @ w_out


def _rope_tables(t):
    inv_freq = ROPE_THETA ** (-jnp.arange(0, MLA_ROPE, 2, dtype=jnp.float32) / MLA_ROPE)
    ang = jnp.arange(t, dtype=jnp.float32)[:, None] * inv_freq[None, :]
    return jnp.cos(ang), jnp.sin(ang)


def _apply_rope(x, cos, sin):
    x1, x2 = jnp.split(x.astype(jnp.float32), 2, axis=-1)
    c, s = cos[None, :, None, :], sin[None, :, None, :]
    return jnp.concatenate([x1 * c - x2 * s, x2 * c + x1 * s], axis=-1)


def _mla_mixer(h, w_down, q_a_norm, kv_a_norm, w_uq, w_ukv, q_nope_norm, q_rope_norm, k_nope_norm, k_rope_norm, w_out):
    b, t, _ = h.shape
    c_q, c_kv, k_rope = jnp.split(h @ w_down, [MLA_Q_RANK, MLA_Q_RANK + MLA_KV_RANK], axis=-1)
    q = (_rmsnorm(c_q, q_a_norm) @ w_uq).reshape(b, t, MLA_HEADS, MLA_QK)
    kv = (_rmsnorm(c_kv, kv_a_norm) @ w_ukv).reshape(b, t, MLA_HEADS, MLA_NOPE + MLA_V)
    v = kv[..., MLA_NOPE:]
    cos, sin = _rope_tables(t)
    q_nope = _rmsnorm(q[..., :MLA_NOPE], q_nope_norm).astype(jnp.float32)
    q_rot = _apply_rope(_rmsnorm(q[..., MLA_NOPE:], q_rope_norm), cos, sin)
    q = jnp.concatenate([q_nope, q_rot], axis=-1) * (MLA_QK ** -0.5)
    k_nope = _rmsnorm(kv[..., :MLA_NOPE], k_nope_norm).astype(jnp.float32)
    k_rot = _apply_rope(_rmsnorm(k_rope, k_rope_norm)[:, :, None, :], cos, sin)
    k = jnp.concatenate([k_nope, jnp.broadcast_to(k_rot, (b, t, MLA_HEADS, MLA_ROPE))], axis=-1)
    o = _causal_block_sweep(_heads_first(q), _heads_first(k), _heads_first(v.astype(jnp.float32)), _softmax_weights)
    return _heads_first(o).reshape(b, t, MLA_HEADS * MLA_V).astype(h.dtype) @ w_out


def _dense(key, fan_in, fan_out):
    return jax.random.normal(key, (fan_in, fan_out), jnp.float32) * (fan_in ** -0.5)


def _gain(key, n):
    return 1.0 + 0.02 * jax.random.normal(key, (n,), jnp.float32)


def _deltanet_params(key, p):
    ks = jax.random.split(key, 6)
    dt = jnp.exp(jax.random.uniform(ks[3], (DN_HEADS,), jnp.float32, math.log(1e-3), math.log(1e-1)))
    return {
        p + 'dn_w_in': _dense(ks[0], D_MODEL, 4 * DN_WIDTH + 2 * DN_HEADS),
        p + 'dn_conv_w': jax.random.normal(ks[1], (DN_CONV, 3 * DN_WIDTH), jnp.float32) * (DN_CONV ** -0.5),
        p + 'dn_a_log': jnp.log(jax.random.uniform(ks[2], (DN_HEADS,), jnp.float32, 1.0, 16.0)),
        p + 'dn_dt_bias': dt + jnp.log(-jnp.expm1(-dt)),
        p + 'dn_out_norm': _gain(ks[4], DN_HEAD_DIM),
        p + 'dn_w_out': _dense(ks[5], DN_WIDTH, D_MODEL),
    }


def _stick_breaking_params(key, p):
    ks = jax.random.split(key, 4)
    return {
        p + 'sb_w_qkv': _dense(ks[0], D_MODEL, 3 * SB_HEADS * SB_HEAD_DIM),
        p + 'sb_q_norm': _gain(ks[1], SB_HEAD_DIM),
        p + 'sb_k_norm': _gain(ks[2], SB_HEAD_DIM),
        p + 'sb_w_out': _dense(ks[3], SB_HEADS * SB_HEAD_DIM, D_MODEL),
    }


def _mla_params(key, p):
    ks = jax.random.split(key, 10)
    return {
        p + 'mla_w_down': _dense(ks[0], D_MODEL, MLA_Q_RANK + MLA_KV_RANK + MLA_ROPE),
        p + 'mla_q_a_norm': _gain(ks[1], MLA_Q_RANK),
        p + 'mla_kv_a_norm': _gain(ks[2], MLA_KV_RANK),
        p + 'mla_w_uq': _dense(ks[3], MLA_Q_RANK, MLA_HEADS * MLA_QK),
        p + 'mla_w_ukv': _dense(ks[4], MLA_KV_RANK, MLA_HEADS * (MLA_NOPE + MLA_V)),
        p + 'mla_q_nope_norm': _gain(ks[5], MLA_NOPE),
        p + 'mla_q_rope_norm': _gain(ks[6], MLA_ROPE),
        p + 'mla_k_nope_norm': _gain(ks[7], MLA_NOPE),
        p + 'mla_k_rope_norm': _gain(ks[8], MLA_ROPE),
        p + 'mla_w_out': _dense(ks[9], MLA_HEADS * MLA_V, D_MODEL),
    }


def _fwd_setup_inputs(seed: int = 0) -> dict:
    key = jax.random.key(seed)
    k_x, k_layers = jax.random.split(key)
    inputs = {'x': jax.random.normal(k_x, (BATCH, SEQ, D_MODEL), jnp.float32)}
    builders = (_deltanet_params, _stick_breaking_params, _mla_params)
    layer_keys = jax.random.split(k_layers, DEPTH)
    for i in range(DEPTH):
        ks = jax.random.split(layer_keys[i], 5)
        p = 'l' + str(i) + '_'
        inputs[p + 'mix_norm'] = _gain(ks[0], D_MODEL)
        inputs.update(builders[i % N_MIXERS](ks[1], p))
        inputs[p + 'ffn_norm'] = _gain(ks[2], D_MODEL)
        inputs[p + 'ffn_w_gate_up'] = _dense(ks[3], D_MODEL, 2 * FFN_HIDDEN)
        inputs[p + 'ffn_w_down'] = _dense(ks[4], FFN_HIDDEN, D_MODEL)
    return inputs


def _fwd_reference(x,
              l0_mix_norm, l0_dn_w_in, l0_dn_conv_w, l0_dn_a_log, l0_dn_dt_bias, l0_dn_out_norm, l0_dn_w_out,
              l0_ffn_norm, l0_ffn_w_gate_up, l0_ffn_w_down,
              l1_mix_norm, l1_sb_w_qkv, l1_sb_q_norm, l1_sb_k_norm, l1_sb_w_out,
              l1_ffn_norm, l1_ffn_w_gate_up, l1_ffn_w_down,
              l2_mix_norm, l2_mla_w_down, l2_mla_q_a_norm, l2_mla_kv_a_norm, l2_mla_w_uq, l2_mla_w_ukv,
              l2_mla_q_nope_norm, l2_mla_q_rope_norm, l2_mla_k_nope_norm, l2_mla_k_rope_norm, l2_mla_w_out,
              l2_ffn_norm, l2_ffn_w_gate_up, l2_ffn_w_down,
              l3_mix_norm, l3_dn_w_in, l3_dn_conv_w, l3_dn_a_log, l3_dn_dt_bias, l3_dn_out_norm, l3_dn_w_out,
              l3_ffn_norm, l3_ffn_w_gate_up, l3_ffn_w_down):
    mixers = (_gated_deltanet_mixer, _stick_breaking_mixer, _mla_mixer)
    layers = (
        (l0_mix_norm, (l0_dn_w_in, l0_dn_conv_w, l0_dn_a_log, l0_dn_dt_bias, l0_dn_out_norm, l0_dn_w_out),
         l0_ffn_norm, l0_ffn_w_gate_up, l0_ffn_w_down),
        (l1_mix_norm, (l1_sb_w_qkv, l1_sb_q_norm, l1_sb_k_norm, l1_sb_w_out),
         l1_ffn_norm, l1_ffn_w_gate_up, l1_ffn_w_down),
        (l2_mix_norm, (l2_mla_w_down, l2_mla_q_a_norm, l2_mla_kv_a_norm, l2_mla_w_uq, l2_mla_w_ukv,
                       l2_mla_q_nope_norm, l2_mla_q_rope_norm, l2_mla_k_nope_norm, l2_mla_k_rope_norm, l2_mla_w_out),
         l2_ffn_norm, l2_ffn_w_gate_up, l2_ffn_w_down),
        (l3_mix_norm, (l3_dn_w_in, l3_dn_conv_w, l3_dn_a_log, l3_dn_dt_bias, l3_dn_out_norm, l3_dn_w_out),
         l3_ffn_norm, l3_ffn_w_gate_up, l3_ffn_w_down),
    )
    for i in range(DEPTH):
        mix_norm, mix_params, ffn_norm, w_gate_up, w_down = layers[i]
        x = x + mixers[i % N_MIXERS](_rmsnorm(x, mix_norm), *mix_params)
        x = x + _swiglu(_rmsnorm(x, ffn_norm), w_gate_up, w_down)
    return x


import jax as _jax
import jax.numpy as _jnp

TWIN_FORMAT = 'train_step'
FWD_PARAMS = ['x', 'l0_mix_norm', 'l0_dn_w_in', 'l0_dn_conv_w', 'l0_dn_a_log', 'l0_dn_dt_bias', 'l0_dn_out_norm', 'l0_dn_w_out', 'l0_ffn_norm', 'l0_ffn_w_gate_up', 'l0_ffn_w_down', 'l1_mix_norm', 'l1_sb_w_qkv', 'l1_sb_q_norm', 'l1_sb_k_norm', 'l1_sb_w_out', 'l1_ffn_norm', 'l1_ffn_w_gate_up', 'l1_ffn_w_down', 'l2_mix_norm', 'l2_mla_w_down', 'l2_mla_q_a_norm', 'l2_mla_kv_a_norm', 'l2_mla_w_uq', 'l2_mla_w_ukv', 'l2_mla_q_nope_norm', 'l2_mla_q_rope_norm', 'l2_mla_k_nope_norm', 'l2_mla_k_rope_norm', 'l2_mla_w_out', 'l2_ffn_norm', 'l2_ffn_w_gate_up', 'l2_ffn_w_down', 'l3_mix_norm', 'l3_dn_w_in', 'l3_dn_conv_w', 'l3_dn_a_log', 'l3_dn_dt_bias', 'l3_dn_out_norm', 'l3_dn_w_out', 'l3_ffn_norm', 'l3_ffn_w_gate_up', 'l3_ffn_w_down']
TWIN_WEIGHTS = ['l0_mix_norm', 'l0_dn_w_in', 'l0_dn_conv_w', 'l0_dn_a_log', 'l0_dn_dt_bias', 'l0_dn_out_norm', 'l0_dn_w_out', 'l0_ffn_norm', 'l0_ffn_w_gate_up', 'l0_ffn_w_down', 'l1_mix_norm', 'l1_sb_w_qkv', 'l1_sb_q_norm', 'l1_sb_k_norm', 'l1_sb_w_out', 'l1_ffn_norm', 'l1_ffn_w_gate_up', 'l1_ffn_w_down', 'l2_mix_norm', 'l2_mla_w_down', 'l2_mla_q_a_norm', 'l2_mla_kv_a_norm', 'l2_mla_w_uq', 'l2_mla_w_ukv', 'l2_mla_q_nope_norm', 'l2_mla_q_rope_norm', 'l2_mla_k_nope_norm', 'l2_mla_k_rope_norm', 'l2_mla_w_out', 'l2_ffn_norm', 'l2_ffn_w_gate_up', 'l2_ffn_w_down', 'l3_mix_norm', 'l3_dn_w_in', 'l3_dn_conv_w', 'l3_dn_a_log', 'l3_dn_dt_bias', 'l3_dn_out_norm', 'l3_dn_w_out', 'l3_ffn_norm', 'l3_ffn_w_gate_up', 'l3_ffn_w_down']
TWIN_DIFF_INPUT = 'x'
TWIN_INPUTS = ['x', 'l0_mix_norm', 'l0_dn_w_in', 'l0_dn_conv_w', 'l0_dn_a_log', 'l0_dn_dt_bias', 'l0_dn_out_norm', 'l0_dn_w_out', 'l0_ffn_norm', 'l0_ffn_w_gate_up', 'l0_ffn_w_down', 'l1_mix_norm', 'l1_sb_w_qkv', 'l1_sb_q_norm', 'l1_sb_k_norm', 'l1_sb_w_out', 'l1_ffn_norm', 'l1_ffn_w_gate_up', 'l1_ffn_w_down', 'l2_mix_norm', 'l2_mla_w_down', 'l2_mla_q_a_norm', 'l2_mla_kv_a_norm', 'l2_mla_w_uq', 'l2_mla_w_ukv', 'l2_mla_q_nope_norm', 'l2_mla_q_rope_norm', 'l2_mla_k_nope_norm', 'l2_mla_k_rope_norm', 'l2_mla_w_out', 'l2_ffn_norm', 'l2_ffn_w_gate_up', 'l2_ffn_w_down', 'l3_mix_norm', 'l3_dn_w_in', 'l3_dn_conv_w', 'l3_dn_a_log', 'l3_dn_dt_bias', 'l3_dn_out_norm', 'l3_dn_w_out', 'l3_ffn_norm', 'l3_ffn_w_gate_up', 'l3_ffn_w_down', 'loss_target', 'm_l0_mix_norm', 'm_l0_dn_w_in', 'm_l0_dn_conv_w', 'm_l0_dn_a_log', 'm_l0_dn_dt_bias', 'm_l0_dn_out_norm', 'm_l0_dn_w_out', 'm_l0_ffn_norm', 'm_l0_ffn_w_gate_up', 'm_l0_ffn_w_down', 'm_l1_mix_norm', 'm_l1_sb_w_qkv', 'm_l1_sb_q_norm', 'm_l1_sb_k_norm', 'm_l1_sb_w_out', 'm_l1_ffn_norm', 'm_l1_ffn_w_gate_up', 'm_l1_ffn_w_down', 'm_l2_mix_norm', 'm_l2_mla_w_down', 'm_l2_mla_q_a_norm', 'm_l2_mla_kv_a_norm', 'm_l2_mla_w_uq', 'm_l2_mla_w_ukv', 'm_l2_mla_q_nope_norm', 'm_l2_mla_q_rope_norm', 'm_l2_mla_k_nope_norm', 'm_l2_mla_k_rope_norm', 'm_l2_mla_w_out', 'm_l2_ffn_norm', 'm_l2_ffn_w_gate_up', 'm_l2_ffn_w_down', 'm_l3_mix_norm', 'm_l3_dn_w_in', 'm_l3_dn_conv_w', 'm_l3_dn_a_log', 'm_l3_dn_dt_bias', 'm_l3_dn_out_norm', 'm_l3_dn_w_out', 'm_l3_ffn_norm', 'm_l3_ffn_w_gate_up', 'm_l3_ffn_w_down', 'v_l0_mix_norm', 'v_l0_dn_w_in', 'v_l0_dn_conv_w', 'v_l0_dn_a_log', 'v_l0_dn_dt_bias', 'v_l0_dn_out_norm', 'v_l0_dn_w_out', 'v_l0_ffn_norm', 'v_l0_ffn_w_gate_up', 'v_l0_ffn_w_down', 'v_l1_mix_norm', 'v_l1_sb_w_qkv', 'v_l1_sb_q_norm', 'v_l1_sb_k_norm', 'v_l1_sb_w_out', 'v_l1_ffn_norm', 'v_l1_ffn_w_gate_up', 'v_l1_ffn_w_down', 'v_l2_mix_norm', 'v_l2_mla_w_down', 'v_l2_mla_q_a_norm', 'v_l2_mla_kv_a_norm', 'v_l2_mla_w_uq', 'v_l2_mla_w_ukv', 'v_l2_mla_q_nope_norm', 'v_l2_mla_q_rope_norm', 'v_l2_mla_k_nope_norm', 'v_l2_mla_k_rope_norm', 'v_l2_mla_w_out', 'v_l2_ffn_norm', 'v_l2_ffn_w_gate_up', 'v_l2_ffn_w_down', 'v_l3_mix_norm', 'v_l3_dn_w_in', 'v_l3_dn_conv_w', 'v_l3_dn_a_log', 'v_l3_dn_dt_bias', 'v_l3_dn_out_norm', 'v_l3_dn_w_out', 'v_l3_ffn_norm', 'v_l3_ffn_w_gate_up', 'v_l3_ffn_w_down']
TWIN_OUTPUTS = ['loss', 'grad_x', 'grad_l0_mix_norm', 'grad_l0_dn_w_in', 'grad_l0_dn_conv_w', 'grad_l0_dn_a_log', 'grad_l0_dn_dt_bias', 'grad_l0_dn_out_norm', 'grad_l0_dn_w_out', 'grad_l0_ffn_norm', 'grad_l0_ffn_w_gate_up', 'grad_l0_ffn_w_down', 'grad_l1_mix_norm', 'grad_l1_sb_w_qkv', 'grad_l1_sb_q_norm', 'grad_l1_sb_k_norm', 'grad_l1_sb_w_out', 'grad_l1_ffn_norm', 'grad_l1_ffn_w_gate_up', 'grad_l1_ffn_w_down', 'grad_l2_mix_norm', 'grad_l2_mla_w_down', 'grad_l2_mla_q_a_norm', 'grad_l2_mla_kv_a_norm', 'grad_l2_mla_w_uq', 'grad_l2_mla_w_ukv', 'grad_l2_mla_q_nope_norm', 'grad_l2_mla_q_rope_norm', 'grad_l2_mla_k_nope_norm', 'grad_l2_mla_k_rope_norm', 'grad_l2_mla_w_out', 'grad_l2_ffn_norm', 'grad_l2_ffn_w_gate_up', 'grad_l2_ffn_w_down', 'grad_l3_mix_norm', 'grad_l3_dn_w_in', 'grad_l3_dn_conv_w', 'grad_l3_dn_a_log', 'grad_l3_dn_dt_bias', 'grad_l3_dn_out_norm', 'grad_l3_dn_w_out', 'grad_l3_ffn_norm', 'grad_l3_ffn_w_gate_up', 'grad_l3_ffn_w_down', 'delta_l0_mix_norm', 'delta_l0_dn_w_in', 'delta_l0_dn_conv_w', 'delta_l0_dn_a_log', 'delta_l0_dn_dt_bias', 'delta_l0_dn_out_norm', 'delta_l0_dn_w_out', 'delta_l0_ffn_norm', 'delta_l0_ffn_w_gate_up', 'delta_l0_ffn_w_down', 'delta_l1_mix_norm', 'delta_l1_sb_w_qkv', 'delta_l1_sb_q_norm', 'delta_l1_sb_k_norm', 'delta_l1_sb_w_out', 'delta_l1_ffn_norm', 'delta_l1_ffn_w_gate_up', 'delta_l1_ffn_w_down', 'delta_l2_mix_norm', 'delta_l2_mla_w_down', 'delta_l2_mla_q_a_norm', 'delta_l2_mla_kv_a_norm', 'delta_l2_mla_w_uq', 'delta_l2_mla_w_ukv', 'delta_l2_mla_q_nope_norm', 'delta_l2_mla_q_rope_norm', 'delta_l2_mla_k_nope_norm', 'delta_l2_mla_k_rope_norm', 'delta_l2_mla_w_out', 'delta_l2_ffn_norm', 'delta_l2_ffn_w_gate_up', 'delta_l2_ffn_w_down', 'delta_l3_mix_norm', 'delta_l3_dn_w_in', 'delta_l3_dn_conv_w', 'delta_l3_dn_a_log', 'delta_l3_dn_dt_bias', 'delta_l3_dn_out_norm', 'delta_l3_dn_w_out', 'delta_l3_ffn_norm', 'delta_l3_ffn_w_gate_up', 'delta_l3_ffn_w_down', 'new_m_l0_mix_norm', 'new_m_l0_dn_w_in', 'new_m_l0_dn_conv_w', 'new_m_l0_dn_a_log', 'new_m_l0_dn_dt_bias', 'new_m_l0_dn_out_norm', 'new_m_l0_dn_w_out', 'new_m_l0_ffn_norm', 'new_m_l0_ffn_w_gate_up', 'new_m_l0_ffn_w_down', 'new_m_l1_mix_norm', 'new_m_l1_sb_w_qkv', 'new_m_l1_sb_q_norm', 'new_m_l1_sb_k_norm', 'new_m_l1_sb_w_out', 'new_m_l1_ffn_norm', 'new_m_l1_ffn_w_gate_up', 'new_m_l1_ffn_w_down', 'new_m_l2_mix_norm', 'new_m_l2_mla_w_down', 'new_m_l2_mla_q_a_norm', 'new_m_l2_mla_kv_a_norm', 'new_m_l2_mla_w_uq', 'new_m_l2_mla_w_ukv', 'new_m_l2_mla_q_nope_norm', 'new_m_l2_mla_q_rope_norm', 'new_m_l2_mla_k_nope_norm', 'new_m_l2_mla_k_rope_norm', 'new_m_l2_mla_w_out', 'new_m_l2_ffn_norm', 'new_m_l2_ffn_w_gate_up', 'new_m_l2_ffn_w_down', 'new_m_l3_mix_norm', 'new_m_l3_dn_w_in', 'new_m_l3_dn_conv_w', 'new_m_l3_dn_a_log', 'new_m_l3_dn_dt_bias', 'new_m_l3_dn_out_norm', 'new_m_l3_dn_w_out', 'new_m_l3_ffn_norm', 'new_m_l3_ffn_w_gate_up', 'new_m_l3_ffn_w_down', 'new_v_l0_mix_norm', 'new_v_l0_dn_w_in', 'new_v_l0_dn_conv_w', 'new_v_l0_dn_a_log', 'new_v_l0_dn_dt_bias', 'new_v_l0_dn_out_norm', 'new_v_l0_dn_w_out', 'new_v_l0_ffn_norm', 'new_v_l0_ffn_w_gate_up', 'new_v_l0_ffn_w_down', 'new_v_l1_mix_norm', 'new_v_l1_sb_w_qkv', 'new_v_l1_sb_q_norm', 'new_v_l1_sb_k_norm', 'new_v_l1_sb_w_out', 'new_v_l1_ffn_norm', 'new_v_l1_ffn_w_gate_up', 'new_v_l1_ffn_w_down', 'new_v_l2_mix_norm', 'new_v_l2_mla_w_down', 'new_v_l2_mla_q_a_norm', 'new_v_l2_mla_kv_a_norm', 'new_v_l2_mla_w_uq', 'new_v_l2_mla_w_ukv', 'new_v_l2_mla_q_nope_norm', 'new_v_l2_mla_q_rope_norm', 'new_v_l2_mla_k_nope_norm', 'new_v_l2_mla_k_rope_norm', 'new_v_l2_mla_w_out', 'new_v_l2_ffn_norm', 'new_v_l2_ffn_w_gate_up', 'new_v_l2_ffn_w_down', 'new_v_l3_mix_norm', 'new_v_l3_dn_w_in', 'new_v_l3_dn_conv_w', 'new_v_l3_dn_a_log', 'new_v_l3_dn_dt_bias', 'new_v_l3_dn_out_norm', 'new_v_l3_dn_w_out', 'new_v_l3_ffn_norm', 'new_v_l3_ffn_w_gate_up', 'new_v_l3_ffn_w_down']
TWIN_LEAF_KINDS = {'loss': 'loss', 'grad_x': 'grad_x', 'grad_l0_mix_norm': 'grad_w', 'grad_l0_dn_w_in': 'grad_w', 'grad_l0_dn_conv_w': 'grad_w', 'grad_l0_dn_a_log': 'grad_w', 'grad_l0_dn_dt_bias': 'grad_w', 'grad_l0_dn_out_norm': 'grad_w', 'grad_l0_dn_w_out': 'grad_w', 'grad_l0_ffn_norm': 'grad_w', 'grad_l0_ffn_w_gate_up': 'grad_w', 'grad_l0_ffn_w_down': 'grad_w', 'grad_l1_mix_norm': 'grad_w', 'grad_l1_sb_w_qkv': 'grad_w', 'grad_l1_sb_q_norm': 'grad_w', 'grad_l1_sb_k_norm': 'grad_w', 'grad_l1_sb_w_out': 'grad_w', 'grad_l1_ffn_norm': 'grad_w', 'grad_l1_ffn_w_gate_up': 'grad_w', 'grad_l1_ffn_w_down': 'grad_w', 'grad_l2_mix_norm': 'grad_w', 'grad_l2_mla_w_down': 'grad_w', 'grad_l2_mla_q_a_norm': 'grad_w', 'grad_l2_mla_kv_a_norm': 'grad_w', 'grad_l2_mla_w_uq': 'grad_w', 'grad_l2_mla_w_ukv': 'grad_w', 'grad_l2_mla_q_nope_norm': 'grad_w', 'grad_l2_mla_q_rope_norm': 'grad_w', 'grad_l2_mla_k_nope_norm': 'grad_w', 'grad_l2_mla_k_rope_norm': 'grad_w', 'grad_l2_mla_w_out': 'grad_w', 'grad_l2_ffn_norm': 'grad_w', 'grad_l2_ffn_w_gate_up': 'grad_w', 'grad_l2_ffn_w_down': 'grad_w', 'grad_l3_mix_norm': 'grad_w', 'grad_l3_dn_w_in': 'grad_w', 'grad_l3_dn_conv_w': 'grad_w', 'grad_l3_dn_a_log': 'grad_w', 'grad_l3_dn_dt_bias': 'grad_w', 'grad_l3_dn_out_norm': 'grad_w', 'grad_l3_dn_w_out': 'grad_w', 'grad_l3_ffn_norm': 'grad_w', 'grad_l3_ffn_w_gate_up': 'grad_w', 'grad_l3_ffn_w_down': 'grad_w', 'delta_l0_mix_norm': 'delta_w', 'delta_l0_dn_w_in': 'delta_w', 'delta_l0_dn_conv_w': 'delta_w', 'delta_l0_dn_a_log': 'delta_w', 'delta_l0_dn_dt_bias': 'delta_w', 'delta_l0_dn_out_norm': 'delta_w', 'delta_l0_dn_w_out': 'delta_w', 'delta_l0_ffn_norm': 'delta_w', 'delta_l0_ffn_w_gate_up': 'delta_w', 'delta_l0_ffn_w_down': 'delta_w', 'delta_l1_mix_norm': 'delta_w', 'delta_l1_sb_w_qkv': 'delta_w', 'delta_l1_sb_q_norm': 'delta_w', 'delta_l1_sb_k_norm': 'delta_w', 'delta_l1_sb_w_out': 'delta_w', 'delta_l1_ffn_norm': 'delta_w', 'delta_l1_ffn_w_gate_up': 'delta_w', 'delta_l1_ffn_w_down': 'delta_w', 'delta_l2_mix_norm': 'delta_w', 'delta_l2_mla_w_down': 'delta_w', 'delta_l2_mla_q_a_norm': 'delta_w', 'delta_l2_mla_kv_a_norm': 'delta_w', 'delta_l2_mla_w_uq': 'delta_w', 'delta_l2_mla_w_ukv': 'delta_w', 'delta_l2_mla_q_nope_norm': 'delta_w', 'delta_l2_mla_q_rope_norm': 'delta_w', 'delta_l2_mla_k_nope_norm': 'delta_w', 'delta_l2_mla_k_rope_norm': 'delta_w', 'delta_l2_mla_w_out': 'delta_w', 'delta_l2_ffn_norm': 'delta_w', 'delta_l2_ffn_w_gate_up': 'delta_w', 'delta_l2_ffn_w_down': 'delta_w', 'delta_l3_mix_norm': 'delta_w', 'delta_l3_dn_w_in': 'delta_w', 'delta_l3_dn_conv_w': 'delta_w', 'delta_l3_dn_a_log': 'delta_w', 'delta_l3_dn_dt_bias': 'delta_w', 'delta_l3_dn_out_norm': 'delta_w', 'delta_l3_dn_w_out': 'delta_w', 'delta_l3_ffn_norm': 'delta_w', 'delta_l3_ffn_w_gate_up': 'delta_w', 'delta_l3_ffn_w_down': 'delta_w', 'new_m_l0_mix_norm': 'new_m', 'new_m_l0_dn_w_in': 'new_m', 'new_m_l0_dn_conv_w': 'new_m', 'new_m_l0_dn_a_log': 'new_m', 'new_m_l0_dn_dt_bias': 'new_m', 'new_m_l0_dn_out_norm': 'new_m', 'new_m_l0_dn_w_out': 'new_m', 'new_m_l0_ffn_norm': 'new_m', 'new_m_l0_ffn_w_gate_up': 'new_m', 'new_m_l0_ffn_w_down': 'new_m', 'new_m_l1_mix_norm': 'new_m', 'new_m_l1_sb_w_qkv': 'new_m', 'new_m_l1_sb_q_norm': 'new_m', 'new_m_l1_sb_k_norm': 'new_m', 'new_m_l1_sb_w_out': 'new_m', 'new_m_l1_ffn_norm': 'new_m', 'new_m_l1_ffn_w_gate_up': 'new_m', 'new_m_l1_ffn_w_down': 'new_m', 'new_m_l2_mix_norm': 'new_m', 'new_m_l2_mla_w_down': 'new_m', 'new_m_l2_mla_q_a_norm': 'new_m', 'new_m_l2_mla_kv_a_norm': 'new_m', 'new_m_l2_mla_w_uq': 'new_m', 'new_m_l2_mla_w_ukv': 'new_m', 'new_m_l2_mla_q_nope_norm': 'new_m', 'new_m_l2_mla_q_rope_norm': 'new_m', 'new_m_l2_mla_k_nope_norm': 'new_m', 'new_m_l2_mla_k_rope_norm': 'new_m', 'new_m_l2_mla_w_out': 'new_m', 'new_m_l2_ffn_norm': 'new_m', 'new_m_l2_ffn_w_gate_up': 'new_m', 'new_m_l2_ffn_w_down': 'new_m', 'new_m_l3_mix_norm': 'new_m', 'new_m_l3_dn_w_in': 'new_m', 'new_m_l3_dn_conv_w': 'new_m', 'new_m_l3_dn_a_log': 'new_m', 'new_m_l3_dn_dt_bias': 'new_m', 'new_m_l3_dn_out_norm': 'new_m', 'new_m_l3_dn_w_out': 'new_m', 'new_m_l3_ffn_norm': 'new_m', 'new_m_l3_ffn_w_gate_up': 'new_m', 'new_m_l3_ffn_w_down': 'new_m', 'new_v_l0_mix_norm': 'new_v', 'new_v_l0_dn_w_in': 'new_v', 'new_v_l0_dn_conv_w': 'new_v', 'new_v_l0_dn_a_log': 'new_v', 'new_v_l0_dn_dt_bias': 'new_v', 'new_v_l0_dn_out_norm': 'new_v', 'new_v_l0_dn_w_out': 'new_v', 'new_v_l0_ffn_norm': 'new_v', 'new_v_l0_ffn_w_gate_up': 'new_v', 'new_v_l0_ffn_w_down': 'new_v', 'new_v_l1_mix_norm': 'new_v', 'new_v_l1_sb_w_qkv': 'new_v', 'new_v_l1_sb_q_norm': 'new_v', 'new_v_l1_sb_k_norm': 'new_v', 'new_v_l1_sb_w_out': 'new_v', 'new_v_l1_ffn_norm': 'new_v', 'new_v_l1_ffn_w_gate_up': 'new_v', 'new_v_l1_ffn_w_down': 'new_v', 'new_v_l2_mix_norm': 'new_v', 'new_v_l2_mla_w_down': 'new_v', 'new_v_l2_mla_q_a_norm': 'new_v', 'new_v_l2_mla_kv_a_norm': 'new_v', 'new_v_l2_mla_w_uq': 'new_v', 'new_v_l2_mla_w_ukv': 'new_v', 'new_v_l2_mla_q_nope_norm': 'new_v', 'new_v_l2_mla_q_rope_norm': 'new_v', 'new_v_l2_mla_k_nope_norm': 'new_v', 'new_v_l2_mla_k_rope_norm': 'new_v', 'new_v_l2_mla_w_out': 'new_v', 'new_v_l2_ffn_norm': 'new_v', 'new_v_l2_ffn_w_gate_up': 'new_v', 'new_v_l2_ffn_w_down': 'new_v', 'new_v_l3_mix_norm': 'new_v', 'new_v_l3_dn_w_in': 'new_v', 'new_v_l3_dn_conv_w': 'new_v', 'new_v_l3_dn_a_log': 'new_v', 'new_v_l3_dn_dt_bias': 'new_v', 'new_v_l3_dn_out_norm': 'new_v', 'new_v_l3_dn_w_out': 'new_v', 'new_v_l3_ffn_norm': 'new_v', 'new_v_l3_ffn_w_gate_up': 'new_v', 'new_v_l3_ffn_w_down': 'new_v'}


def _forward(args):
    return _fwd_reference(*[args[k] for k in FWD_PARAMS])


def _output_shape():
    def fwd():
        inp = _fwd_setup_inputs(0)
        return _fwd_reference(*[inp[k] for k in FWD_PARAMS])
    out = _jax.eval_shape(fwd)
    return out.shape, out.dtype

N_MICROBATCH = 1
ADAM_LR = 0.001
ADAM_B1 = 0.9
ADAM_B2 = 0.999
ADAM_EPS = 1e-08
ADAM_WD = 0.01
ADAM_STEP = 10
PER_EXAMPLE_BATCH_AXIS = {'x': 0, 'loss_target': 0}
SHARED_INPUTS = []
_WEIGHT_DTYPES = {'l0_mix_norm': _jnp.float32, 'l0_dn_w_in': _jnp.float32, 'l0_dn_conv_w': _jnp.float32, 'l0_dn_a_log': _jnp.float32, 'l0_dn_dt_bias': _jnp.float32, 'l0_dn_out_norm': _jnp.float32, 'l0_dn_w_out': _jnp.float32, 'l0_ffn_norm': _jnp.float32, 'l0_ffn_w_gate_up': _jnp.float32, 'l0_ffn_w_down': _jnp.float32, 'l1_mix_norm': _jnp.float32, 'l1_sb_w_qkv': _jnp.float32, 'l1_sb_q_norm': _jnp.float32, 'l1_sb_k_norm': _jnp.float32, 'l1_sb_w_out': _jnp.float32, 'l1_ffn_norm': _jnp.float32, 'l1_ffn_w_gate_up': _jnp.float32, 'l1_ffn_w_down': _jnp.float32, 'l2_mix_norm': _jnp.float32, 'l2_mla_w_down': _jnp.float32, 'l2_mla_q_a_norm': _jnp.float32, 'l2_mla_kv_a_norm': _jnp.float32, 'l2_mla_w_uq': _jnp.float32, 'l2_mla_w_ukv': _jnp.float32, 'l2_mla_q_nope_norm': _jnp.float32, 'l2_mla_q_rope_norm': _jnp.float32, 'l2_mla_k_nope_norm': _jnp.float32, 'l2_mla_k_rope_norm': _jnp.float32, 'l2_mla_w_out': _jnp.float32, 'l2_ffn_norm': _jnp.float32, 'l2_ffn_w_gate_up': _jnp.float32, 'l2_ffn_w_down': _jnp.float32, 'l3_mix_norm': _jnp.float32, 'l3_dn_w_in': _jnp.float32, 'l3_dn_conv_w': _jnp.float32, 'l3_dn_a_log': _jnp.float32, 'l3_dn_dt_bias': _jnp.float32, 'l3_dn_out_norm': _jnp.float32, 'l3_dn_w_out': _jnp.float32, 'l3_ffn_norm': _jnp.float32, 'l3_ffn_w_gate_up': _jnp.float32, 'l3_ffn_w_down': _jnp.float32}
MOMENT_SCALE = {'l0_mix_norm': 1.405619e+01, 'l0_dn_w_in': 6.607940e-01, 'l0_dn_conv_w': 1.177468e+00, 'l0_dn_a_log': 4.166329e+01, 'l0_dn_dt_bias': 3.983748e+01, 'l0_dn_out_norm': 9.493709e+01, 'l0_dn_w_out': 2.747607e+00, 'l0_ffn_norm': 2.457922e+01, 'l0_ffn_w_gate_up': 4.705783e-01, 'l0_ffn_w_down': 7.706498e-01, 'l1_mix_norm': 1.373478e+01, 'l1_sb_w_qkv': 9.678704e-01, 'l1_sb_q_norm': 1.563548e+01, 'l1_sb_k_norm': 1.566737e+01, 'l1_sb_w_out': 1.613665e+00, 'l1_ffn_norm': 2.457994e+01, 'l1_ffn_w_gate_up': 3.914061e-01, 'l1_ffn_w_down': 6.156265e-01, 'l2_mix_norm': 1.596444e+00, 'l2_mla_w_down': 2.136464e+00, 'l2_mla_q_a_norm': 2.696548e-01, 'l2_mla_kv_a_norm': 5.085783e+00, 'l2_mla_w_uq': 1.130588e-01, 'l2_mla_w_ukv': 7.811233e-01, 'l2_mla_q_nope_norm': 1.630522e+00, 'l2_mla_q_rope_norm': 1.136293e+00, 'l2_mla_k_nope_norm': 1.634729e+00, 'l2_mla_k_rope_norm': 1.129266e+00, 'l2_mla_w_out': 1.150153e+00, 'l2_ffn_norm': 2.453291e+01, 'l2_ffn_w_gate_up': 3.842608e-01, 'l2_ffn_w_down': 5.511670e-01, 'l3_mix_norm': 1.357846e+01, 'l3_dn_w_in': 8.366730e-01, 'l3_dn_conv_w': 8.910644e-01, 'l3_dn_a_log': 2.860324e+01, 'l3_dn_dt_bias': 2.759915e+01, 'l3_dn_out_norm': 9.456298e+01, 'l3_dn_w_out': 1.346016e+00, 'l3_ffn_norm': 2.491266e+01, 'l3_ffn_w_gate_up': 3.796805e-01, 'l3_ffn_w_down': 5.013836e-01}


def _to_microbatches(a, axis):
    t = _jnp.moveaxis(a, axis, 0)
    t = t.reshape((N_MICROBATCH, t.shape[0] // N_MICROBATCH) + t.shape[1:])
    return _jnp.moveaxis(t, 1, axis + 1)


def setup_inputs(seed: int = 0) -> dict:
    inp = _fwd_setup_inputs(seed)
    key = _jax.random.fold_in(_jax.random.key(seed), 7919)
    shape, _ = _output_shape()
    out = dict(inp)
    out["loss_target"] = _jax.random.normal(_jax.random.fold_in(key, 0), shape, _jnp.float32)
    for i, name in enumerate(TWIN_WEIGHTS):
        w = inp[name].astype(_jnp.float32)
        if MOMENT_SCALE is None:
            s = _jnp.sqrt(_jnp.mean(_jnp.square(w)) + 1e-30)
        else:
            s = MOMENT_SCALE[name]
        km, kv = _jax.random.split(_jax.random.fold_in(key, i + 1))
        out[name] = w
        out["m_" + name] = s * _jax.random.normal(km, w.shape, _jnp.float32)
        out["v_" + name] = (s * s) * _jax.random.uniform(kv, w.shape, _jnp.float32, 0.5, 1.5)
    if N_MICROBATCH > 1:
        for name, axis in PER_EXAMPLE_BATCH_AXIS.items():
            out[name] = _to_microbatches(out[name], axis)
    return {'x': out['x'], 'l0_mix_norm': out['l0_mix_norm'], 'l0_dn_w_in': out['l0_dn_w_in'], 'l0_dn_conv_w': out['l0_dn_conv_w'], 'l0_dn_a_log': out['l0_dn_a_log'], 'l0_dn_dt_bias': out['l0_dn_dt_bias'], 'l0_dn_out_norm': out['l0_dn_out_norm'], 'l0_dn_w_out': out['l0_dn_w_out'], 'l0_ffn_norm': out['l0_ffn_norm'], 'l0_ffn_w_gate_up': out['l0_ffn_w_gate_up'], 'l0_ffn_w_down': out['l0_ffn_w_down'], 'l1_mix_norm': out['l1_mix_norm'], 'l1_sb_w_qkv': out['l1_sb_w_qkv'], 'l1_sb_q_norm': out['l1_sb_q_norm'], 'l1_sb_k_norm': out['l1_sb_k_norm'], 'l1_sb_w_out': out['l1_sb_w_out'], 'l1_ffn_norm': out['l1_ffn_norm'], 'l1_ffn_w_gate_up': out['l1_ffn_w_gate_up'], 'l1_ffn_w_down': out['l1_ffn_w_down'], 'l2_mix_norm': out['l2_mix_norm'], 'l2_mla_w_down': out['l2_mla_w_down'], 'l2_mla_q_a_norm': out['l2_mla_q_a_norm'], 'l2_mla_kv_a_norm': out['l2_mla_kv_a_norm'], 'l2_mla_w_uq': out['l2_mla_w_uq'], 'l2_mla_w_ukv': out['l2_mla_w_ukv'], 'l2_mla_q_nope_norm': out['l2_mla_q_nope_norm'], 'l2_mla_q_rope_norm': out['l2_mla_q_rope_norm'], 'l2_mla_k_nope_norm': out['l2_mla_k_nope_norm'], 'l2_mla_k_rope_norm': out['l2_mla_k_rope_norm'], 'l2_mla_w_out': out['l2_mla_w_out'], 'l2_ffn_norm': out['l2_ffn_norm'], 'l2_ffn_w_gate_up': out['l2_ffn_w_gate_up'], 'l2_ffn_w_down': out['l2_ffn_w_down'], 'l3_mix_norm': out['l3_mix_norm'], 'l3_dn_w_in': out['l3_dn_w_in'], 'l3_dn_conv_w': out['l3_dn_conv_w'], 'l3_dn_a_log': out['l3_dn_a_log'], 'l3_dn_dt_bias': out['l3_dn_dt_bias'], 'l3_dn_out_norm': out['l3_dn_out_norm'], 'l3_dn_w_out': out['l3_dn_w_out'], 'l3_ffn_norm': out['l3_ffn_norm'], 'l3_ffn_w_gate_up': out['l3_ffn_w_gate_up'], 'l3_ffn_w_down': out['l3_ffn_w_down'], 'loss_target': out['loss_target'], 'm_l0_mix_norm': out['m_l0_mix_norm'], 'm_l0_dn_w_in': out['m_l0_dn_w_in'], 'm_l0_dn_conv_w': out['m_l0_dn_conv_w'], 'm_l0_dn_a_log': out['m_l0_dn_a_log'], 'm_l0_dn_dt_bias': out['m_l0_dn_dt_bias'], 'm_l0_dn_out_norm': out['m_l0_dn_out_norm'], 'm_l0_dn_w_out': out['m_l0_dn_w_out'], 'm_l0_ffn_norm': out['m_l0_ffn_norm'], 'm_l0_ffn_w_gate_up': out['m_l0_ffn_w_gate_up'], 'm_l0_ffn_w_down': out['m_l0_ffn_w_down'], 'm_l1_mix_norm': out['m_l1_mix_norm'], 'm_l1_sb_w_qkv': out['m_l1_sb_w_qkv'], 'm_l1_sb_q_norm': out['m_l1_sb_q_norm'], 'm_l1_sb_k_norm': out['m_l1_sb_k_norm'], 'm_l1_sb_w_out': out['m_l1_sb_w_out'], 'm_l1_ffn_norm': out['m_l1_ffn_norm'], 'm_l1_ffn_w_gate_up': out['m_l1_ffn_w_gate_up'], 'm_l1_ffn_w_down': out['m_l1_ffn_w_down'], 'm_l2_mix_norm': out['m_l2_mix_norm'], 'm_l2_mla_w_down': out['m_l2_mla_w_down'], 'm_l2_mla_q_a_norm': out['m_l2_mla_q_a_norm'], 'm_l2_mla_kv_a_norm': out['m_l2_mla_kv_a_norm'], 'm_l2_mla_w_uq': out['m_l2_mla_w_uq'], 'm_l2_mla_w_ukv': out['m_l2_mla_w_ukv'], 'm_l2_mla_q_nope_norm': out['m_l2_mla_q_nope_norm'], 'm_l2_mla_q_rope_norm': out['m_l2_mla_q_rope_norm'], 'm_l2_mla_k_nope_norm': out['m_l2_mla_k_nope_norm'], 'm_l2_mla_k_rope_norm': out['m_l2_mla_k_rope_norm'], 'm_l2_mla_w_out': out['m_l2_mla_w_out'], 'm_l2_ffn_norm': out['m_l2_ffn_norm'], 'm_l2_ffn_w_gate_up': out['m_l2_ffn_w_gate_up'], 'm_l2_ffn_w_down': out['m_l2_ffn_w_down'], 'm_l3_mix_norm': out['m_l3_mix_norm'], 'm_l3_dn_w_in': out['m_l3_dn_w_in'], 'm_l3_dn_conv_w': out['m_l3_dn_conv_w'], 'm_l3_dn_a_log': out['m_l3_dn_a_log'], 'm_l3_dn_dt_bias': out['m_l3_dn_dt_bias'], 'm_l3_dn_out_norm': out['m_l3_dn_out_norm'], 'm_l3_dn_w_out': out['m_l3_dn_w_out'], 'm_l3_ffn_norm': out['m_l3_ffn_norm'], 'm_l3_ffn_w_gate_up': out['m_l3_ffn_w_gate_up'], 'm_l3_ffn_w_down': out['m_l3_ffn_w_down'], 'v_l0_mix_norm': out['v_l0_mix_norm'], 'v_l0_dn_w_in': out['v_l0_dn_w_in'], 'v_l0_dn_conv_w': out['v_l0_dn_conv_w'], 'v_l0_dn_a_log': out['v_l0_dn_a_log'], 'v_l0_dn_dt_bias': out['v_l0_dn_dt_bias'], 'v_l0_dn_out_norm': out['v_l0_dn_out_norm'], 'v_l0_dn_w_out': out['v_l0_dn_w_out'], 'v_l0_ffn_norm': out['v_l0_ffn_norm'], 'v_l0_ffn_w_gate_up': out['v_l0_ffn_w_gate_up'], 'v_l0_ffn_w_down': out['v_l0_ffn_w_down'], 'v_l1_mix_norm': out['v_l1_mix_norm'], 'v_l1_sb_w_qkv': out['v_l1_sb_w_qkv'], 'v_l1_sb_q_norm': out['v_l1_sb_q_norm'], 'v_l1_sb_k_norm': out['v_l1_sb_k_norm'], 'v_l1_sb_w_out': out['v_l1_sb_w_out'], 'v_l1_ffn_norm': out['v_l1_ffn_norm'], 'v_l1_ffn_w_gate_up': out['v_l1_ffn_w_gate_up'], 'v_l1_ffn_w_down': out['v_l1_ffn_w_down'], 'v_l2_mix_norm': out['v_l2_mix_norm'], 'v_l2_mla_w_down': out['v_l2_mla_w_down'], 'v_l2_mla_q_a_norm': out['v_l2_mla_q_a_norm'], 'v_l2_mla_kv_a_norm': out['v_l2_mla_kv_a_norm'], 'v_l2_mla_w_uq': out['v_l2_mla_w_uq'], 'v_l2_mla_w_ukv': out['v_l2_mla_w_ukv'], 'v_l2_mla_q_nope_norm': out['v_l2_mla_q_nope_norm'], 'v_l2_mla_q_rope_norm': out['v_l2_mla_q_rope_norm'], 'v_l2_mla_k_nope_norm': out['v_l2_mla_k_nope_norm'], 'v_l2_mla_k_rope_norm': out['v_l2_mla_k_rope_norm'], 'v_l2_mla_w_out': out['v_l2_mla_w_out'], 'v_l2_ffn_norm': out['v_l2_ffn_norm'], 'v_l2_ffn_w_gate_up': out['v_l2_ffn_w_gate_up'], 'v_l2_ffn_w_down': out['v_l2_ffn_w_down'], 'v_l3_mix_norm': out['v_l3_mix_norm'], 'v_l3_dn_w_in': out['v_l3_dn_w_in'], 'v_l3_dn_conv_w': out['v_l3_dn_conv_w'], 'v_l3_dn_a_log': out['v_l3_dn_a_log'], 'v_l3_dn_dt_bias': out['v_l3_dn_dt_bias'], 'v_l3_dn_out_norm': out['v_l3_dn_out_norm'], 'v_l3_dn_w_out': out['v_l3_dn_w_out'], 'v_l3_ffn_norm': out['v_l3_ffn_norm'], 'v_l3_ffn_w_gate_up': out['v_l3_ffn_w_gate_up'], 'v_l3_ffn_w_down': out['v_l3_ffn_w_down']}


def _loss(weights, diff, rest, loss_target):
    with _jax.named_scope("forward"):
        args = {**rest, TWIN_DIFF_INPUT: diff, **{k: w.astype(_WEIGHT_DTYPES[k]) for k, w in weights.items()}}
        y = _forward(args)
    with _jax.named_scope("loss_head"):
        err = _jnp.square(y.astype(_jnp.float32) - loss_target)
        return 0.5 * _jnp.sum(_jnp.mean(err, axis=-1)) if err.ndim else 0.5 * err


def _adamw(w, g, m, v):
    m = ADAM_B1 * m + (1.0 - ADAM_B1) * g
    v = ADAM_B2 * v + (1.0 - ADAM_B2) * _jnp.square(g)
    m_hat = m / (1.0 - ADAM_B1 ** ADAM_STEP)
    v_hat = v / (1.0 - ADAM_B2 ** ADAM_STEP)
    delta = -ADAM_LR * (m_hat / (_jnp.sqrt(v_hat) + ADAM_EPS) + ADAM_WD * w)
    return delta, m, v


def reference(x, l0_mix_norm, l0_dn_w_in, l0_dn_conv_w, l0_dn_a_log, l0_dn_dt_bias, l0_dn_out_norm, l0_dn_w_out, l0_ffn_norm, l0_ffn_w_gate_up, l0_ffn_w_down, l1_mix_norm, l1_sb_w_qkv, l1_sb_q_norm, l1_sb_k_norm, l1_sb_w_out, l1_ffn_norm, l1_ffn_w_gate_up, l1_ffn_w_down, l2_mix_norm, l2_mla_w_down, l2_mla_q_a_norm, l2_mla_kv_a_norm, l2_mla_w_uq, l2_mla_w_ukv, l2_mla_q_nope_norm, l2_mla_q_rope_norm, l2_mla_k_nope_norm, l2_mla_k_rope_norm, l2_mla_w_out, l2_ffn_norm, l2_ffn_w_gate_up, l2_ffn_w_down, l3_mix_norm, l3_dn_w_in, l3_dn_conv_w, l3_dn_a_log, l3_dn_dt_bias, l3_dn_out_norm, l3_dn_w_out, l3_ffn_norm, l3_ffn_w_gate_up, l3_ffn_w_down, loss_target, m_l0_mix_norm, m_l0_dn_w_in, m_l0_dn_conv_w, m_l0_dn_a_log, m_l0_dn_dt_bias, m_l0_dn_out_norm, m_l0_dn_w_out, m_l0_ffn_norm, m_l0_ffn_w_gate_up, m_l0_ffn_w_down, m_l1_mix_norm, m_l1_sb_w_qkv, m_l1_sb_q_norm, m_l1_sb_k_norm, m_l1_sb_w_out, m_l1_ffn_norm, m_l1_ffn_w_gate_up, m_l1_ffn_w_down, m_l2_mix_norm, m_l2_mla_w_down, m_l2_mla_q_a_norm, m_l2_mla_kv_a_norm, m_l2_mla_w_uq, m_l2_mla_w_ukv, m_l2_mla_q_nope_norm, m_l2_mla_q_rope_norm, m_l2_mla_k_nope_norm, m_l2_mla_k_rope_norm, m_l2_mla_w_out, m_l2_ffn_norm, m_l2_ffn_w_gate_up, m_l2_ffn_w_down, m_l3_mix_norm, m_l3_dn_w_in, m_l3_dn_conv_w, m_l3_dn_a_log, m_l3_dn_dt_bias, m_l3_dn_out_norm, m_l3_dn_w_out, m_l3_ffn_norm, m_l3_ffn_w_gate_up, m_l3_ffn_w_down, v_l0_mix_norm, v_l0_dn_w_in, v_l0_dn_conv_w, v_l0_dn_a_log, v_l0_dn_dt_bias, v_l0_dn_out_norm, v_l0_dn_w_out, v_l0_ffn_norm, v_l0_ffn_w_gate_up, v_l0_ffn_w_down, v_l1_mix_norm, v_l1_sb_w_qkv, v_l1_sb_q_norm, v_l1_sb_k_norm, v_l1_sb_w_out, v_l1_ffn_norm, v_l1_ffn_w_gate_up, v_l1_ffn_w_down, v_l2_mix_norm, v_l2_mla_w_down, v_l2_mla_q_a_norm, v_l2_mla_kv_a_norm, v_l2_mla_w_uq, v_l2_mla_w_ukv, v_l2_mla_q_nope_norm, v_l2_mla_q_rope_norm, v_l2_mla_k_nope_norm, v_l2_mla_k_rope_norm, v_l2_mla_w_out, v_l2_ffn_norm, v_l2_ffn_w_gate_up, v_l2_ffn_w_down, v_l3_mix_norm, v_l3_dn_w_in, v_l3_dn_conv_w, v_l3_dn_a_log, v_l3_dn_dt_bias, v_l3_dn_out_norm, v_l3_dn_w_out, v_l3_ffn_norm, v_l3_ffn_w_gate_up, v_l3_ffn_w_down):
    given = dict(x=x, l0_mix_norm=l0_mix_norm, l0_dn_w_in=l0_dn_w_in, l0_dn_conv_w=l0_dn_conv_w, l0_dn_a_log=l0_dn_a_log, l0_dn_dt_bias=l0_dn_dt_bias, l0_dn_out_norm=l0_dn_out_norm, l0_dn_w_out=l0_dn_w_out, l0_ffn_norm=l0_ffn_norm, l0_ffn_w_gate_up=l0_ffn_w_gate_up, l0_ffn_w_down=l0_ffn_w_down, l1_mix_norm=l1_mix_norm, l1_sb_w_qkv=l1_sb_w_qkv, l1_sb_q_norm=l1_sb_q_norm, l1_sb_k_norm=l1_sb_k_norm, l1_sb_w_out=l1_sb_w_out, l1_ffn_norm=l1_ffn_norm, l1_ffn_w_gate_up=l1_ffn_w_gate_up, l1_ffn_w_down=l1_ffn_w_down, l2_mix_norm=l2_mix_norm, l2_mla_w_down=l2_mla_w_down, l2_mla_q_a_norm=l2_mla_q_a_norm, l2_mla_kv_a_norm=l2_mla_kv_a_norm, l2_mla_w_uq=l2_mla_w_uq, l2_mla_w_ukv=l2_mla_w_ukv, l2_mla_q_nope_norm=l2_mla_q_nope_norm, l2_mla_q_rope_norm=l2_mla_q_rope_norm, l2_mla_k_nope_norm=l2_mla_k_nope_norm, l2_mla_k_rope_norm=l2_mla_k_rope_norm, l2_mla_w_out=l2_mla_w_out, l2_ffn_norm=l2_ffn_norm, l2_ffn_w_gate_up=l2_ffn_w_gate_up, l2_ffn_w_down=l2_ffn_w_down, l3_mix_norm=l3_mix_norm, l3_dn_w_in=l3_dn_w_in, l3_dn_conv_w=l3_dn_conv_w, l3_dn_a_log=l3_dn_a_log, l3_dn_dt_bias=l3_dn_dt_bias, l3_dn_out_norm=l3_dn_out_norm, l3_dn_w_out=l3_dn_w_out, l3_ffn_norm=l3_ffn_norm, l3_ffn_w_gate_up=l3_ffn_w_gate_up, l3_ffn_w_down=l3_ffn_w_down, loss_target=loss_target, m_l0_mix_norm=m_l0_mix_norm, m_l0_dn_w_in=m_l0_dn_w_in, m_l0_dn_conv_w=m_l0_dn_conv_w, m_l0_dn_a_log=m_l0_dn_a_log, m_l0_dn_dt_bias=m_l0_dn_dt_bias, m_l0_dn_out_norm=m_l0_dn_out_norm, m_l0_dn_w_out=m_l0_dn_w_out, m_l0_ffn_norm=m_l0_ffn_norm, m_l0_ffn_w_gate_up=m_l0_ffn_w_gate_up, m_l0_ffn_w_down=m_l0_ffn_w_down, m_l1_mix_norm=m_l1_mix_norm, m_l1_sb_w_qkv=m_l1_sb_w_qkv, m_l1_sb_q_norm=m_l1_sb_q_norm, m_l1_sb_k_norm=m_l1_sb_k_norm, m_l1_sb_w_out=m_l1_sb_w_out, m_l1_ffn_norm=m_l1_ffn_norm, m_l1_ffn_w_gate_up=m_l1_ffn_w_gate_up, m_l1_ffn_w_down=m_l1_ffn_w_down, m_l2_mix_norm=m_l2_mix_norm, m_l2_mla_w_down=m_l2_mla_w_down, m_l2_mla_q_a_norm=m_l2_mla_q_a_norm, m_l2_mla_kv_a_norm=m_l2_mla_kv_a_norm, m_l2_mla_w_uq=m_l2_mla_w_uq, m_l2_mla_w_ukv=m_l2_mla_w_ukv, m_l2_mla_q_nope_norm=m_l2_mla_q_nope_norm, m_l2_mla_q_rope_norm=m_l2_mla_q_rope_norm, m_l2_mla_k_nope_norm=m_l2_mla_k_nope_norm, m_l2_mla_k_rope_norm=m_l2_mla_k_rope_norm, m_l2_mla_w_out=m_l2_mla_w_out, m_l2_ffn_norm=m_l2_ffn_norm, m_l2_ffn_w_gate_up=m_l2_ffn_w_gate_up, m_l2_ffn_w_down=m_l2_ffn_w_down, m_l3_mix_norm=m_l3_mix_norm, m_l3_dn_w_in=m_l3_dn_w_in, m_l3_dn_conv_w=m_l3_dn_conv_w, m_l3_dn_a_log=m_l3_dn_a_log, m_l3_dn_dt_bias=m_l3_dn_dt_bias, m_l3_dn_out_norm=m_l3_dn_out_norm, m_l3_dn_w_out=m_l3_dn_w_out, m_l3_ffn_norm=m_l3_ffn_norm, m_l3_ffn_w_gate_up=m_l3_ffn_w_gate_up, m_l3_ffn_w_down=m_l3_ffn_w_down, v_l0_mix_norm=v_l0_mix_norm, v_l0_dn_w_in=v_l0_dn_w_in, v_l0_dn_conv_w=v_l0_dn_conv_w, v_l0_dn_a_log=v_l0_dn_a_log, v_l0_dn_dt_bias=v_l0_dn_dt_bias, v_l0_dn_out_norm=v_l0_dn_out_norm, v_l0_dn_w_out=v_l0_dn_w_out, v_l0_ffn_norm=v_l0_ffn_norm, v_l0_ffn_w_gate_up=v_l0_ffn_w_gate_up, v_l0_ffn_w_down=v_l0_ffn_w_down, v_l1_mix_norm=v_l1_mix_norm, v_l1_sb_w_qkv=v_l1_sb_w_qkv, v_l1_sb_q_norm=v_l1_sb_q_norm, v_l1_sb_k_norm=v_l1_sb_k_norm, v_l1_sb_w_out=v_l1_sb_w_out, v_l1_ffn_norm=v_l1_ffn_norm, v_l1_ffn_w_gate_up=v_l1_ffn_w_gate_up, v_l1_ffn_w_down=v_l1_ffn_w_down, v_l2_mix_norm=v_l2_mix_norm, v_l2_mla_w_down=v_l2_mla_w_down, v_l2_mla_q_a_norm=v_l2_mla_q_a_norm, v_l2_mla_kv_a_norm=v_l2_mla_kv_a_norm, v_l2_mla_w_uq=v_l2_mla_w_uq, v_l2_mla_w_ukv=v_l2_mla_w_ukv, v_l2_mla_q_nope_norm=v_l2_mla_q_nope_norm, v_l2_mla_q_rope_norm=v_l2_mla_q_rope_norm, v_l2_mla_k_nope_norm=v_l2_mla_k_nope_norm, v_l2_mla_k_rope_norm=v_l2_mla_k_rope_norm, v_l2_mla_w_out=v_l2_mla_w_out, v_l2_ffn_norm=v_l2_ffn_norm, v_l2_ffn_w_gate_up=v_l2_ffn_w_gate_up, v_l2_ffn_w_down=v_l2_ffn_w_down, v_l3_mix_norm=v_l3_mix_norm, v_l3_dn_w_in=v_l3_dn_w_in, v_l3_dn_conv_w=v_l3_dn_conv_w, v_l3_dn_a_log=v_l3_dn_a_log, v_l3_dn_dt_bias=v_l3_dn_dt_bias, v_l3_dn_out_norm=v_l3_dn_out_norm, v_l3_dn_w_out=v_l3_dn_w_out, v_l3_ffn_norm=v_l3_ffn_norm, v_l3_ffn_w_gate_up=v_l3_ffn_w_gate_up, v_l3_ffn_w_down=v_l3_ffn_w_down)
    weights = {n: given[n] for n in TWIN_WEIGHTS}
    shared = {n: given[n] for n in SHARED_INPUTS}
    per_example = {n: given[n] for n in ['x']}
    grad_fn = _jax.value_and_grad(_loss, argnums=(0, 1))

    def one_microbatch(ex, loss_target):
        ex = dict(ex)
        diff = ex.pop(TWIN_DIFF_INPUT)
        return grad_fn(weights, diff, {**shared, **ex}, loss_target)

    if N_MICROBATCH == 1:
        loss, (grad_w, grad_x) = one_microbatch(per_example, given["loss_target"])
    else:
        def body(carry, xs):
            loss_sum, grad_sum = carry
            l_k, (gw_k, gx_k) = one_microbatch(xs[0], xs[1])
            with _jax.named_scope("update"):
                return (loss_sum + l_k, _jax.tree.map(_jnp.add, grad_sum, gw_k)), gx_k

        init = (_jnp.zeros((), _jnp.float32), _jax.tree.map(_jnp.zeros_like, weights))
        (loss, grad_w), grad_x = _jax.lax.scan(body, init, (per_example, given["loss_target"]))
    with _jax.named_scope("update"):
        delta_w, new_m, new_v = {}, {}, {}
        for n in TWIN_WEIGHTS:
            delta_w[n], new_m[n], new_v[n] = _adamw(weights[n], grad_w[n], given["m_" + n], given["v_" + n])
    return (loss, grad_x, *[grad_w[n] for n in TWIN_WEIGHTS], *[delta_w[n] for n in TWIN_WEIGHTS],
            *[new_m[n] for n in TWIN_WEIGHTS], *[new_v[n] for n in TWIN_WEIGHTS])
```

```python
import functools
import math

import jax
import jax.numpy as jnp
from jax import lax
from jax.experimental import pallas as pl
from jax.experimental.pallas import tpu as pltpu

F32 = jnp.float32
BF16 = jnp.bfloat16
HI = lax.Precision.HIGHEST

LANES = 128
N_DEV = 8
N_HEADS = 8
HEAD = 128
NORM_EPS = 1e-6
DN_CHUNK = 64
ATT_BLOCK = 128
MLA_ROPE = 64
MLA_QK = 192
ROPE_THETA = 10000.0
VMEM_LIMIT = 56 * 1024 * 1024

ADAM_LR = 0.001
ADAM_B1 = 0.9
ADAM_B2 = 0.999
ADAM_EPS = 1e-08
ADAM_WD = 0.01
ADAM_STEP = 10


def _cparams(**kw):
    return pltpu.CompilerParams(vmem_limit_bytes=VMEM_LIMIT, **kw)


def _pick(n, cands):
    for c in cands:
        if c <= n and n % c == 0:
            return c
    return n


def _mm(name, a, b, *, ta=False, tb=False, out_dtype=F32, add=None, tm=None, tn=None, tk=None):
    if ta:
        K, M = a.shape
    else:
        M, K = a.shape
    N = b.shape[0] if tb else b.shape[1]
    tm = tm or _pick(M, (1024, 512, 256, 128))
    tn = tn or _pick(N, (1024, 512, 384, 256, 128))
    tk = tk or _pick(K, (1024, 1408, 512, 384, 256, 128))
    nk = K // tk
    dn = (((0 if ta else 1,), (1 if tb else 0,)), ((), ()))
    has_add = add is not None

    def kern(*refs):
        if has_add:
            a_ref, b_ref, add_ref, o_ref, acc_ref = refs
        else:
            a_ref, b_ref, o_ref, acc_ref = refs
        k = pl.program_id(2)
        part = lax.dot_general(a_ref[...].astype(BF16), b_ref[...].astype(BF16), dn, preferred_element_type=F32)

        @pl.when(k == 0)
        def _():
            acc_ref[...] = part

        @pl.when(k > 0)
        def _():
            acc_ref[...] += part

        @pl.when(k == nk - 1)
        def _():
            r = acc_ref[...]
            if has_add:
                r = r + add_ref[...]
            o_ref[...] = r.astype(out_dtype)

    a_spec = pl.BlockSpec((tk, tm), lambda i, j, k: (k, i)) if ta else pl.BlockSpec((tm, tk), lambda i, j, k: (i, k))
    b_spec = pl.BlockSpec((tn, tk), lambda i, j, k: (j, k)) if tb else pl.BlockSpec((tk, tn), lambda i, j, k: (k, j))
    in_specs = [a_spec, b_spec]
    args = [a, b]
    if has_add:
        in_specs.append(pl.BlockSpec((tm, tn), lambda i, j, k: (i, j)))
        args.append(add)
    return pl.pallas_call(
        kern, name=name,
        grid=(M // tm, N // tn, nk),
        in_specs=in_specs,
        out_specs=pl.BlockSpec((tm, tn), lambda i, j, k: (i, j)),
        out_shape=jax.ShapeDtypeStruct((M, N), out_dtype),
        scratch_shapes=[pltpu.VMEM((tm, tn), F32)],
        compiler_params=_cparams(dimension_semantics=("parallel", "parallel", "arbitrary")),
    )(*args)


def _rows(name, body, ins, outs, *, tt, consts=(), accs=()):
    in_specs, args = [], []
    first = ins[0][0] if isinstance(ins[0], tuple) else ins[0]
    t = first.shape[-2]
    tt = min(tt, t)
    for x in ins:
        if isinstance(x, tuple):
            arr, bs, im = x
            in_specs.append(pl.BlockSpec(bs, im))
            args.append(arr)
        else:
            in_specs.append(_row_spec(x.shape, tt))
            args.append(x)
    for c in consts:
        in_specs.append(pl.BlockSpec(c.shape, lambda i, _n=c.ndim: (0,) * _n))
        args.append(c)
    out_specs = [_row_spec(o.shape, tt) for o in outs]
    out_specs += [pl.BlockSpec(a.shape, lambda i, _n=len(a.shape): (0,) * _n) for a in accs]
    res = pl.pallas_call(
        body, name=name, grid=(t // tt,),
        in_specs=in_specs, out_specs=out_specs, out_shape=list(outs) + list(accs),
        compiler_params=_cparams(dimension_semantics=("arbitrary",)),
    )(*args)
    return res


def _row_spec(shape, tt):
    if len(shape) == 2:
        return pl.BlockSpec((tt, shape[1]), lambda i: (i, 0))
    return pl.BlockSpec((shape[0], tt, shape[2]), lambda i: (0, i, 0))


def _sds(shape, dtype=F32):
    return jax.ShapeDtypeStruct(tuple(shape), dtype)


def _acc(ref, val):
    i = pl.program_id(0)

    @pl.when(i == 0)
    def _():
        ref[...] = val

    @pl.when(i > 0)
    def _():
        ref[...] += val


def _rms(x, g):
    return x * lax.rsqrt(jnp.mean(x * x, axis=-1, keepdims=True) + NORM_EPS) * g


def _silu(x):
    return x / (1.0 + jnp.exp(-x))


def _softplus(x):
    return jnp.maximum(x, 0.0) + jnp.log(1.0 + jnp.exp(-jnp.abs(x)))


def _sigmoid(x):
    return 1.0 / (1.0 + jnp.exp(-x))


def _rmsnorm_fwd(name, x, g, tt=512):
    def body(x_ref, g_ref, h_ref):
        h_ref[...] = _rms(x_ref[...], g_ref[...]).astype(BF16)

    return _rows(name, body, [x], [_sds(x.shape, BF16)], tt=tt, consts=[g])[0]


def _rmsnorm_bwd(name, x, g, dh, dres, tt=512):
    def body(x_ref, dh_ref, dres_ref, g_ref, dx_ref, dg_ref):
        _, vjp = jax.vjp(_rms, x_ref[...], g_ref[...])
        dx, dg = vjp(dh_ref[...])
        dx_ref[...] = dx + dres_ref[...]
        _acc(dg_ref, dg)

    return _rows(name, body, [x, dh, dres], [_sds(x.shape)], tt=tt, consts=[g], accs=[_sds(g.shape)])


def _swiglu_act(gu):
    f = gu.shape[1] // 2

    def body(gu_ref, a_ref):
        a_ref[...] = (_silu(gu_ref[:, :f]) * gu_ref[:, f:]).astype(BF16)

    return _rows("swiglu_act", body, [gu], [_sds((gu.shape[0], f), BF16)], tt=256)[0]


def _swiglu_bwd(gu, dact):
    f = gu.shape[1] // 2

    def body(gu_ref, da_ref, dgu_ref):
        g, u, da = gu_ref[:, :f], gu_ref[:, f:], da_ref[...]
        s = _sigmoid(g)
        dgu_ref[:, :f] = (da * u * s * (1.0 + g * (1.0 - s))).astype(BF16)
        dgu_ref[:, f:] = (da * g * s).astype(BF16)

    return _rows("swiglu_bwd", body, [gu, dact], [_sds(gu.shape, BF16)], tt=256)[0]


def _ffn_fwd(x, norm_g, w_gu, w_down):
    h = _rmsnorm_fwd("ffn_norm", x, norm_g)
    gu = _mm("ffn_gate_up", h, w_gu)
    act = _swiglu_act(gu)
    y = _mm("ffn_down", act, w_down, add=x)
    return y, (x, h, gu, act)


def _ffn_bwd(saved, dy, norm_g, w_gu, w_down):
    x, h, gu, act = saved
    d_wdown = _mm("ffn_down_wgrad", act, dy, ta=True)
    dact = _mm("ffn_down_dgrad", dy, w_down, tb=True)
    dgu = _swiglu_bwd(gu, dact)
    d_wgu = _mm("ffn_gate_up_wgrad", h, dgu, ta=True)
    dh = _mm("ffn_gate_up_dgrad", dgu, w_gu, tb=True)
    dx, dg = _rmsnorm_bwd("ffn_norm_bwd", x, norm_g, dh, dy)
    return dx, dg, d_wgu, d_wdown


def _dot_nt(a, b):
    return lax.dot_general(a, b, (((1,), (1,)), ((), ())), preferred_element_type=F32)


def _dot_tn(a, b):
    return lax.dot_general(a, b, (((0,), (0,)), ((), ())), preferred_element_type=F32)


def _dot(a, b):
    return jnp.dot(a, b, preferred_element_type=F32)


def _dot_split(x, m):
    hi = x.astype(BF16)
    lo = (x - hi.astype(F32)).astype(BF16)
    return _dot(hi, m) + _dot(lo, m)


def _log_sigmoid(z):
    return jnp.minimum(z, 0.0) - jnp.log(1.0 + jnp.exp(-jnp.abs(z)))


def _heads_in(ref, h, width=HEAD):
    return ref[:, h * width:(h + 1) * width]


def _sb_qk(q, k, gq, gk):
    return _rms(q, gq) * (HEAD ** -0.5), _rms(k, gk)


def _sb_prep_fwd(qkv, gq, gk):
    t = qkv.shape[0]

    def body(x_ref, gq_ref, gk_ref, q_ref, k_ref, v_ref):
        for h in range(N_HEADS):
            q, k = _sb_qk(_heads_in(x_ref, h), _heads_in(x_ref, N_HEADS + h), gq_ref[...], gk_ref[...])
            q_ref[h] = q.astype(BF16)
            k_ref[h] = k.astype(BF16)
            v_ref[h] = _heads_in(x_ref, 2 * N_HEADS + h).astype(BF16)

    hm = _sds((N_HEADS, t, HEAD), BF16)
    return _rows("sb_prep", body, [qkv], [hm, hm, hm], tt=256, consts=[gq, gk])


def _sb_prep_bwd(qkv, gq, gk, dq, dk, dv):
    def body(x_ref, dq_ref, dk_ref, dv_ref, gq_ref, gk_ref, dx_ref, dgq_ref, dgk_ref):
        dgq = jnp.zeros(gq_ref.shape, F32)
        dgk = jnp.zeros(gk_ref.shape, F32)
        for h in range(N_HEADS):
            _, vjp = jax.vjp(_sb_qk, _heads_in(x_ref, h), _heads_in(x_ref, N_HEADS + h), gq_ref[...], gk_ref[...])
            a, b, c, d = vjp((dq_ref[h], dk_ref[h]))
            dx_ref[:, h * HEAD:(h + 1) * HEAD] = a.astype(BF16)
            dx_ref[:, (N_HEADS + h) * HEAD:(N_HEADS + h + 1) * HEAD] = b.astype(BF16)
            dx_ref[:, (2 * N_HEADS + h) * HEAD:(2 * N_HEADS + h + 1) * HEAD] = dv_ref[h].astype(BF16)
            dgq, dgk = dgq + c, dgk + d
        _acc(dgq_ref, dgq)
        _acc(dgk_ref, dgk)

    return _rows("sb_prep_bwd", body, [qkv, dq, dk, dv], [_sds(qkv.shape, BF16)], tt=256, consts=[gq, gk],
                 accs=[_sds(gq.shape), _sds(gk.shape)])


def _tri_masks():
    rows = lax.broadcasted_iota(jnp.int32, (ATT_BLOCK, ATT_BLOCK), 0)
    cols = lax.broadcasted_iota(jnp.int32, (ATT_BLOCK, ATT_BLOCK), 1)
    return rows, cols


def _sb_attn_fwd(q, k, v):
    nh, t, _ = q.shape

    def kern(q_ref, k_ref, v_ref, o_ref):
        qb = pl.program_id(1)
        qv = q_ref[0]
        rows, cols = _tri_masks()
        after = (rows > cols).astype(BF16)

        def body(i, carry):
            o_acc, run = carry
            kb = qb - i
            off = pl.multiple_of(kb * ATT_BLOCK, ATT_BLOCK)
            kv = k_ref[0, pl.ds(off, ATT_BLOCK), :]
            vv = v_ref[0, pl.ds(off, ATT_BLOCK), :]
            z = _dot_nt(qv, kv)
            past = (cols < rows) | (i > 0)
            lsz = _log_sigmoid(z)
            lsn = jnp.where(past, lsz - z, 0.0)
            la = _dot_split(lsn, after) + run
            a = jnp.where(past, jnp.exp(lsz + la), 0.0)
            o_acc = o_acc + _dot(a.astype(BF16), vv)
            run = run + jnp.sum(lsn, axis=1, keepdims=True)
            return o_acc, run

        o, _ = lax.fori_loop(0, qb + 1, body, (jnp.zeros((ATT_BLOCK, HEAD), F32), jnp.zeros((ATT_BLOCK, 1), F32)))
        o_ref[...] = o

    return pl.pallas_call(
        kern, name="sb_attn_fwd", grid=(nh, t // ATT_BLOCK),
        in_specs=[pl.BlockSpec((1, ATT_BLOCK, HEAD), lambda h, i: (h, i, 0)),
                  pl.BlockSpec((1, t, HEAD), lambda h, i: (h, 0, 0)),
                  pl.BlockSpec((1, t, HEAD), lambda h, i: (h, 0, 0))],
        out_specs=pl.BlockSpec((ATT_BLOCK, HEAD), lambda h, i: (i, h)),
        out_shape=_sds((t, nh * HEAD)),
        compiler_params=_cparams(dimension_semantics=("parallel", "arbitrary")),
    )(q, k, v)


def _sb_attn_bwd(q, k, v, do):
    nh, t, _ = q.shape
    nq = t // ATT_BLOCK

    def kern(q_ref, k_ref, v_ref, do_ref, dq_ref, dk_ref, dv_ref, g_s, ls_s):
        qb = pl.program_id(1)

        @pl.when(qb == 0)
        def _():
            dk_ref[...] = jnp.zeros(dk_ref.shape, F32)
            dv_ref[...] = jnp.zeros(dv_ref.shape, F32)

        qv = q_ref[0]
        dob = do_ref[...].astype(BF16)
        rows, cols = _tri_masks()
        after = (rows > cols).astype(BF16)
        before = (rows < cols).astype(BF16)

        def sweep_left(i, run):
            kb = qb - i
            off = pl.multiple_of(kb * ATT_BLOCK, ATT_BLOCK)
            kv = k_ref[0, pl.ds(off, ATT_BLOCK), :]
            vv = v_ref[0, pl.ds(off, ATT_BLOCK), :]
            z = _dot_nt(qv, kv)
            past = (cols < rows) | (i > 0)
            lsz = _log_sigmoid(z)
            lsn = jnp.where(past, lsz - z, 0.0)
            la = _dot_split(lsn, after) + run
            a = jnp.where(past, jnp.exp(lsz + la), 0.0)
            g_s[kb] = _dot_nt(dob, vv) * a
            ls_s[kb] = lsz
            dv_ref[0, pl.ds(off, ATT_BLOCK), :] += _dot_tn(a.astype(BF16), dob)
            return run + jnp.sum(lsn, axis=1, keepdims=True)

        zero = jnp.zeros((ATT_BLOCK, 1), F32)
        lax.fori_loop(0, qb + 1, sweep_left, zero)

        def sweep_right(kb, carry):
            dq_acc, run_g = carry
            off = pl.multiple_of(kb * ATT_BLOCK, ATT_BLOCK)
            kv = k_ref[0, pl.ds(off, ATT_BLOCK), :]
            g = g_s[kb]
            sg = jnp.exp(ls_s[kb])
            past = (cols < rows) | (kb < qb)
            dls = run_g + _dot_split(g, before)
            dzb = jnp.where(past, g * (1.0 - sg) - dls * sg, 0.0).astype(BF16)
            dk_ref[0, pl.ds(off, ATT_BLOCK), :] += _dot_tn(dzb, qv)
            return dq_acc + _dot(dzb, kv), run_g + jnp.sum(g, axis=1, keepdims=True)

        dq, _ = lax.fori_loop(0, qb + 1, sweep_right, (jnp.zeros((ATT_BLOCK, HEAD), F32), zero))
        dq_ref[0] = dq

    hm = _sds((nh, t, HEAD))
    full = pl.BlockSpec((1, t, HEAD), lambda h, i: (h, 0, 0))
    tok = pl.BlockSpec((ATT_BLOCK, HEAD), lambda h, i: (i, h))
    return pl.pallas_call(
        kern, name="sb_attn_bwd", grid=(nh, nq),
        in_specs=[pl.BlockSpec((1, ATT_BLOCK, HEAD), lambda h, i: (h, i, 0)), full, full, tok],
        out_specs=[pl.BlockSpec((1, ATT_BLOCK, HEAD), lambda h, i: (h, i, 0)), full, full],
        out_shape=[hm, hm, hm],
        scratch_shapes=[pltpu.VMEM((nq, ATT_BLOCK, ATT_BLOCK), F32), pltpu.VMEM((nq, ATT_BLOCK, ATT_BLOCK), F32)],
        compiler_params=_cparams(dimension_semantics=("parallel", "arbitrary")),
    )(q, k, v, do)


def _sb_fwd(x, h, w_qkv, gq, gk, w_out):
    qkv = _mm("sb_qkv", h, w_qkv)
    q, k, v = _sb_prep_fwd(qkv, gq, gk)
    o = _sb_attn_fwd(q, k, v)
    y = _mm("sb_out", o, w_out, add=x)
    return y, (h, qkv, q, k, v, o)


def _sb_bwd(saved, dy, w_qkv, gq, gk, w_out):
    h, qkv, q, k, v, o = saved
    d_wout = _mm("sb_out_wgrad", o, dy, ta=True)
    do = _mm("sb_out_dgrad", dy, w_out, tb=True)
    dq, dk, dv = _sb_attn_bwd(q, k, v, do)
    dqkv, dgq, dgk = _sb_prep_bwd(qkv, gq, gk, dq, dk, dv)
    d_wqkv = _mm("sb_qkv_wgrad", h, dqkv, ta=True)
    dh = _mm("sb_qkv_dgrad", dqkv, w_qkv, tb=True)
    return dh, d_wqkv, dgq, dgk, d_wout


DN_QKV = 3 * N_HEADS * HEAD
DN_PROJ = DN_QKV + N_HEADS * HEAD + LANES
DN_CONV = 4
HALO = 8
CONV_COLS = 512


def _dn_conv_fwd(proj, conv_w, tt=256):
    t = proj.shape[0]
    tt = min(tt, t)

    def body(u_ref, prev_ref, w_ref, c_ref):
        i = pl.program_id(0)
        for cc in range(DN_QKV // CONV_COLS):
            cs = slice(cc * CONV_COLS, (cc + 1) * CONV_COLS)
            cur = u_ref[:, cs]
            prev = jnp.where(i > 0, prev_ref[:, cs], 0.0)
            ext = jnp.concatenate([prev, cur], axis=0)
            y = cur * w_ref[DN_CONV - 1:DN_CONV, cs]
            for j in range(DN_CONV - 1):
                y = y + pltpu.roll(ext, DN_CONV - 1 - j, 0)[HALO:] * w_ref[j:j + 1, cs]
            c_ref[:, cs] = y

    return _rows("dn_conv", body,
                 [(proj, (tt, DN_QKV), lambda i: (i, 0)),
                  (proj, (HALO, DN_QKV), lambda i: (jnp.maximum(i * (tt // HALO) - 1, 0), 0))],
                 [_sds((t, DN_QKV))], tt=tt, consts=[conv_w])[0]


def _dn_conv_bwd(proj, conv_w, dc, dz, dab, tt=256):
    t = proj.shape[0]
    tt = min(tt, t)
    nblk = t // tt

    def body(u_ref, prev_ref, dc_ref, next_ref, dz_ref, dab_ref, w_ref, dp_ref, dw_ref):
        i = pl.program_id(0)
        dws = []
        for cc in range(DN_QKV // CONV_COLS):
            cs = slice(cc * CONV_COLS, (cc + 1) * CONV_COLS)
            cur = u_ref[:, cs]
            prev = jnp.where(i > 0, prev_ref[:, cs], 0.0)
            ext_u = jnp.concatenate([prev, cur], axis=0)
            d = dc_ref[:, cs]
            nxt = jnp.where(i < nblk - 1, next_ref[:, cs], 0.0)
            ext_d = jnp.concatenate([d, nxt], axis=0)
            du = d * w_ref[DN_CONV - 1:DN_CONV, cs]
            rows = [jnp.sum(d * cur, axis=0, keepdims=True)]
            for j in range(DN_CONV - 2, -1, -1):
                sh = DN_CONV - 1 - j
                du = du + pltpu.roll(ext_d, tt + HALO - sh, 0)[:tt] * w_ref[j:j + 1, cs]
                rows.insert(0, jnp.sum(d * pltpu.roll(ext_u, sh, 0)[HALO:], axis=0, keepdims=True))
            dp_ref[:, cs] = du.astype(BF16)
            dws.append(jnp.concatenate(rows, axis=0))
        dp_ref[:, DN_QKV:DN_QKV + N_HEADS * HEAD] = dz_ref[...].astype(BF16)
        dp_ref[:, DN_QKV + N_HEADS * HEAD:] = dab_ref[...].astype(BF16)
        _acc(dw_ref, jnp.concatenate(dws, axis=1))

    return _rows("dn_conv_bwd", body,
                 [(proj, (tt, DN_QKV), lambda i: (i, 0)),
                  (proj, (HALO, DN_QKV), lambda i: (jnp.maximum(i * (tt // HALO) - 1, 0), 0)),
                  dc,
                  (dc, (HALO, DN_QKV), lambda i: (jnp.minimum((i + 1) * (tt // HALO), t // HALO - 1), 0)),
                  dz, dab],
                 [_sds((t, DN_PROJ), BF16)], tt=tt, consts=[conv_w], accs=[_sds(conv_w.shape)])


def _l2n(x):
    return x * lax.rsqrt(jnp.sum(x * x, axis=-1, keepdims=True) + NORM_EPS)


def _dn_qkv(cq, ck, cv):
    return _l2n(_silu(cq)) * (HEAD ** -0.5), _l2n(_silu(ck)), _silu(cv)


def _dn_gates(ab, a_log, dt_bias):
    lane = lax.broadcasted_iota(jnp.int32, ab.shape, 1)
    g = -jnp.exp(a_log) * _softplus(ab + dt_bias)
    return jnp.where(lane < N_HEADS, g, jnp.where(lane < 2 * N_HEADS, _sigmoid(ab), 0.0))


def _ab_spec(tt):
    return (tt, LANES), lambda i: (i, DN_PROJ // LANES - 1)


def _dn_prep_fwd(c, proj, a_log, dt_bias, tt=256):
    t = c.shape[0]
    tt = min(tt, t)

    def body(c_ref, ab_ref, al_ref, dt_ref, q_ref, k_ref, v_ref, g_ref):
        for h in range(N_HEADS):
            q_ref[h], k_ref[h], v_ref[h] = _dn_qkv(_heads_in(c_ref, h), _heads_in(c_ref, N_HEADS + h), _heads_in(c_ref, 2 * N_HEADS + h))
        g_ref[...] = _dn_gates(ab_ref[...], al_ref[...], dt_ref[...])

    hm = _sds((N_HEADS, t, HEAD))
    return _rows("dn_prep", body, [c, (proj,) + _ab_spec(tt)], [hm, hm, hm, _sds((t, LANES))], tt=tt, consts=[a_log, dt_bias])


def _dn_prep_bwd(c, proj, a_log, dt_bias, dq, dk, dv, dgates, tt=256):
    t = c.shape[0]
    tt = min(tt, t)

    def body(c_ref, ab_ref, dq_ref, dk_ref, dv_ref, dg_ref, al_ref, dt_ref, dc_ref, dab_ref, dal_ref, ddt_ref):
        for h in range(N_HEADS):
            _, vjp = jax.vjp(_dn_qkv, _heads_in(c_ref, h), _heads_in(c_ref, N_HEADS + h), _heads_in(c_ref, 2 * N_HEADS + h))
            a, b, d = vjp((dq_ref[h], dk_ref[h], dv_ref[h]))
            dc_ref[:, h * HEAD:(h + 1) * HEAD] = a
            dc_ref[:, (N_HEADS + h) * HEAD:(N_HEADS + h + 1) * HEAD] = b
            dc_ref[:, (2 * N_HEADS + h) * HEAD:(2 * N_HEADS + h + 1) * HEAD] = d
        _, vjp = jax.vjp(_dn_gates, ab_ref[...], al_ref[...], dt_ref[...])
        dab, dal, ddt = vjp(dg_ref[...])
        dab_ref[...] = dab
        _acc(dal_ref, dal)
        _acc(ddt_ref, ddt)

    return _rows("dn_prep_bwd", body, [c, (proj,) + _ab_spec(tt), dq, dk, dv, dgates], [_sds(c.shape), _sds((t, LANES))],
                 tt=tt, consts=[a_log, dt_bias], accs=[_sds(a_log.shape), _sds(dt_bias.shape)])


def _hdot(a, b):
    return jnp.dot(a, b, precision=HI, preferred_element_type=F32)


def _inv_raw(low):
    c = low.shape[0]
    r = lax.broadcasted_iota(jnp.int32, (c, c), 0)
    s = lax.broadcasted_iota(jnp.int32, (c, c), 1)
    m = jnp.where(r == s, 1.0, 0.0) - low
    p = _hdot(low, low)
    n_fac = int(math.log2(c)) - 1
    for i in range(n_fac):
        m = m + _hdot(m, p)
        if i < n_fac - 1:
            p = _hdot(p, p)
    return m


@jax.custom_vjp
def _inv_unit_lower(low):
    return _inv_raw(low)


def _inv_fwd(low):
    m = _inv_raw(low)
    return m, m


def _inv_bwd(m, dm):
    mt_dm = lax.dot_general(m, dm, (((0,), (0,)), ((), ())), precision=HI, preferred_element_type=F32)
    return (-lax.dot_general(mt_dm, m, (((1,), (1,)), ((), ())), precision=HI, preferred_element_type=F32),)


_inv_unit_lower.defvjp(_inv_fwd, _inv_bwd)


def _dn_chunk(q, k, v, gates, s, h):
    c = q.shape[0]
    lane = lax.broadcasted_iota(jnp.int32, gates.shape, 1)
    g_col = jnp.sum(jnp.where(lane == h, gates, 0.0), axis=1, keepdims=True)
    b_col = jnp.sum(jnp.where(lane == h + N_HEADS, gates, 0.0), axis=1, keepdims=True)
    r = lax.broadcasted_iota(jnp.int32, (c, c), 0)
    cc = lax.broadcasted_iota(jnp.int32, (c, c), 1)
    causal, strict = r >= cc, r > cc
    incl = jnp.where(causal, 1.0, 0.0)
    upper = jnp.where(r <= cc, 1.0, 0.0)
    gb = jnp.broadcast_to(g_col, (c, LANES))
    gbc = jnp.broadcast_to(g_col, (c, c))
    gc = _hdot(incl, gb)
    gc_r = _hdot(incl, gbc)
    gc_c = lax.dot_general(gbc, upper, (((0,), (0,)), ((), ())), precision=HI, preferred_element_type=F32)
    decay = jnp.where(causal, jnp.exp(jnp.where(causal, gc_r - gc_c, 0.0)), 0.0)
    kb = k * b_col
    low = jnp.where(strict, _dot_nt(kb, k) * decay, 0.0)
    m = _inv_unit_lower(low)
    egc = jnp.exp(gc)
    u = _hdot(m, v * b_col)
    w = _hdot(m, kb * egc)
    attn = _dot_nt(q, k) * decay
    gl = jnp.sum(gb, axis=0, keepdims=True)
    v_new = u - _dot(w, s)
    o = _dot(q * egc, s) + _dot(attn, v_new)
    s_new = s * jnp.exp(gl) + _dot_tn(k * jnp.exp(gl - gc), v_new)
    return o, s_new


def _dn_chunks_fwd(q, k, v, gates):
    nh, t, _ = q.shape
    n = t // DN_CHUNK

    def kern(q_ref, k_ref, v_ref, g_ref, o_ref, sin_ref, s_scr):
        @pl.when(pl.program_id(0) == 0)
        def _():
            s_scr[...] = jnp.zeros(s_scr.shape, F32)

        gates_v = g_ref[...]

        def head(h, carry):
            s = s_scr[h]
            sin_ref[0, h] = s
            o, s_new = _dn_chunk(q_ref[h], k_ref[h], v_ref[h], gates_v, s, h)
            o_ref[h] = o
            s_scr[h] = s_new
            return carry

        lax.fori_loop(0, nh, head, 0)

    blk = pl.BlockSpec((nh, DN_CHUNK, HEAD), lambda i: (0, i, 0))
    return pl.pallas_call(
        kern, name="dn_chunks_fwd", grid=(n,),
        in_specs=[blk, blk, blk, pl.BlockSpec((DN_CHUNK, LANES), lambda i: (i, 0))],
        out_specs=[blk, pl.BlockSpec((1, nh, HEAD, HEAD), lambda i: (i, 0, 0, 0))],
        out_shape=[_sds((nh, t, HEAD)), _sds((n, nh, HEAD, HEAD))],
        scratch_shapes=[pltpu.VMEM((nh, HEAD, HEAD), F32)],
        compiler_params=_cparams(dimension_semantics=("arbitrary",)),
    )(q, k, v, gates)


def _dn_chunks_bwd(q, k, v, gates, s_in, do):
    nh, t, _ = q.shape
    n = t // DN_CHUNK

    def kern(q_ref, k_ref, v_ref, g_ref, sin_ref, do_ref, dq_ref, dk_ref, dv_ref, dg_ref, ds_scr):
        @pl.when(pl.program_id(0) == 0)
        def _():
            ds_scr[...] = jnp.zeros(ds_scr.shape, F32)

        gates_v = g_ref[...]

        def head(h, dgates):
            _, vjp = jax.vjp(lambda a, b, c, g, s: _dn_chunk(a, b, c, g, s, h), q_ref[h], k_ref[h], v_ref[h], gates_v, sin_ref[0, h])
            dq, dk, dv, dg, ds = vjp((do_ref[h], ds_scr[h]))
            dq_ref[h] = dq
            dk_ref[h] = dk
            dv_ref[h] = dv
            ds_scr[h] = ds
            return dgates + dg

        dg_ref[...] = lax.fori_loop(0, nh, head, jnp.zeros((DN_CHUNK, LANES), F32))

    blk = pl.BlockSpec((nh, DN_CHUNK, HEAD), lambda i: (0, n - 1 - i, 0))
    gblk = pl.BlockSpec((DN_CHUNK, LANES), lambda i: (n - 1 - i, 0))
    hm = _sds((nh, t, HEAD))
    return pl.pallas_call(
        kern, name="dn_chunks_bwd", grid=(n,),
        in_specs=[blk, blk, blk, gblk, pl.BlockSpec((1, nh, HEAD, HEAD), lambda i: (n - 1 - i, 0, 0, 0)), blk],
        out_specs=[blk, blk, blk, gblk],
        out_shape=[hm, hm, hm, _sds((t, LANES))],
        scratch_shapes=[pltpu.VMEM((nh, HEAD, HEAD), F32)],
        compiler_params=_cparams(dimension_semantics=("arbitrary",)),
    )(q, k, v, gates, s_in, do)


def _dn_gate_out(o, z, g):
    return _rms(o, g) * _silu(z)


def _z_spec(tt):
    return (tt, N_HEADS * HEAD), lambda i: (i, DN_QKV // (N_HEADS * HEAD))


def _dn_post_fwd(o, proj, out_norm, tt=256):
    t = o.shape[1]
    tt = min(tt, t)

    def body(o_ref, z_ref, g_ref, y_ref):
        for h in range(N_HEADS):
            y_ref[:, h * HEAD:(h + 1) * HEAD] = _dn_gate_out(o_ref[h], _heads_in(z_ref, h), g_ref[...]).astype(BF16)

    return _rows("dn_post", body, [o, (proj,) + _z_spec(tt)], [_sds((t, N_HEADS * HEAD), BF16)], tt=tt, consts=[out_norm])[0]


def _dn_post_bwd(o, proj, out_norm, dy, tt=256):
    t = o.shape[1]
    tt = min(tt, t)

    def body(o_ref, z_ref, dy_ref, g_ref, do_ref, dz_ref, dg_ref):
        dg = jnp.zeros(g_ref.shape, F32)
        for h in range(N_HEADS):
            _, vjp = jax.vjp(_dn_gate_out, o_ref[h], _heads_in(z_ref, h), g_ref[...])
            a, b, d = vjp(_heads_in(dy_ref, h))
            do_ref[h] = a
            dz_ref[:, h * HEAD:(h + 1) * HEAD] = b
            dg = dg + d
        _acc(dg_ref, dg)

    return _rows("dn_post_bwd", body, [o, (proj,) + _z_spec(tt), dy], [_sds(o.shape), _sds((t, N_HEADS * HEAD))], tt=tt,
                 consts=[out_norm], accs=[_sds(out_norm.shape)])


def _dn_fwd(x, h, w_in, conv_w, a_log, dt_bias, out_norm, w_out):
    proj = _mm("dn_in", h, w_in)
    c = _dn_conv_fwd(proj, conv_w)
    q, k, v, gates = _dn_prep_fwd(c, proj, a_log, dt_bias)
    o, s_in = _dn_chunks_fwd(q, k, v, gates)
    on = _dn_post_fwd(o, proj, out_norm)
    y = _mm("dn_out", on, w_out, add=x)
    return y, (h, proj, c, q, k, v, gates, o, s_in, on)


def _dn_bwd(saved, dy, w_in, conv_w, a_log, dt_bias, out_norm, w_out):
    h, proj, c, q, k, v, gates, o, s_in, on = saved
    d_wout = _mm("dn_out_wgrad", on, dy, ta=True)
    don = _mm("dn_out_dgrad", dy, w_out, tb=True)
    do, dz, d_out_norm = _dn_post_bwd(o, proj, out_norm, don)
    dq, dk, dv, dgates = _dn_chunks_bwd(q, k, v, gates, s_in, do)
    dc, dab, d_a_log, d_dt_bias = _dn_prep_bwd(c, proj, a_log, dt_bias, dq, dk, dv, dgates)
    dproj, d_conv_w = _dn_conv_bwd(proj, conv_w, dc, dz, dab)
    d_win = _mm("dn_in_wgrad", h, dproj, ta=True)
    dh = _mm("dn_in_dgrad", dproj, w_in, tb=True)
    return dh, d_win, d_conv_w, d_a_log, d_dt_bias, d_out_norm, d_wout


MLA_SCALE = MLA_QK ** -0.5
MLA_C = 512


def _swap_raw(x):
    lane = lax.broadcasted_iota(jnp.int32, x.shape, 1)
    half = MLA_ROPE // 2
    y = jnp.where(lane < half, pltpu.roll(x, LANES - half, 1), pltpu.roll(x, half, 1))
    return jnp.where(lane < MLA_ROPE, y, 0.0)


@jax.custom_vjp
def _swap_halves(x):
    return _swap_raw(x)


_swap_halves.defvjp(lambda x: (_swap_raw(x), None), lambda _, d: (_swap_raw(d),))


def _rms_rope(x, g, cos, sin):
    y = x * lax.rsqrt(jnp.sum(x * x, axis=-1, keepdims=True) * (1.0 / MLA_ROPE) + NORM_EPS) * g
    return y * cos + _swap_halves(y) * sin


def _mla_latent(cq, ckv, kr, gq, gkv, gkr, cos, sin):
    return _rms(cq, gq), _rms(ckv, gkv), _rms_rope(kr, gkr, cos, sin)


def _mla_prep1_fwd(c, gq, gkv, gkr, cos, sin):
    t = c.shape[0]

    def body(c_ref, cos_ref, sin_ref, gq_ref, gkv_ref, gkr_ref, cq_ref, ckv_ref, kr_ref):
        a, b, r = _mla_latent(c_ref[:, :256], c_ref[:, 256:384], c_ref[:, 384:], gq_ref[...], gkv_ref[...], gkr_ref[...],
                              cos_ref[...], sin_ref[...])
        cq_ref[...] = a.astype(BF16)
        ckv_ref[...] = b.astype(BF16)
        kr_ref[...] = r.astype(BF16)

    return _rows("mla_prep1", body, [c, cos, sin], [_sds((t, 256), BF16), _sds((t, HEAD), BF16), _sds((t, HEAD), BF16)],
                 tt=512, consts=[gq, gkv, gkr])


def _mla_prep1_bwd(c, gq, gkv, gkr, cos, sin, dcq, dckv, dkr_heads):
    def body(c_ref, cos_ref, sin_ref, dcq_ref, dckv_ref, dkr_ref, gq_ref, gkv_ref, gkr_ref, dc_ref, dgq_ref, dgkv_ref, dgkr_ref):
        dkr = dkr_ref[0]
        for h in range(1, N_HEADS):
            dkr = dkr + dkr_ref[h]
        _, vjp = jax.vjp(_mla_latent, c_ref[:, :256], c_ref[:, 256:384], c_ref[:, 384:], gq_ref[...], gkv_ref[...], gkr_ref[...],
                         cos_ref[...], sin_ref[...])
        a, b, r, d1, d2, d3, _, _ = vjp((dcq_ref[...], dckv_ref[...], dkr))
        dc_ref[:, :256] = a.astype(BF16)
        dc_ref[:, 256:384] = b.astype(BF16)
        dc_ref[:, 384:] = r.astype(BF16)
        _acc(dgq_ref, d1)
        _acc(dgkv_ref, d2)
        _acc(dgkr_ref, d3)

    return _rows("mla_prep1_bwd", body, [c, cos, sin, dcq, dckv, dkr_heads], [_sds(c.shape, BF16)], tt=512,
                 consts=[gq, gkv, gkr], accs=[_sds(gq.shape), _sds(gkv.shape), _sds(gkr.shape)])


def _mla_heads(qn, qr, kn, gqn, gqr, gkn, cos, sin):
    return _rms(qn, gqn) * MLA_SCALE, _rms_rope(qr, gqr, cos, sin) * MLA_SCALE, _rms(kn, gkn)


def _mla_prep2_fwd(qa, kv, gqn, gqr, gkn, cos, sin):
    t = qa.shape[0]

    def body(qa_ref, kv_ref, cos_ref, sin_ref, gqn_ref, gqr_ref, gkn_ref, qn_ref, qr_ref, kn_ref, v_ref):
        for h in range(N_HEADS):
            a, b, c = _mla_heads(_heads_in(qa_ref, h), _heads_in(qa_ref, N_HEADS + h), _heads_in(kv_ref, h),
                                 gqn_ref[...], gqr_ref[...], gkn_ref[...], cos_ref[...], sin_ref[...])
            qn_ref[h] = a.astype(BF16)
            qr_ref[h] = b.astype(BF16)
            kn_ref[h] = c.astype(BF16)
            v_ref[h] = _heads_in(kv_ref, N_HEADS + h).astype(BF16)

    hm = _sds((N_HEADS, t, HEAD), BF16)
    return _rows("mla_prep2", body, [qa, kv, cos, sin], [hm, hm, hm, hm], tt=256, consts=[gqn, gqr, gkn])


def _mla_prep2_bwd(qa, kv, gqn, gqr, gkn, cos, sin, dqn, dqr, dkn, dv):
    def body(qa_ref, kv_ref, cos_ref, sin_ref, dqn_ref, dqr_ref, dkn_ref, dv_ref, gqn_ref, gqr_ref, gkn_ref,
             dqa_ref, dkv_ref, d1_ref, d2_ref, d3_ref):
        d1 = jnp.zeros(gqn_ref.shape, F32)
        d2 = jnp.zeros(gqr_ref.shape, F32)
        d3 = jnp.zeros(gkn_ref.shape, F32)
        for h in range(N_HEADS):
            _, vjp = jax.vjp(_mla_heads, _heads_in(qa_ref, h), _heads_in(qa_ref, N_HEADS + h), _heads_in(kv_ref, h),
                             gqn_ref[...], gqr_ref[...], gkn_ref[...], cos_ref[...], sin_ref[...])
            a, b, c, e1, e2, e3, _, _ = vjp((dqn_ref[h], dqr_ref[h], dkn_ref[h]))
            dqa_ref[:, h * HEAD:(h + 1) * HEAD] = a.astype(BF16)
            dqa_ref[:, (N_HEADS + h) * HEAD:(N_HEADS + h + 1) * HEAD] = b.astype(BF16)
            dkv_ref[:, h * HEAD:(h + 1) * HEAD] = c.astype(BF16)
            dkv_ref[:, (N_HEADS + h) * HEAD:(N_HEADS + h + 1) * HEAD] = dv_ref[h].astype(BF16)
            d1, d2, d3 = d1 + e1, d2 + e2, d3 + e3
        _acc(d1_ref, d1)
        _acc(d2_ref, d2)
        _acc(d3_ref, d3)

    return _rows("mla_prep2_bwd", body, [qa, kv, cos, sin, dqn, dqr, dkn, dv], [_sds(qa.shape, BF16), _sds(kv.shape, BF16)],
                 tt=256, consts=[gqn, gqr, gkn], accs=[_sds(gqn.shape), _sds(gqr.shape), _sds(gkn.shape)])


def _mla_attn_fwd(qn, qr, kn, kr, v):
    nh, t, _ = qn.shape

    def kern(qn_ref, qr_ref, kn_ref, kr_ref, v_ref, o_ref, lse_ref):
        qb = pl.program_id(1)
        qnv, qrv = qn_ref[0], qr_ref[0]
        rows, cols = _tri_masks()

        def body(kb, carry):
            acc, m, l = carry
            off = pl.multiple_of(kb * ATT_BLOCK, ATT_BLOCK)
            s = _dot_nt(qnv, kn_ref[0, pl.ds(off, ATT_BLOCK), :]) + _dot_nt(qrv, kr_ref[pl.ds(off, ATT_BLOCK), :])
            s = jnp.where((cols <= rows) | (kb < qb), s, -jnp.inf)
            m_new = jnp.maximum(m, jnp.max(s, axis=1, keepdims=True))
            alpha = jnp.exp(m - m_new)
            p = jnp.exp(s - m_new)
            acc = acc * alpha + _dot(p.astype(BF16), v_ref[0, pl.ds(off, ATT_BLOCK), :])
            return acc, m_new, l * alpha + jnp.sum(p, axis=1, keepdims=True)

        init = (jnp.zeros((ATT_BLOCK, HEAD), F32), jnp.full((ATT_BLOCK, 1), -jnp.inf, F32), jnp.zeros((ATT_BLOCK, 1), F32))
        acc, m, l = lax.fori_loop(0, qb + 1, body, init)
        o_ref[...] = acc / l
        lse_ref[...] = jnp.broadcast_to(m + jnp.log(l), (ATT_BLOCK, HEAD))

    blk = pl.BlockSpec((1, ATT_BLOCK, HEAD), lambda h, i: (h, i, 0))
    full = pl.BlockSpec((1, t, HEAD), lambda h, i: (h, 0, 0))
    tok = pl.BlockSpec((ATT_BLOCK, HEAD), lambda h, i: (i, h))
    return pl.pallas_call(
        kern, name="mla_attn_fwd", grid=(nh, t // ATT_BLOCK),
        in_specs=[blk, blk, full, pl.BlockSpec((t, HEAD), lambda h, i: (0, 0)), full],
        out_specs=[tok, tok], out_shape=[_sds((t, nh * HEAD)), _sds((t, nh * HEAD))],
        compiler_params=_cparams(dimension_semantics=("parallel", "arbitrary")),
    )(qn, qr, kn, kr, v)


def _mla_attn_bwd(qn, qr, kn, kr, v, o, lse, do):
    nh, t, _ = qn.shape

    def kern(qn_ref, qr_ref, kn_ref, kr_ref, v_ref, o_ref, lse_ref, do_ref, dqn_ref, dqr_ref, dkn_ref, dkr_ref, dv_ref):
        qb = pl.program_id(1)

        @pl.when(qb == 0)
        def _():
            dkn_ref[...] = jnp.zeros(dkn_ref.shape, F32)
            dkr_ref[...] = jnp.zeros(dkr_ref.shape, F32)
            dv_ref[...] = jnp.zeros(dv_ref.shape, F32)

        qnv, qrv = qn_ref[0], qr_ref[0]
        dov = do_ref[...]
        dob = dov.astype(BF16)
        delta = jnp.sum(dov * o_ref[...], axis=1, keepdims=True)
        lse_col = lse_ref[:, :1]
        rows, cols = _tri_masks()

        def body(kb, carry):
            dqn, dqr = carry
            off = pl.multiple_of(kb * ATT_BLOCK, ATT_BLOCK)
            knv = kn_ref[0, pl.ds(off, ATT_BLOCK), :]
            krv = kr_ref[pl.ds(off, ATT_BLOCK), :]
            vv = v_ref[0, pl.ds(off, ATT_BLOCK), :]
            s = _dot_nt(qnv, knv) + _dot_nt(qrv, krv)
            p = jnp.where((cols <= rows) | (kb < qb), jnp.exp(s - lse_col), 0.0)
            ds = (p * (_dot_nt(dob, vv) - delta)).astype(BF16)
            dkn_ref[0, pl.ds(off, ATT_BLOCK), :] += _dot_tn(ds, qnv)
            dkr_ref[0, pl.ds(off, ATT_BLOCK), :] += _dot_tn(ds, qrv)
            dv_ref[0, pl.ds(off, ATT_BLOCK), :] += _dot_tn(p.astype(BF16), dob)
            return dqn + _dot(ds, knv), dqr + _dot(ds, krv)

        zero = jnp.zeros((ATT_BLOCK, HEAD), F32)
        dqn, dqr = lax.fori_loop(0, qb + 1, body, (zero, zero))
        dqn_ref[0] = dqn
        dqr_ref[0] = dqr

    hm = _sds((nh, t, HEAD))
    blk = pl.BlockSpec((1, ATT_BLOCK, HEAD), lambda h, i: (h, i, 0))
    full = pl.BlockSpec((1, t, HEAD), lambda h, i: (h, 0, 0))
    tok = pl.BlockSpec((ATT_BLOCK, HEAD), lambda h, i: (i, h))
    return pl.pallas_call(
        kern, name="mla_attn_bwd", grid=(nh, t // ATT_BLOCK),
        in_specs=[blk, blk, full, pl.BlockSpec((t, HEAD), lambda h, i: (0, 0)), full, tok, tok, tok],
        out_specs=[blk, blk, full, full, full], out_shape=[hm, hm, hm, hm, hm],
        compiler_params=_cparams(dimension_semantics=("parallel", "arbitrary")),
    )(qn, qr, kn, kr, v, o, lse, do)


def _rope_tables(t):
    inv_freq = ROPE_THETA ** (-jnp.arange(0, MLA_ROPE, 2, dtype=F32) / MLA_ROPE)
    ang = jnp.arange(t, dtype=F32)[:, None] * inv_freq[None, :]
    c, s = jnp.cos(ang), jnp.sin(ang)
    pad = ((0, 0), (0, LANES - MLA_ROPE))
    return jnp.pad(jnp.concatenate([c, c], axis=1), pad), jnp.pad(jnp.concatenate([-s, s], axis=1), pad)


def _pad_lanes(v, n=LANES):
    return jnp.pad(v, (0, n - v.shape[0])).reshape(1, n)


def _mla_layout(w_down, w_uq, w_ukv):
    w_down_p = jnp.pad(w_down, ((0, 0), (0, MLA_C - w_down.shape[1])))
    uq = w_uq.reshape(w_uq.shape[0], N_HEADS, MLA_QK)
    rope = jnp.pad(uq[:, :, HEAD:], ((0, 0), (0, 0), (0, LANES - MLA_ROPE)))
    w_uq_p = jnp.concatenate([uq[:, :, :HEAD].reshape(-1, N_HEADS * HEAD), rope.reshape(-1, N_HEADS * LANES)], axis=1)
    ukv = w_ukv.reshape(w_ukv.shape[0], N_HEADS, 2 * HEAD)
    w_ukv_p = jnp.concatenate([ukv[:, :, :HEAD].reshape(-1, N_HEADS * HEAD), ukv[:, :, HEAD:].reshape(-1, N_HEADS * HEAD)], axis=1)
    return w_down_p, w_uq_p, w_ukv_p


def _mla_unlayout(d_down_p, d_uq_p, d_ukv_p):
    d_down = d_down_p[:, :256 + HEAD + MLA_ROPE]
    nope = d_uq_p[:, :N_HEADS * HEAD].reshape(-1, N_HEADS, HEAD)
    rope = d_uq_p[:, N_HEADS * HEAD:].reshape(-1, N_HEADS, LANES)[:, :, :MLA_ROPE]
    d_uq = jnp.concatenate([nope, rope], axis=2).reshape(-1, N_HEADS * MLA_QK)
    kn = d_ukv_p[:, :N_HEADS * HEAD].reshape(-1, N_HEADS, HEAD)
    vv = d_ukv_p[:, N_HEADS * HEAD:].reshape(-1, N_HEADS, HEAD)
    d_ukv = jnp.concatenate([kn, vv], axis=2).reshape(-1, N_HEADS * 2 * HEAD)
    return d_down, d_uq, d_ukv


def _mla_weight_shapes():
    return (_sds((1024, MLA_C), BF16), _sds((1, 256)), _sds((1, HEAD)), _sds((256, 2048), BF16), _sds((HEAD, 2048), BF16),
            _sds((1, HEAD)), _sds((1, HEAD)), _sds((1, HEAD)), _sds((1, HEAD)), _sds((1024, 1024), BF16),
            _sds((4096, HEAD)), _sds((4096, HEAD)))


def _mla_fwd(x, h, w_down, gq, gkv, w_uq, w_ukv, gqn, gqr, gkn, gkr, w_out, cos, sin):
    c = _mm("mla_down", h, w_down)
    cq, ckv, kr = _mla_prep1_fwd(c, gq, gkv, gkr, cos, sin)
    qa = _mm("mla_uq", cq, w_uq)
    kv = _mm("mla_ukv", ckv, w_ukv)
    qn, qr, kn, v = _mla_prep2_fwd(qa, kv, gqn, gqr, gkn, cos, sin)
    o, lse = _mla_attn_fwd(qn, qr, kn, kr, v)
    y = _mm("mla_out", o, w_out, add=x)
    return y, (h, c, cq, ckv, kr, qa, kv, qn, qr, kn, v, o, lse)


def _mla_bwd(saved, dy, w_down, gq, gkv, w_uq, w_ukv, gqn, gqr, gkn, gkr, w_out, cos, sin):
    h, c, cq, ckv, kr, qa, kv, qn, qr, kn, v, o, lse = saved
    d_wout = _mm("mla_out_wgrad", o, dy, ta=True)
    do = _mm("mla_out_dgrad", dy, w_out, tb=True)
    dqn, dqr, dkn, dkr, dv = _mla_attn_bwd(qn, qr, kn, kr, v, o, lse, do)
    dqa, dkv, dgqn, dgqr, dgkn = _mla_prep2_bwd(qa, kv, gqn, gqr, gkn, cos, sin, dqn, dqr, dkn, dv)
    d_wuq = _mm("mla_uq_wgrad", cq, dqa, ta=True)
    d_wukv = _mm("mla_ukv_wgrad", ckv, dkv, ta=True)
    dcq = _mm("mla_uq_dgrad", dqa, w_uq, tb=True)
    dckv = _mm("mla_ukv_dgrad", dkv, w_ukv, tb=True)
    dc, dgq, dgkv, dgkr = _mla_prep1_bwd(c, gq, gkv, gkr, cos, sin, dcq, dckv, dkr)
    d_wdown = _mm("mla_down_wgrad", h, dc, ta=True)
    dh = _mm("mla_down_dgrad", dc, w_down, tb=True)
    return dh, d_wdown, dgq, dgkv, d_wuq, d_wukv, dgqn, dgqr, dgkn, dgkr, d_wout


def _loss_head(y, target):
    d = y.shape[1]

    def body(y_ref, t_ref, dy_ref, l_ref):
        err = y_ref[...] - t_ref[...]
        dy_ref[...] = err * (1.0 / d)
        part = 0.5 * jnp.sum(jnp.sum(err * err, axis=1, keepdims=True) * (1.0 / d), axis=0, keepdims=True)
        _acc(l_ref, jnp.broadcast_to(part, (1, LANES)))

    return _rows("loss_head", body, [y, target], [_sds(y.shape)], tt=512, accs=[_sds((1, LANES))])


MESH_ID = pl.DeviceIdType.MESH
HBM_SPEC = pl.BlockSpec(memory_space=pltpu.HBM)
PACK_COLS = 1024
PACK_ROW_BLOCK = 256


def _all_gather(name, x):
    m_per, n = x.shape

    def body(x_ref, out_ref, send_sems, recv_sems, local_sem):
        x, y, c = lax.axis_index("x"), lax.axis_index("y"), lax.axis_index("c")
        me, sibling = (x, y, c), (x, y, 1 - c)
        chips = [(1 - x, y), (x, 1 - y), (1 - x, 1 - y)]

        def rows(px, py, pc):
            return out_ref.at[pl.ds((4 * px + 2 * py + pc) * m_per, m_per), :]

        def copy(k, block, to, src=None):
            return pltpu.make_async_remote_copy(
                src_ref=rows(*block) if src is None else src, dst_ref=rows(*block),
                send_sem=send_sems.at[k], recv_sem=recv_sems.at[k], device_id=to, device_id_type=MESH_ID)

        mine = pltpu.make_async_copy(x_ref, rows(*me), local_sem)
        mine.start()
        first = [copy(0, me, sibling, src=x_ref)]
        first += [copy(1 + j, me, (*chip, c), src=x_ref) for j, chip in enumerate(chips)]
        for cp in first:
            cp.start()
        passed = [copy(4 + j, (*chip, c), sibling) for j, chip in enumerate(chips)]
        for j, chip in enumerate(chips):
            copy(1 + j, (*chip, c), me).wait_recv()
            passed[j].start()
        copy(0, sibling, me).wait_recv()
        for j, chip in enumerate(chips):
            copy(4 + j, (*chip, 1 - c), me).wait_recv()
        for cp in first + passed:
            cp.wait_send()
        mine.wait()

    return pl.pallas_call(
        body, name=name,
        out_shape=jax.ShapeDtypeStruct((N_DEV * m_per, n), x.dtype),
        in_specs=[HBM_SPEC], out_specs=HBM_SPEC,
        scratch_shapes=[pltpu.SemaphoreType.DMA((7,)), pltpu.SemaphoreType.DMA((7,)), pltpu.SemaphoreType.DMA],
    )(x)


def _rs_cores(g):
    _, nchip, r, c_ = g.shape

    def body(g_ref, recv_ref, send_sem, recv_sem):
        x, y, c = lax.axis_index("x"), lax.axis_index("y"), lax.axis_index("c")
        cp = pltpu.make_async_remote_copy(src_ref=g_ref.at[1 - c], dst_ref=recv_ref, send_sem=send_sem, recv_sem=recv_sem,
                                          device_id=(x, y, 1 - c), device_id_type=MESH_ID)
        cp.start()
        cp.wait()

    return pl.pallas_call(
        body, name="rs_cores", out_shape=jax.ShapeDtypeStruct((nchip, r, c_), g.dtype),
        in_specs=[HBM_SPEC], out_specs=HBM_SPEC,
        scratch_shapes=[pltpu.SemaphoreType.DMA, pltpu.SemaphoreType.DMA],
    )(g)


def _rs_add(g, recv, core):
    _, nchip, r, c_ = g.shape
    tr = PACK_ROW_BLOCK

    def body(core_ref, g_ref, r_ref, p_ref, pb_ref):
        s = g_ref[0, 0] + r_ref[0]
        p_ref[0] = s
        pb_ref[0] = s.astype(BF16)

    return pl.pallas_call(
        body, name="rs_add",
        grid_spec=pltpu.PrefetchScalarGridSpec(
            num_scalar_prefetch=1, grid=(nchip, r // tr),
            in_specs=[pl.BlockSpec((1, 1, tr, c_), lambda k, i, core_ref: (core_ref[0], k, i, 0)),
                      pl.BlockSpec((1, tr, c_), lambda k, i, core_ref: (k, i, 0))],
            out_specs=[pl.BlockSpec((1, tr, c_), lambda k, i, core_ref: (k, i, 0)),
                       pl.BlockSpec((1, tr, c_), lambda k, i, core_ref: (k, i, 0))]),
        out_shape=[_sds((nchip, r, c_)), _sds((nchip, r, c_), BF16)],
        compiler_params=_cparams(dimension_semantics=("arbitrary", "arbitrary")),
    )(core, g, recv)


def _rs_chips(pb):
    nchip, r, c_ = pb.shape

    def body(p_ref, recv_ref, send_sems, recv_sems, local_sem):
        x, y, c = lax.axis_index("x"), lax.axis_index("y"), lax.axis_index("c")
        my = 2 * x + y
        chips = [(1 - x, y), (x, 1 - y), (1 - x, 1 - y)]
        local = pltpu.make_async_copy(p_ref.at[my], recv_ref.at[my], local_sem)
        local.start()

        def copy(k, src_chip, dst_chip, to):
            return pltpu.make_async_remote_copy(
                src_ref=p_ref.at[src_chip], dst_ref=recv_ref.at[dst_chip], send_sem=send_sems.at[k], recv_sem=recv_sems.at[k],
                device_id=to, device_id_type=MESH_ID)

        sends = [copy(k, 2 * px + py, my, (px, py, c)) for k, (px, py) in enumerate(chips)]
        for cp in sends:
            cp.start()
        for k, (px, py) in enumerate(chips):
            copy(k, my, 2 * px + py, (px, py, c)).wait_recv()
        for cp in sends:
            cp.wait_send()
        local.wait()

    return pl.pallas_call(
        body, name="rs_chips", out_shape=jax.ShapeDtypeStruct((nchip, r, c_), pb.dtype),
        in_specs=[HBM_SPEC], out_specs=HBM_SPEC,
        scratch_shapes=[pltpu.SemaphoreType.DMA((3,)), pltpu.SemaphoreType.DMA((3,)), pltpu.SemaphoreType.DMA],
    )(pb)


def _adam(w, g, m, v):
    m = ADAM_B1 * m + (1.0 - ADAM_B1) * g
    v = ADAM_B2 * v + (1.0 - ADAM_B2) * (g * g)
    m_hat = m / (1.0 - ADAM_B1 ** ADAM_STEP)
    v_hat = v / (1.0 - ADAM_B2 ** ADAM_STEP)
    return -ADAM_LR * (m_hat / (jnp.sqrt(v_hat) + ADAM_EPS) + ADAM_WD * w), m, v


def _sum_adam(p, recv, chip, w, m, v):
    nchip, r, c_ = recv.shape
    tr = PACK_ROW_BLOCK

    def body(chip_ref, p_ref, r_ref, w_ref, m_ref, v_ref, g_ref, d_ref, mo_ref, vo_ref):
        my = chip_ref[0]
        g = jnp.where(my == 0, p_ref[0], r_ref[0].astype(F32))
        for k in range(1, nchip):
            g = g + jnp.where(my == k, p_ref[0], r_ref[k].astype(F32))
        g_ref[...] = g
        d_ref[...], mo_ref[...], vo_ref[...] = _adam(w_ref[...], g, m_ref[...], v_ref[...])

    row = pl.BlockSpec((tr, c_), lambda i, chip_ref: (i, 0))
    return pl.pallas_call(
        body, name="sum_adam",
        grid_spec=pltpu.PrefetchScalarGridSpec(
            num_scalar_prefetch=1, grid=(r // tr,),
            in_specs=[pl.BlockSpec((1, tr, c_), lambda i, chip_ref: (chip_ref[0], i, 0)),
                      pl.BlockSpec((nchip, tr, c_), lambda i, chip_ref: (0, i, 0)), row, row, row],
            out_specs=[row, row, row, row]),
        out_shape=[_sds((r, c_))] * 4,
        compiler_params=_cparams(dimension_semantics=("arbitrary",)),
    )(chip, p, recv, w, m, v)


def _sum_devices(gathered):
    m_all, n = gathered.shape
    m_per = m_all // N_DEV

    def body(x_ref, o_ref):
        s = x_ref[0:m_per, :]
        for j in range(1, N_DEV):
            s = s + x_ref[j * m_per:(j + 1) * m_per, :]
        o_ref[...] = s

    return pl.pallas_call(body, name="sum_devices", out_shape=_sds((m_per, n)), compiler_params=_cparams())(gathered)


def _adam_small(w, g, m, v):
    def body(w_ref, g_ref, m_ref, v_ref, d_ref, mo_ref, vo_ref):
        d_ref[...], mo_ref[...], vo_ref[...] = _adam(w_ref[...], g_ref[...], m_ref[...], v_ref[...])

    return pl.pallas_call(body, name="adam_small", out_shape=[_sds(w.shape)] * 3, compiler_params=_cparams())(w, g, m, v)


N_LAYERS = 4
_MIXER = ("dn", "sb", "mla")
_MIXER_PARAMS = {
    "dn": ("dn_w_in", "dn_conv_w", "dn_a_log", "dn_dt_bias", "dn_out_norm", "dn_w_out"),
    "sb": ("sb_w_qkv", "sb_q_norm", "sb_k_norm", "sb_w_out"),
    "mla": ("mla_w_down", "mla_q_a_norm", "mla_kv_a_norm", "mla_w_uq", "mla_w_ukv", "mla_q_nope_norm", "mla_q_rope_norm",
            "mla_k_nope_norm", "mla_k_rope_norm", "mla_w_out"),
}
_BIG_AXIS = {"dn_w_in": 1, "dn_w_out": 0, "sb_w_qkv": 1, "sb_w_out": 0, "mla_w_down": 0, "mla_w_uq": 1, "mla_w_ukv": 1,
             "mla_w_out": 0, "ffn_w_gate_up": 1, "ffn_w_down": 0}


def _weight_names():
    names = []
    for i in range(N_LAYERS):
        p = "l%d_" % i
        names += [p + "mix_norm"] + [p + n for n in _MIXER_PARAMS[_MIXER[i % 3]]] + [p + "ffn_norm", p + "ffn_w_gate_up", p + "ffn_w_down"]
    return names


WEIGHTS = _weight_names()
BIG = [n for n in WEIGHTS if n[3:] in _BIG_AXIS]
SMALL = [n for n in WEIGHTS if n[3:] not in _BIG_AXIS]
CONV = [n for n in SMALL if n.endswith("conv_w")]


def _ceil_to(n, k):
    return -(-n // k) * k


def _pack(arrs, cols, row_mult):
    parts = []
    for a in arrs:
        f = a.reshape(-1)
        parts.append(jnp.pad(f, (0, _ceil_to(f.shape[0], cols) - f.shape[0])))
    flat = jnp.concatenate(parts)
    rows = _ceil_to(flat.shape[0] // cols, row_mult)
    return jnp.pad(flat, (0, rows * cols - flat.shape[0])).reshape(rows, cols)


def _unpack(buf, shapes):
    cols = buf.shape[-1]
    out, r0 = [], 0
    for s in shapes:
        n = math.prod(s)
        nr = _ceil_to(n, cols) // cols
        out.append(buf[r0:r0 + nr].reshape(-1)[:n].reshape(s))
        r0 += nr
    return out


def _shards_of(full, axis):
    r, c = full.shape
    if axis == 0:
        return full.reshape(N_DEV, -1)
    return full.reshape(r, N_DEV, c // N_DEV).transpose(1, 0, 2).reshape(N_DEV, -1)


def _from_shards(parts, shard_shape, axis):
    rs, cs = shard_shape
    if axis == 0:
        return parts.reshape(N_DEV * rs, cs)
    return parts.transpose(1, 0, 2).reshape(rs, N_DEV * cs)


def kernel(x, l0_mix_norm, l0_dn_w_in, l0_dn_conv_w, l0_dn_a_log, l0_dn_dt_bias, l0_dn_out_norm, l0_dn_w_out, l0_ffn_norm, l0_ffn_w_gate_up, l0_ffn_w_down, l1_mix_norm, l1_sb_w_qkv, l1_sb_q_norm, l1_sb_k_norm, l1_sb_w_out, l1_ffn_norm, l1_ffn_w_gate_up, l1_ffn_w_down, l2_mix_norm, l2_mla_w_down, l2_mla_q_a_norm, l2_mla_kv_a_norm, l2_mla_w_uq, l2_mla_w_ukv, l2_mla_q_nope_norm, l2_mla_q_rope_norm, l2_mla_k_nope_norm, l2_mla_k_rope_norm, l2_mla_w_out, l2_ffn_norm, l2_ffn_w_gate_up, l2_ffn_w_down, l3_mix_norm, l3_dn_w_in, l3_dn_conv_w, l3_dn_a_log, l3_dn_dt_bias, l3_dn_out_norm, l3_dn_w_out, l3_ffn_norm, l3_ffn_w_gate_up, l3_ffn_w_down, loss_target, m_l0_mix_norm, m_l0_dn_w_in, m_l0_dn_conv_w, m_l0_dn_a_log, m_l0_dn_dt_bias, m_l0_dn_out_norm, m_l0_dn_w_out, m_l0_ffn_norm, m_l0_ffn_w_gate_up, m_l0_ffn_w_down, m_l1_mix_norm, m_l1_sb_w_qkv, m_l1_sb_q_norm, m_l1_sb_k_norm, m_l1_sb_w_out, m_l1_ffn_norm, m_l1_ffn_w_gate_up, m_l1_ffn_w_down, m_l2_mix_norm, m_l2_mla_w_down, m_l2_mla_q_a_norm, m_l2_mla_kv_a_norm, m_l2_mla_w_uq, m_l2_mla_w_ukv, m_l2_mla_q_nope_norm, m_l2_mla_q_rope_norm, m_l2_mla_k_nope_norm, m_l2_mla_k_rope_norm, m_l2_mla_w_out, m_l2_ffn_norm, m_l2_ffn_w_gate_up, m_l2_ffn_w_down, m_l3_mix_norm, m_l3_dn_w_in, m_l3_dn_conv_w, m_l3_dn_a_log, m_l3_dn_dt_bias, m_l3_dn_out_norm, m_l3_dn_w_out, m_l3_ffn_norm, m_l3_ffn_w_gate_up, m_l3_ffn_w_down, v_l0_mix_norm, v_l0_dn_w_in, v_l0_dn_conv_w, v_l0_dn_a_log, v_l0_dn_dt_bias, v_l0_dn_out_norm, v_l0_dn_w_out, v_l0_ffn_norm, v_l0_ffn_w_gate_up, v_l0_ffn_w_down, v_l1_mix_norm, v_l1_sb_w_qkv, v_l1_sb_q_norm, v_l1_sb_k_norm, v_l1_sb_w_out, v_l1_ffn_norm, v_l1_ffn_w_gate_up, v_l1_ffn_w_down, v_l2_mix_norm, v_l2_mla_w_down, v_l2_mla_q_a_norm, v_l2_mla_kv_a_norm, v_l2_mla_w_uq, v_l2_mla_w_ukv, v_l2_mla_q_nope_norm, v_l2_mla_q_rope_norm, v_l2_mla_k_nope_norm, v_l2_mla_k_rope_norm, v_l2_mla_w_out, v_l2_ffn_norm, v_l2_ffn_w_gate_up, v_l2_ffn_w_down, v_l3_mix_norm, v_l3_dn_w_in, v_l3_dn_conv_w, v_l3_dn_a_log, v_l3_dn_dt_bias, v_l3_dn_out_norm, v_l3_dn_w_out, v_l3_ffn_norm, v_l3_ffn_w_gate_up, v_l3_ffn_w_down):
    a = dict(locals())
    return _train_step(a)


def _train_step(a):
    mx, my, mc = lax.axis_index("x"), lax.axis_index("y"), lax.axis_index("c")
    dev = 4 * mx + 2 * my + mc
    core = jnp.reshape(mc, (1,)).astype(jnp.int32)
    chip = jnp.reshape(2 * mx + my, (1,)).astype(jnp.int32)
    t, d = a["x"].shape[1], a["x"].shape[2]
    xs = a["x"].reshape(t, d)
    target = a["loss_target"].reshape(t, d)

    big_shapes = [a[n].shape for n in BIG]
    w_pack = _pack([a[n].astype(BF16) for n in BIG], PACK_COLS, PACK_ROW_BLOCK)
    n_rows = w_pack.shape[0]
    gathered = _all_gather("gather_weights", w_pack).reshape(N_DEV, n_rows, PACK_COLS)
    full = {}
    r0 = 0
    for n, s in zip(BIG, big_shapes):
        nr = _ceil_to(s[0] * s[1], PACK_COLS) // PACK_COLS
        parts = gathered[:, r0:r0 + nr].reshape(N_DEV, -1)[:, :s[0] * s[1]].reshape(N_DEV, s[0], s[1])
        full[n] = _from_shards(parts, s, _BIG_AXIS[n[3:]])
        r0 += nr
    conv_pack = _pack([a[n] for n in CONV], LANES, 8)
    conv_all = _all_gather("gather_conv", conv_pack).reshape(N_DEV, conv_pack.shape[0], LANES)
    for n, parts in zip(CONV, zip(*[_unpack(conv_all[j], [a[n].shape for n in CONV]) for j in range(N_DEV)])):
        full[n] = jnp.concatenate(parts, axis=1)

    def vec(n):
        return a[n].reshape(1, -1)

    cos, sin = _rope_tables(t)
    layer_args = []
    for i in range(N_LAYERS):
        p = "l%d_" % i
        kind = _MIXER[i % 3]
        if kind == "dn":
            w_in = jnp.pad(full[p + "dn_w_in"], ((0, 0), (0, DN_PROJ - full[p + "dn_w_in"].shape[1])))
            args = (w_in, full[p + "dn_conv_w"], _pad_lanes(a[p + "dn_a_log"]), _pad_lanes(a[p + "dn_dt_bias"]),
                    vec(p + "dn_out_norm"), full[p + "dn_w_out"])
        elif kind == "sb":
            args = (full[p + "sb_w_qkv"], vec(p + "sb_q_norm"), vec(p + "sb_k_norm"), full[p + "sb_w_out"])
        else:
            w_down, w_uq, w_ukv = _mla_layout(full[p + "mla_w_down"], full[p + "mla_w_uq"], full[p + "mla_w_ukv"])
            args = (w_down, vec(p + "mla_q_a_norm"), vec(p + "mla_kv_a_norm"), w_uq, w_ukv, vec(p + "mla_q_nope_norm"),
                    _pad_lanes(a[p + "mla_q_rope_norm"]), vec(p + "mla_k_nope_norm"), _pad_lanes(a[p + "mla_k_rope_norm"]),
                    full[p + "mla_w_out"], cos, sin)
        layer_args.append((kind, args))

    fwd = {"dn": _dn_fwd, "sb": _sb_fwd, "mla": _mla_fwd}
    bwd = {"dn": _dn_bwd, "sb": _sb_bwd, "mla": _mla_bwd}
    saved = []
    for i, (kind, args) in enumerate(layer_args):
        p = "l%d_" % i
        h = _rmsnorm_fwd("mix_norm", xs, vec(p + "mix_norm"))
        x_mid, sv_mix = fwd[kind](xs, h, *args)
        x_out, sv_ffn = _ffn_fwd(x_mid, vec(p + "ffn_norm"), full[p + "ffn_w_gate_up"], full[p + "ffn_w_down"])
        saved.append((xs, sv_mix, sv_ffn))
        xs = x_out
    dy, loss_part = _loss_head(xs, target)

    grads = {}
    for i in reversed(range(N_LAYERS)):
        p = "l%d_" % i
        kind, args = layer_args[i]
        x_in, sv_mix, sv_ffn = saved[i]
        dx_mid, grads[p + "ffn_norm"], grads[p + "ffn_w_gate_up"], grads[p + "ffn_w_down"] = _ffn_bwd(
            sv_ffn, dy, vec(p + "ffn_norm"), full[p + "ffn_w_gate_up"], full[p + "ffn_w_down"])
        res = bwd[kind](sv_mix, dx_mid, *args)
        dh = res[0]
        if kind == "mla":
            res = list(res)
            res[1], res[4], res[5] = _mla_unlayout(res[1], res[4], res[5])
        for n, g in zip(_MIXER_PARAMS[kind], res[1:]):
            grads[p + n] = g
        dy, grads[p + "mix_norm"] = _rmsnorm_bwd("mix_norm_bwd", x_in, vec(p + "mix_norm"), dh, dx_mid)
    grad_x = dy.reshape(a["x"].shape)

    small_full_shapes = [full[n].shape if n in CONV else a[n].shape for n in SMALL]
    small_grads = []
    for n, s in zip(SMALL, small_full_shapes):
        g = grads[n].reshape(-1)
        small_grads.append(g[:math.prod(s)])
    small_pack = _pack(small_grads + [loss_part.reshape(-1)], LANES, 8)
    small_sum = _sum_devices(_all_gather("gather_small_grads", small_pack))
    small_red = _unpack(small_sum, small_full_shapes + [(LANES,)])
    loss = small_red[-1][0]
    g_small = {}
    for n, g in zip(SMALL, small_red[:-1]):
        if n in CONV:
            cs = a[n].shape[1]
            g = lax.dynamic_slice_in_dim(g, dev * cs, cs, axis=1)
        g_small[n] = g
    small_shapes = [a[n].shape for n in SMALL]
    packs = [_pack([src[n] for n in SMALL], LANES, 8) for src in
             ({n: a[n] for n in SMALL}, g_small, {n: a["m_" + n] for n in SMALL}, {n: a["v_" + n] for n in SMALL})]
    d_small, m_small, v_small = (_unpack(o, small_shapes) for o in _adam_small(*packs))

    g_parts = []
    for n in BIG:
        sh = _shards_of(grads[n][:, :full[n].shape[1]], _BIG_AXIS[n[3:]])
        g_parts.append(jnp.pad(sh, ((0, 0), (0, _ceil_to(sh.shape[1], PACK_COLS) - sh.shape[1]))))
    g_all = jnp.concatenate(g_parts, axis=1)
    g_all = jnp.pad(g_all, ((0, 0), (0, n_rows * PACK_COLS - g_all.shape[1])))
    g_all = g_all.reshape(4, 2, n_rows, PACK_COLS).transpose(1, 0, 2, 3)
    from_sibling = _rs_cores(g_all)
    part, part_bf16 = _rs_add(g_all, from_sibling, core)
    from_chips = _rs_chips(part_bf16)
    packs = [_pack([a[pre + n] for n in BIG], PACK_COLS, PACK_ROW_BLOCK) for pre in ("", "m_", "v_")]
    g_big, d_big, m_big, v_big = (_unpack(o, big_shapes) for o in _sum_adam(part, from_chips, chip, *packs))

    def out(small, big):
        s, b = dict(zip(SMALL, small)), dict(zip(BIG, big))
        return [s[n] if n in s else b[n] for n in WEIGHTS]

    g_small_list = [g_small[n] for n in SMALL]
    return (loss, grad_x, *out(g_small_list, g_big), *out(d_small, d_big), *out(m_small, m_big), *out(v_small, v_big))
```

```python
import math

import jax
import jax.numpy as jnp
from jax import lax
from jax.experimental import pallas as pl
from jax.experimental.pallas import tpu as pltpu

F32 = jnp.float32
BF16 = jnp.bfloat16
HI = lax.Precision.HIGHEST

LANES = 128
N_DEV = 8
N_HEADS = 8
HEAD = 128
NORM_EPS = 1e-6
DN_CHUNK = 64
ATT_BLOCK = 128
ATT_Q = 512
MLA_ROPE = 64
MLA_QK = 192
ROPE_THETA = 10000.0
VMEM_LIMIT = 56 * 1024 * 1024

ADAM_LR = 0.001
ADAM_B1 = 0.9
ADAM_B2 = 0.999
ADAM_EPS = 1e-08
ADAM_WD = 0.01
ADAM_STEP = 10


def _cparams(**kw):
    return pltpu.CompilerParams(vmem_limit_bytes=VMEM_LIMIT, **kw)


def _pick(n, cands):
    for c in cands:
        if c <= n and n % c == 0:
            return c
    return n


def _mm(name, a, b, *, ta=False, tb=False, out_dtype=F32, add=None, tm=None, tn=None, tk=None):
    if ta:
        K, M = a.shape
    else:
        M, K = a.shape
    N = b.shape[0] if tb else b.shape[1]
    tm = tm or _pick(M, (1024, 512, 256, 128))
    tn = tn or _pick(N, (1024, 512, 384, 256, 128))
    tk = tk or _pick(K, (1024, 1408, 512, 384, 256, 128))
    return _mm_raw(
        name, a, b, ta=ta, tb=tb, out_dtype=out_dtype, add=add, grid=(M // tm, N // tn, K // tk), out_shape=(M, N),
        a_block=(tk, tm) if ta else (tm, tk), a_map=(lambda i, j, k: (k, i)) if ta else (lambda i, j, k: (i, k)),
        b_block=(tn, tk) if tb else (tk, tn), b_map=(lambda i, j, k: (j, k)) if tb else (lambda i, j, k: (k, j)),
        o_block=(tm, tn), o_map=lambda i, j, k: (i, j))


def _mm_raw(name, a, b, *, ta, tb, out_dtype, add, grid, out_shape, a_block, a_map, b_block, b_map, o_block, o_map):
    nk = grid[2]
    tm, tn = o_block
    dn = (((0 if ta else 1,), (1 if tb else 0,)), ((), ()))
    has_add = add is not None

    def kern(*refs):
        if has_add:
            a_ref, b_ref, add_ref, o_ref, acc_ref = refs
        else:
            a_ref, b_ref, o_ref, acc_ref = refs
        k = pl.program_id(2)
        part = lax.dot_general(a_ref[...].astype(BF16), b_ref[...].astype(BF16), dn, preferred_element_type=F32)

        @pl.when(k == 0)
        def _():
            acc_ref[...] = part

        @pl.when(k > 0)
        def _():
            acc_ref[...] += part

        @pl.when(k == nk - 1)
        def _():
            r = acc_ref[...]
            if has_add:
                r = r + add_ref[...]
            o_ref[...] = r.astype(out_dtype)

    in_specs = [pl.BlockSpec(a_block, a_map), pl.BlockSpec(b_block, b_map)]
    args = [a, b]
    if has_add:
        in_specs.append(pl.BlockSpec(o_block, o_map))
        args.append(add)
    return pl.pallas_call(
        kern, name=name,
        grid=grid,
        in_specs=in_specs,
        out_specs=pl.BlockSpec(o_block, o_map),
        out_shape=jax.ShapeDtypeStruct(out_shape, out_dtype),
        scratch_shapes=[pltpu.VMEM((tm, tn), F32)],
        compiler_params=_cparams(dimension_semantics=("parallel", "parallel", "arbitrary")),
    )(*args)


def _rows(name, body, ins, outs, *, tt, consts=(), accs=()):
    in_specs, args = [], []
    first = ins[0][0] if isinstance(ins[0], tuple) else ins[0]
    t = first.shape[-2]
    tt = min(tt, t)
    for x in ins:
        if isinstance(x, tuple):
            arr, bs, im = x
            in_specs.append(pl.BlockSpec(bs, im))
            args.append(arr)
        else:
            in_specs.append(_row_spec(x.shape, tt))
            args.append(x)
    for c in consts:
        in_specs.append(pl.BlockSpec(c.shape, lambda i, _n=c.ndim: (0,) * _n))
        args.append(c)
    out_specs = [_row_spec(o.shape, tt) for o in outs]
    out_specs += [pl.BlockSpec(a.shape, lambda i, _n=len(a.shape): (0,) * _n) for a in accs]
    res = pl.pallas_call(
        body, name=name, grid=(t // tt,),
        in_specs=in_specs, out_specs=out_specs, out_shape=list(outs) + list(accs),
        compiler_params=_cparams(dimension_semantics=("arbitrary",)),
    )(*args)
    return res


def _row_spec(shape, tt):
    if len(shape) == 2:
        return pl.BlockSpec((tt, shape[1]), lambda i: (i, 0))
    return pl.BlockSpec((shape[0], tt, shape[2]), lambda i: (0, i, 0))


def _sds(shape, dtype=F32):
    return jax.ShapeDtypeStruct(tuple(shape), dtype)


def _acc(ref, val):
    i = pl.program_id(0)

    @pl.when(i == 0)
    def _():
        ref[...] = val

    @pl.when(i > 0)
    def _():
        ref[...] += val


def _rms(x, g):
    return x * lax.rsqrt(jnp.mean(x * x, axis=-1, keepdims=True) + NORM_EPS) * g


def _silu(x):
    return x / (1.0 + jnp.exp(-x))


def _softplus(x):
    return jnp.maximum(x, 0.0) + jnp.log(1.0 + jnp.exp(-jnp.abs(x)))


def _sigmoid(x):
    return 1.0 / (1.0 + jnp.exp(-x))


def _rmsnorm_fwd(name, x, g, tt=512):
    def body(x_ref, g_ref, h_ref):
        h_ref[...] = _rms(x_ref[...], g_ref[...]).astype(BF16)

    return _rows(name, body, [x], [_sds(x.shape, BF16)], tt=tt, consts=[g])[0]


def _rmsnorm_bwd(name, x, g, dh, dres, tt=512):
    def body(x_ref, dh_ref, dres_ref, g_ref, dx_ref, dg_ref):
        _, vjp = jax.vjp(_rms, x_ref[...], g_ref[...])
        dx, dg = vjp(dh_ref[...])
        dx_ref[...] = dx + dres_ref[...]
        _acc(dg_ref, dg)

    return _rows(name, body, [x, dh, dres], [_sds(x.shape)], tt=tt, consts=[g], accs=[_sds(g.shape)])


def _swiglu_act(g, u):
    def body(g_ref, u_ref, a_ref):
        a_ref[...] = (_silu(g_ref[...]) * u_ref[...]).astype(BF16)

    return _rows("swiglu_act", body, [g, u], [_sds(g.shape, BF16)], tt=512)[0]


def _swiglu_bwd(g, u, dact):
    def body(g_ref, u_ref, da_ref, dg_ref, du_ref):
        g, u, da = g_ref[...], u_ref[...], da_ref[...]
        s = _sigmoid(g)
        dg_ref[...] = (da * u * s * (1.0 + g * (1.0 - s))).astype(BF16)
        du_ref[...] = (da * g * s).astype(BF16)

    return _rows("swiglu_bwd", body, [g, u, dact], [_sds(g.shape, BF16), _sds(g.shape, BF16)], tt=512)


def _ffn_fwd(x, norm_g, w3, w_down):
    t, d = x.shape
    ns, _, cs = w3.shape
    half = ns // 2
    w2 = w3.reshape(ns * d, cs)
    h = _rmsnorm_fwd("ffn_norm", x, norm_g)
    tm = _pick(t, (512, 256, 128))
    nm = t // tm

    def proj(name, off):
        return _mm_raw(name, h, w2, ta=False, tb=False, out_dtype=F32, add=None, grid=(half, nm, 1), out_shape=(half * t, cs),
                       a_block=(tm, d), a_map=lambda j, i, k: (i, 0), b_block=(d, cs), b_map=lambda j, i, k: (j + off, 0),
                       o_block=(tm, cs), o_map=lambda j, i, k: (j * nm + i, 0))

    g, u = proj("ffn_gate", 0), proj("ffn_up", half)
    act = _swiglu_act(g, u)
    y = _mm_raw("ffn_down", act, w_down, ta=False, tb=False, out_dtype=F32, add=x, grid=(nm, 1, half), out_shape=(t, d),
                a_block=(tm, cs), a_map=lambda i, j, k: (k * nm + i, 0), b_block=(cs, d), b_map=lambda i, j, k: (k, 0),
                o_block=(tm, d), o_map=lambda i, j, k: (i, 0))
    return y, (x, h, g, u, act)


def _ffn_bwd(saved, dy, norm_g, w3, w_down):
    x, h, g, u, act = saved
    t, d = x.shape
    ns, _, cs = w3.shape
    half = ns // 2
    w2 = w3.reshape(ns * d, cs)
    tm = _pick(t, (512, 256, 128))
    nm = t // tm
    tk = _pick(t, (1024, 512, 256, 128))
    nk = t // tk
    d_wdown = _mm_raw("ffn_down_wgrad", act, dy, ta=True, tb=False, out_dtype=F32, add=None, grid=(half, 1, nk),
                      out_shape=(half * cs, d), a_block=(tk, cs), a_map=lambda i, j, k: (i * nk + k, 0),
                      b_block=(tk, d), b_map=lambda i, j, k: (k, 0), o_block=(cs, d), o_map=lambda i, j, k: (i, 0))
    dact = _mm_raw("ffn_down_dgrad", dy, w_down, ta=False, tb=True, out_dtype=F32, add=None, grid=(nm, half, 1),
                   out_shape=(half * t, cs), a_block=(tm, d), a_map=lambda i, j, k: (i, 0),
                   b_block=(cs, d), b_map=lambda i, j, k: (j, 0), o_block=(tm, cs), o_map=lambda i, j, k: (j * nm + i, 0))
    dg, du = _swiglu_bwd(g, u, dact)

    def wgrad(name, dd):
        return _mm_raw(name, h, dd, ta=True, tb=False, out_dtype=F32, add=None, grid=(1, half, nk), out_shape=(half * d, cs),
                       a_block=(tk, d), a_map=lambda i, j, k: (k, 0), b_block=(tk, cs), b_map=lambda i, j, k: (j * nk + k, 0),
                       o_block=(d, cs), o_map=lambda i, j, k: (j, 0))

    def dgrad(name, dd, off, add):
        return _mm_raw(name, dd, w2, ta=False, tb=True, out_dtype=F32, add=add, grid=(nm, 1, half), out_shape=(t, d),
                       a_block=(tm, cs), a_map=lambda i, j, k: (k * nm + i, 0), b_block=(d, cs), b_map=lambda i, j, k: (k + off, 0),
                       o_block=(tm, d), o_map=lambda i, j, k: (i, 0))

    d_w3 = jnp.concatenate([wgrad("ffn_gate_wgrad", dg), wgrad("ffn_up_wgrad", du)], axis=0).reshape(ns, d, cs)
    dh = dgrad("ffn_up_dgrad", du, half, dgrad("ffn_gate_dgrad", dg, 0, None))
    dx, dgain = _rmsnorm_bwd("ffn_norm_bwd", x, norm_g, dh, dy)
    return dx, dgain, d_w3, d_wdown


def _dot_nt(a, b):
    return lax.dot_general(a, b, (((1,), (1,)), ((), ())), preferred_element_type=F32)


def _dot_tn(a, b):
    return lax.dot_general(a, b, (((0,), (0,)), ((), ())), preferred_element_type=F32)


def _dot(a, b):
    return jnp.dot(a, b, preferred_element_type=F32)


def _dot_split(x, m):
    hi = x.astype(BF16)
    lo = (x - hi.astype(F32)).astype(BF16)
    return _dot(hi, m) + _dot(lo, m)


def _log_sigmoid(z):
    return jnp.minimum(z, 0.0) - jnp.log(1.0 + jnp.exp(-jnp.abs(z)))


def _heads_in(ref, h, width=HEAD):
    return ref[:, h * width:(h + 1) * width]


def _sb_qk(q, k, gq, gk):
    return _rms(q, gq) * (HEAD ** -0.5), _rms(k, gk)


def _sb_prep_fwd(qkv, gq, gk):
    t = qkv.shape[0]

    def body(x_ref, gq_ref, gk_ref, q_ref, k_ref, v_ref):
        for h in range(N_HEADS):
            q, k = _sb_qk(_heads_in(x_ref, h), _heads_in(x_ref, N_HEADS + h), gq_ref[...], gk_ref[...])
            q_ref[h] = q.astype(BF16)
            k_ref[h] = k.astype(BF16)
            v_ref[h] = _heads_in(x_ref, 2 * N_HEADS + h).astype(BF16)

    hm = _sds((N_HEADS, t, HEAD), BF16)
    return _rows("sb_prep", body, [qkv], [hm, hm, hm], tt=256, consts=[gq, gk])


def _sb_prep_bwd(qkv, gq, gk, dq, dk, dv):
    def body(x_ref, dq_ref, dk_ref, dv_ref, gq_ref, gk_ref, dx_ref, dgq_ref, dgk_ref):
        dgq = jnp.zeros(gq_ref.shape, F32)
        dgk = jnp.zeros(gk_ref.shape, F32)
        for h in range(N_HEADS):
            _, vjp = jax.vjp(_sb_qk, _heads_in(x_ref, h), _heads_in(x_ref, N_HEADS + h), gq_ref[...], gk_ref[...])
            a, b, c, d = vjp((dq_ref[h], dk_ref[h]))
            dx_ref[:, h * HEAD:(h + 1) * HEAD] = a.astype(BF16)
            dx_ref[:, (N_HEADS + h) * HEAD:(N_HEADS + h + 1) * HEAD] = b.astype(BF16)
            dx_ref[:, (2 * N_HEADS + h) * HEAD:(2 * N_HEADS + h + 1) * HEAD] = dv_ref[h].astype(BF16)
            dgq, dgk = dgq + c, dgk + d
        _acc(dgq_ref, dgq)
        _acc(dgk_ref, dgk)

    return _rows("sb_prep_bwd", body, [qkv, dq, dk, dv], [_sds(qkv.shape, BF16)], tt=256, consts=[gq, gk],
                 accs=[_sds(gq.shape), _sds(gk.shape)])


def _tri_masks():
    rows = lax.broadcasted_iota(jnp.int32, (ATT_BLOCK, ATT_BLOCK), 0)
    cols = lax.broadcasted_iota(jnp.int32, (ATT_BLOCK, ATT_BLOCK), 1)
    return rows, cols


def _q_block(t):
    return min(ATT_Q, t)


def _key_order(bq, qb):
    rows = lax.broadcasted_iota(jnp.int32, (bq, ATT_BLOCK), 0)
    cols = lax.broadcasted_iota(jnp.int32, (bq, ATT_BLOCK), 1)
    return rows - cols + qb * bq


def _sb_attn_fwd(q, k, v):
    nh, t, _ = q.shape
    bq = _q_block(t)
    per = bq // ATT_BLOCK

    def kern(q_ref, k_ref, v_ref, o_ref):
        qb = pl.program_id(1)
        qv = q_ref[0]
        rows, cols = _tri_masks()
        after = (rows > cols).astype(BF16)
        order = _key_order(bq, qb)
        nkb = (qb + 1) * per

        def body(i, carry):
            o_acc, run = carry
            kb = nkb - 1 - i
            off = pl.multiple_of(kb * ATT_BLOCK, ATT_BLOCK)
            kv = k_ref[0, pl.ds(off, ATT_BLOCK), :]
            vv = v_ref[0, pl.ds(off, ATT_BLOCK), :]
            z = _dot_nt(qv, kv)
            past = order > kb * ATT_BLOCK
            lsz = _log_sigmoid(z)
            lsn = jnp.where(past, lsz - z, 0.0)
            la = _dot_split(lsn, after) + run
            a = jnp.where(past, jnp.exp(lsz + la), 0.0)
            o_acc = o_acc + _dot(a.astype(BF16), vv)
            run = run + jnp.sum(lsn, axis=1, keepdims=True)
            return o_acc, run

        o, _ = lax.fori_loop(0, nkb, body, (jnp.zeros((bq, HEAD), F32), jnp.zeros((bq, 1), F32)))
        o_ref[...] = o

    return pl.pallas_call(
        kern, name="sb_attn_fwd", grid=(nh, t // bq),
        in_specs=[pl.BlockSpec((1, bq, HEAD), lambda h, i: (h, i, 0)),
                  pl.BlockSpec((1, t, HEAD), lambda h, i: (h, 0, 0)),
                  pl.BlockSpec((1, t, HEAD), lambda h, i: (h, 0, 0))],
        out_specs=pl.BlockSpec((bq, HEAD), lambda h, i: (i, h)),
        out_shape=_sds((t, nh * HEAD)),
        compiler_params=_cparams(dimension_semantics=("parallel", "arbitrary")),
    )(q, k, v)


def _sb_attn_bwd(q, k, v, do):
    nh, t, _ = q.shape
    bq = _q_block(t)
    per = bq // ATT_BLOCK

    def kern(q_ref, k_ref, v_ref, do_ref, dq_ref, dk_ref, dv_ref, g_s, ls_s):
        qb = pl.program_id(1)

        @pl.when(qb == 0)
        def _():
            dk_ref[...] = jnp.zeros(dk_ref.shape, F32)
            dv_ref[...] = jnp.zeros(dv_ref.shape, F32)

        qv = q_ref[0]
        dob = do_ref[...].astype(BF16)
        rows, cols = _tri_masks()
        after = (rows > cols).astype(BF16)
        before = (rows < cols).astype(BF16)
        order = _key_order(bq, qb)
        nkb = (qb + 1) * per

        def sweep_left(i, run):
            kb = nkb - 1 - i
            off = pl.multiple_of(kb * ATT_BLOCK, ATT_BLOCK)
            kv = k_ref[0, pl.ds(off, ATT_BLOCK), :]
            vv = v_ref[0, pl.ds(off, ATT_BLOCK), :]
            z = _dot_nt(qv, kv)
            past = order > kb * ATT_BLOCK
            lsz = _log_sigmoid(z)
            lsn = jnp.where(past, lsz - z, 0.0)
            la = _dot_split(lsn, after) + run
            a = jnp.where(past, jnp.exp(lsz + la), 0.0)
            g_s[kb] = _dot_nt(dob, vv) * a
            ls_s[kb] = lsz
            dv_ref[0, pl.ds(off, ATT_BLOCK), :] += _dot_tn(a.astype(BF16), dob)
            return run + jnp.sum(lsn, axis=1, keepdims=True)

        zero = jnp.zeros((bq, 1), F32)
        lax.fori_loop(0, nkb, sweep_left, zero)

        def sweep_right(kb, carry):
            dq_acc, run_g = carry
            off = pl.multiple_of(kb * ATT_BLOCK, ATT_BLOCK)
            kv = k_ref[0, pl.ds(off, ATT_BLOCK), :]
            g = g_s[kb]
            sg = jnp.exp(ls_s[kb])
            past = order > kb * ATT_BLOCK
            dls = run_g + _dot_split(g, before)
            dzb = jnp.where(past, g * (1.0 - sg) - dls * sg, 0.0).astype(BF16)
            dk_ref[0, pl.ds(off, ATT_BLOCK), :] += _dot_tn(dzb, qv)
            return dq_acc + _dot(dzb, kv), run_g + jnp.sum(g, axis=1, keepdims=True)

        dq, _ = lax.fori_loop(0, nkb, sweep_right, (jnp.zeros((bq, HEAD), F32), zero))
        dq_ref[0] = dq

    hm = _sds((nh, t, HEAD))
    full = pl.BlockSpec((1, t, HEAD), lambda h, i: (h, 0, 0))
    tok = pl.BlockSpec((bq, HEAD), lambda h, i: (i, h))
    nkb_max = t // ATT_BLOCK
    return pl.pallas_call(
        kern, name="sb_attn_bwd", grid=(nh, t // bq),
        in_specs=[pl.BlockSpec((1, bq, HEAD), lambda h, i: (h, i, 0)), full, full, tok],
        out_specs=[pl.BlockSpec((1, bq, HEAD), lambda h, i: (h, i, 0)), full, full],
        out_shape=[hm, hm, hm],
        scratch_shapes=[pltpu.VMEM((nkb_max, bq, ATT_BLOCK), F32), pltpu.VMEM((nkb_max, bq, ATT_BLOCK), F32)],
        compiler_params=_cparams(dimension_semantics=("parallel", "arbitrary")),
    )(q, k, v, do)


def _sb_fwd(x, h, w_qkv, gq, gk, w_out):
    qkv = _mm("sb_qkv", h, w_qkv)
    q, k, v = _sb_prep_fwd(qkv, gq, gk)
    o = _sb_attn_fwd(q, k, v)
    y = _mm("sb_out", o, w_out, add=x)
    return y, (h, qkv, q, k, v, o)


def _sb_bwd(saved, dy, w_qkv, gq, gk, w_out):
    h, qkv, q, k, v, o = saved
    d_wout = _mm("sb_out_wgrad", o, dy, ta=True)
    do = _mm("sb_out_dgrad", dy, w_out, tb=True)
    dq, dk, dv = _sb_attn_bwd(q, k, v, do)
    dqkv, dgq, dgk = _sb_prep_bwd(qkv, gq, gk, dq, dk, dv)
    d_wqkv = _mm("sb_qkv_wgrad", h, dqkv, ta=True)
    dh = _mm("sb_qkv_dgrad", dqkv, w_qkv, tb=True)
    return dh, d_wqkv, dgq, dgk, d_wout


DN_QKV = 3 * N_HEADS * HEAD
DN_PROJ = DN_QKV + N_HEADS * HEAD + LANES
DN_CONV = 4
HALO = 8
CONV_COLS = 512


def _dn_conv_fwd(proj, conv_w, tt=256):
    t = proj.shape[0]
    tt = min(tt, t)

    def body(u_ref, prev_ref, w_ref, c_ref):
        i = pl.program_id(0)
        for cc in range(DN_QKV // CONV_COLS):
            cs = slice(cc * CONV_COLS, (cc + 1) * CONV_COLS)
            cur = u_ref[:, cs]
            prev = jnp.where(i > 0, prev_ref[:, cs], 0.0)
            ext = jnp.concatenate([prev, cur], axis=0)
            y = cur * w_ref[DN_CONV - 1:DN_CONV, cs]
            for j in range(DN_CONV - 1):
                y = y + pltpu.roll(ext, DN_CONV - 1 - j, 0)[HALO:] * w_ref[j:j + 1, cs]
            c_ref[:, cs] = y

    return _rows("dn_conv", body,
                 [(proj, (tt, DN_QKV), lambda i: (i, 0)),
                  (proj, (HALO, DN_QKV), lambda i: (jnp.maximum(i * (tt // HALO) - 1, 0), 0))],
                 [_sds((t, DN_QKV))], tt=tt, consts=[conv_w])[0]


def _dn_conv_bwd(proj, conv_w, dc, dz, dab, tt=256):
    t = proj.shape[0]
    tt = min(tt, t)
    nblk = t // tt

    def body(u_ref, prev_ref, dc_ref, next_ref, dz_ref, dab_ref, w_ref, dp_ref, dw_ref):
        i = pl.program_id(0)
        dws = []
        for cc in range(DN_QKV // CONV_COLS):
            cs = slice(cc * CONV_COLS, (cc + 1) * CONV_COLS)
            cur = u_ref[:, cs]
            prev = jnp.where(i > 0, prev_ref[:, cs], 0.0)
            ext_u = jnp.concatenate([prev, cur], axis=0)
            d = dc_ref[:, cs]
            nxt = jnp.where(i < nblk - 1, next_ref[:, cs], 0.0)
            ext_d = jnp.concatenate([d, nxt], axis=0)
            du = d * w_ref[DN_CONV - 1:DN_CONV, cs]
            rows = [jnp.sum(d * cur, axis=0, keepdims=True)]
            for j in range(DN_CONV - 2, -1, -1):
                sh = DN_CONV - 1 - j
                du = du + pltpu.roll(ext_d, tt + HALO - sh, 0)[:tt] * w_ref[j:j + 1, cs]
                rows.insert(0, jnp.sum(d * pltpu.roll(ext_u, sh, 0)[HALO:], axis=0, keepdims=True))
            dp_ref[:, cs] = du.astype(BF16)
            dws.append(jnp.concatenate(rows, axis=0))
        dp_ref[:, DN_QKV:DN_QKV + N_HEADS * HEAD] = dz_ref[...].astype(BF16)
        dp_ref[:, DN_QKV + N_HEADS * HEAD:] = dab_ref[...].astype(BF16)
        _acc(dw_ref, jnp.concatenate(dws, axis=1))

    return _rows("dn_conv_bwd", body,
                 [(proj, (tt, DN_QKV), lambda i: (i, 0)),
                  (proj, (HALO, DN_QKV), lambda i: (jnp.maximum(i * (tt // HALO) - 1, 0), 0)),
                  dc,
                  (dc, (HALO, DN_QKV), lambda i: (jnp.minimum((i + 1) * (tt // HALO), t // HALO - 1), 0)),
                  dz, dab],
                 [_sds((t, DN_PROJ), BF16)], tt=tt, consts=[conv_w], accs=[_sds(conv_w.shape)])


def _l2n(x):
    return x * lax.rsqrt(jnp.sum(x * x, axis=-1, keepdims=True) + NORM_EPS)


def _dn_qkv(cq, ck, cv):
    return _l2n(_silu(cq)) * (HEAD ** -0.5), _l2n(_silu(ck)), _silu(cv)


def _dn_gates(ab, a_log, dt_bias):
    lane = lax.broadcasted_iota(jnp.int32, ab.shape, 1)
    g = -jnp.exp(a_log) * _softplus(ab + dt_bias)
    return jnp.where(lane < N_HEADS, g, jnp.where(lane < 2 * N_HEADS, _sigmoid(ab), 0.0))


def _ab_spec(tt):
    return (tt, LANES), lambda i: (i, DN_PROJ // LANES - 1)


def _dn_prep_fwd(c, proj, a_log, dt_bias, tt=256):
    t = c.shape[0]
    tt = min(tt, t)

    def body(c_ref, ab_ref, al_ref, dt_ref, q_ref, k_ref, v_ref, g_ref):
        for h in range(N_HEADS):
            q_ref[h], k_ref[h], v_ref[h] = _dn_qkv(_heads_in(c_ref, h), _heads_in(c_ref, N_HEADS + h), _heads_in(c_ref, 2 * N_HEADS + h))
        g_ref[...] = _dn_gates(ab_ref[...], al_ref[...], dt_ref[...])

    hm = _sds((N_HEADS, t, HEAD))
    return _rows("dn_prep", body, [c, (proj,) + _ab_spec(tt)], [hm, hm, hm, _sds((t, LANES))], tt=tt, consts=[a_log, dt_bias])


def _dn_prep_bwd(c, proj, a_log, dt_bias, dq, dk, dv, dgates, tt=256):
    t = c.shape[0]
    tt = min(tt, t)

    def body(c_ref, ab_ref, dq_ref, dk_ref, dv_ref, dg_ref, al_ref, dt_ref, dc_ref, dab_ref, dal_ref, ddt_ref):
        for h in range(N_HEADS):
            _, vjp = jax.vjp(_dn_qkv, _heads_in(c_ref, h), _heads_in(c_ref, N_HEADS + h), _heads_in(c_ref, 2 * N_HEADS + h))
            a, b, d = vjp((dq_ref[h], dk_ref[h], dv_ref[h]))
            dc_ref[:, h * HEAD:(h + 1) * HEAD] = a
            dc_ref[:, (N_HEADS + h) * HEAD:(N_HEADS + h + 1) * HEAD] = b
            dc_ref[:, (2 * N_HEADS + h) * HEAD:(2 * N_HEADS + h + 1) * HEAD] = d
        _, vjp = jax.vjp(_dn_gates, ab_ref[...], al_ref[...], dt_ref[...])
        dab, dal, ddt = vjp(dg_ref[...])
        dab_ref[...] = dab
        _acc(dal_ref, dal)
        _acc(ddt_ref, ddt)

    return _rows("dn_prep_bwd", body, [c, (proj,) + _ab_spec(tt), dq, dk, dv, dgates], [_sds(c.shape), _sds((t, LANES))],
                 tt=tt, consts=[a_log, dt_bias], accs=[_sds(a_log.shape), _sds(dt_bias.shape)])


def _bdot(a, b, prec=None):
    return lax.dot_general(a, b, (((2,), (1,)), ((0,), (0,))), precision=prec, preferred_element_type=F32)


def _bdot_nt(a, b, prec=None):
    return lax.dot_general(a, b, (((2,), (2,)), ((0,), (0,))), precision=prec, preferred_element_type=F32)


def _bdot_tn(a, b, prec=None):
    return lax.dot_general(a, b, (((1,), (1,)), ((0,), (0,))), precision=prec, preferred_element_type=F32)


def _inv_raw(low):
    c = low.shape[-1]
    r = lax.broadcasted_iota(jnp.int32, (c, c), 0)
    s = lax.broadcasted_iota(jnp.int32, (c, c), 1)
    m = jnp.where(r == s, 1.0, 0.0) - low
    p = _bdot(low, low, HI)
    n_fac = int(math.log2(c)) - 1
    for i in range(n_fac):
        m = m + _bdot(m, p, HI)
        if i < n_fac - 1:
            p = _bdot(p, p, HI)
    return m


@jax.custom_vjp
def _inv_unit_lower(low):
    return _inv_raw(low)


def _inv_fwd(low):
    m = _inv_raw(low)
    return m, m


def _inv_bwd(m, dm):
    return (-_bdot_nt(_bdot_tn(m, dm, HI), m, HI),)


_inv_unit_lower.defvjp(_inv_fwd, _inv_bwd)


def _dn_chunk(q, k, v, gates, s):
    nh, c, _ = q.shape
    lane = lax.broadcasted_iota(jnp.int32, gates.shape, 1)
    g_col = jnp.stack([jnp.sum(jnp.where(lane == h, gates, 0.0), axis=1, keepdims=True) for h in range(nh)])
    b_col = jnp.stack([jnp.sum(jnp.where(lane == h + nh, gates, 0.0), axis=1, keepdims=True) for h in range(nh)])
    r = lax.broadcasted_iota(jnp.int32, (c, c), 0)
    cc = lax.broadcasted_iota(jnp.int32, (c, c), 1)
    causal, strict = r >= cc, r > cc
    incl = jnp.broadcast_to(jnp.where(causal, 1.0, 0.0), (nh, c, c))
    upper = jnp.broadcast_to(jnp.where(r <= cc, 1.0, 0.0), (nh, c, c))
    gb = jnp.broadcast_to(g_col, (nh, c, LANES))
    gbc = jnp.broadcast_to(g_col, (nh, c, c))
    gc = _bdot(incl, gb, HI)
    gc_r = _bdot(incl, gbc, HI)
    gc_c = _bdot_tn(gbc, upper, HI)
    decay = jnp.where(causal, jnp.exp(jnp.where(causal, gc_r - gc_c, 0.0)), 0.0)
    kb = k * b_col
    low = jnp.where(strict, _bdot_nt(kb, k) * decay, 0.0)
    m = _inv_unit_lower(low)
    egc = jnp.exp(gc)
    u = _bdot(m, v * b_col, HI)
    w = _bdot(m, kb * egc, HI)
    attn = _bdot_nt(q, k) * decay
    gl = jnp.sum(gb, axis=1, keepdims=True)
    v_new = u - _bdot(w, s)
    o = _bdot(q * egc, s) + _bdot(attn, v_new)
    s_new = s * jnp.exp(gl) + _bdot_tn(k * jnp.exp(gl - gc), v_new)
    return o, s_new


def _dn_chunks_fwd(q, k, v, gates):
    nh, t, _ = q.shape
    n = t // DN_CHUNK

    def kern(q_ref, k_ref, v_ref, g_ref, o_ref, sin_ref, s_scr):
        @pl.when(pl.program_id(0) == 0)
        def _():
            s_scr[...] = jnp.zeros(s_scr.shape, F32)

        s = s_scr[...]
        sin_ref[0] = s
        o_ref[...], s_scr[...] = _dn_chunk(q_ref[...], k_ref[...], v_ref[...], g_ref[...], s)

    blk = pl.BlockSpec((nh, DN_CHUNK, HEAD), lambda i: (0, i, 0))
    return pl.pallas_call(
        kern, name="dn_chunks_fwd", grid=(n,),
        in_specs=[blk, blk, blk, pl.BlockSpec((DN_CHUNK, LANES), lambda i: (i, 0))],
        out_specs=[blk, pl.BlockSpec((1, nh, HEAD, HEAD), lambda i: (i, 0, 0, 0))],
        out_shape=[_sds((nh, t, HEAD)), _sds((n, nh, HEAD, HEAD))],
        scratch_shapes=[pltpu.VMEM((nh, HEAD, HEAD), F32)],
        compiler_params=_cparams(dimension_semantics=("arbitrary",)),
    )(q, k, v, gates)


def _dn_chunks_bwd(q, k, v, gates, s_in, do):
    nh, t, _ = q.shape
    n = t // DN_CHUNK

    def kern(q_ref, k_ref, v_ref, g_ref, sin_ref, do_ref, dq_ref, dk_ref, dv_ref, dg_ref, ds_scr):
        @pl.when(pl.program_id(0) == 0)
        def _():
            ds_scr[...] = jnp.zeros(ds_scr.shape, F32)

        _, vjp = jax.vjp(_dn_chunk, q_ref[...], k_ref[...], v_ref[...], g_ref[...], sin_ref[0])
        dq_ref[...], dk_ref[...], dv_ref[...], dg_ref[...], ds_scr[...] = vjp((do_ref[...], ds_scr[...]))

    blk = pl.BlockSpec((nh, DN_CHUNK, HEAD), lambda i: (0, n - 1 - i, 0))
    gblk = pl.BlockSpec((DN_CHUNK, LANES), lambda i: (n - 1 - i, 0))
    hm = _sds((nh, t, HEAD))
    return pl.pallas_call(
        kern, name="dn_chunks_bwd", grid=(n,),
        in_specs=[blk, blk, blk, gblk, pl.BlockSpec((1, nh, HEAD, HEAD), lambda i: (n - 1 - i, 0, 0, 0)), blk],
        out_specs=[blk, blk, blk, gblk],
        out_shape=[hm, hm, hm, _sds((t, LANES))],
        scratch_shapes=[pltpu.VMEM((nh, HEAD, HEAD), F32)],
        compiler_params=_cparams(dimension_semantics=("arbitrary",)),
    )(q, k, v, gates, s_in, do)


def _dn_gate_out(o, z, g):
    return _rms(o, g) * _silu(z)


def _z_spec(tt):
    return (tt, N_HEADS * HEAD), lambda i: (i, DN_QKV // (N_HEADS * HEAD))


def _dn_post_fwd(o, proj, out_norm, tt=256):
    t = o.shape[1]
    tt = min(tt, t)

    def body(o_ref, z_ref, g_ref, y_ref):
        for h in range(N_HEADS):
            y_ref[:, h * HEAD:(h + 1) * HEAD] = _dn_gate_out(o_ref[h], _heads_in(z_ref, h), g_ref[...]).astype(BF16)

    return _rows("dn_post", body, [o, (proj,) + _z_spec(tt)], [_sds((t, N_HEADS * HEAD), BF16)], tt=tt, consts=[out_norm])[0]


def _dn_post_bwd(o, proj, out_norm, dy, tt=256):
    t = o.shape[1]
    tt = min(tt, t)

    def body(o_ref, z_ref, dy_ref, g_ref, do_ref, dz_ref, dg_ref):
        dg = jnp.zeros(g_ref.shape, F32)
        for h in range(N_HEADS):
            _, vjp = jax.vjp(_dn_gate_out, o_ref[h], _heads_in(z_ref, h), g_ref[...])
            a, b, d = vjp(_heads_in(dy_ref, h))
            do_ref[h] = a
            dz_ref[:, h * HEAD:(h + 1) * HEAD] = b
            dg = dg + d
        _acc(dg_ref, dg)

    return _rows("dn_post_bwd", body, [o, (proj,) + _z_spec(tt), dy], [_sds(o.shape), _sds((t, N_HEADS * HEAD))], tt=tt,
                 consts=[out_norm], accs=[_sds(out_norm.shape)])


def _dn_fwd(x, h, w_in, conv_w, a_log, dt_bias, out_norm, w_out):
    proj = _mm("dn_in", h, w_in)
    c = _dn_conv_fwd(proj, conv_w)
    q, k, v, gates = _dn_prep_fwd(c, proj, a_log, dt_bias)
    o, s_in = _dn_chunks_fwd(q, k, v, gates)
    on = _dn_post_fwd(o, proj, out_norm)
    y = _mm("dn_out", on, w_out, add=x)
    return y, (h, proj, c, q, k, v, gates, o, s_in, on)


def _dn_bwd(saved, dy, w_in, conv_w, a_log, dt_bias, out_norm, w_out):
    h, proj, c, q, k, v, gates, o, s_in, on = saved
    d_wout = _mm("dn_out_wgrad", on, dy, ta=True)
    don = _mm("dn_out_dgrad", dy, w_out, tb=True)
    do, dz, d_out_norm = _dn_post_bwd(o, proj, out_norm, don)
    dq, dk, dv, dgates = _dn_chunks_bwd(q, k, v, gates, s_in, do)
    dc, dab, d_a_log, d_dt_bias = _dn_prep_bwd(c, proj, a_log, dt_bias, dq, dk, dv, dgates)
    dproj, d_conv_w = _dn_conv_bwd(proj, conv_w, dc, dz, dab)
    d_win = _mm("dn_in_wgrad", h, dproj, ta=True)
    dh = _mm("dn_in_dgrad", dproj, w_in, tb=True)
    return dh, d_win, d_conv_w, d_a_log, d_dt_bias, d_out_norm, d_wout


MLA_SCALE = MLA_QK ** -0.5
MLA_C = 512


def _swap_raw(x):
    lane = lax.broadcasted_iota(jnp.int32, x.shape, 1)
    half = MLA_ROPE // 2
    y = jnp.where(lane < half, pltpu.roll(x, LANES - half, 1), pltpu.roll(x, half, 1))
    return jnp.where(lane < MLA_ROPE, y, 0.0)


@jax.custom_vjp
def _swap_halves(x):
    return _swap_raw(x)


_swap_halves.defvjp(lambda x: (_swap_raw(x), None), lambda _, d: (_swap_raw(d),))


def _rms_rope(x, g, cos, sin):
    y = x * lax.rsqrt(jnp.sum(x * x, axis=-1, keepdims=True) * (1.0 / MLA_ROPE) + NORM_EPS) * g
    return y * cos + _swap_halves(y) * sin


def _mla_latent(cq, ckv, kr, gq, gkv, gkr, cos, sin):
    return _rms(cq, gq), _rms(ckv, gkv), _rms_rope(kr, gkr, cos, sin)


def _mla_prep1_fwd(c, gq, gkv, gkr, cos, sin):
    t = c.shape[0]

    def body(c_ref, cos_ref, sin_ref, gq_ref, gkv_ref, gkr_ref, cq_ref, ckv_ref, kr_ref):
        a, b, r = _mla_latent(c_ref[:, :256], c_ref[:, 256:384], c_ref[:, 384:], gq_ref[...], gkv_ref[...], gkr_ref[...],
                              cos_ref[...], sin_ref[...])
        cq_ref[...] = a.astype(BF16)
        ckv_ref[...] = b.astype(BF16)
        kr_ref[...] = r.astype(BF16)

    return _rows("mla_prep1", body, [c, cos, sin], [_sds((t, 256), BF16), _sds((t, HEAD), BF16), _sds((t, HEAD), BF16)],
                 tt=512, consts=[gq, gkv, gkr])


def _mla_prep1_bwd(c, gq, gkv, gkr, cos, sin, dcq, dckv, dkr_heads):
    def body(c_ref, cos_ref, sin_ref, dcq_ref, dckv_ref, dkr_ref, gq_ref, gkv_ref, gkr_ref, dc_ref, dgq_ref, dgkv_ref, dgkr_ref):
        dkr = dkr_ref[0]
        for h in range(1, N_HEADS):
            dkr = dkr + dkr_ref[h]
        _, vjp = jax.vjp(_mla_latent, c_ref[:, :256], c_ref[:, 256:384], c_ref[:, 384:], gq_ref[...], gkv_ref[...], gkr_ref[...],
                         cos_ref[...], sin_ref[...])
        a, b, r, d1, d2, d3, _, _ = vjp((dcq_ref[...], dckv_ref[...], dkr))
        dc_ref[:, :256] = a.astype(BF16)
        dc_ref[:, 256:384] = b.astype(BF16)
        dc_ref[:, 384:] = r.astype(BF16)
        _acc(dgq_ref, d1)
        _acc(dgkv_ref, d2)
        _acc(dgkr_ref, d3)

    return _rows("mla_prep1_bwd", body, [c, cos, sin, dcq, dckv, dkr_heads], [_sds(c.shape, BF16)], tt=512,
                 consts=[gq, gkv, gkr], accs=[_sds(gq.shape), _sds(gkv.shape), _sds(gkr.shape)])


def _mla_heads(qn, qr, kn, gqn, gqr, gkn, cos, sin):
    return _rms(qn, gqn) * MLA_SCALE, _rms_rope(qr, gqr, cos, sin) * MLA_SCALE, _rms(kn, gkn)


def _mla_prep2_fwd(qa, kv, gqn, gqr, gkn, cos, sin):
    t = qa.shape[0]

    def body(qa_ref, kv_ref, cos_ref, sin_ref, gqn_ref, gqr_ref, gkn_ref, qn_ref, qr_ref, kn_ref, v_ref):
        for h in range(N_HEADS):
            a, b, c = _mla_heads(_heads_in(qa_ref, h), _heads_in(qa_ref, N_HEADS + h), _heads_in(kv_ref, h),
                                 gqn_ref[...], gqr_ref[...], gkn_ref[...], cos_ref[...], sin_ref[...])
            qn_ref[h] = a.astype(BF16)
            qr_ref[h] = b.astype(BF16)
            kn_ref[h] = c.astype(BF16)
            v_ref[h] = _heads_in(kv_ref, N_HEADS + h).astype(BF16)

    hm = _sds((N_HEADS, t, HEAD), BF16)
    return _rows("mla_prep2", body, [qa, kv, cos, sin], [hm, hm, hm, hm], tt=256, consts=[gqn, gqr, gkn])


def _mla_prep2_bwd(qa, kv, gqn, gqr, gkn, cos, sin, dqn, dqr, dkn, dv):
    def body(qa_ref, kv_ref, cos_ref, sin_ref, dqn_ref, dqr_ref, dkn_ref, dv_ref, gqn_ref, gqr_ref, gkn_ref,
             dqa_ref, dkv_ref, d1_ref, d2_ref, d3_ref):
        d1 = jnp.zeros(gqn_ref.shape, F32)
        d2 = jnp.zeros(gqr_ref.shape, F32)
        d3 = jnp.zeros(gkn_ref.shape, F32)
        for h in range(N_HEADS):
            _, vjp = jax.vjp(_mla_heads, _heads_in(qa_ref, h), _heads_in(qa_ref, N_HEADS + h), _heads_in(kv_ref, h),
                             gqn_ref[...], gqr_ref[...], gkn_ref[...], cos_ref[...], sin_ref[...])
            a, b, c, e1, e2, e3, _, _ = vjp((dqn_ref[h], dqr_ref[h], dkn_ref[h]))
            dqa_ref[:, h * HEAD:(h + 1) * HEAD] = a.astype(BF16)
            dqa_ref[:, (N_HEADS + h) * HEAD:(N_HEADS + h + 1) * HEAD] = b.astype(BF16)
            dkv_ref[:, h * HEAD:(h + 1) * HEAD] = c.astype(BF16)
            dkv_ref[:, (N_HEADS + h) * HEAD:(N_HEADS + h + 1) * HEAD] = dv_ref[h].astype(BF16)
            d1, d2, d3 = d1 + e1, d2 + e2, d3 + e3
        _acc(d1_ref, d1)
        _acc(d2_ref, d2)
        _acc(d3_ref, d3)

    return _rows("mla_prep2_bwd", body, [qa, kv, cos, sin, dqn, dqr, dkn, dv], [_sds(qa.shape, BF16), _sds(kv.shape, BF16)],
                 tt=256, consts=[gqn, gqr, gkn], accs=[_sds(gqn.shape), _sds(gqr.shape), _sds(gkn.shape)])


def _mla_attn_fwd(qn, qr, kn, kr, v):
    nh, t, _ = qn.shape
    bq = _q_block(t)
    per = bq // ATT_BLOCK

    def kern(qn_ref, qr_ref, kn_ref, kr_ref, v_ref, o_ref, lse_ref):
        qb = pl.program_id(1)
        qv = jnp.concatenate([qn_ref[0], qr_ref[0]], axis=1)
        order = _key_order(bq, qb)

        def body(kb, carry):
            acc, m, l = carry
            off = pl.multiple_of(kb * ATT_BLOCK, ATT_BLOCK)
            kv = jnp.concatenate([kn_ref[0, pl.ds(off, ATT_BLOCK), :], kr_ref[pl.ds(off, ATT_BLOCK), :]], axis=1)
            s = jnp.where(order >= kb * ATT_BLOCK, _dot_nt(qv, kv), -jnp.inf)
            m_new = jnp.maximum(m, jnp.max(s, axis=1, keepdims=True))
            alpha = jnp.exp(m - m_new)
            p = jnp.exp(s - m_new)
            acc = acc * alpha + _dot(p.astype(BF16), v_ref[0, pl.ds(off, ATT_BLOCK), :])
            return acc, m_new, l * alpha + jnp.sum(p, axis=1, keepdims=True)

        init = (jnp.zeros((bq, HEAD), F32), jnp.full((bq, 1), -jnp.inf, F32), jnp.zeros((bq, 1), F32))
        acc, m, l = lax.fori_loop(0, (qb + 1) * per, body, init)
        o_ref[...] = acc / l
        lse_ref[...] = jnp.broadcast_to(m + jnp.log(l), (bq, HEAD))

    blk = pl.BlockSpec((1, bq, HEAD), lambda h, i: (h, i, 0))
    full = pl.BlockSpec((1, t, HEAD), lambda h, i: (h, 0, 0))
    tok = pl.BlockSpec((bq, HEAD), lambda h, i: (i, h))
    return pl.pallas_call(
        kern, name="mla_attn_fwd", grid=(nh, t // bq),
        in_specs=[blk, blk, full, pl.BlockSpec((t, HEAD), lambda h, i: (0, 0)), full],
        out_specs=[tok, tok], out_shape=[_sds((t, nh * HEAD)), _sds((t, nh * HEAD))],
        compiler_params=_cparams(dimension_semantics=("parallel", "arbitrary")),
    )(qn, qr, kn, kr, v)


def _mla_attn_bwd(qn, qr, kn, kr, v, o, lse, do):
    nh, t, _ = qn.shape
    bq = _q_block(t)
    per = bq // ATT_BLOCK

    def kern(qn_ref, qr_ref, kn_ref, kr_ref, v_ref, o_ref, lse_ref, do_ref, dqn_ref, dqr_ref, dkn_ref, dkr_ref, dv_ref):
        qb = pl.program_id(1)

        @pl.when(qb == 0)
        def _():
            dkn_ref[...] = jnp.zeros(dkn_ref.shape, F32)
            dkr_ref[...] = jnp.zeros(dkr_ref.shape, F32)
            dv_ref[...] = jnp.zeros(dv_ref.shape, F32)

        qv = jnp.concatenate([qn_ref[0], qr_ref[0]], axis=1)
        dov = do_ref[...]
        dob = dov.astype(BF16)
        delta = jnp.sum(dov * o_ref[...], axis=1, keepdims=True)
        lse_col = lse_ref[:, :1]
        order = _key_order(bq, qb)

        def body(kb, dq):
            off = pl.multiple_of(kb * ATT_BLOCK, ATT_BLOCK)
            kv = jnp.concatenate([kn_ref[0, pl.ds(off, ATT_BLOCK), :], kr_ref[pl.ds(off, ATT_BLOCK), :]], axis=1)
            vv = v_ref[0, pl.ds(off, ATT_BLOCK), :]
            p = jnp.where(order >= kb * ATT_BLOCK, jnp.exp(_dot_nt(qv, kv) - lse_col), 0.0)
            ds = (p * (_dot_nt(dob, vv) - delta)).astype(BF16)
            dk = _dot_tn(ds, qv)
            dkn_ref[0, pl.ds(off, ATT_BLOCK), :] += dk[:, :HEAD]
            dkr_ref[0, pl.ds(off, ATT_BLOCK), :] += dk[:, HEAD:]
            dv_ref[0, pl.ds(off, ATT_BLOCK), :] += _dot_tn(p.astype(BF16), dob)
            return dq + _dot(ds, kv)

        dq = lax.fori_loop(0, (qb + 1) * per, body, jnp.zeros((bq, 2 * HEAD), F32))
        dqn_ref[0] = dq[:, :HEAD]
        dqr_ref[0] = dq[:, HEAD:]

    hm = _sds((nh, t, HEAD))
    blk = pl.BlockSpec((1, bq, HEAD), lambda h, i: (h, i, 0))
    full = pl.BlockSpec((1, t, HEAD), lambda h, i: (h, 0, 0))
    tok = pl.BlockSpec((bq, HEAD), lambda h, i: (i, h))
    return pl.pallas_call(
        kern, name="mla_attn_bwd", grid=(nh, t // bq),
        in_specs=[blk, blk, full, pl.BlockSpec((t, HEAD), lambda h, i: (0, 0)), full, tok, tok, tok],
        out_specs=[blk, blk, full, full, full], out_shape=[hm, hm, hm, hm, hm],
        compiler_params=_cparams(dimension_semantics=("parallel", "arbitrary")),
    )(qn, qr, kn, kr, v, o, lse, do)


def _rope_tables(t):
    inv_freq = ROPE_THETA ** (-jnp.arange(0, MLA_ROPE, 2, dtype=F32) / MLA_ROPE)
    ang = jnp.arange(t, dtype=F32)[:, None] * inv_freq[None, :]
    c, s = jnp.cos(ang), jnp.sin(ang)
    pad = ((0, 0), (0, LANES - MLA_ROPE))
    return jnp.pad(jnp.concatenate([c, c], axis=1), pad), jnp.pad(jnp.concatenate([-s, s], axis=1), pad)


def _pad_lanes(v, n=LANES):
    return jnp.pad(v, (0, n - v.shape[0])).reshape(1, n)


def _mla_layout(w_down, w_uq, w_ukv):
    w_down_p = jnp.pad(w_down, ((0, 0), (0, MLA_C - w_down.shape[1])))
    uq = w_uq.reshape(w_uq.shape[0], N_HEADS, MLA_QK)
    rope = jnp.pad(uq[:, :, HEAD:], ((0, 0), (0, 0), (0, LANES - MLA_ROPE)))
    w_uq_p = jnp.concatenate([uq[:, :, :HEAD].reshape(-1, N_HEADS * HEAD), rope.reshape(-1, N_HEADS * LANES)], axis=1)
    ukv = w_ukv.reshape(w_ukv.shape[0], N_HEADS, 2 * HEAD)
    w_ukv_p = jnp.concatenate([ukv[:, :, :HEAD].reshape(-1, N_HEADS * HEAD), ukv[:, :, HEAD:].reshape(-1, N_HEADS * HEAD)], axis=1)
    return w_down_p, w_uq_p, w_ukv_p


def _mla_unlayout(d_down_p, d_uq_p, d_ukv_p):
    d_down = d_down_p[:, :256 + HEAD + MLA_ROPE]
    nope = d_uq_p[:, :N_HEADS * HEAD].reshape(-1, N_HEADS, HEAD)
    rope = d_uq_p[:, N_HEADS * HEAD:].reshape(-1, N_HEADS, LANES)[:, :, :MLA_ROPE]
    d_uq = jnp.concatenate([nope, rope], axis=2).reshape(-1, N_HEADS * MLA_QK)
    kn = d_ukv_p[:, :N_HEADS * HEAD].reshape(-1, N_HEADS, HEAD)
    vv = d_ukv_p[:, N_HEADS * HEAD:].reshape(-1, N_HEADS, HEAD)
    d_ukv = jnp.concatenate([kn, vv], axis=2).reshape(-1, N_HEADS * 2 * HEAD)
    return d_down, d_uq, d_ukv


def _mla_weight_shapes():
    return (_sds((1024, MLA_C), BF16), _sds((1, 256)), _sds((1, HEAD)), _sds((256, 2048), BF16), _sds((HEAD, 2048), BF16),
            _sds((1, HEAD)), _sds((1, HEAD)), _sds((1, HEAD)), _sds((1, HEAD)), _sds((1024, 1024), BF16),
            _sds((4096, HEAD)), _sds((4096, HEAD)))


def _mla_fwd(x, h, w_down, gq, gkv, w_uq, w_ukv, gqn, gqr, gkn, gkr, w_out, cos, sin):
    c = _mm("mla_down", h, w_down)
    cq, ckv, kr = _mla_prep1_fwd(c, gq, gkv, gkr, cos, sin)
    qa = _mm("mla_uq", cq, w_uq)
    kv = _mm("mla_ukv", ckv, w_ukv)
    qn, qr, kn, v = _mla_prep2_fwd(qa, kv, gqn, gqr, gkn, cos, sin)
    o, lse = _mla_attn_fwd(qn, qr, kn, kr, v)
    y = _mm("mla_out", o, w_out, add=x)
    return y, (h, c, cq, ckv, kr, qa, kv, qn, qr, kn, v, o, lse)


def _mla_bwd(saved, dy, w_down, gq, gkv, w_uq, w_ukv, gqn, gqr, gkn, gkr, w_out, cos, sin):
    h, c, cq, ckv, kr, qa, kv, qn, qr, kn, v, o, lse = saved
    d_wout = _mm("mla_out_wgrad", o, dy, ta=True)
    do = _mm("mla_out_dgrad", dy, w_out, tb=True)
    dqn, dqr, dkn, dkr, dv = _mla_attn_bwd(qn, qr, kn, kr, v, o, lse, do)
    dqa, dkv, dgqn, dgqr, dgkn = _mla_prep2_bwd(qa, kv, gqn, gqr, gkn, cos, sin, dqn, dqr, dkn, dv)
    d_wuq = _mm("mla_uq_wgrad", cq, dqa, ta=True)
    d_wukv = _mm("mla_ukv_wgrad", ckv, dkv, ta=True)
    dcq = _mm("mla_uq_dgrad", dqa, w_uq, tb=True)
    dckv = _mm("mla_ukv_dgrad", dkv, w_ukv, tb=True)
    dc, dgq, dgkv, dgkr = _mla_prep1_bwd(c, gq, gkv, gkr, cos, sin, dcq, dckv, dkr)
    d_wdown = _mm("mla_down_wgrad", h, dc, ta=True)
    dh = _mm("mla_down_dgrad", dc, w_down, tb=True)
    return dh, d_wdown, dgq, dgkv, d_wuq, d_wukv, dgqn, dgqr, dgkn, dgkr, d_wout


def _loss_head(y, target):
    d = y.shape[1]

    def body(y_ref, t_ref, dy_ref, l_ref):
        err = y_ref[...] - t_ref[...]
        dy_ref[...] = err * (1.0 / d)
        part = 0.5 * jnp.sum(jnp.sum(err * err, axis=1, keepdims=True) * (1.0 / d), axis=0, keepdims=True)
        _acc(l_ref, jnp.broadcast_to(part, (1, LANES)))

    return _rows("loss_head", body, [y, target], [_sds(y.shape)], tt=512, accs=[_sds((1, LANES))])


MESH_ID = pl.DeviceIdType.MESH
HBM_SPEC = pl.BlockSpec(memory_space=pltpu.HBM)


def _all_gather(name, x):
    m_per, n = x.shape

    def body(x_ref, out_ref, send_sems, recv_sems, local_sem):
        x, y, c = lax.axis_index("x"), lax.axis_index("y"), lax.axis_index("c")
        me, sibling = (x, y, c), (x, y, 1 - c)
        chips = [(1 - x, y), (x, 1 - y), (1 - x, 1 - y)]

        def rows(px, py, pc):
            return out_ref.at[pl.ds((4 * px + 2 * py + pc) * m_per, m_per), :]

        def copy(k, block, to, src=None):
            return pltpu.make_async_remote_copy(
                src_ref=rows(*block) if src is None else src, dst_ref=rows(*block),
                send_sem=send_sems.at[k], recv_sem=recv_sems.at[k], device_id=to, device_id_type=MESH_ID)

        mine = pltpu.make_async_copy(x_ref, rows(*me), local_sem)
        mine.start()
        first = [copy(0, me, sibling, src=x_ref)]
        first += [copy(1 + j, me, (*chip, c), src=x_ref) for j, chip in enumerate(chips)]
        for cp in first:
            cp.start()
        passed = [copy(4 + j, (*chip, c), sibling) for j, chip in enumerate(chips)]
        for j, chip in enumerate(chips):
            copy(1 + j, (*chip, c), me).wait_recv()
            passed[j].start()
        copy(0, sibling, me).wait_recv()
        for j, chip in enumerate(chips):
            copy(4 + j, (*chip, 1 - c), me).wait_recv()
        for cp in first + passed:
            cp.wait_send()
        mine.wait()

    return pl.pallas_call(
        body, name=name,
        out_shape=jax.ShapeDtypeStruct((N_DEV * m_per, n), x.dtype),
        in_specs=[HBM_SPEC], out_specs=HBM_SPEC,
        scratch_shapes=[pltpu.SemaphoreType.DMA((7,)), pltpu.SemaphoreType.DMA((7,)), pltpu.SemaphoreType.DMA],
    )(x)


def _all_gather_groups(name, xs):
    ng = len(xs)

    def body(*refs):
        x_refs, out_refs = refs[:ng], refs[ng:2 * ng]
        send_sems, recv_sems, local_sems = refs[2 * ng:]
        x, y, c = lax.axis_index("x"), lax.axis_index("y"), lax.axis_index("c")
        me, sibling = (x, y, c), (x, y, 1 - c)
        chips = [(1 - x, y), (x, 1 - y), (1 - x, 1 - y)]

        def copy(g, k, block, to, src=None):
            px, py, pc = block
            dst = out_refs[g].at[4 * px + 2 * py + pc]
            return pltpu.make_async_remote_copy(
                src_ref=dst if src is None else src, dst_ref=dst,
                send_sem=send_sems.at[g, k], recv_sem=recv_sems.at[g, k], device_id=to, device_id_type=MESH_ID)

        mine = [pltpu.make_async_copy(x_refs[g], out_refs[g].at[4 * x + 2 * y + c], local_sems.at[g]) for g in range(ng)]
        for cp in mine:
            cp.start()
        first = []
        for g in range(ng):
            first.append(copy(g, 0, me, sibling, src=x_refs[g]))
            first += [copy(g, 1 + j, me, (*chip, c), src=x_refs[g]) for j, chip in enumerate(chips)]
        for cp in first:
            cp.start()
        passed = []
        for j, chip in enumerate(chips):
            for g in range(ng):
                copy(g, 1 + j, (*chip, c), me).wait_recv()
                passed.append(copy(g, 4 + j, (*chip, c), sibling))
                passed[-1].start()
        for g in range(ng):
            copy(g, 0, sibling, me).wait_recv()
            for j, chip in enumerate(chips):
                copy(g, 4 + j, (*chip, 1 - c), me).wait_recv()
        for cp in first + passed:
            cp.wait_send()
        for cp in mine:
            cp.wait()

    return pl.pallas_call(
        body, name=name,
        out_shape=[jax.ShapeDtypeStruct((N_DEV,) + x.shape, x.dtype) for x in xs],
        in_specs=[HBM_SPEC] * ng, out_specs=[HBM_SPEC] * ng,
        scratch_shapes=[pltpu.SemaphoreType.DMA((ng, 7)), pltpu.SemaphoreType.DMA((ng, 7)), pltpu.SemaphoreType.DMA((ng,))],
    )(*xs)


def _rs_cores(gs):
    ng = len(gs)

    def body(*refs):
        g_refs, recv_refs = refs[:ng], refs[ng:2 * ng]
        send_sems, recv_sems = refs[2 * ng:]
        x, y, c = lax.axis_index("x"), lax.axis_index("y"), lax.axis_index("c")
        copies = [pltpu.make_async_remote_copy(
            src_ref=g_refs[g].at[:, 1 - c], dst_ref=recv_refs[g], send_sem=send_sems.at[g], recv_sem=recv_sems.at[g],
            device_id=(x, y, 1 - c), device_id_type=MESH_ID) for g in range(ng)]
        for cp in copies:
            cp.start()
        for cp in copies:
            cp.wait()

    return pl.pallas_call(
        body, name="rs_cores",
        out_shape=[jax.ShapeDtypeStruct((g.shape[0],) + g.shape[2:], g.dtype) for g in gs],
        in_specs=[HBM_SPEC] * ng, out_specs=[HBM_SPEC] * ng,
        scratch_shapes=[pltpu.SemaphoreType.DMA((ng,)), pltpu.SemaphoreType.DMA((ng,))],
    )(*gs)


def _rs_add(g, recv, core):
    nchip, _, r, c_ = g.shape
    tr = _pick(r, (512, 384, 256, 128))

    def body(core_ref, g_ref, r_ref, p_ref, pb_ref):
        s = g_ref[0, 0] + r_ref[0]
        p_ref[0] = s
        pb_ref[0] = s.astype(BF16)

    return pl.pallas_call(
        body, name="rs_add",
        grid_spec=pltpu.PrefetchScalarGridSpec(
            num_scalar_prefetch=1, grid=(nchip, r // tr),
            in_specs=[pl.BlockSpec((1, 1, tr, c_), lambda k, i, core_ref: (k, core_ref[0], i, 0)),
                      pl.BlockSpec((1, tr, c_), lambda k, i, core_ref: (k, i, 0))],
            out_specs=[pl.BlockSpec((1, tr, c_), lambda k, i, core_ref: (k, i, 0)),
                       pl.BlockSpec((1, tr, c_), lambda k, i, core_ref: (k, i, 0))]),
        out_shape=[_sds((nchip, r, c_)), _sds((nchip, r, c_), BF16)],
        compiler_params=_cparams(dimension_semantics=("arbitrary", "arbitrary")),
    )(core, g, recv)


def _rs_chips(pbs):
    ng = len(pbs)

    def body(*refs):
        p_refs, recv_refs = refs[:ng], refs[ng:2 * ng]
        send_sems, recv_sems, local_sems = refs[2 * ng:]
        x, y, c = lax.axis_index("x"), lax.axis_index("y"), lax.axis_index("c")
        my = 2 * x + y
        chips = [(1 - x, y), (x, 1 - y), (1 - x, 1 - y)]
        local = [pltpu.make_async_copy(p_refs[g].at[my], recv_refs[g].at[my], local_sems.at[g]) for g in range(ng)]
        for cp in local:
            cp.start()

        def copy(g, k, src_chip, dst_chip, to):
            return pltpu.make_async_remote_copy(
                src_ref=p_refs[g].at[src_chip], dst_ref=recv_refs[g].at[dst_chip],
                send_sem=send_sems.at[g, k], recv_sem=recv_sems.at[g, k], device_id=to, device_id_type=MESH_ID)

        sends = [copy(g, k, 2 * px + py, my, (px, py, c)) for g in range(ng) for k, (px, py) in enumerate(chips)]
        for cp in sends:
            cp.start()
        for g in range(ng):
            for k, (px, py) in enumerate(chips):
                copy(g, k, my, 2 * px + py, (px, py, c)).wait_recv()
        for cp in sends:
            cp.wait_send()
        for cp in local:
            cp.wait()

    return pl.pallas_call(
        body, name="rs_chips", out_shape=[jax.ShapeDtypeStruct(p.shape, p.dtype) for p in pbs],
        in_specs=[HBM_SPEC] * ng, out_specs=[HBM_SPEC] * ng,
        scratch_shapes=[pltpu.SemaphoreType.DMA((ng, 3)), pltpu.SemaphoreType.DMA((ng, 3)), pltpu.SemaphoreType.DMA((ng,))],
    )(*pbs)


def _cols_from_shards(w, width):
    ns, r, cs = w.shape
    tr = _pick(r, (256, 128))

    def body(w_ref, o_ref):
        parts = [w_ref[j] for j in range(ns)]
        if width > ns * cs:
            parts.append(jnp.zeros((tr, width - ns * cs), w.dtype))
        o_ref[...] = jnp.concatenate(parts, axis=1)

    return pl.pallas_call(
        body, name="cols_from_shards", grid=(r // tr,),
        in_specs=[pl.BlockSpec((ns, tr, cs), lambda i: (0, i, 0))], out_specs=pl.BlockSpec((tr, width), lambda i: (i, 0)),
        out_shape=jax.ShapeDtypeStruct((r, width), w.dtype), compiler_params=_cparams(dimension_semantics=("arbitrary",)),
    )(w)


def _shards_from_cols(g, cs):
    r, width = g.shape
    tr = _pick(r, (256, 128))

    def body(g_ref, o_ref):
        for j in range(N_DEV):
            o_ref[j] = g_ref[:, j * cs:(j + 1) * cs]

    return pl.pallas_call(
        body, name="shards_from_cols", grid=(r // tr,),
        in_specs=[pl.BlockSpec((tr, width), lambda i: (i, 0))], out_specs=pl.BlockSpec((N_DEV, tr, cs), lambda i: (0, i, 0)),
        out_shape=jax.ShapeDtypeStruct((N_DEV, r, cs), g.dtype), compiler_params=_cparams(dimension_semantics=("arbitrary",)),
    )(g)


def _adam(w, g, m, v):
    m = ADAM_B1 * m + (1.0 - ADAM_B1) * g
    v = ADAM_B2 * v + (1.0 - ADAM_B2) * (g * g)
    m_hat = m / (1.0 - ADAM_B1 ** ADAM_STEP)
    v_hat = v / (1.0 - ADAM_B2 ** ADAM_STEP)
    return -ADAM_LR * (m_hat / (jnp.sqrt(v_hat) + ADAM_EPS) + ADAM_WD * w), m, v


def _sum_adam(p, recv, chip, w, m, v):
    nchip, r, c_ = recv.shape
    tr = _pick(r, (512, 384, 256, 128))

    def body(chip_ref, p_ref, r_ref, w_ref, m_ref, v_ref, g_ref, d_ref, mo_ref, vo_ref):
        my = chip_ref[0]
        g = jnp.where(my == 0, p_ref[0], r_ref[0].astype(F32))
        for k in range(1, nchip):
            g = g + jnp.where(my == k, p_ref[0], r_ref[k].astype(F32))
        g_ref[...] = g
        d_ref[...], mo_ref[...], vo_ref[...] = _adam(w_ref[...], g, m_ref[...], v_ref[...])

    row = pl.BlockSpec((tr, c_), lambda i, chip_ref: (i, 0))
    return pl.pallas_call(
        body, name="sum_adam",
        grid_spec=pltpu.PrefetchScalarGridSpec(
            num_scalar_prefetch=1, grid=(r // tr,),
            in_specs=[pl.BlockSpec((1, tr, c_), lambda i, chip_ref: (chip_ref[0], i, 0)),
                      pl.BlockSpec((nchip, tr, c_), lambda i, chip_ref: (0, i, 0)), row, row, row],
            out_specs=[row, row, row, row]),
        out_shape=[_sds((r, c_))] * 4,
        compiler_params=_cparams(dimension_semantics=("arbitrary",)),
    )(chip, p, recv, w, m, v)


def _sum_devices(gathered):
    m_all, n = gathered.shape
    m_per = m_all // N_DEV

    def body(x_ref, o_ref):
        s = x_ref[0:m_per, :]
        for j in range(1, N_DEV):
            s = s + x_ref[j * m_per:(j + 1) * m_per, :]
        o_ref[...] = s

    return pl.pallas_call(body, name="sum_devices", out_shape=_sds((m_per, n)), compiler_params=_cparams())(gathered)


def _adam_small(w, g, m, v):
    def body(w_ref, g_ref, m_ref, v_ref, d_ref, mo_ref, vo_ref):
        d_ref[...], mo_ref[...], vo_ref[...] = _adam(w_ref[...], g_ref[...], m_ref[...], v_ref[...])

    return pl.pallas_call(body, name="adam_small", out_shape=[_sds(w.shape)] * 3, compiler_params=_cparams())(w, g, m, v)


N_LAYERS = 4
_MIXER = ("dn", "sb", "mla")
_MIXER_PARAMS = {
    "dn": ("dn_w_in", "dn_conv_w", "dn_a_log", "dn_dt_bias", "dn_out_norm", "dn_w_out"),
    "sb": ("sb_w_qkv", "sb_q_norm", "sb_k_norm", "sb_w_out"),
    "mla": ("mla_w_down", "mla_q_a_norm", "mla_kv_a_norm", "mla_w_uq", "mla_w_ukv", "mla_q_nope_norm", "mla_q_rope_norm",
            "mla_k_nope_norm", "mla_k_rope_norm", "mla_w_out"),
}
_BIG_AXIS = {"dn_w_in": 1, "dn_w_out": 0, "sb_w_qkv": 1, "sb_w_out": 0, "mla_w_down": 0, "mla_w_uq": 1, "mla_w_ukv": 1,
             "mla_w_out": 0, "ffn_w_gate_up": 1, "ffn_w_down": 0}


def _weight_names():
    names = []
    for i in range(N_LAYERS):
        p = "l%d_" % i
        names += [p + "mix_norm"] + [p + n for n in _MIXER_PARAMS[_MIXER[i % 3]]] + [p + "ffn_norm", p + "ffn_w_gate_up", p + "ffn_w_down"]
    return names


WEIGHTS = _weight_names()
BIG = [n for n in WEIGHTS if n[3:] in _BIG_AXIS]
SMALL = [n for n in WEIGHTS if n[3:] not in _BIG_AXIS]
CONV = [n for n in SMALL if n.endswith("conv_w")]


def _ceil_to(n, k):
    return -(-n // k) * k


def _pack(arrs, cols, row_mult):
    parts = []
    for a in arrs:
        f = a.reshape(-1)
        parts.append(jnp.pad(f, (0, _ceil_to(f.shape[0], cols) - f.shape[0])))
    flat = jnp.concatenate(parts)
    rows = _ceil_to(flat.shape[0] // cols, row_mult)
    return jnp.pad(flat, (0, rows * cols - flat.shape[0])).reshape(rows, cols)


def _unpack(buf, shapes):
    cols = buf.shape[-1]
    out, r0 = [], 0
    for s in shapes:
        n = math.prod(s)
        nr = _ceil_to(n, cols) // cols
        out.append(buf[r0:r0 + nr].reshape(-1)[:n].reshape(s))
        r0 += nr
    return out


def _groups():
    by = {"gu": (704, []), "row": (1024, []), "dn_in": (514, []), "sb_qkv": (384, []), "mla": (512, [])}
    key = {"ffn_w_gate_up": "gu", "dn_w_in": "dn_in", "sb_w_qkv": "sb_qkv", "mla_w_down": "mla", "mla_w_uq": "mla", "mla_w_ukv": "mla"}
    for n in BIG:
        by[key.get(n[3:], "row")][1].append(n)
    return list(by.values())


GROUPS = _groups()


def _stack_group(grp, get):
    width, names = grp
    return jnp.concatenate([jnp.pad(get(n), ((0, 0), (0, width - get(n).shape[1]))) for n in names], axis=0)


def _unstack_group(grp, buf, shape_of):
    out, r0 = [], 0
    for n in grp[1]:
        rs, cs = shape_of(n)
        out.append(buf[..., r0:r0 + rs, :cs])
        r0 += rs
    return out


def kernel(x, l0_mix_norm, l0_dn_w_in, l0_dn_conv_w, l0_dn_a_log, l0_dn_dt_bias, l0_dn_out_norm, l0_dn_w_out, l0_ffn_norm, l0_ffn_w_gate_up, l0_ffn_w_down, l1_mix_norm, l1_sb_w_qkv, l1_sb_q_norm, l1_sb_k_norm, l1_sb_w_out, l1_ffn_norm, l1_ffn_w_gate_up, l1_ffn_w_down, l2_mix_norm, l2_mla_w_down, l2_mla_q_a_norm, l2_mla_kv_a_norm, l2_mla_w_uq, l2_mla_w_ukv, l2_mla_q_nope_norm, l2_mla_q_rope_norm, l2_mla_k_nope_norm, l2_mla_k_rope_norm, l2_mla_w_out, l2_ffn_norm, l2_ffn_w_gate_up, l2_ffn_w_down, l3_mix_norm, l3_dn_w_in, l3_dn_conv_w, l3_dn_a_log, l3_dn_dt_bias, l3_dn_out_norm, l3_dn_w_out, l3_ffn_norm, l3_ffn_w_gate_up, l3_ffn_w_down, loss_target, m_l0_mix_norm, m_l0_dn_w_in, m_l0_dn_conv_w, m_l0_dn_a_log, m_l0_dn_dt_bias, m_l0_dn_out_norm, m_l0_dn_w_out, m_l0_ffn_norm, m_l0_ffn_w_gate_up, m_l0_ffn_w_down, m_l1_mix_norm, m_l1_sb_w_qkv, m_l1_sb_q_norm, m_l1_sb_k_norm, m_l1_sb_w_out, m_l1_ffn_norm, m_l1_ffn_w_gate_up, m_l1_ffn_w_down, m_l2_mix_norm, m_l2_mla_w_down, m_l2_mla_q_a_norm, m_l2_mla_kv_a_norm, m_l2_mla_w_uq, m_l2_mla_w_ukv, m_l2_mla_q_nope_norm, m_l2_mla_q_rope_norm, m_l2_mla_k_nope_norm, m_l2_mla_k_rope_norm, m_l2_mla_w_out, m_l2_ffn_norm, m_l2_ffn_w_gate_up, m_l2_ffn_w_down, m_l3_mix_norm, m_l3_dn_w_in, m_l3_dn_conv_w, m_l3_dn_a_log, m_l3_dn_dt_bias, m_l3_dn_out_norm, m_l3_dn_w_out, m_l3_ffn_norm, m_l3_ffn_w_gate_up, m_l3_ffn_w_down, v_l0_mix_norm, v_l0_dn_w_in, v_l0_dn_conv_w, v_l0_dn_a_log, v_l0_dn_dt_bias, v_l0_dn_out_norm, v_l0_dn_w_out, v_l0_ffn_norm, v_l0_ffn_w_gate_up, v_l0_ffn_w_down, v_l1_mix_norm, v_l1_sb_w_qkv, v_l1_sb_q_norm, v_l1_sb_k_norm, v_l1_sb_w_out, v_l1_ffn_norm, v_l1_ffn_w_gate_up, v_l1_ffn_w_down, v_l2_mix_norm, v_l2_mla_w_down, v_l2_mla_q_a_norm, v_l2_mla_kv_a_norm, v_l2_mla_w_uq, v_l2_mla_w_ukv, v_l2_mla_q_nope_norm, v_l2_mla_q_rope_norm, v_l2_mla_k_nope_norm, v_l2_mla_k_rope_norm, v_l2_mla_w_out, v_l2_ffn_norm, v_l2_ffn_w_gate_up, v_l2_ffn_w_down, v_l3_mix_norm, v_l3_dn_w_in, v_l3_dn_conv_w, v_l3_dn_a_log, v_l3_dn_dt_bias, v_l3_dn_out_norm, v_l3_dn_w_out, v_l3_ffn_norm, v_l3_ffn_w_gate_up, v_l3_ffn_w_down):
    a = dict(locals())
    return _train_step(a)


def _train_step(a):
    mx, my, mc = lax.axis_index("x"), lax.axis_index("y"), lax.axis_index("c")
    dev = 4 * mx + 2 * my + mc
    core = jnp.reshape(mc, (1,)).astype(jnp.int32)
    chip = jnp.reshape(2 * mx + my, (1,)).astype(jnp.int32)
    t, d = a["x"].shape[1], a["x"].shape[2]
    xs = a["x"].reshape(t, d)
    target = a["loss_target"].reshape(t, d)

    gathered = _all_gather_groups("gather_weights", [_stack_group(grp, lambda n: a[n].astype(BF16)) for grp in GROUPS])
    full = {}
    for grp, buf in zip(GROUPS, gathered):
        for n, shards in zip(grp[1], _unstack_group(grp, buf, lambda n: a[n].shape)):
            kind = n[3:]
            if kind == "ffn_w_gate_up":
                full[n] = shards
            elif _BIG_AXIS[kind] == 0:
                full[n] = shards.reshape(N_DEV * shards.shape[1], shards.shape[2])
            else:
                width = DN_PROJ if kind == "dn_w_in" else N_DEV * shards.shape[2]
                full[n] = _cols_from_shards(shards, width)
    conv_pack = _pack([a[n] for n in CONV], LANES, 8)
    conv_all = _all_gather("gather_conv", conv_pack).reshape(N_DEV, conv_pack.shape[0], LANES)
    for n, parts in zip(CONV, zip(*[_unpack(conv_all[j], [a[n].shape for n in CONV]) for j in range(N_DEV)])):
        full[n] = jnp.concatenate(parts, axis=1)

    def vec(n):
        return a[n].reshape(1, -1)

    cos, sin = _rope_tables(t)
    layer_args = []
    for i in range(N_LAYERS):
        p = "l%d_" % i
        kind = _MIXER[i % 3]
        if kind == "dn":
            args = (full[p + "dn_w_in"], full[p + "dn_conv_w"], _pad_lanes(a[p + "dn_a_log"]), _pad_lanes(a[p + "dn_dt_bias"]),
                    vec(p + "dn_out_norm"), full[p + "dn_w_out"])
        elif kind == "sb":
            args = (full[p + "sb_w_qkv"], vec(p + "sb_q_norm"), vec(p + "sb_k_norm"), full[p + "sb_w_out"])
        else:
            w_down, w_uq, w_ukv = _mla_layout(full[p + "mla_w_down"], full[p + "mla_w_uq"], full[p + "mla_w_ukv"])
            args = (w_down, vec(p + "mla_q_a_norm"), vec(p + "mla_kv_a_norm"), w_uq, w_ukv, vec(p + "mla_q_nope_norm"),
                    _pad_lanes(a[p + "mla_q_rope_norm"]), vec(p + "mla_k_nope_norm"), _pad_lanes(a[p + "mla_k_rope_norm"]),
                    full[p + "mla_w_out"], cos, sin)
        layer_args.append((kind, args))

    fwd = {"dn": _dn_fwd, "sb": _sb_fwd, "mla": _mla_fwd}
    bwd = {"dn": _dn_bwd, "sb": _sb_bwd, "mla": _mla_bwd}
    saved = []
    for i, (kind, args) in enumerate(layer_args):
        p = "l%d_" % i
        h = _rmsnorm_fwd("mix_norm", xs, vec(p + "mix_norm"))
        x_mid, sv_mix = fwd[kind](xs, h, *args)
        x_out, sv_ffn = _ffn_fwd(x_mid, vec(p + "ffn_norm"), full[p + "ffn_w_gate_up"], full[p + "ffn_w_down"])
        saved.append((xs, sv_mix, sv_ffn))
        xs = x_out
    dy, loss_part = _loss_head(xs, target)

    grads = {}
    for i in reversed(range(N_LAYERS)):
        p = "l%d_" % i
        kind, args = layer_args[i]
        x_in, sv_mix, sv_ffn = saved[i]
        dx_mid, grads[p + "ffn_norm"], grads[p + "ffn_w_gate_up"], grads[p + "ffn_w_down"] = _ffn_bwd(
            sv_ffn, dy, vec(p + "ffn_norm"), full[p + "ffn_w_gate_up"], full[p + "ffn_w_down"])
        res = bwd[kind](sv_mix, dx_mid, *args)
        dh = res[0]
        if kind == "mla":
            res = list(res)
            res[1], res[4], res[5] = _mla_unlayout(res[1], res[4], res[5])
        for n, g in zip(_MIXER_PARAMS[kind], res[1:]):
            grads[p + n] = g
        dy, grads[p + "mix_norm"] = _rmsnorm_bwd("mix_norm_bwd", x_in, vec(p + "mix_norm"), dh, dx_mid)
    grad_x = dy.reshape(a["x"].shape)

    small_full_shapes = [full[n].shape if n in CONV else a[n].shape for n in SMALL]
    small_grads = []
    for n, s in zip(SMALL, small_full_shapes):
        g = grads[n].reshape(-1)
        small_grads.append(g[:math.prod(s)])
    small_pack = _pack(small_grads + [loss_part.reshape(-1)], LANES, 8)
    small_sum = _sum_devices(_all_gather("gather_small_grads", small_pack))
    small_red = _unpack(small_sum, small_full_shapes + [(LANES,)])
    loss = small_red[-1][0]
    g_small = {}
    for n, g in zip(SMALL, small_red[:-1]):
        if n in CONV:
            cs = a[n].shape[1]
            g = lax.dynamic_slice_in_dim(g, dev * cs, cs, axis=1)
        g_small[n] = g
    small_shapes = [a[n].shape for n in SMALL]
    packs = [_pack([src[n] for n in SMALL], LANES, 8) for src in
             ({n: a[n] for n in SMALL}, g_small, {n: a["m_" + n] for n in SMALL}, {n: a["v_" + n] for n in SMALL})]
    d_small, m_small, v_small = (_unpack(o, small_shapes) for o in _adam_small(*packs))

    def grad_shards(n):
        g, (rs, cs) = grads[n], a[n].shape
        if g.ndim == 3:
            return g
        if _BIG_AXIS[n[3:]] == 0:
            return g.reshape(N_DEV, rs, cs)
        return _shards_from_cols(g, cs)

    g_groups = []
    for grp in GROUPS:
        parts = [jnp.pad(grad_shards(n), ((0, 0), (0, 0), (0, grp[0] - a[n].shape[1]))) for n in grp[1]]
        g_all = jnp.concatenate(parts, axis=1)
        g_groups.append(g_all.reshape((N_DEV // 2, 2) + g_all.shape[1:]))
    from_sibling = _rs_cores(g_groups)
    parts = [_rs_add(g, r, core) for g, r in zip(g_groups, from_sibling)]
    from_chips = _rs_chips([pb for _, pb in parts])
    big_out = {}
    for grp, (part, _), recv in zip(GROUPS, parts, from_chips):
        packs = [_stack_group(grp, lambda n, pre=pre: a[pre + n]) for pre in ("", "m_", "v_")]
        outs = [_unstack_group(grp, o, lambda n: a[n].shape) for o in _sum_adam(part, recv, chip, *packs)]
        for i, n in enumerate(grp[1]):
            big_out[n] = [o[i] for o in outs]

    small_out = dict(zip(SMALL, zip([g_small[n] for n in SMALL], d_small, m_small, v_small)))

    def out(k):
        return [small_out[n][k] if n in small_out else big_out[n][k] for n in WEIGHTS]

    return (loss, grad_x, *out(0), *out(1), *out(2), *out(3))
```

```python
import math

import jax
import jax.numpy as jnp
from jax import lax
from jax.experimental import pallas as pl
from jax.experimental.pallas import tpu as pltpu

F32 = jnp.float32
BF16 = jnp.bfloat16
F32X3 = lax.Precision.HIGH

LANES = 128
N_DEV = 8
N_HEADS = 8
HEAD = 128
NORM_EPS = 1e-6
DN_CHUNK = 64
ATT_BLOCK = 512
ATT_Q = 512
MLA_ROPE = 64
MLA_QK = 192
ROPE_THETA = 10000.0
VMEM_LIMIT = 56 * 1024 * 1024

ADAM_LR = 0.001
ADAM_B1 = 0.9
ADAM_B2 = 0.999
ADAM_EPS = 1e-08
ADAM_WD = 0.01
ADAM_STEP = 10


def _cparams(**kw):
    return pltpu.CompilerParams(vmem_limit_bytes=VMEM_LIMIT, **kw)


def _pick(n, cands):
    for c in cands:
        if c <= n and n % c == 0:
            return c
    return n


def _mm(name, a, b, *, ta=False, tb=False, out_dtype=F32, add=None, tm=None, tn=None, tk=None):
    if ta:
        K, M = a.shape
    else:
        M, K = a.shape
    N = b.shape[0] if tb else b.shape[1]
    tm = tm or _pick(M, (1024, 512, 256, 128))
    tn = tn or _pick(N, (1024, 512, 384, 256, 128))
    tk = tk or _pick(K, (1024, 1408, 512, 384, 256, 128))
    return _mm_raw(
        name, a, b, ta=ta, tb=tb, out_dtype=out_dtype, add=add, grid=(M // tm, N // tn, K // tk), out_shape=(M, N),
        a_block=(tk, tm) if ta else (tm, tk), a_map=(lambda i, j, k: (k, i)) if ta else (lambda i, j, k: (i, k)),
        b_block=(tn, tk) if tb else (tk, tn), b_map=(lambda i, j, k: (j, k)) if tb else (lambda i, j, k: (k, j)),
        o_block=(tm, tn), o_map=lambda i, j, k: (i, j))


def _mm_raw(name, a, b, *, ta, tb, out_dtype, add, grid, out_shape, a_block, a_map, b_block, b_map, o_block, o_map):
    nk = grid[2]
    tm, tn = o_block
    dn = (((0 if ta else 1,), (1 if tb else 0,)), ((), ()))
    has_add = add is not None

    def kern(*refs):
        if has_add:
            a_ref, b_ref, add_ref, o_ref, acc_ref = refs
        else:
            a_ref, b_ref, o_ref, acc_ref = refs
        k = pl.program_id(2)
        part = lax.dot_general(a_ref[...].astype(BF16), b_ref[...].astype(BF16), dn, preferred_element_type=F32)

        @pl.when(k == 0)
        def _():
            acc_ref[...] = part

        @pl.when(k > 0)
        def _():
            acc_ref[...] += part

        @pl.when(k == nk - 1)
        def _():
            r = acc_ref[...]
            if has_add:
                r = r + add_ref[...]
            o_ref[...] = r.astype(out_dtype)

    in_specs = [pl.BlockSpec(a_block, a_map), pl.BlockSpec(b_block, b_map)]
    args = [a, b]
    if has_add:
        in_specs.append(pl.BlockSpec(o_block, o_map))
        args.append(add)
    return pl.pallas_call(
        kern, name=name,
        grid=grid,
        in_specs=in_specs,
        out_specs=pl.BlockSpec(o_block, o_map),
        out_shape=jax.ShapeDtypeStruct(out_shape, out_dtype),
        scratch_shapes=[pltpu.VMEM((tm, tn), F32)],
        compiler_params=_cparams(dimension_semantics=("parallel", "parallel", "arbitrary")),
    )(*args)


def _rows(name, body, ins, outs, *, tt, consts=(), accs=()):
    in_specs, args = [], []
    first = ins[0][0] if isinstance(ins[0], tuple) else ins[0]
    t = first.shape[-2]
    tt = min(tt, t)
    for x in ins:
        if isinstance(x, tuple):
            arr, bs, im = x
            in_specs.append(pl.BlockSpec(bs, im))
            args.append(arr)
        else:
            in_specs.append(_row_spec(x.shape, tt))
            args.append(x)
    for c in consts:
        in_specs.append(pl.BlockSpec(c.shape, lambda i, _n=c.ndim: (0,) * _n))
        args.append(c)
    out_specs = [_row_spec(o.shape, tt) for o in outs]
    out_specs += [pl.BlockSpec(a.shape, lambda i, _n=len(a.shape): (0,) * _n) for a in accs]
    res = pl.pallas_call(
        body, name=name, grid=(t // tt,),
        in_specs=in_specs, out_specs=out_specs, out_shape=list(outs) + list(accs),
        compiler_params=_cparams(dimension_semantics=("arbitrary",)),
    )(*args)
    return res


def _row_spec(shape, tt):
    if len(shape) == 2:
        return pl.BlockSpec((tt, shape[1]), lambda i: (i, 0))
    return pl.BlockSpec((shape[0], tt, shape[2]), lambda i: (0, i, 0))


def _sds(shape, dtype=F32):
    return jax.ShapeDtypeStruct(tuple(shape), dtype)


def _acc(ref, val):
    i = pl.program_id(0)

    @pl.when(i == 0)
    def _():
        ref[...] = val

    @pl.when(i > 0)
    def _():
        ref[...] += val


def _rms(x, g):
    return x * lax.rsqrt(jnp.mean(x * x, axis=-1, keepdims=True) + NORM_EPS) * g


def _silu(x):
    return x / (1.0 + jnp.exp(-x))


def _softplus(x):
    return jnp.maximum(x, 0.0) + jnp.log(1.0 + jnp.exp(-jnp.abs(x)))


def _sigmoid(x):
    return 1.0 / (1.0 + jnp.exp(-x))


def _rmsnorm_fwd(name, x, g, tt=512):
    def body(x_ref, g_ref, h_ref):
        h_ref[...] = _rms(x_ref[...], g_ref[...]).astype(BF16)

    return _rows(name, body, [x], [_sds(x.shape, BF16)], tt=tt, consts=[g])[0]


def _rmsnorm_bwd(name, x, g, dh, dres, tt=512):
    def body(x_ref, dh_ref, dres_ref, g_ref, dx_ref, dg_ref):
        _, vjp = jax.vjp(_rms, x_ref[...], g_ref[...])
        dx, dg = vjp(dh_ref[...])
        dx_ref[...] = dx + dres_ref[...]
        _acc(dg_ref, dg)

    return _rows(name, body, [x, dh, dres], [_sds(x.shape)], tt=tt, consts=[g], accs=[_sds(g.shape)])


def _ffn_fwd(x, norm_g, w3, w_down):
    t, d = x.shape
    ns, _, cs = w3.shape
    half = ns // 2
    w2 = w3.reshape(ns * d, cs)
    h = _rmsnorm_fwd("ffn_norm", x, norm_g)
    tm = _pick(t, (512, 256, 128))
    nm = t // tm

    def gate_up(h_ref, wg_ref, wu_ref, g_ref, u_ref, a_ref):
        hv = h_ref[...]
        g = _dot(hv, wg_ref[...])
        u = _dot(hv, wu_ref[...])
        g_ref[...] = g
        u_ref[...] = u
        a_ref[...] = (_silu(g) * u).astype(BF16)

    hid = pl.BlockSpec((tm, cs), lambda j, i: (j * nm + i, 0))
    g, u, act = pl.pallas_call(
        gate_up, name="ffn_gate_up", grid=(half, nm),
        in_specs=[pl.BlockSpec((tm, d), lambda j, i: (i, 0)), pl.BlockSpec((d, cs), lambda j, i: (j, 0)),
                  pl.BlockSpec((d, cs), lambda j, i: (j + half, 0))],
        out_specs=[hid, hid, hid], out_shape=[_sds((half * t, cs)), _sds((half * t, cs)), _sds((half * t, cs), BF16)],
        compiler_params=_cparams(dimension_semantics=("parallel", "arbitrary")),
    )(h, w2, w2)
    y = _mm_raw("ffn_down", act, w_down, ta=False, tb=False, out_dtype=F32, add=x, grid=(nm, 1, half), out_shape=(t, d),
                a_block=(tm, cs), a_map=lambda i, j, k: (k * nm + i, 0), b_block=(cs, d), b_map=lambda i, j, k: (k, 0),
                o_block=(tm, d), o_map=lambda i, j, k: (i, 0))
    return y, (x, h, g, u, act)


def _ffn_bwd(saved, dy, norm_g, w3, w_down):
    x, h, g, u, act = saved
    t, d = x.shape
    ns, _, cs = w3.shape
    half = ns // 2
    w2 = w3.reshape(ns * d, cs)
    tm = _pick(t, (512, 256, 128))
    nm = t // tm
    tk = _pick(t, (1024, 512, 256, 128))
    nk = t // tk
    d_wdown = _mm_raw("ffn_down_wgrad", act, dy, ta=True, tb=False, out_dtype=F32, add=None, grid=(half, 1, nk),
                      out_shape=(half * cs, d), a_block=(tk, cs), a_map=lambda i, j, k: (i * nk + k, 0),
                      b_block=(tk, d), b_map=lambda i, j, k: (k, 0), o_block=(cs, d), o_map=lambda i, j, k: (i, 0))
    def down_dgrad(dy_ref, wd_ref, g_ref, u_ref, dg_ref, du_ref):
        da = _dot_nt(dy_ref[...].astype(BF16), wd_ref[...])
        gv, uv = g_ref[...], u_ref[...]
        s = _sigmoid(gv)
        dg_ref[...] = (da * uv * s * (1.0 + gv * (1.0 - s))).astype(BF16)
        du_ref[...] = (da * gv * s).astype(BF16)

    hid = pl.BlockSpec((tm, cs), lambda j, i: (j * nm + i, 0))
    dg, du = pl.pallas_call(
        down_dgrad, name="ffn_down_dgrad", grid=(half, nm),
        in_specs=[pl.BlockSpec((tm, d), lambda j, i: (i, 0)), pl.BlockSpec((cs, d), lambda j, i: (j, 0)), hid, hid],
        out_specs=[hid, hid], out_shape=[_sds((half * t, cs), BF16), _sds((half * t, cs), BF16)],
        compiler_params=_cparams(dimension_semantics=("parallel", "arbitrary")),
    )(dy, w_down, g, u)

    def wgrad(name, dd):
        return _mm_raw(name, h, dd, ta=True, tb=False, out_dtype=F32, add=None, grid=(1, half, nk), out_shape=(half * d, cs),
                       a_block=(tk, d), a_map=lambda i, j, k: (k, 0), b_block=(tk, cs), b_map=lambda i, j, k: (j * nk + k, 0),
                       o_block=(d, cs), o_map=lambda i, j, k: (j, 0))

    def dgrad(name, dd, off, add):
        return _mm_raw(name, dd, w2, ta=False, tb=True, out_dtype=F32, add=add, grid=(nm, 1, half), out_shape=(t, d),
                       a_block=(tm, cs), a_map=lambda i, j, k: (k * nm + i, 0), b_block=(d, cs), b_map=lambda i, j, k: (k + off, 0),
                       o_block=(tm, d), o_map=lambda i, j, k: (i, 0))

    d_w3 = jnp.concatenate([wgrad("ffn_gate_wgrad", dg), wgrad("ffn_up_wgrad", du)], axis=0).reshape(ns, d, cs)
    dh = dgrad("ffn_up_dgrad", du, half, dgrad("ffn_gate_dgrad", dg, 0, None))
    dx, dgain = _rmsnorm_bwd("ffn_norm_bwd", x, norm_g, dh, dy)
    return dx, dgain, d_w3, d_wdown


def _dot_nt(a, b):
    return lax.dot_general(a, b, (((1,), (1,)), ((), ())), preferred_element_type=F32)


def _dot_tn(a, b):
    return lax.dot_general(a, b, (((0,), (0,)), ((), ())), preferred_element_type=F32)


def _dot(a, b):
    return jnp.dot(a, b, preferred_element_type=F32)


def _dot_split(x, m):
    hi = x.astype(BF16)
    lo = (x - hi.astype(F32)).astype(BF16)
    return _dot(hi, m) + _dot(lo, m)


def _log_sigmoid(z):
    return jnp.minimum(z, 0.0) - jnp.log(1.0 + jnp.exp(-jnp.abs(z)))


def _heads_in(ref, h, width=HEAD):
    return ref[:, h * width:(h + 1) * width]


def _sb_qk(q, k, gq, gk):
    return _rms(q, gq) * (HEAD ** -0.5), _rms(k, gk)


def _sb_prep_fwd(qkv, gq, gk):
    t = qkv.shape[0]

    def body(x_ref, gq_ref, gk_ref, q_ref, k_ref, v_ref):
        for h in range(N_HEADS):
            q, k = _sb_qk(_heads_in(x_ref, h), _heads_in(x_ref, N_HEADS + h), gq_ref[...], gk_ref[...])
            q_ref[h] = q.astype(BF16)
            k_ref[h] = k.astype(BF16)
            v_ref[h] = _heads_in(x_ref, 2 * N_HEADS + h).astype(BF16)

    hm = _sds((N_HEADS, t, HEAD), BF16)
    return _rows("sb_prep", body, [qkv], [hm, hm, hm], tt=256, consts=[gq, gk])


def _sb_prep_bwd(qkv, gq, gk, dq, dk, dv):
    def body(x_ref, dq_ref, dk_ref, dv_ref, gq_ref, gk_ref, dx_ref, dgq_ref, dgk_ref):
        dgq = jnp.zeros(gq_ref.shape, F32)
        dgk = jnp.zeros(gk_ref.shape, F32)
        for h in range(N_HEADS):
            _, vjp = jax.vjp(_sb_qk, _heads_in(x_ref, h), _heads_in(x_ref, N_HEADS + h), gq_ref[...], gk_ref[...])
            a, b, c, d = vjp((dq_ref[h], dk_ref[h]))
            dx_ref[:, h * HEAD:(h + 1) * HEAD] = a.astype(BF16)
            dx_ref[:, (N_HEADS + h) * HEAD:(N_HEADS + h + 1) * HEAD] = b.astype(BF16)
            dx_ref[:, (2 * N_HEADS + h) * HEAD:(2 * N_HEADS + h + 1) * HEAD] = dv_ref[h].astype(BF16)
            dgq, dgk = dgq + c, dgk + d
        _acc(dgq_ref, dgq)
        _acc(dgk_ref, dgk)

    return _rows("sb_prep_bwd", body, [qkv, dq, dk, dv], [_sds(qkv.shape, BF16)], tt=256, consts=[gq, gk],
                 accs=[_sds(gq.shape), _sds(gk.shape)])


def _tri_masks():
    rows = lax.broadcasted_iota(jnp.int32, (ATT_BLOCK, ATT_BLOCK), 0)
    cols = lax.broadcasted_iota(jnp.int32, (ATT_BLOCK, ATT_BLOCK), 1)
    return rows, cols


def _q_block(t):
    return min(ATT_Q, t)


def _key_order(bq, qb):
    rows = lax.broadcasted_iota(jnp.int32, (bq, ATT_BLOCK), 0)
    cols = lax.broadcasted_iota(jnp.int32, (bq, ATT_BLOCK), 1)
    return rows - cols + qb * bq


def _sb_attn_fwd(q, k, v):
    nh, t, _ = q.shape
    bq = _q_block(t)
    per = bq // ATT_BLOCK

    def kern(q_ref, k_ref, v_ref, o_ref):
        qb = pl.program_id(1)
        qv = q_ref[0]
        rows, cols = _tri_masks()
        after = (rows > cols).astype(BF16)
        order = _key_order(bq, qb)
        nkb = (qb + 1) * per

        def body(i, carry):
            o_acc, run = carry
            kb = nkb - 1 - i
            off = pl.multiple_of(kb * ATT_BLOCK, ATT_BLOCK)
            kv = k_ref[0, pl.ds(off, ATT_BLOCK), :]
            vv = v_ref[0, pl.ds(off, ATT_BLOCK), :]
            z = _dot_nt(qv, kv)
            past = order > kb * ATT_BLOCK
            lsz = _log_sigmoid(z)
            lsn = jnp.where(past, lsz - z, 0.0)
            la = _dot_split(lsn, after) + run
            a = jnp.where(past, jnp.exp(lsz + la), 0.0)
            o_acc = o_acc + _dot(a.astype(BF16), vv)
            run = run + jnp.sum(lsn, axis=1, keepdims=True)
            return o_acc, run

        o, _ = lax.fori_loop(0, nkb, body, (jnp.zeros((bq, HEAD), F32), jnp.zeros((bq, 1), F32)))
        o_ref[...] = o

    return pl.pallas_call(
        kern, name="sb_attn_fwd", grid=(nh, t // bq),
        in_specs=[pl.BlockSpec((1, bq, HEAD), lambda h, i: (h, i, 0)),
                  pl.BlockSpec((1, t, HEAD), lambda h, i: (h, 0, 0)),
                  pl.BlockSpec((1, t, HEAD), lambda h, i: (h, 0, 0))],
        out_specs=pl.BlockSpec((bq, HEAD), lambda h, i: (i, h)),
        out_shape=_sds((t, nh * HEAD)),
        compiler_params=_cparams(dimension_semantics=("parallel", "arbitrary")),
    )(q, k, v)


def _sb_attn_bwd(q, k, v, do):
    nh, t, _ = q.shape
    bq = _q_block(t)
    per = bq // ATT_BLOCK

    def kern(q_ref, k_ref, v_ref, do_ref, dq_ref, dk_ref, dv_ref, g_s, ls_s):
        qb = pl.program_id(1)

        @pl.when(qb == 0)
        def _():
            dk_ref[...] = jnp.zeros(dk_ref.shape, F32)
            dv_ref[...] = jnp.zeros(dv_ref.shape, F32)

        qv = q_ref[0]
        dob = do_ref[...].astype(BF16)
        rows, cols = _tri_masks()
        after = (rows > cols).astype(BF16)
        before = (rows < cols).astype(BF16)
        order = _key_order(bq, qb)
        nkb = (qb + 1) * per

        def sweep_left(i, run):
            kb = nkb - 1 - i
            off = pl.multiple_of(kb * ATT_BLOCK, ATT_BLOCK)
            kv = k_ref[0, pl.ds(off, ATT_BLOCK), :]
            vv = v_ref[0, pl.ds(off, ATT_BLOCK), :]
            z = _dot_nt(qv, kv)
            past = order > kb * ATT_BLOCK
            lsz = _log_sigmoid(z)
            lsn = jnp.where(past, lsz - z, 0.0)
            la = _dot_split(lsn, after) + run
            a = jnp.where(past, jnp.exp(lsz + la), 0.0)
            g_s[kb] = _dot_nt(dob, vv) * a
            ls_s[kb] = lsz
            dv_ref[0, pl.ds(off, ATT_BLOCK), :] += _dot_tn(a.astype(BF16), dob)
            return run + jnp.sum(lsn, axis=1, keepdims=True)

        zero = jnp.zeros((bq, 1), F32)
        lax.fori_loop(0, nkb, sweep_left, zero)

        def sweep_right(kb, carry):
            dq_acc, run_g = carry
            off = pl.multiple_of(kb * ATT_BLOCK, ATT_BLOCK)
            kv = k_ref[0, pl.ds(off, ATT_BLOCK), :]
            g = g_s[kb]
            sg = jnp.exp(ls_s[kb])
            past = order > kb * ATT_BLOCK
            dls = run_g + _dot_split(g, before)
            dzb = jnp.where(past, g * (1.0 - sg) - dls * sg, 0.0).astype(BF16)
            dk_ref[0, pl.ds(off, ATT_BLOCK), :] += _dot_tn(dzb, qv)
            return dq_acc + _dot(dzb, kv), run_g + jnp.sum(g, axis=1, keepdims=True)

        dq, _ = lax.fori_loop(0, nkb, sweep_right, (jnp.zeros((bq, HEAD), F32), zero))
        dq_ref[0] = dq

    hm = _sds((nh, t, HEAD))
    full = pl.BlockSpec((1, t, HEAD), lambda h, i: (h, 0, 0))
    tok = pl.BlockSpec((bq, HEAD), lambda h, i: (i, h))
    nkb_max = t // ATT_BLOCK
    return pl.pallas_call(
        kern, name="sb_attn_bwd", grid=(nh, t // bq),
        in_specs=[pl.BlockSpec((1, bq, HEAD), lambda h, i: (h, i, 0)), full, full, tok],
        out_specs=[pl.BlockSpec((1, bq, HEAD), lambda h, i: (h, i, 0)), full, full],
        out_shape=[hm, hm, hm],
        scratch_shapes=[pltpu.VMEM((nkb_max, bq, ATT_BLOCK), F32), pltpu.VMEM((nkb_max, bq, ATT_BLOCK), F32)],
        compiler_params=_cparams(dimension_semantics=("parallel", "arbitrary")),
    )(q, k, v, do)


def _sb_fwd(x, h, w_qkv, gq, gk, w_out):
    qkv = _mm("sb_qkv", h, w_qkv)
    q, k, v = _sb_prep_fwd(qkv, gq, gk)
    o = _sb_attn_fwd(q, k, v)
    y = _mm("sb_out", o, w_out, add=x)
    return y, (h, qkv, q, k, v, o)


def _sb_bwd(saved, dy, w_qkv, gq, gk, w_out):
    h, qkv, q, k, v, o = saved
    d_wout = _mm("sb_out_wgrad", o, dy, ta=True)
    do = _mm("sb_out_dgrad", dy, w_out, tb=True)
    dq, dk, dv = _sb_attn_bwd(q, k, v, do)
    dqkv, dgq, dgk = _sb_prep_bwd(qkv, gq, gk, dq, dk, dv)
    d_wqkv = _mm("sb_qkv_wgrad", h, dqkv, ta=True)
    dh = _mm("sb_qkv_dgrad", dqkv, w_qkv, tb=True)
    return dh, d_wqkv, dgq, dgk, d_wout


DN_QKV = 3 * N_HEADS * HEAD
DN_PROJ = DN_QKV + N_HEADS * HEAD + LANES
DN_CONV = 4
HALO = 8
CONV_COLS = 512


def _dn_conv_fwd(proj, conv_w, tt=256):
    t = proj.shape[0]
    tt = min(tt, t)

    def body(u_ref, prev_ref, w_ref, c_ref):
        i = pl.program_id(0)
        for cc in range(DN_QKV // CONV_COLS):
            cs = slice(cc * CONV_COLS, (cc + 1) * CONV_COLS)
            cur = u_ref[:, cs]
            prev = jnp.where(i > 0, prev_ref[:, cs], 0.0)
            ext = jnp.concatenate([prev, cur], axis=0)
            y = cur * w_ref[DN_CONV - 1:DN_CONV, cs]
            for j in range(DN_CONV - 1):
                y = y + pltpu.roll(ext, DN_CONV - 1 - j, 0)[HALO:] * w_ref[j:j + 1, cs]
            c_ref[:, cs] = y

    return _rows("dn_conv", body,
                 [(proj, (tt, DN_QKV), lambda i: (i, 0)),
                  (proj, (HALO, DN_QKV), lambda i: (jnp.maximum(i * (tt // HALO) - 1, 0), 0))],
                 [_sds((t, DN_QKV))], tt=tt, consts=[conv_w])[0]


def _dn_conv_bwd(proj, conv_w, dc, dz, dab, tt=256):
    t = proj.shape[0]
    tt = min(tt, t)
    nblk = t // tt

    def body(u_ref, prev_ref, dc_ref, next_ref, dz_ref, dab_ref, w_ref, dp_ref, dw_ref):
        i = pl.program_id(0)
        dws = []
        for cc in range(DN_QKV // CONV_COLS):
            cs = slice(cc * CONV_COLS, (cc + 1) * CONV_COLS)
            cur = u_ref[:, cs]
            prev = jnp.where(i > 0, prev_ref[:, cs], 0.0)
            ext_u = jnp.concatenate([prev, cur], axis=0)
            d = dc_ref[:, cs]
            nxt = jnp.where(i < nblk - 1, next_ref[:, cs], 0.0)
            ext_d = jnp.concatenate([d, nxt], axis=0)
            du = d * w_ref[DN_CONV - 1:DN_CONV, cs]
            rows = [jnp.sum(d * cur, axis=0, keepdims=True)]
            for j in range(DN_CONV - 2, -1, -1):
                sh = DN_CONV - 1 - j
                du = du + pltpu.roll(ext_d, tt + HALO - sh, 0)[:tt] * w_ref[j:j + 1, cs]
                rows.insert(0, jnp.sum(d * pltpu.roll(ext_u, sh, 0)[HALO:], axis=0, keepdims=True))
            dp_ref[:, cs] = du.astype(BF16)
            dws.append(jnp.concatenate(rows, axis=0))
        dp_ref[:, DN_QKV:DN_QKV + N_HEADS * HEAD] = dz_ref[...].astype(BF16)
        dp_ref[:, DN_QKV + N_HEADS * HEAD:] = dab_ref[...].astype(BF16)
        _acc(dw_ref, jnp.concatenate(dws, axis=1))

    return _rows("dn_conv_bwd", body,
                 [(proj, (tt, DN_QKV), lambda i: (i, 0)),
                  (proj, (HALO, DN_QKV), lambda i: (jnp.maximum(i * (tt // HALO) - 1, 0), 0)),
                  dc,
                  (dc, (HALO, DN_QKV), lambda i: (jnp.minimum((i + 1) * (tt // HALO), t // HALO - 1), 0)),
                  dz, dab],
                 [_sds((t, DN_PROJ), BF16)], tt=tt, consts=[conv_w], accs=[_sds(conv_w.shape)])


def _l2n(x):
    return x * lax.rsqrt(jnp.sum(x * x, axis=-1, keepdims=True) + NORM_EPS)


def _dn_qkv(cq, ck, cv):
    return _l2n(_silu(cq)) * (HEAD ** -0.5), _l2n(_silu(ck)), _silu(cv)


def _dn_gates(ab, a_log, dt_bias):
    lane = lax.broadcasted_iota(jnp.int32, ab.shape, 1)
    g = -jnp.exp(a_log) * _softplus(ab + dt_bias)
    return jnp.where(lane < N_HEADS, g, jnp.where(lane < 2 * N_HEADS, _sigmoid(ab), 0.0))


def _ab_spec(tt):
    return (tt, LANES), lambda i: (i, DN_PROJ // LANES - 1)


def _dn_prep_fwd(c, proj, a_log, dt_bias, tt=256):
    t = c.shape[0]
    tt = min(tt, t)

    def body(c_ref, ab_ref, al_ref, dt_ref, q_ref, k_ref, v_ref, g_ref):
        for h in range(N_HEADS):
            q_ref[h], k_ref[h], v_ref[h] = _dn_qkv(_heads_in(c_ref, h), _heads_in(c_ref, N_HEADS + h), _heads_in(c_ref, 2 * N_HEADS + h))
        g_ref[...] = _dn_gates(ab_ref[...], al_ref[...], dt_ref[...])

    hm = _sds((N_HEADS, t, HEAD))
    return _rows("dn_prep", body, [c, (proj,) + _ab_spec(tt)], [hm, hm, hm, _sds((t, LANES))], tt=tt, consts=[a_log, dt_bias])


def _dn_prep_bwd(c, proj, a_log, dt_bias, dq, dk, dv, dgates, tt=256):
    t = c.shape[0]
    tt = min(tt, t)

    def body(c_ref, ab_ref, dq_ref, dk_ref, dv_ref, dg_ref, al_ref, dt_ref, dc_ref, dab_ref, dal_ref, ddt_ref):
        for h in range(N_HEADS):
            _, vjp = jax.vjp(_dn_qkv, _heads_in(c_ref, h), _heads_in(c_ref, N_HEADS + h), _heads_in(c_ref, 2 * N_HEADS + h))
            a, b, d = vjp((dq_ref[h], dk_ref[h], dv_ref[h]))
            dc_ref[:, h * HEAD:(h + 1) * HEAD] = a
            dc_ref[:, (N_HEADS + h) * HEAD:(N_HEADS + h + 1) * HEAD] = b
            dc_ref[:, (2 * N_HEADS + h) * HEAD:(2 * N_HEADS + h + 1) * HEAD] = d
        _, vjp = jax.vjp(_dn_gates, ab_ref[...], al_ref[...], dt_ref[...])
        dab, dal, ddt = vjp(dg_ref[...])
        dab_ref[...] = dab
        _acc(dal_ref, dal)
        _acc(ddt_ref, ddt)

    return _rows("dn_prep_bwd", body, [c, (proj,) + _ab_spec(tt), dq, dk, dv, dgates], [_sds(c.shape), _sds((t, LANES))],
                 tt=tt, consts=[a_log, dt_bias], accs=[_sds(a_log.shape), _sds(dt_bias.shape)])


def _bdot(a, b, prec=None):
    return lax.dot_general(a, b, (((2,), (1,)), ((0,), (0,))), precision=prec, preferred_element_type=F32)


def _bdot_nt(a, b, prec=None):
    return lax.dot_general(a, b, (((2,), (2,)), ((0,), (0,))), precision=prec, preferred_element_type=F32)


def _bdot_tn(a, b, prec=None):
    return lax.dot_general(a, b, (((1,), (1,)), ((0,), (0,))), precision=prec, preferred_element_type=F32)


def _inv_raw(low):
    c = low.shape[-1]
    r = lax.broadcasted_iota(jnp.int32, (c, c), 0)
    s = lax.broadcasted_iota(jnp.int32, (c, c), 1)
    m = jnp.where(r == s, 1.0, 0.0) - low
    p = _bdot(low, low, F32X3)
    n_fac = int(math.log2(c)) - 1
    for i in range(n_fac):
        m = m + _bdot(m, p, F32X3)
        if i < n_fac - 1:
            p = _bdot(p, p, F32X3)
    return m


@jax.custom_vjp
def _inv_unit_lower(low):
    return _inv_raw(low)


def _inv_fwd(low):
    m = _inv_raw(low)
    return m, m


def _inv_bwd(m, dm):
    return (-_bdot_nt(_bdot_tn(m, dm, F32X3), m, F32X3),)


_inv_unit_lower.defvjp(_inv_fwd, _inv_bwd)


def _dn_chunk(q, k, v, gates, s):
    nh, c, _ = q.shape
    lane = lax.broadcasted_iota(jnp.int32, gates.shape, 1)
    def column(j):
        return jnp.sum(jnp.where(lane == j, gates, 0.0), axis=1, keepdims=True)[None]

    g_col = jnp.concatenate([column(h) for h in range(nh)], axis=0)
    b_col = jnp.concatenate([column(h + nh) for h in range(nh)], axis=0)
    r = lax.broadcasted_iota(jnp.int32, (c, c), 0)
    cc = lax.broadcasted_iota(jnp.int32, (c, c), 1)
    causal, strict = r >= cc, r > cc
    incl = jnp.broadcast_to(jnp.where(causal, 1.0, 0.0), (nh, c, c))
    upper = jnp.broadcast_to(jnp.where(r <= cc, 1.0, 0.0), (nh, c, c))
    gb = jnp.broadcast_to(g_col, (nh, c, LANES))
    gbc = jnp.broadcast_to(g_col, (nh, c, c))
    gc = _bdot(incl, gb, F32X3)
    gc_r = _bdot(incl, gbc, F32X3)
    gc_c = _bdot_tn(gbc, upper, F32X3)
    decay = jnp.where(causal, jnp.exp(jnp.where(causal, gc_r - gc_c, 0.0)), 0.0)
    kb = k * b_col
    low = jnp.where(strict, _bdot_nt(kb, k) * decay, 0.0)
    m = _inv_unit_lower(low)
    egc = jnp.exp(gc)
    u = _bdot(m, v * b_col, F32X3)
    w = _bdot(m, kb * egc, F32X3)
    attn = _bdot_nt(q, k) * decay
    gl = jnp.sum(gb, axis=1, keepdims=True)
    v_new = u - _bdot(w, s)
    o = _bdot(q * egc, s) + _bdot(attn, v_new)
    s_new = s * jnp.exp(gl) + _bdot_tn(k * jnp.exp(gl - gc), v_new)
    return o, s_new


def _dn_chunks_fwd(q, k, v, gates):
    nh, t, _ = q.shape
    n = t // DN_CHUNK

    def kern(q_ref, k_ref, v_ref, g_ref, o_ref, sin_ref, s_scr):
        @pl.when(pl.program_id(0) == 0)
        def _():
            s_scr[...] = jnp.zeros(s_scr.shape, F32)

        s = s_scr[...]
        sin_ref[0] = s
        o_ref[...], s_scr[...] = _dn_chunk(q_ref[...], k_ref[...], v_ref[...], g_ref[...], s)

    blk = pl.BlockSpec((nh, DN_CHUNK, HEAD), lambda i: (0, i, 0))
    return pl.pallas_call(
        kern, name="dn_chunks_fwd", grid=(n,),
        in_specs=[blk, blk, blk, pl.BlockSpec((DN_CHUNK, LANES), lambda i: (i, 0))],
        out_specs=[blk, pl.BlockSpec((1, nh, HEAD, HEAD), lambda i: (i, 0, 0, 0))],
        out_shape=[_sds((nh, t, HEAD)), _sds((n, nh, HEAD, HEAD))],
        scratch_shapes=[pltpu.VMEM((nh, HEAD, HEAD), F32)],
        compiler_params=_cparams(dimension_semantics=("arbitrary",)),
    )(q, k, v, gates)


def _dn_chunks_bwd(q, k, v, gates, s_in, do):
    nh, t, _ = q.shape
    n = t // DN_CHUNK

    def kern(q_ref, k_ref, v_ref, g_ref, sin_ref, do_ref, dq_ref, dk_ref, dv_ref, dg_ref, ds_scr):
        @pl.when(pl.program_id(0) == 0)
        def _():
            ds_scr[...] = jnp.zeros(ds_scr.shape, F32)

        _, vjp = jax.vjp(_dn_chunk, q_ref[...], k_ref[...], v_ref[...], g_ref[...], sin_ref[0])
        dq_ref[...], dk_ref[...], dv_ref[...], dg_ref[...], ds_scr[...] = vjp((do_ref[...], ds_scr[...]))

    blk = pl.BlockSpec((nh, DN_CHUNK, HEAD), lambda i: (0, n - 1 - i, 0))
    gblk = pl.BlockSpec((DN_CHUNK, LANES), lambda i: (n - 1 - i, 0))
    hm = _sds((nh, t, HEAD))
    return pl.pallas_call(
        kern, name="dn_chunks_bwd", grid=(n,),
        in_specs=[blk, blk, blk, gblk, pl.BlockSpec((1, nh, HEAD, HEAD), lambda i: (n - 1 - i, 0, 0, 0)), blk],
        out_specs=[blk, blk, blk, gblk],
        out_shape=[hm, hm, hm, _sds((t, LANES))],
        scratch_shapes=[pltpu.VMEM((nh, HEAD, HEAD), F32)],
        compiler_params=_cparams(dimension_semantics=("arbitrary",)),
    )(q, k, v, gates, s_in, do)


def _dn_gate_out(o, z, g):
    return _rms(o, g) * _silu(z)


def _z_spec(tt):
    return (tt, N_HEADS * HEAD), lambda i: (i, DN_QKV // (N_HEADS * HEAD))


def _dn_post_fwd(o, proj, out_norm, tt=256):
    t = o.shape[1]
    tt = min(tt, t)

    def body(o_ref, z_ref, g_ref, y_ref):
        for h in range(N_HEADS):
            y_ref[:, h * HEAD:(h + 1) * HEAD] = _dn_gate_out(o_ref[h], _heads_in(z_ref, h), g_ref[...]).astype(BF16)

    return _rows("dn_post", body, [o, (proj,) + _z_spec(tt)], [_sds((t, N_HEADS * HEAD), BF16)], tt=tt, consts=[out_norm])[0]


def _dn_post_bwd(o, proj, out_norm, dy, tt=256):
    t = o.shape[1]
    tt = min(tt, t)

    def body(o_ref, z_ref, dy_ref, g_ref, do_ref, dz_ref, dg_ref):
        dg = jnp.zeros(g_ref.shape, F32)
        for h in range(N_HEADS):
            _, vjp = jax.vjp(_dn_gate_out, o_ref[h], _heads_in(z_ref, h), g_ref[...])
            a, b, d = vjp(_heads_in(dy_ref, h))
            do_ref[h] = a
            dz_ref[:, h * HEAD:(h + 1) * HEAD] = b
            dg = dg + d
        _acc(dg_ref, dg)

    return _rows("dn_post_bwd", body, [o, (proj,) + _z_spec(tt), dy], [_sds(o.shape), _sds((t, N_HEADS * HEAD))], tt=tt,
                 consts=[out_norm], accs=[_sds(out_norm.shape)])


def _dn_fwd(x, h, w_in, conv_w, a_log, dt_bias, out_norm, w_out):
    proj = _mm("dn_in", h, w_in)
    c = _dn_conv_fwd(proj, conv_w)
    q, k, v, gates = _dn_prep_fwd(c, proj, a_log, dt_bias)
    o, s_in = _dn_chunks_fwd(q, k, v, gates)
    on = _dn_post_fwd(o, proj, out_norm)
    y = _mm("dn_out", on, w_out, add=x)
    return y, (h, proj, c, q, k, v, gates, o, s_in, on)


def _dn_bwd(saved, dy, w_in, conv_w, a_log, dt_bias, out_norm, w_out):
    h, proj, c, q, k, v, gates, o, s_in, on = saved
    d_wout = _mm("dn_out_wgrad", on, dy, ta=True)
    don = _mm("dn_out_dgrad", dy, w_out, tb=True)
    do, dz, d_out_norm = _dn_post_bwd(o, proj, out_norm, don)
    dq, dk, dv, dgates = _dn_chunks_bwd(q, k, v, gates, s_in, do)
    dc, dab, d_a_log, d_dt_bias = _dn_prep_bwd(c, proj, a_log, dt_bias, dq, dk, dv, dgates)
    dproj, d_conv_w = _dn_conv_bwd(proj, conv_w, dc, dz, dab)
    d_win = _mm("dn_in_wgrad", h, dproj, ta=True)
    dh = _mm("dn_in_dgrad", dproj, w_in, tb=True)
    return dh, d_win, d_conv_w, d_a_log, d_dt_bias, d_out_norm, d_wout


MLA_SCALE = MLA_QK ** -0.5
MLA_C = 512


def _swap_raw(x):
    lane = lax.broadcasted_iota(jnp.int32, x.shape, 1)
    half = MLA_ROPE // 2
    y = jnp.where(lane < half, pltpu.roll(x, LANES - half, 1), pltpu.roll(x, half, 1))
    return jnp.where(lane < MLA_ROPE, y, 0.0)


@jax.custom_vjp
def _swap_halves(x):
    return _swap_raw(x)


_swap_halves.defvjp(lambda x: (_swap_raw(x), None), lambda _, d: (_swap_raw(d),))


def _rms_rope(x, g, cos, sin):
    y = x * lax.rsqrt(jnp.sum(x * x, axis=-1, keepdims=True) * (1.0 / MLA_ROPE) + NORM_EPS) * g
    return y * cos + _swap_halves(y) * sin


def _mla_latent(cq, ckv, kr, gq, gkv, gkr, cos, sin):
    return _rms(cq, gq), _rms(ckv, gkv), _rms_rope(kr, gkr, cos, sin)


def _mla_prep1_fwd(c, gq, gkv, gkr, cos, sin):
    t = c.shape[0]

    def body(c_ref, cos_ref, sin_ref, gq_ref, gkv_ref, gkr_ref, cq_ref, ckv_ref, kr_ref):
        a, b, r = _mla_latent(c_ref[:, :256], c_ref[:, 256:384], c_ref[:, 384:], gq_ref[...], gkv_ref[...], gkr_ref[...],
                              cos_ref[...], sin_ref[...])
        cq_ref[...] = a.astype(BF16)
        ckv_ref[...] = b.astype(BF16)
        kr_ref[...] = r.astype(BF16)

    return _rows("mla_prep1", body, [c, cos, sin], [_sds((t, 256), BF16), _sds((t, HEAD), BF16), _sds((t, HEAD), BF16)],
                 tt=512, consts=[gq, gkv, gkr])


def _mla_prep1_bwd(c, gq, gkv, gkr, cos, sin, dcq, dckv, dkr_heads):
    def body(c_ref, cos_ref, sin_ref, dcq_ref, dckv_ref, dkr_ref, gq_ref, gkv_ref, gkr_ref, dc_ref, dgq_ref, dgkv_ref, dgkr_ref):
        dkr = dkr_ref[0]
        for h in range(1, N_HEADS):
            dkr = dkr + dkr_ref[h]
        _, vjp = jax.vjp(_mla_latent, c_ref[:, :256], c_ref[:, 256:384], c_ref[:, 384:], gq_ref[...], gkv_ref[...], gkr_ref[...],
                         cos_ref[...], sin_ref[...])
        a, b, r, d1, d2, d3, _, _ = vjp((dcq_ref[...], dckv_ref[...], dkr))
        dc_ref[:, :256] = a.astype(BF16)
        dc_ref[:, 256:384] = b.astype(BF16)
        dc_ref[:, 384:] = r.astype(BF16)
        _acc(dgq_ref, d1)
        _acc(dgkv_ref, d2)
        _acc(dgkr_ref, d3)

    return _rows("mla_prep1_bwd", body, [c, cos, sin, dcq, dckv, dkr_heads], [_sds(c.shape, BF16)], tt=512,
                 consts=[gq, gkv, gkr], accs=[_sds(gq.shape), _sds(gkv.shape), _sds(gkr.shape)])


def _mla_heads(qn, qr, kn, gqn, gqr, gkn, cos, sin):
    return _rms(qn, gqn) * MLA_SCALE, _rms_rope(qr, gqr, cos, sin) * MLA_SCALE, _rms(kn, gkn)


def _mla_prep2_fwd(qa, kv, gqn, gqr, gkn, cos, sin):
    t = qa.shape[0]

    def body(qa_ref, kv_ref, cos_ref, sin_ref, gqn_ref, gqr_ref, gkn_ref, qn_ref, qr_ref, kn_ref, v_ref):
        for h in range(N_HEADS):
            a, b, c = _mla_heads(_heads_in(qa_ref, h), _heads_in(qa_ref, N_HEADS + h), _heads_in(kv_ref, h),
                                 gqn_ref[...], gqr_ref[...], gkn_ref[...], cos_ref[...], sin_ref[...])
            qn_ref[h] = a.astype(BF16)
            qr_ref[h] = b.astype(BF16)
            kn_ref[h] = c.astype(BF16)
            v_ref[h] = _heads_in(kv_ref, N_HEADS + h).astype(BF16)

    hm = _sds((N_HEADS, t, HEAD), BF16)
    return _rows("mla_prep2", body, [qa, kv, cos, sin], [hm, hm, hm, hm], tt=256, consts=[gqn, gqr, gkn])


def _mla_prep2_bwd(qa, kv, gqn, gqr, gkn, cos, sin, dqn, dqr, dkn, dv):
    def body(qa_ref, kv_ref, cos_ref, sin_ref, dqn_ref, dqr_ref, dkn_ref, dv_ref, gqn_ref, gqr_ref, gkn_ref,
             dqa_ref, dkv_ref, d1_ref, d2_ref, d3_ref):
        d1 = jnp.zeros(gqn_ref.shape, F32)
        d2 = jnp.zeros(gqr_ref.shape, F32)
        d3 = jnp.zeros(gkn_ref.shape, F32)
        for h in range(N_HEADS):
            _, vjp = jax.vjp(_mla_heads, _heads_in(qa_ref, h), _heads_in(qa_ref, N_HEADS + h), _heads_in(kv_ref, h),
                             gqn_ref[...], gqr_ref[...], gkn_ref[...], cos_ref[...], sin_ref[...])
            a, b, c, e1, e2, e3, _, _ = vjp((dqn_ref[h], dqr_ref[h], dkn_ref[h]))
            dqa_ref[:, h * HEAD:(h + 1) * HEAD] = a.astype(BF16)
            dqa_ref[:, (N_HEADS + h) * HEAD:(N_HEADS + h + 1) * HEAD] = b.astype(BF16)
            dkv_ref[:, h * HEAD:(h + 1) * HEAD] = c.astype(BF16)
            dkv_ref[:, (N_HEADS + h) * HEAD:(N_HEADS + h + 1) * HEAD] = dv_ref[h].astype(BF16)
            d1, d2, d3 = d1 + e1, d2 + e2, d3 + e3
        _acc(d1_ref, d1)
        _acc(d2_ref, d2)
        _acc(d3_ref, d3)

    return _rows("mla_prep2_bwd", body, [qa, kv, cos, sin, dqn, dqr, dkn, dv], [_sds(qa.shape, BF16), _sds(kv.shape, BF16)],
                 tt=256, consts=[gqn, gqr, gkn], accs=[_sds(gqn.shape), _sds(gqr.shape), _sds(gkn.shape)])


def _mla_attn_fwd(qn, qr, kn, kr, v):
    nh, t, _ = qn.shape
    bq = _q_block(t)
    per = bq // ATT_BLOCK

    def kern(qn_ref, qr_ref, kn_ref, kr_ref, v_ref, o_ref, lse_ref):
        qb = pl.program_id(1)
        qv = jnp.concatenate([qn_ref[0], qr_ref[0]], axis=1)
        order = _key_order(bq, qb)

        def body(kb, carry):
            acc, m, l = carry
            off = pl.multiple_of(kb * ATT_BLOCK, ATT_BLOCK)
            kv = jnp.concatenate([kn_ref[0, pl.ds(off, ATT_BLOCK), :], kr_ref[pl.ds(off, ATT_BLOCK), :]], axis=1)
            s = jnp.where(order >= kb * ATT_BLOCK, _dot_nt(qv, kv), -jnp.inf)
            m_new = jnp.maximum(m, jnp.max(s, axis=1, keepdims=True))
            alpha = jnp.exp(m - m_new)
            p = jnp.exp(s - m_new)
            acc = acc * alpha + _dot(p.astype(BF16), v_ref[0, pl.ds(off, ATT_BLOCK), :])
            return acc, m_new, l * alpha + jnp.sum(p, axis=1, keepdims=True)

        init = (jnp.zeros((bq, HEAD), F32), jnp.full((bq, 1), -jnp.inf, F32), jnp.zeros((bq, 1), F32))
        acc, m, l = lax.fori_loop(0, (qb + 1) * per, body, init)
        o_ref[...] = acc / l
        lse_ref[...] = jnp.broadcast_to(m + jnp.log(l), (bq, HEAD))

    blk = pl.BlockSpec((1, bq, HEAD), lambda h, i: (h, i, 0))
    full = pl.BlockSpec((1, t, HEAD), lambda h, i: (h, 0, 0))
    tok = pl.BlockSpec((bq, HEAD), lambda h, i: (i, h))
    return pl.pallas_call(
        kern, name="mla_attn_fwd", grid=(nh, t // bq),
        in_specs=[blk, blk, full, pl.BlockSpec((t, HEAD), lambda h, i: (0, 0)), full],
        out_specs=[tok, tok], out_shape=[_sds((t, nh * HEAD)), _sds((t, nh * HEAD))],
        compiler_params=_cparams(dimension_semantics=("parallel", "arbitrary")),
    )(qn, qr, kn, kr, v)


def _mla_attn_bwd(qn, qr, kn, kr, v, o, lse, do):
    nh, t, _ = qn.shape
    bq = _q_block(t)
    per = bq // ATT_BLOCK

    def kern(qn_ref, qr_ref, kn_ref, kr_ref, v_ref, o_ref, lse_ref, do_ref, dqn_ref, dqr_ref, dkn_ref, dkr_ref, dv_ref):
        qb = pl.program_id(1)

        @pl.when(qb == 0)
        def _():
            dkn_ref[...] = jnp.zeros(dkn_ref.shape, F32)
            dkr_ref[...] = jnp.zeros(dkr_ref.shape, F32)
            dv_ref[...] = jnp.zeros(dv_ref.shape, F32)

        qv = jnp.concatenate([qn_ref[0], qr_ref[0]], axis=1)
        dov = do_ref[...]
        dob = dov.astype(BF16)
        delta = jnp.sum(dov * o_ref[...], axis=1, keepdims=True)
        lse_col = lse_ref[:, :1]
        order = _key_order(bq, qb)

        def body(kb, dq):
            off = pl.multiple_of(kb * ATT_BLOCK, ATT_BLOCK)
            kv = jnp.concatenate([kn_ref[0, pl.ds(off, ATT_BLOCK), :], kr_ref[pl.ds(off, ATT_BLOCK), :]], axis=1)
            vv = v_ref[0, pl.ds(off, ATT_BLOCK), :]
            p = jnp.where(order >= kb * ATT_BLOCK, jnp.exp(_dot_nt(qv, kv) - lse_col), 0.0)
            ds = (p * (_dot_nt(dob, vv) - delta)).astype(BF16)
            dk = _dot_tn(ds, qv)
            dkn_ref[0, pl.ds(off, ATT_BLOCK), :] += dk[:, :HEAD]
            dkr_ref[0, pl.ds(off, ATT_BLOCK), :] += dk[:, HEAD:]
            dv_ref[0, pl.ds(off, ATT_BLOCK), :] += _dot_tn(p.astype(BF16), dob)
            return dq + _dot(ds, kv)

        dq = lax.fori_loop(0, (qb + 1) * per, body, jnp.zeros((bq, 2 * HEAD), F32))
        dqn_ref[0] = dq[:, :HEAD]
        dqr_ref[0] = dq[:, HEAD:]

    hm = _sds((nh, t, HEAD))
    blk = pl.BlockSpec((1, bq, HEAD), lambda h, i: (h, i, 0))
    full = pl.BlockSpec((1, t, HEAD), lambda h, i: (h, 0, 0))
    tok = pl.BlockSpec((bq, HEAD), lambda h, i: (i, h))
    return pl.pallas_call(
        kern, name="mla_attn_bwd", grid=(nh, t // bq),
        in_specs=[blk, blk, full, pl.BlockSpec((t, HEAD), lambda h, i: (0, 0)), full, tok, tok, tok],
        out_specs=[blk, blk, full, full, full], out_shape=[hm, hm, hm, hm, hm],
        compiler_params=_cparams(dimension_semantics=("parallel", "arbitrary")),
    )(qn, qr, kn, kr, v, o, lse, do)


def _rope_tables(t):
    inv_freq = ROPE_THETA ** (-jnp.arange(0, MLA_ROPE, 2, dtype=F32) / MLA_ROPE)
    ang = jnp.arange(t, dtype=F32)[:, None] * inv_freq[None, :]
    c, s = jnp.cos(ang), jnp.sin(ang)
    pad = ((0, 0), (0, LANES - MLA_ROPE))
    return jnp.pad(jnp.concatenate([c, c], axis=1), pad), jnp.pad(jnp.concatenate([-s, s], axis=1), pad)


def _pad_lanes(v, n=LANES):
    return jnp.pad(v, (0, n - v.shape[0])).reshape(1, n)


def _mla_layout(w_down, w_uq, w_ukv):
    w_down_p = jnp.pad(w_down, ((0, 0), (0, MLA_C - w_down.shape[1])))
    uq = w_uq.reshape(w_uq.shape[0], N_HEADS, MLA_QK)
    rope = jnp.pad(uq[:, :, HEAD:], ((0, 0), (0, 0), (0, LANES - MLA_ROPE)))
    w_uq_p = jnp.concatenate([uq[:, :, :HEAD].reshape(-1, N_HEADS * HEAD), rope.reshape(-1, N_HEADS * LANES)], axis=1)
    ukv = w_ukv.reshape(w_ukv.shape[0], N_HEADS, 2 * HEAD)
    w_ukv_p = jnp.concatenate([ukv[:, :, :HEAD].reshape(-1, N_HEADS * HEAD), ukv[:, :, HEAD:].reshape(-1, N_HEADS * HEAD)], axis=1)
    return w_down_p, w_uq_p, w_ukv_p


def _mla_unlayout(d_down_p, d_uq_p, d_ukv_p):
    d_down = d_down_p[:, :256 + HEAD + MLA_ROPE]
    nope = d_uq_p[:, :N_HEADS * HEAD].reshape(-1, N_HEADS, HEAD)
    rope = d_uq_p[:, N_HEADS * HEAD:].reshape(-1, N_HEADS, LANES)[:, :, :MLA_ROPE]
    d_uq = jnp.concatenate([nope, rope], axis=2).reshape(-1, N_HEADS * MLA_QK)
    kn = d_ukv_p[:, :N_HEADS * HEAD].reshape(-1, N_HEADS, HEAD)
    vv = d_ukv_p[:, N_HEADS * HEAD:].reshape(-1, N_HEADS, HEAD)
    d_ukv = jnp.concatenate([kn, vv], axis=2).reshape(-1, N_HEADS * 2 * HEAD)
    return d_down, d_uq, d_ukv


def _mla_weight_shapes():
    return (_sds((1024, MLA_C), BF16), _sds((1, 256)), _sds((1, HEAD)), _sds((256, 2048), BF16), _sds((HEAD, 2048), BF16),
            _sds((1, HEAD)), _sds((1, HEAD)), _sds((1, HEAD)), _sds((1, HEAD)), _sds((1024, 1024), BF16),
            _sds((4096, HEAD)), _sds((4096, HEAD)))


def _mla_fwd(x, h, w_down, gq, gkv, w_uq, w_ukv, gqn, gqr, gkn, gkr, w_out, cos, sin):
    c = _mm("mla_down", h, w_down)
    cq, ckv, kr = _mla_prep1_fwd(c, gq, gkv, gkr, cos, sin)
    qa = _mm("mla_uq", cq, w_uq)
    kv = _mm("mla_ukv", ckv, w_ukv)
    qn, qr, kn, v = _mla_prep2_fwd(qa, kv, gqn, gqr, gkn, cos, sin)
    o, lse = _mla_attn_fwd(qn, qr, kn, kr, v)
    y = _mm("mla_out", o, w_out, add=x)
    return y, (h, c, cq, ckv, kr, qa, kv, qn, qr, kn, v, o, lse)


def _mla_bwd(saved, dy, w_down, gq, gkv, w_uq, w_ukv, gqn, gqr, gkn, gkr, w_out, cos, sin):
    h, c, cq, ckv, kr, qa, kv, qn, qr, kn, v, o, lse = saved
    d_wout = _mm("mla_out_wgrad", o, dy, ta=True)
    do = _mm("mla_out_dgrad", dy, w_out, tb=True)
    dqn, dqr, dkn, dkr, dv = _mla_attn_bwd(qn, qr, kn, kr, v, o, lse, do)
    dqa, dkv, dgqn, dgqr, dgkn = _mla_prep2_bwd(qa, kv, gqn, gqr, gkn, cos, sin, dqn, dqr, dkn, dv)
    d_wuq = _mm("mla_uq_wgrad", cq, dqa, ta=True)
    d_wukv = _mm("mla_ukv_wgrad", ckv, dkv, ta=True)
    dcq = _mm("mla_uq_dgrad", dqa, w_uq, tb=True)
    dckv = _mm("mla_ukv_dgrad", dkv, w_ukv, tb=True)
    dc, dgq, dgkv, dgkr = _mla_prep1_bwd(c, gq, gkv, gkr, cos, sin, dcq, dckv, dkr)
    d_wdown = _mm("mla_down_wgrad", h, dc, ta=True)
    dh = _mm("mla_down_dgrad", dc, w_down, tb=True)
    return dh, d_wdown, dgq, dgkv, d_wuq, d_wukv, dgqn, dgqr, dgkn, dgkr, d_wout


def _loss_head(y, target):
    d = y.shape[1]

    def body(y_ref, t_ref, dy_ref, l_ref):
        err = y_ref[...] - t_ref[...]
        dy_ref[...] = err * (1.0 / d)
        part = 0.5 * jnp.sum(jnp.sum(err * err, axis=1, keepdims=True) * (1.0 / d), axis=0, keepdims=True)
        _acc(l_ref, jnp.broadcast_to(part, (1, LANES)))

    return _rows("loss_head", body, [y, target], [_sds(y.shape)], tt=512, accs=[_sds((1, LANES))])


MESH_ID = pl.DeviceIdType.MESH
HBM_SPEC = pl.BlockSpec(memory_space=pltpu.HBM)


def _all_gather(name, x):
    m_per, n = x.shape

    def body(x_ref, out_ref, send_sems, recv_sems, local_sem):
        x, y, c = lax.axis_index("x"), lax.axis_index("y"), lax.axis_index("c")
        me, sibling = (x, y, c), (x, y, 1 - c)
        chips = [(1 - x, y), (x, 1 - y), (1 - x, 1 - y)]

        def rows(px, py, pc):
            return out_ref.at[pl.ds((4 * px + 2 * py + pc) * m_per, m_per), :]

        def copy(k, block, to, src=None):
            return pltpu.make_async_remote_copy(
                src_ref=rows(*block) if src is None else src, dst_ref=rows(*block),
                send_sem=send_sems.at[k], recv_sem=recv_sems.at[k], device_id=to, device_id_type=MESH_ID)

        mine = pltpu.make_async_copy(x_ref, rows(*me), local_sem)
        mine.start()
        first = [copy(0, me, sibling, src=x_ref)]
        first += [copy(1 + j, me, (*chip, c), src=x_ref) for j, chip in enumerate(chips)]
        for cp in first:
            cp.start()
        passed = [copy(4 + j, (*chip, c), sibling) for j, chip in enumerate(chips)]
        for j, chip in enumerate(chips):
            copy(1 + j, (*chip, c), me).wait_recv()
            passed[j].start()
        copy(0, sibling, me).wait_recv()
        for j, chip in enumerate(chips):
            copy(4 + j, (*chip, 1 - c), me).wait_recv()
        for cp in first + passed:
            cp.wait_send()
        mine.wait()

    return pl.pallas_call(
        body, name=name,
        out_shape=jax.ShapeDtypeStruct((N_DEV * m_per, n), x.dtype),
        in_specs=[HBM_SPEC], out_specs=HBM_SPEC,
        scratch_shapes=[pltpu.SemaphoreType.DMA((7,)), pltpu.SemaphoreType.DMA((7,)), pltpu.SemaphoreType.DMA],
    )(x)


def _all_gather_groups(name, xs):
    ng = len(xs)

    def body(*refs):
        x_refs, out_refs = refs[:ng], refs[ng:2 * ng]
        send_sems, recv_sems, local_sems = refs[2 * ng:]
        x, y, c = lax.axis_index("x"), lax.axis_index("y"), lax.axis_index("c")
        me, sibling = (x, y, c), (x, y, 1 - c)
        chips = [(1 - x, y), (x, 1 - y), (1 - x, 1 - y)]

        def copy(g, k, block, to, src=None):
            px, py, pc = block
            dst = out_refs[g].at[4 * px + 2 * py + pc]
            return pltpu.make_async_remote_copy(
                src_ref=dst if src is None else src, dst_ref=dst,
                send_sem=send_sems.at[g, k], recv_sem=recv_sems.at[g, k], device_id=to, device_id_type=MESH_ID)

        mine = [pltpu.make_async_copy(x_refs[g], out_refs[g].at[4 * x + 2 * y + c], local_sems.at[g]) for g in range(ng)]
        for cp in mine:
            cp.start()
        first = []
        for g in range(ng):
            first.append(copy(g, 0, me, sibling, src=x_refs[g]))
            first += [copy(g, 1 + j, me, (*chip, c), src=x_refs[g]) for j, chip in enumerate(chips)]
        for cp in first:
            cp.start()
        passed = []
        for j, chip in enumerate(chips):
            for g in range(ng):
                copy(g, 1 + j, (*chip, c), me).wait_recv()
                passed.append(copy(g, 4 + j, (*chip, c), sibling))
                passed[-1].start()
        for g in range(ng):
            copy(g, 0, sibling, me).wait_recv()
            for j, chip in enumerate(chips):
                copy(g, 4 + j, (*chip, 1 - c), me).wait_recv()
        for cp in first + passed:
            cp.wait_send()
        for cp in mine:
            cp.wait()

    return pl.pallas_call(
        body, name=name,
        out_shape=[jax.ShapeDtypeStruct((N_DEV,) + x.shape, x.dtype) for x in xs],
        in_specs=[HBM_SPEC] * ng, out_specs=[HBM_SPEC] * ng,
        scratch_shapes=[pltpu.SemaphoreType.DMA((ng, 7)), pltpu.SemaphoreType.DMA((ng, 7)), pltpu.SemaphoreType.DMA((ng,))],
    )(*xs)


def _rs_cores(gs):
    ng = len(gs)

    def body(*refs):
        g_refs, recv_refs = refs[:ng], refs[ng:2 * ng]
        send_sems, recv_sems = refs[2 * ng:]
        x, y, c = lax.axis_index("x"), lax.axis_index("y"), lax.axis_index("c")
        copies = [pltpu.make_async_remote_copy(
            src_ref=g_refs[g].at[:, 1 - c], dst_ref=recv_refs[g], send_sem=send_sems.at[g], recv_sem=recv_sems.at[g],
            device_id=(x, y, 1 - c), device_id_type=MESH_ID) for g in range(ng)]
        for cp in copies:
            cp.start()
        for cp in copies:
            cp.wait()

    return pl.pallas_call(
        body, name="rs_cores",
        out_shape=[jax.ShapeDtypeStruct((g.shape[0],) + g.shape[2:], g.dtype) for g in gs],
        in_specs=[HBM_SPEC] * ng, out_specs=[HBM_SPEC] * ng,
        scratch_shapes=[pltpu.SemaphoreType.DMA((ng,)), pltpu.SemaphoreType.DMA((ng,))],
    )(*gs)


def _rs_add(g, recv, core):
    nchip, _, r, c_ = g.shape
    tr = _pick(r, (512, 384, 256, 128))

    def body(core_ref, g_ref, r_ref, p_ref, pb_ref):
        s = g_ref[0, 0] + r_ref[0]
        p_ref[0] = s
        pb_ref[0] = s.astype(BF16)

    return pl.pallas_call(
        body, name="rs_add",
        grid_spec=pltpu.PrefetchScalarGridSpec(
            num_scalar_prefetch=1, grid=(nchip, r // tr),
            in_specs=[pl.BlockSpec((1, 1, tr, c_), lambda k, i, core_ref: (k, core_ref[0], i, 0)),
                      pl.BlockSpec((1, tr, c_), lambda k, i, core_ref: (k, i, 0))],
            out_specs=[pl.BlockSpec((1, tr, c_), lambda k, i, core_ref: (k, i, 0)),
                       pl.BlockSpec((1, tr, c_), lambda k, i, core_ref: (k, i, 0))]),
        out_shape=[_sds((nchip, r, c_)), _sds((nchip, r, c_), BF16)],
        compiler_params=_cparams(dimension_semantics=("arbitrary", "arbitrary")),
    )(core, g, recv)


def _rs_chips(pbs):
    ng = len(pbs)

    def body(*refs):
        p_refs, recv_refs = refs[:ng], refs[ng:2 * ng]
        send_sems, recv_sems, local_sems = refs[2 * ng:]
        x, y, c = lax.axis_index("x"), lax.axis_index("y"), lax.axis_index("c")
        my = 2 * x + y
        chips = [(1 - x, y), (x, 1 - y), (1 - x, 1 - y)]
        local = [pltpu.make_async_copy(p_refs[g].at[my], recv_refs[g].at[my], local_sems.at[g]) for g in range(ng)]
        for cp in local:
            cp.start()

        def copy(g, k, src_chip, dst_chip, to):
            return pltpu.make_async_remote_copy(
                src_ref=p_refs[g].at[src_chip], dst_ref=recv_refs[g].at[dst_chip],
                send_sem=send_sems.at[g, k], recv_sem=recv_sems.at[g, k], device_id=to, device_id_type=MESH_ID)

        sends = [copy(g, k, 2 * px + py, my, (px, py, c)) for g in range(ng) for k, (px, py) in enumerate(chips)]
        for cp in sends:
            cp.start()
        for g in range(ng):
            for k, (px, py) in enumerate(chips):
                copy(g, k, my, 2 * px + py, (px, py, c)).wait_recv()
        for cp in sends:
            cp.wait_send()
        for cp in local:
            cp.wait()

    return pl.pallas_call(
        body, name="rs_chips", out_shape=[jax.ShapeDtypeStruct(p.shape, p.dtype) for p in pbs],
        in_specs=[HBM_SPEC] * ng, out_specs=[HBM_SPEC] * ng,
        scratch_shapes=[pltpu.SemaphoreType.DMA((ng, 3)), pltpu.SemaphoreType.DMA((ng, 3)), pltpu.SemaphoreType.DMA((ng,))],
    )(*pbs)


def _cols_from_shards(w, width):
    ns, r, cs = w.shape
    tr = _pick(r, (256, 128))

    def body(w_ref, o_ref):
        parts = [w_ref[j] for j in range(ns)]
        if width > ns * cs:
            parts.append(jnp.zeros((tr, width - ns * cs), w.dtype))
        o_ref[...] = jnp.concatenate(parts, axis=1)

    return pl.pallas_call(
        body, name="cols_from_shards", grid=(r // tr,),
        in_specs=[pl.BlockSpec((ns, tr, cs), lambda i: (0, i, 0))], out_specs=pl.BlockSpec((tr, width), lambda i: (i, 0)),
        out_shape=jax.ShapeDtypeStruct((r, width), w.dtype), compiler_params=_cparams(dimension_semantics=("arbitrary",)),
    )(w)


def _shards_from_cols(g, cs):
    r, width = g.shape
    tr = _pick(r, (256, 128))

    def body(g_ref, o_ref):
        for j in range(N_DEV):
            o_ref[j] = g_ref[:, j * cs:(j + 1) * cs]

    return pl.pallas_call(
        body, name="shards_from_cols", grid=(r // tr,),
        in_specs=[pl.BlockSpec((tr, width), lambda i: (i, 0))], out_specs=pl.BlockSpec((N_DEV, tr, cs), lambda i: (0, i, 0)),
        out_shape=jax.ShapeDtypeStruct((N_DEV, r, cs), g.dtype), compiler_params=_cparams(dimension_semantics=("arbitrary",)),
    )(g)


def _adam(w, g, m, v):
    m = ADAM_B1 * m + (1.0 - ADAM_B1) * g
    v = ADAM_B2 * v + (1.0 - ADAM_B2) * (g * g)
    m_hat = m / (1.0 - ADAM_B1 ** ADAM_STEP)
    v_hat = v / (1.0 - ADAM_B2 ** ADAM_STEP)
    return -ADAM_LR * (m_hat / (jnp.sqrt(v_hat) + ADAM_EPS) + ADAM_WD * w), m, v


def _sum_adam(p, recv, chip, w, m, v):
    nchip, r, c_ = recv.shape
    tr = _pick(r, (512, 384, 256, 128))

    def body(chip_ref, p_ref, r_ref, w_ref, m_ref, v_ref, g_ref, d_ref, mo_ref, vo_ref):
        my = chip_ref[0]
        g = jnp.where(my == 0, p_ref[0], r_ref[0].astype(F32))
        for k in range(1, nchip):
            g = g + jnp.where(my == k, p_ref[0], r_ref[k].astype(F32))
        g_ref[...] = g
        d_ref[...], mo_ref[...], vo_ref[...] = _adam(w_ref[...], g, m_ref[...], v_ref[...])

    row = pl.BlockSpec((tr, c_), lambda i, chip_ref: (i, 0))
    return pl.pallas_call(
        body, name="sum_adam",
        grid_spec=pltpu.PrefetchScalarGridSpec(
            num_scalar_prefetch=1, grid=(r // tr,),
            in_specs=[pl.BlockSpec((1, tr, c_), lambda i, chip_ref: (chip_ref[0], i, 0)),
                      pl.BlockSpec((nchip, tr, c_), lambda i, chip_ref: (0, i, 0)), row, row, row],
            out_specs=[row, row, row, row]),
        out_shape=[_sds((r, c_))] * 4,
        compiler_params=_cparams(dimension_semantics=("arbitrary",)),
    )(chip, p, recv, w, m, v)


def _sum_devices(gathered):
    m_all, n = gathered.shape
    m_per = m_all // N_DEV

    def body(x_ref, o_ref):
        s = x_ref[0:m_per, :]
        for j in range(1, N_DEV):
            s = s + x_ref[j * m_per:(j + 1) * m_per, :]
        o_ref[...] = s

    return pl.pallas_call(body, name="sum_devices", out_shape=_sds((m_per, n)), compiler_params=_cparams())(gathered)


def _adam_small(w, g, m, v):
    def body(w_ref, g_ref, m_ref, v_ref, d_ref, mo_ref, vo_ref):
        d_ref[...], mo_ref[...], vo_ref[...] = _adam(w_ref[...], g_ref[...], m_ref[...], v_ref[...])

    return pl.pallas_call(body, name="adam_small", out_shape=[_sds(w.shape)] * 3, compiler_params=_cparams())(w, g, m, v)


N_LAYERS = 4
_MIXER = ("dn", "sb", "mla")
_MIXER_PARAMS = {
    "dn": ("dn_w_in", "dn_conv_w", "dn_a_log", "dn_dt_bias", "dn_out_norm", "dn_w_out"),
    "sb": ("sb_w_qkv", "sb_q_norm", "sb_k_norm", "sb_w_out"),
    "mla": ("mla_w_down", "mla_q_a_norm", "mla_kv_a_norm", "mla_w_uq", "mla_w_ukv", "mla_q_nope_norm", "mla_q_rope_norm",
            "mla_k_nope_norm", "mla_k_rope_norm", "mla_w_out"),
}
_BIG_AXIS = {"dn_w_in": 1, "dn_w_out": 0, "sb_w_qkv": 1, "sb_w_out": 0, "mla_w_down": 0, "mla_w_uq": 1, "mla_w_ukv": 1,
             "mla_w_out": 0, "ffn_w_gate_up": 1, "ffn_w_down": 0}


def _weight_names():
    names = []
    for i in range(N_LAYERS):
        p = "l%d_" % i
        names += [p + "mix_norm"] + [p + n for n in _MIXER_PARAMS[_MIXER[i % 3]]] + [p + "ffn_norm", p + "ffn_w_gate_up", p + "ffn_w_down"]
    return names


WEIGHTS = _weight_names()
BIG = [n for n in WEIGHTS if n[3:] in _BIG_AXIS]
SMALL = [n for n in WEIGHTS if n[3:] not in _BIG_AXIS]
CONV = [n for n in SMALL if n.endswith("conv_w")]


def _ceil_to(n, k):
    return -(-n // k) * k


def _pack(arrs, cols, row_mult):
    parts = []
    for a in arrs:
        f = a.reshape(-1)
        parts.append(jnp.pad(f, (0, _ceil_to(f.shape[0], cols) - f.shape[0])))
    flat = jnp.concatenate(parts)
    rows = _ceil_to(flat.shape[0] // cols, row_mult)
    return jnp.pad(flat, (0, rows * cols - flat.shape[0])).reshape(rows, cols)


def _unpack(buf, shapes):
    cols = buf.shape[-1]
    out, r0 = [], 0
    for s in shapes:
        n = math.prod(s)
        nr = _ceil_to(n, cols) // cols
        out.append(buf[r0:r0 + nr].reshape(-1)[:n].reshape(s))
        r0 += nr
    return out


def _groups():
    by = {"gu": (704, []), "row": (1024, []), "dn_in": (514, []), "sb_qkv": (384, []), "mla": (512, [])}
    key = {"ffn_w_gate_up": "gu", "dn_w_in": "dn_in", "sb_w_qkv": "sb_qkv", "mla_w_down": "mla", "mla_w_uq": "mla", "mla_w_ukv": "mla"}
    for n in BIG:
        by[key.get(n[3:], "row")][1].append(n)
    return list(by.values())


GROUPS = _groups()


def _stack_group(grp, get):
    width, names = grp
    return jnp.concatenate([jnp.pad(get(n), ((0, 0), (0, width - get(n).shape[1]))) for n in names], axis=0)


def _unstack_group(grp, buf, shape_of):
    out, r0 = [], 0
    for n in grp[1]:
        rs, cs = shape_of(n)
        out.append(buf[..., r0:r0 + rs, :cs])
        r0 += rs
    return out


def kernel(x, l0_mix_norm, l0_dn_w_in, l0_dn_conv_w, l0_dn_a_log, l0_dn_dt_bias, l0_dn_out_norm, l0_dn_w_out, l0_ffn_norm, l0_ffn_w_gate_up, l0_ffn_w_down, l1_mix_norm, l1_sb_w_qkv, l1_sb_q_norm, l1_sb_k_norm, l1_sb_w_out, l1_ffn_norm, l1_ffn_w_gate_up, l1_ffn_w_down, l2_mix_norm, l2_mla_w_down, l2_mla_q_a_norm, l2_mla_kv_a_norm, l2_mla_w_uq, l2_mla_w_ukv, l2_mla_q_nope_norm, l2_mla_q_rope_norm, l2_mla_k_nope_norm, l2_mla_k_rope_norm, l2_mla_w_out, l2_ffn_norm, l2_ffn_w_gate_up, l2_ffn_w_down, l3_mix_norm, l3_dn_w_in, l3_dn_conv_w, l3_dn_a_log, l3_dn_dt_bias, l3_dn_out_norm, l3_dn_w_out, l3_ffn_norm, l3_ffn_w_gate_up, l3_ffn_w_down, loss_target, m_l0_mix_norm, m_l0_dn_w_in, m_l0_dn_conv_w, m_l0_dn_a_log, m_l0_dn_dt_bias, m_l0_dn_out_norm, m_l0_dn_w_out, m_l0_ffn_norm, m_l0_ffn_w_gate_up, m_l0_ffn_w_down, m_l1_mix_norm, m_l1_sb_w_qkv, m_l1_sb_q_norm, m_l1_sb_k_norm, m_l1_sb_w_out, m_l1_ffn_norm, m_l1_ffn_w_gate_up, m_l1_ffn_w_down, m_l2_mix_norm, m_l2_mla_w_down, m_l2_mla_q_a_norm, m_l2_mla_kv_a_norm, m_l2_mla_w_uq, m_l2_mla_w_ukv, m_l2_mla_q_nope_norm, m_l2_mla_q_rope_norm, m_l2_mla_k_nope_norm, m_l2_mla_k_rope_norm, m_l2_mla_w_out, m_l2_ffn_norm, m_l2_ffn_w_gate_up, m_l2_ffn_w_down, m_l3_mix_norm, m_l3_dn_w_in, m_l3_dn_conv_w, m_l3_dn_a_log, m_l3_dn_dt_bias, m_l3_dn_out_norm, m_l3_dn_w_out, m_l3_ffn_norm, m_l3_ffn_w_gate_up, m_l3_ffn_w_down, v_l0_mix_norm, v_l0_dn_w_in, v_l0_dn_conv_w, v_l0_dn_a_log, v_l0_dn_dt_bias, v_l0_dn_out_norm, v_l0_dn_w_out, v_l0_ffn_norm, v_l0_ffn_w_gate_up, v_l0_ffn_w_down, v_l1_mix_norm, v_l1_sb_w_qkv, v_l1_sb_q_norm, v_l1_sb_k_norm, v_l1_sb_w_out, v_l1_ffn_norm, v_l1_ffn_w_gate_up, v_l1_ffn_w_down, v_l2_mix_norm, v_l2_mla_w_down, v_l2_mla_q_a_norm, v_l2_mla_kv_a_norm, v_l2_mla_w_uq, v_l2_mla_w_ukv, v_l2_mla_q_nope_norm, v_l2_mla_q_rope_norm, v_l2_mla_k_nope_norm, v_l2_mla_k_rope_norm, v_l2_mla_w_out, v_l2_ffn_norm, v_l2_ffn_w_gate_up, v_l2_ffn_w_down, v_l3_mix_norm, v_l3_dn_w_in, v_l3_dn_conv_w, v_l3_dn_a_log, v_l3_dn_dt_bias, v_l3_dn_out_norm, v_l3_dn_w_out, v_l3_ffn_norm, v_l3_ffn_w_gate_up, v_l3_ffn_w_down):
    a = dict(locals())
    return _train_step(a)


def _train_step(a):
    mx, my, mc = lax.axis_index("x"), lax.axis_index("y"), lax.axis_index("c")
    dev = 4 * mx + 2 * my + mc
    core = jnp.reshape(mc, (1,)).astype(jnp.int32)
    chip = jnp.reshape(2 * mx + my, (1,)).astype(jnp.int32)
    t, d = a["x"].shape[1], a["x"].shape[2]
    xs = a["x"].reshape(t, d)
    target = a["loss_target"].reshape(t, d)

    gathered = _all_gather_groups("gather_weights", [_stack_group(grp, lambda n: a[n].astype(BF16)) for grp in GROUPS])
    full = {}
    for grp, buf in zip(GROUPS, gathered):
        for n, shards in zip(grp[1], _unstack_group(grp, buf, lambda n: a[n].shape)):
            kind = n[3:]
            if kind == "ffn_w_gate_up":
                full[n] = shards
            elif _BIG_AXIS[kind] == 0:
                full[n] = shards.reshape(N_DEV * shards.shape[1], shards.shape[2])
            else:
                width = DN_PROJ if kind == "dn_w_in" else N_DEV * shards.shape[2]
                full[n] = _cols_from_shards(shards, width)
    conv_pack = _pack([a[n] for n in CONV], LANES, 8)
    conv_all = _all_gather("gather_conv", conv_pack).reshape(N_DEV, conv_pack.shape[0], LANES)
    for n, parts in zip(CONV, zip(*[_unpack(conv_all[j], [a[n].shape for n in CONV]) for j in range(N_DEV)])):
        full[n] = jnp.concatenate(parts, axis=1)

    def vec(n):
        return a[n].reshape(1, -1)

    cos, sin = _rope_tables(t)
    layer_args = []
    for i in range(N_LAYERS):
        p = "l%d_" % i
        kind = _MIXER[i % 3]
        if kind == "dn":
            args = (full[p + "dn_w_in"], full[p + "dn_conv_w"], _pad_lanes(a[p + "dn_a_log"]), _pad_lanes(a[p + "dn_dt_bias"]),
                    vec(p + "dn_out_norm"), full[p + "dn_w_out"])
        elif kind == "sb":
            args = (full[p + "sb_w_qkv"], vec(p + "sb_q_norm"), vec(p + "sb_k_norm"), full[p + "sb_w_out"])
        else:
            w_down, w_uq, w_ukv = _mla_layout(full[p + "mla_w_down"], full[p + "mla_w_uq"], full[p + "mla_w_ukv"])
            args = (w_down, vec(p + "mla_q_a_norm"), vec(p + "mla_kv_a_norm"), w_uq, w_ukv, vec(p + "mla_q_nope_norm"),
                    _pad_lanes(a[p + "mla_q_rope_norm"]), vec(p + "mla_k_nope_norm"), _pad_lanes(a[p + "mla_k_rope_norm"]),
                    full[p + "mla_w_out"], cos, sin)
        layer_args.append((kind, args))

    fwd = {"dn": _dn_fwd, "sb": _sb_fwd, "mla": _mla_fwd}
    bwd = {"dn": _dn_bwd, "sb": _sb_bwd, "mla": _mla_bwd}
    saved = []
    for i, (kind, args) in enumerate(layer_args):
        p = "l%d_" % i
        h = _rmsnorm_fwd("mix_norm", xs, vec(p + "mix_norm"))
        x_mid, sv_mix = fwd[kind](xs, h, *args)
        x_out, sv_ffn = _ffn_fwd(x_mid, vec(p + "ffn_norm"), full[p + "ffn_w_gate_up"], full[p + "ffn_w_down"])
        saved.append((xs, sv_mix, sv_ffn))
        xs = x_out
    dy, loss_part = _loss_head(xs, target)

    grads = {}
    for i in reversed(range(N_LAYERS)):
        p = "l%d_" % i
        kind, args = layer_args[i]
        x_in, sv_mix, sv_ffn = saved[i]
        dx_mid, grads[p + "ffn_norm"], grads[p + "ffn_w_gate_up"], grads[p + "ffn_w_down"] = _ffn_bwd(
            sv_ffn, dy, vec(p + "ffn_norm"), full[p + "ffn_w_gate_up"], full[p + "ffn_w_down"])
        res = bwd[kind](sv_mix, dx_mid, *args)
        dh = res[0]
        if kind == "mla":
            res = list(res)
            res[1], res[4], res[5] = _mla_unlayout(res[1], res[4], res[5])
        for n, g in zip(_MIXER_PARAMS[kind], res[1:]):
            grads[p + n] = g
        dy, grads[p + "mix_norm"] = _rmsnorm_bwd("mix_norm_bwd", x_in, vec(p + "mix_norm"), dh, dx_mid)
    grad_x = dy.reshape(a["x"].shape)

    small_full_shapes = [full[n].shape if n in CONV else a[n].shape for n in SMALL]
    small_grads = []
    for n, s in zip(SMALL, small_full_shapes):
        g = grads[n].reshape(-1)
        small_grads.append(g[:math.prod(s)])
    small_pack = _pack(small_grads + [loss_part.reshape(-1)], LANES, 8)
    small_sum = _sum_devices(_all_gather("gather_small_grads", small_pack))
    small_red = _unpack(small_sum, small_full_shapes + [(LANES,)])
    loss = small_red[-1][0]
    g_small = {}
    for n, g in zip(SMALL, small_red[:-1]):
        if n in CONV:
            cs = a[n].shape[1]
            g = lax.dynamic_slice_in_dim(g, dev * cs, cs, axis=1)
        g_small[n] = g
    small_shapes = [a[n].shape for n in SMALL]
    packs = [_pack([src[n] for n in SMALL], LANES, 8) for src in
             ({n: a[n] for n in SMALL}, g_small, {n: a["m_" + n] for n in SMALL}, {n: a["v_" + n] for n in SMALL})]
    d_small, m_small, v_small = (_unpack(o, small_shapes) for o in _adam_small(*packs))

    def grad_shards(n):
        g, (rs, cs) = grads[n], a[n].shape
        if g.ndim == 3:
            return g
        if _BIG_AXIS[n[3:]] == 0:
            return g.reshape(N_DEV, rs, cs)
        return _shards_from_cols(g, cs)

    g_groups = []
    for grp in GROUPS:
        parts = [jnp.pad(grad_shards(n), ((0, 0), (0, 0), (0, grp[0] - a[n].shape[1]))) for n in grp[1]]
        g_all = jnp.concatenate(parts, axis=1)
        g_groups.append(g_all.reshape((N_DEV // 2, 2) + g_all.shape[1:]))
    from_sibling = _rs_cores(g_groups)
    parts = [_rs_add(g, r, core) for g, r in zip(g_groups, from_sibling)]
    from_chips = _rs_chips([pb for _, pb in parts])
    big_out = {}
    for grp, (part, _), recv in zip(GROUPS, parts, from_chips):
        packs = [_stack_group(grp, lambda n, pre=pre: a[pre + n]) for pre in ("", "m_", "v_")]
        outs = [_unstack_group(grp, o, lambda n: a[n].shape) for o in _sum_adam(part, recv, chip, *packs)]
        for i, n in enumerate(grp[1]):
            big_out[n] = [o[i] for o in outs]

    small_out = dict(zip(SMALL, zip([g_small[n] for n in SMALL], d_small, m_small, v_small)))

    def out(k):
        return [small_out[n][k] if n in small_out else big_out[n][k] for n in WEIGHTS]

    return (loss, grad_x, *out(0), *out(1), *out(2), *out(3))
```

```python
import math

import jax
import jax.numpy as jnp
from jax import lax
from jax.experimental import pallas as pl
from jax.experimental.pallas import tpu as pltpu

F32 = jnp.float32
BF16 = jnp.bfloat16
F32X3 = lax.Precision.HIGH

LANES = 128
N_DEV = 8
N_HEADS = 8
HEAD = 128
NORM_EPS = 1e-6
DN_CHUNK = 64
ATT_BLOCK = 512
ATT_Q = 512
MLA_ROPE = 64
MLA_QK = 192
ROPE_THETA = 10000.0
VMEM_LIMIT = 56 * 1024 * 1024

ADAM_LR = 0.001
ADAM_B1 = 0.9
ADAM_B2 = 0.999
ADAM_EPS = 1e-08
ADAM_WD = 0.01
ADAM_STEP = 10


def _cparams(**kw):
    return pltpu.CompilerParams(vmem_limit_bytes=VMEM_LIMIT, **kw)


def _pick(n, cands):
    for c in cands:
        if c <= n and n % c == 0:
            return c
    return n


def _mm(name, a, b, *, ta=False, tb=False, out_dtype=F32, add=None, tm=None, tn=None, tk=None):
    if ta:
        K, M = a.shape
    else:
        M, K = a.shape
    N = b.shape[0] if tb else b.shape[1]
    tm = tm or _pick(M, (1024, 512, 256, 128))
    tn = tn or _pick(N, (1024, 512, 384, 256, 128))
    tk = tk or _pick(K, (1024, 1408, 512, 384, 256, 128))
    return _mm_raw(
        name, a, b, ta=ta, tb=tb, out_dtype=out_dtype, add=add, grid=(M // tm, N // tn, K // tk), out_shape=(M, N),
        a_block=(tk, tm) if ta else (tm, tk), a_map=(lambda i, j, k: (k, i)) if ta else (lambda i, j, k: (i, k)),
        b_block=(tn, tk) if tb else (tk, tn), b_map=(lambda i, j, k: (j, k)) if tb else (lambda i, j, k: (k, j)),
        o_block=(tm, tn), o_map=lambda i, j, k: (i, j))


def _mm_raw(name, a, b, *, ta, tb, out_dtype, add, grid, out_shape, a_block, a_map, b_block, b_map, o_block, o_map):
    nk = grid[2]
    tm, tn = o_block
    dn = (((0 if ta else 1,), (1 if tb else 0,)), ((), ()))
    has_add = add is not None

    def kern(*refs):
        if has_add:
            a_ref, b_ref, add_ref, o_ref, acc_ref = refs
        else:
            a_ref, b_ref, o_ref, acc_ref = refs
        k = pl.program_id(2)
        part = lax.dot_general(a_ref[...].astype(BF16), b_ref[...].astype(BF16), dn, preferred_element_type=F32)

        @pl.when(k == 0)
        def _():
            acc_ref[...] = part

        @pl.when(k > 0)
        def _():
            acc_ref[...] += part

        @pl.when(k == nk - 1)
        def _():
            r = acc_ref[...]
            if has_add:
                r = r + add_ref[...]
            o_ref[...] = r.astype(out_dtype)

    in_specs = [pl.BlockSpec(a_block, a_map), pl.BlockSpec(b_block, b_map)]
    args = [a, b]
    if has_add:
        in_specs.append(pl.BlockSpec(o_block, o_map))
        args.append(add)
    return pl.pallas_call(
        kern, name=name,
        grid=grid,
        in_specs=in_specs,
        out_specs=pl.BlockSpec(o_block, o_map),
        out_shape=jax.ShapeDtypeStruct(out_shape, out_dtype),
        scratch_shapes=[pltpu.VMEM((tm, tn), F32)],
        compiler_params=_cparams(dimension_semantics=("parallel", "parallel", "arbitrary")),
    )(*args)


def _rows(name, body, ins, outs, *, tt, consts=(), accs=()):
    in_specs, args = [], []
    first = ins[0][0] if isinstance(ins[0], tuple) else ins[0]
    t = first.shape[-2]
    tt = min(tt, t)
    for x in ins:
        if isinstance(x, tuple):
            arr, bs, im = x
            in_specs.append(pl.BlockSpec(bs, im))
            args.append(arr)
        else:
            in_specs.append(_row_spec(x.shape, tt))
            args.append(x)
    for c in consts:
        in_specs.append(pl.BlockSpec(c.shape, lambda i, _n=c.ndim: (0,) * _n))
        args.append(c)
    out_specs = [_row_spec(o.shape, tt) for o in outs]
    out_specs += [pl.BlockSpec(a.shape, lambda i, _n=len(a.shape): (0,) * _n) for a in accs]
    res = pl.pallas_call(
        body, name=name, grid=(t // tt,),
        in_specs=in_specs, out_specs=out_specs, out_shape=list(outs) + list(accs),
        compiler_params=_cparams(dimension_semantics=("arbitrary",)),
    )(*args)
    return res


def _row_spec(shape, tt):
    if len(shape) == 2:
        return pl.BlockSpec((tt, shape[1]), lambda i: (i, 0))
    return pl.BlockSpec((shape[0], tt, shape[2]), lambda i: (0, i, 0))


def _sds(shape, dtype=F32):
    return jax.ShapeDtypeStruct(tuple(shape), dtype)


def _acc(ref, val):
    i = pl.program_id(0)

    @pl.when(i == 0)
    def _():
        ref[...] = val

    @pl.when(i > 0)
    def _():
        ref[...] += val


def _rms(x, g):
    return x * lax.rsqrt(jnp.mean(x * x, axis=-1, keepdims=True) + NORM_EPS) * g


def _silu(x):
    return x / (1.0 + jnp.exp(-x))


def _softplus(x):
    return jnp.maximum(x, 0.0) + jnp.log(1.0 + jnp.exp(-jnp.abs(x)))


def _sigmoid(x):
    return 1.0 / (1.0 + jnp.exp(-x))


def _rmsnorm_fwd(name, x, g, tt=512):
    def body(x_ref, g_ref, h_ref):
        h_ref[...] = _rms(x_ref[...], g_ref[...]).astype(BF16)

    return _rows(name, body, [x], [_sds(x.shape, BF16)], tt=tt, consts=[g])[0]


def _rmsnorm_bwd(name, x, g, dh, dres, tt=512):
    def body(x_ref, dh_ref, dres_ref, g_ref, dx_ref, dg_ref):
        _, vjp = jax.vjp(_rms, x_ref[...], g_ref[...])
        dx, dg = vjp(dh_ref[...])
        dx_ref[...] = dx + dres_ref[...]
        _acc(dg_ref, dg)

    return _rows(name, body, [x, dh, dres], [_sds(x.shape)], tt=tt, consts=[g], accs=[_sds(g.shape)])


def _ffn_fwd(x, norm_g, w3, w_down):
    t, d = x.shape
    ns, _, cs = w3.shape
    half = ns // 2
    w2 = w3.reshape(ns * d, cs)
    h = _rmsnorm_fwd("ffn_norm", x, norm_g)
    tm = _pick(t, (1024, 512, 256, 128))
    nm = t // tm

    def gate_up(h_ref, wg_ref, wu_ref, g_ref, u_ref, a_ref):
        hv = h_ref[...]
        g = _dot(hv, wg_ref[...])
        u = _dot(hv, wu_ref[...])
        g_ref[...] = g
        u_ref[...] = u
        a_ref[...] = (_silu(g) * u).astype(BF16)

    hid = pl.BlockSpec((tm, cs), lambda j, i: (j * nm + i, 0))
    g, u, act = pl.pallas_call(
        gate_up, name="ffn_gate_up", grid=(half, nm),
        in_specs=[pl.BlockSpec((tm, d), lambda j, i: (i, 0)), pl.BlockSpec((d, cs), lambda j, i: (j, 0)),
                  pl.BlockSpec((d, cs), lambda j, i: (j + half, 0))],
        out_specs=[hid, hid, hid], out_shape=[_sds((half * t, cs)), _sds((half * t, cs)), _sds((half * t, cs), BF16)],
        compiler_params=_cparams(dimension_semantics=("parallel", "arbitrary")),
    )(h, w2, w2)
    y = _mm_raw("ffn_down", act, w_down, ta=False, tb=False, out_dtype=F32, add=x, grid=(nm, 1, half), out_shape=(t, d),
                a_block=(tm, cs), a_map=lambda i, j, k: (k * nm + i, 0), b_block=(cs, d), b_map=lambda i, j, k: (k, 0),
                o_block=(tm, d), o_map=lambda i, j, k: (i, 0))
    return y, (x, h, g, u, act)


def _ffn_bwd(saved, dy, norm_g, w3, w_down):
    x, h, g, u, act = saved
    t, d = x.shape
    ns, _, cs = w3.shape
    half = ns // 2
    w2 = w3.reshape(ns * d, cs)
    tm = _pick(t, (1024, 512, 256, 128))
    nm = t // tm
    tk = _pick(t, (1024, 512, 256, 128))
    nk = t // tk
    d_wdown = _mm_raw("ffn_down_wgrad", act, dy, ta=True, tb=False, out_dtype=F32, add=None, grid=(half, 1, nk),
                      out_shape=(half * cs, d), a_block=(tk, cs), a_map=lambda i, j, k: (i * nk + k, 0),
                      b_block=(tk, d), b_map=lambda i, j, k: (k, 0), o_block=(cs, d), o_map=lambda i, j, k: (i, 0))
    def down_dgrad(dy_ref, wd_ref, g_ref, u_ref, dg_ref, du_ref):
        da = _dot_nt(dy_ref[...].astype(BF16), wd_ref[...])
        gv, uv = g_ref[...], u_ref[...]
        s = _sigmoid(gv)
        dg_ref[...] = (da * uv * s * (1.0 + gv * (1.0 - s))).astype(BF16)
        du_ref[...] = (da * gv * s).astype(BF16)

    hid = pl.BlockSpec((tm, cs), lambda j, i: (j * nm + i, 0))
    dg, du = pl.pallas_call(
        down_dgrad, name="ffn_down_dgrad", grid=(half, nm),
        in_specs=[pl.BlockSpec((tm, d), lambda j, i: (i, 0)), pl.BlockSpec((cs, d), lambda j, i: (j, 0)), hid, hid],
        out_specs=[hid, hid], out_shape=[_sds((half * t, cs), BF16), _sds((half * t, cs), BF16)],
        compiler_params=_cparams(dimension_semantics=("parallel", "arbitrary")),
    )(dy, w_down, g, u)

    def wgrad(name, dd):
        return _mm_raw(name, h, dd, ta=True, tb=False, out_dtype=F32, add=None, grid=(1, half, nk), out_shape=(half * d, cs),
                       a_block=(tk, d), a_map=lambda i, j, k: (k, 0), b_block=(tk, cs), b_map=lambda i, j, k: (j * nk + k, 0),
                       o_block=(d, cs), o_map=lambda i, j, k: (j, 0))

    def dgrad(name, dd, off, add):
        return _mm_raw(name, dd, w2, ta=False, tb=True, out_dtype=F32, add=add, grid=(nm, 1, half), out_shape=(t, d),
                       a_block=(tm, cs), a_map=lambda i, j, k: (k * nm + i, 0), b_block=(d, cs), b_map=lambda i, j, k: (k + off, 0),
                       o_block=(tm, d), o_map=lambda i, j, k: (i, 0))

    d_w3 = jnp.concatenate([wgrad("ffn_gate_wgrad", dg), wgrad("ffn_up_wgrad", du)], axis=0).reshape(ns, d, cs)
    dh = dgrad("ffn_up_dgrad", du, half, dgrad("ffn_gate_dgrad", dg, 0, None))
    dx, dgain = _rmsnorm_bwd("ffn_norm_bwd", x, norm_g, dh, dy)
    return dx, dgain, d_w3, d_wdown


def _dot_nt(a, b):
    return lax.dot_general(a, b, (((1,), (1,)), ((), ())), preferred_element_type=F32)


def _dot_tn(a, b):
    return lax.dot_general(a, b, (((0,), (0,)), ((), ())), preferred_element_type=F32)


def _dot(a, b):
    return jnp.dot(a, b, preferred_element_type=F32)


CUM_BLOCK = 128


def _tri2(lower):
    r = lax.broadcasted_iota(jnp.int32, (CUM_BLOCK, CUM_BLOCK), 0)
    c = lax.broadcasted_iota(jnp.int32, (CUM_BLOCK, CUM_BLOCK), 1)
    tri = ((r > c) if lower else (r < c)).astype(BF16)
    return jnp.concatenate([tri, tri], axis=0)


def _run_sums(x, tri2, run, reverse):
    nb = x.shape[1] // CUM_BLOCK
    outs = [None] * nb
    for j in (reversed(range(nb)) if reverse else range(nb)):
        xj = x[:, j * CUM_BLOCK:(j + 1) * CUM_BLOCK]
        hi = xj.astype(BF16)
        lo = (xj - hi.astype(F32)).astype(BF16)
        outs[j] = _dot(jnp.concatenate([hi, lo], axis=1), tri2) + run
        run = run + jnp.sum(xj, axis=1, keepdims=True)
    return jnp.concatenate(outs, axis=1), run


def _log_sigmoid(z):
    return jnp.minimum(z, 0.0) - jnp.log(1.0 + jnp.exp(-jnp.abs(z)))


def _heads_in(ref, h, width=HEAD):
    return ref[:, h * width:(h + 1) * width]


def _sb_qk(q, k, gq, gk):
    return _rms(q, gq) * (HEAD ** -0.5), _rms(k, gk)


def _sb_prep_fwd(qkv, gq, gk):
    t = qkv.shape[0]

    def body(x_ref, gq_ref, gk_ref, q_ref, k_ref, v_ref):
        for h in range(N_HEADS):
            q, k = _sb_qk(_heads_in(x_ref, h), _heads_in(x_ref, N_HEADS + h), gq_ref[...], gk_ref[...])
            q_ref[h] = q.astype(BF16)
            k_ref[h] = k.astype(BF16)
            v_ref[h] = _heads_in(x_ref, 2 * N_HEADS + h).astype(BF16)

    hm = _sds((N_HEADS, t, HEAD), BF16)
    return _rows("sb_prep", body, [qkv], [hm, hm, hm], tt=256, consts=[gq, gk])


def _sb_prep_bwd(qkv, gq, gk, dq, dk, dv):
    def body(x_ref, dq_ref, dk_ref, dv_ref, gq_ref, gk_ref, dx_ref, dgq_ref, dgk_ref):
        dgq = jnp.zeros(gq_ref.shape, F32)
        dgk = jnp.zeros(gk_ref.shape, F32)
        for h in range(N_HEADS):
            _, vjp = jax.vjp(_sb_qk, _heads_in(x_ref, h), _heads_in(x_ref, N_HEADS + h), gq_ref[...], gk_ref[...])
            a, b, c, d = vjp((dq_ref[h], dk_ref[h]))
            dx_ref[:, h * HEAD:(h + 1) * HEAD] = a.astype(BF16)
            dx_ref[:, (N_HEADS + h) * HEAD:(N_HEADS + h + 1) * HEAD] = b.astype(BF16)
            dx_ref[:, (2 * N_HEADS + h) * HEAD:(2 * N_HEADS + h + 1) * HEAD] = dv_ref[h].astype(BF16)
            dgq, dgk = dgq + c, dgk + d
        _acc(dgq_ref, dgq)
        _acc(dgk_ref, dgk)

    return _rows("sb_prep_bwd", body, [qkv, dq, dk, dv], [_sds(qkv.shape, BF16)], tt=256, consts=[gq, gk],
                 accs=[_sds(gq.shape), _sds(gk.shape)])


def _q_block(t):
    return min(ATT_Q, t)


def _key_order(bq, qb):
    rows = lax.broadcasted_iota(jnp.int32, (bq, ATT_BLOCK), 0)
    cols = lax.broadcasted_iota(jnp.int32, (bq, ATT_BLOCK), 1)
    return rows - cols + qb * bq


def _sb_attn_fwd(q, k, v):
    nh, t, _ = q.shape
    bq = _q_block(t)
    per = bq // ATT_BLOCK

    def kern(q_ref, k_ref, v_ref, o_ref):
        qb = pl.program_id(1)
        qv = q_ref[0]
        after = _tri2(True)
        order = _key_order(bq, qb)
        nkb = (qb + 1) * per

        def body(i, carry):
            o_acc, run = carry
            kb = nkb - 1 - i
            off = pl.multiple_of(kb * ATT_BLOCK, ATT_BLOCK)
            kv = k_ref[0, pl.ds(off, ATT_BLOCK), :]
            vv = v_ref[0, pl.ds(off, ATT_BLOCK), :]
            z = _dot_nt(qv, kv)
            past = order > kb * ATT_BLOCK
            lsz = _log_sigmoid(z)
            lsn = jnp.where(past, lsz - z, 0.0)
            la, run = _run_sums(lsn, after, run, True)
            a = jnp.where(past, jnp.exp(lsz + la), 0.0)
            o_acc = o_acc + _dot(a.astype(BF16), vv)
            return o_acc, run

        o, _ = lax.fori_loop(0, nkb, body, (jnp.zeros((bq, HEAD), F32), jnp.zeros((bq, 1), F32)))
        o_ref[...] = o

    return pl.pallas_call(
        kern, name="sb_attn_fwd", grid=(nh, t // bq),
        in_specs=[pl.BlockSpec((1, bq, HEAD), lambda h, i: (h, i, 0)),
                  pl.BlockSpec((1, t, HEAD), lambda h, i: (h, 0, 0)),
                  pl.BlockSpec((1, t, HEAD), lambda h, i: (h, 0, 0))],
        out_specs=pl.BlockSpec((bq, HEAD), lambda h, i: (i, h)),
        out_shape=_sds((t, nh * HEAD)),
        compiler_params=_cparams(dimension_semantics=("parallel", "arbitrary")),
    )(q, k, v)


def _sb_attn_bwd(q, k, v, do):
    nh, t, _ = q.shape
    bq = _q_block(t)
    per = bq // ATT_BLOCK

    def kern(q_ref, k_ref, v_ref, do_ref, dq_ref, dk_ref, dv_ref, g_s, ls_s):
        qb = pl.program_id(1)

        @pl.when(qb == 0)
        def _():
            dk_ref[...] = jnp.zeros(dk_ref.shape, F32)
            dv_ref[...] = jnp.zeros(dv_ref.shape, F32)

        qv = q_ref[0]
        dob = do_ref[...].astype(BF16)
        after, before = _tri2(True), _tri2(False)
        order = _key_order(bq, qb)
        nkb = (qb + 1) * per

        def sweep_left(i, run):
            kb = nkb - 1 - i
            off = pl.multiple_of(kb * ATT_BLOCK, ATT_BLOCK)
            kv = k_ref[0, pl.ds(off, ATT_BLOCK), :]
            vv = v_ref[0, pl.ds(off, ATT_BLOCK), :]
            z = _dot_nt(qv, kv)
            past = order > kb * ATT_BLOCK
            lsz = _log_sigmoid(z)
            lsn = jnp.where(past, lsz - z, 0.0)
            la, run = _run_sums(lsn, after, run, True)
            a = jnp.where(past, jnp.exp(lsz + la), 0.0)
            g_s[kb] = _dot_nt(dob, vv) * a
            ls_s[kb] = lsz
            dv_ref[0, pl.ds(off, ATT_BLOCK), :] += _dot_tn(a.astype(BF16), dob)
            return run

        zero = jnp.zeros((bq, 1), F32)
        lax.fori_loop(0, nkb, sweep_left, zero)

        def sweep_right(kb, carry):
            dq_acc, run_g = carry
            off = pl.multiple_of(kb * ATT_BLOCK, ATT_BLOCK)
            kv = k_ref[0, pl.ds(off, ATT_BLOCK), :]
            g = g_s[kb]
            sg = jnp.exp(ls_s[kb])
            past = order > kb * ATT_BLOCK
            dls, run_g = _run_sums(g, before, run_g, False)
            dzb = jnp.where(past, g * (1.0 - sg) - dls * sg, 0.0).astype(BF16)
            dk_ref[0, pl.ds(off, ATT_BLOCK), :] += _dot_tn(dzb, qv)
            return dq_acc + _dot(dzb, kv), run_g

        dq, _ = lax.fori_loop(0, nkb, sweep_right, (jnp.zeros((bq, HEAD), F32), zero))
        dq_ref[0] = dq

    hm = _sds((nh, t, HEAD))
    full = pl.BlockSpec((1, t, HEAD), lambda h, i: (h, 0, 0))
    tok = pl.BlockSpec((bq, HEAD), lambda h, i: (i, h))
    nkb_max = t // ATT_BLOCK
    return pl.pallas_call(
        kern, name="sb_attn_bwd", grid=(nh, t // bq),
        in_specs=[pl.BlockSpec((1, bq, HEAD), lambda h, i: (h, i, 0)), full, full, tok],
        out_specs=[pl.BlockSpec((1, bq, HEAD), lambda h, i: (h, i, 0)), full, full],
        out_shape=[hm, hm, hm],
        scratch_shapes=[pltpu.VMEM((nkb_max, bq, ATT_BLOCK), F32), pltpu.VMEM((nkb_max, bq, ATT_BLOCK), F32)],
        compiler_params=_cparams(dimension_semantics=("parallel", "arbitrary")),
    )(q, k, v, do)


def _sb_fwd(x, h, w_qkv, gq, gk, w_out):
    qkv = _mm("sb_qkv", h, w_qkv)
    q, k, v = _sb_prep_fwd(qkv, gq, gk)
    o = _sb_attn_fwd(q, k, v)
    y = _mm("sb_out", o, w_out, add=x)
    return y, (h, qkv, q, k, v, o)


def _sb_bwd(saved, dy, w_qkv, gq, gk, w_out):
    h, qkv, q, k, v, o = saved
    d_wout = _mm("sb_out_wgrad", o, dy, ta=True)
    do = _mm("sb_out_dgrad", dy, w_out, tb=True)
    dq, dk, dv = _sb_attn_bwd(q, k, v, do)
    dqkv, dgq, dgk = _sb_prep_bwd(qkv, gq, gk, dq, dk, dv)
    d_wqkv = _mm("sb_qkv_wgrad", h, dqkv, ta=True)
    dh = _mm("sb_qkv_dgrad", dqkv, w_qkv, tb=True)
    return dh, d_wqkv, dgq, dgk, d_wout


DN_QKV = 3 * N_HEADS * HEAD
DN_PROJ = DN_QKV + N_HEADS * HEAD + LANES
DN_CONV = 4
HALO = 8
CONV_COLS = 512


def _dn_conv_fwd(proj, conv_w, tt=256):
    t = proj.shape[0]
    tt = min(tt, t)

    def body(u_ref, prev_ref, w_ref, c_ref):
        i = pl.program_id(0)
        for cc in range(DN_QKV // CONV_COLS):
            cs = slice(cc * CONV_COLS, (cc + 1) * CONV_COLS)
            cur = u_ref[:, cs]
            prev = jnp.where(i > 0, prev_ref[:, cs], 0.0)
            ext = jnp.concatenate([prev, cur], axis=0)
            y = cur * w_ref[DN_CONV - 1:DN_CONV, cs]
            for j in range(DN_CONV - 1):
                y = y + pltpu.roll(ext, DN_CONV - 1 - j, 0)[HALO:] * w_ref[j:j + 1, cs]
            c_ref[:, cs] = y

    return _rows("dn_conv", body,
                 [(proj, (tt, DN_QKV), lambda i: (i, 0)),
                  (proj, (HALO, DN_QKV), lambda i: (jnp.maximum(i * (tt // HALO) - 1, 0), 0))],
                 [_sds((t, DN_QKV))], tt=tt, consts=[conv_w])[0]


def _dn_conv_bwd(proj, conv_w, dc, dz, dab, tt=256):
    t = proj.shape[0]
    tt = min(tt, t)
    nblk = t // tt

    def body(u_ref, prev_ref, dc_ref, next_ref, dz_ref, dab_ref, w_ref, dp_ref, dw_ref):
        i = pl.program_id(0)
        dws = []
        for cc in range(DN_QKV // CONV_COLS):
            cs = slice(cc * CONV_COLS, (cc + 1) * CONV_COLS)
            cur = u_ref[:, cs]
            prev = jnp.where(i > 0, prev_ref[:, cs], 0.0)
            ext_u = jnp.concatenate([prev, cur], axis=0)
            d = dc_ref[:, cs]
            nxt = jnp.where(i < nblk - 1, next_ref[:, cs], 0.0)
            ext_d = jnp.concatenate([d, nxt], axis=0)
            du = d * w_ref[DN_CONV - 1:DN_CONV, cs]
            rows = [jnp.sum(d * cur, axis=0, keepdims=True)]
            for j in range(DN_CONV - 2, -1, -1):
                sh = DN_CONV - 1 - j
                du = du + pltpu.roll(ext_d, tt + HALO - sh, 0)[:tt] * w_ref[j:j + 1, cs]
                rows.insert(0, jnp.sum(d * pltpu.roll(ext_u, sh, 0)[HALO:], axis=0, keepdims=True))
            dp_ref[:, cs] = du.astype(BF16)
            dws.append(jnp.concatenate(rows, axis=0))
        dp_ref[:, DN_QKV:DN_QKV + N_HEADS * HEAD] = dz_ref[...].astype(BF16)
        dp_ref[:, DN_QKV + N_HEADS * HEAD:] = dab_ref[...].astype(BF16)
        _acc(dw_ref, jnp.concatenate(dws, axis=1))

    return _rows("dn_conv_bwd", body,
                 [(proj, (tt, DN_QKV), lambda i: (i, 0)),
                  (proj, (HALO, DN_QKV), lambda i: (jnp.maximum(i * (tt // HALO) - 1, 0), 0)),
                  dc,
                  (dc, (HALO, DN_QKV), lambda i: (jnp.minimum((i + 1) * (tt // HALO), t // HALO - 1), 0)),
                  dz, dab],
                 [_sds((t, DN_PROJ), BF16)], tt=tt, consts=[conv_w], accs=[_sds(conv_w.shape)])


def _l2n(x):
    return x * lax.rsqrt(jnp.sum(x * x, axis=-1, keepdims=True) + NORM_EPS)


def _dn_qkv(cq, ck, cv):
    return _l2n(_silu(cq)) * (HEAD ** -0.5), _l2n(_silu(ck)), _silu(cv)


def _dn_gates(ab, a_log, dt_bias):
    lane = lax.broadcasted_iota(jnp.int32, ab.shape, 1)
    g = -jnp.exp(a_log) * _softplus(ab + dt_bias)
    return jnp.where(lane < N_HEADS, g, jnp.where(lane < 2 * N_HEADS, _sigmoid(ab), 0.0))


def _ab_spec(tt):
    return (tt, LANES), lambda i: (i, DN_PROJ // LANES - 1)


def _dn_prep_fwd(c, proj, a_log, dt_bias, tt=256):
    t = c.shape[0]
    tt = min(tt, t)

    def body(c_ref, ab_ref, al_ref, dt_ref, q_ref, k_ref, v_ref, g_ref):
        for h in range(N_HEADS):
            q_ref[h], k_ref[h], v_ref[h] = _dn_qkv(_heads_in(c_ref, h), _heads_in(c_ref, N_HEADS + h), _heads_in(c_ref, 2 * N_HEADS + h))
        g_ref[...] = _dn_gates(ab_ref[...], al_ref[...], dt_ref[...])

    hm = _sds((N_HEADS, t, HEAD))
    return _rows("dn_prep", body, [c, (proj,) + _ab_spec(tt)], [hm, hm, hm, _sds((t, LANES))], tt=tt, consts=[a_log, dt_bias])


def _dn_prep_bwd(c, proj, a_log, dt_bias, dq, dk, dv, dgates, tt=256):
    t = c.shape[0]
    tt = min(tt, t)

    def body(c_ref, ab_ref, dq_ref, dk_ref, dv_ref, dg_ref, al_ref, dt_ref, dc_ref, dab_ref, dal_ref, ddt_ref):
        for h in range(N_HEADS):
            _, vjp = jax.vjp(_dn_qkv, _heads_in(c_ref, h), _heads_in(c_ref, N_HEADS + h), _heads_in(c_ref, 2 * N_HEADS + h))
            a, b, d = vjp((dq_ref[h], dk_ref[h], dv_ref[h]))
            dc_ref[:, h * HEAD:(h + 1) * HEAD] = a
            dc_ref[:, (N_HEADS + h) * HEAD:(N_HEADS + h + 1) * HEAD] = b
            dc_ref[:, (2 * N_HEADS + h) * HEAD:(2 * N_HEADS + h + 1) * HEAD] = d
        _, vjp = jax.vjp(_dn_gates, ab_ref[...], al_ref[...], dt_ref[...])
        dab, dal, ddt = vjp(dg_ref[...])
        dab_ref[...] = dab
        _acc(dal_ref, dal)
        _acc(ddt_ref, ddt)

    return _rows("dn_prep_bwd", body, [c, (proj,) + _ab_spec(tt), dq, dk, dv, dgates], [_sds(c.shape), _sds((t, LANES))],
                 tt=tt, consts=[a_log, dt_bias], accs=[_sds(a_log.shape), _sds(dt_bias.shape)])


def _bdot(a, b, prec=None):
    return lax.dot_general(a, b, (((2,), (1,)), ((0,), (0,))), precision=prec, preferred_element_type=F32)


def _bdot_nt(a, b, prec=None):
    return lax.dot_general(a, b, (((2,), (2,)), ((0,), (0,))), precision=prec, preferred_element_type=F32)


def _bdot_tn(a, b, prec=None):
    return lax.dot_general(a, b, (((1,), (1,)), ((0,), (0,))), precision=prec, preferred_element_type=F32)


def _inv_raw(low):
    c = low.shape[-1]
    r = lax.broadcasted_iota(jnp.int32, (c, c), 0)
    s = lax.broadcasted_iota(jnp.int32, (c, c), 1)
    m = jnp.where(r == s, 1.0, 0.0) - low
    p = _bdot(low, low, F32X3)
    n_fac = int(math.log2(c)) - 1
    for i in range(n_fac):
        m = m + _bdot(m, p, F32X3)
        if i < n_fac - 1:
            p = _bdot(p, p, F32X3)
    return m


@jax.custom_vjp
def _inv_unit_lower(low):
    return _inv_raw(low)


def _inv_fwd(low):
    m = _inv_raw(low)
    return m, m


def _inv_bwd(m, dm):
    return (-_bdot_nt(_bdot_tn(m, dm, F32X3), m, F32X3),)


_inv_unit_lower.defvjp(_inv_fwd, _inv_bwd)


def _dn_chunk(q, k, v, gates, s):
    nh, c, _ = q.shape
    lane = lax.broadcasted_iota(jnp.int32, gates.shape, 1)
    def column(j):
        return jnp.sum(jnp.where(lane == j, gates, 0.0), axis=1, keepdims=True)[None]

    g_col = jnp.concatenate([column(h) for h in range(nh)], axis=0)
    b_col = jnp.concatenate([column(h + nh) for h in range(nh)], axis=0)
    r = lax.broadcasted_iota(jnp.int32, (c, c), 0)
    cc = lax.broadcasted_iota(jnp.int32, (c, c), 1)
    causal, strict = r >= cc, r > cc
    incl = jnp.broadcast_to(jnp.where(causal, 1.0, 0.0), (nh, c, c))
    upper = jnp.broadcast_to(jnp.where(r <= cc, 1.0, 0.0), (nh, c, c))
    gb = jnp.broadcast_to(g_col, (nh, c, LANES))
    gbc = jnp.broadcast_to(g_col, (nh, c, c))
    gc = _bdot(incl, gb, F32X3)
    gc_r = _bdot(incl, gbc, F32X3)
    gc_c = _bdot_tn(gbc, upper, F32X3)
    decay = jnp.where(causal, jnp.exp(jnp.where(causal, gc_r - gc_c, 0.0)), 0.0)
    kb = k * b_col
    low = jnp.where(strict, _bdot_nt(kb, k) * decay, 0.0)
    m = _inv_unit_lower(low)
    egc = jnp.exp(gc)
    u = _bdot(m, v * b_col, F32X3)
    w = _bdot(m, kb * egc, F32X3)
    attn = _bdot_nt(q, k) * decay
    gl = jnp.sum(gb, axis=1, keepdims=True)
    v_new = u - _bdot(w, s)
    o = _bdot(q * egc, s) + _bdot(attn, v_new)
    s_new = s * jnp.exp(gl) + _bdot_tn(k * jnp.exp(gl - gc), v_new)
    return o, s_new


def _dn_chunks_fwd(q, k, v, gates):
    nh, t, _ = q.shape
    n = t // DN_CHUNK

    def kern(q_ref, k_ref, v_ref, g_ref, o_ref, sin_ref, s_scr):
        @pl.when(pl.program_id(0) == 0)
        def _():
            s_scr[...] = jnp.zeros(s_scr.shape, F32)

        s = s_scr[...]
        sin_ref[0] = s
        o_ref[...], s_scr[...] = _dn_chunk(q_ref[...], k_ref[...], v_ref[...], g_ref[...], s)

    blk = pl.BlockSpec((nh, DN_CHUNK, HEAD), lambda i: (0, i, 0))
    return pl.pallas_call(
        kern, name="dn_chunks_fwd", grid=(n,),
        in_specs=[blk, blk, blk, pl.BlockSpec((DN_CHUNK, LANES), lambda i: (i, 0))],
        out_specs=[blk, pl.BlockSpec((1, nh, HEAD, HEAD), lambda i: (i, 0, 0, 0))],
        out_shape=[_sds((nh, t, HEAD)), _sds((n, nh, HEAD, HEAD))],
        scratch_shapes=[pltpu.VMEM((nh, HEAD, HEAD), F32)],
        compiler_params=_cparams(dimension_semantics=("arbitrary",)),
    )(q, k, v, gates)


def _dn_chunks_bwd(q, k, v, gates, s_in, do):
    nh, t, _ = q.shape
    n = t // DN_CHUNK

    def kern(q_ref, k_ref, v_ref, g_ref, sin_ref, do_ref, dq_ref, dk_ref, dv_ref, dg_ref, ds_scr):
        @pl.when(pl.program_id(0) == 0)
        def _():
            ds_scr[...] = jnp.zeros(ds_scr.shape, F32)

        _, vjp = jax.vjp(_dn_chunk, q_ref[...], k_ref[...], v_ref[...], g_ref[...], sin_ref[0])
        dq_ref[...], dk_ref[...], dv_ref[...], dg_ref[...], ds_scr[...] = vjp((do_ref[...], ds_scr[...]))

    blk = pl.BlockSpec((nh, DN_CHUNK, HEAD), lambda i: (0, n - 1 - i, 0))
    gblk = pl.BlockSpec((DN_CHUNK, LANES), lambda i: (n - 1 - i, 0))
    hm = _sds((nh, t, HEAD))
    return pl.pallas_call(
        kern, name="dn_chunks_bwd", grid=(n,),
        in_specs=[blk, blk, blk, gblk, pl.BlockSpec((1, nh, HEAD, HEAD), lambda i: (n - 1 - i, 0, 0, 0)), blk],
        out_specs=[blk, blk, blk, gblk],
        out_shape=[hm, hm, hm, _sds((t, LANES))],
        scratch_shapes=[pltpu.VMEM((nh, HEAD, HEAD), F32)],
        compiler_params=_cparams(dimension_semantics=("arbitrary",)),
    )(q, k, v, gates, s_in, do)


def _dn_gate_out(o, z, g):
    return _rms(o, g) * _silu(z)


def _z_spec(tt):
    return (tt, N_HEADS * HEAD), lambda i: (i, DN_QKV // (N_HEADS * HEAD))


def _dn_post_fwd(o, proj, out_norm, tt=256):
    t = o.shape[1]
    tt = min(tt, t)

    def body(o_ref, z_ref, g_ref, y_ref):
        for h in range(N_HEADS):
            y_ref[:, h * HEAD:(h + 1) * HEAD] = _dn_gate_out(o_ref[h], _heads_in(z_ref, h), g_ref[...]).astype(BF16)

    return _rows("dn_post", body, [o, (proj,) + _z_spec(tt)], [_sds((t, N_HEADS * HEAD), BF16)], tt=tt, consts=[out_norm])[0]


def _dn_post_bwd(o, proj, out_norm, dy, tt=256):
    t = o.shape[1]
    tt = min(tt, t)

    def body(o_ref, z_ref, dy_ref, g_ref, do_ref, dz_ref, dg_ref):
        dg = jnp.zeros(g_ref.shape, F32)
        for h in range(N_HEADS):
            _, vjp = jax.vjp(_dn_gate_out, o_ref[h], _heads_in(z_ref, h), g_ref[...])
            a, b, d = vjp(_heads_in(dy_ref, h))
            do_ref[h] = a
            dz_ref[:, h * HEAD:(h + 1) * HEAD] = b
            dg = dg + d
        _acc(dg_ref, dg)

    return _rows("dn_post_bwd", body, [o, (proj,) + _z_spec(tt), dy], [_sds(o.shape), _sds((t, N_HEADS * HEAD))], tt=tt,
                 consts=[out_norm], accs=[_sds(out_norm.shape)])


def _dn_fwd(x, h, w_in, conv_w, a_log, dt_bias, out_norm, w_out):
    proj = _mm("dn_in", h, w_in)
    c = _dn_conv_fwd(proj, conv_w)
    q, k, v, gates = _dn_prep_fwd(c, proj, a_log, dt_bias)
    o, s_in = _dn_chunks_fwd(q, k, v, gates)
    on = _dn_post_fwd(o, proj, out_norm)
    y = _mm("dn_out", on, w_out, add=x)
    return y, (h, proj, c, q, k, v, gates, o, s_in, on)


def _dn_bwd(saved, dy, w_in, conv_w, a_log, dt_bias, out_norm, w_out):
    h, proj, c, q, k, v, gates, o, s_in, on = saved
    d_wout = _mm("dn_out_wgrad", on, dy, ta=True)
    don = _mm("dn_out_dgrad", dy, w_out, tb=True)
    do, dz, d_out_norm = _dn_post_bwd(o, proj, out_norm, don)
    dq, dk, dv, dgates = _dn_chunks_bwd(q, k, v, gates, s_in, do)
    dc, dab, d_a_log, d_dt_bias = _dn_prep_bwd(c, proj, a_log, dt_bias, dq, dk, dv, dgates)
    dproj, d_conv_w = _dn_conv_bwd(proj, conv_w, dc, dz, dab)
    d_win = _mm("dn_in_wgrad", h, dproj, ta=True)
    dh = _mm("dn_in_dgrad", dproj, w_in, tb=True)
    return dh, d_win, d_conv_w, d_a_log, d_dt_bias, d_out_norm, d_wout


MLA_SCALE = MLA_QK ** -0.5
MLA_C = 512


def _swap_raw(x):
    lane = lax.broadcasted_iota(jnp.int32, x.shape, 1)
    half = MLA_ROPE // 2
    y = jnp.where(lane < half, pltpu.roll(x, LANES - half, 1), pltpu.roll(x, half, 1))
    return jnp.where(lane < MLA_ROPE, y, 0.0)


@jax.custom_vjp
def _swap_halves(x):
    return _swap_raw(x)


_swap_halves.defvjp(lambda x: (_swap_raw(x), None), lambda _, d: (_swap_raw(d),))


def _rms_rope(x, g, cos, sin):
    y = x * lax.rsqrt(jnp.sum(x * x, axis=-1, keepdims=True) * (1.0 / MLA_ROPE) + NORM_EPS) * g
    return y * cos + _swap_halves(y) * sin


def _mla_latent(cq, ckv, kr, gq, gkv, gkr, cos, sin):
    return _rms(cq, gq), _rms(ckv, gkv), _rms_rope(kr, gkr, cos, sin)


def _mla_prep1_fwd(c, gq, gkv, gkr, cos, sin):
    t = c.shape[0]

    def body(c_ref, cos_ref, sin_ref, gq_ref, gkv_ref, gkr_ref, cq_ref, ckv_ref, kr_ref):
        a, b, r = _mla_latent(c_ref[:, :256], c_ref[:, 256:384], c_ref[:, 384:], gq_ref[...], gkv_ref[...], gkr_ref[...],
                              cos_ref[...], sin_ref[...])
        cq_ref[...] = a.astype(BF16)
        ckv_ref[...] = b.astype(BF16)
        kr_ref[...] = r.astype(BF16)

    return _rows("mla_prep1", body, [c, cos, sin], [_sds((t, 256), BF16), _sds((t, HEAD), BF16), _sds((t, HEAD), BF16)],
                 tt=512, consts=[gq, gkv, gkr])


def _mla_prep1_bwd(c, gq, gkv, gkr, cos, sin, dcq, dckv, dkr_heads):
    def body(c_ref, cos_ref, sin_ref, dcq_ref, dckv_ref, dkr_ref, gq_ref, gkv_ref, gkr_ref, dc_ref, dgq_ref, dgkv_ref, dgkr_ref):
        dkr = dkr_ref[0]
        for h in range(1, N_HEADS):
            dkr = dkr + dkr_ref[h]
        _, vjp = jax.vjp(_mla_latent, c_ref[:, :256], c_ref[:, 256:384], c_ref[:, 384:], gq_ref[...], gkv_ref[...], gkr_ref[...],
                         cos_ref[...], sin_ref[...])
        a, b, r, d1, d2, d3, _, _ = vjp((dcq_ref[...], dckv_ref[...], dkr))
        dc_ref[:, :256] = a.astype(BF16)
        dc_ref[:, 256:384] = b.astype(BF16)
        dc_ref[:, 384:] = r.astype(BF16)
        _acc(dgq_ref, d1)
        _acc(dgkv_ref, d2)
        _acc(dgkr_ref, d3)

    return _rows("mla_prep1_bwd", body, [c, cos, sin, dcq, dckv, dkr_heads], [_sds(c.shape, BF16)], tt=512,
                 consts=[gq, gkv, gkr], accs=[_sds(gq.shape), _sds(gkv.shape), _sds(gkr.shape)])


def _mla_heads(qn, qr, kn, gqn, gqr, gkn, cos, sin):
    return _rms(qn, gqn) * MLA_SCALE, _rms_rope(qr, gqr, cos, sin) * MLA_SCALE, _rms(kn, gkn)


def _mla_prep2_fwd(qa, kv, gqn, gqr, gkn, cos, sin):
    t = qa.shape[0]

    def body(qa_ref, kv_ref, cos_ref, sin_ref, gqn_ref, gqr_ref, gkn_ref, qn_ref, qr_ref, kn_ref, v_ref):
        for h in range(N_HEADS):
            a, b, c = _mla_heads(_heads_in(qa_ref, h), _heads_in(qa_ref, N_HEADS + h), _heads_in(kv_ref, h),
                                 gqn_ref[...], gqr_ref[...], gkn_ref[...], cos_ref[...], sin_ref[...])
            qn_ref[h] = a.astype(BF16)
            qr_ref[h] = b.astype(BF16)
            kn_ref[h] = c.astype(BF16)
            v_ref[h] = _heads_in(kv_ref, N_HEADS + h).astype(BF16)

    hm = _sds((N_HEADS, t, HEAD), BF16)
    return _rows("mla_prep2", body, [qa, kv, cos, sin], [hm, hm, hm, hm], tt=256, consts=[gqn, gqr, gkn])


def _mla_prep2_bwd(qa, kv, gqn, gqr, gkn, cos, sin, dqn, dqr, dkn, dv):
    def body(qa_ref, kv_ref, cos_ref, sin_ref, dqn_ref, dqr_ref, dkn_ref, dv_ref, gqn_ref, gqr_ref, gkn_ref,
             dqa_ref, dkv_ref, d1_ref, d2_ref, d3_ref):
        d1 = jnp.zeros(gqn_ref.shape, F32)
        d2 = jnp.zeros(gqr_ref.shape, F32)
        d3 = jnp.zeros(gkn_ref.shape, F32)
        for h in range(N_HEADS):
            _, vjp = jax.vjp(_mla_heads, _heads_in(qa_ref, h), _heads_in(qa_ref, N_HEADS + h), _heads_in(kv_ref, h),
                             gqn_ref[...], gqr_ref[...], gkn_ref[...], cos_ref[...], sin_ref[...])
            a, b, c, e1, e2, e3, _, _ = vjp((dqn_ref[h], dqr_ref[h], dkn_ref[h]))
            dqa_ref[:, h * HEAD:(h + 1) * HEAD] = a.astype(BF16)
            dqa_ref[:, (N_HEADS + h) * HEAD:(N_HEADS + h + 1) * HEAD] = b.astype(BF16)
            dkv_ref[:, h * HEAD:(h + 1) * HEAD] = c.astype(BF16)
            dkv_ref[:, (N_HEADS + h) * HEAD:(N_HEADS + h + 1) * HEAD] = dv_ref[h].astype(BF16)
            d1, d2, d3 = d1 + e1, d2 + e2, d3 + e3
        _acc(d1_ref, d1)
        _acc(d2_ref, d2)
        _acc(d3_ref, d3)

    return _rows("mla_prep2_bwd", body, [qa, kv, cos, sin, dqn, dqr, dkn, dv], [_sds(qa.shape, BF16), _sds(kv.shape, BF16)],
                 tt=256, consts=[gqn, gqr, gkn], accs=[_sds(gqn.shape), _sds(gqr.shape), _sds(gkn.shape)])


def _mla_attn_fwd(qn, qr, kn, kr, v):
    nh, t, _ = qn.shape
    bq = _q_block(t)
    per = bq // ATT_BLOCK

    def kern(qn_ref, qr_ref, kn_ref, kr_ref, v_ref, o_ref, lse_ref):
        qb = pl.program_id(1)
        qv = jnp.concatenate([qn_ref[0], qr_ref[0]], axis=1)
        order = _key_order(bq, qb)

        def body(kb, carry):
            acc, m, l = carry
            off = pl.multiple_of(kb * ATT_BLOCK, ATT_BLOCK)
            kv = jnp.concatenate([kn_ref[0, pl.ds(off, ATT_BLOCK), :], kr_ref[pl.ds(off, ATT_BLOCK), :]], axis=1)
            s = jnp.where(order >= kb * ATT_BLOCK, _dot_nt(qv, kv), -jnp.inf)
            m_new = jnp.maximum(m, jnp.max(s, axis=1, keepdims=True))
            alpha = jnp.exp(m - m_new)
            p = jnp.exp(s - m_new)
            acc = acc * alpha + _dot(p.astype(BF16), v_ref[0, pl.ds(off, ATT_BLOCK), :])
            return acc, m_new, l * alpha + jnp.sum(p, axis=1, keepdims=True)

        init = (jnp.zeros((bq, HEAD), F32), jnp.full((bq, 1), -jnp.inf, F32), jnp.zeros((bq, 1), F32))
        acc, m, l = lax.fori_loop(0, (qb + 1) * per, body, init)
        o_ref[...] = acc / l
        lse_ref[...] = jnp.broadcast_to(m + jnp.log(l), (bq, HEAD))

    blk = pl.BlockSpec((1, bq, HEAD), lambda h, i: (h, i, 0))
    full = pl.BlockSpec((1, t, HEAD), lambda h, i: (h, 0, 0))
    tok = pl.BlockSpec((bq, HEAD), lambda h, i: (i, h))
    return pl.pallas_call(
        kern, name="mla_attn_fwd", grid=(nh, t // bq),
        in_specs=[blk, blk, full, pl.BlockSpec((t, HEAD), lambda h, i: (0, 0)), full],
        out_specs=[tok, tok], out_shape=[_sds((t, nh * HEAD)), _sds((t, nh * HEAD))],
        compiler_params=_cparams(dimension_semantics=("parallel", "arbitrary")),
    )(qn, qr, kn, kr, v)


def _mla_attn_bwd(qn, qr, kn, kr, v, o, lse, do):
    nh, t, _ = qn.shape
    bq = _q_block(t)
    per = bq // ATT_BLOCK

    def kern(qn_ref, qr_ref, kn_ref, kr_ref, v_ref, o_ref, lse_ref, do_ref, dqn_ref, dqr_ref, dkn_ref, dkr_ref, dv_ref):
        qb = pl.program_id(1)

        @pl.when(qb == 0)
        def _():
            dkn_ref[...] = jnp.zeros(dkn_ref.shape, F32)
            dkr_ref[...] = jnp.zeros(dkr_ref.shape, F32)
            dv_ref[...] = jnp.zeros(dv_ref.shape, F32)

        qv = jnp.concatenate([qn_ref[0], qr_ref[0]], axis=1)
        dov = do_ref[...]
        dob = dov.astype(BF16)
        delta = jnp.sum(dov * o_ref[...], axis=1, keepdims=True)
        lse_col = lse_ref[:, :1]
        order = _key_order(bq, qb)

        def body(kb, dq):
            off = pl.multiple_of(kb * ATT_BLOCK, ATT_BLOCK)
            kv = jnp.concatenate([kn_ref[0, pl.ds(off, ATT_BLOCK), :], kr_ref[pl.ds(off, ATT_BLOCK), :]], axis=1)
            vv = v_ref[0, pl.ds(off, ATT_BLOCK), :]
            p = jnp.where(order >= kb * ATT_BLOCK, jnp.exp(_dot_nt(qv, kv) - lse_col), 0.0)
            ds = (p * (_dot_nt(dob, vv) - delta)).astype(BF16)
            dk = _dot_tn(ds, qv)
            dkn_ref[0, pl.ds(off, ATT_BLOCK), :] += dk[:, :HEAD]
            dkr_ref[0, pl.ds(off, ATT_BLOCK), :] += dk[:, HEAD:]
            dv_ref[0, pl.ds(off, ATT_BLOCK), :] += _dot_tn(p.astype(BF16), dob)
            return dq + _dot(ds, kv)

        dq = lax.fori_loop(0, (qb + 1) * per, body, jnp.zeros((bq, 2 * HEAD), F32))
        dqn_ref[0] = dq[:, :HEAD]
        dqr_ref[0] = dq[:, HEAD:]

    hm = _sds((nh, t, HEAD))
    blk = pl.BlockSpec((1, bq, HEAD), lambda h, i: (h, i, 0))
    full = pl.BlockSpec((1, t, HEAD), lambda h, i: (h, 0, 0))
    tok = pl.BlockSpec((bq, HEAD), lambda h, i: (i, h))
    return pl.pallas_call(
        kern, name="mla_attn_bwd", grid=(nh, t // bq),
        in_specs=[blk, blk, full, pl.BlockSpec((t, HEAD), lambda h, i: (0, 0)), full, tok, tok, tok],
        out_specs=[blk, blk, full, full, full], out_shape=[hm, hm, hm, hm, hm],
        compiler_params=_cparams(dimension_semantics=("parallel", "arbitrary")),
    )(qn, qr, kn, kr, v, o, lse, do)


def _rope_tables(t):
    inv_freq = ROPE_THETA ** (-jnp.arange(0, MLA_ROPE, 2, dtype=F32) / MLA_ROPE)
    ang = jnp.arange(t, dtype=F32)[:, None] * inv_freq[None, :]
    c, s = jnp.cos(ang), jnp.sin(ang)
    pad = ((0, 0), (0, LANES - MLA_ROPE))
    return jnp.pad(jnp.concatenate([c, c], axis=1), pad), jnp.pad(jnp.concatenate([-s, s], axis=1), pad)


def _pad_lanes(v, n=LANES):
    return jnp.pad(v, (0, n - v.shape[0])).reshape(1, n)


def _mla_layout(w_down, w_uq, w_ukv):
    w_down_p = jnp.pad(w_down, ((0, 0), (0, MLA_C - w_down.shape[1])))
    uq = w_uq.reshape(w_uq.shape[0], N_HEADS, MLA_QK)
    rope = jnp.pad(uq[:, :, HEAD:], ((0, 0), (0, 0), (0, LANES - MLA_ROPE)))
    w_uq_p = jnp.concatenate([uq[:, :, :HEAD].reshape(-1, N_HEADS * HEAD), rope.reshape(-1, N_HEADS * LANES)], axis=1)
    ukv = w_ukv.reshape(w_ukv.shape[0], N_HEADS, 2 * HEAD)
    w_ukv_p = jnp.concatenate([ukv[:, :, :HEAD].reshape(-1, N_HEADS * HEAD), ukv[:, :, HEAD:].reshape(-1, N_HEADS * HEAD)], axis=1)
    return w_down_p, w_uq_p, w_ukv_p


def _mla_unlayout(d_down_p, d_uq_p, d_ukv_p):
    d_down = d_down_p[:, :256 + HEAD + MLA_ROPE]
    nope = d_uq_p[:, :N_HEADS * HEAD].reshape(-1, N_HEADS, HEAD)
    rope = d_uq_p[:, N_HEADS * HEAD:].reshape(-1, N_HEADS, LANES)[:, :, :MLA_ROPE]
    d_uq = jnp.concatenate([nope, rope], axis=2).reshape(-1, N_HEADS * MLA_QK)
    kn = d_ukv_p[:, :N_HEADS * HEAD].reshape(-1, N_HEADS, HEAD)
    vv = d_ukv_p[:, N_HEADS * HEAD:].reshape(-1, N_HEADS, HEAD)
    d_ukv = jnp.concatenate([kn, vv], axis=2).reshape(-1, N_HEADS * 2 * HEAD)
    return d_down, d_uq, d_ukv


def _mla_weight_shapes():
    return (_sds((1024, MLA_C), BF16), _sds((1, 256)), _sds((1, HEAD)), _sds((256, 2048), BF16), _sds((HEAD, 2048), BF16),
            _sds((1, HEAD)), _sds((1, HEAD)), _sds((1, HEAD)), _sds((1, HEAD)), _sds((1024, 1024), BF16),
            _sds((4096, HEAD)), _sds((4096, HEAD)))


def _mla_fwd(x, h, w_down, gq, gkv, w_uq, w_ukv, gqn, gqr, gkn, gkr, w_out, cos, sin):
    c = _mm("mla_down", h, w_down)
    cq, ckv, kr = _mla_prep1_fwd(c, gq, gkv, gkr, cos, sin)
    qa = _mm("mla_uq", cq, w_uq)
    kv = _mm("mla_ukv", ckv, w_ukv)
    qn, qr, kn, v = _mla_prep2_fwd(qa, kv, gqn, gqr, gkn, cos, sin)
    o, lse = _mla_attn_fwd(qn, qr, kn, kr, v)
    y = _mm("mla_out", o, w_out, add=x)
    return y, (h, c, cq, ckv, kr, qa, kv, qn, qr, kn, v, o, lse)


def _mla_bwd(saved, dy, w_down, gq, gkv, w_uq, w_ukv, gqn, gqr, gkn, gkr, w_out, cos, sin):
    h, c, cq, ckv, kr, qa, kv, qn, qr, kn, v, o, lse = saved
    d_wout = _mm("mla_out_wgrad", o, dy, ta=True)
    do = _mm("mla_out_dgrad", dy, w_out, tb=True)
    dqn, dqr, dkn, dkr, dv = _mla_attn_bwd(qn, qr, kn, kr, v, o, lse, do)
    dqa, dkv, dgqn, dgqr, dgkn = _mla_prep2_bwd(qa, kv, gqn, gqr, gkn, cos, sin, dqn, dqr, dkn, dv)
    d_wuq = _mm("mla_uq_wgrad", cq, dqa, ta=True)
    d_wukv = _mm("mla_ukv_wgrad", ckv, dkv, ta=True)
    dcq = _mm("mla_uq_dgrad", dqa, w_uq, tb=True)
    dckv = _mm("mla_ukv_dgrad", dkv, w_ukv, tb=True)
    dc, dgq, dgkv, dgkr = _mla_prep1_bwd(c, gq, gkv, gkr, cos, sin, dcq, dckv, dkr)
    d_wdown = _mm("mla_down_wgrad", h, dc, ta=True)
    dh = _mm("mla_down_dgrad", dc, w_down, tb=True)
    return dh, d_wdown, dgq, dgkv, d_wuq, d_wukv, dgqn, dgqr, dgkn, dgkr, d_wout


def _loss_head(y, target):
    d = y.shape[1]

    def body(y_ref, t_ref, dy_ref, l_ref):
        err = y_ref[...] - t_ref[...]
        dy_ref[...] = err * (1.0 / d)
        part = 0.5 * jnp.sum(jnp.sum(err * err, axis=1, keepdims=True) * (1.0 / d), axis=0, keepdims=True)
        _acc(l_ref, jnp.broadcast_to(part, (1, LANES)))

    return _rows("loss_head", body, [y, target], [_sds(y.shape)], tt=512, accs=[_sds((1, LANES))])


MESH_ID = pl.DeviceIdType.MESH
HBM_SPEC = pl.BlockSpec(memory_space=pltpu.HBM)


def _all_gather(name, x):
    m_per, n = x.shape

    def body(x_ref, out_ref, send_sems, recv_sems, local_sem):
        x, y, c = lax.axis_index("x"), lax.axis_index("y"), lax.axis_index("c")
        me, sibling = (x, y, c), (x, y, 1 - c)
        chips = [(1 - x, y), (x, 1 - y), (1 - x, 1 - y)]

        def rows(px, py, pc):
            return out_ref.at[pl.ds((4 * px + 2 * py + pc) * m_per, m_per), :]

        def copy(k, block, to, src=None):
            return pltpu.make_async_remote_copy(
                src_ref=rows(*block) if src is None else src, dst_ref=rows(*block),
                send_sem=send_sems.at[k], recv_sem=recv_sems.at[k], device_id=to, device_id_type=MESH_ID)

        mine = pltpu.make_async_copy(x_ref, rows(*me), local_sem)
        mine.start()
        first = [copy(0, me, sibling, src=x_ref)]
        first += [copy(1 + j, me, (*chip, c), src=x_ref) for j, chip in enumerate(chips)]
        for cp in first:
            cp.start()
        passed = [copy(4 + j, (*chip, c), sibling) for j, chip in enumerate(chips)]
        for j, chip in enumerate(chips):
            copy(1 + j, (*chip, c), me).wait_recv()
            passed[j].start()
        copy(0, sibling, me).wait_recv()
        for j, chip in enumerate(chips):
            copy(4 + j, (*chip, 1 - c), me).wait_recv()
        for cp in first + passed:
            cp.wait_send()
        mine.wait()

    return pl.pallas_call(
        body, name=name,
        out_shape=jax.ShapeDtypeStruct((N_DEV * m_per, n), x.dtype),
        in_specs=[HBM_SPEC], out_specs=HBM_SPEC,
        scratch_shapes=[pltpu.SemaphoreType.DMA((7,)), pltpu.SemaphoreType.DMA((7,)), pltpu.SemaphoreType.DMA],
    )(x)


def _all_gather_groups(name, xs):
    ng = len(xs)

    def body(*refs):
        x_refs, out_refs = refs[:ng], refs[ng:2 * ng]
        send_sems, recv_sems, local_sems = refs[2 * ng:]
        x, y, c = lax.axis_index("x"), lax.axis_index("y"), lax.axis_index("c")
        me, sibling = (x, y, c), (x, y, 1 - c)
        chips = [(1 - x, y), (x, 1 - y), (1 - x, 1 - y)]

        def copy(g, k, block, to, src=None):
            px, py, pc = block
            dst = out_refs[g].at[4 * px + 2 * py + pc]
            return pltpu.make_async_remote_copy(
                src_ref=dst if src is None else src, dst_ref=dst,
                send_sem=send_sems.at[g, k], recv_sem=recv_sems.at[g, k], device_id=to, device_id_type=MESH_ID)

        mine = [pltpu.make_async_copy(x_refs[g], out_refs[g].at[4 * x + 2 * y + c], local_sems.at[g]) for g in range(ng)]
        for cp in mine:
            cp.start()
        first = []
        for g in range(ng):
            first.append(copy(g, 0, me, sibling, src=x_refs[g]))
            first += [copy(g, 1 + j, me, (*chip, c), src=x_refs[g]) for j, chip in enumerate(chips)]
        for cp in first:
            cp.start()
        passed = []
        for j, chip in enumerate(chips):
            for g in range(ng):
                copy(g, 1 + j, (*chip, c), me).wait_recv()
                passed.append(copy(g, 4 + j, (*chip, c), sibling))
                passed[-1].start()
        for g in range(ng):
            copy(g, 0, sibling, me).wait_recv()
            for j, chip in enumerate(chips):
                copy(g, 4 + j, (*chip, 1 - c), me).wait_recv()
        for cp in first + passed:
            cp.wait_send()
        for cp in mine:
            cp.wait()

    return pl.pallas_call(
        body, name=name,
        out_shape=[jax.ShapeDtypeStruct((N_DEV,) + x.shape, x.dtype) for x in xs],
        in_specs=[HBM_SPEC] * ng, out_specs=[HBM_SPEC] * ng,
        scratch_shapes=[pltpu.SemaphoreType.DMA((ng, 7)), pltpu.SemaphoreType.DMA((ng, 7)), pltpu.SemaphoreType.DMA((ng,))],
    )(*xs)


def _rs_cores(gs):
    ng = len(gs)

    def body(*refs):
        g_refs, recv_refs = refs[:ng], refs[ng:2 * ng]
        send_sems, recv_sems = refs[2 * ng:]
        x, y, c = lax.axis_index("x"), lax.axis_index("y"), lax.axis_index("c")
        copies = [pltpu.make_async_remote_copy(
            src_ref=g_refs[g].at[:, 1 - c], dst_ref=recv_refs[g], send_sem=send_sems.at[g], recv_sem=recv_sems.at[g],
            device_id=(x, y, 1 - c), device_id_type=MESH_ID) for g in range(ng)]
        for cp in copies:
            cp.start()
        for cp in copies:
            cp.wait()

    return pl.pallas_call(
        body, name="rs_cores",
        out_shape=[jax.ShapeDtypeStruct((g.shape[0],) + g.shape[2:], g.dtype) for g in gs],
        in_specs=[HBM_SPEC] * ng, out_specs=[HBM_SPEC] * ng,
        scratch_shapes=[pltpu.SemaphoreType.DMA((ng,)), pltpu.SemaphoreType.DMA((ng,))],
    )(*gs)


def _rs_add(g, recv, core):
    nchip, _, r, c_ = g.shape
    tr = _pick(r, (512, 384, 256, 128))

    def body(core_ref, g_ref, r_ref, p_ref, pb_ref):
        s = g_ref[0, 0] + r_ref[0]
        p_ref[0] = s
        pb_ref[0] = s.astype(BF16)

    return pl.pallas_call(
        body, name="rs_add",
        grid_spec=pltpu.PrefetchScalarGridSpec(
            num_scalar_prefetch=1, grid=(nchip, r // tr),
            in_specs=[pl.BlockSpec((1, 1, tr, c_), lambda k, i, core_ref: (k, core_ref[0], i, 0)),
                      pl.BlockSpec((1, tr, c_), lambda k, i, core_ref: (k, i, 0))],
            out_specs=[pl.BlockSpec((1, tr, c_), lambda k, i, core_ref: (k, i, 0)),
                       pl.BlockSpec((1, tr, c_), lambda k, i, core_ref: (k, i, 0))]),
        out_shape=[_sds((nchip, r, c_)), _sds((nchip, r, c_), BF16)],
        compiler_params=_cparams(dimension_semantics=("arbitrary", "arbitrary")),
    )(core, g, recv)


def _rs_chips(pbs):
    ng = len(pbs)

    def body(*refs):
        p_refs, recv_refs = refs[:ng], refs[ng:2 * ng]
        send_sems, recv_sems, local_sems = refs[2 * ng:]
        x, y, c = lax.axis_index("x"), lax.axis_index("y"), lax.axis_index("c")
        my = 2 * x + y
        chips = [(1 - x, y), (x, 1 - y), (1 - x, 1 - y)]
        local = [pltpu.make_async_copy(p_refs[g].at[my], recv_refs[g].at[my], local_sems.at[g]) for g in range(ng)]
        for cp in local:
            cp.start()

        def copy(g, k, src_chip, dst_chip, to):
            return pltpu.make_async_remote_copy(
                src_ref=p_refs[g].at[src_chip], dst_ref=recv_refs[g].at[dst_chip],
                send_sem=send_sems.at[g, k], recv_sem=recv_sems.at[g, k], device_id=to, device_id_type=MESH_ID)

        sends = [copy(g, k, 2 * px + py, my, (px, py, c)) for g in range(ng) for k, (px, py) in enumerate(chips)]
        for cp in sends:
            cp.start()
        for g in range(ng):
            for k, (px, py) in enumerate(chips):
                copy(g, k, my, 2 * px + py, (px, py, c)).wait_recv()
        for cp in sends:
            cp.wait_send()
        for cp in local:
            cp.wait()

    return pl.pallas_call(
        body, name="rs_chips", out_shape=[jax.ShapeDtypeStruct(p.shape, p.dtype) for p in pbs],
        in_specs=[HBM_SPEC] * ng, out_specs=[HBM_SPEC] * ng,
        scratch_shapes=[pltpu.SemaphoreType.DMA((ng, 3)), pltpu.SemaphoreType.DMA((ng, 3)), pltpu.SemaphoreType.DMA((ng,))],
    )(*pbs)


def _cols_from_shards(w, width):
    ns, r, cs = w.shape
    tr = _pick(r, (256, 128))

    def body(w_ref, o_ref):
        parts = [w_ref[j] for j in range(ns)]
        if width > ns * cs:
            parts.append(jnp.zeros((tr, width - ns * cs), w.dtype))
        o_ref[...] = jnp.concatenate(parts, axis=1)

    return pl.pallas_call(
        body, name="cols_from_shards", grid=(r // tr,),
        in_specs=[pl.BlockSpec((ns, tr, cs), lambda i: (0, i, 0))], out_specs=pl.BlockSpec((tr, width), lambda i: (i, 0)),
        out_shape=jax.ShapeDtypeStruct((r, width), w.dtype), compiler_params=_cparams(dimension_semantics=("arbitrary",)),
    )(w)


def _shards_from_cols(g, cs):
    r, width = g.shape
    tr = _pick(r, (256, 128))

    def body(g_ref, o_ref):
        for j in range(N_DEV):
            o_ref[j] = g_ref[:, j * cs:(j + 1) * cs]

    return pl.pallas_call(
        body, name="shards_from_cols", grid=(r // tr,),
        in_specs=[pl.BlockSpec((tr, width), lambda i: (i, 0))], out_specs=pl.BlockSpec((N_DEV, tr, cs), lambda i: (0, i, 0)),
        out_shape=jax.ShapeDtypeStruct((N_DEV, r, cs), g.dtype), compiler_params=_cparams(dimension_semantics=("arbitrary",)),
    )(g)


def _adam(w, g, m, v):
    m = ADAM_B1 * m + (1.0 - ADAM_B1) * g
    v = ADAM_B2 * v + (1.0 - ADAM_B2) * (g * g)
    m_hat = m / (1.0 - ADAM_B1 ** ADAM_STEP)
    v_hat = v / (1.0 - ADAM_B2 ** ADAM_STEP)
    return -ADAM_LR * (m_hat / (jnp.sqrt(v_hat) + ADAM_EPS) + ADAM_WD * w), m, v


def _sum_adam(p, recv, chip, w, m, v):
    nchip, r, c_ = recv.shape
    tr = _pick(r, (512, 384, 256, 128))

    def body(chip_ref, p_ref, r_ref, w_ref, m_ref, v_ref, g_ref, d_ref, mo_ref, vo_ref):
        my = chip_ref[0]
        g = jnp.where(my == 0, p_ref[0], r_ref[0].astype(F32))
        for k in range(1, nchip):
            g = g + jnp.where(my == k, p_ref[0], r_ref[k].astype(F32))
        g_ref[...] = g
        d_ref[...], mo_ref[...], vo_ref[...] = _adam(w_ref[...], g, m_ref[...], v_ref[...])

    row = pl.BlockSpec((tr, c_), lambda i, chip_ref: (i, 0))
    return pl.pallas_call(
        body, name="sum_adam",
        grid_spec=pltpu.PrefetchScalarGridSpec(
            num_scalar_prefetch=1, grid=(r // tr,),
            in_specs=[pl.BlockSpec((1, tr, c_), lambda i, chip_ref: (chip_ref[0], i, 0)),
                      pl.BlockSpec((nchip, tr, c_), lambda i, chip_ref: (0, i, 0)), row, row, row],
            out_specs=[row, row, row, row]),
        out_shape=[_sds((r, c_))] * 4,
        compiler_params=_cparams(dimension_semantics=("arbitrary",)),
    )(chip, p, recv, w, m, v)


def _sum_devices(gathered):
    m_all, n = gathered.shape
    m_per = m_all // N_DEV

    def body(x_ref, o_ref):
        s = x_ref[0:m_per, :]
        for j in range(1, N_DEV):
            s = s + x_ref[j * m_per:(j + 1) * m_per, :]
        o_ref[...] = s

    return pl.pallas_call(body, name="sum_devices", out_shape=_sds((m_per, n)), compiler_params=_cparams())(gathered)


def _adam_small(w, g, m, v):
    def body(w_ref, g_ref, m_ref, v_ref, d_ref, mo_ref, vo_ref):
        d_ref[...], mo_ref[...], vo_ref[...] = _adam(w_ref[...], g_ref[...], m_ref[...], v_ref[...])

    return pl.pallas_call(body, name="adam_small", out_shape=[_sds(w.shape)] * 3, compiler_params=_cparams())(w, g, m, v)


N_LAYERS = 4
_MIXER = ("dn", "sb", "mla")
_MIXER_PARAMS = {
    "dn": ("dn_w_in", "dn_conv_w", "dn_a_log", "dn_dt_bias", "dn_out_norm", "dn_w_out"),
    "sb": ("sb_w_qkv", "sb_q_norm", "sb_k_norm", "sb_w_out"),
    "mla": ("mla_w_down", "mla_q_a_norm", "mla_kv_a_norm", "mla_w_uq", "mla_w_ukv", "mla_q_nope_norm", "mla_q_rope_norm",
            "mla_k_nope_norm", "mla_k_rope_norm", "mla_w_out"),
}
_BIG_AXIS = {"dn_w_in": 1, "dn_w_out": 0, "sb_w_qkv": 1, "sb_w_out": 0, "mla_w_down": 0, "mla_w_uq": 1, "mla_w_ukv": 1,
             "mla_w_out": 0, "ffn_w_gate_up": 1, "ffn_w_down": 0}


def _weight_names():
    names = []
    for i in range(N_LAYERS):
        p = "l%d_" % i
        names += [p + "mix_norm"] + [p + n for n in _MIXER_PARAMS[_MIXER[i % 3]]] + [p + "ffn_norm", p + "ffn_w_gate_up", p + "ffn_w_down"]
    return names


WEIGHTS = _weight_names()
BIG = [n for n in WEIGHTS if n[3:] in _BIG_AXIS]
SMALL = [n for n in WEIGHTS if n[3:] not in _BIG_AXIS]
CONV = [n for n in SMALL if n.endswith("conv_w")]


def _ceil_to(n, k):
    return -(-n // k) * k


def _pack(arrs, cols, row_mult):
    parts = []
    for a in arrs:
        f = a.reshape(-1)
        parts.append(jnp.pad(f, (0, _ceil_to(f.shape[0], cols) - f.shape[0])))
    flat = jnp.concatenate(parts)
    rows = _ceil_to(flat.shape[0] // cols, row_mult)
    return jnp.pad(flat, (0, rows * cols - flat.shape[0])).reshape(rows, cols)


def _unpack(buf, shapes):
    cols = buf.shape[-1]
    out, r0 = [], 0
    for s in shapes:
        n = math.prod(s)
        nr = _ceil_to(n, cols) // cols
        out.append(buf[r0:r0 + nr].reshape(-1)[:n].reshape(s))
        r0 += nr
    return out


def _groups():
    by = {"gu": (704, []), "row": (1024, []), "dn_in": (514, []), "sb_qkv": (384, []), "mla": (512, [])}
    key = {"ffn_w_gate_up": "gu", "dn_w_in": "dn_in", "sb_w_qkv": "sb_qkv", "mla_w_down": "mla", "mla_w_uq": "mla", "mla_w_ukv": "mla"}
    for n in BIG:
        by[key.get(n[3:], "row")][1].append(n)
    return list(by.values())


GROUPS = _groups()


def _stack_group(grp, get):
    width, names = grp
    return jnp.concatenate([jnp.pad(get(n), ((0, 0), (0, width - get(n).shape[1]))) for n in names], axis=0)


def _unstack_group(grp, buf, shape_of):
    out, r0 = [], 0
    for n in grp[1]:
        rs, cs = shape_of(n)
        out.append(buf[..., r0:r0 + rs, :cs])
        r0 += rs
    return out


def kernel(x, l0_mix_norm, l0_dn_w_in, l0_dn_conv_w, l0_dn_a_log, l0_dn_dt_bias, l0_dn_out_norm, l0_dn_w_out, l0_ffn_norm, l0_ffn_w_gate_up, l0_ffn_w_down, l1_mix_norm, l1_sb_w_qkv, l1_sb_q_norm, l1_sb_k_norm, l1_sb_w_out, l1_ffn_norm, l1_ffn_w_gate_up, l1_ffn_w_down, l2_mix_norm, l2_mla_w_down, l2_mla_q_a_norm, l2_mla_kv_a_norm, l2_mla_w_uq, l2_mla_w_ukv, l2_mla_q_nope_norm, l2_mla_q_rope_norm, l2_mla_k_nope_norm, l2_mla_k_rope_norm, l2_mla_w_out, l2_ffn_norm, l2_ffn_w_gate_up, l2_ffn_w_down, l3_mix_norm, l3_dn_w_in, l3_dn_conv_w, l3_dn_a_log, l3_dn_dt_bias, l3_dn_out_norm, l3_dn_w_out, l3_ffn_norm, l3_ffn_w_gate_up, l3_ffn_w_down, loss_target, m_l0_mix_norm, m_l0_dn_w_in, m_l0_dn_conv_w, m_l0_dn_a_log, m_l0_dn_dt_bias, m_l0_dn_out_norm, m_l0_dn_w_out, m_l0_ffn_norm, m_l0_ffn_w_gate_up, m_l0_ffn_w_down, m_l1_mix_norm, m_l1_sb_w_qkv, m_l1_sb_q_norm, m_l1_sb_k_norm, m_l1_sb_w_out, m_l1_ffn_norm, m_l1_ffn_w_gate_up, m_l1_ffn_w_down, m_l2_mix_norm, m_l2_mla_w_down, m_l2_mla_q_a_norm, m_l2_mla_kv_a_norm, m_l2_mla_w_uq, m_l2_mla_w_ukv, m_l2_mla_q_nope_norm, m_l2_mla_q_rope_norm, m_l2_mla_k_nope_norm, m_l2_mla_k_rope_norm, m_l2_mla_w_out, m_l2_ffn_norm, m_l2_ffn_w_gate_up, m_l2_ffn_w_down, m_l3_mix_norm, m_l3_dn_w_in, m_l3_dn_conv_w, m_l3_dn_a_log, m_l3_dn_dt_bias, m_l3_dn_out_norm, m_l3_dn_w_out, m_l3_ffn_norm, m_l3_ffn_w_gate_up, m_l3_ffn_w_down, v_l0_mix_norm, v_l0_dn_w_in, v_l0_dn_conv_w, v_l0_dn_a_log, v_l0_dn_dt_bias, v_l0_dn_out_norm, v_l0_dn_w_out, v_l0_ffn_norm, v_l0_ffn_w_gate_up, v_l0_ffn_w_down, v_l1_mix_norm, v_l1_sb_w_qkv, v_l1_sb_q_norm, v_l1_sb_k_norm, v_l1_sb_w_out, v_l1_ffn_norm, v_l1_ffn_w_gate_up, v_l1_ffn_w_down, v_l2_mix_norm, v_l2_mla_w_down, v_l2_mla_q_a_norm, v_l2_mla_kv_a_norm, v_l2_mla_w_uq, v_l2_mla_w_ukv, v_l2_mla_q_nope_norm, v_l2_mla_q_rope_norm, v_l2_mla_k_nope_norm, v_l2_mla_k_rope_norm, v_l2_mla_w_out, v_l2_ffn_norm, v_l2_ffn_w_gate_up, v_l2_ffn_w_down, v_l3_mix_norm, v_l3_dn_w_in, v_l3_dn_conv_w, v_l3_dn_a_log, v_l3_dn_dt_bias, v_l3_dn_out_norm, v_l3_dn_w_out, v_l3_ffn_norm, v_l3_ffn_w_gate_up, v_l3_ffn_w_down):
    a = dict(locals())
    return _train_step(a)


def _train_step(a):
    mx, my, mc = lax.axis_index("x"), lax.axis_index("y"), lax.axis_index("c")
    dev = 4 * mx + 2 * my + mc
    core = jnp.reshape(mc, (1,)).astype(jnp.int32)
    chip = jnp.reshape(2 * mx + my, (1,)).astype(jnp.int32)
    t, d = a["x"].shape[1], a["x"].shape[2]
    xs = a["x"].reshape(t, d)
    target = a["loss_target"].reshape(t, d)

    gathered = _all_gather_groups("gather_weights", [_stack_group(grp, lambda n: a[n].astype(BF16)) for grp in GROUPS])
    full = {}
    for grp, buf in zip(GROUPS, gathered):
        for n, shards in zip(grp[1], _unstack_group(grp, buf, lambda n: a[n].shape)):
            kind = n[3:]
            if kind == "ffn_w_gate_up":
                full[n] = shards
            elif _BIG_AXIS[kind] == 0:
                full[n] = shards.reshape(N_DEV * shards.shape[1], shards.shape[2])
            else:
                width = DN_PROJ if kind == "dn_w_in" else N_DEV * shards.shape[2]
                full[n] = _cols_from_shards(shards, width)
    conv_pack = _pack([a[n] for n in CONV], LANES, 8)
    conv_all = _all_gather("gather_conv", conv_pack).reshape(N_DEV, conv_pack.shape[0], LANES)
    for n, parts in zip(CONV, zip(*[_unpack(conv_all[j], [a[n].shape for n in CONV]) for j in range(N_DEV)])):
        full[n] = jnp.concatenate(parts, axis=1)

    def vec(n):
        return a[n].reshape(1, -1)

    cos, sin = _rope_tables(t)
    layer_args = []
    for i in range(N_LAYERS):
        p = "l%d_" % i
        kind = _MIXER[i % 3]
        if kind == "dn":
            args = (full[p + "dn_w_in"], full[p + "dn_conv_w"], _pad_lanes(a[p + "dn_a_log"]), _pad_lanes(a[p + "dn_dt_bias"]),
                    vec(p + "dn_out_norm"), full[p + "dn_w_out"])
        elif kind == "sb":
            args = (full[p + "sb_w_qkv"], vec(p + "sb_q_norm"), vec(p + "sb_k_norm"), full[p + "sb_w_out"])
        else:
            w_down, w_uq, w_ukv = _mla_layout(full[p + "mla_w_down"], full[p + "mla_w_uq"], full[p + "mla_w_ukv"])
            args = (w_down, vec(p + "mla_q_a_norm"), vec(p + "mla_kv_a_norm"), w_uq, w_ukv, vec(p + "mla_q_nope_norm"),
                    _pad_lanes(a[p + "mla_q_rope_norm"]), vec(p + "mla_k_nope_norm"), _pad_lanes(a[p + "mla_k_rope_norm"]),
                    full[p + "mla_w_out"], cos, sin)
        layer_args.append((kind, args))

    fwd = {"dn": _dn_fwd, "sb": _sb_fwd, "mla": _mla_fwd}
    bwd = {"dn": _dn_bwd, "sb": _sb_bwd, "mla": _mla_bwd}
    saved = []
    for i, (kind, args) in enumerate(layer_args):
        p = "l%d_" % i
        h = _rmsnorm_fwd("mix_norm", xs, vec(p + "mix_norm"))
        x_mid, sv_mix = fwd[kind](xs, h, *args)
        x_out, sv_ffn = _ffn_fwd(x_mid, vec(p + "ffn_norm"), full[p + "ffn_w_gate_up"], full[p + "ffn_w_down"])
        saved.append((xs, sv_mix, sv_ffn))
        xs = x_out
    dy, loss_part = _loss_head(xs, target)

    grads = {}
    for i in reversed(range(N_LAYERS)):
        p = "l%d_" % i
        kind, args = layer_args[i]
        x_in, sv_mix, sv_ffn = saved[i]
        dx_mid, grads[p + "ffn_norm"], grads[p + "ffn_w_gate_up"], grads[p + "ffn_w_down"] = _ffn_bwd(
            sv_ffn, dy, vec(p + "ffn_norm"), full[p + "ffn_w_gate_up"], full[p + "ffn_w_down"])
        res = bwd[kind](sv_mix, dx_mid, *args)
        dh = res[0]
        if kind == "mla":
            res = list(res)
            res[1], res[4], res[5] = _mla_unlayout(res[1], res[4], res[5])
        for n, g in zip(_MIXER_PARAMS[kind], res[1:]):
            grads[p + n] = g
        dy, grads[p + "mix_norm"] = _rmsnorm_bwd("mix_norm_bwd", x_in, vec(p + "mix_norm"), dh, dx_mid)
    grad_x = dy.reshape(a["x"].shape)

    small_full_shapes = [full[n].shape if n in CONV else a[n].shape for n in SMALL]
    small_grads = []
    for n, s in zip(SMALL, small_full_shapes):
        g = grads[n].reshape(-1)
        small_grads.append(g[:math.prod(s)])
    small_pack = _pack(small_grads + [loss_part.reshape(-1)], LANES, 8)
    small_sum = _sum_devices(_all_gather("gather_small_grads", small_pack))
    small_red = _unpack(small_sum, small_full_shapes + [(LANES,)])
    loss = small_red[-1][0]
    g_small = {}
    for n, g in zip(SMALL, small_red[:-1]):
        if n in CONV:
            cs = a[n].shape[1]
            g = lax.dynamic_slice_in_dim(g, dev * cs, cs, axis=1)
        g_small[n] = g
    small_shapes = [a[n].shape for n in SMALL]
    packs = [_pack([src[n] for n in SMALL], LANES, 8) for src in
             ({n: a[n] for n in SMALL}, g_small, {n: a["m_" + n] for n in SMALL}, {n: a["v_" + n] for n in SMALL})]
    d_small, m_small, v_small = (_unpack(o, small_shapes) for o in _adam_small(*packs))

    def grad_shards(n):
        g, (rs, cs) = grads[n], a[n].shape
        if g.ndim == 3:
            return g
        if _BIG_AXIS[n[3:]] == 0:
            return g.reshape(N_DEV, rs, cs)
        return _shards_from_cols(g, cs)

    g_groups = []
    for grp in GROUPS:
        parts = [jnp.pad(grad_shards(n), ((0, 0), (0, 0), (0, grp[0] - a[n].shape[1]))) for n in grp[1]]
        g_all = jnp.concatenate(parts, axis=1)
        g_groups.append(g_all.reshape((N_DEV // 2, 2) + g_all.shape[1:]))
    from_sibling = _rs_cores(g_groups)
    parts = [_rs_add(g, r, core) for g, r in zip(g_groups, from_sibling)]
    from_chips = _rs_chips([pb for _, pb in parts])
    big_out = {}
    for grp, (part, _), recv in zip(GROUPS, parts, from_chips):
        packs = [_stack_group(grp, lambda n, pre=pre: a[pre + n]) for pre in ("", "m_", "v_")]
        outs = [_unstack_group(grp, o, lambda n: a[n].shape) for o in _sum_adam(part, recv, chip, *packs)]
        for i, n in enumerate(grp[1]):
            big_out[n] = [o[i] for o in outs]

    small_out = dict(zip(SMALL, zip([g_small[n] for n in SMALL], d_small, m_small, v_small)))

    def out(k):
        return [small_out[n][k] if n in small_out else big_out[n][k] for n in WEIGHTS]

    return (loss, grad_x, *out(0), *out(1), *out(2), *out(3))
```

```python
import math

import jax
import jax.numpy as jnp
from jax import lax
from jax.experimental import pallas as pl
from jax.experimental.pallas import tpu as pltpu

F32 = jnp.float32
BF16 = jnp.bfloat16
F32X3 = lax.Precision.HIGH

LANES = 128
N_DEV = 8
N_HEADS = 8
HEAD = 128
NORM_EPS = 1e-6
DN_CHUNK = 64
ATT_BLOCK = 512
ATT_Q = 512
MLA_ROPE = 64
MLA_QK = 192
ROPE_THETA = 10000.0
VMEM_LIMIT = 56 * 1024 * 1024

ADAM_LR = 0.001
ADAM_B1 = 0.9
ADAM_B2 = 0.999
ADAM_EPS = 1e-08
ADAM_WD = 0.01
ADAM_STEP = 10


def _cparams(**kw):
    return pltpu.CompilerParams(vmem_limit_bytes=VMEM_LIMIT, **kw)


def _pick(n, cands):
    for c in cands:
        if c <= n and n % c == 0:
            return c
    return n


def _mm(name, a, b, *, ta=False, tb=False, out_dtype=F32, add=None, tm=None, tn=None, tk=None):
    if ta:
        K, M = a.shape
    else:
        M, K = a.shape
    N = b.shape[0] if tb else b.shape[1]
    tm = tm or _pick(M, (1024, 512, 256, 128))
    tn = tn or _pick(N, (1024, 512, 384, 256, 128))
    tk = tk or _pick(K, (1024, 1408, 512, 384, 256, 128))
    return _mm_raw(
        name, a, b, ta=ta, tb=tb, out_dtype=out_dtype, add=add, grid=(M // tm, N // tn, K // tk), out_shape=(M, N),
        a_block=(tk, tm) if ta else (tm, tk), a_map=(lambda i, j, k: (k, i)) if ta else (lambda i, j, k: (i, k)),
        b_block=(tn, tk) if tb else (tk, tn), b_map=(lambda i, j, k: (j, k)) if tb else (lambda i, j, k: (k, j)),
        o_block=(tm, tn), o_map=lambda i, j, k: (i, j))


def _mm_raw(name, a, b, *, ta, tb, out_dtype, add, grid, out_shape, a_block, a_map, b_block, b_map, o_block, o_map):
    nk = grid[2]
    tm, tn = o_block
    dn = (((0 if ta else 1,), (1 if tb else 0,)), ((), ()))
    has_add = add is not None

    def kern(*refs):
        if has_add:
            a_ref, b_ref, add_ref, o_ref, acc_ref = refs
        else:
            a_ref, b_ref, o_ref, acc_ref = refs
        k = pl.program_id(2)
        part = lax.dot_general(a_ref[...].astype(BF16), b_ref[...].astype(BF16), dn, preferred_element_type=F32)

        @pl.when(k == 0)
        def _():
            acc_ref[...] = part

        @pl.when(k > 0)
        def _():
            acc_ref[...] += part

        @pl.when(k == nk - 1)
        def _():
            r = acc_ref[...]
            if has_add:
                r = r + add_ref[...]
            o_ref[...] = r.astype(out_dtype)

    in_specs = [pl.BlockSpec(a_block, a_map), pl.BlockSpec(b_block, b_map)]
    args = [a, b]
    if has_add:
        in_specs.append(pl.BlockSpec(o_block, o_map))
        args.append(add)
    return pl.pallas_call(
        kern, name=name,
        grid=grid,
        in_specs=in_specs,
        out_specs=pl.BlockSpec(o_block, o_map),
        out_shape=jax.ShapeDtypeStruct(out_shape, out_dtype),
        scratch_shapes=[pltpu.VMEM((tm, tn), F32)],
        compiler_params=_cparams(dimension_semantics=("parallel", "parallel", "arbitrary")),
    )(*args)


def _rows(name, body, ins, outs, *, tt, consts=(), accs=()):
    in_specs, args = [], []
    first = ins[0][0] if isinstance(ins[0], tuple) else ins[0]
    t = first.shape[-2]
    tt = min(tt, t)
    for x in ins:
        if isinstance(x, tuple):
            arr, bs, im = x
            in_specs.append(pl.BlockSpec(bs, im))
            args.append(arr)
        else:
            in_specs.append(_row_spec(x.shape, tt))
            args.append(x)
    for c in consts:
        in_specs.append(pl.BlockSpec(c.shape, lambda i, _n=c.ndim: (0,) * _n))
        args.append(c)
    out_specs = [_row_spec(o.shape, tt) for o in outs]
    out_specs += [pl.BlockSpec(a.shape, lambda i, _n=len(a.shape): (0,) * _n) for a in accs]
    res = pl.pallas_call(
        body, name=name, grid=(t // tt,),
        in_specs=in_specs, out_specs=out_specs, out_shape=list(outs) + list(accs),
        compiler_params=_cparams(dimension_semantics=("arbitrary",)),
    )(*args)
    return res


def _row_spec(shape, tt):
    if len(shape) == 2:
        return pl.BlockSpec((tt, shape[1]), lambda i: (i, 0))
    return pl.BlockSpec((shape[0], tt, shape[2]), lambda i: (0, i, 0))


def _sds(shape, dtype=F32):
    return jax.ShapeDtypeStruct(tuple(shape), dtype)


def _acc(ref, val):
    i = pl.program_id(0)

    @pl.when(i == 0)
    def _():
        ref[...] = val

    @pl.when(i > 0)
    def _():
        ref[...] += val


def _rms(x, g):
    return x * lax.rsqrt(jnp.mean(x * x, axis=-1, keepdims=True) + NORM_EPS) * g


def _silu(x):
    return x / (1.0 + jnp.exp(-x))


def _softplus(x):
    return jnp.maximum(x, 0.0) + jnp.log(1.0 + jnp.exp(-jnp.abs(x)))


def _sigmoid(x):
    return 1.0 / (1.0 + jnp.exp(-x))


def _rmsnorm_fwd(name, x, g, tt=512):
    def body(x_ref, g_ref, h_ref):
        h_ref[...] = _rms(x_ref[...], g_ref[...]).astype(BF16)

    return _rows(name, body, [x], [_sds(x.shape, BF16)], tt=tt, consts=[g])[0]


def _rmsnorm_bwd(name, x, g, dh, dres, tt=512):
    def body(x_ref, dh_ref, dres_ref, g_ref, dx_ref, dg_ref):
        _, vjp = jax.vjp(_rms, x_ref[...], g_ref[...])
        dx, dg = vjp(dh_ref[...])
        dx_ref[...] = dx + dres_ref[...]
        _acc(dg_ref, dg)

    return _rows(name, body, [x, dh, dres], [_sds(x.shape)], tt=tt, consts=[g], accs=[_sds(g.shape)])


def _ffn_fwd(x, norm_g, w3, w_down):
    t, d = x.shape
    ns, _, cs = w3.shape
    half = ns // 2
    w2 = w3.reshape(ns * d, cs)
    h = _rmsnorm_fwd("ffn_norm", x, norm_g)
    tm = _pick(t, (1024, 512, 256, 128))
    nm = t // tm

    def gate_up(h_ref, wg_ref, wu_ref, g_ref, u_ref, a_ref):
        hv = h_ref[...]
        g = _dot(hv, wg_ref[...])
        u = _dot(hv, wu_ref[...])
        g_ref[...] = g
        u_ref[...] = u
        a_ref[...] = (_silu(g) * u).astype(BF16)

    hid = pl.BlockSpec((tm, cs), lambda j, i: (j * nm + i, 0))
    g, u, act = pl.pallas_call(
        gate_up, name="ffn_gate_up", grid=(half, nm),
        in_specs=[pl.BlockSpec((tm, d), lambda j, i: (i, 0)), pl.BlockSpec((d, cs), lambda j, i: (j, 0)),
                  pl.BlockSpec((d, cs), lambda j, i: (j + half, 0))],
        out_specs=[hid, hid, hid], out_shape=[_sds((half * t, cs)), _sds((half * t, cs)), _sds((half * t, cs), BF16)],
        compiler_params=_cparams(dimension_semantics=("parallel", "arbitrary")),
    )(h, w2, w2)
    y = _mm_raw("ffn_down", act, w_down, ta=False, tb=False, out_dtype=F32, add=x, grid=(nm, 1, half), out_shape=(t, d),
                a_block=(tm, cs), a_map=lambda i, j, k: (k * nm + i, 0), b_block=(cs, d), b_map=lambda i, j, k: (k, 0),
                o_block=(tm, d), o_map=lambda i, j, k: (i, 0))
    return y, (x, h, g, u, act)


def _ffn_bwd(saved, dy, norm_g, w3, w_down):
    x, h, g, u, act = saved
    t, d = x.shape
    ns, _, cs = w3.shape
    half = ns // 2
    w2 = w3.reshape(ns * d, cs)
    tm = _pick(t, (1024, 512, 256, 128))
    nm = t // tm
    tk = _pick(t, (1024, 512, 256, 128))
    nk = t // tk
    d_wdown = _mm_raw("ffn_down_wgrad", act, dy, ta=True, tb=False, out_dtype=F32, add=None, grid=(half, 1, nk),
                      out_shape=(half * cs, d), a_block=(tk, cs), a_map=lambda i, j, k: (i * nk + k, 0),
                      b_block=(tk, d), b_map=lambda i, j, k: (k, 0), o_block=(cs, d), o_map=lambda i, j, k: (i, 0))
    def down_dgrad(dy_ref, wd_ref, g_ref, u_ref, dg_ref, du_ref):
        da = _dot_nt(dy_ref[...].astype(BF16), wd_ref[...])
        gv, uv = g_ref[...], u_ref[...]
        s = _sigmoid(gv)
        dg_ref[...] = (da * uv * s * (1.0 + gv * (1.0 - s))).astype(BF16)
        du_ref[...] = (da * gv * s).astype(BF16)

    hid = pl.BlockSpec((tm, cs), lambda j, i: (j * nm + i, 0))
    dg, du = pl.pallas_call(
        down_dgrad, name="ffn_down_dgrad", grid=(half, nm),
        in_specs=[pl.BlockSpec((tm, d), lambda j, i: (i, 0)), pl.BlockSpec((cs, d), lambda j, i: (j, 0)), hid, hid],
        out_specs=[hid, hid], out_shape=[_sds((half * t, cs), BF16), _sds((half * t, cs), BF16)],
        compiler_params=_cparams(dimension_semantics=("parallel", "arbitrary")),
    )(dy, w_down, g, u)

    def wgrad(name, dd):
        return _mm_raw(name, h, dd, ta=True, tb=False, out_dtype=F32, add=None, grid=(1, half, nk), out_shape=(half * d, cs),
                       a_block=(tk, d), a_map=lambda i, j, k: (k, 0), b_block=(tk, cs), b_map=lambda i, j, k: (j * nk + k, 0),
                       o_block=(d, cs), o_map=lambda i, j, k: (j, 0))

    def dgrad(name, dd, off, add):
        return _mm_raw(name, dd, w2, ta=False, tb=True, out_dtype=F32, add=add, grid=(nm, 1, half), out_shape=(t, d),
                       a_block=(tm, cs), a_map=lambda i, j, k: (k * nm + i, 0), b_block=(d, cs), b_map=lambda i, j, k: (k + off, 0),
                       o_block=(tm, d), o_map=lambda i, j, k: (i, 0))

    d_w3 = jnp.concatenate([wgrad("ffn_gate_wgrad", dg), wgrad("ffn_up_wgrad", du)], axis=0).reshape(ns, d, cs)
    dh = dgrad("ffn_up_dgrad", du, half, dgrad("ffn_gate_dgrad", dg, 0, None))
    dx, dgain = _rmsnorm_bwd("ffn_norm_bwd", x, norm_g, dh, dy)
    return dx, dgain, d_w3, d_wdown


def _dot_nt(a, b):
    return lax.dot_general(a, b, (((1,), (1,)), ((), ())), preferred_element_type=F32)


def _dot_tn(a, b):
    return lax.dot_general(a, b, (((0,), (0,)), ((), ())), preferred_element_type=F32)


def _dot(a, b):
    return jnp.dot(a, b, preferred_element_type=F32)


CUM_BLOCK = 128


def _tri2(lower):
    r = lax.broadcasted_iota(jnp.int32, (CUM_BLOCK, CUM_BLOCK), 0)
    c = lax.broadcasted_iota(jnp.int32, (CUM_BLOCK, CUM_BLOCK), 1)
    tri = ((r > c) if lower else (r < c)).astype(BF16)
    return jnp.concatenate([tri, tri], axis=0)


def _run_sums(x, tri2, run, reverse):
    nb = x.shape[1] // CUM_BLOCK
    outs = [None] * nb
    for j in (reversed(range(nb)) if reverse else range(nb)):
        xj = x[:, j * CUM_BLOCK:(j + 1) * CUM_BLOCK]
        hi = xj.astype(BF16)
        lo = (xj - hi.astype(F32)).astype(BF16)
        outs[j] = _dot(jnp.concatenate([hi, lo], axis=1), tri2) + run
        run = run + jnp.sum(xj, axis=1, keepdims=True)
    return jnp.concatenate(outs, axis=1), run


def _log_sigmoid(z):
    return jnp.minimum(z, 0.0) - jnp.log(1.0 + jnp.exp(-jnp.abs(z)))


def _heads_in(ref, h, width=HEAD):
    return ref[:, h * width:(h + 1) * width]


def _sb_qk(q, k, gq, gk):
    return _rms(q, gq) * (HEAD ** -0.5), _rms(k, gk)


def _sb_prep_fwd(qkv, gq, gk):
    t = qkv.shape[0]

    def body(x_ref, gq_ref, gk_ref, q_ref, k_ref, v_ref):
        for h in range(N_HEADS):
            q, k = _sb_qk(_heads_in(x_ref, h), _heads_in(x_ref, N_HEADS + h), gq_ref[...], gk_ref[...])
            q_ref[h] = q.astype(BF16)
            k_ref[h] = k.astype(BF16)
            v_ref[h] = _heads_in(x_ref, 2 * N_HEADS + h).astype(BF16)

    hm = _sds((N_HEADS, t, HEAD), BF16)
    return _rows("sb_prep", body, [qkv], [hm, hm, hm], tt=256, consts=[gq, gk])


def _sb_prep_bwd(qkv, gq, gk, dq, dk, dv):
    def body(x_ref, dq_ref, dk_ref, dv_ref, gq_ref, gk_ref, dx_ref, dgq_ref, dgk_ref):
        dgq = jnp.zeros(gq_ref.shape, F32)
        dgk = jnp.zeros(gk_ref.shape, F32)
        for h in range(N_HEADS):
            _, vjp = jax.vjp(_sb_qk, _heads_in(x_ref, h), _heads_in(x_ref, N_HEADS + h), gq_ref[...], gk_ref[...])
            a, b, c, d = vjp((dq_ref[h], dk_ref[h]))
            dx_ref[:, h * HEAD:(h + 1) * HEAD] = a.astype(BF16)
            dx_ref[:, (N_HEADS + h) * HEAD:(N_HEADS + h + 1) * HEAD] = b.astype(BF16)
            dx_ref[:, (2 * N_HEADS + h) * HEAD:(2 * N_HEADS + h + 1) * HEAD] = dv_ref[h].astype(BF16)
            dgq, dgk = dgq + c, dgk + d
        _acc(dgq_ref, dgq)
        _acc(dgk_ref, dgk)

    return _rows("sb_prep_bwd", body, [qkv, dq, dk, dv], [_sds(qkv.shape, BF16)], tt=256, consts=[gq, gk],
                 accs=[_sds(gq.shape), _sds(gk.shape)])


def _q_block(t):
    return min(ATT_Q, t)


def _key_order(bq, qb):
    rows = lax.broadcasted_iota(jnp.int32, (bq, ATT_BLOCK), 0)
    cols = lax.broadcasted_iota(jnp.int32, (bq, ATT_BLOCK), 1)
    return rows - cols + qb * bq


def _sb_attn_fwd(q, k, v):
    nh, t, _ = q.shape
    bq = _q_block(t)
    per = bq // ATT_BLOCK

    def kern(q_ref, k_ref, v_ref, o_ref):
        qb = pl.program_id(1)
        qv = q_ref[0]
        after = _tri2(True)
        order = _key_order(bq, qb)
        nkb = (qb + 1) * per

        def body(i, carry):
            o_acc, run = carry
            kb = nkb - 1 - i
            off = pl.multiple_of(kb * ATT_BLOCK, ATT_BLOCK)
            kv = k_ref[0, pl.ds(off, ATT_BLOCK), :]
            vv = v_ref[0, pl.ds(off, ATT_BLOCK), :]
            z = _dot_nt(qv, kv)
            past = order > kb * ATT_BLOCK
            lsz = _log_sigmoid(z)
            lsn = jnp.where(past, lsz - z, 0.0)
            la, run = _run_sums(lsn, after, run, True)
            a = jnp.where(past, jnp.exp(lsz + la), 0.0)
            o_acc = o_acc + _dot(a.astype(BF16), vv)
            return o_acc, run

        o, _ = lax.fori_loop(0, nkb, body, (jnp.zeros((bq, HEAD), F32), jnp.zeros((bq, 1), F32)))
        o_ref[...] = o

    return pl.pallas_call(
        kern, name="sb_attn_fwd", grid=(nh, t // bq),
        in_specs=[pl.BlockSpec((1, bq, HEAD), lambda h, i: (h, i, 0)),
                  pl.BlockSpec((1, t, HEAD), lambda h, i: (h, 0, 0)),
                  pl.BlockSpec((1, t, HEAD), lambda h, i: (h, 0, 0))],
        out_specs=pl.BlockSpec((bq, HEAD), lambda h, i: (i, h)),
        out_shape=_sds((t, nh * HEAD)),
        compiler_params=_cparams(dimension_semantics=("parallel", "arbitrary")),
    )(q, k, v)


def _sb_attn_bwd(q, k, v, do):
    nh, t, _ = q.shape
    bq = _q_block(t)
    per = bq // ATT_BLOCK

    def kern(q_ref, k_ref, v_ref, do_ref, dq_ref, dk_ref, dv_ref, g_s, ls_s):
        qb = pl.program_id(1)

        @pl.when(qb == 0)
        def _():
            dk_ref[...] = jnp.zeros(dk_ref.shape, F32)
            dv_ref[...] = jnp.zeros(dv_ref.shape, F32)

        qv = q_ref[0]
        dob = do_ref[...].astype(BF16)
        after, before = _tri2(True), _tri2(False)
        order = _key_order(bq, qb)
        nkb = (qb + 1) * per

        def sweep_left(i, run):
            kb = nkb - 1 - i
            off = pl.multiple_of(kb * ATT_BLOCK, ATT_BLOCK)
            kv = k_ref[0, pl.ds(off, ATT_BLOCK), :]
            vv = v_ref[0, pl.ds(off, ATT_BLOCK), :]
            z = _dot_nt(qv, kv)
            past = order > kb * ATT_BLOCK
            lsz = _log_sigmoid(z)
            lsn = jnp.where(past, lsz - z, 0.0)
            la, run = _run_sums(lsn, after, run, True)
            a = jnp.where(past, jnp.exp(lsz + la), 0.0)
            g_s[kb] = _dot_nt(dob, vv) * a
            ls_s[kb] = lsz
            dv_ref[0, pl.ds(off, ATT_BLOCK), :] += _dot_tn(a.astype(BF16), dob)
            return run

        zero = jnp.zeros((bq, 1), F32)
        lax.fori_loop(0, nkb, sweep_left, zero)

        def sweep_right(kb, carry):
            dq_acc, run_g = carry
            off = pl.multiple_of(kb * ATT_BLOCK, ATT_BLOCK)
            kv = k_ref[0, pl.ds(off, ATT_BLOCK), :]
            g = g_s[kb]
            sg = jnp.exp(ls_s[kb])
            past = order > kb * ATT_BLOCK
            dls, run_g = _run_sums(g, before, run_g, False)
            dzb = jnp.where(past, g * (1.0 - sg) - dls * sg, 0.0).astype(BF16)
            dk_ref[0, pl.ds(off, ATT_BLOCK), :] += _dot_tn(dzb, qv)
            return dq_acc + _dot(dzb, kv), run_g

        dq, _ = lax.fori_loop(0, nkb, sweep_right, (jnp.zeros((bq, HEAD), F32), zero))
        dq_ref[0] = dq

    hm = _sds((nh, t, HEAD))
    full = pl.BlockSpec((1, t, HEAD), lambda h, i: (h, 0, 0))
    tok = pl.BlockSpec((bq, HEAD), lambda h, i: (i, h))
    nkb_max = t // ATT_BLOCK
    return pl.pallas_call(
        kern, name="sb_attn_bwd", grid=(nh, t // bq),
        in_specs=[pl.BlockSpec((1, bq, HEAD), lambda h, i: (h, i, 0)), full, full, tok],
        out_specs=[pl.BlockSpec((1, bq, HEAD), lambda h, i: (h, i, 0)), full, full],
        out_shape=[hm, hm, hm],
        scratch_shapes=[pltpu.VMEM((nkb_max, bq, ATT_BLOCK), F32), pltpu.VMEM((nkb_max, bq, ATT_BLOCK), F32)],
        compiler_params=_cparams(dimension_semantics=("parallel", "arbitrary")),
    )(q, k, v, do)


def _sb_fwd(x, h, w_qkv, gq, gk, w_out):
    qkv = _mm("sb_qkv", h, w_qkv)
    q, k, v = _sb_prep_fwd(qkv, gq, gk)
    o = _sb_attn_fwd(q, k, v)
    y = _mm("sb_out", o, w_out, add=x)
    return y, (h, qkv, q, k, v, o)


def _sb_bwd(saved, dy, w_qkv, gq, gk, w_out):
    h, qkv, q, k, v, o = saved
    d_wout = _mm("sb_out_wgrad", o, dy, ta=True)
    do = _mm("sb_out_dgrad", dy, w_out, tb=True)
    dq, dk, dv = _sb_attn_bwd(q, k, v, do)
    dqkv, dgq, dgk = _sb_prep_bwd(qkv, gq, gk, dq, dk, dv)
    d_wqkv = _mm("sb_qkv_wgrad", h, dqkv, ta=True)
    dh = _mm("sb_qkv_dgrad", dqkv, w_qkv, tb=True)
    return dh, d_wqkv, dgq, dgk, d_wout


DN_QKV = 3 * N_HEADS * HEAD
DN_PROJ = DN_QKV + N_HEADS * HEAD + LANES
DN_CONV = 4
HALO = 8
CONV_COLS = 512


def _dn_conv_fwd(proj, conv_w, tt=256):
    t = proj.shape[0]
    tt = min(tt, t)

    def body(u_ref, prev_ref, w_ref, c_ref):
        i = pl.program_id(0)
        for cc in range(DN_QKV // CONV_COLS):
            cs = slice(cc * CONV_COLS, (cc + 1) * CONV_COLS)
            cur = u_ref[:, cs]
            prev = jnp.where(i > 0, prev_ref[:, cs], 0.0)
            ext = jnp.concatenate([prev, cur], axis=0)
            y = cur * w_ref[DN_CONV - 1:DN_CONV, cs]
            for j in range(DN_CONV - 1):
                y = y + pltpu.roll(ext, DN_CONV - 1 - j, 0)[HALO:] * w_ref[j:j + 1, cs]
            c_ref[:, cs] = y

    return _rows("dn_conv", body,
                 [(proj, (tt, DN_QKV), lambda i: (i, 0)),
                  (proj, (HALO, DN_QKV), lambda i: (jnp.maximum(i * (tt // HALO) - 1, 0), 0))],
                 [_sds((t, DN_QKV))], tt=tt, consts=[conv_w])[0]


def _dn_conv_bwd(proj, conv_w, dc, dz, dab, tt=256):
    t = proj.shape[0]
    tt = min(tt, t)
    nblk = t // tt

    def body(u_ref, prev_ref, dc_ref, next_ref, dz_ref, dab_ref, w_ref, dp_ref, dw_ref):
        i = pl.program_id(0)
        dws = []
        for cc in range(DN_QKV // CONV_COLS):
            cs = slice(cc * CONV_COLS, (cc + 1) * CONV_COLS)
            cur = u_ref[:, cs]
            prev = jnp.where(i > 0, prev_ref[:, cs], 0.0)
            ext_u = jnp.concatenate([prev, cur], axis=0)
            d = dc_ref[:, cs]
            nxt = jnp.where(i < nblk - 1, next_ref[:, cs], 0.0)
            ext_d = jnp.concatenate([d, nxt], axis=0)
            du = d * w_ref[DN_CONV - 1:DN_CONV, cs]
            rows = [jnp.sum(d * cur, axis=0, keepdims=True)]
            for j in range(DN_CONV - 2, -1, -1):
                sh = DN_CONV - 1 - j
                du = du + pltpu.roll(ext_d, tt + HALO - sh, 0)[:tt] * w_ref[j:j + 1, cs]
                rows.insert(0, jnp.sum(d * pltpu.roll(ext_u, sh, 0)[HALO:], axis=0, keepdims=True))
            dp_ref[:, cs] = du.astype(BF16)
            dws.append(jnp.concatenate(rows, axis=0))
        dp_ref[:, DN_QKV:DN_QKV + N_HEADS * HEAD] = dz_ref[...].astype(BF16)
        dp_ref[:, DN_QKV + N_HEADS * HEAD:] = dab_ref[...].astype(BF16)
        _acc(dw_ref, jnp.concatenate(dws, axis=1))

    return _rows("dn_conv_bwd", body,
                 [(proj, (tt, DN_QKV), lambda i: (i, 0)),
                  (proj, (HALO, DN_QKV), lambda i: (jnp.maximum(i * (tt // HALO) - 1, 0), 0)),
                  dc,
                  (dc, (HALO, DN_QKV), lambda i: (jnp.minimum((i + 1) * (tt // HALO), t // HALO - 1), 0)),
                  dz, dab],
                 [_sds((t, DN_PROJ), BF16)], tt=tt, consts=[conv_w], accs=[_sds(conv_w.shape)])


def _l2n(x):
    return x * lax.rsqrt(jnp.sum(x * x, axis=-1, keepdims=True) + NORM_EPS)


def _dn_qkv(cq, ck, cv):
    return _l2n(_silu(cq)) * (HEAD ** -0.5), _l2n(_silu(ck)), _silu(cv)


def _dn_gates(ab, a_log, dt_bias):
    lane = lax.broadcasted_iota(jnp.int32, ab.shape, 1)
    g = -jnp.exp(a_log) * _softplus(ab + dt_bias)
    return jnp.where(lane < N_HEADS, g, jnp.where(lane < 2 * N_HEADS, _sigmoid(ab), 0.0))


def _ab_spec(tt):
    return (tt, LANES), lambda i: (i, DN_PROJ // LANES - 1)


def _dn_prep_fwd(c, proj, a_log, dt_bias, tt=256):
    t = c.shape[0]
    tt = min(tt, t)

    def body(c_ref, ab_ref, al_ref, dt_ref, q_ref, k_ref, v_ref, g_ref):
        for h in range(N_HEADS):
            q_ref[h], k_ref[h], v_ref[h] = _dn_qkv(_heads_in(c_ref, h), _heads_in(c_ref, N_HEADS + h), _heads_in(c_ref, 2 * N_HEADS + h))
        g_ref[...] = _dn_gates(ab_ref[...], al_ref[...], dt_ref[...])

    hm = _sds((N_HEADS, t, HEAD))
    return _rows("dn_prep", body, [c, (proj,) + _ab_spec(tt)], [hm, hm, hm, _sds((t, LANES))], tt=tt, consts=[a_log, dt_bias])


def _dn_prep_bwd(c, proj, a_log, dt_bias, dq, dk, dv, dgates, tt=256):
    t = c.shape[0]
    tt = min(tt, t)

    def body(c_ref, ab_ref, dq_ref, dk_ref, dv_ref, dg_ref, al_ref, dt_ref, dc_ref, dab_ref, dal_ref, ddt_ref):
        for h in range(N_HEADS):
            _, vjp = jax.vjp(_dn_qkv, _heads_in(c_ref, h), _heads_in(c_ref, N_HEADS + h), _heads_in(c_ref, 2 * N_HEADS + h))
            a, b, d = vjp((dq_ref[h], dk_ref[h], dv_ref[h]))
            dc_ref[:, h * HEAD:(h + 1) * HEAD] = a
            dc_ref[:, (N_HEADS + h) * HEAD:(N_HEADS + h + 1) * HEAD] = b
            dc_ref[:, (2 * N_HEADS + h) * HEAD:(2 * N_HEADS + h + 1) * HEAD] = d
        _, vjp = jax.vjp(_dn_gates, ab_ref[...], al_ref[...], dt_ref[...])
        dab, dal, ddt = vjp(dg_ref[...])
        dab_ref[...] = dab
        _acc(dal_ref, dal)
        _acc(ddt_ref, ddt)

    return _rows("dn_prep_bwd", body, [c, (proj,) + _ab_spec(tt), dq, dk, dv, dgates], [_sds(c.shape), _sds((t, LANES))],
                 tt=tt, consts=[a_log, dt_bias], accs=[_sds(a_log.shape), _sds(dt_bias.shape)])


def _bdot(a, b, prec=None):
    return lax.dot_general(a, b, (((2,), (1,)), ((0,), (0,))), precision=prec, preferred_element_type=F32)


def _bdot_nt(a, b, prec=None):
    return lax.dot_general(a, b, (((2,), (2,)), ((0,), (0,))), precision=prec, preferred_element_type=F32)


def _bdot_tn(a, b, prec=None):
    return lax.dot_general(a, b, (((1,), (1,)), ((0,), (0,))), precision=prec, preferred_element_type=F32)


def _inv_raw(low):
    c = low.shape[-1]
    r = lax.broadcasted_iota(jnp.int32, (c, c), 0)
    s = lax.broadcasted_iota(jnp.int32, (c, c), 1)
    m = jnp.where(r == s, 1.0, 0.0) - low
    p = _bdot(low, low, F32X3)
    n_fac = int(math.log2(c)) - 1
    for i in range(n_fac):
        m = m + _bdot(m, p, F32X3)
        if i < n_fac - 1:
            p = _bdot(p, p, F32X3)
    return m


@jax.custom_vjp
def _inv_unit_lower(low):
    return _inv_raw(low)


def _inv_fwd(low):
    m = _inv_raw(low)
    return m, m


def _inv_bwd(m, dm):
    return (-_bdot_nt(_bdot_tn(m, dm, F32X3), m, F32X3),)


_inv_unit_lower.defvjp(_inv_fwd, _inv_bwd)


def _dn_chunk(q, k, v, gates, s):
    nh, c, _ = q.shape
    lane = lax.broadcasted_iota(jnp.int32, gates.shape, 1)
    def column(j):
        return jnp.sum(jnp.where(lane == j, gates, 0.0), axis=1, keepdims=True)[None]

    g_col = jnp.concatenate([column(h) for h in range(nh)], axis=0)
    b_col = jnp.concatenate([column(h + nh) for h in range(nh)], axis=0)
    r = lax.broadcasted_iota(jnp.int32, (c, c), 0)
    cc = lax.broadcasted_iota(jnp.int32, (c, c), 1)
    causal, strict = r >= cc, r > cc
    incl = jnp.broadcast_to(jnp.where(causal, 1.0, 0.0), (nh, c, c))
    upper = jnp.broadcast_to(jnp.where(r <= cc, 1.0, 0.0), (nh, c, c))
    gb = jnp.broadcast_to(g_col, (nh, c, LANES))
    gbc = jnp.broadcast_to(g_col, (nh, c, c))
    gc = _bdot(incl, gb, F32X3)
    gc_r = _bdot(incl, gbc, F32X3)
    gc_c = _bdot_tn(gbc, upper, F32X3)
    decay = jnp.where(causal, jnp.exp(jnp.where(causal, gc_r - gc_c, 0.0)), 0.0)
    kb = k * b_col
    low = jnp.where(strict, _bdot_nt(kb, k) * decay, 0.0)
    m = _inv_unit_lower(low)
    egc = jnp.exp(gc)
    u = _bdot(m, v * b_col, F32X3)
    w = _bdot(m, kb * egc, F32X3)
    attn = _bdot_nt(q, k) * decay
    gl = jnp.sum(gb, axis=1, keepdims=True)
    v_new = u - _bdot(w, s)
    o = _bdot(q * egc, s) + _bdot(attn, v_new)
    s_new = s * jnp.exp(gl) + _bdot_tn(k * jnp.exp(gl - gc), v_new)
    return o, s_new


def _dn_chunks_fwd(q, k, v, gates):
    nh, t, _ = q.shape
    n = t // DN_CHUNK

    def kern(q_ref, k_ref, v_ref, g_ref, o_ref, sin_ref, s_scr):
        @pl.when(pl.program_id(0) == 0)
        def _():
            s_scr[...] = jnp.zeros(s_scr.shape, F32)

        s = s_scr[...]
        sin_ref[0] = s
        o_ref[...], s_scr[...] = _dn_chunk(q_ref[...], k_ref[...], v_ref[...], g_ref[...], s)

    blk = pl.BlockSpec((nh, DN_CHUNK, HEAD), lambda i: (0, i, 0))
    return pl.pallas_call(
        kern, name="dn_chunks_fwd", grid=(n,),
        in_specs=[blk, blk, blk, pl.BlockSpec((DN_CHUNK, LANES), lambda i: (i, 0))],
        out_specs=[blk, pl.BlockSpec((1, nh, HEAD, HEAD), lambda i: (i, 0, 0, 0))],
        out_shape=[_sds((nh, t, HEAD)), _sds((n, nh, HEAD, HEAD))],
        scratch_shapes=[pltpu.VMEM((nh, HEAD, HEAD), F32)],
        compiler_params=_cparams(dimension_semantics=("arbitrary",)),
    )(q, k, v, gates)


def _dn_chunks_bwd(q, k, v, gates, s_in, do):
    nh, t, _ = q.shape
    n = t // DN_CHUNK

    def kern(q_ref, k_ref, v_ref, g_ref, sin_ref, do_ref, dq_ref, dk_ref, dv_ref, dg_ref, ds_scr):
        @pl.when(pl.program_id(0) == 0)
        def _():
            ds_scr[...] = jnp.zeros(ds_scr.shape, F32)

        _, vjp = jax.vjp(_dn_chunk, q_ref[...], k_ref[...], v_ref[...], g_ref[...], sin_ref[0])
        dq_ref[...], dk_ref[...], dv_ref[...], dg_ref[...], ds_scr[...] = vjp((do_ref[...], ds_scr[...]))

    blk = pl.BlockSpec((nh, DN_CHUNK, HEAD), lambda i: (0, n - 1 - i, 0))
    gblk = pl.BlockSpec((DN_CHUNK, LANES), lambda i: (n - 1 - i, 0))
    hm = _sds((nh, t, HEAD))
    return pl.pallas_call(
        kern, name="dn_chunks_bwd", grid=(n,),
        in_specs=[blk, blk, blk, gblk, pl.BlockSpec((1, nh, HEAD, HEAD), lambda i: (n - 1 - i, 0, 0, 0)), blk],
        out_specs=[blk, blk, blk, gblk],
        out_shape=[hm, hm, hm, _sds((t, LANES))],
        scratch_shapes=[pltpu.VMEM((nh, HEAD, HEAD), F32)],
        compiler_params=_cparams(dimension_semantics=("arbitrary",)),
    )(q, k, v, gates, s_in, do)


def _dn_gate_out(o, z, g):
    return _rms(o, g) * _silu(z)


def _z_spec(tt):
    return (tt, N_HEADS * HEAD), lambda i: (i, DN_QKV // (N_HEADS * HEAD))


def _dn_post_fwd(o, proj, out_norm, tt=256):
    t = o.shape[1]
    tt = min(tt, t)

    def body(o_ref, z_ref, g_ref, y_ref):
        for h in range(N_HEADS):
            y_ref[:, h * HEAD:(h + 1) * HEAD] = _dn_gate_out(o_ref[h], _heads_in(z_ref, h), g_ref[...]).astype(BF16)

    return _rows("dn_post", body, [o, (proj,) + _z_spec(tt)], [_sds((t, N_HEADS * HEAD), BF16)], tt=tt, consts=[out_norm])[0]


def _dn_post_bwd(o, proj, out_norm, dy, tt=256):
    t = o.shape[1]
    tt = min(tt, t)

    def body(o_ref, z_ref, dy_ref, g_ref, do_ref, dz_ref, dg_ref):
        dg = jnp.zeros(g_ref.shape, F32)
        for h in range(N_HEADS):
            _, vjp = jax.vjp(_dn_gate_out, o_ref[h], _heads_in(z_ref, h), g_ref[...])
            a, b, d = vjp(_heads_in(dy_ref, h))
            do_ref[h] = a
            dz_ref[:, h * HEAD:(h + 1) * HEAD] = b
            dg = dg + d
        _acc(dg_ref, dg)

    return _rows("dn_post_bwd", body, [o, (proj,) + _z_spec(tt), dy], [_sds(o.shape), _sds((t, N_HEADS * HEAD))], tt=tt,
                 consts=[out_norm], accs=[_sds(out_norm.shape)])


def _dn_fwd(x, h, w_in, conv_w, a_log, dt_bias, out_norm, w_out):
    proj = _mm("dn_in", h, w_in)
    c = _dn_conv_fwd(proj, conv_w)
    q, k, v, gates = _dn_prep_fwd(c, proj, a_log, dt_bias)
    o, s_in = _dn_chunks_fwd(q, k, v, gates)
    on = _dn_post_fwd(o, proj, out_norm)
    y = _mm("dn_out", on, w_out, add=x)
    return y, (h, proj, c, q, k, v, gates, o, s_in, on)


def _dn_bwd(saved, dy, w_in, conv_w, a_log, dt_bias, out_norm, w_out):
    h, proj, c, q, k, v, gates, o, s_in, on = saved
    d_wout = _mm("dn_out_wgrad", on, dy, ta=True)
    don = _mm("dn_out_dgrad", dy, w_out, tb=True)
    do, dz, d_out_norm = _dn_post_bwd(o, proj, out_norm, don)
    dq, dk, dv, dgates = _dn_chunks_bwd(q, k, v, gates, s_in, do)
    dc, dab, d_a_log, d_dt_bias = _dn_prep_bwd(c, proj, a_log, dt_bias, dq, dk, dv, dgates)
    dproj, d_conv_w = _dn_conv_bwd(proj, conv_w, dc, dz, dab)
    d_win = _mm("dn_in_wgrad", h, dproj, ta=True)
    dh = _mm("dn_in_dgrad", dproj, w_in, tb=True)
    return dh, d_win, d_conv_w, d_a_log, d_dt_bias, d_out_norm, d_wout


MLA_SCALE = MLA_QK ** -0.5
MLA_C = 512


def _swap_raw(x):
    lane = lax.broadcasted_iota(jnp.int32, x.shape, 1)
    half = MLA_ROPE // 2
    y = jnp.where(lane < half, pltpu.roll(x, LANES - half, 1), pltpu.roll(x, half, 1))
    return jnp.where(lane < MLA_ROPE, y, 0.0)


@jax.custom_vjp
def _swap_halves(x):
    return _swap_raw(x)


_swap_halves.defvjp(lambda x: (_swap_raw(x), None), lambda _, d: (_swap_raw(d),))


def _rms_rope(x, g, cos, sin):
    y = x * lax.rsqrt(jnp.sum(x * x, axis=-1, keepdims=True) * (1.0 / MLA_ROPE) + NORM_EPS) * g
    return y * cos + _swap_halves(y) * sin


def _mla_latent(cq, ckv, kr, gq, gkv, gkr, cos, sin):
    return _rms(cq, gq), _rms(ckv, gkv), _rms_rope(kr, gkr, cos, sin)


def _mla_prep1_fwd(c, gq, gkv, gkr, cos, sin):
    t = c.shape[0]

    def body(c_ref, cos_ref, sin_ref, gq_ref, gkv_ref, gkr_ref, cq_ref, ckv_ref, kr_ref):
        a, b, r = _mla_latent(c_ref[:, :256], c_ref[:, 256:384], c_ref[:, 384:], gq_ref[...], gkv_ref[...], gkr_ref[...],
                              cos_ref[...], sin_ref[...])
        cq_ref[...] = a.astype(BF16)
        ckv_ref[...] = b.astype(BF16)
        kr_ref[...] = r.astype(BF16)

    return _rows("mla_prep1", body, [c, cos, sin], [_sds((t, 256), BF16), _sds((t, HEAD), BF16), _sds((t, HEAD), BF16)],
                 tt=512, consts=[gq, gkv, gkr])


def _mla_prep1_bwd(c, gq, gkv, gkr, cos, sin, dcq, dckv, dkr_heads):
    def body(c_ref, cos_ref, sin_ref, dcq_ref, dckv_ref, dkr_ref, gq_ref, gkv_ref, gkr_ref, dc_ref, dgq_ref, dgkv_ref, dgkr_ref):
        dkr = dkr_ref[0]
        for h in range(1, N_HEADS):
            dkr = dkr + dkr_ref[h]
        _, vjp = jax.vjp(_mla_latent, c_ref[:, :256], c_ref[:, 256:384], c_ref[:, 384:], gq_ref[...], gkv_ref[...], gkr_ref[...],
                         cos_ref[...], sin_ref[...])
        a, b, r, d1, d2, d3, _, _ = vjp((dcq_ref[...], dckv_ref[...], dkr))
        dc_ref[:, :256] = a.astype(BF16)
        dc_ref[:, 256:384] = b.astype(BF16)
        dc_ref[:, 384:] = r.astype(BF16)
        _acc(dgq_ref, d1)
        _acc(dgkv_ref, d2)
        _acc(dgkr_ref, d3)

    return _rows("mla_prep1_bwd", body, [c, cos, sin, dcq, dckv, dkr_heads], [_sds(c.shape, BF16)], tt=512,
                 consts=[gq, gkv, gkr], accs=[_sds(gq.shape), _sds(gkv.shape), _sds(gkr.shape)])


def _mla_heads(qn, qr, kn, gqn, gqr, gkn, cos, sin):
    return _rms(qn, gqn) * MLA_SCALE, _rms_rope(qr, gqr, cos, sin) * MLA_SCALE, _rms(kn, gkn)


def _mla_prep2_fwd(qa, kv, gqn, gqr, gkn, cos, sin):
    t = qa.shape[0]

    def body(qa_ref, kv_ref, cos_ref, sin_ref, gqn_ref, gqr_ref, gkn_ref, qn_ref, qr_ref, kn_ref, v_ref):
        for h in range(N_HEADS):
            a, b, c = _mla_heads(_heads_in(qa_ref, h), _heads_in(qa_ref, N_HEADS + h), _heads_in(kv_ref, h),
                                 gqn_ref[...], gqr_ref[...], gkn_ref[...], cos_ref[...], sin_ref[...])
            qn_ref[h] = a.astype(BF16)
            qr_ref[h] = b.astype(BF16)
            kn_ref[h] = c.astype(BF16)
            v_ref[h] = _heads_in(kv_ref, N_HEADS + h).astype(BF16)

    hm = _sds((N_HEADS, t, HEAD), BF16)
    return _rows("mla_prep2", body, [qa, kv, cos, sin], [hm, hm, hm, hm], tt=256, consts=[gqn, gqr, gkn])


def _mla_prep2_bwd(qa, kv, gqn, gqr, gkn, cos, sin, dqn, dqr, dkn, dv):
    def body(qa_ref, kv_ref, cos_ref, sin_ref, dqn_ref, dqr_ref, dkn_ref, dv_ref, gqn_ref, gqr_ref, gkn_ref,
             dqa_ref, dkv_ref, d1_ref, d2_ref, d3_ref):
        d1 = jnp.zeros(gqn_ref.shape, F32)
        d2 = jnp.zeros(gqr_ref.shape, F32)
        d3 = jnp.zeros(gkn_ref.shape, F32)
        for h in range(N_HEADS):
            _, vjp = jax.vjp(_mla_heads, _heads_in(qa_ref, h), _heads_in(qa_ref, N_HEADS + h), _heads_in(kv_ref, h),
                             gqn_ref[...], gqr_ref[...], gkn_ref[...], cos_ref[...], sin_ref[...])
            a, b, c, e1, e2, e3, _, _ = vjp((dqn_ref[h], dqr_ref[h], dkn_ref[h]))
            dqa_ref[:, h * HEAD:(h + 1) * HEAD] = a.astype(BF16)
            dqa_ref[:, (N_HEADS + h) * HEAD:(N_HEADS + h + 1) * HEAD] = b.astype(BF16)
            dkv_ref[:, h * HEAD:(h + 1) * HEAD] = c.astype(BF16)
            dkv_ref[:, (N_HEADS + h) * HEAD:(N_HEADS + h + 1) * HEAD] = dv_ref[h].astype(BF16)
            d1, d2, d3 = d1 + e1, d2 + e2, d3 + e3
        _acc(d1_ref, d1)
        _acc(d2_ref, d2)
        _acc(d3_ref, d3)

    return _rows("mla_prep2_bwd", body, [qa, kv, cos, sin, dqn, dqr, dkn, dv], [_sds(qa.shape, BF16), _sds(kv.shape, BF16)],
                 tt=256, consts=[gqn, gqr, gkn], accs=[_sds(gqn.shape), _sds(gqr.shape), _sds(gkn.shape)])


def _mla_attn_fwd(qn, qr, kn, kr, v):
    nh, t, _ = qn.shape
    bq = _q_block(t)
    per = bq // ATT_BLOCK

    def kern(qn_ref, qr_ref, kn_ref, kr_ref, v_ref, o_ref, lse_ref):
        qb = pl.program_id(1)
        qv = jnp.concatenate([qn_ref[0], qr_ref[0]], axis=1)
        order = _key_order(bq, qb)

        def body(kb, carry):
            acc, m, l = carry
            off = pl.multiple_of(kb * ATT_BLOCK, ATT_BLOCK)
            kv = jnp.concatenate([kn_ref[0, pl.ds(off, ATT_BLOCK), :], kr_ref[pl.ds(off, ATT_BLOCK), :]], axis=1)
            s = jnp.where(order >= kb * ATT_BLOCK, _dot_nt(qv, kv), -jnp.inf)
            m_new = jnp.maximum(m, jnp.max(s, axis=1, keepdims=True))
            alpha = jnp.exp(m - m_new)
            p = jnp.exp(s - m_new)
            acc = acc * alpha + _dot(p.astype(BF16), v_ref[0, pl.ds(off, ATT_BLOCK), :])
            return acc, m_new, l * alpha + jnp.sum(p, axis=1, keepdims=True)

        init = (jnp.zeros((bq, HEAD), F32), jnp.full((bq, 1), -jnp.inf, F32), jnp.zeros((bq, 1), F32))
        acc, m, l = lax.fori_loop(0, (qb + 1) * per, body, init)
        o_ref[...] = acc / l
        lse_ref[...] = jnp.broadcast_to(m + jnp.log(l), (bq, HEAD))

    blk = pl.BlockSpec((1, bq, HEAD), lambda h, i: (h, i, 0))
    full = pl.BlockSpec((1, t, HEAD), lambda h, i: (h, 0, 0))
    tok = pl.BlockSpec((bq, HEAD), lambda h, i: (i, h))
    return pl.pallas_call(
        kern, name="mla_attn_fwd", grid=(nh, t // bq),
        in_specs=[blk, blk, full, pl.BlockSpec((t, HEAD), lambda h, i: (0, 0)), full],
        out_specs=[tok, tok], out_shape=[_sds((t, nh * HEAD)), _sds((t, nh * HEAD))],
        compiler_params=_cparams(dimension_semantics=("parallel", "arbitrary")),
    )(qn, qr, kn, kr, v)


def _mla_attn_bwd(qn, qr, kn, kr, v, o, lse, do):
    nh, t, _ = qn.shape
    bq = _q_block(t)
    per = bq // ATT_BLOCK

    def kern(qn_ref, qr_ref, kn_ref, kr_ref, v_ref, o_ref, lse_ref, do_ref, dqn_ref, dqr_ref, dkn_ref, dkr_ref, dv_ref):
        qb = pl.program_id(1)

        @pl.when(qb == 0)
        def _():
            dkn_ref[...] = jnp.zeros(dkn_ref.shape, F32)
            dkr_ref[...] = jnp.zeros(dkr_ref.shape, F32)
            dv_ref[...] = jnp.zeros(dv_ref.shape, F32)

        qv = jnp.concatenate([qn_ref[0], qr_ref[0]], axis=1)
        dov = do_ref[...]
        dob = dov.astype(BF16)
        delta = jnp.sum(dov * o_ref[...], axis=1, keepdims=True)
        lse_col = lse_ref[:, :1]
        order = _key_order(bq, qb)

        def body(kb, dq):
            off = pl.multiple_of(kb * ATT_BLOCK, ATT_BLOCK)
            kv = jnp.concatenate([kn_ref[0, pl.ds(off, ATT_BLOCK), :], kr_ref[pl.ds(off, ATT_BLOCK), :]], axis=1)
            vv = v_ref[0, pl.ds(off, ATT_BLOCK), :]
            p = jnp.where(order >= kb * ATT_BLOCK, jnp.exp(_dot_nt(qv, kv) - lse_col), 0.0)
            ds = (p * (_dot_nt(dob, vv) - delta)).astype(BF16)
            dk = _dot_tn(ds, qv)
            dkn_ref[0, pl.ds(off, ATT_BLOCK), :] += dk[:, :HEAD]
            dkr_ref[0, pl.ds(off, ATT_BLOCK), :] += dk[:, HEAD:]
            dv_ref[0, pl.ds(off, ATT_BLOCK), :] += _dot_tn(p.astype(BF16), dob)
            return dq + _dot(ds, kv)

        dq = lax.fori_loop(0, (qb + 1) * per, body, jnp.zeros((bq, 2 * HEAD), F32))
        dqn_ref[0] = dq[:, :HEAD]
        dqr_ref[0] = dq[:, HEAD:]

    hm = _sds((nh, t, HEAD))
    blk = pl.BlockSpec((1, bq, HEAD), lambda h, i: (h, i, 0))
    full = pl.BlockSpec((1, t, HEAD), lambda h, i: (h, 0, 0))
    tok = pl.BlockSpec((bq, HEAD), lambda h, i: (i, h))
    return pl.pallas_call(
        kern, name="mla_attn_bwd", grid=(nh, t // bq),
        in_specs=[blk, blk, full, pl.BlockSpec((t, HEAD), lambda h, i: (0, 0)), full, tok, tok, tok],
        out_specs=[blk, blk, full, full, full], out_shape=[hm, hm, hm, hm, hm],
        compiler_params=_cparams(dimension_semantics=("parallel", "arbitrary")),
    )(qn, qr, kn, kr, v, o, lse, do)


def _rope_tables(t):
    inv_freq = ROPE_THETA ** (-jnp.arange(0, MLA_ROPE, 2, dtype=F32) / MLA_ROPE)
    ang = jnp.arange(t, dtype=F32)[:, None] * inv_freq[None, :]
    c, s = jnp.cos(ang), jnp.sin(ang)
    pad = ((0, 0), (0, LANES - MLA_ROPE))
    return jnp.pad(jnp.concatenate([c, c], axis=1), pad), jnp.pad(jnp.concatenate([-s, s], axis=1), pad)


def _pad_lanes(v, n=LANES):
    return jnp.pad(v, (0, n - v.shape[0])).reshape(1, n)


def _mla_layout(w_down, w_uq, w_ukv):
    w_down_p = jnp.pad(w_down, ((0, 0), (0, MLA_C - w_down.shape[1])))
    uq = w_uq.reshape(w_uq.shape[0], N_HEADS, MLA_QK)
    rope = jnp.pad(uq[:, :, HEAD:], ((0, 0), (0, 0), (0, LANES - MLA_ROPE)))
    w_uq_p = jnp.concatenate([uq[:, :, :HEAD].reshape(-1, N_HEADS * HEAD), rope.reshape(-1, N_HEADS * LANES)], axis=1)
    ukv = w_ukv.reshape(w_ukv.shape[0], N_HEADS, 2 * HEAD)
    w_ukv_p = jnp.concatenate([ukv[:, :, :HEAD].reshape(-1, N_HEADS * HEAD), ukv[:, :, HEAD:].reshape(-1, N_HEADS * HEAD)], axis=1)
    return w_down_p, w_uq_p, w_ukv_p


def _mla_unlayout(d_down_p, d_uq_p, d_ukv_p):
    d_down = d_down_p[:, :256 + HEAD + MLA_ROPE]
    nope = d_uq_p[:, :N_HEADS * HEAD].reshape(-1, N_HEADS, HEAD)
    rope = d_uq_p[:, N_HEADS * HEAD:].reshape(-1, N_HEADS, LANES)[:, :, :MLA_ROPE]
    d_uq = jnp.concatenate([nope, rope], axis=2).reshape(-1, N_HEADS * MLA_QK)
    kn = d_ukv_p[:, :N_HEADS * HEAD].reshape(-1, N_HEADS, HEAD)
    vv = d_ukv_p[:, N_HEADS * HEAD:].reshape(-1, N_HEADS, HEAD)
    d_ukv = jnp.concatenate([kn, vv], axis=2).reshape(-1, N_HEADS * 2 * HEAD)
    return d_down, d_uq, d_ukv


def _mla_weight_shapes():
    return (_sds((1024, MLA_C), BF16), _sds((1, 256)), _sds((1, HEAD)), _sds((256, 2048), BF16), _sds((HEAD, 2048), BF16),
            _sds((1, HEAD)), _sds((1, HEAD)), _sds((1, HEAD)), _sds((1, HEAD)), _sds((1024, 1024), BF16),
            _sds((4096, HEAD)), _sds((4096, HEAD)))


def _mla_fwd(x, h, w_down, gq, gkv, w_uq, w_ukv, gqn, gqr, gkn, gkr, w_out, cos, sin):
    c = _mm("mla_down", h, w_down)
    cq, ckv, kr = _mla_prep1_fwd(c, gq, gkv, gkr, cos, sin)
    qa = _mm("mla_uq", cq, w_uq)
    kv = _mm("mla_ukv", ckv, w_ukv)
    qn, qr, kn, v = _mla_prep2_fwd(qa, kv, gqn, gqr, gkn, cos, sin)
    o, lse = _mla_attn_fwd(qn, qr, kn, kr, v)
    y = _mm("mla_out", o, w_out, add=x)
    return y, (h, c, cq, ckv, kr, qa, kv, qn, qr, kn, v, o, lse)


def _mla_bwd(saved, dy, w_down, gq, gkv, w_uq, w_ukv, gqn, gqr, gkn, gkr, w_out, cos, sin):
    h, c, cq, ckv, kr, qa, kv, qn, qr, kn, v, o, lse = saved
    d_wout = _mm("mla_out_wgrad", o, dy, ta=True)
    do = _mm("mla_out_dgrad", dy, w_out, tb=True)
    dqn, dqr, dkn, dkr, dv = _mla_attn_bwd(qn, qr, kn, kr, v, o, lse, do)
    dqa, dkv, dgqn, dgqr, dgkn = _mla_prep2_bwd(qa, kv, gqn, gqr, gkn, cos, sin, dqn, dqr, dkn, dv)
    d_wuq = _mm("mla_uq_wgrad", cq, dqa, ta=True)
    d_wukv = _mm("mla_ukv_wgrad", ckv, dkv, ta=True)
    dcq = _mm("mla_uq_dgrad", dqa, w_uq, tb=True)
    dckv = _mm("mla_ukv_dgrad", dkv, w_ukv, tb=True)
    dc, dgq, dgkv, dgkr = _mla_prep1_bwd(c, gq, gkv, gkr, cos, sin, dcq, dckv, dkr)
    d_wdown = _mm("mla_down_wgrad", h, dc, ta=True)
    dh = _mm("mla_down_dgrad", dc, w_down, tb=True)
    return dh, d_wdown, dgq, dgkv, d_wuq, d_wukv, dgqn, dgqr, dgkn, dgkr, d_wout


def _loss_head(y, target):
    d = y.shape[1]

    def body(y_ref, t_ref, dy_ref, l_ref):
        err = y_ref[...] - t_ref[...]
        dy_ref[...] = err * (1.0 / d)
        part = 0.5 * jnp.sum(jnp.sum(err * err, axis=1, keepdims=True) * (1.0 / d), axis=0, keepdims=True)
        _acc(l_ref, jnp.broadcast_to(part, (1, LANES)))

    return _rows("loss_head", body, [y, target], [_sds(y.shape)], tt=512, accs=[_sds((1, LANES))])


MESH_ID = pl.DeviceIdType.MESH
HBM_SPEC = pl.BlockSpec(memory_space=pltpu.HBM)


def _all_gather(name, x):
    m_per, n = x.shape

    def body(x_ref, out_ref, send_sems, recv_sems, local_sem):
        x, y, c = lax.axis_index("x"), lax.axis_index("y"), lax.axis_index("c")
        me, sibling = (x, y, c), (x, y, 1 - c)
        chips = [(1 - x, y), (x, 1 - y), (1 - x, 1 - y)]

        def rows(px, py, pc):
            return out_ref.at[pl.ds((4 * px + 2 * py + pc) * m_per, m_per), :]

        def copy(k, block, to, src=None):
            return pltpu.make_async_remote_copy(
                src_ref=rows(*block) if src is None else src, dst_ref=rows(*block),
                send_sem=send_sems.at[k], recv_sem=recv_sems.at[k], device_id=to, device_id_type=MESH_ID)

        mine = pltpu.make_async_copy(x_ref, rows(*me), local_sem)
        mine.start()
        first = [copy(0, me, sibling, src=x_ref)]
        first += [copy(1 + j, me, (*chip, c), src=x_ref) for j, chip in enumerate(chips)]
        for cp in first:
            cp.start()
        passed = [copy(4 + j, (*chip, c), sibling) for j, chip in enumerate(chips)]
        for j, chip in enumerate(chips):
            copy(1 + j, (*chip, c), me).wait_recv()
            passed[j].start()
        copy(0, sibling, me).wait_recv()
        for j, chip in enumerate(chips):
            copy(4 + j, (*chip, 1 - c), me).wait_recv()
        for cp in first + passed:
            cp.wait_send()
        mine.wait()

    return pl.pallas_call(
        body, name=name,
        out_shape=jax.ShapeDtypeStruct((N_DEV * m_per, n), x.dtype),
        in_specs=[HBM_SPEC], out_specs=HBM_SPEC,
        scratch_shapes=[pltpu.SemaphoreType.DMA((7,)), pltpu.SemaphoreType.DMA((7,)), pltpu.SemaphoreType.DMA],
    )(x)


def _all_gather_groups(name, xs):
    ng = len(xs)

    def body(*refs):
        x_refs, out_refs = refs[:ng], refs[ng:2 * ng]
        send_sems, recv_sems, local_sems = refs[2 * ng:]
        x, y, c = lax.axis_index("x"), lax.axis_index("y"), lax.axis_index("c")
        me, sibling = (x, y, c), (x, y, 1 - c)
        chips = [(1 - x, y), (x, 1 - y), (1 - x, 1 - y)]

        def copy(g, k, block, to, src=None):
            px, py, pc = block
            dst = out_refs[g].at[4 * px + 2 * py + pc]
            return pltpu.make_async_remote_copy(
                src_ref=dst if src is None else src, dst_ref=dst,
                send_sem=send_sems.at[g, k], recv_sem=recv_sems.at[g, k], device_id=to, device_id_type=MESH_ID)

        mine = [pltpu.make_async_copy(x_refs[g], out_refs[g].at[4 * x + 2 * y + c], local_sems.at[g]) for g in range(ng)]
        for cp in mine:
            cp.start()
        first = []
        for g in range(ng):
            first.append(copy(g, 0, me, sibling, src=x_refs[g]))
            first += [copy(g, 1 + j, me, (*chip, c), src=x_refs[g]) for j, chip in enumerate(chips)]
        for cp in first:
            cp.start()
        passed = []
        for j, chip in enumerate(chips):
            for g in range(ng):
                copy(g, 1 + j, (*chip, c), me).wait_recv()
                passed.append(copy(g, 4 + j, (*chip, c), sibling))
                passed[-1].start()
        for g in range(ng):
            copy(g, 0, sibling, me).wait_recv()
            for j, chip in enumerate(chips):
                copy(g, 4 + j, (*chip, 1 - c), me).wait_recv()
        for cp in first + passed:
            cp.wait_send()
        for cp in mine:
            cp.wait()

    return pl.pallas_call(
        body, name=name,
        out_shape=[jax.ShapeDtypeStruct((N_DEV,) + x.shape, x.dtype) for x in xs],
        in_specs=[HBM_SPEC] * ng, out_specs=[HBM_SPEC] * ng,
        scratch_shapes=[pltpu.SemaphoreType.DMA((ng, 7)), pltpu.SemaphoreType.DMA((ng, 7)), pltpu.SemaphoreType.DMA((ng,))],
    )(*xs)


def _rs_cores(gs):
    ng = len(gs)

    def body(*refs):
        g_refs, recv_refs = refs[:ng], refs[ng:2 * ng]
        send_sems, recv_sems = refs[2 * ng:]
        x, y, c = lax.axis_index("x"), lax.axis_index("y"), lax.axis_index("c")
        copies = [pltpu.make_async_remote_copy(
            src_ref=g_refs[g].at[:, 1 - c], dst_ref=recv_refs[g], send_sem=send_sems.at[g], recv_sem=recv_sems.at[g],
            device_id=(x, y, 1 - c), device_id_type=MESH_ID) for g in range(ng)]
        for cp in copies:
            cp.start()
        for cp in copies:
            cp.wait()

    return pl.pallas_call(
        body, name="rs_cores",
        out_shape=[jax.ShapeDtypeStruct((g.shape[0],) + g.shape[2:], g.dtype) for g in gs],
        in_specs=[HBM_SPEC] * ng, out_specs=[HBM_SPEC] * ng,
        scratch_shapes=[pltpu.SemaphoreType.DMA((ng,)), pltpu.SemaphoreType.DMA((ng,))],
    )(*gs)


def _rs_add(g, recv, core):
    nchip, _, r, c_ = g.shape
    tr = _pick(r, (512, 384, 256, 128))

    def body(core_ref, g_ref, r_ref, p_ref, pb_ref):
        s = g_ref[0, 0] + r_ref[0]
        p_ref[0] = s
        pb_ref[0] = s.astype(BF16)

    return pl.pallas_call(
        body, name="rs_add",
        grid_spec=pltpu.PrefetchScalarGridSpec(
            num_scalar_prefetch=1, grid=(nchip, r // tr),
            in_specs=[pl.BlockSpec((1, 1, tr, c_), lambda k, i, core_ref: (k, core_ref[0], i, 0)),
                      pl.BlockSpec((1, tr, c_), lambda k, i, core_ref: (k, i, 0))],
            out_specs=[pl.BlockSpec((1, tr, c_), lambda k, i, core_ref: (k, i, 0)),
                       pl.BlockSpec((1, tr, c_), lambda k, i, core_ref: (k, i, 0))]),
        out_shape=[_sds((nchip, r, c_)), _sds((nchip, r, c_), BF16)],
        compiler_params=_cparams(dimension_semantics=("arbitrary", "arbitrary")),
    )(core, g, recv)


def _rs_chips(pbs):
    ng = len(pbs)

    def body(*refs):
        p_refs, recv_refs = refs[:ng], refs[ng:2 * ng]
        send_sems, recv_sems, local_sems = refs[2 * ng:]
        x, y, c = lax.axis_index("x"), lax.axis_index("y"), lax.axis_index("c")
        my = 2 * x + y
        chips = [(1 - x, y), (x, 1 - y), (1 - x, 1 - y)]
        local = [pltpu.make_async_copy(p_refs[g].at[my], recv_refs[g].at[my], local_sems.at[g]) for g in range(ng)]
        for cp in local:
            cp.start()

        def copy(g, k, src_chip, dst_chip, to):
            return pltpu.make_async_remote_copy(
                src_ref=p_refs[g].at[src_chip], dst_ref=recv_refs[g].at[dst_chip],
                send_sem=send_sems.at[g, k], recv_sem=recv_sems.at[g, k], device_id=to, device_id_type=MESH_ID)

        sends = [copy(g, k, 2 * px + py, my, (px, py, c)) for g in range(ng) for k, (px, py) in enumerate(chips)]
        for cp in sends:
            cp.start()
        for g in range(ng):
            for k, (px, py) in enumerate(chips):
                copy(g, k, my, 2 * px + py, (px, py, c)).wait_recv()
        for cp in sends:
            cp.wait_send()
        for cp in local:
            cp.wait()

    return pl.pallas_call(
        body, name="rs_chips", out_shape=[jax.ShapeDtypeStruct(p.shape, p.dtype) for p in pbs],
        in_specs=[HBM_SPEC] * ng, out_specs=[HBM_SPEC] * ng,
        scratch_shapes=[pltpu.SemaphoreType.DMA((ng, 3)), pltpu.SemaphoreType.DMA((ng, 3)), pltpu.SemaphoreType.DMA((ng,))],
    )(*pbs)


EFFECT = pltpu.SideEffectType.DATAFLOW_SIDE_EFFECTING
SEM_SPEC = pl.BlockSpec(memory_space=pltpu.SEMAPHORE)


def _push_copies(src_refs, land_refs, send_sems, recv_sems, chunked):
    x, y, c = lax.axis_index("x"), lax.axis_index("y"), lax.axis_index("c")
    me = 4 * x + 2 * y + c
    copies = []
    for g, (src, land) in enumerate(zip(src_refs, land_refs)):
        for k in range(1, N_DEV):
            px = 1 - x if k & 4 else x
            py = 1 - y if k & 2 else y
            pc = 1 - c if k & 1 else c
            copies.append(pltpu.make_async_remote_copy(
                src_ref=src.at[4 * px + 2 * py + pc] if chunked else src, dst_ref=land.at[me],
                send_sem=send_sems.at[g * (N_DEV - 1) + k - 1], recv_sem=recv_sems.at[g * (N_DEV - 1) + k - 1],
                device_id=(px, py, pc), device_id_type=MESH_ID))
    return copies


def _hbm(x):
    return pltpu.with_memory_space_constraint(x, pltpu.HBM)


def _push_start(name, srcs, lands, chunked):
    ng = len(srcs)

    def body(*refs):
        for cp in _push_copies(refs[:ng], refs[ng:2 * ng], refs[2 * ng], refs[2 * ng + 1], chunked):
            cp.start()
        refs[-1][...] = jnp.zeros(refs[-1].shape, F32)

    bufs = list(srcs) + list(lands)
    outs = pl.pallas_call(
        body, name=name,
        out_shape=(pltpu.SemaphoreType.DMA((ng * (N_DEV - 1),)), pltpu.SemaphoreType.DMA((ng * (N_DEV - 1),)),
                   *[pltpu.HBM(b.shape, b.dtype) for b in bufs], jax.ShapeDtypeStruct((8, LANES), F32)),
        in_specs=[HBM_SPEC] * (2 * ng),
        out_specs=(SEM_SPEC, SEM_SPEC, *[HBM_SPEC] * (2 * ng), pl.BlockSpec(memory_space=pltpu.VMEM)),
        input_output_aliases={i: 2 + i for i in range(2 * ng)},
        compiler_params=pltpu.CompilerParams(has_side_effects=EFFECT),
    )(*[_hbm(b) for b in bufs])
    return outs[0], outs[1], list(outs[2:2 + ng]), list(outs[2 + ng:2 + 2 * ng]), outs[-1]


def _push_wait(name, started, after, chunked):
    send_sems, recv_sems, srcs, lands, _ = started
    ng = len(srcs)

    def body(*refs):
        copies = _push_copies(refs[:ng], refs[ng:2 * ng], refs[2 * ng], refs[2 * ng + 1], chunked)
        for cp in copies:
            cp.wait_send()
        for cp in copies:
            cp.wait_recv()

    bufs = srcs + lands
    outs = pl.pallas_call(
        body, name=name,
        out_shape=tuple(pltpu.HBM(b.shape, b.dtype) for b in bufs),
        in_specs=[HBM_SPEC] * (2 * ng) + [SEM_SPEC, SEM_SPEC, pl.BlockSpec(memory_space=pl.ANY)],
        out_specs=tuple([HBM_SPEC] * (2 * ng)),
        input_output_aliases={i: i for i in range(2 * ng)},
        compiler_params=pltpu.CompilerParams(has_side_effects=EFFECT),
    )(*bufs, send_sems, recv_sems, after)
    return list(outs[ng:])


def _sum_adam_devices(own, recv, dev, w, m, v):
    ndev, r, c_ = recv.shape
    tr = _pick(r, (256, 128, 96, 32))

    def body(dev_ref, own_ref, r_ref, w_ref, m_ref, v_ref, g_ref, d_ref, mo_ref, vo_ref):
        me = dev_ref[0]
        g = jnp.where(me == 0, own_ref[...], r_ref[0].astype(F32))
        for j in range(1, ndev):
            g = g + jnp.where(me == j, own_ref[...], r_ref[j].astype(F32))
        g_ref[...] = g
        d_ref[...], mo_ref[...], vo_ref[...] = _adam(w_ref[...], g, m_ref[...], v_ref[...])

    row = pl.BlockSpec((tr, c_), lambda i, dev_ref: (i, 0))
    return pl.pallas_call(
        body, name="sum_adam",
        grid_spec=pltpu.PrefetchScalarGridSpec(
            num_scalar_prefetch=1, grid=(r // tr,),
            in_specs=[row, pl.BlockSpec((ndev, tr, c_), lambda i, dev_ref: (0, i, 0)), row, row, row],
            out_specs=[row, row, row, row]),
        out_shape=[_sds((r, c_))] * 4,
        compiler_params=_cparams(dimension_semantics=("arbitrary",)),
    )(dev, own, recv, w, m, v)


def _cols_from_shards(w, width):
    ns, r, cs = w.shape
    tr = _pick(r, (256, 128))

    def body(w_ref, o_ref):
        parts = [w_ref[j] for j in range(ns)]
        if width > ns * cs:
            parts.append(jnp.zeros((tr, width - ns * cs), w.dtype))
        o_ref[...] = jnp.concatenate(parts, axis=1)

    return pl.pallas_call(
        body, name="cols_from_shards", grid=(r // tr,),
        in_specs=[pl.BlockSpec((ns, tr, cs), lambda i: (0, i, 0))], out_specs=pl.BlockSpec((tr, width), lambda i: (i, 0)),
        out_shape=jax.ShapeDtypeStruct((r, width), w.dtype), compiler_params=_cparams(dimension_semantics=("arbitrary",)),
    )(w)


def _shards_from_cols(g, cs):
    r, width = g.shape
    tr = _pick(r, (256, 128))

    def body(g_ref, o_ref):
        for j in range(N_DEV):
            o_ref[j] = g_ref[:, j * cs:(j + 1) * cs]

    return pl.pallas_call(
        body, name="shards_from_cols", grid=(r // tr,),
        in_specs=[pl.BlockSpec((tr, width), lambda i: (i, 0))], out_specs=pl.BlockSpec((N_DEV, tr, cs), lambda i: (0, i, 0)),
        out_shape=jax.ShapeDtypeStruct((N_DEV, r, cs), g.dtype), compiler_params=_cparams(dimension_semantics=("arbitrary",)),
    )(g)


def _adam(w, g, m, v):
    m = ADAM_B1 * m + (1.0 - ADAM_B1) * g
    v = ADAM_B2 * v + (1.0 - ADAM_B2) * (g * g)
    m_hat = m / (1.0 - ADAM_B1 ** ADAM_STEP)
    v_hat = v / (1.0 - ADAM_B2 ** ADAM_STEP)
    return -ADAM_LR * (m_hat / (jnp.sqrt(v_hat) + ADAM_EPS) + ADAM_WD * w), m, v


def _sum_adam(p, recv, chip, w, m, v):
    nchip, r, c_ = recv.shape
    tr = _pick(r, (512, 384, 256, 128))

    def body(chip_ref, p_ref, r_ref, w_ref, m_ref, v_ref, g_ref, d_ref, mo_ref, vo_ref):
        my = chip_ref[0]
        g = jnp.where(my == 0, p_ref[0], r_ref[0].astype(F32))
        for k in range(1, nchip):
            g = g + jnp.where(my == k, p_ref[0], r_ref[k].astype(F32))
        g_ref[...] = g
        d_ref[...], mo_ref[...], vo_ref[...] = _adam(w_ref[...], g, m_ref[...], v_ref[...])

    row = pl.BlockSpec((tr, c_), lambda i, chip_ref: (i, 0))
    return pl.pallas_call(
        body, name="sum_adam",
        grid_spec=pltpu.PrefetchScalarGridSpec(
            num_scalar_prefetch=1, grid=(r // tr,),
            in_specs=[pl.BlockSpec((1, tr, c_), lambda i, chip_ref: (chip_ref[0], i, 0)),
                      pl.BlockSpec((nchip, tr, c_), lambda i, chip_ref: (0, i, 0)), row, row, row],
            out_specs=[row, row, row, row]),
        out_shape=[_sds((r, c_))] * 4,
        compiler_params=_cparams(dimension_semantics=("arbitrary",)),
    )(chip, p, recv, w, m, v)


def _sum_devices(gathered):
    m_all, n = gathered.shape
    m_per = m_all // N_DEV

    def body(x_ref, o_ref):
        s = x_ref[0:m_per, :]
        for j in range(1, N_DEV):
            s = s + x_ref[j * m_per:(j + 1) * m_per, :]
        o_ref[...] = s

    return pl.pallas_call(body, name="sum_devices", out_shape=_sds((m_per, n)), compiler_params=_cparams())(gathered)


def _adam_small(w, g, m, v):
    def body(w_ref, g_ref, m_ref, v_ref, d_ref, mo_ref, vo_ref):
        d_ref[...], mo_ref[...], vo_ref[...] = _adam(w_ref[...], g_ref[...], m_ref[...], v_ref[...])

    return pl.pallas_call(body, name="adam_small", out_shape=[_sds(w.shape)] * 3, compiler_params=_cparams())(w, g, m, v)


N_LAYERS = 4
_MIXER = ("dn", "sb", "mla")
_MIXER_PARAMS = {
    "dn": ("dn_w_in", "dn_conv_w", "dn_a_log", "dn_dt_bias", "dn_out_norm", "dn_w_out"),
    "sb": ("sb_w_qkv", "sb_q_norm", "sb_k_norm", "sb_w_out"),
    "mla": ("mla_w_down", "mla_q_a_norm", "mla_kv_a_norm", "mla_w_uq", "mla_w_ukv", "mla_q_nope_norm", "mla_q_rope_norm",
            "mla_k_nope_norm", "mla_k_rope_norm", "mla_w_out"),
}
_BIG_AXIS = {"dn_w_in": 1, "dn_w_out": 0, "sb_w_qkv": 1, "sb_w_out": 0, "mla_w_down": 0, "mla_w_uq": 1, "mla_w_ukv": 1,
             "mla_w_out": 0, "ffn_w_gate_up": 1, "ffn_w_down": 0}


def _weight_names():
    names = []
    for i in range(N_LAYERS):
        p = "l%d_" % i
        names += [p + "mix_norm"] + [p + n for n in _MIXER_PARAMS[_MIXER[i % 3]]] + [p + "ffn_norm", p + "ffn_w_gate_up", p + "ffn_w_down"]
    return names


WEIGHTS = _weight_names()
BIG = [n for n in WEIGHTS if n[3:] in _BIG_AXIS]
SMALL = [n for n in WEIGHTS if n[3:] not in _BIG_AXIS]
CONV = [n for n in SMALL if n.endswith("conv_w")]


def _ceil_to(n, k):
    return -(-n // k) * k


def _pack(arrs, cols, row_mult):
    parts = []
    for a in arrs:
        f = a.reshape(-1)
        parts.append(jnp.pad(f, (0, _ceil_to(f.shape[0], cols) - f.shape[0])))
    flat = jnp.concatenate(parts)
    rows = _ceil_to(flat.shape[0] // cols, row_mult)
    return jnp.pad(flat, (0, rows * cols - flat.shape[0])).reshape(rows, cols)


def _unpack(buf, shapes):
    cols = buf.shape[-1]
    out, r0 = [], 0
    for s in shapes:
        n = math.prod(s)
        nr = _ceil_to(n, cols) // cols
        out.append(buf[r0:r0 + nr].reshape(-1)[:n].reshape(s))
        r0 += nr
    return out


def _layer_groups(i):
    by = {"gu": (704, []), "row": (1024, []), "dn_in": (514, []), "sb_qkv": (384, []), "mla": (512, [])}
    key = {"ffn_w_gate_up": "gu", "dn_w_in": "dn_in", "sb_w_qkv": "sb_qkv", "mla_w_down": "mla", "mla_w_uq": "mla", "mla_w_ukv": "mla"}
    for n in BIG:
        if n.startswith("l%d_" % i):
            by[key.get(n[3:], "row")][1].append(n)
    return [g for g in by.values() if g[1]]


LAYER_GROUPS = [_layer_groups(i) for i in range(N_LAYERS)]


def _stack_group(grp, get):
    width, names = grp
    return jnp.concatenate([jnp.pad(get(n), ((0, 0), (0, width - get(n).shape[1]))) for n in names], axis=0)


def _unstack_group(grp, buf, shape_of):
    out, r0 = [], 0
    for n in grp[1]:
        rs, cs = shape_of(n)
        out.append(buf[..., r0:r0 + rs, :cs])
        r0 += rs
    return out


def kernel(x, l0_mix_norm, l0_dn_w_in, l0_dn_conv_w, l0_dn_a_log, l0_dn_dt_bias, l0_dn_out_norm, l0_dn_w_out, l0_ffn_norm, l0_ffn_w_gate_up, l0_ffn_w_down, l1_mix_norm, l1_sb_w_qkv, l1_sb_q_norm, l1_sb_k_norm, l1_sb_w_out, l1_ffn_norm, l1_ffn_w_gate_up, l1_ffn_w_down, l2_mix_norm, l2_mla_w_down, l2_mla_q_a_norm, l2_mla_kv_a_norm, l2_mla_w_uq, l2_mla_w_ukv, l2_mla_q_nope_norm, l2_mla_q_rope_norm, l2_mla_k_nope_norm, l2_mla_k_rope_norm, l2_mla_w_out, l2_ffn_norm, l2_ffn_w_gate_up, l2_ffn_w_down, l3_mix_norm, l3_dn_w_in, l3_dn_conv_w, l3_dn_a_log, l3_dn_dt_bias, l3_dn_out_norm, l3_dn_w_out, l3_ffn_norm, l3_ffn_w_gate_up, l3_ffn_w_down, loss_target, m_l0_mix_norm, m_l0_dn_w_in, m_l0_dn_conv_w, m_l0_dn_a_log, m_l0_dn_dt_bias, m_l0_dn_out_norm, m_l0_dn_w_out, m_l0_ffn_norm, m_l0_ffn_w_gate_up, m_l0_ffn_w_down, m_l1_mix_norm, m_l1_sb_w_qkv, m_l1_sb_q_norm, m_l1_sb_k_norm, m_l1_sb_w_out, m_l1_ffn_norm, m_l1_ffn_w_gate_up, m_l1_ffn_w_down, m_l2_mix_norm, m_l2_mla_w_down, m_l2_mla_q_a_norm, m_l2_mla_kv_a_norm, m_l2_mla_w_uq, m_l2_mla_w_ukv, m_l2_mla_q_nope_norm, m_l2_mla_q_rope_norm, m_l2_mla_k_nope_norm, m_l2_mla_k_rope_norm, m_l2_mla_w_out, m_l2_ffn_norm, m_l2_ffn_w_gate_up, m_l2_ffn_w_down, m_l3_mix_norm, m_l3_dn_w_in, m_l3_dn_conv_w, m_l3_dn_a_log, m_l3_dn_dt_bias, m_l3_dn_out_norm, m_l3_dn_w_out, m_l3_ffn_norm, m_l3_ffn_w_gate_up, m_l3_ffn_w_down, v_l0_mix_norm, v_l0_dn_w_in, v_l0_dn_conv_w, v_l0_dn_a_log, v_l0_dn_dt_bias, v_l0_dn_out_norm, v_l0_dn_w_out, v_l0_ffn_norm, v_l0_ffn_w_gate_up, v_l0_ffn_w_down, v_l1_mix_norm, v_l1_sb_w_qkv, v_l1_sb_q_norm, v_l1_sb_k_norm, v_l1_sb_w_out, v_l1_ffn_norm, v_l1_ffn_w_gate_up, v_l1_ffn_w_down, v_l2_mix_norm, v_l2_mla_w_down, v_l2_mla_q_a_norm, v_l2_mla_kv_a_norm, v_l2_mla_w_uq, v_l2_mla_w_ukv, v_l2_mla_q_nope_norm, v_l2_mla_q_rope_norm, v_l2_mla_k_nope_norm, v_l2_mla_k_rope_norm, v_l2_mla_w_out, v_l2_ffn_norm, v_l2_ffn_w_gate_up, v_l2_ffn_w_down, v_l3_mix_norm, v_l3_dn_w_in, v_l3_dn_conv_w, v_l3_dn_a_log, v_l3_dn_dt_bias, v_l3_dn_out_norm, v_l3_dn_w_out, v_l3_ffn_norm, v_l3_ffn_w_gate_up, v_l3_ffn_w_down):
    a = dict(locals())
    return _train_step(a)


def _train_step(a):
    mx, my, mc = lax.axis_index("x"), lax.axis_index("y"), lax.axis_index("c")
    dev = 4 * mx + 2 * my + mc
    dev_arr = jnp.reshape(dev, (1,)).astype(jnp.int32)
    t, d = a["x"].shape[1], a["x"].shape[2]
    xs = a["x"].reshape(t, d)
    target = a["loss_target"].reshape(t, d)

    full = {}

    def unpack_layer(i, bufs):
        for grp, buf in zip(LAYER_GROUPS[i], bufs):
            for n, shards in zip(grp[1], _unstack_group(grp, buf, lambda n: a[n].shape)):
                kind = n[3:]
                if kind == "ffn_w_gate_up":
                    full[n] = shards
                elif _BIG_AXIS[kind] == 0:
                    full[n] = shards.reshape(N_DEV * shards.shape[1], shards.shape[2])
                else:
                    width = DN_PROJ if kind == "dn_w_in" else N_DEV * shards.shape[2]
                    full[n] = _cols_from_shards(shards, width)

    def local_shards(i):
        return [_stack_group(grp, lambda n: a[n].astype(BF16)) for grp in LAYER_GROUPS[i]]

    unpack_layer(0, _all_gather_groups("gather_weights", local_shards(0)))
    gathers, started = {}, jnp.zeros((), F32)
    for i in range(1, N_LAYERS):
        srcs = local_shards(i)
        lands = [lax.dynamic_update_index_in_dim(lax.empty((N_DEV,) + s.shape, s.dtype), s, dev, 0) for s in srcs]
        gathers[i] = _push_start("gather_start_l%d" % i, srcs, lands, False)
        started = started + gathers[i][-1][0, 0]
    conv_pack = _pack([a[n] for n in CONV], LANES, 8)
    conv_all = _all_gather("gather_conv", conv_pack).reshape(N_DEV, conv_pack.shape[0], LANES)
    for n, parts in zip(CONV, zip(*[_unpack(conv_all[j], [a[n].shape for n in CONV]) for j in range(N_DEV)])):
        full[n] = jnp.concatenate(parts, axis=1)

    def vec(n):
        return a[n].reshape(1, -1)

    cos, sin = _rope_tables(t)

    def mixer_args(i):
        p = "l%d_" % i
        kind = _MIXER[i % 3]
        if kind == "dn":
            args = (full[p + "dn_w_in"], full[p + "dn_conv_w"], _pad_lanes(a[p + "dn_a_log"]), _pad_lanes(a[p + "dn_dt_bias"]),
                    vec(p + "dn_out_norm"), full[p + "dn_w_out"])
        elif kind == "sb":
            args = (full[p + "sb_w_qkv"], vec(p + "sb_q_norm"), vec(p + "sb_k_norm"), full[p + "sb_w_out"])
        else:
            w_down, w_uq, w_ukv = _mla_layout(full[p + "mla_w_down"], full[p + "mla_w_uq"], full[p + "mla_w_ukv"])
            args = (w_down, vec(p + "mla_q_a_norm"), vec(p + "mla_kv_a_norm"), w_uq, w_ukv, vec(p + "mla_q_nope_norm"),
                    _pad_lanes(a[p + "mla_q_rope_norm"]), vec(p + "mla_k_nope_norm"), _pad_lanes(a[p + "mla_k_rope_norm"]),
                    full[p + "mla_w_out"], cos, sin)
        return kind, args

    fwd = {"dn": _dn_fwd, "sb": _sb_fwd, "mla": _mla_fwd}
    bwd = {"dn": _dn_bwd, "sb": _sb_bwd, "mla": _mla_bwd}
    saved, layer_args = [], []
    for i in range(N_LAYERS):
        p = "l%d_" % i
        if i > 0:
            unpack_layer(i, _push_wait("gather_wait_l%d" % i, gathers[i], xs, False))
        kind, args = mixer_args(i)
        layer_args.append((kind, args))
        gain = vec(p + "mix_norm") + started if i == 0 else vec(p + "mix_norm")
        h = _rmsnorm_fwd("mix_norm", xs, gain)
        x_mid, sv_mix = fwd[kind](xs, h, *args)
        x_out, sv_ffn = _ffn_fwd(x_mid, vec(p + "ffn_norm"), full[p + "ffn_w_gate_up"], full[p + "ffn_w_down"])
        saved.append((xs, sv_mix, sv_ffn))
        xs = x_out
    dy, loss_part = _loss_head(xs, target)

    grads, big_out = {}, {}

    def grad_shards(n):
        g, (rs, cs) = grads[n], a[n].shape
        if g.ndim == 3:
            return g
        if _BIG_AXIS[n[3:]] == 0:
            return g.reshape(N_DEV, rs, cs)
        return _shards_from_cols(g, cs)

    def push_grads(i):
        owns, sends = [], []
        for grp in LAYER_GROUPS[i]:
            parts = [jnp.pad(grad_shards(n), ((0, 0), (0, 0), (0, grp[0] - a[n].shape[1]))) for n in grp[1]]
            g_all = jnp.concatenate(parts, axis=1)
            owns.append(lax.dynamic_index_in_dim(g_all, dev, 0, keepdims=False))
            sends.append(g_all.astype(BF16))
        lands = [lax.empty(s.shape, s.dtype) for s in sends]
        return owns, _push_start("grads_start_l%d" % i, sends, lands, True)

    def finish_grads(i, owns, pushed, after):
        recvs = _push_wait("grads_wait_l%d" % i, pushed, after, True)
        for grp, own, recv in zip(LAYER_GROUPS[i], owns, recvs):
            packs = [_stack_group(grp, lambda n, pre=pre: a[pre + n]) for pre in ("", "m_", "v_")]
            outs = [_unstack_group(grp, o, lambda n: a[n].shape) for o in _sum_adam_devices(own, recv, dev_arr, *packs)]
            for j, n in enumerate(grp[1]):
                big_out[n] = [o[j] for o in outs]

    in_flight = None
    for i in reversed(range(N_LAYERS)):
        p = "l%d_" % i
        kind, args = layer_args[i]
        x_in, sv_mix, sv_ffn = saved[i]
        gain = vec(p + "ffn_norm") if in_flight is None else vec(p + "ffn_norm") + in_flight[2][-1][0, 0]
        dx_mid, grads[p + "ffn_norm"], grads[p + "ffn_w_gate_up"], grads[p + "ffn_w_down"] = _ffn_bwd(
            sv_ffn, dy, gain, full[p + "ffn_w_gate_up"], full[p + "ffn_w_down"])
        res = bwd[kind](sv_mix, dx_mid, *args)
        dh = res[0]
        if kind == "mla":
            res = list(res)
            res[1], res[4], res[5] = _mla_unlayout(res[1], res[4], res[5])
        for n, g in zip(_MIXER_PARAMS[kind], res[1:]):
            grads[p + n] = g
        dy, grads[p + "mix_norm"] = _rmsnorm_bwd("mix_norm_bwd", x_in, vec(p + "mix_norm"), dh, dx_mid)
        owns, pushed = push_grads(i)
        if in_flight is not None:
            finish_grads(in_flight[0], in_flight[1], in_flight[2], dy)
        in_flight = (i, owns, pushed)
    finish_grads(in_flight[0], in_flight[1], in_flight[2], dy)
    grad_x = dy.reshape(a["x"].shape)

    small_full_shapes = [full[n].shape if n in CONV else a[n].shape for n in SMALL]
    small_grads = []
    for n, s in zip(SMALL, small_full_shapes):
        g = grads[n].reshape(-1)
        small_grads.append(g[:math.prod(s)])
    small_pack = _pack(small_grads + [loss_part.reshape(-1)], LANES, 8)
    small_sum = _sum_devices(_all_gather("gather_small_grads", small_pack))
    small_red = _unpack(small_sum, small_full_shapes + [(LANES,)])
    loss = small_red[-1][0]
    g_small = {}
    for n, g in zip(SMALL, small_red[:-1]):
        if n in CONV:
            cs = a[n].shape[1]
            g = lax.dynamic_slice_in_dim(g, dev * cs, cs, axis=1)
        g_small[n] = g
    small_shapes = [a[n].shape for n in SMALL]
    packs = [_pack([src[n] for n in SMALL], LANES, 8) for src in
             ({n: a[n] for n in SMALL}, g_small, {n: a["m_" + n] for n in SMALL}, {n: a["v_" + n] for n in SMALL})]
    d_small, m_small, v_small = (_unpack(o, small_shapes) for o in _adam_small(*packs))

    small_out = dict(zip(SMALL, zip([g_small[n] for n in SMALL], d_small, m_small, v_small)))

    def out(k):
        return [small_out[n][k] if n in small_out else big_out[n][k] for n in WEIGHTS]

    return (loss, grad_x, *out(0), *out(1), *out(2), *out(3))
```

```python
import math

import jax
import jax.numpy as jnp
from jax import lax
from jax.experimental import pallas as pl
from jax.experimental.pallas import tpu as pltpu

F32 = jnp.float32
BF16 = jnp.bfloat16
F32X3 = lax.Precision.HIGH

LANES = 128
N_DEV = 8
N_HEADS = 8
HEAD = 128
NORM_EPS = 1e-6
DN_CHUNK = 64
ATT_BLOCK = 512
ATT_Q = 512
MLA_ROPE = 64
MLA_QK = 192
ROPE_THETA = 10000.0
VMEM_LIMIT = 56 * 1024 * 1024

ADAM_LR = 0.001
ADAM_B1 = 0.9
ADAM_B2 = 0.999
ADAM_EPS = 1e-08
ADAM_WD = 0.01
ADAM_STEP = 10


def _cparams(**kw):
    return pltpu.CompilerParams(vmem_limit_bytes=VMEM_LIMIT, **kw)


def _pick(n, cands):
    for c in cands:
        if c <= n and n % c == 0:
            return c
    return n


def _mm(name, a, b, *, ta=False, tb=False, out_dtype=F32, add=None, tm=None, tn=None, tk=None):
    if ta:
        K, M = a.shape
    else:
        M, K = a.shape
    N = b.shape[0] if tb else b.shape[1]
    tm = tm or _pick(M, (1024, 512, 256, 128))
    tn = tn or _pick(N, (1024, 512, 384, 256, 128))
    tk = tk or _pick(K, (1024, 1408, 512, 384, 256, 128))
    return _mm_raw(
        name, a, b, ta=ta, tb=tb, out_dtype=out_dtype, add=add, grid=(M // tm, N // tn, K // tk), out_shape=(M, N),
        a_block=(tk, tm) if ta else (tm, tk), a_map=(lambda i, j, k: (k, i)) if ta else (lambda i, j, k: (i, k)),
        b_block=(tn, tk) if tb else (tk, tn), b_map=(lambda i, j, k: (j, k)) if tb else (lambda i, j, k: (k, j)),
        o_block=(tm, tn), o_map=lambda i, j, k: (i, j))


def _mm_raw(name, a, b, *, ta, tb, out_dtype, add, grid, out_shape, a_block, a_map, b_block, b_map, o_block, o_map):
    nk = grid[2]
    tm, tn = o_block
    dn = (((0 if ta else 1,), (1 if tb else 0,)), ((), ()))
    has_add = add is not None

    def kern(*refs):
        if has_add:
            a_ref, b_ref, add_ref, o_ref, acc_ref = refs
        else:
            a_ref, b_ref, o_ref, acc_ref = refs
        k = pl.program_id(2)
        part = lax.dot_general(a_ref[...].astype(BF16), b_ref[...].astype(BF16), dn, preferred_element_type=F32)

        @pl.when(k == 0)
        def _():
            acc_ref[...] = part

        @pl.when(k > 0)
        def _():
            acc_ref[...] += part

        @pl.when(k == nk - 1)
        def _():
            r = acc_ref[...]
            if has_add:
                r = r + add_ref[...]
            o_ref[...] = r.astype(out_dtype)

    in_specs = [pl.BlockSpec(a_block, a_map), pl.BlockSpec(b_block, b_map)]
    args = [a, b]
    if has_add:
        in_specs.append(pl.BlockSpec(o_block, o_map))
        args.append(add)
    return pl.pallas_call(
        kern, name=name,
        grid=grid,
        in_specs=in_specs,
        out_specs=pl.BlockSpec(o_block, o_map),
        out_shape=jax.ShapeDtypeStruct(out_shape, out_dtype),
        scratch_shapes=[pltpu.VMEM((tm, tn), F32)],
        compiler_params=_cparams(dimension_semantics=("parallel", "parallel", "arbitrary")),
    )(*args)


def _rows(name, body, ins, outs, *, tt, consts=(), accs=()):
    in_specs, args = [], []
    first = ins[0][0] if isinstance(ins[0], tuple) else ins[0]
    t = first.shape[-2]
    tt = min(tt, t)
    for x in ins:
        if isinstance(x, tuple):
            arr, bs, im = x
            in_specs.append(pl.BlockSpec(bs, im))
            args.append(arr)
        else:
            in_specs.append(_row_spec(x.shape, tt))
            args.append(x)
    for c in consts:
        in_specs.append(pl.BlockSpec(c.shape, lambda i, _n=c.ndim: (0,) * _n))
        args.append(c)
    out_specs = [_row_spec(o.shape, tt) for o in outs]
    out_specs += [pl.BlockSpec(a.shape, lambda i, _n=len(a.shape): (0,) * _n) for a in accs]
    res = pl.pallas_call(
        body, name=name, grid=(t // tt,),
        in_specs=in_specs, out_specs=out_specs, out_shape=list(outs) + list(accs),
        compiler_params=_cparams(dimension_semantics=("arbitrary",)),
    )(*args)
    return res


def _row_spec(shape, tt):
    if len(shape) == 2:
        return pl.BlockSpec((tt, shape[1]), lambda i: (i, 0))
    return pl.BlockSpec((shape[0], tt, shape[2]), lambda i: (0, i, 0))


def _sds(shape, dtype=F32):
    return jax.ShapeDtypeStruct(tuple(shape), dtype)


def _acc(ref, val):
    i = pl.program_id(0)

    @pl.when(i == 0)
    def _():
        ref[...] = val

    @pl.when(i > 0)
    def _():
        ref[...] += val


def _rms(x, g):
    return x * lax.rsqrt(jnp.mean(x * x, axis=-1, keepdims=True) + NORM_EPS) * g


def _silu(x):
    return x / (1.0 + jnp.exp(-x))


def _softplus(x):
    return jnp.maximum(x, 0.0) + jnp.log(1.0 + jnp.exp(-jnp.abs(x)))


def _sigmoid(x):
    return 1.0 / (1.0 + jnp.exp(-x))


def _rmsnorm_fwd(name, x, g, tt=512):
    def body(x_ref, g_ref, h_ref):
        h_ref[...] = _rms(x_ref[...], g_ref[...]).astype(BF16)

    return _rows(name, body, [x], [_sds(x.shape, BF16)], tt=tt, consts=[g])[0]


def _rmsnorm_bwd(name, x, g, dh, dres, tt=512):
    def body(x_ref, dh_ref, dres_ref, g_ref, dx_ref, dg_ref):
        _, vjp = jax.vjp(_rms, x_ref[...], g_ref[...])
        dx, dg = vjp(dh_ref[...])
        dx_ref[...] = dx + dres_ref[...]
        _acc(dg_ref, dg)

    return _rows(name, body, [x, dh, dres], [_sds(x.shape)], tt=tt, consts=[g], accs=[_sds(g.shape)])


def _ffn_fwd(x, norm_g, w3, w_down):
    t, d = x.shape
    ns, _, cs = w3.shape
    half = ns // 2
    w2 = w3.reshape(ns * d, cs)
    h = _rmsnorm_fwd("ffn_norm", x, norm_g)
    tm = _pick(t, (1024, 512, 256, 128))
    nm = t // tm

    def gate_up(h_ref, wg_ref, wu_ref, g_ref, u_ref, a_ref):
        hv = h_ref[...]
        g = _dot(hv, wg_ref[...])
        u = _dot(hv, wu_ref[...])
        g_ref[...] = g
        u_ref[...] = u
        a_ref[...] = (_silu(g) * u).astype(BF16)

    hid = pl.BlockSpec((tm, cs), lambda j, i: (j * nm + i, 0))
    g, u, act = pl.pallas_call(
        gate_up, name="ffn_gate_up", grid=(half, nm),
        in_specs=[pl.BlockSpec((tm, d), lambda j, i: (i, 0)), pl.BlockSpec((d, cs), lambda j, i: (j, 0)),
                  pl.BlockSpec((d, cs), lambda j, i: (j + half, 0))],
        out_specs=[hid, hid, hid], out_shape=[_sds((half * t, cs)), _sds((half * t, cs)), _sds((half * t, cs), BF16)],
        compiler_params=_cparams(dimension_semantics=("parallel", "arbitrary")),
    )(h, w2, w2)
    y = _mm_raw("ffn_down", act, w_down, ta=False, tb=False, out_dtype=F32, add=x, grid=(nm, 1, half), out_shape=(t, d),
                a_block=(tm, cs), a_map=lambda i, j, k: (k * nm + i, 0), b_block=(cs, d), b_map=lambda i, j, k: (k, 0),
                o_block=(tm, d), o_map=lambda i, j, k: (i, 0))
    return y, (x, h, g, u, act)


def _ffn_bwd(saved, dy, norm_g, w3, w_down):
    x, h, g, u, act = saved
    t, d = x.shape
    ns, _, cs = w3.shape
    half = ns // 2
    w2 = w3.reshape(ns * d, cs)
    tm = _pick(t, (1024, 512, 256, 128))
    nm = t // tm
    tk = _pick(t, (1024, 512, 256, 128))
    nk = t // tk
    d_wdown = _mm_raw("ffn_down_wgrad", act, dy, ta=True, tb=False, out_dtype=F32, add=None, grid=(half, 1, nk),
                      out_shape=(half * cs, d), a_block=(tk, cs), a_map=lambda i, j, k: (i * nk + k, 0),
                      b_block=(tk, d), b_map=lambda i, j, k: (k, 0), o_block=(cs, d), o_map=lambda i, j, k: (i, 0))
    def down_dgrad(dy_ref, wd_ref, g_ref, u_ref, dg_ref, du_ref):
        da = _dot_nt(dy_ref[...].astype(BF16), wd_ref[...])
        gv, uv = g_ref[...], u_ref[...]
        s = _sigmoid(gv)
        dg_ref[...] = (da * uv * s * (1.0 + gv * (1.0 - s))).astype(BF16)
        du_ref[...] = (da * gv * s).astype(BF16)

    hid = pl.BlockSpec((tm, cs), lambda j, i: (j * nm + i, 0))
    dg, du = pl.pallas_call(
        down_dgrad, name="ffn_down_dgrad", grid=(half, nm),
        in_specs=[pl.BlockSpec((tm, d), lambda j, i: (i, 0)), pl.BlockSpec((cs, d), lambda j, i: (j, 0)), hid, hid],
        out_specs=[hid, hid], out_shape=[_sds((half * t, cs), BF16), _sds((half * t, cs), BF16)],
        compiler_params=_cparams(dimension_semantics=("parallel", "arbitrary")),
    )(dy, w_down, g, u)

    def wgrad(name, dd):
        return _mm_raw(name, h, dd, ta=True, tb=False, out_dtype=F32, add=None, grid=(1, half, nk), out_shape=(half * d, cs),
                       a_block=(tk, d), a_map=lambda i, j, k: (k, 0), b_block=(tk, cs), b_map=lambda i, j, k: (j * nk + k, 0),
                       o_block=(d, cs), o_map=lambda i, j, k: (j, 0))

    def dgrad(name, dd, off, add):
        return _mm_raw(name, dd, w2, ta=False, tb=True, out_dtype=F32, add=add, grid=(nm, 1, half), out_shape=(t, d),
                       a_block=(tm, cs), a_map=lambda i, j, k: (k * nm + i, 0), b_block=(d, cs), b_map=lambda i, j, k: (k + off, 0),
                       o_block=(tm, d), o_map=lambda i, j, k: (i, 0))

    d_w3 = jnp.concatenate([wgrad("ffn_gate_wgrad", dg), wgrad("ffn_up_wgrad", du)], axis=0).reshape(ns, d, cs)
    dh = dgrad("ffn_up_dgrad", du, half, dgrad("ffn_gate_dgrad", dg, 0, None))
    dx, dgain = _rmsnorm_bwd("ffn_norm_bwd", x, norm_g, dh, dy)
    return dx, dgain, d_w3, d_wdown


def _dot_nt(a, b):
    return lax.dot_general(a, b, (((1,), (1,)), ((), ())), preferred_element_type=F32)


def _dot_tn(a, b):
    return lax.dot_general(a, b, (((0,), (0,)), ((), ())), preferred_element_type=F32)


def _dot(a, b):
    return jnp.dot(a, b, preferred_element_type=F32)


CUM_BLOCK = 128


def _tri2(lower):
    r = lax.broadcasted_iota(jnp.int32, (CUM_BLOCK, CUM_BLOCK), 0)
    c = lax.broadcasted_iota(jnp.int32, (CUM_BLOCK, CUM_BLOCK), 1)
    tri = ((r > c) if lower else (r < c)).astype(BF16)
    return jnp.concatenate([tri, tri], axis=0)


def _run_sums(x, tri2, run, reverse):
    nb = x.shape[1] // CUM_BLOCK
    outs = [None] * nb
    for j in (reversed(range(nb)) if reverse else range(nb)):
        xj = x[:, j * CUM_BLOCK:(j + 1) * CUM_BLOCK]
        hi = xj.astype(BF16)
        lo = (xj - hi.astype(F32)).astype(BF16)
        outs[j] = _dot(jnp.concatenate([hi, lo], axis=1), tri2) + run
        run = run + jnp.sum(xj, axis=1, keepdims=True)
    return jnp.concatenate(outs, axis=1), run


def _log_sigmoid(z):
    return jnp.minimum(z, 0.0) - jnp.log(1.0 + jnp.exp(-jnp.abs(z)))


def _heads_in(ref, h, width=HEAD):
    return ref[:, h * width:(h + 1) * width]


def _sb_qk(q, k, gq, gk):
    return _rms(q, gq) * (HEAD ** -0.5), _rms(k, gk)


def _sb_prep_fwd(qkv, gq, gk):
    t = qkv.shape[0]

    def body(x_ref, gq_ref, gk_ref, q_ref, k_ref, v_ref):
        for h in range(N_HEADS):
            q, k = _sb_qk(_heads_in(x_ref, h), _heads_in(x_ref, N_HEADS + h), gq_ref[...], gk_ref[...])
            q_ref[h] = q.astype(BF16)
            k_ref[h] = k.astype(BF16)
            v_ref[h] = _heads_in(x_ref, 2 * N_HEADS + h).astype(BF16)

    hm = _sds((N_HEADS, t, HEAD), BF16)
    return _rows("sb_prep", body, [qkv], [hm, hm, hm], tt=256, consts=[gq, gk])


def _sb_prep_bwd(qkv, gq, gk, dq, dk, dv):
    def body(x_ref, dq_ref, dk_ref, dv_ref, gq_ref, gk_ref, dx_ref, dgq_ref, dgk_ref):
        dgq = jnp.zeros(gq_ref.shape, F32)
        dgk = jnp.zeros(gk_ref.shape, F32)
        for h in range(N_HEADS):
            _, vjp = jax.vjp(_sb_qk, _heads_in(x_ref, h), _heads_in(x_ref, N_HEADS + h), gq_ref[...], gk_ref[...])
            a, b, c, d = vjp((dq_ref[h], dk_ref[h]))
            dx_ref[:, h * HEAD:(h + 1) * HEAD] = a.astype(BF16)
            dx_ref[:, (N_HEADS + h) * HEAD:(N_HEADS + h + 1) * HEAD] = b.astype(BF16)
            dx_ref[:, (2 * N_HEADS + h) * HEAD:(2 * N_HEADS + h + 1) * HEAD] = dv_ref[h].astype(BF16)
            dgq, dgk = dgq + c, dgk + d
        _acc(dgq_ref, dgq)
        _acc(dgk_ref, dgk)

    return _rows("sb_prep_bwd", body, [qkv, dq, dk, dv], [_sds(qkv.shape, BF16)], tt=256, consts=[gq, gk],
                 accs=[_sds(gq.shape), _sds(gk.shape)])


def _q_block(t):
    return min(ATT_Q, t)


def _key_order(bq, qb):
    rows = lax.broadcasted_iota(jnp.int32, (bq, ATT_BLOCK), 0)
    cols = lax.broadcasted_iota(jnp.int32, (bq, ATT_BLOCK), 1)
    return rows - cols + qb * bq


def _sb_attn_fwd(q, k, v):
    nh, t, _ = q.shape
    bq = _q_block(t)
    per = bq // ATT_BLOCK

    def kern(q_ref, k_ref, v_ref, o_ref):
        qb = pl.program_id(1)
        qv = q_ref[0]
        after = _tri2(True)
        order = _key_order(bq, qb)
        nkb = (qb + 1) * per

        def body(i, carry):
            o_acc, run = carry
            kb = nkb - 1 - i
            off = pl.multiple_of(kb * ATT_BLOCK, ATT_BLOCK)
            kv = k_ref[0, pl.ds(off, ATT_BLOCK), :]
            vv = v_ref[0, pl.ds(off, ATT_BLOCK), :]
            z = _dot_nt(qv, kv)
            past = order > kb * ATT_BLOCK
            lsz = _log_sigmoid(z)
            lsn = jnp.where(past, lsz - z, 0.0)
            la, run = _run_sums(lsn, after, run, True)
            a = jnp.where(past, jnp.exp(lsz + la), 0.0)
            o_acc = o_acc + _dot(a.astype(BF16), vv)
            return o_acc, run

        o, _ = lax.fori_loop(0, nkb, body, (jnp.zeros((bq, HEAD), F32), jnp.zeros((bq, 1), F32)))
        o_ref[...] = o

    return pl.pallas_call(
        kern, name="sb_attn_fwd", grid=(nh, t // bq),
        in_specs=[pl.BlockSpec((1, bq, HEAD), lambda h, i: (h, i, 0)),
                  pl.BlockSpec((1, t, HEAD), lambda h, i: (h, 0, 0)),
                  pl.BlockSpec((1, t, HEAD), lambda h, i: (h, 0, 0))],
        out_specs=pl.BlockSpec((bq, HEAD), lambda h, i: (i, h)),
        out_shape=_sds((t, nh * HEAD)),
        compiler_params=_cparams(dimension_semantics=("parallel", "arbitrary")),
    )(q, k, v)


def _sb_attn_bwd(q, k, v, do):
    nh, t, _ = q.shape
    bq = _q_block(t)
    per = bq // ATT_BLOCK

    def kern(q_ref, k_ref, v_ref, do_ref, dq_ref, dk_ref, dv_ref, g_s, ls_s):
        qb = pl.program_id(1)

        @pl.when(qb == 0)
        def _():
            dk_ref[...] = jnp.zeros(dk_ref.shape, F32)
            dv_ref[...] = jnp.zeros(dv_ref.shape, F32)

        qv = q_ref[0]
        dob = do_ref[...].astype(BF16)
        after, before = _tri2(True), _tri2(False)
        order = _key_order(bq, qb)
        nkb = (qb + 1) * per

        def sweep_left(i, run):
            kb = nkb - 1 - i
            off = pl.multiple_of(kb * ATT_BLOCK, ATT_BLOCK)
            kv = k_ref[0, pl.ds(off, ATT_BLOCK), :]
            vv = v_ref[0, pl.ds(off, ATT_BLOCK), :]
            z = _dot_nt(qv, kv)
            past = order > kb * ATT_BLOCK
            lsz = _log_sigmoid(z)
            lsn = jnp.where(past, lsz - z, 0.0)
            la, run = _run_sums(lsn, after, run, True)
            a = jnp.where(past, jnp.exp(lsz + la), 0.0)
            g_s[kb] = _dot_nt(dob, vv) * a
            ls_s[kb] = lsz
            dv_ref[0, pl.ds(off, ATT_BLOCK), :] += _dot_tn(a.astype(BF16), dob)
            return run

        zero = jnp.zeros((bq, 1), F32)
        lax.fori_loop(0, nkb, sweep_left, zero)

        def sweep_right(kb, carry):
            dq_acc, run_g = carry
            off = pl.multiple_of(kb * ATT_BLOCK, ATT_BLOCK)
            kv = k_ref[0, pl.ds(off, ATT_BLOCK), :]
            g = g_s[kb]
            sg = jnp.exp(ls_s[kb])
            past = order > kb * ATT_BLOCK
            dls, run_g = _run_sums(g, before, run_g, False)
            dzb = jnp.where(past, g * (1.0 - sg) - dls * sg, 0.0).astype(BF16)
            dk_ref[0, pl.ds(off, ATT_BLOCK), :] += _dot_tn(dzb, qv)
            return dq_acc + _dot(dzb, kv), run_g

        dq, _ = lax.fori_loop(0, nkb, sweep_right, (jnp.zeros((bq, HEAD), F32), zero))
        dq_ref[0] = dq

    hm = _sds((nh, t, HEAD))
    full = pl.BlockSpec((1, t, HEAD), lambda h, i: (h, 0, 0))
    tok = pl.BlockSpec((bq, HEAD), lambda h, i: (i, h))
    nkb_max = t // ATT_BLOCK
    return pl.pallas_call(
        kern, name="sb_attn_bwd", grid=(nh, t // bq),
        in_specs=[pl.BlockSpec((1, bq, HEAD), lambda h, i: (h, i, 0)), full, full, tok],
        out_specs=[pl.BlockSpec((1, bq, HEAD), lambda h, i: (h, i, 0)), full, full],
        out_shape=[hm, hm, hm],
        scratch_shapes=[pltpu.VMEM((nkb_max, bq, ATT_BLOCK), F32), pltpu.VMEM((nkb_max, bq, ATT_BLOCK), F32)],
        compiler_params=_cparams(dimension_semantics=("parallel", "arbitrary")),
    )(q, k, v, do)


def _sb_fwd(x, h, w_qkv, gq, gk, w_out):
    qkv = _mm("sb_qkv", h, w_qkv)
    q, k, v = _sb_prep_fwd(qkv, gq, gk)
    o = _sb_attn_fwd(q, k, v)
    y = _mm("sb_out", o, w_out, add=x)
    return y, (h, qkv, q, k, v, o)


def _sb_bwd(saved, dy, w_qkv, gq, gk, w_out):
    h, qkv, q, k, v, o = saved
    d_wout = _mm("sb_out_wgrad", o, dy, ta=True)
    do = _mm("sb_out_dgrad", dy, w_out, tb=True)
    dq, dk, dv = _sb_attn_bwd(q, k, v, do)
    dqkv, dgq, dgk = _sb_prep_bwd(qkv, gq, gk, dq, dk, dv)
    d_wqkv = _mm("sb_qkv_wgrad", h, dqkv, ta=True)
    dh = _mm("sb_qkv_dgrad", dqkv, w_qkv, tb=True)
    return dh, d_wqkv, dgq, dgk, d_wout


DN_QKV = 3 * N_HEADS * HEAD
DN_PROJ = DN_QKV + N_HEADS * HEAD + LANES
DN_CONV = 4
HALO = 8
CONV_COLS = 512


def _dn_conv_fwd(proj, conv_w, tt=256):
    t = proj.shape[0]
    tt = min(tt, t)

    def body(u_ref, prev_ref, w_ref, c_ref):
        i = pl.program_id(0)
        for cc in range(DN_QKV // CONV_COLS):
            cs = slice(cc * CONV_COLS, (cc + 1) * CONV_COLS)
            cur = u_ref[:, cs]
            prev = jnp.where(i > 0, prev_ref[:, cs], 0.0)
            ext = jnp.concatenate([prev, cur], axis=0)
            y = cur * w_ref[DN_CONV - 1:DN_CONV, cs]
            for j in range(DN_CONV - 1):
                y = y + pltpu.roll(ext, DN_CONV - 1 - j, 0)[HALO:] * w_ref[j:j + 1, cs]
            c_ref[:, cs] = y

    return _rows("dn_conv", body,
                 [(proj, (tt, DN_QKV), lambda i: (i, 0)),
                  (proj, (HALO, DN_QKV), lambda i: (jnp.maximum(i * (tt // HALO) - 1, 0), 0))],
                 [_sds((t, DN_QKV))], tt=tt, consts=[conv_w])[0]


def _dn_conv_bwd(proj, conv_w, dc, dz, dab, tt=256):
    t = proj.shape[0]
    tt = min(tt, t)
    nblk = t // tt

    def body(u_ref, prev_ref, dc_ref, next_ref, dz_ref, dab_ref, w_ref, dp_ref, dw_ref):
        i = pl.program_id(0)
        dws = []
        for cc in range(DN_QKV // CONV_COLS):
            cs = slice(cc * CONV_COLS, (cc + 1) * CONV_COLS)
            cur = u_ref[:, cs]
            prev = jnp.where(i > 0, prev_ref[:, cs], 0.0)
            ext_u = jnp.concatenate([prev, cur], axis=0)
            d = dc_ref[:, cs]
            nxt = jnp.where(i < nblk - 1, next_ref[:, cs], 0.0)
            ext_d = jnp.concatenate([d, nxt], axis=0)
            du = d * w_ref[DN_CONV - 1:DN_CONV, cs]
            rows = [jnp.sum(d * cur, axis=0, keepdims=True)]
            for j in range(DN_CONV - 2, -1, -1):
                sh = DN_CONV - 1 - j
                du = du + pltpu.roll(ext_d, tt + HALO - sh, 0)[:tt] * w_ref[j:j + 1, cs]
                rows.insert(0, jnp.sum(d * pltpu.roll(ext_u, sh, 0)[HALO:], axis=0, keepdims=True))
            dp_ref[:, cs] = du.astype(BF16)
            dws.append(jnp.concatenate(rows, axis=0))
        dp_ref[:, DN_QKV:DN_QKV + N_HEADS * HEAD] = dz_ref[...].astype(BF16)
        dp_ref[:, DN_QKV + N_HEADS * HEAD:] = dab_ref[...].astype(BF16)
        _acc(dw_ref, jnp.concatenate(dws, axis=1))

    return _rows("dn_conv_bwd", body,
                 [(proj, (tt, DN_QKV), lambda i: (i, 0)),
                  (proj, (HALO, DN_QKV), lambda i: (jnp.maximum(i * (tt // HALO) - 1, 0), 0)),
                  dc,
                  (dc, (HALO, DN_QKV), lambda i: (jnp.minimum((i + 1) * (tt // HALO), t // HALO - 1), 0)),
                  dz, dab],
                 [_sds((t, DN_PROJ), BF16)], tt=tt, consts=[conv_w], accs=[_sds(conv_w.shape)])


def _l2n(x):
    return x * lax.rsqrt(jnp.sum(x * x, axis=-1, keepdims=True) + NORM_EPS)


def _dn_qkv(cq, ck, cv):
    return _l2n(_silu(cq)) * (HEAD ** -0.5), _l2n(_silu(ck)), _silu(cv)


def _dn_gates(ab, a_log, dt_bias):
    lane = lax.broadcasted_iota(jnp.int32, ab.shape, 1)
    g = -jnp.exp(a_log) * _softplus(ab + dt_bias)
    return jnp.where(lane < N_HEADS, g, jnp.where(lane < 2 * N_HEADS, _sigmoid(ab), 0.0))


def _ab_spec(tt):
    return (tt, LANES), lambda i: (i, DN_PROJ // LANES - 1)


def _dn_prep_fwd(c, proj, a_log, dt_bias, tt=256):
    t = c.shape[0]
    tt = min(tt, t)

    def body(c_ref, ab_ref, al_ref, dt_ref, q_ref, k_ref, v_ref, g_ref):
        for h in range(N_HEADS):
            q_ref[h], k_ref[h], v_ref[h] = _dn_qkv(_heads_in(c_ref, h), _heads_in(c_ref, N_HEADS + h), _heads_in(c_ref, 2 * N_HEADS + h))
        g_ref[...] = _dn_gates(ab_ref[...], al_ref[...], dt_ref[...])

    hm = _sds((N_HEADS, t, HEAD))
    return _rows("dn_prep", body, [c, (proj,) + _ab_spec(tt)], [hm, hm, hm, _sds((t, LANES))], tt=tt, consts=[a_log, dt_bias])


def _dn_prep_bwd(c, proj, a_log, dt_bias, dq, dk, dv, dgates, tt=256):
    t = c.shape[0]
    tt = min(tt, t)

    def body(c_ref, ab_ref, dq_ref, dk_ref, dv_ref, dg_ref, al_ref, dt_ref, dc_ref, dab_ref, dal_ref, ddt_ref):
        for h in range(N_HEADS):
            _, vjp = jax.vjp(_dn_qkv, _heads_in(c_ref, h), _heads_in(c_ref, N_HEADS + h), _heads_in(c_ref, 2 * N_HEADS + h))
            a, b, d = vjp((dq_ref[h], dk_ref[h], dv_ref[h]))
            dc_ref[:, h * HEAD:(h + 1) * HEAD] = a
            dc_ref[:, (N_HEADS + h) * HEAD:(N_HEADS + h + 1) * HEAD] = b
            dc_ref[:, (2 * N_HEADS + h) * HEAD:(2 * N_HEADS + h + 1) * HEAD] = d
        _, vjp = jax.vjp(_dn_gates, ab_ref[...], al_ref[...], dt_ref[...])
        dab, dal, ddt = vjp(dg_ref[...])
        dab_ref[...] = dab
        _acc(dal_ref, dal)
        _acc(ddt_ref, ddt)

    return _rows("dn_prep_bwd", body, [c, (proj,) + _ab_spec(tt), dq, dk, dv, dgates], [_sds(c.shape), _sds((t, LANES))],
                 tt=tt, consts=[a_log, dt_bias], accs=[_sds(a_log.shape), _sds(dt_bias.shape)])


def _bdot(a, b, prec=None):
    return lax.dot_general(a, b, (((2,), (1,)), ((0,), (0,))), precision=prec, preferred_element_type=F32)


def _bdot_nt(a, b, prec=None):
    return lax.dot_general(a, b, (((2,), (2,)), ((0,), (0,))), precision=prec, preferred_element_type=F32)


def _bdot_tn(a, b, prec=None):
    return lax.dot_general(a, b, (((1,), (1,)), ((0,), (0,))), precision=prec, preferred_element_type=F32)


def _inv_raw(low):
    c = low.shape[-1]
    r = lax.broadcasted_iota(jnp.int32, (c, c), 0)
    s = lax.broadcasted_iota(jnp.int32, (c, c), 1)
    m = jnp.where(r == s, 1.0, 0.0) - low
    p = _bdot(low, low, F32X3)
    n_fac = int(math.log2(c)) - 1
    for i in range(n_fac):
        m = m + _bdot(m, p, F32X3)
        if i < n_fac - 1:
            p = _bdot(p, p, F32X3)
    return m


@jax.custom_vjp
def _inv_unit_lower(low):
    return _inv_raw(low)


def _inv_fwd(low):
    m = _inv_raw(low)
    return m, m


def _inv_bwd(m, dm):
    return (-_bdot_nt(_bdot_tn(m, dm, F32X3), m, F32X3),)


_inv_unit_lower.defvjp(_inv_fwd, _inv_bwd)


def _dn_chunk(q, k, v, gates, s):
    nh, c, _ = q.shape
    lane = lax.broadcasted_iota(jnp.int32, gates.shape, 1)
    def column(j):
        return jnp.sum(jnp.where(lane == j, gates, 0.0), axis=1, keepdims=True)[None]

    g_col = jnp.concatenate([column(h) for h in range(nh)], axis=0)
    b_col = jnp.concatenate([column(h + nh) for h in range(nh)], axis=0)
    r = lax.broadcasted_iota(jnp.int32, (c, c), 0)
    cc = lax.broadcasted_iota(jnp.int32, (c, c), 1)
    causal, strict = r >= cc, r > cc
    incl = jnp.broadcast_to(jnp.where(causal, 1.0, 0.0), (nh, c, c))
    upper = jnp.broadcast_to(jnp.where(r <= cc, 1.0, 0.0), (nh, c, c))
    gb = jnp.broadcast_to(g_col, (nh, c, LANES))
    gbc = jnp.broadcast_to(g_col, (nh, c, c))
    gc = _bdot(incl, gb, F32X3)
    gc_r = _bdot(incl, gbc, F32X3)
    gc_c = _bdot_tn(gbc, upper, F32X3)
    decay = jnp.where(causal, jnp.exp(jnp.where(causal, gc_r - gc_c, 0.0)), 0.0)
    kb = k * b_col
    low = jnp.where(strict, _bdot_nt(kb, k) * decay, 0.0)
    m = _inv_unit_lower(low)
    egc = jnp.exp(gc)
    u = _bdot(m, v * b_col, F32X3)
    w = _bdot(m, kb * egc, F32X3)
    attn = _bdot_nt(q, k) * decay
    gl = jnp.sum(gb, axis=1, keepdims=True)
    v_new = u - _bdot(w, s)
    o = _bdot(q * egc, s) + _bdot(attn, v_new)
    s_new = s * jnp.exp(gl) + _bdot_tn(k * jnp.exp(gl - gc), v_new)
    return o, s_new


def _dn_chunks_fwd(q, k, v, gates):
    nh, t, _ = q.shape
    n = t // DN_CHUNK

    def kern(q_ref, k_ref, v_ref, g_ref, o_ref, sin_ref, s_scr):
        @pl.when(pl.program_id(0) == 0)
        def _():
            s_scr[...] = jnp.zeros(s_scr.shape, F32)

        s = s_scr[...]
        sin_ref[0] = s
        o_ref[...], s_scr[...] = _dn_chunk(q_ref[...], k_ref[...], v_ref[...], g_ref[...], s)

    blk = pl.BlockSpec((nh, DN_CHUNK, HEAD), lambda i: (0, i, 0))
    return pl.pallas_call(
        kern, name="dn_chunks_fwd", grid=(n,),
        in_specs=[blk, blk, blk, pl.BlockSpec((DN_CHUNK, LANES), lambda i: (i, 0))],
        out_specs=[blk, pl.BlockSpec((1, nh, HEAD, HEAD), lambda i: (i, 0, 0, 0))],
        out_shape=[_sds((nh, t, HEAD)), _sds((n, nh, HEAD, HEAD))],
        scratch_shapes=[pltpu.VMEM((nh, HEAD, HEAD), F32)],
        compiler_params=_cparams(dimension_semantics=("arbitrary",)),
    )(q, k, v, gates)


def _dn_chunks_bwd(q, k, v, gates, s_in, do):
    nh, t, _ = q.shape
    n = t // DN_CHUNK

    def kern(q_ref, k_ref, v_ref, g_ref, sin_ref, do_ref, dq_ref, dk_ref, dv_ref, dg_ref, ds_scr):
        @pl.when(pl.program_id(0) == 0)
        def _():
            ds_scr[...] = jnp.zeros(ds_scr.shape, F32)

        _, vjp = jax.vjp(_dn_chunk, q_ref[...], k_ref[...], v_ref[...], g_ref[...], sin_ref[0])
        dq_ref[...], dk_ref[...], dv_ref[...], dg_ref[...], ds_scr[...] = vjp((do_ref[...], ds_scr[...]))

    blk = pl.BlockSpec((nh, DN_CHUNK, HEAD), lambda i: (0, n - 1 - i, 0))
    gblk = pl.BlockSpec((DN_CHUNK, LANES), lambda i: (n - 1 - i, 0))
    hm = _sds((nh, t, HEAD))
    return pl.pallas_call(
        kern, name="dn_chunks_bwd", grid=(n,),
        in_specs=[blk, blk, blk, gblk, pl.BlockSpec((1, nh, HEAD, HEAD), lambda i: (n - 1 - i, 0, 0, 0)), blk],
        out_specs=[blk, blk, blk, gblk],
        out_shape=[hm, hm, hm, _sds((t, LANES))],
        scratch_shapes=[pltpu.VMEM((nh, HEAD, HEAD), F32)],
        compiler_params=_cparams(dimension_semantics=("arbitrary",)),
    )(q, k, v, gates, s_in, do)


def _dn_gate_out(o, z, g):
    return _rms(o, g) * _silu(z)


def _z_spec(tt):
    return (tt, N_HEADS * HEAD), lambda i: (i, DN_QKV // (N_HEADS * HEAD))


def _dn_post_fwd(o, proj, out_norm, tt=256):
    t = o.shape[1]
    tt = min(tt, t)

    def body(o_ref, z_ref, g_ref, y_ref):
        for h in range(N_HEADS):
            y_ref[:, h * HEAD:(h + 1) * HEAD] = _dn_gate_out(o_ref[h], _heads_in(z_ref, h), g_ref[...]).astype(BF16)

    return _rows("dn_post", body, [o, (proj,) + _z_spec(tt)], [_sds((t, N_HEADS * HEAD), BF16)], tt=tt, consts=[out_norm])[0]


def _dn_post_bwd(o, proj, out_norm, dy, tt=256):
    t = o.shape[1]
    tt = min(tt, t)

    def body(o_ref, z_ref, dy_ref, g_ref, do_ref, dz_ref, dg_ref):
        dg = jnp.zeros(g_ref.shape, F32)
        for h in range(N_HEADS):
            _, vjp = jax.vjp(_dn_gate_out, o_ref[h], _heads_in(z_ref, h), g_ref[...])
            a, b, d = vjp(_heads_in(dy_ref, h))
            do_ref[h] = a
            dz_ref[:, h * HEAD:(h + 1) * HEAD] = b
            dg = dg + d
        _acc(dg_ref, dg)

    return _rows("dn_post_bwd", body, [o, (proj,) + _z_spec(tt), dy], [_sds(o.shape), _sds((t, N_HEADS * HEAD))], tt=tt,
                 consts=[out_norm], accs=[_sds(out_norm.shape)])


def _dn_fwd(x, h, w_in, conv_w, a_log, dt_bias, out_norm, w_out):
    proj = _mm("dn_in", h, w_in)
    c = _dn_conv_fwd(proj, conv_w)
    q, k, v, gates = _dn_prep_fwd(c, proj, a_log, dt_bias)
    o, s_in = _dn_chunks_fwd(q, k, v, gates)
    on = _dn_post_fwd(o, proj, out_norm)
    y = _mm("dn_out", on, w_out, add=x)
    return y, (h, proj, c, q, k, v, gates, o, s_in, on)


def _dn_bwd(saved, dy, w_in, conv_w, a_log, dt_bias, out_norm, w_out):
    h, proj, c, q, k, v, gates, o, s_in, on = saved
    d_wout = _mm("dn_out_wgrad", on, dy, ta=True)
    don = _mm("dn_out_dgrad", dy, w_out, tb=True)
    do, dz, d_out_norm = _dn_post_bwd(o, proj, out_norm, don)
    dq, dk, dv, dgates = _dn_chunks_bwd(q, k, v, gates, s_in, do)
    dc, dab, d_a_log, d_dt_bias = _dn_prep_bwd(c, proj, a_log, dt_bias, dq, dk, dv, dgates)
    dproj, d_conv_w = _dn_conv_bwd(proj, conv_w, dc, dz, dab)
    d_win = _mm("dn_in_wgrad", h, dproj, ta=True)
    dh = _mm("dn_in_dgrad", dproj, w_in, tb=True)
    return dh, d_win, d_conv_w, d_a_log, d_dt_bias, d_out_norm, d_wout


MLA_SCALE = MLA_QK ** -0.5
MLA_C = 512


def _swap_raw(x):
    lane = lax.broadcasted_iota(jnp.int32, x.shape, 1)
    half = MLA_ROPE // 2
    y = jnp.where(lane < half, pltpu.roll(x, LANES - half, 1), pltpu.roll(x, half, 1))
    return jnp.where(lane < MLA_ROPE, y, 0.0)


@jax.custom_vjp
def _swap_halves(x):
    return _swap_raw(x)


_swap_halves.defvjp(lambda x: (_swap_raw(x), None), lambda _, d: (_swap_raw(d),))


def _rms_rope(x, g, cos, sin):
    y = x * lax.rsqrt(jnp.sum(x * x, axis=-1, keepdims=True) * (1.0 / MLA_ROPE) + NORM_EPS) * g
    return y * cos + _swap_halves(y) * sin


def _mla_latent(cq, ckv, kr, gq, gkv, gkr, cos, sin):
    return _rms(cq, gq), _rms(ckv, gkv), _rms_rope(kr, gkr, cos, sin)


def _mla_prep1_fwd(c, gq, gkv, gkr, cos, sin):
    t = c.shape[0]

    def body(c_ref, cos_ref, sin_ref, gq_ref, gkv_ref, gkr_ref, cq_ref, ckv_ref, kr_ref):
        a, b, r = _mla_latent(c_ref[:, :256], c_ref[:, 256:384], c_ref[:, 384:], gq_ref[...], gkv_ref[...], gkr_ref[...],
                              cos_ref[...], sin_ref[...])
        cq_ref[...] = a.astype(BF16)
        ckv_ref[...] = b.astype(BF16)
        kr_ref[...] = r.astype(BF16)

    return _rows("mla_prep1", body, [c, cos, sin], [_sds((t, 256), BF16), _sds((t, HEAD), BF16), _sds((t, HEAD), BF16)],
                 tt=512, consts=[gq, gkv, gkr])


def _mla_prep1_bwd(c, gq, gkv, gkr, cos, sin, dcq, dckv, dkr_heads):
    def body(c_ref, cos_ref, sin_ref, dcq_ref, dckv_ref, dkr_ref, gq_ref, gkv_ref, gkr_ref, dc_ref, dgq_ref, dgkv_ref, dgkr_ref):
        dkr = dkr_ref[0]
        for h in range(1, N_HEADS):
            dkr = dkr + dkr_ref[h]
        _, vjp = jax.vjp(_mla_latent, c_ref[:, :256], c_ref[:, 256:384], c_ref[:, 384:], gq_ref[...], gkv_ref[...], gkr_ref[...],
                         cos_ref[...], sin_ref[...])
        a, b, r, d1, d2, d3, _, _ = vjp((dcq_ref[...], dckv_ref[...], dkr))
        dc_ref[:, :256] = a.astype(BF16)
        dc_ref[:, 256:384] = b.astype(BF16)
        dc_ref[:, 384:] = r.astype(BF16)
        _acc(dgq_ref, d1)
        _acc(dgkv_ref, d2)
        _acc(dgkr_ref, d3)

    return _rows("mla_prep1_bwd", body, [c, cos, sin, dcq, dckv, dkr_heads], [_sds(c.shape, BF16)], tt=512,
                 consts=[gq, gkv, gkr], accs=[_sds(gq.shape), _sds(gkv.shape), _sds(gkr.shape)])


def _mla_heads(qn, qr, kn, gqn, gqr, gkn, cos, sin):
    return _rms(qn, gqn) * MLA_SCALE, _rms_rope(qr, gqr, cos, sin) * MLA_SCALE, _rms(kn, gkn)


def _mla_prep2_fwd(qa, kv, gqn, gqr, gkn, cos, sin):
    t = qa.shape[0]

    def body(qa_ref, kv_ref, cos_ref, sin_ref, gqn_ref, gqr_ref, gkn_ref, qn_ref, qr_ref, kn_ref, v_ref):
        for h in range(N_HEADS):
            a, b, c = _mla_heads(_heads_in(qa_ref, h), _heads_in(qa_ref, N_HEADS + h), _heads_in(kv_ref, h),
                                 gqn_ref[...], gqr_ref[...], gkn_ref[...], cos_ref[...], sin_ref[...])
            qn_ref[h] = a.astype(BF16)
            qr_ref[h] = b.astype(BF16)
            kn_ref[h] = c.astype(BF16)
            v_ref[h] = _heads_in(kv_ref, N_HEADS + h).astype(BF16)

    hm = _sds((N_HEADS, t, HEAD), BF16)
    return _rows("mla_prep2", body, [qa, kv, cos, sin], [hm, hm, hm, hm], tt=256, consts=[gqn, gqr, gkn])


def _mla_prep2_bwd(qa, kv, gqn, gqr, gkn, cos, sin, dqn, dqr, dkn, dv):
    def body(qa_ref, kv_ref, cos_ref, sin_ref, dqn_ref, dqr_ref, dkn_ref, dv_ref, gqn_ref, gqr_ref, gkn_ref,
             dqa_ref, dkv_ref, d1_ref, d2_ref, d3_ref):
        d1 = jnp.zeros(gqn_ref.shape, F32)
        d2 = jnp.zeros(gqr_ref.shape, F32)
        d3 = jnp.zeros(gkn_ref.shape, F32)
        for h in range(N_HEADS):
            _, vjp = jax.vjp(_mla_heads, _heads_in(qa_ref, h), _heads_in(qa_ref, N_HEADS + h), _heads_in(kv_ref, h),
                             gqn_ref[...], gqr_ref[...], gkn_ref[...], cos_ref[...], sin_ref[...])
            a, b, c, e1, e2, e3, _, _ = vjp((dqn_ref[h], dqr_ref[h], dkn_ref[h]))
            dqa_ref[:, h * HEAD:(h + 1) * HEAD] = a.astype(BF16)
            dqa_ref[:, (N_HEADS + h) * HEAD:(N_HEADS + h + 1) * HEAD] = b.astype(BF16)
            dkv_ref[:, h * HEAD:(h + 1) * HEAD] = c.astype(BF16)
            dkv_ref[:, (N_HEADS + h) * HEAD:(N_HEADS + h + 1) * HEAD] = dv_ref[h].astype(BF16)
            d1, d2, d3 = d1 + e1, d2 + e2, d3 + e3
        _acc(d1_ref, d1)
        _acc(d2_ref, d2)
        _acc(d3_ref, d3)

    return _rows("mla_prep2_bwd", body, [qa, kv, cos, sin, dqn, dqr, dkn, dv], [_sds(qa.shape, BF16), _sds(kv.shape, BF16)],
                 tt=256, consts=[gqn, gqr, gkn], accs=[_sds(gqn.shape), _sds(gqr.shape), _sds(gkn.shape)])


def _mla_attn_fwd(qn, qr, kn, kr, v):
    nh, t, _ = qn.shape
    bq = _q_block(t)
    per = bq // ATT_BLOCK

    def kern(qn_ref, qr_ref, kn_ref, kr_ref, v_ref, o_ref, lse_ref):
        qb = pl.program_id(1)
        qv = jnp.concatenate([qn_ref[0], qr_ref[0]], axis=1)
        order = _key_order(bq, qb)

        def body(kb, carry):
            acc, m, l = carry
            off = pl.multiple_of(kb * ATT_BLOCK, ATT_BLOCK)
            kv = jnp.concatenate([kn_ref[0, pl.ds(off, ATT_BLOCK), :], kr_ref[pl.ds(off, ATT_BLOCK), :]], axis=1)
            s = jnp.where(order >= kb * ATT_BLOCK, _dot_nt(qv, kv), -jnp.inf)
            m_new = jnp.maximum(m, jnp.max(s, axis=1, keepdims=True))
            alpha = jnp.exp(m - m_new)
            p = jnp.exp(s - m_new)
            acc = acc * alpha + _dot(p.astype(BF16), v_ref[0, pl.ds(off, ATT_BLOCK), :])
            return acc, m_new, l * alpha + jnp.sum(p, axis=1, keepdims=True)

        init = (jnp.zeros((bq, HEAD), F32), jnp.full((bq, 1), -jnp.inf, F32), jnp.zeros((bq, 1), F32))
        acc, m, l = lax.fori_loop(0, (qb + 1) * per, body, init)
        o_ref[...] = acc / l
        lse_ref[...] = jnp.broadcast_to(m + jnp.log(l), (bq, HEAD))

    blk = pl.BlockSpec((1, bq, HEAD), lambda h, i: (h, i, 0))
    full = pl.BlockSpec((1, t, HEAD), lambda h, i: (h, 0, 0))
    tok = pl.BlockSpec((bq, HEAD), lambda h, i: (i, h))
    return pl.pallas_call(
        kern, name="mla_attn_fwd", grid=(nh, t // bq),
        in_specs=[blk, blk, full, pl.BlockSpec((t, HEAD), lambda h, i: (0, 0)), full],
        out_specs=[tok, tok], out_shape=[_sds((t, nh * HEAD)), _sds((t, nh * HEAD))],
        compiler_params=_cparams(dimension_semantics=("parallel", "arbitrary")),
    )(qn, qr, kn, kr, v)


def _mla_attn_bwd(qn, qr, kn, kr, v, o, lse, do):
    nh, t, _ = qn.shape
    bq = _q_block(t)
    per = bq // ATT_BLOCK

    def kern(qn_ref, qr_ref, kn_ref, kr_ref, v_ref, o_ref, lse_ref, do_ref, dqn_ref, dqr_ref, dkn_ref, dkr_ref, dv_ref):
        qb = pl.program_id(1)

        @pl.when(qb == 0)
        def _():
            dkn_ref[...] = jnp.zeros(dkn_ref.shape, F32)
            dkr_ref[...] = jnp.zeros(dkr_ref.shape, F32)
            dv_ref[...] = jnp.zeros(dv_ref.shape, F32)

        qv = jnp.concatenate([qn_ref[0], qr_ref[0]], axis=1)
        dov = do_ref[...]
        dob = dov.astype(BF16)
        delta = jnp.sum(dov * o_ref[...], axis=1, keepdims=True)
        lse_col = lse_ref[:, :1]
        order = _key_order(bq, qb)

        def body(kb, dq):
            off = pl.multiple_of(kb * ATT_BLOCK, ATT_BLOCK)
            kv = jnp.concatenate([kn_ref[0, pl.ds(off, ATT_BLOCK), :], kr_ref[pl.ds(off, ATT_BLOCK), :]], axis=1)
            vv = v_ref[0, pl.ds(off, ATT_BLOCK), :]
            p = jnp.where(order >= kb * ATT_BLOCK, jnp.exp(_dot_nt(qv, kv) - lse_col), 0.0)
            ds = (p * (_dot_nt(dob, vv) - delta)).astype(BF16)
            dk = _dot_tn(ds, qv)
            dkn_ref[0, pl.ds(off, ATT_BLOCK), :] += dk[:, :HEAD]
            dkr_ref[0, pl.ds(off, ATT_BLOCK), :] += dk[:, HEAD:]
            dv_ref[0, pl.ds(off, ATT_BLOCK), :] += _dot_tn(p.astype(BF16), dob)
            return dq + _dot(ds, kv)

        dq = lax.fori_loop(0, (qb + 1) * per, body, jnp.zeros((bq, 2 * HEAD), F32))
        dqn_ref[0] = dq[:, :HEAD]
        dqr_ref[0] = dq[:, HEAD:]

    hm = _sds((nh, t, HEAD))
    blk = pl.BlockSpec((1, bq, HEAD), lambda h, i: (h, i, 0))
    full = pl.BlockSpec((1, t, HEAD), lambda h, i: (h, 0, 0))
    tok = pl.BlockSpec((bq, HEAD), lambda h, i: (i, h))
    return pl.pallas_call(
        kern, name="mla_attn_bwd", grid=(nh, t // bq),
        in_specs=[blk, blk, full, pl.BlockSpec((t, HEAD), lambda h, i: (0, 0)), full, tok, tok, tok],
        out_specs=[blk, blk, full, full, full], out_shape=[hm, hm, hm, hm, hm],
        compiler_params=_cparams(dimension_semantics=("parallel", "arbitrary")),
    )(qn, qr, kn, kr, v, o, lse, do)


def _rope_tables(t):
    inv_freq = ROPE_THETA ** (-jnp.arange(0, MLA_ROPE, 2, dtype=F32) / MLA_ROPE)
    ang = jnp.arange(t, dtype=F32)[:, None] * inv_freq[None, :]
    c, s = jnp.cos(ang), jnp.sin(ang)
    pad = ((0, 0), (0, LANES - MLA_ROPE))
    return jnp.pad(jnp.concatenate([c, c], axis=1), pad), jnp.pad(jnp.concatenate([-s, s], axis=1), pad)


def _pad_lanes(v, n=LANES):
    return jnp.pad(v, (0, n - v.shape[0])).reshape(1, n)


def _mla_layout(w_down, w_uq, w_ukv):
    w_down_p = jnp.pad(w_down, ((0, 0), (0, MLA_C - w_down.shape[1])))
    uq = w_uq.reshape(w_uq.shape[0], N_HEADS, MLA_QK)
    rope = jnp.pad(uq[:, :, HEAD:], ((0, 0), (0, 0), (0, LANES - MLA_ROPE)))
    w_uq_p = jnp.concatenate([uq[:, :, :HEAD].reshape(-1, N_HEADS * HEAD), rope.reshape(-1, N_HEADS * LANES)], axis=1)
    ukv = w_ukv.reshape(w_ukv.shape[0], N_HEADS, 2 * HEAD)
    w_ukv_p = jnp.concatenate([ukv[:, :, :HEAD].reshape(-1, N_HEADS * HEAD), ukv[:, :, HEAD:].reshape(-1, N_HEADS * HEAD)], axis=1)
    return w_down_p, w_uq_p, w_ukv_p


def _mla_unlayout(d_down_p, d_uq_p, d_ukv_p):
    d_down = d_down_p[:, :256 + HEAD + MLA_ROPE]
    nope = d_uq_p[:, :N_HEADS * HEAD].reshape(-1, N_HEADS, HEAD)
    rope = d_uq_p[:, N_HEADS * HEAD:].reshape(-1, N_HEADS, LANES)[:, :, :MLA_ROPE]
    d_uq = jnp.concatenate([nope, rope], axis=2).reshape(-1, N_HEADS * MLA_QK)
    kn = d_ukv_p[:, :N_HEADS * HEAD].reshape(-1, N_HEADS, HEAD)
    vv = d_ukv_p[:, N_HEADS * HEAD:].reshape(-1, N_HEADS, HEAD)
    d_ukv = jnp.concatenate([kn, vv], axis=2).reshape(-1, N_HEADS * 2 * HEAD)
    return d_down, d_uq, d_ukv


def _mla_weight_shapes():
    return (_sds((1024, MLA_C), BF16), _sds((1, 256)), _sds((1, HEAD)), _sds((256, 2048), BF16), _sds((HEAD, 2048), BF16),
            _sds((1, HEAD)), _sds((1, HEAD)), _sds((1, HEAD)), _sds((1, HEAD)), _sds((1024, 1024), BF16),
            _sds((4096, HEAD)), _sds((4096, HEAD)))


def _mla_fwd(x, h, w_down, gq, gkv, w_uq, w_ukv, gqn, gqr, gkn, gkr, w_out, cos, sin):
    c = _mm("mla_down", h, w_down)
    cq, ckv, kr = _mla_prep1_fwd(c, gq, gkv, gkr, cos, sin)
    qa = _mm("mla_uq", cq, w_uq)
    kv = _mm("mla_ukv", ckv, w_ukv)
    qn, qr, kn, v = _mla_prep2_fwd(qa, kv, gqn, gqr, gkn, cos, sin)
    o, lse = _mla_attn_fwd(qn, qr, kn, kr, v)
    y = _mm("mla_out", o, w_out, add=x)
    return y, (h, c, cq, ckv, kr, qa, kv, qn, qr, kn, v, o, lse)


def _mla_bwd(saved, dy, w_down, gq, gkv, w_uq, w_ukv, gqn, gqr, gkn, gkr, w_out, cos, sin):
    h, c, cq, ckv, kr, qa, kv, qn, qr, kn, v, o, lse = saved
    d_wout = _mm("mla_out_wgrad", o, dy, ta=True)
    do = _mm("mla_out_dgrad", dy, w_out, tb=True)
    dqn, dqr, dkn, dkr, dv = _mla_attn_bwd(qn, qr, kn, kr, v, o, lse, do)
    dqa, dkv, dgqn, dgqr, dgkn = _mla_prep2_bwd(qa, kv, gqn, gqr, gkn, cos, sin, dqn, dqr, dkn, dv)
    d_wuq = _mm("mla_uq_wgrad", cq, dqa, ta=True)
    d_wukv = _mm("mla_ukv_wgrad", ckv, dkv, ta=True)
    dcq = _mm("mla_uq_dgrad", dqa, w_uq, tb=True)
    dckv = _mm("mla_ukv_dgrad", dkv, w_ukv, tb=True)
    dc, dgq, dgkv, dgkr = _mla_prep1_bwd(c, gq, gkv, gkr, cos, sin, dcq, dckv, dkr)
    d_wdown = _mm("mla_down_wgrad", h, dc, ta=True)
    dh = _mm("mla_down_dgrad", dc, w_down, tb=True)
    return dh, d_wdown, dgq, dgkv, d_wuq, d_wukv, dgqn, dgqr, dgkn, dgkr, d_wout


def _loss_head(y, target):
    d = y.shape[1]

    def body(y_ref, t_ref, dy_ref, l_ref):
        err = y_ref[...] - t_ref[...]
        dy_ref[...] = err * (1.0 / d)
        part = 0.5 * jnp.sum(jnp.sum(err * err, axis=1, keepdims=True) * (1.0 / d), axis=0, keepdims=True)
        _acc(l_ref, jnp.broadcast_to(part, (1, LANES)))

    return _rows("loss_head", body, [y, target], [_sds(y.shape)], tt=512, accs=[_sds((1, LANES))])


MESH_ID = pl.DeviceIdType.MESH
HBM_SPEC = pl.BlockSpec(memory_space=pltpu.HBM)


def _all_gather(name, x):
    m_per, n = x.shape

    def body(x_ref, out_ref, send_sems, recv_sems, local_sem):
        x, y, c = lax.axis_index("x"), lax.axis_index("y"), lax.axis_index("c")
        me, sibling = (x, y, c), (x, y, 1 - c)
        chips = [(1 - x, y), (x, 1 - y), (1 - x, 1 - y)]

        def rows(px, py, pc):
            return out_ref.at[pl.ds((4 * px + 2 * py + pc) * m_per, m_per), :]

        def copy(k, block, to, src=None):
            return pltpu.make_async_remote_copy(
                src_ref=rows(*block) if src is None else src, dst_ref=rows(*block),
                send_sem=send_sems.at[k], recv_sem=recv_sems.at[k], device_id=to, device_id_type=MESH_ID)

        mine = pltpu.make_async_copy(x_ref, rows(*me), local_sem)
        mine.start()
        first = [copy(0, me, sibling, src=x_ref)]
        first += [copy(1 + j, me, (*chip, c), src=x_ref) for j, chip in enumerate(chips)]
        for cp in first:
            cp.start()
        passed = [copy(4 + j, (*chip, c), sibling) for j, chip in enumerate(chips)]
        for j, chip in enumerate(chips):
            copy(1 + j, (*chip, c), me).wait_recv()
            passed[j].start()
        copy(0, sibling, me).wait_recv()
        for j, chip in enumerate(chips):
            copy(4 + j, (*chip, 1 - c), me).wait_recv()
        for cp in first + passed:
            cp.wait_send()
        mine.wait()

    return pl.pallas_call(
        body, name=name,
        out_shape=jax.ShapeDtypeStruct((N_DEV * m_per, n), x.dtype),
        in_specs=[HBM_SPEC], out_specs=HBM_SPEC,
        scratch_shapes=[pltpu.SemaphoreType.DMA((7,)), pltpu.SemaphoreType.DMA((7,)), pltpu.SemaphoreType.DMA],
    )(x)


def _all_gather_groups(name, xs):
    ng = len(xs)

    def body(*refs):
        x_refs, out_refs, token = refs[:ng], refs[ng:2 * ng], refs[2 * ng]
        send_sems, recv_sems, local_sems = refs[2 * ng + 1:]
        token[...] = jnp.zeros(token.shape, F32)
        x, y, c = lax.axis_index("x"), lax.axis_index("y"), lax.axis_index("c")
        me, sibling = (x, y, c), (x, y, 1 - c)
        chips = [(1 - x, y), (x, 1 - y), (1 - x, 1 - y)]

        def copy(g, k, block, to, src=None):
            px, py, pc = block
            dst = out_refs[g].at[4 * px + 2 * py + pc]
            return pltpu.make_async_remote_copy(
                src_ref=dst if src is None else src, dst_ref=dst,
                send_sem=send_sems.at[g, k], recv_sem=recv_sems.at[g, k], device_id=to, device_id_type=MESH_ID)

        mine = [pltpu.make_async_copy(x_refs[g], out_refs[g].at[4 * x + 2 * y + c], local_sems.at[g]) for g in range(ng)]
        for cp in mine:
            cp.start()
        first = []
        for g in range(ng):
            first.append(copy(g, 0, me, sibling, src=x_refs[g]))
            first += [copy(g, 1 + j, me, (*chip, c), src=x_refs[g]) for j, chip in enumerate(chips)]
        for cp in first:
            cp.start()
        passed = []
        for j, chip in enumerate(chips):
            for g in range(ng):
                copy(g, 1 + j, (*chip, c), me).wait_recv()
                passed.append(copy(g, 4 + j, (*chip, c), sibling))
                passed[-1].start()
        for g in range(ng):
            copy(g, 0, sibling, me).wait_recv()
            for j, chip in enumerate(chips):
                copy(g, 4 + j, (*chip, 1 - c), me).wait_recv()
        for cp in first + passed:
            cp.wait_send()
        for cp in mine:
            cp.wait()

    return pl.pallas_call(
        body, name=name,
        out_shape=[jax.ShapeDtypeStruct((N_DEV,) + x.shape, x.dtype) for x in xs] + [_sds((8, LANES))],
        in_specs=[HBM_SPEC] * ng, out_specs=[HBM_SPEC] * ng + [pl.BlockSpec(memory_space=pltpu.VMEM)],
        scratch_shapes=[pltpu.SemaphoreType.DMA((ng, 7)), pltpu.SemaphoreType.DMA((ng, 7)), pltpu.SemaphoreType.DMA((ng,))],
    )(*xs)


def _rs_cores(gs):
    ng = len(gs)

    def body(*refs):
        g_refs, recv_refs = refs[:ng], refs[ng:2 * ng]
        send_sems, recv_sems = refs[2 * ng:]
        x, y, c = lax.axis_index("x"), lax.axis_index("y"), lax.axis_index("c")
        copies = [pltpu.make_async_remote_copy(
            src_ref=g_refs[g].at[:, 1 - c], dst_ref=recv_refs[g], send_sem=send_sems.at[g], recv_sem=recv_sems.at[g],
            device_id=(x, y, 1 - c), device_id_type=MESH_ID) for g in range(ng)]
        for cp in copies:
            cp.start()
        for cp in copies:
            cp.wait()

    return pl.pallas_call(
        body, name="rs_cores",
        out_shape=[jax.ShapeDtypeStruct((g.shape[0],) + g.shape[2:], g.dtype) for g in gs],
        in_specs=[HBM_SPEC] * ng, out_specs=[HBM_SPEC] * ng,
        scratch_shapes=[pltpu.SemaphoreType.DMA((ng,)), pltpu.SemaphoreType.DMA((ng,))],
    )(*gs)


def _rs_add(g, recv, core):
    nchip, _, r, c_ = g.shape
    tr = _pick(r, (512, 384, 256, 128))

    def body(core_ref, g_ref, r_ref, p_ref, pb_ref):
        s = g_ref[0, 0] + r_ref[0]
        p_ref[0] = s
        pb_ref[0] = s.astype(BF16)

    return pl.pallas_call(
        body, name="rs_add",
        grid_spec=pltpu.PrefetchScalarGridSpec(
            num_scalar_prefetch=1, grid=(nchip, r // tr),
            in_specs=[pl.BlockSpec((1, 1, tr, c_), lambda k, i, core_ref: (k, core_ref[0], i, 0)),
                      pl.BlockSpec((1, tr, c_), lambda k, i, core_ref: (k, i, 0))],
            out_specs=[pl.BlockSpec((1, tr, c_), lambda k, i, core_ref: (k, i, 0)),
                       pl.BlockSpec((1, tr, c_), lambda k, i, core_ref: (k, i, 0))]),
        out_shape=[_sds((nchip, r, c_)), _sds((nchip, r, c_), BF16)],
        compiler_params=_cparams(dimension_semantics=("arbitrary", "arbitrary")),
    )(core, g, recv)


def _rs_chips(pbs):
    ng = len(pbs)

    def body(*refs):
        p_refs, recv_refs = refs[:ng], refs[ng:2 * ng]
        send_sems, recv_sems, local_sems = refs[2 * ng:]
        x, y, c = lax.axis_index("x"), lax.axis_index("y"), lax.axis_index("c")
        my = 2 * x + y
        chips = [(1 - x, y), (x, 1 - y), (1 - x, 1 - y)]
        local = [pltpu.make_async_copy(p_refs[g].at[my], recv_refs[g].at[my], local_sems.at[g]) for g in range(ng)]
        for cp in local:
            cp.start()

        def copy(g, k, src_chip, dst_chip, to):
            return pltpu.make_async_remote_copy(
                src_ref=p_refs[g].at[src_chip], dst_ref=recv_refs[g].at[dst_chip],
                send_sem=send_sems.at[g, k], recv_sem=recv_sems.at[g, k], device_id=to, device_id_type=MESH_ID)

        sends = [copy(g, k, 2 * px + py, my, (px, py, c)) for g in range(ng) for k, (px, py) in enumerate(chips)]
        for cp in sends:
            cp.start()
        for g in range(ng):
            for k, (px, py) in enumerate(chips):
                copy(g, k, my, 2 * px + py, (px, py, c)).wait_recv()
        for cp in sends:
            cp.wait_send()
        for cp in local:
            cp.wait()

    return pl.pallas_call(
        body, name="rs_chips", out_shape=[jax.ShapeDtypeStruct(p.shape, p.dtype) for p in pbs],
        in_specs=[HBM_SPEC] * ng, out_specs=[HBM_SPEC] * ng,
        scratch_shapes=[pltpu.SemaphoreType.DMA((ng, 3)), pltpu.SemaphoreType.DMA((ng, 3)), pltpu.SemaphoreType.DMA((ng,))],
    )(*pbs)


EFFECT = pltpu.SideEffectType.DATAFLOW_SIDE_EFFECTING
SEM_SPEC = pl.BlockSpec(memory_space=pltpu.SEMAPHORE)


def _push_copies(src_refs, land_refs, send_sems, recv_sems, chunked):
    x, y, c = lax.axis_index("x"), lax.axis_index("y"), lax.axis_index("c")
    me = 4 * x + 2 * y + c
    copies = []
    for g, (src, land) in enumerate(zip(src_refs, land_refs)):
        for k in range(1, N_DEV):
            px = 1 - x if k & 4 else x
            py = 1 - y if k & 2 else y
            pc = 1 - c if k & 1 else c
            copies.append(pltpu.make_async_remote_copy(
                src_ref=src.at[4 * px + 2 * py + pc] if chunked else src, dst_ref=land.at[me],
                send_sem=send_sems.at[g * (N_DEV - 1) + k - 1], recv_sem=recv_sems.at[g * (N_DEV - 1) + k - 1],
                device_id=(px, py, pc), device_id_type=MESH_ID))
    return copies


def _hbm(x):
    return pltpu.with_memory_space_constraint(x, pltpu.HBM)


def _push_start(name, srcs, lands, chunked):
    ng = len(srcs)

    def body(*refs):
        for cp in _push_copies(refs[:ng], refs[ng:2 * ng], refs[2 * ng], refs[2 * ng + 1], chunked):
            cp.start()
        refs[-1][...] = jnp.zeros(refs[-1].shape, F32)

    bufs = list(srcs) + list(lands)
    outs = pl.pallas_call(
        body, name=name,
        out_shape=(pltpu.SemaphoreType.DMA((ng * (N_DEV - 1),)), pltpu.SemaphoreType.DMA((ng * (N_DEV - 1),)),
                   *[pltpu.HBM(b.shape, b.dtype) for b in bufs], jax.ShapeDtypeStruct((8, LANES), F32)),
        in_specs=[HBM_SPEC] * (2 * ng),
        out_specs=(SEM_SPEC, SEM_SPEC, *[HBM_SPEC] * (2 * ng), pl.BlockSpec(memory_space=pltpu.VMEM)),
        input_output_aliases={i: 2 + i for i in range(2 * ng)},
        compiler_params=pltpu.CompilerParams(has_side_effects=EFFECT),
    )(*[_hbm(b) for b in bufs])
    return outs[0], outs[1], list(outs[2:2 + ng]), list(outs[2 + ng:2 + 2 * ng]), outs[-1]


def _push_wait(name, started, after, chunked):
    send_sems, recv_sems, srcs, lands, _ = started
    ng = len(srcs)

    def body(*refs):
        copies = _push_copies(refs[:ng], refs[ng:2 * ng], refs[2 * ng], refs[2 * ng + 1], chunked)
        for cp in copies:
            cp.wait_send()
        for cp in copies:
            cp.wait_recv()

    bufs = srcs + lands
    outs = pl.pallas_call(
        body, name=name,
        out_shape=tuple(pltpu.HBM(b.shape, b.dtype) for b in bufs),
        in_specs=[HBM_SPEC] * (2 * ng) + [SEM_SPEC, SEM_SPEC, pl.BlockSpec(memory_space=pl.ANY)],
        out_specs=tuple([HBM_SPEC] * (2 * ng)),
        input_output_aliases={i: i for i in range(2 * ng)},
        compiler_params=pltpu.CompilerParams(has_side_effects=EFFECT),
    )(*bufs, send_sems, recv_sems, after)
    return list(outs[ng:])


def _sum_adam_devices(own, recv, dev, w, m, v):
    ndev, r, c_ = recv.shape
    tr = _pick(r, (256, 128, 96, 32))

    def body(dev_ref, own_ref, r_ref, w_ref, m_ref, v_ref, g_ref, d_ref, mo_ref, vo_ref):
        me = dev_ref[0]
        g = jnp.where(me == 0, own_ref[...], r_ref[0].astype(F32))
        for j in range(1, ndev):
            g = g + jnp.where(me == j, own_ref[...], r_ref[j].astype(F32))
        g_ref[...] = g
        d_ref[...], mo_ref[...], vo_ref[...] = _adam(w_ref[...], g, m_ref[...], v_ref[...])

    row = pl.BlockSpec((tr, c_), lambda i, dev_ref: (i, 0))
    return pl.pallas_call(
        body, name="sum_adam",
        grid_spec=pltpu.PrefetchScalarGridSpec(
            num_scalar_prefetch=1, grid=(r // tr,),
            in_specs=[row, pl.BlockSpec((ndev, tr, c_), lambda i, dev_ref: (0, i, 0)), row, row, row],
            out_specs=[row, row, row, row]),
        out_shape=[_sds((r, c_))] * 4,
        compiler_params=_cparams(dimension_semantics=("arbitrary",)),
    )(dev, own, recv, w, m, v)


def _cols_from_shards(w, width):
    ns, r, cs = w.shape
    tr = _pick(r, (256, 128))

    def body(w_ref, o_ref):
        parts = [w_ref[j] for j in range(ns)]
        if width > ns * cs:
            parts.append(jnp.zeros((tr, width - ns * cs), w.dtype))
        o_ref[...] = jnp.concatenate(parts, axis=1)

    return pl.pallas_call(
        body, name="cols_from_shards", grid=(r // tr,),
        in_specs=[pl.BlockSpec((ns, tr, cs), lambda i: (0, i, 0))], out_specs=pl.BlockSpec((tr, width), lambda i: (i, 0)),
        out_shape=jax.ShapeDtypeStruct((r, width), w.dtype), compiler_params=_cparams(dimension_semantics=("arbitrary",)),
    )(w)


def _shards_from_cols(g, cs):
    r, width = g.shape
    tr = _pick(r, (256, 128))

    def body(g_ref, o_ref):
        for j in range(N_DEV):
            o_ref[j] = g_ref[:, j * cs:(j + 1) * cs]

    return pl.pallas_call(
        body, name="shards_from_cols", grid=(r // tr,),
        in_specs=[pl.BlockSpec((tr, width), lambda i: (i, 0))], out_specs=pl.BlockSpec((N_DEV, tr, cs), lambda i: (0, i, 0)),
        out_shape=jax.ShapeDtypeStruct((N_DEV, r, cs), g.dtype), compiler_params=_cparams(dimension_semantics=("arbitrary",)),
    )(g)


def _adam(w, g, m, v):
    m = ADAM_B1 * m + (1.0 - ADAM_B1) * g
    v = ADAM_B2 * v + (1.0 - ADAM_B2) * (g * g)
    m_hat = m / (1.0 - ADAM_B1 ** ADAM_STEP)
    v_hat = v / (1.0 - ADAM_B2 ** ADAM_STEP)
    return -ADAM_LR * (m_hat / (jnp.sqrt(v_hat) + ADAM_EPS) + ADAM_WD * w), m, v


def _sum_adam(p, recv, chip, w, m, v):
    nchip, r, c_ = recv.shape
    tr = _pick(r, (512, 384, 256, 128))

    def body(chip_ref, p_ref, r_ref, w_ref, m_ref, v_ref, g_ref, d_ref, mo_ref, vo_ref):
        my = chip_ref[0]
        g = jnp.where(my == 0, p_ref[0], r_ref[0].astype(F32))
        for k in range(1, nchip):
            g = g + jnp.where(my == k, p_ref[0], r_ref[k].astype(F32))
        g_ref[...] = g
        d_ref[...], mo_ref[...], vo_ref[...] = _adam(w_ref[...], g, m_ref[...], v_ref[...])

    row = pl.BlockSpec((tr, c_), lambda i, chip_ref: (i, 0))
    return pl.pallas_call(
        body, name="sum_adam",
        grid_spec=pltpu.PrefetchScalarGridSpec(
            num_scalar_prefetch=1, grid=(r // tr,),
            in_specs=[pl.BlockSpec((1, tr, c_), lambda i, chip_ref: (chip_ref[0], i, 0)),
                      pl.BlockSpec((nchip, tr, c_), lambda i, chip_ref: (0, i, 0)), row, row, row],
            out_specs=[row, row, row, row]),
        out_shape=[_sds((r, c_))] * 4,
        compiler_params=_cparams(dimension_semantics=("arbitrary",)),
    )(chip, p, recv, w, m, v)


def _sum_devices(gathered):
    m_all, n = gathered.shape
    m_per = m_all // N_DEV

    def body(x_ref, o_ref):
        s = x_ref[0:m_per, :]
        for j in range(1, N_DEV):
            s = s + x_ref[j * m_per:(j + 1) * m_per, :]
        o_ref[...] = s

    return pl.pallas_call(body, name="sum_devices", out_shape=_sds((m_per, n)), compiler_params=_cparams())(gathered)


def _adam_small(w, g, m, v):
    def body(w_ref, g_ref, m_ref, v_ref, d_ref, mo_ref, vo_ref):
        d_ref[...], mo_ref[...], vo_ref[...] = _adam(w_ref[...], g_ref[...], m_ref[...], v_ref[...])

    return pl.pallas_call(body, name="adam_small", out_shape=[_sds(w.shape)] * 3, compiler_params=_cparams())(w, g, m, v)


N_LAYERS = 4
_MIXER = ("dn", "sb", "mla")
_MIXER_PARAMS = {
    "dn": ("dn_w_in", "dn_conv_w", "dn_a_log", "dn_dt_bias", "dn_out_norm", "dn_w_out"),
    "sb": ("sb_w_qkv", "sb_q_norm", "sb_k_norm", "sb_w_out"),
    "mla": ("mla_w_down", "mla_q_a_norm", "mla_kv_a_norm", "mla_w_uq", "mla_w_ukv", "mla_q_nope_norm", "mla_q_rope_norm",
            "mla_k_nope_norm", "mla_k_rope_norm", "mla_w_out"),
}
_BIG_AXIS = {"dn_w_in": 1, "dn_w_out": 0, "sb_w_qkv": 1, "sb_w_out": 0, "mla_w_down": 0, "mla_w_uq": 1, "mla_w_ukv": 1,
             "mla_w_out": 0, "ffn_w_gate_up": 1, "ffn_w_down": 0}


def _weight_names():
    names = []
    for i in range(N_LAYERS):
        p = "l%d_" % i
        names += [p + "mix_norm"] + [p + n for n in _MIXER_PARAMS[_MIXER[i % 3]]] + [p + "ffn_norm", p + "ffn_w_gate_up", p + "ffn_w_down"]
    return names


WEIGHTS = _weight_names()
BIG = [n for n in WEIGHTS if n[3:] in _BIG_AXIS]
SMALL = [n for n in WEIGHTS if n[3:] not in _BIG_AXIS]
CONV = [n for n in SMALL if n.endswith("conv_w")]


def _ceil_to(n, k):
    return -(-n // k) * k


def _pack(arrs, cols, row_mult):
    parts = []
    for a in arrs:
        f = a.reshape(-1)
        parts.append(jnp.pad(f, (0, _ceil_to(f.shape[0], cols) - f.shape[0])))
    flat = jnp.concatenate(parts)
    rows = _ceil_to(flat.shape[0] // cols, row_mult)
    return jnp.pad(flat, (0, rows * cols - flat.shape[0])).reshape(rows, cols)


def _unpack(buf, shapes):
    cols = buf.shape[-1]
    out, r0 = [], 0
    for s in shapes:
        n = math.prod(s)
        nr = _ceil_to(n, cols) // cols
        out.append(buf[r0:r0 + nr].reshape(-1)[:n].reshape(s))
        r0 += nr
    return out


def _layer_groups(i):
    by = {"gu": (704, []), "down": (1024, []), "out": (1024, []), "dn_in": (514, []), "sb_qkv": (384, []), "mla": (512, [])}
    key = {"ffn_w_gate_up": "gu", "ffn_w_down": "down", "dn_w_in": "dn_in", "sb_w_qkv": "sb_qkv", "mla_w_down": "mla",
           "mla_w_uq": "mla", "mla_w_ukv": "mla"}
    for n in BIG:
        if n.startswith("l%d_" % i):
            by[key.get(n[3:], "out")][1].append(n)
    return [g for g in by.values() if g[1]]


LAYER_GROUPS = [_layer_groups(i) for i in range(N_LAYERS)]
N_FFN_GROUPS = 2


def _stack_group(grp, get):
    width, names = grp
    return jnp.concatenate([jnp.pad(get(n), ((0, 0), (0, width - get(n).shape[1]))) for n in names], axis=0)


def _unstack_group(grp, buf, shape_of):
    out, r0 = [], 0
    for n in grp[1]:
        rs, cs = shape_of(n)
        out.append(buf[..., r0:r0 + rs, :cs])
        r0 += rs
    return out


def kernel(x, l0_mix_norm, l0_dn_w_in, l0_dn_conv_w, l0_dn_a_log, l0_dn_dt_bias, l0_dn_out_norm, l0_dn_w_out, l0_ffn_norm, l0_ffn_w_gate_up, l0_ffn_w_down, l1_mix_norm, l1_sb_w_qkv, l1_sb_q_norm, l1_sb_k_norm, l1_sb_w_out, l1_ffn_norm, l1_ffn_w_gate_up, l1_ffn_w_down, l2_mix_norm, l2_mla_w_down, l2_mla_q_a_norm, l2_mla_kv_a_norm, l2_mla_w_uq, l2_mla_w_ukv, l2_mla_q_nope_norm, l2_mla_q_rope_norm, l2_mla_k_nope_norm, l2_mla_k_rope_norm, l2_mla_w_out, l2_ffn_norm, l2_ffn_w_gate_up, l2_ffn_w_down, l3_mix_norm, l3_dn_w_in, l3_dn_conv_w, l3_dn_a_log, l3_dn_dt_bias, l3_dn_out_norm, l3_dn_w_out, l3_ffn_norm, l3_ffn_w_gate_up, l3_ffn_w_down, loss_target, m_l0_mix_norm, m_l0_dn_w_in, m_l0_dn_conv_w, m_l0_dn_a_log, m_l0_dn_dt_bias, m_l0_dn_out_norm, m_l0_dn_w_out, m_l0_ffn_norm, m_l0_ffn_w_gate_up, m_l0_ffn_w_down, m_l1_mix_norm, m_l1_sb_w_qkv, m_l1_sb_q_norm, m_l1_sb_k_norm, m_l1_sb_w_out, m_l1_ffn_norm, m_l1_ffn_w_gate_up, m_l1_ffn_w_down, m_l2_mix_norm, m_l2_mla_w_down, m_l2_mla_q_a_norm, m_l2_mla_kv_a_norm, m_l2_mla_w_uq, m_l2_mla_w_ukv, m_l2_mla_q_nope_norm, m_l2_mla_q_rope_norm, m_l2_mla_k_nope_norm, m_l2_mla_k_rope_norm, m_l2_mla_w_out, m_l2_ffn_norm, m_l2_ffn_w_gate_up, m_l2_ffn_w_down, m_l3_mix_norm, m_l3_dn_w_in, m_l3_dn_conv_w, m_l3_dn_a_log, m_l3_dn_dt_bias, m_l3_dn_out_norm, m_l3_dn_w_out, m_l3_ffn_norm, m_l3_ffn_w_gate_up, m_l3_ffn_w_down, v_l0_mix_norm, v_l0_dn_w_in, v_l0_dn_conv_w, v_l0_dn_a_log, v_l0_dn_dt_bias, v_l0_dn_out_norm, v_l0_dn_w_out, v_l0_ffn_norm, v_l0_ffn_w_gate_up, v_l0_ffn_w_down, v_l1_mix_norm, v_l1_sb_w_qkv, v_l1_sb_q_norm, v_l1_sb_k_norm, v_l1_sb_w_out, v_l1_ffn_norm, v_l1_ffn_w_gate_up, v_l1_ffn_w_down, v_l2_mix_norm, v_l2_mla_w_down, v_l2_mla_q_a_norm, v_l2_mla_kv_a_norm, v_l2_mla_w_uq, v_l2_mla_w_ukv, v_l2_mla_q_nope_norm, v_l2_mla_q_rope_norm, v_l2_mla_k_nope_norm, v_l2_mla_k_rope_norm, v_l2_mla_w_out, v_l2_ffn_norm, v_l2_ffn_w_gate_up, v_l2_ffn_w_down, v_l3_mix_norm, v_l3_dn_w_in, v_l3_dn_conv_w, v_l3_dn_a_log, v_l3_dn_dt_bias, v_l3_dn_out_norm, v_l3_dn_w_out, v_l3_ffn_norm, v_l3_ffn_w_gate_up, v_l3_ffn_w_down):
    a = dict(locals())
    return _train_step(a)


def _train_step(a):
    mx, my, mc = lax.axis_index("x"), lax.axis_index("y"), lax.axis_index("c")
    dev = 4 * mx + 2 * my + mc
    dev_arr = jnp.reshape(dev, (1,)).astype(jnp.int32)
    t, d = a["x"].shape[1], a["x"].shape[2]
    xs = a["x"].reshape(t, d)
    target = a["loss_target"].reshape(t, d)

    full = {}

    def unpack_layer(i, bufs):
        for grp, buf in zip(LAYER_GROUPS[i], bufs):
            for n, shards in zip(grp[1], _unstack_group(grp, buf, lambda n: a[n].shape)):
                kind = n[3:]
                if kind == "ffn_w_gate_up":
                    full[n] = shards
                elif _BIG_AXIS[kind] == 0:
                    full[n] = shards.reshape(N_DEV * shards.shape[1], shards.shape[2])
                else:
                    width = DN_PROJ if kind == "dn_w_in" else N_DEV * shards.shape[2]
                    full[n] = _cols_from_shards(shards, width)

    def local_shards(i):
        return [_stack_group(grp, lambda n: a[n].astype(BF16)) for grp in LAYER_GROUPS[i]]

    *first_layer, gathered = _all_gather_groups("gather_weights", local_shards(0))
    unpack_layer(0, first_layer)
    after_first = gathered[0, 0].astype(BF16)
    gathers, started = {}, jnp.zeros((), F32)
    for i in range(1, N_LAYERS):
        srcs = [s + after_first for s in local_shards(i)]
        lands = [lax.dynamic_update_index_in_dim(lax.empty((N_DEV,) + s.shape, s.dtype), s, dev, 0) for s in srcs]
        gathers[i] = _push_start("gather_start_l%d" % i, srcs, lands, False)
        started = started + gathers[i][-1][0, 0]
    conv_pack = _pack([a[n] for n in CONV], LANES, 8)
    conv_all = _all_gather("gather_conv", conv_pack).reshape(N_DEV, conv_pack.shape[0], LANES)
    for n, parts in zip(CONV, zip(*[_unpack(conv_all[j], [a[n].shape for n in CONV]) for j in range(N_DEV)])):
        full[n] = jnp.concatenate(parts, axis=1)

    def vec(n):
        return a[n].reshape(1, -1)

    cos, sin = _rope_tables(t)

    def mixer_args(i):
        p = "l%d_" % i
        kind = _MIXER[i % 3]
        if kind == "dn":
            args = (full[p + "dn_w_in"], full[p + "dn_conv_w"], _pad_lanes(a[p + "dn_a_log"]), _pad_lanes(a[p + "dn_dt_bias"]),
                    vec(p + "dn_out_norm"), full[p + "dn_w_out"])
        elif kind == "sb":
            args = (full[p + "sb_w_qkv"], vec(p + "sb_q_norm"), vec(p + "sb_k_norm"), full[p + "sb_w_out"])
        else:
            w_down, w_uq, w_ukv = _mla_layout(full[p + "mla_w_down"], full[p + "mla_w_uq"], full[p + "mla_w_ukv"])
            args = (w_down, vec(p + "mla_q_a_norm"), vec(p + "mla_kv_a_norm"), w_uq, w_ukv, vec(p + "mla_q_nope_norm"),
                    _pad_lanes(a[p + "mla_q_rope_norm"]), vec(p + "mla_k_nope_norm"), _pad_lanes(a[p + "mla_k_rope_norm"]),
                    full[p + "mla_w_out"], cos, sin)
        return kind, args

    fwd = {"dn": _dn_fwd, "sb": _sb_fwd, "mla": _mla_fwd}
    bwd = {"dn": _dn_bwd, "sb": _sb_bwd, "mla": _mla_bwd}
    saved, layer_args = [], []
    for i in range(N_LAYERS):
        p = "l%d_" % i
        if i > 0:
            unpack_layer(i, _push_wait("gather_wait_l%d" % i, gathers[i], xs, False))
        kind, args = mixer_args(i)
        layer_args.append((kind, args))
        gain = vec(p + "mix_norm") + started if i == 0 else vec(p + "mix_norm")
        h = _rmsnorm_fwd("mix_norm", xs, gain)
        x_mid, sv_mix = fwd[kind](xs, h, *args)
        x_out, sv_ffn = _ffn_fwd(x_mid, vec(p + "ffn_norm"), full[p + "ffn_w_gate_up"], full[p + "ffn_w_down"])
        saved.append((xs, sv_mix, sv_ffn))
        xs = x_out
    dy, loss_part = _loss_head(xs, target)

    grads, big_out = {}, {}

    def grad_shards(n):
        g, (rs, cs) = grads[n], a[n].shape
        if g.ndim == 3:
            return g
        if _BIG_AXIS[n[3:]] == 0:
            return g.reshape(N_DEV, rs, cs)
        return _shards_from_cols(g, cs)

    def push_grads(tag, groups):
        owns, sends = [], []
        for grp in groups:
            parts = [jnp.pad(grad_shards(n), ((0, 0), (0, 0), (0, grp[0] - a[n].shape[1]))) for n in grp[1]]
            g_all = jnp.concatenate(parts, axis=1)
            owns.append(lax.dynamic_index_in_dim(g_all, dev, 0, keepdims=False))
            sends.append(g_all.astype(BF16))
        lands = [lax.empty(s.shape, s.dtype) for s in sends]
        return tag, groups, owns, _push_start("grads_start_" + tag, sends, lands, True)

    def finish_grads(push, after):
        tag, groups, owns, pushed = push
        recvs = _push_wait("grads_wait_" + tag, pushed, after, True)
        for grp, own, recv in zip(groups, owns, recvs):
            packs = [_stack_group(grp, lambda n, pre=pre: a[pre + n]) for pre in ("", "m_", "v_")]
            outs = [_unstack_group(grp, o, lambda n: a[n].shape) for o in _sum_adam_devices(own, recv, dev_arr, *packs)]
            for j, n in enumerate(grp[1]):
                big_out[n] = [o[j] for o in outs]

    mixer_push = None
    for i in reversed(range(N_LAYERS)):
        p = "l%d_" % i
        kind, args = layer_args[i]
        x_in, sv_mix, sv_ffn = saved[i]
        gain = vec(p + "ffn_norm") if mixer_push is None else vec(p + "ffn_norm") + mixer_push[3][-1][0, 0]
        dx_mid, grads[p + "ffn_norm"], grads[p + "ffn_w_gate_up"], grads[p + "ffn_w_down"] = _ffn_bwd(
            sv_ffn, dy, gain, full[p + "ffn_w_gate_up"], full[p + "ffn_w_down"])
        ffn_push = push_grads("l%d_ffn" % i, LAYER_GROUPS[i][:N_FFN_GROUPS])
        if mixer_push is not None:
            finish_grads(mixer_push, dx_mid)
        res = bwd[kind](sv_mix, dx_mid, *args)
        dh = res[0]
        if kind == "mla":
            res = list(res)
            res[1], res[4], res[5] = _mla_unlayout(res[1], res[4], res[5])
        for n, g in zip(_MIXER_PARAMS[kind], res[1:]):
            grads[p + n] = g
        dy, grads[p + "mix_norm"] = _rmsnorm_bwd("mix_norm_bwd", x_in, vec(p + "mix_norm") + ffn_push[3][-1][0, 0], dh, dx_mid)
        mixer_push = push_grads("l%d_mix" % i, LAYER_GROUPS[i][N_FFN_GROUPS:])
        finish_grads(ffn_push, dy)
    grad_x = dy.reshape(a["x"].shape)

    small_full_shapes = [full[n].shape if n in CONV else a[n].shape for n in SMALL]
    small_grads = []
    for n, s in zip(SMALL, small_full_shapes):
        g = grads[n].reshape(-1)
        small_grads.append(g[:math.prod(s)])
    small_pack = _pack(small_grads + [loss_part.reshape(-1)], LANES, 8)
    small_sum = _sum_devices(_all_gather("gather_small_grads", small_pack))
    small_red = _unpack(small_sum, small_full_shapes + [(LANES,)])
    loss = small_red[-1][0]
    g_small = {}
    for n, g in zip(SMALL, small_red[:-1]):
        if n in CONV:
            cs = a[n].shape[1]
            g = lax.dynamic_slice_in_dim(g, dev * cs, cs, axis=1)
        g_small[n] = g
    small_shapes = [a[n].shape for n in SMALL]
    packs = [_pack([src[n] for n in SMALL], LANES, 8) for src in
             ({n: a[n] for n in SMALL}, g_small, {n: a["m_" + n] for n in SMALL}, {n: a["v_" + n] for n in SMALL})]
    d_small, m_small, v_small = (_unpack(o, small_shapes) for o in _adam_small(*packs))
    finish_grads(mixer_push, small_sum)

    small_out = dict(zip(SMALL, zip([g_small[n] for n in SMALL], d_small, m_small, v_small)))

    def out(k):
        return [small_out[n][k] if n in small_out else big_out[n][k] for n in WEIGHTS]

    return (loss, grad_x, *out(0), *out(1), *out(2), *out(3))
```

```python
import math

import jax
import jax.numpy as jnp
from jax import lax
from jax.experimental import pallas as pl
from jax.experimental.pallas import tpu as pltpu

F32 = jnp.float32
BF16 = jnp.bfloat16
F32X3 = lax.Precision.HIGH

LANES = 128
N_DEV = 8
N_HEADS = 8
HEAD = 128
NORM_EPS = 1e-6
DN_CHUNK = 64
ATT_BLOCK = 512
ATT_Q = 512
MLA_ROPE = 64
MLA_QK = 192
ROPE_THETA = 10000.0
VMEM_LIMIT = 56 * 1024 * 1024

ADAM_LR = 0.001
ADAM_B1 = 0.9
ADAM_B2 = 0.999
ADAM_EPS = 1e-08
ADAM_WD = 0.01
ADAM_STEP = 10


def _cparams(**kw):
    return pltpu.CompilerParams(vmem_limit_bytes=VMEM_LIMIT, **kw)


def _pick(n, cands):
    for c in cands:
        if c <= n and n % c == 0:
            return c
    return n


def _mm(name, a, b, *, ta=False, tb=False, out_dtype=F32, add=None, tm=None, tn=None, tk=None):
    if ta:
        K, M = a.shape
    else:
        M, K = a.shape
    N = b.shape[0] if tb else b.shape[1]
    tm = tm or _pick(M, (1024, 512, 256, 128))
    tn = tn or _pick(N, (1024, 512, 384, 256, 128))
    tk = tk or _pick(K, (1024, 1408, 512, 384, 256, 128))
    return _mm_raw(
        name, a, b, ta=ta, tb=tb, out_dtype=out_dtype, add=add, grid=(M // tm, N // tn, K // tk), out_shape=(M, N),
        a_block=(tk, tm) if ta else (tm, tk), a_map=(lambda i, j, k: (k, i)) if ta else (lambda i, j, k: (i, k)),
        b_block=(tn, tk) if tb else (tk, tn), b_map=(lambda i, j, k: (j, k)) if tb else (lambda i, j, k: (k, j)),
        o_block=(tm, tn), o_map=lambda i, j, k: (i, j))


def _mm_raw(name, a, b, *, ta, tb, out_dtype, add, grid, out_shape, a_block, a_map, b_block, b_map, o_block, o_map):
    nk = grid[2]
    tm, tn = o_block
    dn = (((0 if ta else 1,), (1 if tb else 0,)), ((), ()))
    has_add = add is not None

    def kern(*refs):
        if has_add:
            a_ref, b_ref, add_ref, o_ref, acc_ref = refs
        else:
            a_ref, b_ref, o_ref, acc_ref = refs
        k = pl.program_id(2)
        part = lax.dot_general(a_ref[...].astype(BF16), b_ref[...].astype(BF16), dn, preferred_element_type=F32)

        @pl.when(k == 0)
        def _():
            acc_ref[...] = part

        @pl.when(k > 0)
        def _():
            acc_ref[...] += part

        @pl.when(k == nk - 1)
        def _():
            r = acc_ref[...]
            if has_add:
                r = r + add_ref[...]
            o_ref[...] = r.astype(out_dtype)

    in_specs = [pl.BlockSpec(a_block, a_map), pl.BlockSpec(b_block, b_map)]
    args = [a, b]
    if has_add:
        in_specs.append(pl.BlockSpec(o_block, o_map))
        args.append(add)
    return pl.pallas_call(
        kern, name=name,
        grid=grid,
        in_specs=in_specs,
        out_specs=pl.BlockSpec(o_block, o_map),
        out_shape=jax.ShapeDtypeStruct(out_shape, out_dtype),
        scratch_shapes=[pltpu.VMEM((tm, tn), F32)],
        compiler_params=_cparams(dimension_semantics=("parallel", "parallel", "arbitrary")),
    )(*args)


def _rows(name, body, ins, outs, *, tt, consts=(), accs=()):
    in_specs, args = [], []
    first = ins[0][0] if isinstance(ins[0], tuple) else ins[0]
    t = first.shape[-2]
    tt = min(tt, t)
    for x in ins:
        if isinstance(x, tuple):
            arr, bs, im = x
            in_specs.append(pl.BlockSpec(bs, im))
            args.append(arr)
        else:
            in_specs.append(_row_spec(x.shape, tt))
            args.append(x)
    for c in consts:
        in_specs.append(pl.BlockSpec(c.shape, lambda i, _n=c.ndim: (0,) * _n))
        args.append(c)
    out_specs = [_row_spec(o.shape, tt) for o in outs]
    out_specs += [pl.BlockSpec(a.shape, lambda i, _n=len(a.shape): (0,) * _n) for a in accs]
    res = pl.pallas_call(
        body, name=name, grid=(t // tt,),
        in_specs=in_specs, out_specs=out_specs, out_shape=list(outs) + list(accs),
        compiler_params=_cparams(dimension_semantics=("arbitrary",)),
    )(*args)
    return res


def _row_spec(shape, tt):
    if len(shape) == 2:
        return pl.BlockSpec((tt, shape[1]), lambda i: (i, 0))
    return pl.BlockSpec((shape[0], tt, shape[2]), lambda i: (0, i, 0))


def _sds(shape, dtype=F32):
    return jax.ShapeDtypeStruct(tuple(shape), dtype)


def _acc(ref, val):
    i = pl.program_id(0)

    @pl.when(i == 0)
    def _():
        ref[...] = val

    @pl.when(i > 0)
    def _():
        ref[...] += val


def _rms(x, g):
    return x * lax.rsqrt(jnp.mean(x * x, axis=-1, keepdims=True) + NORM_EPS) * g


def _silu(x):
    return x / (1.0 + jnp.exp(-x))


def _softplus(x):
    return jnp.maximum(x, 0.0) + jnp.log(1.0 + jnp.exp(-jnp.abs(x)))


def _sigmoid(x):
    return 1.0 / (1.0 + jnp.exp(-x))


def _rmsnorm_fwd(name, x, g, tt=512):
    def body(x_ref, g_ref, h_ref):
        h_ref[...] = _rms(x_ref[...], g_ref[...]).astype(BF16)

    return _rows(name, body, [x], [_sds(x.shape, BF16)], tt=tt, consts=[g])[0]


def _rmsnorm_bwd(name, x, g, dh, dres, tt=512):
    def body(x_ref, dh_ref, dres_ref, g_ref, dx_ref, dg_ref):
        _, vjp = jax.vjp(_rms, x_ref[...], g_ref[...])
        dx, dg = vjp(dh_ref[...])
        dx_ref[...] = dx + dres_ref[...]
        _acc(dg_ref, dg)

    return _rows(name, body, [x, dh, dres], [_sds(x.shape)], tt=tt, consts=[g], accs=[_sds(g.shape)])


def _ffn_fwd(x, norm_g, w3, w_down):
    t, d = x.shape
    ns, _, cs = w3.shape
    half = ns // 2
    w2 = w3.reshape(ns * d, cs)
    h = _rmsnorm_fwd("ffn_norm", x, norm_g)
    tm = _pick(t, (1024, 512, 256, 128))
    nm = t // tm

    def gate_up(h_ref, wg_ref, wu_ref, g_ref, u_ref, a_ref):
        hv = h_ref[...]
        g = _dot(hv, wg_ref[...])
        u = _dot(hv, wu_ref[...])
        g_ref[...] = g
        u_ref[...] = u
        a_ref[...] = (_silu(g) * u).astype(BF16)

    hid = pl.BlockSpec((tm, cs), lambda j, i: (j * nm + i, 0))
    g, u, act = pl.pallas_call(
        gate_up, name="ffn_gate_up", grid=(half, nm),
        in_specs=[pl.BlockSpec((tm, d), lambda j, i: (i, 0)), pl.BlockSpec((d, cs), lambda j, i: (j, 0)),
                  pl.BlockSpec((d, cs), lambda j, i: (j + half, 0))],
        out_specs=[hid, hid, hid], out_shape=[_sds((half * t, cs)), _sds((half * t, cs)), _sds((half * t, cs), BF16)],
        compiler_params=_cparams(dimension_semantics=("parallel", "arbitrary")),
    )(h, w2, w2)
    y = _mm_raw("ffn_down", act, w_down, ta=False, tb=False, out_dtype=F32, add=x, grid=(nm, 1, half), out_shape=(t, d),
                a_block=(tm, cs), a_map=lambda i, j, k: (k * nm + i, 0), b_block=(cs, d), b_map=lambda i, j, k: (k, 0),
                o_block=(tm, d), o_map=lambda i, j, k: (i, 0))
    return y, (x, h, g, u, act)


def _ffn_bwd(saved, dy, norm_g, w3, w_down):
    x, h, g, u, act = saved
    t, d = x.shape
    ns, _, cs = w3.shape
    half = ns // 2
    w2 = w3.reshape(ns * d, cs)
    tm = _pick(t, (1024, 512, 256, 128))
    nm = t // tm
    tk = _pick(t, (1024, 512, 256, 128))
    nk = t // tk
    d_wdown = _mm_raw("ffn_down_wgrad", act, dy, ta=True, tb=False, out_dtype=F32, add=None, grid=(half, 1, nk),
                      out_shape=(half * cs, d), a_block=(tk, cs), a_map=lambda i, j, k: (i * nk + k, 0),
                      b_block=(tk, d), b_map=lambda i, j, k: (k, 0), o_block=(cs, d), o_map=lambda i, j, k: (i, 0))
    def down_dgrad(dy_ref, wd_ref, g_ref, u_ref, dg_ref, du_ref):
        da = _dot_nt(dy_ref[...].astype(BF16), wd_ref[...])
        gv, uv = g_ref[...], u_ref[...]
        s = _sigmoid(gv)
        dg_ref[...] = (da * uv * s * (1.0 + gv * (1.0 - s))).astype(BF16)
        du_ref[...] = (da * gv * s).astype(BF16)

    hid = pl.BlockSpec((tm, cs), lambda j, i: (j * nm + i, 0))
    dg, du = pl.pallas_call(
        down_dgrad, name="ffn_down_dgrad", grid=(half, nm),
        in_specs=[pl.BlockSpec((tm, d), lambda j, i: (i, 0)), pl.BlockSpec((cs, d), lambda j, i: (j, 0)), hid, hid],
        out_specs=[hid, hid], out_shape=[_sds((half * t, cs), BF16), _sds((half * t, cs), BF16)],
        compiler_params=_cparams(dimension_semantics=("parallel", "arbitrary")),
    )(dy, w_down, g, u)

    def wgrad(name, dd):
        return _mm_raw(name, h, dd, ta=True, tb=False, out_dtype=F32, add=None, grid=(1, half, nk), out_shape=(half * d, cs),
                       a_block=(tk, d), a_map=lambda i, j, k: (k, 0), b_block=(tk, cs), b_map=lambda i, j, k: (j * nk + k, 0),
                       o_block=(d, cs), o_map=lambda i, j, k: (j, 0))

    def dgrad(name, dd, off, add):
        return _mm_raw(name, dd, w2, ta=False, tb=True, out_dtype=F32, add=add, grid=(nm, 1, half), out_shape=(t, d),
                       a_block=(tm, cs), a_map=lambda i, j, k: (k * nm + i, 0), b_block=(d, cs), b_map=lambda i, j, k: (k + off, 0),
                       o_block=(tm, d), o_map=lambda i, j, k: (i, 0))

    d_w3 = jnp.concatenate([wgrad("ffn_gate_wgrad", dg), wgrad("ffn_up_wgrad", du)], axis=0).reshape(ns, d, cs)
    dh = dgrad("ffn_up_dgrad", du, half, dgrad("ffn_gate_dgrad", dg, 0, None))
    dx, dgain = _rmsnorm_bwd("ffn_norm_bwd", x, norm_g, dh, dy)
    return dx, dgain, d_w3, d_wdown


def _dot_nt(a, b):
    return lax.dot_general(a, b, (((1,), (1,)), ((), ())), preferred_element_type=F32)


def _dot_tn(a, b):
    return lax.dot_general(a, b, (((0,), (0,)), ((), ())), preferred_element_type=F32)


def _dot(a, b):
    return jnp.dot(a, b, preferred_element_type=F32)


CUM_BLOCK = 128


def _tri2(lower):
    r = lax.broadcasted_iota(jnp.int32, (CUM_BLOCK, CUM_BLOCK), 0)
    c = lax.broadcasted_iota(jnp.int32, (CUM_BLOCK, CUM_BLOCK), 1)
    tri = ((r > c) if lower else (r < c)).astype(BF16)
    return jnp.concatenate([tri, tri], axis=0)


def _run_sums(x, tri2, run, reverse):
    nb = x.shape[1] // CUM_BLOCK
    outs = [None] * nb
    for j in (reversed(range(nb)) if reverse else range(nb)):
        xj = x[:, j * CUM_BLOCK:(j + 1) * CUM_BLOCK]
        hi = xj.astype(BF16)
        lo = (xj - hi.astype(F32)).astype(BF16)
        outs[j] = _dot(jnp.concatenate([hi, lo], axis=1), tri2) + run
        run = run + jnp.sum(xj, axis=1, keepdims=True)
    return jnp.concatenate(outs, axis=1), run


def _log_sigmoid(z):
    return jnp.minimum(z, 0.0) - jnp.log(1.0 + jnp.exp(-jnp.abs(z)))


def _heads_in(ref, h, width=HEAD):
    return ref[:, h * width:(h + 1) * width]


def _sb_qk(q, k, gq, gk):
    return _rms(q, gq) * (HEAD ** -0.5), _rms(k, gk)


def _sb_prep_fwd(qkv, gq, gk):
    t = qkv.shape[0]

    def body(x_ref, gq_ref, gk_ref, q_ref, k_ref, v_ref):
        for h in range(N_HEADS):
            q, k = _sb_qk(_heads_in(x_ref, h), _heads_in(x_ref, N_HEADS + h), gq_ref[...], gk_ref[...])
            q_ref[h] = q.astype(BF16)
            k_ref[h] = k.astype(BF16)
            v_ref[h] = _heads_in(x_ref, 2 * N_HEADS + h).astype(BF16)

    hm = _sds((N_HEADS, t, HEAD), BF16)
    return _rows("sb_prep", body, [qkv], [hm, hm, hm], tt=256, consts=[gq, gk])


def _sb_prep_bwd(qkv, gq, gk, dq, dk, dv):
    def body(x_ref, dq_ref, dk_ref, dv_ref, gq_ref, gk_ref, dx_ref, dgq_ref, dgk_ref):
        dgq = jnp.zeros(gq_ref.shape, F32)
        dgk = jnp.zeros(gk_ref.shape, F32)
        for h in range(N_HEADS):
            _, vjp = jax.vjp(_sb_qk, _heads_in(x_ref, h), _heads_in(x_ref, N_HEADS + h), gq_ref[...], gk_ref[...])
            a, b, c, d = vjp((dq_ref[h], dk_ref[h]))
            dx_ref[:, h * HEAD:(h + 1) * HEAD] = a.astype(BF16)
            dx_ref[:, (N_HEADS + h) * HEAD:(N_HEADS + h + 1) * HEAD] = b.astype(BF16)
            dx_ref[:, (2 * N_HEADS + h) * HEAD:(2 * N_HEADS + h + 1) * HEAD] = dv_ref[h].astype(BF16)
            dgq, dgk = dgq + c, dgk + d
        _acc(dgq_ref, dgq)
        _acc(dgk_ref, dgk)

    return _rows("sb_prep_bwd", body, [qkv, dq, dk, dv], [_sds(qkv.shape, BF16)], tt=256, consts=[gq, gk],
                 accs=[_sds(gq.shape), _sds(gk.shape)])


def _q_block(t):
    return min(ATT_Q, t)


def _key_order(bq, qb):
    rows = lax.broadcasted_iota(jnp.int32, (bq, ATT_BLOCK), 0)
    cols = lax.broadcasted_iota(jnp.int32, (bq, ATT_BLOCK), 1)
    return rows - cols + qb * bq


def _sb_attn_fwd(q, k, v):
    nh, t, _ = q.shape
    bq = _q_block(t)
    per = bq // ATT_BLOCK

    def kern(q_ref, k_ref, v_ref, o_ref):
        qb = pl.program_id(1)
        qv = q_ref[0]
        after = _tri2(True)
        order = _key_order(bq, qb)
        nkb = (qb + 1) * per

        def body(i, carry, diagonal):
            o_acc, run = carry
            kb = nkb - 1 - i
            off = pl.multiple_of(kb * ATT_BLOCK, ATT_BLOCK)
            kv = k_ref[0, pl.ds(off, ATT_BLOCK), :]
            vv = v_ref[0, pl.ds(off, ATT_BLOCK), :]
            z = _dot_nt(qv, kv)
            lsz = _log_sigmoid(z)
            lsn = lsz - z
            if diagonal:
                past = order > kb * ATT_BLOCK
                lsn = jnp.where(past, lsn, 0.0)
            la, run = _run_sums(lsn, after, run, True)
            a = jnp.exp(lsz + la)
            if diagonal:
                a = jnp.where(past, a, 0.0)
            o_acc = o_acc + _dot(a.astype(BF16), vv)
            return o_acc, run

        carry = lax.fori_loop(0, per, lambda i, c: body(i, c, True), (jnp.zeros((bq, HEAD), F32), jnp.zeros((bq, 1), F32)))
        o, _ = lax.fori_loop(per, nkb, lambda i, c: body(i, c, False), carry)
        o_ref[...] = o

    return pl.pallas_call(
        kern, name="sb_attn_fwd", grid=(nh, t // bq),
        in_specs=[pl.BlockSpec((1, bq, HEAD), lambda h, i: (h, i, 0)),
                  pl.BlockSpec((1, t, HEAD), lambda h, i: (h, 0, 0)),
                  pl.BlockSpec((1, t, HEAD), lambda h, i: (h, 0, 0))],
        out_specs=pl.BlockSpec((bq, HEAD), lambda h, i: (i, h)),
        out_shape=_sds((t, nh * HEAD)),
        compiler_params=_cparams(dimension_semantics=("parallel", "arbitrary")),
    )(q, k, v)


def _sb_attn_bwd(q, k, v, do):
    nh, t, _ = q.shape
    bq = _q_block(t)
    per = bq // ATT_BLOCK

    def kern(q_ref, k_ref, v_ref, do_ref, dq_ref, dk_ref, dv_ref, g_s, ls_s):
        qb = pl.program_id(1)

        @pl.when(qb == 0)
        def _():
            dk_ref[...] = jnp.zeros(dk_ref.shape, F32)
            dv_ref[...] = jnp.zeros(dv_ref.shape, F32)

        qv = q_ref[0]
        dob = do_ref[...].astype(BF16)
        after, before = _tri2(True), _tri2(False)
        order = _key_order(bq, qb)
        nkb = (qb + 1) * per

        def sweep_left(i, run, diagonal):
            kb = nkb - 1 - i
            off = pl.multiple_of(kb * ATT_BLOCK, ATT_BLOCK)
            kv = k_ref[0, pl.ds(off, ATT_BLOCK), :]
            vv = v_ref[0, pl.ds(off, ATT_BLOCK), :]
            z = _dot_nt(qv, kv)
            lsz = _log_sigmoid(z)
            lsn = lsz - z
            if diagonal:
                past = order > kb * ATT_BLOCK
                lsn = jnp.where(past, lsn, 0.0)
            la, run = _run_sums(lsn, after, run, True)
            a = jnp.exp(lsz + la)
            if diagonal:
                a = jnp.where(past, a, 0.0)
            g_s[kb] = _dot_nt(dob, vv) * a
            ls_s[kb] = lsz
            dv_ref[0, pl.ds(off, ATT_BLOCK), :] += _dot_tn(a.astype(BF16), dob)
            return run

        zero = jnp.zeros((bq, 1), F32)
        run = lax.fori_loop(0, per, lambda i, c: sweep_left(i, c, True), zero)
        lax.fori_loop(per, nkb, lambda i, c: sweep_left(i, c, False), run)

        def sweep_right(kb, carry, diagonal):
            dq_acc, run_g = carry
            off = pl.multiple_of(kb * ATT_BLOCK, ATT_BLOCK)
            kv = k_ref[0, pl.ds(off, ATT_BLOCK), :]
            g = g_s[kb]
            sg = jnp.exp(ls_s[kb])
            dls, run_g = _run_sums(g, before, run_g, False)
            dz = g * (1.0 - sg) - dls * sg
            if diagonal:
                dz = jnp.where(order > kb * ATT_BLOCK, dz, 0.0)
            dzb = dz.astype(BF16)
            dk_ref[0, pl.ds(off, ATT_BLOCK), :] += _dot_tn(dzb, qv)
            return dq_acc + _dot(dzb, kv), run_g

        carry = lax.fori_loop(0, nkb - per, lambda i, c: sweep_right(i, c, False), (jnp.zeros((bq, HEAD), F32), zero))
        dq, _ = lax.fori_loop(nkb - per, nkb, lambda i, c: sweep_right(i, c, True), carry)
        dq_ref[0] = dq

    hm = _sds((nh, t, HEAD))
    full = pl.BlockSpec((1, t, HEAD), lambda h, i: (h, 0, 0))
    tok = pl.BlockSpec((bq, HEAD), lambda h, i: (i, h))
    nkb_max = t // ATT_BLOCK
    return pl.pallas_call(
        kern, name="sb_attn_bwd", grid=(nh, t // bq),
        in_specs=[pl.BlockSpec((1, bq, HEAD), lambda h, i: (h, i, 0)), full, full, tok],
        out_specs=[pl.BlockSpec((1, bq, HEAD), lambda h, i: (h, i, 0)), full, full],
        out_shape=[hm, hm, hm],
        scratch_shapes=[pltpu.VMEM((nkb_max, bq, ATT_BLOCK), F32), pltpu.VMEM((nkb_max, bq, ATT_BLOCK), F32)],
        compiler_params=_cparams(dimension_semantics=("parallel", "arbitrary")),
    )(q, k, v, do)


def _sb_fwd(x, h, w_qkv, gq, gk, w_out):
    qkv = _mm("sb_qkv", h, w_qkv)
    q, k, v = _sb_prep_fwd(qkv, gq, gk)
    o = _sb_attn_fwd(q, k, v)
    y = _mm("sb_out", o, w_out, add=x)
    return y, (h, qkv, q, k, v, o)


def _sb_bwd(saved, dy, w_qkv, gq, gk, w_out):
    h, qkv, q, k, v, o = saved
    d_wout = _mm("sb_out_wgrad", o, dy, ta=True)
    do = _mm("sb_out_dgrad", dy, w_out, tb=True)
    dq, dk, dv = _sb_attn_bwd(q, k, v, do)
    dqkv, dgq, dgk = _sb_prep_bwd(qkv, gq, gk, dq, dk, dv)
    d_wqkv = _mm("sb_qkv_wgrad", h, dqkv, ta=True)
    dh = _mm("sb_qkv_dgrad", dqkv, w_qkv, tb=True)
    return dh, d_wqkv, dgq, dgk, d_wout


DN_QKV = 3 * N_HEADS * HEAD
DN_PROJ = DN_QKV + N_HEADS * HEAD + LANES
DN_CONV = 4
HALO = 8
CONV_COLS = 512


def _dn_conv_fwd(proj, conv_w, tt=256):
    t = proj.shape[0]
    tt = min(tt, t)

    def body(u_ref, prev_ref, w_ref, c_ref):
        i = pl.program_id(0)
        for cc in range(DN_QKV // CONV_COLS):
            cs = slice(cc * CONV_COLS, (cc + 1) * CONV_COLS)
            cur = u_ref[:, cs]
            prev = jnp.where(i > 0, prev_ref[:, cs], 0.0)
            ext = jnp.concatenate([prev, cur], axis=0)
            y = cur * w_ref[DN_CONV - 1:DN_CONV, cs]
            for j in range(DN_CONV - 1):
                y = y + pltpu.roll(ext, DN_CONV - 1 - j, 0)[HALO:] * w_ref[j:j + 1, cs]
            c_ref[:, cs] = y

    return _rows("dn_conv", body,
                 [(proj, (tt, DN_QKV), lambda i: (i, 0)),
                  (proj, (HALO, DN_QKV), lambda i: (jnp.maximum(i * (tt // HALO) - 1, 0), 0))],
                 [_sds((t, DN_QKV))], tt=tt, consts=[conv_w])[0]


def _dn_conv_bwd(proj, conv_w, dc, dz, dab, tt=256):
    t = proj.shape[0]
    tt = min(tt, t)
    nblk = t // tt

    def body(u_ref, prev_ref, dc_ref, next_ref, dz_ref, dab_ref, w_ref, dp_ref, dw_ref):
        i = pl.program_id(0)
        dws = []
        for cc in range(DN_QKV // CONV_COLS):
            cs = slice(cc * CONV_COLS, (cc + 1) * CONV_COLS)
            cur = u_ref[:, cs]
            prev = jnp.where(i > 0, prev_ref[:, cs], 0.0)
            ext_u = jnp.concatenate([prev, cur], axis=0)
            d = dc_ref[:, cs]
            nxt = jnp.where(i < nblk - 1, next_ref[:, cs], 0.0)
            ext_d = jnp.concatenate([d, nxt], axis=0)
            du = d * w_ref[DN_CONV - 1:DN_CONV, cs]
            rows = [jnp.sum(d * cur, axis=0, keepdims=True)]
            for j in range(DN_CONV - 2, -1, -1):
                sh = DN_CONV - 1 - j
                du = du + pltpu.roll(ext_d, tt + HALO - sh, 0)[:tt] * w_ref[j:j + 1, cs]
                rows.insert(0, jnp.sum(d * pltpu.roll(ext_u, sh, 0)[HALO:], axis=0, keepdims=True))
            dp_ref[:, cs] = du.astype(BF16)
            dws.append(jnp.concatenate(rows, axis=0))
        dp_ref[:, DN_QKV:DN_QKV + N_HEADS * HEAD] = dz_ref[...].astype(BF16)
        dp_ref[:, DN_QKV + N_HEADS * HEAD:] = dab_ref[...].astype(BF16)
        _acc(dw_ref, jnp.concatenate(dws, axis=1))

    return _rows("dn_conv_bwd", body,
                 [(proj, (tt, DN_QKV), lambda i: (i, 0)),
                  (proj, (HALO, DN_QKV), lambda i: (jnp.maximum(i * (tt // HALO) - 1, 0), 0)),
                  dc,
                  (dc, (HALO, DN_QKV), lambda i: (jnp.minimum((i + 1) * (tt // HALO), t // HALO - 1), 0)),
                  dz, dab],
                 [_sds((t, DN_PROJ), BF16)], tt=tt, consts=[conv_w], accs=[_sds(conv_w.shape)])


def _l2n(x):
    return x * lax.rsqrt(jnp.sum(x * x, axis=-1, keepdims=True) + NORM_EPS)


def _dn_qkv(cq, ck, cv):
    return _l2n(_silu(cq)) * (HEAD ** -0.5), _l2n(_silu(ck)), _silu(cv)


def _dn_gates(ab, a_log, dt_bias):
    lane = lax.broadcasted_iota(jnp.int32, ab.shape, 1)
    g = -jnp.exp(a_log) * _softplus(ab + dt_bias)
    return jnp.where(lane < N_HEADS, g, jnp.where(lane < 2 * N_HEADS, _sigmoid(ab), 0.0))


def _ab_spec(tt):
    return (tt, LANES), lambda i: (i, DN_PROJ // LANES - 1)


def _dn_prep_fwd(c, proj, a_log, dt_bias, tt=256):
    t = c.shape[0]
    tt = min(tt, t)

    def body(c_ref, ab_ref, al_ref, dt_ref, q_ref, k_ref, v_ref, g_ref):
        for h in range(N_HEADS):
            q_ref[h], k_ref[h], v_ref[h] = _dn_qkv(_heads_in(c_ref, h), _heads_in(c_ref, N_HEADS + h), _heads_in(c_ref, 2 * N_HEADS + h))
        g_ref[...] = _dn_gates(ab_ref[...], al_ref[...], dt_ref[...])

    hm = _sds((N_HEADS, t, HEAD))
    return _rows("dn_prep", body, [c, (proj,) + _ab_spec(tt)], [hm, hm, hm, _sds((t, LANES))], tt=tt, consts=[a_log, dt_bias])


def _dn_prep_bwd(c, proj, a_log, dt_bias, dq, dk, dv, dgates, tt=256):
    t = c.shape[0]
    tt = min(tt, t)

    def body(c_ref, ab_ref, dq_ref, dk_ref, dv_ref, dg_ref, al_ref, dt_ref, dc_ref, dab_ref, dal_ref, ddt_ref):
        for h in range(N_HEADS):
            _, vjp = jax.vjp(_dn_qkv, _heads_in(c_ref, h), _heads_in(c_ref, N_HEADS + h), _heads_in(c_ref, 2 * N_HEADS + h))
            a, b, d = vjp((dq_ref[h], dk_ref[h], dv_ref[h]))
            dc_ref[:, h * HEAD:(h + 1) * HEAD] = a
            dc_ref[:, (N_HEADS + h) * HEAD:(N_HEADS + h + 1) * HEAD] = b
            dc_ref[:, (2 * N_HEADS + h) * HEAD:(2 * N_HEADS + h + 1) * HEAD] = d
        _, vjp = jax.vjp(_dn_gates, ab_ref[...], al_ref[...], dt_ref[...])
        dab, dal, ddt = vjp(dg_ref[...])
        dab_ref[...] = dab
        _acc(dal_ref, dal)
        _acc(ddt_ref, ddt)

    return _rows("dn_prep_bwd", body, [c, (proj,) + _ab_spec(tt), dq, dk, dv, dgates], [_sds(c.shape), _sds((t, LANES))],
                 tt=tt, consts=[a_log, dt_bias], accs=[_sds(a_log.shape), _sds(dt_bias.shape)])


def _bdot(a, b, prec=None):
    return lax.dot_general(a, b, (((2,), (1,)), ((0,), (0,))), precision=prec, preferred_element_type=F32)


def _bdot_nt(a, b, prec=None):
    return lax.dot_general(a, b, (((2,), (2,)), ((0,), (0,))), precision=prec, preferred_element_type=F32)


def _bdot_tn(a, b, prec=None):
    return lax.dot_general(a, b, (((1,), (1,)), ((0,), (0,))), precision=prec, preferred_element_type=F32)


def _inv_raw(low):
    c = low.shape[-1]
    r = lax.broadcasted_iota(jnp.int32, (c, c), 0)
    s = lax.broadcasted_iota(jnp.int32, (c, c), 1)
    m = jnp.where(r == s, 1.0, 0.0) - low
    p = _bdot(low, low, F32X3)
    n_fac = int(math.log2(c)) - 1
    for i in range(n_fac):
        m = m + _bdot(m, p, F32X3)
        if i < n_fac - 1:
            p = _bdot(p, p, F32X3)
    return m


@jax.custom_vjp
def _inv_unit_lower(low):
    return _inv_raw(low)


def _inv_fwd(low):
    m = _inv_raw(low)
    return m, m


def _inv_bwd(m, dm):
    return (-_bdot_nt(_bdot_tn(m, dm, F32X3), m, F32X3),)


_inv_unit_lower.defvjp(_inv_fwd, _inv_bwd)


def _dn_chunk(q, k, v, gates, s):
    nh, c, _ = q.shape
    lane = lax.broadcasted_iota(jnp.int32, gates.shape, 1)
    def column(j):
        return jnp.sum(jnp.where(lane == j, gates, 0.0), axis=1, keepdims=True)[None]

    g_col = jnp.concatenate([column(h) for h in range(nh)], axis=0)
    b_col = jnp.concatenate([column(h + nh) for h in range(nh)], axis=0)
    r = lax.broadcasted_iota(jnp.int32, (c, c), 0)
    cc = lax.broadcasted_iota(jnp.int32, (c, c), 1)
    causal, strict = r >= cc, r > cc
    incl = jnp.broadcast_to(jnp.where(causal, 1.0, 0.0), (nh, c, c))
    upper = jnp.broadcast_to(jnp.where(r <= cc, 1.0, 0.0), (nh, c, c))
    gb = jnp.broadcast_to(g_col, (nh, c, LANES))
    gbc = jnp.broadcast_to(g_col, (nh, c, c))
    gc = _bdot(incl, gb, F32X3)
    gc_r = _bdot(incl, gbc, F32X3)
    gc_c = _bdot_tn(gbc, upper, F32X3)
    decay = jnp.where(causal, jnp.exp(jnp.where(causal, gc_r - gc_c, 0.0)), 0.0)
    kb = k * b_col
    low = jnp.where(strict, _bdot_nt(kb, k) * decay, 0.0)
    m = _inv_unit_lower(low)
    egc = jnp.exp(gc)
    u = _bdot(m, v * b_col, F32X3)
    w = _bdot(m, kb * egc, F32X3)
    attn = _bdot_nt(q, k) * decay
    gl = jnp.sum(gb, axis=1, keepdims=True)
    v_new = u - _bdot(w, s)
    o = _bdot(q * egc, s) + _bdot(attn, v_new)
    s_new = s * jnp.exp(gl) + _bdot_tn(k * jnp.exp(gl - gc), v_new)
    return o, s_new


def _dn_chunks_fwd(q, k, v, gates):
    nh, t, _ = q.shape
    n = t // DN_CHUNK

    def kern(q_ref, k_ref, v_ref, g_ref, o_ref, sin_ref, s_scr):
        @pl.when(pl.program_id(0) == 0)
        def _():
            s_scr[...] = jnp.zeros(s_scr.shape, F32)

        s = s_scr[...]
        sin_ref[0] = s
        o_ref[...], s_scr[...] = _dn_chunk(q_ref[...], k_ref[...], v_ref[...], g_ref[...], s)

    blk = pl.BlockSpec((nh, DN_CHUNK, HEAD), lambda i: (0, i, 0))
    return pl.pallas_call(
        kern, name="dn_chunks_fwd", grid=(n,),
        in_specs=[blk, blk, blk, pl.BlockSpec((DN_CHUNK, LANES), lambda i: (i, 0))],
        out_specs=[blk, pl.BlockSpec((1, nh, HEAD, HEAD), lambda i: (i, 0, 0, 0))],
        out_shape=[_sds((nh, t, HEAD)), _sds((n, nh, HEAD, HEAD))],
        scratch_shapes=[pltpu.VMEM((nh, HEAD, HEAD), F32)],
        compiler_params=_cparams(dimension_semantics=("arbitrary",)),
    )(q, k, v, gates)


def _dn_chunks_bwd(q, k, v, gates, s_in, do):
    nh, t, _ = q.shape
    n = t // DN_CHUNK

    def kern(q_ref, k_ref, v_ref, g_ref, sin_ref, do_ref, dq_ref, dk_ref, dv_ref, dg_ref, ds_scr):
        @pl.when(pl.program_id(0) == 0)
        def _():
            ds_scr[...] = jnp.zeros(ds_scr.shape, F32)

        _, vjp = jax.vjp(_dn_chunk, q_ref[...], k_ref[...], v_ref[...], g_ref[...], sin_ref[0])
        dq_ref[...], dk_ref[...], dv_ref[...], dg_ref[...], ds_scr[...] = vjp((do_ref[...], ds_scr[...]))

    blk = pl.BlockSpec((nh, DN_CHUNK, HEAD), lambda i: (0, n - 1 - i, 0))
    gblk = pl.BlockSpec((DN_CHUNK, LANES), lambda i: (n - 1 - i, 0))
    hm = _sds((nh, t, HEAD))
    return pl.pallas_call(
        kern, name="dn_chunks_bwd", grid=(n,),
        in_specs=[blk, blk, blk, gblk, pl.BlockSpec((1, nh, HEAD, HEAD), lambda i: (n - 1 - i, 0, 0, 0)), blk],
        out_specs=[blk, blk, blk, gblk],
        out_shape=[hm, hm, hm, _sds((t, LANES))],
        scratch_shapes=[pltpu.VMEM((nh, HEAD, HEAD), F32)],
        compiler_params=_cparams(dimension_semantics=("arbitrary",)),
    )(q, k, v, gates, s_in, do)


def _dn_gate_out(o, z, g):
    return _rms(o, g) * _silu(z)


def _z_spec(tt):
    return (tt, N_HEADS * HEAD), lambda i: (i, DN_QKV // (N_HEADS * HEAD))


def _dn_post_fwd(o, proj, out_norm, tt=256):
    t = o.shape[1]
    tt = min(tt, t)

    def body(o_ref, z_ref, g_ref, y_ref):
        for h in range(N_HEADS):
            y_ref[:, h * HEAD:(h + 1) * HEAD] = _dn_gate_out(o_ref[h], _heads_in(z_ref, h), g_ref[...]).astype(BF16)

    return _rows("dn_post", body, [o, (proj,) + _z_spec(tt)], [_sds((t, N_HEADS * HEAD), BF16)], tt=tt, consts=[out_norm])[0]


def _dn_post_bwd(o, proj, out_norm, dy, tt=256):
    t = o.shape[1]
    tt = min(tt, t)

    def body(o_ref, z_ref, dy_ref, g_ref, do_ref, dz_ref, dg_ref):
        dg = jnp.zeros(g_ref.shape, F32)
        for h in range(N_HEADS):
            _, vjp = jax.vjp(_dn_gate_out, o_ref[h], _heads_in(z_ref, h), g_ref[...])
            a, b, d = vjp(_heads_in(dy_ref, h))
            do_ref[h] = a
            dz_ref[:, h * HEAD:(h + 1) * HEAD] = b
            dg = dg + d
        _acc(dg_ref, dg)

    return _rows("dn_post_bwd", body, [o, (proj,) + _z_spec(tt), dy], [_sds(o.shape), _sds((t, N_HEADS * HEAD))], tt=tt,
                 consts=[out_norm], accs=[_sds(out_norm.shape)])


def _dn_fwd(x, h, w_in, conv_w, a_log, dt_bias, out_norm, w_out):
    proj = _mm("dn_in", h, w_in)
    c = _dn_conv_fwd(proj, conv_w)
    q, k, v, gates = _dn_prep_fwd(c, proj, a_log, dt_bias)
    o, s_in = _dn_chunks_fwd(q, k, v, gates)
    on = _dn_post_fwd(o, proj, out_norm)
    y = _mm("dn_out", on, w_out, add=x)
    return y, (h, proj, c, q, k, v, gates, o, s_in, on)


def _dn_bwd(saved, dy, w_in, conv_w, a_log, dt_bias, out_norm, w_out):
    h, proj, c, q, k, v, gates, o, s_in, on = saved
    d_wout = _mm("dn_out_wgrad", on, dy, ta=True)
    don = _mm("dn_out_dgrad", dy, w_out, tb=True)
    do, dz, d_out_norm = _dn_post_bwd(o, proj, out_norm, don)
    dq, dk, dv, dgates = _dn_chunks_bwd(q, k, v, gates, s_in, do)
    dc, dab, d_a_log, d_dt_bias = _dn_prep_bwd(c, proj, a_log, dt_bias, dq, dk, dv, dgates)
    dproj, d_conv_w = _dn_conv_bwd(proj, conv_w, dc, dz, dab)
    d_win = _mm("dn_in_wgrad", h, dproj, ta=True)
    dh = _mm("dn_in_dgrad", dproj, w_in, tb=True)
    return dh, d_win, d_conv_w, d_a_log, d_dt_bias, d_out_norm, d_wout


MLA_SCALE = MLA_QK ** -0.5
MLA_C = 512


def _swap_raw(x):
    lane = lax.broadcasted_iota(jnp.int32, x.shape, 1)
    half = MLA_ROPE // 2
    y = jnp.where(lane < half, pltpu.roll(x, LANES - half, 1), pltpu.roll(x, half, 1))
    return jnp.where(lane < MLA_ROPE, y, 0.0)


@jax.custom_vjp
def _swap_halves(x):
    return _swap_raw(x)


_swap_halves.defvjp(lambda x: (_swap_raw(x), None), lambda _, d: (_swap_raw(d),))


def _rms_rope(x, g, cos, sin):
    y = x * lax.rsqrt(jnp.sum(x * x, axis=-1, keepdims=True) * (1.0 / MLA_ROPE) + NORM_EPS) * g
    return y * cos + _swap_halves(y) * sin


def _mla_latent(cq, ckv, kr, gq, gkv, gkr, cos, sin):
    return _rms(cq, gq), _rms(ckv, gkv), _rms_rope(kr, gkr, cos, sin)


def _mla_prep1_fwd(c, gq, gkv, gkr, cos, sin):
    t = c.shape[0]

    def body(c_ref, cos_ref, sin_ref, gq_ref, gkv_ref, gkr_ref, cq_ref, ckv_ref, kr_ref):
        a, b, r = _mla_latent(c_ref[:, :256], c_ref[:, 256:384], c_ref[:, 384:], gq_ref[...], gkv_ref[...], gkr_ref[...],
                              cos_ref[...], sin_ref[...])
        cq_ref[...] = a.astype(BF16)
        ckv_ref[...] = b.astype(BF16)
        kr_ref[...] = r.astype(BF16)

    return _rows("mla_prep1", body, [c, cos, sin], [_sds((t, 256), BF16), _sds((t, HEAD), BF16), _sds((t, HEAD), BF16)],
                 tt=512, consts=[gq, gkv, gkr])


def _mla_prep1_bwd(c, gq, gkv, gkr, cos, sin, dcq, dckv, dkr_heads):
    def body(c_ref, cos_ref, sin_ref, dcq_ref, dckv_ref, dkr_ref, gq_ref, gkv_ref, gkr_ref, dc_ref, dgq_ref, dgkv_ref, dgkr_ref):
        dkr = dkr_ref[0]
        for h in range(1, N_HEADS):
            dkr = dkr + dkr_ref[h]
        _, vjp = jax.vjp(_mla_latent, c_ref[:, :256], c_ref[:, 256:384], c_ref[:, 384:], gq_ref[...], gkv_ref[...], gkr_ref[...],
                         cos_ref[...], sin_ref[...])
        a, b, r, d1, d2, d3, _, _ = vjp((dcq_ref[...], dckv_ref[...], dkr))
        dc_ref[:, :256] = a.astype(BF16)
        dc_ref[:, 256:384] = b.astype(BF16)
        dc_ref[:, 384:] = r.astype(BF16)
        _acc(dgq_ref, d1)
        _acc(dgkv_ref, d2)
        _acc(dgkr_ref, d3)

    return _rows("mla_prep1_bwd", body, [c, cos, sin, dcq, dckv, dkr_heads], [_sds(c.shape, BF16)], tt=512,
                 consts=[gq, gkv, gkr], accs=[_sds(gq.shape), _sds(gkv.shape), _sds(gkr.shape)])


def _mla_heads(qn, qr, kn, gqn, gqr, gkn, cos, sin):
    return _rms(qn, gqn) * MLA_SCALE, _rms_rope(qr, gqr, cos, sin) * MLA_SCALE, _rms(kn, gkn)


def _mla_prep2_fwd(qa, kv, gqn, gqr, gkn, cos, sin):
    t = qa.shape[0]

    def body(qa_ref, kv_ref, cos_ref, sin_ref, gqn_ref, gqr_ref, gkn_ref, qn_ref, qr_ref, kn_ref, v_ref):
        for h in range(N_HEADS):
            a, b, c = _mla_heads(_heads_in(qa_ref, h), _heads_in(qa_ref, N_HEADS + h), _heads_in(kv_ref, h),
                                 gqn_ref[...], gqr_ref[...], gkn_ref[...], cos_ref[...], sin_ref[...])
            qn_ref[h] = a.astype(BF16)
            qr_ref[h] = b.astype(BF16)
            kn_ref[h] = c.astype(BF16)
            v_ref[h] = _heads_in(kv_ref, N_HEADS + h).astype(BF16)

    hm = _sds((N_HEADS, t, HEAD), BF16)
    return _rows("mla_prep2", body, [qa, kv, cos, sin], [hm, hm, hm, hm], tt=256, consts=[gqn, gqr, gkn])


def _mla_prep2_bwd(qa, kv, gqn, gqr, gkn, cos, sin, dqn, dqr, dkn, dv):
    def body(qa_ref, kv_ref, cos_ref, sin_ref, dqn_ref, dqr_ref, dkn_ref, dv_ref, gqn_ref, gqr_ref, gkn_ref,
             dqa_ref, dkv_ref, d1_ref, d2_ref, d3_ref):
        d1 = jnp.zeros(gqn_ref.shape, F32)
        d2 = jnp.zeros(gqr_ref.shape, F32)
        d3 = jnp.zeros(gkn_ref.shape, F32)
        for h in range(N_HEADS):
            _, vjp = jax.vjp(_mla_heads, _heads_in(qa_ref, h), _heads_in(qa_ref, N_HEADS + h), _heads_in(kv_ref, h),
                             gqn_ref[...], gqr_ref[...], gkn_ref[...], cos_ref[...], sin_ref[...])
            a, b, c, e1, e2, e3, _, _ = vjp((dqn_ref[h], dqr_ref[h], dkn_ref[h]))
            dqa_ref[:, h * HEAD:(h + 1) * HEAD] = a.astype(BF16)
            dqa_ref[:, (N_HEADS + h) * HEAD:(N_HEADS + h + 1) * HEAD] = b.astype(BF16)
            dkv_ref[:, h * HEAD:(h + 1) * HEAD] = c.astype(BF16)
            dkv_ref[:, (N_HEADS + h) * HEAD:(N_HEADS + h + 1) * HEAD] = dv_ref[h].astype(BF16)
            d1, d2, d3 = d1 + e1, d2 + e2, d3 + e3
        _acc(d1_ref, d1)
        _acc(d2_ref, d2)
        _acc(d3_ref, d3)

    return _rows("mla_prep2_bwd", body, [qa, kv, cos, sin, dqn, dqr, dkn, dv], [_sds(qa.shape, BF16), _sds(kv.shape, BF16)],
                 tt=256, consts=[gqn, gqr, gkn], accs=[_sds(gqn.shape), _sds(gqr.shape), _sds(gkn.shape)])


def _mla_attn_fwd(qn, qr, kn, kr, v):
    nh, t, _ = qn.shape
    bq = _q_block(t)
    per = bq // ATT_BLOCK

    def kern(qn_ref, qr_ref, kn_ref, kr_ref, v_ref, o_ref, lse_ref):
        qb = pl.program_id(1)
        qv = jnp.concatenate([qn_ref[0], qr_ref[0]], axis=1)
        order = _key_order(bq, qb)

        def body(kb, carry, diagonal):
            acc, m, l = carry
            off = pl.multiple_of(kb * ATT_BLOCK, ATT_BLOCK)
            kv = jnp.concatenate([kn_ref[0, pl.ds(off, ATT_BLOCK), :], kr_ref[pl.ds(off, ATT_BLOCK), :]], axis=1)
            s = _dot_nt(qv, kv)
            if diagonal:
                s = jnp.where(order >= kb * ATT_BLOCK, s, -jnp.inf)
            m_new = jnp.maximum(m, jnp.max(s, axis=1, keepdims=True))
            alpha = jnp.exp(m - m_new)
            p = jnp.exp(s - m_new)
            acc = acc * alpha + _dot(p.astype(BF16), v_ref[0, pl.ds(off, ATT_BLOCK), :])
            return acc, m_new, l * alpha + jnp.sum(p, axis=1, keepdims=True)

        init = (jnp.zeros((bq, HEAD), F32), jnp.full((bq, 1), -jnp.inf, F32), jnp.zeros((bq, 1), F32))
        carry = lax.fori_loop(0, qb * per, lambda i, c: body(i, c, False), init)
        acc, m, l = lax.fori_loop(qb * per, (qb + 1) * per, lambda i, c: body(i, c, True), carry)
        o_ref[...] = acc / l
        lse_ref[...] = jnp.broadcast_to(m + jnp.log(l), (bq, HEAD))

    blk = pl.BlockSpec((1, bq, HEAD), lambda h, i: (h, i, 0))
    full = pl.BlockSpec((1, t, HEAD), lambda h, i: (h, 0, 0))
    tok = pl.BlockSpec((bq, HEAD), lambda h, i: (i, h))
    return pl.pallas_call(
        kern, name="mla_attn_fwd", grid=(nh, t // bq),
        in_specs=[blk, blk, full, pl.BlockSpec((t, HEAD), lambda h, i: (0, 0)), full],
        out_specs=[tok, tok], out_shape=[_sds((t, nh * HEAD)), _sds((t, nh * HEAD))],
        compiler_params=_cparams(dimension_semantics=("parallel", "arbitrary")),
    )(qn, qr, kn, kr, v)


def _mla_attn_bwd(qn, qr, kn, kr, v, o, lse, do):
    nh, t, _ = qn.shape
    bq = _q_block(t)
    per = bq // ATT_BLOCK

    def kern(qn_ref, qr_ref, kn_ref, kr_ref, v_ref, o_ref, lse_ref, do_ref, dqn_ref, dqr_ref, dkn_ref, dkr_ref, dv_ref):
        qb = pl.program_id(1)

        @pl.when(qb == 0)
        def _():
            dkn_ref[...] = jnp.zeros(dkn_ref.shape, F32)
            dkr_ref[...] = jnp.zeros(dkr_ref.shape, F32)
            dv_ref[...] = jnp.zeros(dv_ref.shape, F32)

        qv = jnp.concatenate([qn_ref[0], qr_ref[0]], axis=1)
        dov = do_ref[...]
        dob = dov.astype(BF16)
        delta = jnp.sum(dov * o_ref[...], axis=1, keepdims=True)
        lse_col = lse_ref[:, :1]
        order = _key_order(bq, qb)

        def body(kb, dq, diagonal):
            off = pl.multiple_of(kb * ATT_BLOCK, ATT_BLOCK)
            kv = jnp.concatenate([kn_ref[0, pl.ds(off, ATT_BLOCK), :], kr_ref[pl.ds(off, ATT_BLOCK), :]], axis=1)
            vv = v_ref[0, pl.ds(off, ATT_BLOCK), :]
            p = jnp.exp(_dot_nt(qv, kv) - lse_col)
            if diagonal:
                p = jnp.where(order >= kb * ATT_BLOCK, p, 0.0)
            ds = (p * (_dot_nt(dob, vv) - delta)).astype(BF16)
            dk = _dot_tn(ds, qv)
            dkn_ref[0, pl.ds(off, ATT_BLOCK), :] += dk[:, :HEAD]
            dkr_ref[0, pl.ds(off, ATT_BLOCK), :] += dk[:, HEAD:]
            dv_ref[0, pl.ds(off, ATT_BLOCK), :] += _dot_tn(p.astype(BF16), dob)
            return dq + _dot(ds, kv)

        dq = lax.fori_loop(0, qb * per, lambda i, c: body(i, c, False), jnp.zeros((bq, 2 * HEAD), F32))
        dq = lax.fori_loop(qb * per, (qb + 1) * per, lambda i, c: body(i, c, True), dq)
        dqn_ref[0] = dq[:, :HEAD]
        dqr_ref[0] = dq[:, HEAD:]

    hm = _sds((nh, t, HEAD))
    blk = pl.BlockSpec((1, bq, HEAD), lambda h, i: (h, i, 0))
    full = pl.BlockSpec((1, t, HEAD), lambda h, i: (h, 0, 0))
    tok = pl.BlockSpec((bq, HEAD), lambda h, i: (i, h))
    return pl.pallas_call(
        kern, name="mla_attn_bwd", grid=(nh, t // bq),
        in_specs=[blk, blk, full, pl.BlockSpec((t, HEAD), lambda h, i: (0, 0)), full, tok, tok, tok],
        out_specs=[blk, blk, full, full, full], out_shape=[hm, hm, hm, hm, hm],
        compiler_params=_cparams(dimension_semantics=("parallel", "arbitrary")),
    )(qn, qr, kn, kr, v, o, lse, do)


def _rope_tables(t):
    inv_freq = ROPE_THETA ** (-jnp.arange(0, MLA_ROPE, 2, dtype=F32) / MLA_ROPE)
    ang = jnp.arange(t, dtype=F32)[:, None] * inv_freq[None, :]
    c, s = jnp.cos(ang), jnp.sin(ang)
    pad = ((0, 0), (0, LANES - MLA_ROPE))
    return jnp.pad(jnp.concatenate([c, c], axis=1), pad), jnp.pad(jnp.concatenate([-s, s], axis=1), pad)


def _pad_lanes(v, n=LANES):
    return jnp.pad(v, (0, n - v.shape[0])).reshape(1, n)


def _mla_layout(w_down, w_uq, w_ukv):
    w_down_p = jnp.pad(w_down, ((0, 0), (0, MLA_C - w_down.shape[1])))
    uq = w_uq.reshape(w_uq.shape[0], N_HEADS, MLA_QK)
    rope = jnp.pad(uq[:, :, HEAD:], ((0, 0), (0, 0), (0, LANES - MLA_ROPE)))
    w_uq_p = jnp.concatenate([uq[:, :, :HEAD].reshape(-1, N_HEADS * HEAD), rope.reshape(-1, N_HEADS * LANES)], axis=1)
    ukv = w_ukv.reshape(w_ukv.shape[0], N_HEADS, 2 * HEAD)
    w_ukv_p = jnp.concatenate([ukv[:, :, :HEAD].reshape(-1, N_HEADS * HEAD), ukv[:, :, HEAD:].reshape(-1, N_HEADS * HEAD)], axis=1)
    return w_down_p, w_uq_p, w_ukv_p


def _mla_unlayout(d_down_p, d_uq_p, d_ukv_p):
    d_down = d_down_p[:, :256 + HEAD + MLA_ROPE]
    nope = d_uq_p[:, :N_HEADS * HEAD].reshape(-1, N_HEADS, HEAD)
    rope = d_uq_p[:, N_HEADS * HEAD:].reshape(-1, N_HEADS, LANES)[:, :, :MLA_ROPE]
    d_uq = jnp.concatenate([nope, rope], axis=2).reshape(-1, N_HEADS * MLA_QK)
    kn = d_ukv_p[:, :N_HEADS * HEAD].reshape(-1, N_HEADS, HEAD)
    vv = d_ukv_p[:, N_HEADS * HEAD:].reshape(-1, N_HEADS, HEAD)
    d_ukv = jnp.concatenate([kn, vv], axis=2).reshape(-1, N_HEADS * 2 * HEAD)
    return d_down, d_uq, d_ukv


def _mla_weight_shapes():
    return (_sds((1024, MLA_C), BF16), _sds((1, 256)), _sds((1, HEAD)), _sds((256, 2048), BF16), _sds((HEAD, 2048), BF16),
            _sds((1, HEAD)), _sds((1, HEAD)), _sds((1, HEAD)), _sds((1, HEAD)), _sds((1024, 1024), BF16),
            _sds((4096, HEAD)), _sds((4096, HEAD)))


def _mla_fwd(x, h, w_down, gq, gkv, w_uq, w_ukv, gqn, gqr, gkn, gkr, w_out, cos, sin):
    c = _mm("mla_down", h, w_down)
    cq, ckv, kr = _mla_prep1_fwd(c, gq, gkv, gkr, cos, sin)
    qa = _mm("mla_uq", cq, w_uq)
    kv = _mm("mla_ukv", ckv, w_ukv)
    qn, qr, kn, v = _mla_prep2_fwd(qa, kv, gqn, gqr, gkn, cos, sin)
    o, lse = _mla_attn_fwd(qn, qr, kn, kr, v)
    y = _mm("mla_out", o, w_out, add=x)
    return y, (h, c, cq, ckv, kr, qa, kv, qn, qr, kn, v, o, lse)


def _mla_bwd(saved, dy, w_down, gq, gkv, w_uq, w_ukv, gqn, gqr, gkn, gkr, w_out, cos, sin):
    h, c, cq, ckv, kr, qa, kv, qn, qr, kn, v, o, lse = saved
    d_wout = _mm("mla_out_wgrad", o, dy, ta=True)
    do = _mm("mla_out_dgrad", dy, w_out, tb=True)
    dqn, dqr, dkn, dkr, dv = _mla_attn_bwd(qn, qr, kn, kr, v, o, lse, do)
    dqa, dkv, dgqn, dgqr, dgkn = _mla_prep2_bwd(qa, kv, gqn, gqr, gkn, cos, sin, dqn, dqr, dkn, dv)
    d_wuq = _mm("mla_uq_wgrad", cq, dqa, ta=True)
    d_wukv = _mm("mla_ukv_wgrad", ckv, dkv, ta=True)
    dcq = _mm("mla_uq_dgrad", dqa, w_uq, tb=True)
    dckv = _mm("mla_ukv_dgrad", dkv, w_ukv, tb=True)
    dc, dgq, dgkv, dgkr = _mla_prep1_bwd(c, gq, gkv, gkr, cos, sin, dcq, dckv, dkr)
    d_wdown = _mm("mla_down_wgrad", h, dc, ta=True)
    dh = _mm("mla_down_dgrad", dc, w_down, tb=True)
    return dh, d_wdown, dgq, dgkv, d_wuq, d_wukv, dgqn, dgqr, dgkn, dgkr, d_wout


def _loss_head(y, target):
    d = y.shape[1]

    def body(y_ref, t_ref, dy_ref, l_ref):
        err = y_ref[...] - t_ref[...]
        dy_ref[...] = err * (1.0 / d)
        part = 0.5 * jnp.sum(jnp.sum(err * err, axis=1, keepdims=True) * (1.0 / d), axis=0, keepdims=True)
        _acc(l_ref, jnp.broadcast_to(part, (1, LANES)))

    return _rows("loss_head", body, [y, target], [_sds(y.shape)], tt=512, accs=[_sds((1, LANES))])


MESH_ID = pl.DeviceIdType.MESH
HBM_SPEC = pl.BlockSpec(memory_space=pltpu.HBM)


def _all_gather(name, x):
    m_per, n = x.shape

    def body(x_ref, out_ref, send_sems, recv_sems, local_sem):
        x, y, c = lax.axis_index("x"), lax.axis_index("y"), lax.axis_index("c")
        me, sibling = (x, y, c), (x, y, 1 - c)
        chips = [(1 - x, y), (x, 1 - y), (1 - x, 1 - y)]

        def rows(px, py, pc):
            return out_ref.at[pl.ds((4 * px + 2 * py + pc) * m_per, m_per), :]

        def copy(k, block, to, src=None):
            return pltpu.make_async_remote_copy(
                src_ref=rows(*block) if src is None else src, dst_ref=rows(*block),
                send_sem=send_sems.at[k], recv_sem=recv_sems.at[k], device_id=to, device_id_type=MESH_ID)

        mine = pltpu.make_async_copy(x_ref, rows(*me), local_sem)
        mine.start()
        first = [copy(0, me, sibling, src=x_ref)]
        first += [copy(1 + j, me, (*chip, c), src=x_ref) for j, chip in enumerate(chips)]
        for cp in first:
            cp.start()
        passed = [copy(4 + j, (*chip, c), sibling) for j, chip in enumerate(chips)]
        for j, chip in enumerate(chips):
            copy(1 + j, (*chip, c), me).wait_recv()
            passed[j].start()
        copy(0, sibling, me).wait_recv()
        for j, chip in enumerate(chips):
            copy(4 + j, (*chip, 1 - c), me).wait_recv()
        for cp in first + passed:
            cp.wait_send()
        mine.wait()

    return pl.pallas_call(
        body, name=name,
        out_shape=jax.ShapeDtypeStruct((N_DEV * m_per, n), x.dtype),
        in_specs=[HBM_SPEC], out_specs=HBM_SPEC,
        scratch_shapes=[pltpu.SemaphoreType.DMA((7,)), pltpu.SemaphoreType.DMA((7,)), pltpu.SemaphoreType.DMA],
    )(x)


def _all_gather_groups(name, xs):
    ng = len(xs)

    def body(*refs):
        x_refs, out_refs, token = refs[:ng], refs[ng:2 * ng], refs[2 * ng]
        send_sems, recv_sems, local_sems = refs[2 * ng + 1:]
        token[...] = jnp.zeros(token.shape, F32)
        x, y, c = lax.axis_index("x"), lax.axis_index("y"), lax.axis_index("c")
        me, sibling = (x, y, c), (x, y, 1 - c)
        chips = [(1 - x, y), (x, 1 - y), (1 - x, 1 - y)]

        def copy(g, k, block, to, src=None):
            px, py, pc = block
            dst = out_refs[g].at[4 * px + 2 * py + pc]
            return pltpu.make_async_remote_copy(
                src_ref=dst if src is None else src, dst_ref=dst,
                send_sem=send_sems.at[g, k], recv_sem=recv_sems.at[g, k], device_id=to, device_id_type=MESH_ID)

        mine = [pltpu.make_async_copy(x_refs[g], out_refs[g].at[4 * x + 2 * y + c], local_sems.at[g]) for g in range(ng)]
        for cp in mine:
            cp.start()
        first = []
        for g in range(ng):
            first.append(copy(g, 0, me, sibling, src=x_refs[g]))
            first += [copy(g, 1 + j, me, (*chip, c), src=x_refs[g]) for j, chip in enumerate(chips)]
        for cp in first:
            cp.start()
        passed = []
        for j, chip in enumerate(chips):
            for g in range(ng):
                copy(g, 1 + j, (*chip, c), me).wait_recv()
                passed.append(copy(g, 4 + j, (*chip, c), sibling))
                passed[-1].start()
        for g in range(ng):
            copy(g, 0, sibling, me).wait_recv()
            for j, chip in enumerate(chips):
                copy(g, 4 + j, (*chip, 1 - c), me).wait_recv()
        for cp in first + passed:
            cp.wait_send()
        for cp in mine:
            cp.wait()

    return pl.pallas_call(
        body, name=name,
        out_shape=[jax.ShapeDtypeStruct((N_DEV,) + x.shape, x.dtype) for x in xs] + [_sds((8, LANES))],
        in_specs=[HBM_SPEC] * ng, out_specs=[HBM_SPEC] * ng + [pl.BlockSpec(memory_space=pltpu.VMEM)],
        scratch_shapes=[pltpu.SemaphoreType.DMA((ng, 7)), pltpu.SemaphoreType.DMA((ng, 7)), pltpu.SemaphoreType.DMA((ng,))],
    )(*xs)


EFFECT = pltpu.SideEffectType.DATAFLOW_SIDE_EFFECTING
SEM_SPEC = pl.BlockSpec(memory_space=pltpu.SEMAPHORE)


def _push_copies(src_refs, land_refs, send_sems, recv_sems, chunked):
    x, y, c = lax.axis_index("x"), lax.axis_index("y"), lax.axis_index("c")
    me = 4 * x + 2 * y + c
    copies = []
    for g, (src, land) in enumerate(zip(src_refs, land_refs)):
        for k in range(1, N_DEV):
            px = 1 - x if k & 4 else x
            py = 1 - y if k & 2 else y
            pc = 1 - c if k & 1 else c
            copies.append(pltpu.make_async_remote_copy(
                src_ref=src.at[4 * px + 2 * py + pc] if chunked else src, dst_ref=land.at[me],
                send_sem=send_sems.at[g * (N_DEV - 1) + k - 1], recv_sem=recv_sems.at[g * (N_DEV - 1) + k - 1],
                device_id=(px, py, pc), device_id_type=MESH_ID))
    return copies


def _hbm(x):
    return pltpu.with_memory_space_constraint(x, pltpu.HBM)


def _push_start(name, srcs, lands, chunked):
    ng = len(srcs)

    def body(*refs):
        for cp in _push_copies(refs[:ng], refs[ng:2 * ng], refs[2 * ng], refs[2 * ng + 1], chunked):
            cp.start()
        refs[-1][...] = jnp.zeros(refs[-1].shape, F32)

    bufs = list(srcs) + list(lands)
    outs = pl.pallas_call(
        body, name=name,
        out_shape=(pltpu.SemaphoreType.DMA((ng * (N_DEV - 1),)), pltpu.SemaphoreType.DMA((ng * (N_DEV - 1),)),
                   *[pltpu.HBM(b.shape, b.dtype) for b in bufs], jax.ShapeDtypeStruct((8, LANES), F32)),
        in_specs=[HBM_SPEC] * (2 * ng),
        out_specs=(SEM_SPEC, SEM_SPEC, *[HBM_SPEC] * (2 * ng), pl.BlockSpec(memory_space=pltpu.VMEM)),
        input_output_aliases={i: 2 + i for i in range(2 * ng)},
        compiler_params=pltpu.CompilerParams(has_side_effects=EFFECT),
    )(*[_hbm(b) for b in bufs])
    return outs[0], outs[1], list(outs[2:2 + ng]), list(outs[2 + ng:2 + 2 * ng]), outs[-1]


def _push_wait(name, started, after, chunked):
    send_sems, recv_sems, srcs, lands, _ = started
    ng = len(srcs)

    def body(*refs):
        copies = _push_copies(refs[:ng], refs[ng:2 * ng], refs[2 * ng], refs[2 * ng + 1], chunked)
        for cp in copies:
            cp.wait_send()
        for cp in copies:
            cp.wait_recv()

    bufs = srcs + lands
    outs = pl.pallas_call(
        body, name=name,
        out_shape=tuple(pltpu.HBM(b.shape, b.dtype) for b in bufs),
        in_specs=[HBM_SPEC] * (2 * ng) + [SEM_SPEC, SEM_SPEC, pl.BlockSpec(memory_space=pl.ANY)],
        out_specs=tuple([HBM_SPEC] * (2 * ng)),
        input_output_aliases={i: i for i in range(2 * ng)},
        compiler_params=pltpu.CompilerParams(has_side_effects=EFFECT),
    )(*bufs, send_sems, recv_sems, after)
    return list(outs[ng:])


def _sum_adam_devices(own, recv, dev, w, m, v):
    ndev, r, c_ = recv.shape
    tr = _pick(r, (256, 128, 96, 32))

    def body(dev_ref, own_ref, r_ref, w_ref, m_ref, v_ref, g_ref, d_ref, mo_ref, vo_ref):
        me = dev_ref[0]
        g = jnp.where(me == 0, own_ref[...], r_ref[0].astype(F32))
        for j in range(1, ndev):
            g = g + jnp.where(me == j, own_ref[...], r_ref[j].astype(F32))
        g_ref[...] = g
        d_ref[...], mo_ref[...], vo_ref[...] = _adam(w_ref[...], g, m_ref[...], v_ref[...])

    row = pl.BlockSpec((tr, c_), lambda i, dev_ref: (i, 0))
    return pl.pallas_call(
        body, name="sum_adam",
        grid_spec=pltpu.PrefetchScalarGridSpec(
            num_scalar_prefetch=1, grid=(r // tr,),
            in_specs=[row, pl.BlockSpec((ndev, tr, c_), lambda i, dev_ref: (0, i, 0)), row, row, row],
            out_specs=[row, row, row, row]),
        out_shape=[_sds((r, c_))] * 4,
        compiler_params=_cparams(dimension_semantics=("arbitrary",)),
    )(dev, own, recv, w, m, v)


def _cols_from_shards(w, width):
    ns, r, cs = w.shape
    tr = _pick(r, (256, 128))

    def body(w_ref, o_ref):
        parts = [w_ref[j] for j in range(ns)]
        if width > ns * cs:
            parts.append(jnp.zeros((tr, width - ns * cs), w.dtype))
        o_ref[...] = jnp.concatenate(parts, axis=1)

    return pl.pallas_call(
        body, name="cols_from_shards", grid=(r // tr,),
        in_specs=[pl.BlockSpec((ns, tr, cs), lambda i: (0, i, 0))], out_specs=pl.BlockSpec((tr, width), lambda i: (i, 0)),
        out_shape=jax.ShapeDtypeStruct((r, width), w.dtype), compiler_params=_cparams(dimension_semantics=("arbitrary",)),
    )(w)


def _shards_from_cols(g, cs):
    r, width = g.shape
    tr = _pick(r, (256, 128))

    def body(g_ref, o_ref):
        for j in range(N_DEV):
            o_ref[j] = g_ref[:, j * cs:(j + 1) * cs]

    return pl.pallas_call(
        body, name="shards_from_cols", grid=(r // tr,),
        in_specs=[pl.BlockSpec((tr, width), lambda i: (i, 0))], out_specs=pl.BlockSpec((N_DEV, tr, cs), lambda i: (0, i, 0)),
        out_shape=jax.ShapeDtypeStruct((N_DEV, r, cs), g.dtype), compiler_params=_cparams(dimension_semantics=("arbitrary",)),
    )(g)


def _adam(w, g, m, v):
    m = ADAM_B1 * m + (1.0 - ADAM_B1) * g
    v = ADAM_B2 * v + (1.0 - ADAM_B2) * (g * g)
    m_hat = m / (1.0 - ADAM_B1 ** ADAM_STEP)
    v_hat = v / (1.0 - ADAM_B2 ** ADAM_STEP)
    return -ADAM_LR * (m_hat / (jnp.sqrt(v_hat) + ADAM_EPS) + ADAM_WD * w), m, v


def _sum_devices(gathered):
    m_all, n = gathered.shape
    m_per = m_all // N_DEV

    def body(x_ref, o_ref):
        s = x_ref[0:m_per, :]
        for j in range(1, N_DEV):
            s = s + x_ref[j * m_per:(j + 1) * m_per, :]
        o_ref[...] = s

    return pl.pallas_call(body, name="sum_devices", out_shape=_sds((m_per, n)), compiler_params=_cparams())(gathered)


def _adam_small(w, g, m, v):
    def body(w_ref, g_ref, m_ref, v_ref, d_ref, mo_ref, vo_ref):
        d_ref[...], mo_ref[...], vo_ref[...] = _adam(w_ref[...], g_ref[...], m_ref[...], v_ref[...])

    return pl.pallas_call(body, name="adam_small", out_shape=[_sds(w.shape)] * 3, compiler_params=_cparams())(w, g, m, v)


N_LAYERS = 4
_MIXER = ("dn", "sb", "mla")
_MIXER_PARAMS = {
    "dn": ("dn_w_in", "dn_conv_w", "dn_a_log", "dn_dt_bias", "dn_out_norm", "dn_w_out"),
    "sb": ("sb_w_qkv", "sb_q_norm", "sb_k_norm", "sb_w_out"),
    "mla": ("mla_w_down", "mla_q_a_norm", "mla_kv_a_norm", "mla_w_uq", "mla_w_ukv", "mla_q_nope_norm", "mla_q_rope_norm",
            "mla_k_nope_norm", "mla_k_rope_norm", "mla_w_out"),
}
_BIG_AXIS = {"dn_w_in": 1, "dn_w_out": 0, "sb_w_qkv": 1, "sb_w_out": 0, "mla_w_down": 0, "mla_w_uq": 1, "mla_w_ukv": 1,
             "mla_w_out": 0, "ffn_w_gate_up": 1, "ffn_w_down": 0}


def _weight_names():
    names = []
    for i in range(N_LAYERS):
        p = "l%d_" % i
        names += [p + "mix_norm"] + [p + n for n in _MIXER_PARAMS[_MIXER[i % 3]]] + [p + "ffn_norm", p + "ffn_w_gate_up", p + "ffn_w_down"]
    return names


WEIGHTS = _weight_names()
BIG = [n for n in WEIGHTS if n[3:] in _BIG_AXIS]
SMALL = [n for n in WEIGHTS if n[3:] not in _BIG_AXIS]
CONV = [n for n in SMALL if n.endswith("conv_w")]


def _ceil_to(n, k):
    return -(-n // k) * k


def _pack(arrs, cols, row_mult):
    parts = []
    for a in arrs:
        f = a.reshape(-1)
        parts.append(jnp.pad(f, (0, _ceil_to(f.shape[0], cols) - f.shape[0])))
    flat = jnp.concatenate(parts)
    rows = _ceil_to(flat.shape[0] // cols, row_mult)
    return jnp.pad(flat, (0, rows * cols - flat.shape[0])).reshape(rows, cols)


def _unpack(buf, shapes):
    cols = buf.shape[-1]
    out, r0 = [], 0
    for s in shapes:
        n = math.prod(s)
        nr = _ceil_to(n, cols) // cols
        out.append(buf[r0:r0 + nr].reshape(-1)[:n].reshape(s))
        r0 += nr
    return out


def _layer_groups(i):
    by = {"gu": (704, []), "down": (1024, []), "out": (1024, []), "dn_in": (514, []), "sb_qkv": (384, []), "mla": (512, [])}
    key = {"ffn_w_gate_up": "gu", "ffn_w_down": "down", "dn_w_in": "dn_in", "sb_w_qkv": "sb_qkv", "mla_w_down": "mla",
           "mla_w_uq": "mla", "mla_w_ukv": "mla"}
    for n in BIG:
        if n.startswith("l%d_" % i):
            by[key.get(n[3:], "out")][1].append(n)
    return [g for g in by.values() if g[1]]


LAYER_GROUPS = [_layer_groups(i) for i in range(N_LAYERS)]
N_FFN_GROUPS = 2


def _stack_group(grp, get):
    width, names = grp
    return jnp.concatenate([jnp.pad(get(n), ((0, 0), (0, width - get(n).shape[1]))) for n in names], axis=0)


def _unstack_group(grp, buf, shape_of):
    out, r0 = [], 0
    for n in grp[1]:
        rs, cs = shape_of(n)
        out.append(buf[..., r0:r0 + rs, :cs])
        r0 += rs
    return out


def kernel(x, l0_mix_norm, l0_dn_w_in, l0_dn_conv_w, l0_dn_a_log, l0_dn_dt_bias, l0_dn_out_norm, l0_dn_w_out, l0_ffn_norm, l0_ffn_w_gate_up, l0_ffn_w_down, l1_mix_norm, l1_sb_w_qkv, l1_sb_q_norm, l1_sb_k_norm, l1_sb_w_out, l1_ffn_norm, l1_ffn_w_gate_up, l1_ffn_w_down, l2_mix_norm, l2_mla_w_down, l2_mla_q_a_norm, l2_mla_kv_a_norm, l2_mla_w_uq, l2_mla_w_ukv, l2_mla_q_nope_norm, l2_mla_q_rope_norm, l2_mla_k_nope_norm, l2_mla_k_rope_norm, l2_mla_w_out, l2_ffn_norm, l2_ffn_w_gate_up, l2_ffn_w_down, l3_mix_norm, l3_dn_w_in, l3_dn_conv_w, l3_dn_a_log, l3_dn_dt_bias, l3_dn_out_norm, l3_dn_w_out, l3_ffn_norm, l3_ffn_w_gate_up, l3_ffn_w_down, loss_target, m_l0_mix_norm, m_l0_dn_w_in, m_l0_dn_conv_w, m_l0_dn_a_log, m_l0_dn_dt_bias, m_l0_dn_out_norm, m_l0_dn_w_out, m_l0_ffn_norm, m_l0_ffn_w_gate_up, m_l0_ffn_w_down, m_l1_mix_norm, m_l1_sb_w_qkv, m_l1_sb_q_norm, m_l1_sb_k_norm, m_l1_sb_w_out, m_l1_ffn_norm, m_l1_ffn_w_gate_up, m_l1_ffn_w_down, m_l2_mix_norm, m_l2_mla_w_down, m_l2_mla_q_a_norm, m_l2_mla_kv_a_norm, m_l2_mla_w_uq, m_l2_mla_w_ukv, m_l2_mla_q_nope_norm, m_l2_mla_q_rope_norm, m_l2_mla_k_nope_norm, m_l2_mla_k_rope_norm, m_l2_mla_w_out, m_l2_ffn_norm, m_l2_ffn_w_gate_up, m_l2_ffn_w_down, m_l3_mix_norm, m_l3_dn_w_in, m_l3_dn_conv_w, m_l3_dn_a_log, m_l3_dn_dt_bias, m_l3_dn_out_norm, m_l3_dn_w_out, m_l3_ffn_norm, m_l3_ffn_w_gate_up, m_l3_ffn_w_down, v_l0_mix_norm, v_l0_dn_w_in, v_l0_dn_conv_w, v_l0_dn_a_log, v_l0_dn_dt_bias, v_l0_dn_out_norm, v_l0_dn_w_out, v_l0_ffn_norm, v_l0_ffn_w_gate_up, v_l0_ffn_w_down, v_l1_mix_norm, v_l1_sb_w_qkv, v_l1_sb_q_norm, v_l1_sb_k_norm, v_l1_sb_w_out, v_l1_ffn_norm, v_l1_ffn_w_gate_up, v_l1_ffn_w_down, v_l2_mix_norm, v_l2_mla_w_down, v_l2_mla_q_a_norm, v_l2_mla_kv_a_norm, v_l2_mla_w_uq, v_l2_mla_w_ukv, v_l2_mla_q_nope_norm, v_l2_mla_q_rope_norm, v_l2_mla_k_nope_norm, v_l2_mla_k_rope_norm, v_l2_mla_w_out, v_l2_ffn_norm, v_l2_ffn_w_gate_up, v_l2_ffn_w_down, v_l3_mix_norm, v_l3_dn_w_in, v_l3_dn_conv_w, v_l3_dn_a_log, v_l3_dn_dt_bias, v_l3_dn_out_norm, v_l3_dn_w_out, v_l3_ffn_norm, v_l3_ffn_w_gate_up, v_l3_ffn_w_down):
    a = dict(locals())
    return _train_step(a)


def _train_step(a):
    mx, my, mc = lax.axis_index("x"), lax.axis_index("y"), lax.axis_index("c")
    dev = 4 * mx + 2 * my + mc
    dev_arr = jnp.reshape(dev, (1,)).astype(jnp.int32)
    t, d = a["x"].shape[1], a["x"].shape[2]
    xs = a["x"].reshape(t, d)
    target = a["loss_target"].reshape(t, d)

    full = {}

    def unpack_layer(i, bufs):
        for grp, buf in zip(LAYER_GROUPS[i], bufs):
            for n, shards in zip(grp[1], _unstack_group(grp, buf, lambda n: a[n].shape)):
                kind = n[3:]
                if kind == "ffn_w_gate_up":
                    full[n] = shards
                elif _BIG_AXIS[kind] == 0:
                    full[n] = shards.reshape(N_DEV * shards.shape[1], shards.shape[2])
                else:
                    width = DN_PROJ if kind == "dn_w_in" else N_DEV * shards.shape[2]
                    full[n] = _cols_from_shards(shards, width)

    def local_shards(i):
        return [_stack_group(grp, lambda n: a[n].astype(BF16)) for grp in LAYER_GROUPS[i]]

    conv_pack = _pack([a[n] for n in CONV], LANES, 8)
    *first_layer, conv_all, gathered = _all_gather_groups("gather_weights", local_shards(0) + [conv_pack])
    unpack_layer(0, first_layer)
    for n, parts in zip(CONV, zip(*[_unpack(conv_all[j], [a[n].shape for n in CONV]) for j in range(N_DEV)])):
        full[n] = jnp.concatenate(parts, axis=1)
    after_first = gathered[0, 0].astype(BF16)
    gathers, started = {}, jnp.zeros((), F32)
    for i in range(1, N_LAYERS):
        srcs = [s + after_first for s in local_shards(i)]
        lands = [lax.dynamic_update_index_in_dim(lax.empty((N_DEV,) + s.shape, s.dtype), s, dev, 0) for s in srcs]
        gathers[i] = _push_start("gather_start_l%d" % i, srcs, lands, False)
        started = started + gathers[i][-1][0, 0]

    def vec(n):
        return a[n].reshape(1, -1)

    cos, sin = _rope_tables(t)

    def mixer_args(i):
        p = "l%d_" % i
        kind = _MIXER[i % 3]
        if kind == "dn":
            args = (full[p + "dn_w_in"], full[p + "dn_conv_w"], _pad_lanes(a[p + "dn_a_log"]), _pad_lanes(a[p + "dn_dt_bias"]),
                    vec(p + "dn_out_norm"), full[p + "dn_w_out"])
        elif kind == "sb":
            args = (full[p + "sb_w_qkv"], vec(p + "sb_q_norm"), vec(p + "sb_k_norm"), full[p + "sb_w_out"])
        else:
            w_down, w_uq, w_ukv = _mla_layout(full[p + "mla_w_down"], full[p + "mla_w_uq"], full[p + "mla_w_ukv"])
            args = (w_down, vec(p + "mla_q_a_norm"), vec(p + "mla_kv_a_norm"), w_uq, w_ukv, vec(p + "mla_q_nope_norm"),
                    _pad_lanes(a[p + "mla_q_rope_norm"]), vec(p + "mla_k_nope_norm"), _pad_lanes(a[p + "mla_k_rope_norm"]),
                    full[p + "mla_w_out"], cos, sin)
        return kind, args

    fwd = {"dn": _dn_fwd, "sb": _sb_fwd, "mla": _mla_fwd}
    bwd = {"dn": _dn_bwd, "sb": _sb_bwd, "mla": _mla_bwd}
    saved, layer_args = [], []
    for i in range(N_LAYERS):
        p = "l%d_" % i
        if i > 0:
            unpack_layer(i, _push_wait("gather_wait_l%d" % i, gathers[i], xs, False))
        kind, args = mixer_args(i)
        layer_args.append((kind, args))
        gain = vec(p + "mix_norm") + started if i == 0 else vec(p + "mix_norm")
        h = _rmsnorm_fwd("mix_norm", xs, gain)
        x_mid, sv_mix = fwd[kind](xs, h, *args)
        x_out, sv_ffn = _ffn_fwd(x_mid, vec(p + "ffn_norm"), full[p + "ffn_w_gate_up"], full[p + "ffn_w_down"])
        saved.append((xs, sv_mix, sv_ffn))
        xs = x_out
    dy, loss_part = _loss_head(xs, target)

    grads, big_out = {}, {}

    def grad_shards(n):
        g, (rs, cs) = grads[n], a[n].shape
        if g.ndim == 3:
            return g
        if _BIG_AXIS[n[3:]] == 0:
            return g.reshape(N_DEV, rs, cs)
        return _shards_from_cols(g, cs)

    def push_grads(tag, groups):
        owns, sends = [], []
        for grp in groups:
            parts = [jnp.pad(grad_shards(n), ((0, 0), (0, 0), (0, grp[0] - a[n].shape[1]))) for n in grp[1]]
            g_all = jnp.concatenate(parts, axis=1)
            owns.append(lax.dynamic_index_in_dim(g_all, dev, 0, keepdims=False))
            sends.append(g_all.astype(BF16))
        lands = [lax.empty(s.shape, s.dtype) for s in sends]
        return tag, groups, owns, _push_start("grads_start_" + tag, sends, lands, True)

    def finish_grads(push, after):
        tag, groups, owns, pushed = push
        recvs = _push_wait("grads_wait_" + tag, pushed, after, True)
        for grp, own, recv in zip(groups, owns, recvs):
            packs = [_stack_group(grp, lambda n, pre=pre: a[pre + n]) for pre in ("", "m_", "v_")]
            outs = [_unstack_group(grp, o, lambda n: a[n].shape) for o in _sum_adam_devices(own, recv, dev_arr, *packs)]
            for j, n in enumerate(grp[1]):
                big_out[n] = [o[j] for o in outs]

    mixer_push = None
    for i in reversed(range(N_LAYERS)):
        p = "l%d_" % i
        kind, args = layer_args[i]
        x_in, sv_mix, sv_ffn = saved[i]
        gain = vec(p + "ffn_norm") if mixer_push is None else vec(p + "ffn_norm") + mixer_push[3][-1][0, 0]
        dx_mid, grads[p + "ffn_norm"], grads[p + "ffn_w_gate_up"], grads[p + "ffn_w_down"] = _ffn_bwd(
            sv_ffn, dy, gain, full[p + "ffn_w_gate_up"], full[p + "ffn_w_down"])
        ffn_push = push_grads("l%d_ffn" % i, LAYER_GROUPS[i][:N_FFN_GROUPS])
        if mixer_push is not None:
            finish_grads(mixer_push, dx_mid)
        res = bwd[kind](sv_mix, dx_mid, *args)
        dh = res[0]
        if kind == "mla":
            res = list(res)
            res[1], res[4], res[5] = _mla_unlayout(res[1], res[4], res[5])
        for n, g in zip(_MIXER_PARAMS[kind], res[1:]):
            grads[p + n] = g
        dy, grads[p + "mix_norm"] = _rmsnorm_bwd("mix_norm_bwd", x_in, vec(p + "mix_norm") + ffn_push[3][-1][0, 0], dh, dx_mid)
        mixer_push = push_grads("l%d_mix" % i, LAYER_GROUPS[i][N_FFN_GROUPS:])
        finish_grads(ffn_push, dy)
    grad_x = dy.reshape(a["x"].shape)

    small_full_shapes = [full[n].shape if n in CONV else a[n].shape for n in SMALL]
    small_grads = []
    for n, s in zip(SMALL, small_full_shapes):
        g = grads[n].reshape(-1)
        small_grads.append(g[:math.prod(s)])
    small_pack = _pack(small_grads + [loss_part.reshape(-1)], LANES, 8)
    small_sum = _sum_devices(_all_gather("gather_small_grads", small_pack))
    small_red = _unpack(small_sum, small_full_shapes + [(LANES,)])
    loss = small_red[-1][0]
    g_small = {}
    for n, g in zip(SMALL, small_red[:-1]):
        if n in CONV:
            cs = a[n].shape[1]
            g = lax.dynamic_slice_in_dim(g, dev * cs, cs, axis=1)
        g_small[n] = g
    small_shapes = [a[n].shape for n in SMALL]
    packs = [_pack([src[n] for n in SMALL], LANES, 8) for src in
             ({n: a[n] for n in SMALL}, g_small, {n: a["m_" + n] for n in SMALL}, {n: a["v_" + n] for n in SMALL})]
    d_small, m_small, v_small = (_unpack(o, small_shapes) for o in _adam_small(*packs))
    finish_grads(mixer_push, small_sum)

    small_out = dict(zip(SMALL, zip([g_small[n] for n in SMALL], d_small, m_small, v_small)))

    def out(k):
        return [small_out[n][k] if n in small_out else big_out[n][k] for n in WEIGHTS]

    return (loss, grad_x, *out(0), *out(1), *out(2), *out(3))
```

```python
import math

import jax
import jax.numpy as jnp
from jax import lax
from jax.experimental import pallas as pl
from jax.experimental.pallas import tpu as pltpu

F32 = jnp.float32
BF16 = jnp.bfloat16
F32X3 = lax.Precision.HIGH

LANES = 128
N_DEV = 8
N_HEADS = 8
HEAD = 128
NORM_EPS = 1e-6
DN_CHUNK = 64
ATT_BLOCK = 512
ATT_Q = 512
MLA_ROPE = 64
MLA_QK = 192
ROPE_THETA = 10000.0
VMEM_LIMIT = 56 * 1024 * 1024
MM_TILE_BYTES = 32 * 1024 * 1024

ADAM_LR = 0.001
ADAM_B1 = 0.9
ADAM_B2 = 0.999
ADAM_EPS = 1e-08
ADAM_WD = 0.01
ADAM_STEP = 10


def _cparams(**kw):
    return pltpu.CompilerParams(vmem_limit_bytes=VMEM_LIMIT, **kw)


def _pick(n, cands):
    for c in cands:
        if c <= n and n % c == 0:
            return c
    return n


def _mm(name, a, b, *, ta=False, tb=False, out_dtype=F32, add=None, tm=None, tn=None, tk=None):
    if ta:
        K, M = a.shape
    else:
        M, K = a.shape
    N = b.shape[0] if tb else b.shape[1]
    tm = tm or _pick(M, (1024, 512, 256, 128))
    tn = tn or _pick(N, (1024, 512, 384, 256, 128))
    if tk is None:
        fits = [c for c in (4096, 2048, 1408, 1024, 512, 384, 256, 128)
                if c <= K and K % c == 0 and 2 * c * (tm * a.dtype.itemsize + tn * b.dtype.itemsize) <= MM_TILE_BYTES]
        tk = fits[0] if fits else K
    return _mm_raw(
        name, a, b, ta=ta, tb=tb, out_dtype=out_dtype, add=add, grid=(M // tm, N // tn, K // tk), out_shape=(M, N),
        a_block=(tk, tm) if ta else (tm, tk), a_map=(lambda i, j, k: (k, i)) if ta else (lambda i, j, k: (i, k)),
        b_block=(tn, tk) if tb else (tk, tn), b_map=(lambda i, j, k: (j, k)) if tb else (lambda i, j, k: (k, j)),
        o_block=(tm, tn), o_map=lambda i, j, k: (i, j))


def _mm_raw(name, a, b, *, ta, tb, out_dtype, add, grid, out_shape, a_block, a_map, b_block, b_map, o_block, o_map):
    nk = grid[2]
    tm, tn = o_block
    dn = (((0 if ta else 1,), (1 if tb else 0,)), ((), ()))
    has_add = add is not None

    def kern(*refs):
        if has_add:
            a_ref, b_ref, add_ref, o_ref, acc_ref = refs
        else:
            a_ref, b_ref, o_ref, acc_ref = refs
        k = pl.program_id(2)
        part = lax.dot_general(a_ref[...].astype(BF16), b_ref[...].astype(BF16), dn, preferred_element_type=F32)

        @pl.when(k == 0)
        def _():
            acc_ref[...] = part

        @pl.when(k > 0)
        def _():
            acc_ref[...] += part

        @pl.when(k == nk - 1)
        def _():
            r = acc_ref[...]
            if has_add:
                r = r + add_ref[...]
            o_ref[...] = r.astype(out_dtype)

    in_specs = [pl.BlockSpec(a_block, a_map), pl.BlockSpec(b_block, b_map)]
    args = [a, b]
    if has_add:
        in_specs.append(pl.BlockSpec(o_block, o_map))
        args.append(add)
    return pl.pallas_call(
        kern, name=name,
        grid=grid,
        in_specs=in_specs,
        out_specs=pl.BlockSpec(o_block, o_map),
        out_shape=jax.ShapeDtypeStruct(out_shape, out_dtype),
        scratch_shapes=[pltpu.VMEM((tm, tn), F32)],
        compiler_params=_cparams(dimension_semantics=("parallel", "parallel", "arbitrary")),
    )(*args)


def _rows(name, body, ins, outs, *, tt, consts=(), accs=()):
    in_specs, args = [], []
    first = ins[0][0] if isinstance(ins[0], tuple) else ins[0]
    t = first.shape[-2]
    tt = min(tt, t)
    for x in ins:
        if isinstance(x, tuple):
            arr, bs, im = x
            in_specs.append(pl.BlockSpec(bs, im))
            args.append(arr)
        else:
            in_specs.append(_row_spec(x.shape, tt))
            args.append(x)
    for c in consts:
        in_specs.append(pl.BlockSpec(c.shape, lambda i, _n=c.ndim: (0,) * _n))
        args.append(c)
    out_specs = [_row_spec(o.shape, tt) for o in outs]
    out_specs += [pl.BlockSpec(a.shape, lambda i, _n=len(a.shape): (0,) * _n) for a in accs]
    res = pl.pallas_call(
        body, name=name, grid=(t // tt,),
        in_specs=in_specs, out_specs=out_specs, out_shape=list(outs) + list(accs),
        compiler_params=_cparams(dimension_semantics=("arbitrary",)),
    )(*args)
    return res


def _row_spec(shape, tt):
    if len(shape) == 2:
        return pl.BlockSpec((tt, shape[1]), lambda i: (i, 0))
    return pl.BlockSpec((shape[0], tt, shape[2]), lambda i: (0, i, 0))


def _sds(shape, dtype=F32):
    return jax.ShapeDtypeStruct(tuple(shape), dtype)


def _acc(ref, val):
    i = pl.program_id(0)

    @pl.when(i == 0)
    def _():
        ref[...] = val

    @pl.when(i > 0)
    def _():
        ref[...] += val


def _rms(x, g):
    return x * lax.rsqrt(jnp.mean(x * x, axis=-1, keepdims=True) + NORM_EPS) * g


def _silu(x):
    return x / (1.0 + jnp.exp(-x))


def _softplus(x):
    return jnp.maximum(x, 0.0) + jnp.log(1.0 + jnp.exp(-jnp.abs(x)))


def _sigmoid(x):
    return 1.0 / (1.0 + jnp.exp(-x))


def _rmsnorm_fwd(name, x, g, tt=512):
    def body(x_ref, g_ref, h_ref):
        h_ref[...] = _rms(x_ref[...], g_ref[...]).astype(BF16)

    return _rows(name, body, [x], [_sds(x.shape, BF16)], tt=tt, consts=[g])[0]


def _rmsnorm_bwd(name, x, g, dh, dres, tt=512):
    def body(x_ref, dh_ref, dres_ref, g_ref, dx_ref, dg_ref):
        _, vjp = jax.vjp(_rms, x_ref[...], g_ref[...])
        dx, dg = vjp(dh_ref[...])
        dx_ref[...] = dx + dres_ref[...]
        _acc(dg_ref, dg)

    return _rows(name, body, [x, dh, dres], [_sds(x.shape)], tt=tt, consts=[g], accs=[_sds(g.shape)])


def _ffn_fwd(x, norm_g, w3, w_down):
    t, d = x.shape
    ns, _, cs = w3.shape
    half = ns // 2
    w2 = w3.reshape(ns * d, cs)
    h = _rmsnorm_fwd("ffn_norm", x, norm_g)
    tm = _pick(t, (1024, 512, 256, 128))
    nm = t // tm

    def gate_up(h_ref, wg_ref, wu_ref, g_ref, u_ref, a_ref):
        hv = h_ref[...]
        g = _dot(hv, wg_ref[...])
        u = _dot(hv, wu_ref[...])
        g_ref[...] = g
        u_ref[...] = u
        a_ref[...] = (_silu(g) * u).astype(BF16)

    hid = pl.BlockSpec((tm, cs), lambda j, i: (j * nm + i, 0))
    g, u, act = pl.pallas_call(
        gate_up, name="ffn_gate_up", grid=(half, nm),
        in_specs=[pl.BlockSpec((tm, d), lambda j, i: (i, 0)), pl.BlockSpec((d, cs), lambda j, i: (j, 0)),
                  pl.BlockSpec((d, cs), lambda j, i: (j + half, 0))],
        out_specs=[hid, hid, hid], out_shape=[_sds((half * t, cs)), _sds((half * t, cs)), _sds((half * t, cs), BF16)],
        compiler_params=_cparams(dimension_semantics=("parallel", "arbitrary")),
    )(h, w2, w2)
    y = _mm_raw("ffn_down", act, w_down, ta=False, tb=False, out_dtype=F32, add=x, grid=(nm, 1, half), out_shape=(t, d),
                a_block=(tm, cs), a_map=lambda i, j, k: (k * nm + i, 0), b_block=(cs, d), b_map=lambda i, j, k: (k, 0),
                o_block=(tm, d), o_map=lambda i, j, k: (i, 0))
    return y, (x, h, g, u, act)


def _ffn_bwd(saved, dy, norm_g, w3, w_down):
    x, h, g, u, act = saved
    t, d = x.shape
    ns, _, cs = w3.shape
    half = ns // 2
    w2 = w3.reshape(ns * d, cs)
    tm = _pick(t, (1024, 512, 256, 128))
    nm = t // tm
    tk = _pick(t, (4096, 2048, 1024, 512, 256, 128))
    nk = t // tk
    d_wdown = _mm_raw("ffn_down_wgrad", act, dy, ta=True, tb=False, out_dtype=F32, add=None, grid=(half, 1, nk),
                      out_shape=(half * cs, d), a_block=(tk, cs), a_map=lambda i, j, k: (i * nk + k, 0),
                      b_block=(tk, d), b_map=lambda i, j, k: (k, 0), o_block=(cs, d), o_map=lambda i, j, k: (i, 0))
    def down_dgrad(dy_ref, wd_ref, g_ref, u_ref, dg_ref, du_ref):
        da = _dot_nt(dy_ref[...].astype(BF16), wd_ref[...])
        gv, uv = g_ref[...], u_ref[...]
        s = _sigmoid(gv)
        dg_ref[...] = (da * uv * s * (1.0 + gv * (1.0 - s))).astype(BF16)
        du_ref[...] = (da * gv * s).astype(BF16)

    hid = pl.BlockSpec((tm, cs), lambda j, i: (j * nm + i, 0))
    dg, du = pl.pallas_call(
        down_dgrad, name="ffn_down_dgrad", grid=(half, nm),
        in_specs=[pl.BlockSpec((tm, d), lambda j, i: (i, 0)), pl.BlockSpec((cs, d), lambda j, i: (j, 0)), hid, hid],
        out_specs=[hid, hid], out_shape=[_sds((half * t, cs), BF16), _sds((half * t, cs), BF16)],
        compiler_params=_cparams(dimension_semantics=("parallel", "arbitrary")),
    )(dy, w_down, g, u)

    def wgrad(name, dd):
        return _mm_raw(name, h, dd, ta=True, tb=False, out_dtype=F32, add=None, grid=(1, half, nk), out_shape=(half * d, cs),
                       a_block=(tk, d), a_map=lambda i, j, k: (k, 0), b_block=(tk, cs), b_map=lambda i, j, k: (j * nk + k, 0),
                       o_block=(d, cs), o_map=lambda i, j, k: (j, 0))

    def gate_up_dgrad(dg_ref, du_ref, wg_ref, wu_ref, dh_ref, acc_ref):
        j = pl.program_id(1)
        part = _dot_nt(dg_ref[...], wg_ref[...]) + _dot_nt(du_ref[...], wu_ref[...])

        @pl.when(j == 0)
        def _():
            acc_ref[...] = part

        @pl.when(j > 0)
        def _():
            acc_ref[...] += part

        @pl.when(j == half - 1)
        def _():
            dh_ref[...] = acc_ref[...]

    hid = pl.BlockSpec((tm, cs), lambda i, j: (j * nm + i, 0))
    dh = pl.pallas_call(
        gate_up_dgrad, name="ffn_gate_up_dgrad", grid=(nm, half),
        in_specs=[hid, hid, pl.BlockSpec((d, cs), lambda i, j: (j, 0)), pl.BlockSpec((d, cs), lambda i, j: (j + half, 0))],
        out_specs=pl.BlockSpec((tm, d), lambda i, j: (i, 0)), out_shape=_sds((t, d)),
        scratch_shapes=[pltpu.VMEM((tm, d), F32)],
        compiler_params=_cparams(dimension_semantics=("parallel", "arbitrary")),
    )(dg, du, w2, w2)
    d_w3 = jnp.concatenate([wgrad("ffn_gate_wgrad", dg), wgrad("ffn_up_wgrad", du)], axis=0).reshape(ns, d, cs)
    dx, dgain = _rmsnorm_bwd("ffn_norm_bwd", x, norm_g, dh, dy)
    return dx, dgain, d_w3, d_wdown


def _dot_nt(a, b):
    return lax.dot_general(a, b, (((1,), (1,)), ((), ())), preferred_element_type=F32)


def _dot_tn(a, b):
    return lax.dot_general(a, b, (((0,), (0,)), ((), ())), preferred_element_type=F32)


def _dot(a, b):
    return jnp.dot(a, b, preferred_element_type=F32)


CUM_BLOCK = 128


def _tri2(lower):
    r = lax.broadcasted_iota(jnp.int32, (CUM_BLOCK, CUM_BLOCK), 0)
    c = lax.broadcasted_iota(jnp.int32, (CUM_BLOCK, CUM_BLOCK), 1)
    tri = ((r > c) if lower else (r < c)).astype(BF16)
    return jnp.concatenate([tri, tri], axis=0)


def _run_sums(x, tri2, run, reverse):
    nb = x.shape[1] // CUM_BLOCK
    outs = [None] * nb
    for j in (reversed(range(nb)) if reverse else range(nb)):
        xj = x[:, j * CUM_BLOCK:(j + 1) * CUM_BLOCK]
        hi = xj.astype(BF16)
        lo = (xj - hi.astype(F32)).astype(BF16)
        outs[j] = _dot(jnp.concatenate([hi, lo], axis=1), tri2) + run
        run = run + jnp.sum(xj, axis=1, keepdims=True)
    return jnp.concatenate(outs, axis=1), run


def _log_sigmoid(z):
    return jnp.minimum(z, 0.0) - jnp.log(1.0 + jnp.exp(-jnp.abs(z)))


def _heads_in(ref, h, width=HEAD):
    return ref[:, h * width:(h + 1) * width]


def _sb_qk(q, k, gq, gk):
    return _rms(q, gq) * (HEAD ** -0.5), _rms(k, gk)


def _sb_prep_fwd(qkv, gq, gk):
    t = qkv.shape[0]

    def body(x_ref, gq_ref, gk_ref, q_ref, k_ref, v_ref):
        for h in range(N_HEADS):
            q, k = _sb_qk(_heads_in(x_ref, h), _heads_in(x_ref, N_HEADS + h), gq_ref[...], gk_ref[...])
            q_ref[h] = q.astype(BF16)
            k_ref[h] = k.astype(BF16)
            v_ref[h] = _heads_in(x_ref, 2 * N_HEADS + h).astype(BF16)

    hm = _sds((N_HEADS, t, HEAD), BF16)
    return _rows("sb_prep", body, [qkv], [hm, hm, hm], tt=256, consts=[gq, gk])


def _sb_prep_bwd(qkv, gq, gk, dq, dk, dv):
    def body(x_ref, dq_ref, dk_ref, dv_ref, gq_ref, gk_ref, dx_ref, dgq_ref, dgk_ref):
        dgq = jnp.zeros(gq_ref.shape, F32)
        dgk = jnp.zeros(gk_ref.shape, F32)
        for h in range(N_HEADS):
            _, vjp = jax.vjp(_sb_qk, _heads_in(x_ref, h), _heads_in(x_ref, N_HEADS + h), gq_ref[...], gk_ref[...])
            a, b, c, d = vjp((dq_ref[h], dk_ref[h]))
            dx_ref[:, h * HEAD:(h + 1) * HEAD] = a.astype(BF16)
            dx_ref[:, (N_HEADS + h) * HEAD:(N_HEADS + h + 1) * HEAD] = b.astype(BF16)
            dx_ref[:, (2 * N_HEADS + h) * HEAD:(2 * N_HEADS + h + 1) * HEAD] = dv_ref[h].astype(BF16)
            dgq, dgk = dgq + c, dgk + d
        _acc(dgq_ref, dgq)
        _acc(dgk_ref, dgk)

    return _rows("sb_prep_bwd", body, [qkv, dq, dk, dv], [_sds(qkv.shape, BF16)], tt=256, consts=[gq, gk],
                 accs=[_sds(gq.shape), _sds(gk.shape)])


def _q_block(t):
    return min(ATT_Q, t)


def _key_order(bq, qb):
    rows = lax.broadcasted_iota(jnp.int32, (bq, ATT_BLOCK), 0)
    cols = lax.broadcasted_iota(jnp.int32, (bq, ATT_BLOCK), 1)
    return rows - cols + qb * bq


def _sb_attn_fwd(q, k, v):
    nh, t, _ = q.shape
    bq = _q_block(t)
    per = bq // ATT_BLOCK

    def kern(q_ref, k_ref, v_ref, o_ref):
        qb = pl.program_id(1)
        qv = q_ref[0]
        after = _tri2(True)
        order = _key_order(bq, qb)
        nkb = (qb + 1) * per

        def body(i, carry, diagonal):
            o_acc, run = carry
            kb = nkb - 1 - i
            off = pl.multiple_of(kb * ATT_BLOCK, ATT_BLOCK)
            kv = k_ref[0, pl.ds(off, ATT_BLOCK), :]
            vv = v_ref[0, pl.ds(off, ATT_BLOCK), :]
            z = _dot_nt(qv, kv)
            lsz = _log_sigmoid(z)
            lsn = lsz - z
            if diagonal:
                past = order > kb * ATT_BLOCK
                lsn = jnp.where(past, lsn, 0.0)
            la, run = _run_sums(lsn, after, run, True)
            a = jnp.exp(lsz + la)
            if diagonal:
                a = jnp.where(past, a, 0.0)
            o_acc = o_acc + _dot(a.astype(BF16), vv)
            return o_acc, run

        carry = lax.fori_loop(0, per, lambda i, c: body(i, c, True), (jnp.zeros((bq, HEAD), F32), jnp.zeros((bq, 1), F32)))
        o, _ = lax.fori_loop(per, nkb, lambda i, c: body(i, c, False), carry)
        o_ref[...] = o

    return pl.pallas_call(
        kern, name="sb_attn_fwd", grid=(nh, t // bq),
        in_specs=[pl.BlockSpec((1, bq, HEAD), lambda h, i: (h, i, 0)),
                  pl.BlockSpec((1, t, HEAD), lambda h, i: (h, 0, 0)),
                  pl.BlockSpec((1, t, HEAD), lambda h, i: (h, 0, 0))],
        out_specs=pl.BlockSpec((bq, HEAD), lambda h, i: (i, h)),
        out_shape=_sds((t, nh * HEAD)),
        compiler_params=_cparams(dimension_semantics=("parallel", "arbitrary")),
    )(q, k, v)


def _sb_attn_bwd(q, k, v, do):
    nh, t, _ = q.shape
    bq = _q_block(t)
    per = bq // ATT_BLOCK

    def kern(q_ref, k_ref, v_ref, do_ref, dq_ref, dk_ref, dv_ref, g_s, ls_s):
        qb = pl.program_id(1)

        @pl.when(qb == 0)
        def _():
            dk_ref[...] = jnp.zeros(dk_ref.shape, F32)
            dv_ref[...] = jnp.zeros(dv_ref.shape, F32)

        qv = q_ref[0]
        dob = do_ref[...].astype(BF16)
        after, before = _tri2(True), _tri2(False)
        order = _key_order(bq, qb)
        nkb = (qb + 1) * per

        def sweep_left(i, run, diagonal):
            kb = nkb - 1 - i
            off = pl.multiple_of(kb * ATT_BLOCK, ATT_BLOCK)
            kv = k_ref[0, pl.ds(off, ATT_BLOCK), :]
            vv = v_ref[0, pl.ds(off, ATT_BLOCK), :]
            z = _dot_nt(qv, kv)
            lsz = _log_sigmoid(z)
            lsn = lsz - z
            if diagonal:
                past = order > kb * ATT_BLOCK
                lsn = jnp.where(past, lsn, 0.0)
            la, run = _run_sums(lsn, after, run, True)
            a = jnp.exp(lsz + la)
            if diagonal:
                a = jnp.where(past, a, 0.0)
            g_s[kb] = _dot_nt(dob, vv) * a
            ls_s[kb] = lsz
            dv_ref[0, pl.ds(off, ATT_BLOCK), :] += _dot_tn(a.astype(BF16), dob)
            return run

        zero = jnp.zeros((bq, 1), F32)
        run = lax.fori_loop(0, per, lambda i, c: sweep_left(i, c, True), zero)
        lax.fori_loop(per, nkb, lambda i, c: sweep_left(i, c, False), run)

        def sweep_right(kb, carry, diagonal):
            dq_acc, run_g = carry
            off = pl.multiple_of(kb * ATT_BLOCK, ATT_BLOCK)
            kv = k_ref[0, pl.ds(off, ATT_BLOCK), :]
            g = g_s[kb]
            sg = jnp.exp(ls_s[kb])
            dls, run_g = _run_sums(g, before, run_g, False)
            dz = g * (1.0 - sg) - dls * sg
            if diagonal:
                dz = jnp.where(order > kb * ATT_BLOCK, dz, 0.0)
            dzb = dz.astype(BF16)
            dk_ref[0, pl.ds(off, ATT_BLOCK), :] += _dot_tn(dzb, qv)
            return dq_acc + _dot(dzb, kv), run_g

        carry = lax.fori_loop(0, nkb - per, lambda i, c: sweep_right(i, c, False), (jnp.zeros((bq, HEAD), F32), zero))
        dq, _ = lax.fori_loop(nkb - per, nkb, lambda i, c: sweep_right(i, c, True), carry)
        dq_ref[0] = dq

    hm = _sds((nh, t, HEAD))
    full = pl.BlockSpec((1, t, HEAD), lambda h, i: (h, 0, 0))
    tok = pl.BlockSpec((bq, HEAD), lambda h, i: (i, h))
    nkb_max = t // ATT_BLOCK
    return pl.pallas_call(
        kern, name="sb_attn_bwd", grid=(nh, t // bq),
        in_specs=[pl.BlockSpec((1, bq, HEAD), lambda h, i: (h, i, 0)), full, full, tok],
        out_specs=[pl.BlockSpec((1, bq, HEAD), lambda h, i: (h, i, 0)), full, full],
        out_shape=[hm, hm, hm],
        scratch_shapes=[pltpu.VMEM((nkb_max, bq, ATT_BLOCK), F32), pltpu.VMEM((nkb_max, bq, ATT_BLOCK), F32)],
        compiler_params=_cparams(dimension_semantics=("parallel", "arbitrary")),
    )(q, k, v, do)


def _sb_fwd(x, h, w_qkv, gq, gk, w_out):
    qkv = _mm("sb_qkv", h, w_qkv)
    q, k, v = _sb_prep_fwd(qkv, gq, gk)
    o = _sb_attn_fwd(q, k, v)
    y = _mm("sb_out", o, w_out, add=x)
    return y, (h, qkv, q, k, v, o)


def _sb_bwd(saved, dy, w_qkv, gq, gk, w_out):
    h, qkv, q, k, v, o = saved
    d_wout = _mm("sb_out_wgrad", o, dy, ta=True)
    do = _mm("sb_out_dgrad", dy, w_out, tb=True)
    dq, dk, dv = _sb_attn_bwd(q, k, v, do)
    dqkv, dgq, dgk = _sb_prep_bwd(qkv, gq, gk, dq, dk, dv)
    d_wqkv = _mm("sb_qkv_wgrad", h, dqkv, ta=True)
    dh = _mm("sb_qkv_dgrad", dqkv, w_qkv, tb=True)
    return dh, d_wqkv, dgq, dgk, d_wout


DN_QKV = 3 * N_HEADS * HEAD
DN_PROJ = DN_QKV + N_HEADS * HEAD + LANES
DN_CONV = 4
HALO = 8
CONV_COLS = 512


def _dn_conv_fwd(proj, conv_w, tt=256):
    t = proj.shape[0]
    tt = min(tt, t)

    def body(u_ref, prev_ref, w_ref, c_ref):
        i = pl.program_id(0)
        for cc in range(DN_QKV // CONV_COLS):
            cs = slice(cc * CONV_COLS, (cc + 1) * CONV_COLS)
            cur = u_ref[:, cs]
            prev = jnp.where(i > 0, prev_ref[:, cs], 0.0)
            ext = jnp.concatenate([prev, cur], axis=0)
            y = cur * w_ref[DN_CONV - 1:DN_CONV, cs]
            for j in range(DN_CONV - 1):
                y = y + pltpu.roll(ext, DN_CONV - 1 - j, 0)[HALO:] * w_ref[j:j + 1, cs]
            c_ref[:, cs] = y

    return _rows("dn_conv", body,
                 [(proj, (tt, DN_QKV), lambda i: (i, 0)),
                  (proj, (HALO, DN_QKV), lambda i: (jnp.maximum(i * (tt // HALO) - 1, 0), 0))],
                 [_sds((t, DN_QKV))], tt=tt, consts=[conv_w])[0]


def _dn_conv_bwd(proj, conv_w, dc, dz, dab, tt=256):
    t = proj.shape[0]
    tt = min(tt, t)
    nblk = t // tt

    def body(u_ref, prev_ref, dc_ref, next_ref, dz_ref, dab_ref, w_ref, dp_ref, dw_ref):
        i = pl.program_id(0)
        dws = []
        for cc in range(DN_QKV // CONV_COLS):
            cs = slice(cc * CONV_COLS, (cc + 1) * CONV_COLS)
            cur = u_ref[:, cs]
            prev = jnp.where(i > 0, prev_ref[:, cs], 0.0)
            ext_u = jnp.concatenate([prev, cur], axis=0)
            d = dc_ref[:, cs]
            nxt = jnp.where(i < nblk - 1, next_ref[:, cs], 0.0)
            ext_d = jnp.concatenate([d, nxt], axis=0)
            du = d * w_ref[DN_CONV - 1:DN_CONV, cs]
            rows = [jnp.sum(d * cur, axis=0, keepdims=True)]
            for j in range(DN_CONV - 2, -1, -1):
                sh = DN_CONV - 1 - j
                du = du + pltpu.roll(ext_d, tt + HALO - sh, 0)[:tt] * w_ref[j:j + 1, cs]
                rows.insert(0, jnp.sum(d * pltpu.roll(ext_u, sh, 0)[HALO:], axis=0, keepdims=True))
            dp_ref[:, cs] = du.astype(BF16)
            dws.append(jnp.concatenate(rows, axis=0))
        dp_ref[:, DN_QKV:DN_QKV + N_HEADS * HEAD] = dz_ref[...].astype(BF16)
        dp_ref[:, DN_QKV + N_HEADS * HEAD:] = dab_ref[...].astype(BF16)
        _acc(dw_ref, jnp.concatenate(dws, axis=1))

    return _rows("dn_conv_bwd", body,
                 [(proj, (tt, DN_QKV), lambda i: (i, 0)),
                  (proj, (HALO, DN_QKV), lambda i: (jnp.maximum(i * (tt // HALO) - 1, 0), 0)),
                  dc,
                  (dc, (HALO, DN_QKV), lambda i: (jnp.minimum((i + 1) * (tt // HALO), t // HALO - 1), 0)),
                  dz, dab],
                 [_sds((t, DN_PROJ), BF16)], tt=tt, consts=[conv_w], accs=[_sds(conv_w.shape)])


def _l2n(x):
    return x * lax.rsqrt(jnp.sum(x * x, axis=-1, keepdims=True) + NORM_EPS)


def _dn_qkv(cq, ck, cv):
    return _l2n(_silu(cq)) * (HEAD ** -0.5), _l2n(_silu(ck)), _silu(cv)


def _dn_gates(ab, a_log, dt_bias):
    lane = lax.broadcasted_iota(jnp.int32, ab.shape, 1)
    g = -jnp.exp(a_log) * _softplus(ab + dt_bias)
    return jnp.where(lane < N_HEADS, g, jnp.where(lane < 2 * N_HEADS, _sigmoid(ab), 0.0))


def _ab_spec(tt):
    return (tt, LANES), lambda i: (i, DN_PROJ // LANES - 1)


def _dn_prep_fwd(c, proj, a_log, dt_bias, tt=256):
    t = c.shape[0]
    tt = min(tt, t)

    def body(c_ref, ab_ref, al_ref, dt_ref, q_ref, k_ref, v_ref, g_ref):
        for h in range(N_HEADS):
            q_ref[h], k_ref[h], v_ref[h] = _dn_qkv(_heads_in(c_ref, h), _heads_in(c_ref, N_HEADS + h), _heads_in(c_ref, 2 * N_HEADS + h))
        g_ref[...] = _dn_gates(ab_ref[...], al_ref[...], dt_ref[...])

    hm = _sds((N_HEADS, t, HEAD))
    return _rows("dn_prep", body, [c, (proj,) + _ab_spec(tt)], [hm, hm, hm, _sds((t, LANES))], tt=tt, consts=[a_log, dt_bias])


def _dn_prep_bwd(c, proj, a_log, dt_bias, dq, dk, dv, dgates, tt=256):
    t = c.shape[0]
    tt = min(tt, t)

    def body(c_ref, ab_ref, dq_ref, dk_ref, dv_ref, dg_ref, al_ref, dt_ref, dc_ref, dab_ref, dal_ref, ddt_ref):
        for h in range(N_HEADS):
            _, vjp = jax.vjp(_dn_qkv, _heads_in(c_ref, h), _heads_in(c_ref, N_HEADS + h), _heads_in(c_ref, 2 * N_HEADS + h))
            a, b, d = vjp((dq_ref[h], dk_ref[h], dv_ref[h]))
            dc_ref[:, h * HEAD:(h + 1) * HEAD] = a
            dc_ref[:, (N_HEADS + h) * HEAD:(N_HEADS + h + 1) * HEAD] = b
            dc_ref[:, (2 * N_HEADS + h) * HEAD:(2 * N_HEADS + h + 1) * HEAD] = d
        _, vjp = jax.vjp(_dn_gates, ab_ref[...], al_ref[...], dt_ref[...])
        dab, dal, ddt = vjp(dg_ref[...])
        dab_ref[...] = dab
        _acc(dal_ref, dal)
        _acc(ddt_ref, ddt)

    return _rows("dn_prep_bwd", body, [c, (proj,) + _ab_spec(tt), dq, dk, dv, dgates], [_sds(c.shape), _sds((t, LANES))],
                 tt=tt, consts=[a_log, dt_bias], accs=[_sds(a_log.shape), _sds(dt_bias.shape)])


def _bdot(a, b, prec=None):
    return lax.dot_general(a, b, (((2,), (1,)), ((0,), (0,))), precision=prec, preferred_element_type=F32)


def _bdot_nt(a, b, prec=None):
    return lax.dot_general(a, b, (((2,), (2,)), ((0,), (0,))), precision=prec, preferred_element_type=F32)


def _bdot_tn(a, b, prec=None):
    return lax.dot_general(a, b, (((1,), (1,)), ((0,), (0,))), precision=prec, preferred_element_type=F32)


def _inv_raw(low):
    c = low.shape[-1]
    r = lax.broadcasted_iota(jnp.int32, (c, c), 0)
    s = lax.broadcasted_iota(jnp.int32, (c, c), 1)
    m = jnp.where(r == s, 1.0, 0.0) - low
    p = _bdot(low, low, F32X3)
    n_fac = int(math.log2(c)) - 1
    for i in range(n_fac):
        m = m + _bdot(m, p, F32X3)
        if i < n_fac - 1:
            p = _bdot(p, p, F32X3)
    return m


@jax.custom_vjp
def _inv_unit_lower(low):
    return _inv_raw(low)


def _inv_fwd(low):
    m = _inv_raw(low)
    return m, m


def _inv_bwd(m, dm):
    return (-_bdot_nt(_bdot_tn(m, dm, F32X3), m, F32X3),)


_inv_unit_lower.defvjp(_inv_fwd, _inv_bwd)


def _dn_chunk(q, k, v, gates, s):
    nh, c, _ = q.shape
    lane = lax.broadcasted_iota(jnp.int32, gates.shape, 1)
    def column(j):
        return jnp.sum(jnp.where(lane == j, gates, 0.0), axis=1, keepdims=True)[None]

    g_col = jnp.concatenate([column(h) for h in range(nh)], axis=0)
    b_col = jnp.concatenate([column(h + nh) for h in range(nh)], axis=0)
    r = lax.broadcasted_iota(jnp.int32, (c, c), 0)
    cc = lax.broadcasted_iota(jnp.int32, (c, c), 1)
    causal, strict = r >= cc, r > cc
    incl = jnp.broadcast_to(jnp.where(causal, 1.0, 0.0), (nh, c, c))
    upper = jnp.broadcast_to(jnp.where(r <= cc, 1.0, 0.0), (nh, c, c))
    gb = jnp.broadcast_to(g_col, (nh, c, LANES))
    gbc = jnp.broadcast_to(g_col, (nh, c, c))
    gc = _bdot(incl, gb, F32X3)
    gc_r = _bdot(incl, gbc, F32X3)
    gc_c = _bdot_tn(gbc, upper, F32X3)
    decay = jnp.where(causal, jnp.exp(jnp.where(causal, gc_r - gc_c, 0.0)), 0.0)
    kb = k * b_col
    low = jnp.where(strict, _bdot_nt(kb, k) * decay, 0.0)
    m = _inv_unit_lower(low)
    egc = jnp.exp(gc)
    u = _bdot(m, v * b_col, F32X3)
    w = _bdot(m, kb * egc, F32X3)
    attn = _bdot_nt(q, k) * decay
    gl = jnp.sum(gb, axis=1, keepdims=True)
    v_new = u - _bdot(w, s)
    o = _bdot(q * egc, s) + _bdot(attn, v_new)
    s_new = s * jnp.exp(gl) + _bdot_tn(k * jnp.exp(gl - gc), v_new)
    return o, s_new


def _dn_chunks_fwd(q, k, v, gates):
    nh, t, _ = q.shape
    n = t // DN_CHUNK

    def kern(q_ref, k_ref, v_ref, g_ref, o_ref, sin_ref, s_scr):
        @pl.when(pl.program_id(0) == 0)
        def _():
            s_scr[...] = jnp.zeros(s_scr.shape, F32)

        s = s_scr[...]
        sin_ref[0] = s
        o_ref[...], s_scr[...] = _dn_chunk(q_ref[...], k_ref[...], v_ref[...], g_ref[...], s)

    blk = pl.BlockSpec((nh, DN_CHUNK, HEAD), lambda i: (0, i, 0))
    return pl.pallas_call(
        kern, name="dn_chunks_fwd", grid=(n,),
        in_specs=[blk, blk, blk, pl.BlockSpec((DN_CHUNK, LANES), lambda i: (i, 0))],
        out_specs=[blk, pl.BlockSpec((1, nh, HEAD, HEAD), lambda i: (i, 0, 0, 0))],
        out_shape=[_sds((nh, t, HEAD)), _sds((n, nh, HEAD, HEAD))],
        scratch_shapes=[pltpu.VMEM((nh, HEAD, HEAD), F32)],
        compiler_params=_cparams(dimension_semantics=("arbitrary",)),
    )(q, k, v, gates)


def _dn_chunks_bwd(q, k, v, gates, s_in, do):
    nh, t, _ = q.shape
    n = t // DN_CHUNK

    def kern(q_ref, k_ref, v_ref, g_ref, sin_ref, do_ref, dq_ref, dk_ref, dv_ref, dg_ref, ds_scr):
        @pl.when(pl.program_id(0) == 0)
        def _():
            ds_scr[...] = jnp.zeros(ds_scr.shape, F32)

        _, vjp = jax.vjp(_dn_chunk, q_ref[...], k_ref[...], v_ref[...], g_ref[...], sin_ref[0])
        dq_ref[...], dk_ref[...], dv_ref[...], dg_ref[...], ds_scr[...] = vjp((do_ref[...], ds_scr[...]))

    blk = pl.BlockSpec((nh, DN_CHUNK, HEAD), lambda i: (0, n - 1 - i, 0))
    gblk = pl.BlockSpec((DN_CHUNK, LANES), lambda i: (n - 1 - i, 0))
    hm = _sds((nh, t, HEAD))
    return pl.pallas_call(
        kern, name="dn_chunks_bwd", grid=(n,),
        in_specs=[blk, blk, blk, gblk, pl.BlockSpec((1, nh, HEAD, HEAD), lambda i: (n - 1 - i, 0, 0, 0)), blk],
        out_specs=[blk, blk, blk, gblk],
        out_shape=[hm, hm, hm, _sds((t, LANES))],
        scratch_shapes=[pltpu.VMEM((nh, HEAD, HEAD), F32)],
        compiler_params=_cparams(dimension_semantics=("arbitrary",)),
    )(q, k, v, gates, s_in, do)


def _dn_gate_out(o, z, g):
    return _rms(o, g) * _silu(z)


def _z_spec(tt):
    return (tt, N_HEADS * HEAD), lambda i: (i, DN_QKV // (N_HEADS * HEAD))


def _dn_post_fwd(o, proj, out_norm, tt=256):
    t = o.shape[1]
    tt = min(tt, t)

    def body(o_ref, z_ref, g_ref, y_ref):
        for h in range(N_HEADS):
            y_ref[:, h * HEAD:(h + 1) * HEAD] = _dn_gate_out(o_ref[h], _heads_in(z_ref, h), g_ref[...]).astype(BF16)

    return _rows("dn_post", body, [o, (proj,) + _z_spec(tt)], [_sds((t, N_HEADS * HEAD), BF16)], tt=tt, consts=[out_norm])[0]


def _dn_post_bwd(o, proj, out_norm, dy, tt=256):
    t = o.shape[1]
    tt = min(tt, t)

    def body(o_ref, z_ref, dy_ref, g_ref, do_ref, dz_ref, dg_ref):
        dg = jnp.zeros(g_ref.shape, F32)
        for h in range(N_HEADS):
            _, vjp = jax.vjp(_dn_gate_out, o_ref[h], _heads_in(z_ref, h), g_ref[...])
            a, b, d = vjp(_heads_in(dy_ref, h))
            do_ref[h] = a
            dz_ref[:, h * HEAD:(h + 1) * HEAD] = b
            dg = dg + d
        _acc(dg_ref, dg)

    return _rows("dn_post_bwd", body, [o, (proj,) + _z_spec(tt), dy], [_sds(o.shape), _sds((t, N_HEADS * HEAD))], tt=tt,
                 consts=[out_norm], accs=[_sds(out_norm.shape)])


def _dn_fwd(x, h, w_in, conv_w, a_log, dt_bias, out_norm, w_out):
    proj = _mm("dn_in", h, w_in)
    c = _dn_conv_fwd(proj, conv_w)
    q, k, v, gates = _dn_prep_fwd(c, proj, a_log, dt_bias)
    o, s_in = _dn_chunks_fwd(q, k, v, gates)
    on = _dn_post_fwd(o, proj, out_norm)
    y = _mm("dn_out", on, w_out, add=x)
    return y, (h, proj, c, q, k, v, gates, o, s_in, on)


def _dn_bwd(saved, dy, w_in, conv_w, a_log, dt_bias, out_norm, w_out):
    h, proj, c, q, k, v, gates, o, s_in, on = saved
    d_wout = _mm("dn_out_wgrad", on, dy, ta=True)
    don = _mm("dn_out_dgrad", dy, w_out, tb=True)
    do, dz, d_out_norm = _dn_post_bwd(o, proj, out_norm, don)
    dq, dk, dv, dgates = _dn_chunks_bwd(q, k, v, gates, s_in, do)
    dc, dab, d_a_log, d_dt_bias = _dn_prep_bwd(c, proj, a_log, dt_bias, dq, dk, dv, dgates)
    dproj, d_conv_w = _dn_conv_bwd(proj, conv_w, dc, dz, dab)
    d_win = _mm("dn_in_wgrad", h, dproj, ta=True)
    dh = _mm("dn_in_dgrad", dproj, w_in, tb=True)
    return dh, d_win, d_conv_w, d_a_log, d_dt_bias, d_out_norm, d_wout


MLA_SCALE = MLA_QK ** -0.5
MLA_C = 512


def _swap_raw(x):
    lane = lax.broadcasted_iota(jnp.int32, x.shape, 1)
    half = MLA_ROPE // 2
    y = jnp.where(lane < half, pltpu.roll(x, LANES - half, 1), pltpu.roll(x, half, 1))
    return jnp.where(lane < MLA_ROPE, y, 0.0)


@jax.custom_vjp
def _swap_halves(x):
    return _swap_raw(x)


_swap_halves.defvjp(lambda x: (_swap_raw(x), None), lambda _, d: (_swap_raw(d),))


def _rms_rope(x, g, cos, sin):
    y = x * lax.rsqrt(jnp.sum(x * x, axis=-1, keepdims=True) * (1.0 / MLA_ROPE) + NORM_EPS) * g
    return y * cos + _swap_halves(y) * sin


def _mla_latent(cq, ckv, kr, gq, gkv, gkr, cos, sin):
    return _rms(cq, gq), _rms(ckv, gkv), _rms_rope(kr, gkr, cos, sin)


def _mla_prep1_fwd(c, gq, gkv, gkr, cos, sin):
    t = c.shape[0]

    def body(c_ref, cos_ref, sin_ref, gq_ref, gkv_ref, gkr_ref, cq_ref, ckv_ref, kr_ref):
        a, b, r = _mla_latent(c_ref[:, :256], c_ref[:, 256:384], c_ref[:, 384:], gq_ref[...], gkv_ref[...], gkr_ref[...],
                              cos_ref[...], sin_ref[...])
        cq_ref[...] = a.astype(BF16)
        ckv_ref[...] = b.astype(BF16)
        kr_ref[...] = r.astype(BF16)

    return _rows("mla_prep1", body, [c, cos, sin], [_sds((t, 256), BF16), _sds((t, HEAD), BF16), _sds((t, HEAD), BF16)],
                 tt=512, consts=[gq, gkv, gkr])


def _mla_prep1_bwd(c, gq, gkv, gkr, cos, sin, dcq, dckv, dkr_heads):
    def body(c_ref, cos_ref, sin_ref, dcq_ref, dckv_ref, dkr_ref, gq_ref, gkv_ref, gkr_ref, dc_ref, dgq_ref, dgkv_ref, dgkr_ref):
        dkr = dkr_ref[0]
        for h in range(1, N_HEADS):
            dkr = dkr + dkr_ref[h]
        _, vjp = jax.vjp(_mla_latent, c_ref[:, :256], c_ref[:, 256:384], c_ref[:, 384:], gq_ref[...], gkv_ref[...], gkr_ref[...],
                         cos_ref[...], sin_ref[...])
        a, b, r, d1, d2, d3, _, _ = vjp((dcq_ref[...], dckv_ref[...], dkr))
        dc_ref[:, :256] = a.astype(BF16)
        dc_ref[:, 256:384] = b.astype(BF16)
        dc_ref[:, 384:] = r.astype(BF16)
        _acc(dgq_ref, d1)
        _acc(dgkv_ref, d2)
        _acc(dgkr_ref, d3)

    return _rows("mla_prep1_bwd", body, [c, cos, sin, dcq, dckv, dkr_heads], [_sds(c.shape, BF16)], tt=512,
                 consts=[gq, gkv, gkr], accs=[_sds(gq.shape), _sds(gkv.shape), _sds(gkr.shape)])


def _mla_heads(qn, qr, kn, gqn, gqr, gkn, cos, sin):
    return _rms(qn, gqn) * MLA_SCALE, _rms_rope(qr, gqr, cos, sin) * MLA_SCALE, _rms(kn, gkn)


def _mla_prep2_fwd(qa, kv, gqn, gqr, gkn, cos, sin):
    t = qa.shape[0]

    def body(qa_ref, kv_ref, cos_ref, sin_ref, gqn_ref, gqr_ref, gkn_ref, qn_ref, qr_ref, kn_ref, v_ref):
        for h in range(N_HEADS):
            a, b, c = _mla_heads(_heads_in(qa_ref, h), _heads_in(qa_ref, N_HEADS + h), _heads_in(kv_ref, h),
                                 gqn_ref[...], gqr_ref[...], gkn_ref[...], cos_ref[...], sin_ref[...])
            qn_ref[h] = a.astype(BF16)
            qr_ref[h] = b.astype(BF16)
            kn_ref[h] = c.astype(BF16)
            v_ref[h] = _heads_in(kv_ref, N_HEADS + h).astype(BF16)

    hm = _sds((N_HEADS, t, HEAD), BF16)
    return _rows("mla_prep2", body, [qa, kv, cos, sin], [hm, hm, hm, hm], tt=256, consts=[gqn, gqr, gkn])


def _mla_prep2_bwd(qa, kv, gqn, gqr, gkn, cos, sin, dqn, dqr, dkn, dv):
    def body(qa_ref, kv_ref, cos_ref, sin_ref, dqn_ref, dqr_ref, dkn_ref, dv_ref, gqn_ref, gqr_ref, gkn_ref,
             dqa_ref, dkv_ref, d1_ref, d2_ref, d3_ref):
        d1 = jnp.zeros(gqn_ref.shape, F32)
        d2 = jnp.zeros(gqr_ref.shape, F32)
        d3 = jnp.zeros(gkn_ref.shape, F32)
        for h in range(N_HEADS):
            _, vjp = jax.vjp(_mla_heads, _heads_in(qa_ref, h), _heads_in(qa_ref, N_HEADS + h), _heads_in(kv_ref, h),
                             gqn_ref[...], gqr_ref[...], gkn_ref[...], cos_ref[...], sin_ref[...])
            a, b, c, e1, e2, e3, _, _ = vjp((dqn_ref[h], dqr_ref[h], dkn_ref[h]))
            dqa_ref[:, h * HEAD:(h + 1) * HEAD] = a.astype(BF16)
            dqa_ref[:, (N_HEADS + h) * HEAD:(N_HEADS + h + 1) * HEAD] = b.astype(BF16)
            dkv_ref[:, h * HEAD:(h + 1) * HEAD] = c.astype(BF16)
            dkv_ref[:, (N_HEADS + h) * HEAD:(N_HEADS + h + 1) * HEAD] = dv_ref[h].astype(BF16)
            d1, d2, d3 = d1 + e1, d2 + e2, d3 + e3
        _acc(d1_ref, d1)
        _acc(d2_ref, d2)
        _acc(d3_ref, d3)

    return _rows("mla_prep2_bwd", body, [qa, kv, cos, sin, dqn, dqr, dkn, dv], [_sds(qa.shape, BF16), _sds(kv.shape, BF16)],
                 tt=256, consts=[gqn, gqr, gkn], accs=[_sds(gqn.shape), _sds(gqr.shape), _sds(gkn.shape)])


def _mla_attn_fwd(qn, qr, kn, kr, v):
    nh, t, _ = qn.shape
    bq = _q_block(t)
    per = bq // ATT_BLOCK

    def kern(qn_ref, qr_ref, kn_ref, kr_ref, v_ref, o_ref, lse_ref):
        qb = pl.program_id(1)
        qv = jnp.concatenate([qn_ref[0], qr_ref[0]], axis=1)
        order = _key_order(bq, qb)

        def body(kb, carry, diagonal):
            acc, m, l = carry
            off = pl.multiple_of(kb * ATT_BLOCK, ATT_BLOCK)
            kv = jnp.concatenate([kn_ref[0, pl.ds(off, ATT_BLOCK), :], kr_ref[pl.ds(off, ATT_BLOCK), :]], axis=1)
            s = _dot_nt(qv, kv)
            if diagonal:
                s = jnp.where(order >= kb * ATT_BLOCK, s, -jnp.inf)
            m_new = jnp.maximum(m, jnp.max(s, axis=1, keepdims=True))
            alpha = jnp.exp(m - m_new)
            p = jnp.exp(s - m_new)
            acc = acc * alpha + _dot(p.astype(BF16), v_ref[0, pl.ds(off, ATT_BLOCK), :])
            return acc, m_new, l * alpha + jnp.sum(p, axis=1, keepdims=True)

        init = (jnp.zeros((bq, HEAD), F32), jnp.full((bq, 1), -jnp.inf, F32), jnp.zeros((bq, 1), F32))
        carry = lax.fori_loop(0, qb * per, lambda i, c: body(i, c, False), init)
        acc, m, l = lax.fori_loop(qb * per, (qb + 1) * per, lambda i, c: body(i, c, True), carry)
        o_ref[...] = acc / l
        lse_ref[...] = jnp.broadcast_to(m + jnp.log(l), (bq, HEAD))

    blk = pl.BlockSpec((1, bq, HEAD), lambda h, i: (h, i, 0))
    full = pl.BlockSpec((1, t, HEAD), lambda h, i: (h, 0, 0))
    tok = pl.BlockSpec((bq, HEAD), lambda h, i: (i, h))
    return pl.pallas_call(
        kern, name="mla_attn_fwd", grid=(nh, t // bq),
        in_specs=[blk, blk, full, pl.BlockSpec((t, HEAD), lambda h, i: (0, 0)), full],
        out_specs=[tok, tok], out_shape=[_sds((t, nh * HEAD)), _sds((t, nh * HEAD))],
        compiler_params=_cparams(dimension_semantics=("parallel", "arbitrary")),
    )(qn, qr, kn, kr, v)


def _mla_attn_bwd(qn, qr, kn, kr, v, o, lse, do):
    nh, t, _ = qn.shape
    bq = _q_block(t)
    per = bq // ATT_BLOCK

    def kern(qn_ref, qr_ref, kn_ref, kr_ref, v_ref, o_ref, lse_ref, do_ref, dqn_ref, dqr_ref, dkn_ref, dkr_ref, dv_ref):
        qb = pl.program_id(1)

        @pl.when(qb == 0)
        def _():
            dkn_ref[...] = jnp.zeros(dkn_ref.shape, F32)
            dkr_ref[...] = jnp.zeros(dkr_ref.shape, F32)
            dv_ref[...] = jnp.zeros(dv_ref.shape, F32)

        qv = jnp.concatenate([qn_ref[0], qr_ref[0]], axis=1)
        dov = do_ref[...]
        dob = dov.astype(BF16)
        delta = jnp.sum(dov * o_ref[...], axis=1, keepdims=True)
        lse_col = lse_ref[:, :1]
        order = _key_order(bq, qb)

        def body(kb, dq, diagonal):
            off = pl.multiple_of(kb * ATT_BLOCK, ATT_BLOCK)
            kv = jnp.concatenate([kn_ref[0, pl.ds(off, ATT_BLOCK), :], kr_ref[pl.ds(off, ATT_BLOCK), :]], axis=1)
            vv = v_ref[0, pl.ds(off, ATT_BLOCK), :]
            p = jnp.exp(_dot_nt(qv, kv) - lse_col)
            if diagonal:
                p = jnp.where(order >= kb * ATT_BLOCK, p, 0.0)
            ds = (p * (_dot_nt(dob, vv) - delta)).astype(BF16)
            dk = _dot_tn(ds, qv)
            dkn_ref[0, pl.ds(off, ATT_BLOCK), :] += dk[:, :HEAD]
            dkr_ref[0, pl.ds(off, ATT_BLOCK), :] += dk[:, HEAD:]
            dv_ref[0, pl.ds(off, ATT_BLOCK), :] += _dot_tn(p.astype(BF16), dob)
            return dq + _dot(ds, kv)

        dq = lax.fori_loop(0, qb * per, lambda i, c: body(i, c, False), jnp.zeros((bq, 2 * HEAD), F32))
        dq = lax.fori_loop(qb * per, (qb + 1) * per, lambda i, c: body(i, c, True), dq)
        dqn_ref[0] = dq[:, :HEAD]
        dqr_ref[0] = dq[:, HEAD:]

    hm = _sds((nh, t, HEAD))
    blk = pl.BlockSpec((1, bq, HEAD), lambda h, i: (h, i, 0))
    full = pl.BlockSpec((1, t, HEAD), lambda h, i: (h, 0, 0))
    tok = pl.BlockSpec((bq, HEAD), lambda h, i: (i, h))
    return pl.pallas_call(
        kern, name="mla_attn_bwd", grid=(nh, t // bq),
        in_specs=[blk, blk, full, pl.BlockSpec((t, HEAD), lambda h, i: (0, 0)), full, tok, tok, tok],
        out_specs=[blk, blk, full, full, full], out_shape=[hm, hm, hm, hm, hm],
        compiler_params=_cparams(dimension_semantics=("parallel", "arbitrary")),
    )(qn, qr, kn, kr, v, o, lse, do)


def _rope_tables(t):
    inv_freq = ROPE_THETA ** (-jnp.arange(0, MLA_ROPE, 2, dtype=F32) / MLA_ROPE)
    ang = jnp.arange(t, dtype=F32)[:, None] * inv_freq[None, :]
    c, s = jnp.cos(ang), jnp.sin(ang)
    pad = ((0, 0), (0, LANES - MLA_ROPE))
    return jnp.pad(jnp.concatenate([c, c], axis=1), pad), jnp.pad(jnp.concatenate([-s, s], axis=1), pad)


def _pad_lanes(v, n=LANES):
    return jnp.pad(v, (0, n - v.shape[0])).reshape(1, n)


def _mla_layout(w_down, w_uq, w_ukv):
    w_down_p = jnp.pad(w_down, ((0, 0), (0, MLA_C - w_down.shape[1])))
    uq = w_uq.reshape(w_uq.shape[0], N_HEADS, MLA_QK)
    rope = jnp.pad(uq[:, :, HEAD:], ((0, 0), (0, 0), (0, LANES - MLA_ROPE)))
    w_uq_p = jnp.concatenate([uq[:, :, :HEAD].reshape(-1, N_HEADS * HEAD), rope.reshape(-1, N_HEADS * LANES)], axis=1)
    ukv = w_ukv.reshape(w_ukv.shape[0], N_HEADS, 2 * HEAD)
    w_ukv_p = jnp.concatenate([ukv[:, :, :HEAD].reshape(-1, N_HEADS * HEAD), ukv[:, :, HEAD:].reshape(-1, N_HEADS * HEAD)], axis=1)
    return w_down_p, w_uq_p, w_ukv_p


def _mla_unlayout(d_down_p, d_uq_p, d_ukv_p):
    d_down = d_down_p[:, :256 + HEAD + MLA_ROPE]
    nope = d_uq_p[:, :N_HEADS * HEAD].reshape(-1, N_HEADS, HEAD)
    rope = d_uq_p[:, N_HEADS * HEAD:].reshape(-1, N_HEADS, LANES)[:, :, :MLA_ROPE]
    d_uq = jnp.concatenate([nope, rope], axis=2).reshape(-1, N_HEADS * MLA_QK)
    kn = d_ukv_p[:, :N_HEADS * HEAD].reshape(-1, N_HEADS, HEAD)
    vv = d_ukv_p[:, N_HEADS * HEAD:].reshape(-1, N_HEADS, HEAD)
    d_ukv = jnp.concatenate([kn, vv], axis=2).reshape(-1, N_HEADS * 2 * HEAD)
    return d_down, d_uq, d_ukv


def _mla_weight_shapes():
    return (_sds((1024, MLA_C), BF16), _sds((1, 256)), _sds((1, HEAD)), _sds((256, 2048), BF16), _sds((HEAD, 2048), BF16),
            _sds((1, HEAD)), _sds((1, HEAD)), _sds((1, HEAD)), _sds((1, HEAD)), _sds((1024, 1024), BF16),
            _sds((4096, HEAD)), _sds((4096, HEAD)))


def _mla_fwd(x, h, w_down, gq, gkv, w_uq, w_ukv, gqn, gqr, gkn, gkr, w_out, cos, sin):
    c = _mm("mla_down", h, w_down)
    cq, ckv, kr = _mla_prep1_fwd(c, gq, gkv, gkr, cos, sin)
    qa = _mm("mla_uq", cq, w_uq)
    kv = _mm("mla_ukv", ckv, w_ukv)
    qn, qr, kn, v = _mla_prep2_fwd(qa, kv, gqn, gqr, gkn, cos, sin)
    o, lse = _mla_attn_fwd(qn, qr, kn, kr, v)
    y = _mm("mla_out", o, w_out, add=x)
    return y, (h, c, cq, ckv, kr, qa, kv, qn, qr, kn, v, o, lse)


def _mla_bwd(saved, dy, w_down, gq, gkv, w_uq, w_ukv, gqn, gqr, gkn, gkr, w_out, cos, sin):
    h, c, cq, ckv, kr, qa, kv, qn, qr, kn, v, o, lse = saved
    d_wout = _mm("mla_out_wgrad", o, dy, ta=True)
    do = _mm("mla_out_dgrad", dy, w_out, tb=True)
    dqn, dqr, dkn, dkr, dv = _mla_attn_bwd(qn, qr, kn, kr, v, o, lse, do)
    dqa, dkv, dgqn, dgqr, dgkn = _mla_prep2_bwd(qa, kv, gqn, gqr, gkn, cos, sin, dqn, dqr, dkn, dv)
    d_wuq = _mm("mla_uq_wgrad", cq, dqa, ta=True)
    d_wukv = _mm("mla_ukv_wgrad", ckv, dkv, ta=True)
    dcq = _mm("mla_uq_dgrad", dqa, w_uq, tb=True)
    dckv = _mm("mla_ukv_dgrad", dkv, w_ukv, tb=True)
    dc, dgq, dgkv, dgkr = _mla_prep1_bwd(c, gq, gkv, gkr, cos, sin, dcq, dckv, dkr)
    d_wdown = _mm("mla_down_wgrad", h, dc, ta=True)
    dh = _mm("mla_down_dgrad", dc, w_down, tb=True)
    return dh, d_wdown, dgq, dgkv, d_wuq, d_wukv, dgqn, dgqr, dgkn, dgkr, d_wout


def _loss_head(y, target):
    d = y.shape[1]

    def body(y_ref, t_ref, dy_ref, l_ref):
        err = y_ref[...] - t_ref[...]
        dy_ref[...] = err * (1.0 / d)
        part = 0.5 * jnp.sum(jnp.sum(err * err, axis=1, keepdims=True) * (1.0 / d), axis=0, keepdims=True)
        _acc(l_ref, jnp.broadcast_to(part, (1, LANES)))

    return _rows("loss_head", body, [y, target], [_sds(y.shape)], tt=512, accs=[_sds((1, LANES))])


MESH_ID = pl.DeviceIdType.MESH
HBM_SPEC = pl.BlockSpec(memory_space=pltpu.HBM)


def _all_gather(name, x):
    m_per, n = x.shape

    def body(x_ref, out_ref, send_sems, recv_sems, local_sem):
        x, y, c = lax.axis_index("x"), lax.axis_index("y"), lax.axis_index("c")
        me, sibling = (x, y, c), (x, y, 1 - c)
        chips = [(1 - x, y), (x, 1 - y), (1 - x, 1 - y)]

        def rows(px, py, pc):
            return out_ref.at[pl.ds((4 * px + 2 * py + pc) * m_per, m_per), :]

        def copy(k, block, to, src=None):
            return pltpu.make_async_remote_copy(
                src_ref=rows(*block) if src is None else src, dst_ref=rows(*block),
                send_sem=send_sems.at[k], recv_sem=recv_sems.at[k], device_id=to, device_id_type=MESH_ID)

        mine = pltpu.make_async_copy(x_ref, rows(*me), local_sem)
        mine.start()
        first = [copy(0, me, sibling, src=x_ref)]
        first += [copy(1 + j, me, (*chip, c), src=x_ref) for j, chip in enumerate(chips)]
        for cp in first:
            cp.start()
        passed = [copy(4 + j, (*chip, c), sibling) for j, chip in enumerate(chips)]
        for j, chip in enumerate(chips):
            copy(1 + j, (*chip, c), me).wait_recv()
            passed[j].start()
        copy(0, sibling, me).wait_recv()
        for j, chip in enumerate(chips):
            copy(4 + j, (*chip, 1 - c), me).wait_recv()
        for cp in first + passed:
            cp.wait_send()
        mine.wait()

    return pl.pallas_call(
        body, name=name,
        out_shape=jax.ShapeDtypeStruct((N_DEV * m_per, n), x.dtype),
        in_specs=[HBM_SPEC], out_specs=HBM_SPEC,
        scratch_shapes=[pltpu.SemaphoreType.DMA((7,)), pltpu.SemaphoreType.DMA((7,)), pltpu.SemaphoreType.DMA],
    )(x)


def _all_gather_groups(name, xs):
    ng = len(xs)

    def body(*refs):
        x_refs, out_refs, token = refs[:ng], refs[ng:2 * ng], refs[2 * ng]
        send_sems, recv_sems, local_sems = refs[2 * ng + 1:]
        token[...] = jnp.zeros(token.shape, F32)
        x, y, c = lax.axis_index("x"), lax.axis_index("y"), lax.axis_index("c")
        me, sibling = (x, y, c), (x, y, 1 - c)
        chips = [(1 - x, y), (x, 1 - y), (1 - x, 1 - y)]

        def copy(g, k, block, to, src=None):
            px, py, pc = block
            dst = out_refs[g].at[4 * px + 2 * py + pc]
            return pltpu.make_async_remote_copy(
                src_ref=dst if src is None else src, dst_ref=dst,
                send_sem=send_sems.at[g, k], recv_sem=recv_sems.at[g, k], device_id=to, device_id_type=MESH_ID)

        mine = [pltpu.make_async_copy(x_refs[g], out_refs[g].at[4 * x + 2 * y + c], local_sems.at[g]) for g in range(ng)]
        for cp in mine:
            cp.start()
        first = []
        for g in range(ng):
            first.append(copy(g, 0, me, sibling, src=x_refs[g]))
            first += [copy(g, 1 + j, me, (*chip, c), src=x_refs[g]) for j, chip in enumerate(chips)]
        for cp in first:
            cp.start()
        passed = []
        for j, chip in enumerate(chips):
            for g in range(ng):
                copy(g, 1 + j, (*chip, c), me).wait_recv()
                passed.append(copy(g, 4 + j, (*chip, c), sibling))
                passed[-1].start()
        for g in range(ng):
            copy(g, 0, sibling, me).wait_recv()
            for j, chip in enumerate(chips):
                copy(g, 4 + j, (*chip, 1 - c), me).wait_recv()
        for cp in first + passed:
            cp.wait_send()
        for cp in mine:
            cp.wait()

    return pl.pallas_call(
        body, name=name,
        out_shape=[jax.ShapeDtypeStruct((N_DEV,) + x.shape, x.dtype) for x in xs] + [_sds((8, LANES))],
        in_specs=[HBM_SPEC] * ng, out_specs=[HBM_SPEC] * ng + [pl.BlockSpec(memory_space=pltpu.VMEM)],
        scratch_shapes=[pltpu.SemaphoreType.DMA((ng, 7)), pltpu.SemaphoreType.DMA((ng, 7)), pltpu.SemaphoreType.DMA((ng,))],
    )(*xs)


EFFECT = pltpu.SideEffectType.DATAFLOW_SIDE_EFFECTING
SEM_SPEC = pl.BlockSpec(memory_space=pltpu.SEMAPHORE)


def _push_copies(src_refs, land_refs, send_sems, recv_sems, chunked):
    x, y, c = lax.axis_index("x"), lax.axis_index("y"), lax.axis_index("c")
    me = 4 * x + 2 * y + c
    copies = []
    for g, (src, land) in enumerate(zip(src_refs, land_refs)):
        for k in range(1, N_DEV):
            px = 1 - x if k & 4 else x
            py = 1 - y if k & 2 else y
            pc = 1 - c if k & 1 else c
            copies.append(pltpu.make_async_remote_copy(
                src_ref=src.at[4 * px + 2 * py + pc] if chunked else src, dst_ref=land.at[me],
                send_sem=send_sems.at[g * (N_DEV - 1) + k - 1], recv_sem=recv_sems.at[g * (N_DEV - 1) + k - 1],
                device_id=(px, py, pc), device_id_type=MESH_ID))
    return copies


def _hbm(x):
    return pltpu.with_memory_space_constraint(x, pltpu.HBM)


def _push_start(name, srcs, lands, chunked):
    ng = len(srcs)

    def body(*refs):
        for cp in _push_copies(refs[:ng], refs[ng:2 * ng], refs[2 * ng], refs[2 * ng + 1], chunked):
            cp.start()
        refs[-1][...] = jnp.zeros(refs[-1].shape, F32)

    bufs = list(srcs) + list(lands)
    outs = pl.pallas_call(
        body, name=name,
        out_shape=(pltpu.SemaphoreType.DMA((ng * (N_DEV - 1),)), pltpu.SemaphoreType.DMA((ng * (N_DEV - 1),)),
                   *[pltpu.HBM(b.shape, b.dtype) for b in bufs], jax.ShapeDtypeStruct((8, LANES), F32)),
        in_specs=[HBM_SPEC] * (2 * ng),
        out_specs=(SEM_SPEC, SEM_SPEC, *[HBM_SPEC] * (2 * ng), pl.BlockSpec(memory_space=pltpu.VMEM)),
        input_output_aliases={i: 2 + i for i in range(2 * ng)},
        compiler_params=pltpu.CompilerParams(has_side_effects=EFFECT),
    )(*[_hbm(b) for b in bufs])
    return outs[0], outs[1], list(outs[2:2 + ng]), list(outs[2 + ng:2 + 2 * ng]), outs[-1]


def _push_wait(name, started, after, chunked):
    send_sems, recv_sems, srcs, lands, _ = started
    ng = len(srcs)

    def body(*refs):
        copies = _push_copies(refs[:ng], refs[ng:2 * ng], refs[2 * ng], refs[2 * ng + 1], chunked)
        for cp in copies:
            cp.wait_send()
        for cp in copies:
            cp.wait_recv()

    bufs = srcs + lands
    outs = pl.pallas_call(
        body, name=name,
        out_shape=tuple(pltpu.HBM(b.shape, b.dtype) for b in bufs),
        in_specs=[HBM_SPEC] * (2 * ng) + [SEM_SPEC, SEM_SPEC, pl.BlockSpec(memory_space=pl.ANY)],
        out_specs=tuple([HBM_SPEC] * (2 * ng)),
        input_output_aliases={i: i for i in range(2 * ng)},
        compiler_params=pltpu.CompilerParams(has_side_effects=EFFECT),
    )(*bufs, send_sems, recv_sems, after)
    return list(outs[ng:])


def _sum_adam_devices(own, recv, dev, w, m, v):
    ndev, r, c_ = recv.shape
    tr = _pick(r, (256, 128, 96, 32))

    def body(dev_ref, own_ref, r_ref, w_ref, m_ref, v_ref, g_ref, d_ref, mo_ref, vo_ref):
        me = dev_ref[0]
        g = jnp.where(me == 0, own_ref[...], r_ref[0].astype(F32))
        for j in range(1, ndev):
            g = g + jnp.where(me == j, own_ref[...], r_ref[j].astype(F32))
        g_ref[...] = g
        d_ref[...], mo_ref[...], vo_ref[...] = _adam(w_ref[...], g, m_ref[...], v_ref[...])

    row = pl.BlockSpec((tr, c_), lambda i, dev_ref: (i, 0))
    return pl.pallas_call(
        body, name="sum_adam",
        grid_spec=pltpu.PrefetchScalarGridSpec(
            num_scalar_prefetch=1, grid=(r // tr,),
            in_specs=[row, pl.BlockSpec((ndev, tr, c_), lambda i, dev_ref: (0, i, 0)), row, row, row],
            out_specs=[row, row, row, row]),
        out_shape=[_sds((r, c_))] * 4,
        compiler_params=_cparams(dimension_semantics=("arbitrary",)),
    )(dev, own, recv, w, m, v)


def _cols_from_shards(w, width):
    ns, r, cs = w.shape
    tr = _pick(r, (256, 128))

    def body(w_ref, o_ref):
        parts = [w_ref[j] for j in range(ns)]
        if width > ns * cs:
            parts.append(jnp.zeros((tr, width - ns * cs), w.dtype))
        o_ref[...] = jnp.concatenate(parts, axis=1)

    return pl.pallas_call(
        body, name="cols_from_shards", grid=(r // tr,),
        in_specs=[pl.BlockSpec((ns, tr, cs), lambda i: (0, i, 0))], out_specs=pl.BlockSpec((tr, width), lambda i: (i, 0)),
        out_shape=jax.ShapeDtypeStruct((r, width), w.dtype), compiler_params=_cparams(dimension_semantics=("arbitrary",)),
    )(w)


def _shards_from_cols(g, cs):
    r, width = g.shape
    tr = _pick(r, (256, 128))

    def body(g_ref, o_ref):
        for j in range(N_DEV):
            o_ref[j] = g_ref[:, j * cs:(j + 1) * cs]

    return pl.pallas_call(
        body, name="shards_from_cols", grid=(r // tr,),
        in_specs=[pl.BlockSpec((tr, width), lambda i: (i, 0))], out_specs=pl.BlockSpec((N_DEV, tr, cs), lambda i: (0, i, 0)),
        out_shape=jax.ShapeDtypeStruct((N_DEV, r, cs), g.dtype), compiler_params=_cparams(dimension_semantics=("arbitrary",)),
    )(g)


def _adam(w, g, m, v):
    m = ADAM_B1 * m + (1.0 - ADAM_B1) * g
    v = ADAM_B2 * v + (1.0 - ADAM_B2) * (g * g)
    m_hat = m / (1.0 - ADAM_B1 ** ADAM_STEP)
    v_hat = v / (1.0 - ADAM_B2 ** ADAM_STEP)
    return -ADAM_LR * (m_hat / (jnp.sqrt(v_hat) + ADAM_EPS) + ADAM_WD * w), m, v


def _sum_devices(gathered):
    m_all, n = gathered.shape
    m_per = m_all // N_DEV

    def body(x_ref, o_ref):
        s = x_ref[0:m_per, :]
        for j in range(1, N_DEV):
            s = s + x_ref[j * m_per:(j + 1) * m_per, :]
        o_ref[...] = s

    return pl.pallas_call(body, name="sum_devices", out_shape=_sds((m_per, n)), compiler_params=_cparams())(gathered)


def _adam_small(w, g, m, v):
    def body(w_ref, g_ref, m_ref, v_ref, d_ref, mo_ref, vo_ref):
        d_ref[...], mo_ref[...], vo_ref[...] = _adam(w_ref[...], g_ref[...], m_ref[...], v_ref[...])

    return pl.pallas_call(body, name="adam_small", out_shape=[_sds(w.shape)] * 3, compiler_params=_cparams())(w, g, m, v)


N_LAYERS = 4
_MIXER = ("dn", "sb", "mla")
_MIXER_PARAMS = {
    "dn": ("dn_w_in", "dn_conv_w", "dn_a_log", "dn_dt_bias", "dn_out_norm", "dn_w_out"),
    "sb": ("sb_w_qkv", "sb_q_norm", "sb_k_norm", "sb_w_out"),
    "mla": ("mla_w_down", "mla_q_a_norm", "mla_kv_a_norm", "mla_w_uq", "mla_w_ukv", "mla_q_nope_norm", "mla_q_rope_norm",
            "mla_k_nope_norm", "mla_k_rope_norm", "mla_w_out"),
}
_BIG_AXIS = {"dn_w_in": 1, "dn_w_out": 0, "sb_w_qkv": 1, "sb_w_out": 0, "mla_w_down": 0, "mla_w_uq": 1, "mla_w_ukv": 1,
             "mla_w_out": 0, "ffn_w_gate_up": 1, "ffn_w_down": 0}


def _weight_names():
    names = []
    for i in range(N_LAYERS):
        p = "l%d_" % i
        names += [p + "mix_norm"] + [p + n for n in _MIXER_PARAMS[_MIXER[i % 3]]] + [p + "ffn_norm", p + "ffn_w_gate_up", p + "ffn_w_down"]
    return names


WEIGHTS = _weight_names()
BIG = [n for n in WEIGHTS if n[3:] in _BIG_AXIS]
SMALL = [n for n in WEIGHTS if n[3:] not in _BIG_AXIS]
CONV = [n for n in SMALL if n.endswith("conv_w")]


def _ceil_to(n, k):
    return -(-n // k) * k


def _pack(arrs, cols, row_mult):
    parts = []
    for a in arrs:
        f = a.reshape(-1)
        parts.append(jnp.pad(f, (0, _ceil_to(f.shape[0], cols) - f.shape[0])))
    flat = jnp.concatenate(parts)
    rows = _ceil_to(flat.shape[0] // cols, row_mult)
    return jnp.pad(flat, (0, rows * cols - flat.shape[0])).reshape(rows, cols)


def _unpack(buf, shapes):
    cols = buf.shape[-1]
    out, r0 = [], 0
    for s in shapes:
        n = math.prod(s)
        nr = _ceil_to(n, cols) // cols
        out.append(buf[r0:r0 + nr].reshape(-1)[:n].reshape(s))
        r0 += nr
    return out


def _layer_groups(i):
    by = {"gu": (704, []), "down": (1024, []), "out": (1024, []), "dn_in": (514, []), "sb_qkv": (384, []), "mla": (512, [])}
    key = {"ffn_w_gate_up": "gu", "ffn_w_down": "down", "dn_w_in": "dn_in", "sb_w_qkv": "sb_qkv", "mla_w_down": "mla",
           "mla_w_uq": "mla", "mla_w_ukv": "mla"}
    for n in BIG:
        if n.startswith("l%d_" % i):
            by[key.get(n[3:], "out")][1].append(n)
    return [g for g in by.values() if g[1]]


LAYER_GROUPS = [_layer_groups(i) for i in range(N_LAYERS)]
N_FFN_GROUPS = 2


def _stack_group(grp, get):
    width, names = grp
    return jnp.concatenate([jnp.pad(get(n), ((0, 0), (0, width - get(n).shape[1]))) for n in names], axis=0)


def _unstack_group(grp, buf, shape_of):
    out, r0 = [], 0
    for n in grp[1]:
        rs, cs = shape_of(n)
        out.append(buf[..., r0:r0 + rs, :cs])
        r0 += rs
    return out


def kernel(x, l0_mix_norm, l0_dn_w_in, l0_dn_conv_w, l0_dn_a_log, l0_dn_dt_bias, l0_dn_out_norm, l0_dn_w_out, l0_ffn_norm, l0_ffn_w_gate_up, l0_ffn_w_down, l1_mix_norm, l1_sb_w_qkv, l1_sb_q_norm, l1_sb_k_norm, l1_sb_w_out, l1_ffn_norm, l1_ffn_w_gate_up, l1_ffn_w_down, l2_mix_norm, l2_mla_w_down, l2_mla_q_a_norm, l2_mla_kv_a_norm, l2_mla_w_uq, l2_mla_w_ukv, l2_mla_q_nope_norm, l2_mla_q_rope_norm, l2_mla_k_nope_norm, l2_mla_k_rope_norm, l2_mla_w_out, l2_ffn_norm, l2_ffn_w_gate_up, l2_ffn_w_down, l3_mix_norm, l3_dn_w_in, l3_dn_conv_w, l3_dn_a_log, l3_dn_dt_bias, l3_dn_out_norm, l3_dn_w_out, l3_ffn_norm, l3_ffn_w_gate_up, l3_ffn_w_down, loss_target, m_l0_mix_norm, m_l0_dn_w_in, m_l0_dn_conv_w, m_l0_dn_a_log, m_l0_dn_dt_bias, m_l0_dn_out_norm, m_l0_dn_w_out, m_l0_ffn_norm, m_l0_ffn_w_gate_up, m_l0_ffn_w_down, m_l1_mix_norm, m_l1_sb_w_qkv, m_l1_sb_q_norm, m_l1_sb_k_norm, m_l1_sb_w_out, m_l1_ffn_norm, m_l1_ffn_w_gate_up, m_l1_ffn_w_down, m_l2_mix_norm, m_l2_mla_w_down, m_l2_mla_q_a_norm, m_l2_mla_kv_a_norm, m_l2_mla_w_uq, m_l2_mla_w_ukv, m_l2_mla_q_nope_norm, m_l2_mla_q_rope_norm, m_l2_mla_k_nope_norm, m_l2_mla_k_rope_norm, m_l2_mla_w_out, m_l2_ffn_norm, m_l2_ffn_w_gate_up, m_l2_ffn_w_down, m_l3_mix_norm, m_l3_dn_w_in, m_l3_dn_conv_w, m_l3_dn_a_log, m_l3_dn_dt_bias, m_l3_dn_out_norm, m_l3_dn_w_out, m_l3_ffn_norm, m_l3_ffn_w_gate_up, m_l3_ffn_w_down, v_l0_mix_norm, v_l0_dn_w_in, v_l0_dn_conv_w, v_l0_dn_a_log, v_l0_dn_dt_bias, v_l0_dn_out_norm, v_l0_dn_w_out, v_l0_ffn_norm, v_l0_ffn_w_gate_up, v_l0_ffn_w_down, v_l1_mix_norm, v_l1_sb_w_qkv, v_l1_sb_q_norm, v_l1_sb_k_norm, v_l1_sb_w_out, v_l1_ffn_norm, v_l1_ffn_w_gate_up, v_l1_ffn_w_down, v_l2_mix_norm, v_l2_mla_w_down, v_l2_mla_q_a_norm, v_l2_mla_kv_a_norm, v_l2_mla_w_uq, v_l2_mla_w_ukv, v_l2_mla_q_nope_norm, v_l2_mla_q_rope_norm, v_l2_mla_k_nope_norm, v_l2_mla_k_rope_norm, v_l2_mla_w_out, v_l2_ffn_norm, v_l2_ffn_w_gate_up, v_l2_ffn_w_down, v_l3_mix_norm, v_l3_dn_w_in, v_l3_dn_conv_w, v_l3_dn_a_log, v_l3_dn_dt_bias, v_l3_dn_out_norm, v_l3_dn_w_out, v_l3_ffn_norm, v_l3_ffn_w_gate_up, v_l3_ffn_w_down):
    a = dict(locals())
    return _train_step(a)


def _train_step(a):
    mx, my, mc = lax.axis_index("x"), lax.axis_index("y"), lax.axis_index("c")
    dev = 4 * mx + 2 * my + mc
    dev_arr = jnp.reshape(dev, (1,)).astype(jnp.int32)
    t, d = a["x"].shape[1], a["x"].shape[2]
    xs = a["x"].reshape(t, d)
    target = a["loss_target"].reshape(t, d)

    full = {}

    def unpack(groups, bufs):
        for grp, buf in zip(groups, bufs):
            for n, shards in zip(grp[1], _unstack_group(grp, buf, lambda n: a[n].shape)):
                kind = n[3:]
                if kind == "ffn_w_gate_up":
                    full[n] = shards
                elif _BIG_AXIS[kind] == 0:
                    full[n] = shards.reshape(N_DEV * shards.shape[1], shards.shape[2])
                else:
                    width = DN_PROJ if kind == "dn_w_in" else N_DEV * shards.shape[2]
                    full[n] = _cols_from_shards(shards, width)

    def local_shards(groups):
        return [_stack_group(grp, lambda n: a[n].astype(BF16)) for grp in groups]

    pushed_groups = [LAYER_GROUPS[0][:N_FFN_GROUPS]] + LAYER_GROUPS[1:]
    conv_pack = _pack([a[n] for n in CONV], LANES, 8)
    first_groups = LAYER_GROUPS[0][N_FFN_GROUPS:]
    *first_bufs, conv_all, gathered = _all_gather_groups("gather_weights", local_shards(first_groups) + [conv_pack])
    unpack(first_groups, first_bufs)
    for n, parts in zip(CONV, zip(*[_unpack(conv_all[j], [a[n].shape for n in CONV]) for j in range(N_DEV)])):
        full[n] = jnp.concatenate(parts, axis=1)
    after_first = gathered[0, 0].astype(BF16)
    gathers, started = {}, jnp.zeros((), F32)
    for i in range(N_LAYERS):
        srcs = [s + after_first for s in local_shards(pushed_groups[i])]
        lands = [lax.dynamic_update_index_in_dim(lax.empty((N_DEV,) + s.shape, s.dtype), s, dev, 0) for s in srcs]
        gathers[i] = _push_start("gather_start_l%d" % i, srcs, lands, False)
        started = started + gathers[i][-1][0, 0]

    def vec(n):
        return a[n].reshape(1, -1)

    cos, sin = _rope_tables(t)

    def mixer_args(i):
        p = "l%d_" % i
        kind = _MIXER[i % 3]
        if kind == "dn":
            args = (full[p + "dn_w_in"], full[p + "dn_conv_w"], _pad_lanes(a[p + "dn_a_log"]), _pad_lanes(a[p + "dn_dt_bias"]),
                    vec(p + "dn_out_norm"), full[p + "dn_w_out"])
        elif kind == "sb":
            args = (full[p + "sb_w_qkv"], vec(p + "sb_q_norm"), vec(p + "sb_k_norm"), full[p + "sb_w_out"])
        else:
            w_down, w_uq, w_ukv = _mla_layout(full[p + "mla_w_down"], full[p + "mla_w_uq"], full[p + "mla_w_ukv"])
            args = (w_down, vec(p + "mla_q_a_norm"), vec(p + "mla_kv_a_norm"), w_uq, w_ukv, vec(p + "mla_q_nope_norm"),
                    _pad_lanes(a[p + "mla_q_rope_norm"]), vec(p + "mla_k_nope_norm"), _pad_lanes(a[p + "mla_k_rope_norm"]),
                    full[p + "mla_w_out"], cos, sin)
        return kind, args

    fwd = {"dn": _dn_fwd, "sb": _sb_fwd, "mla": _mla_fwd}
    bwd = {"dn": _dn_bwd, "sb": _sb_bwd, "mla": _mla_bwd}
    saved, layer_args = [], []
    for i in range(N_LAYERS):
        p = "l%d_" % i
        if i > 0:
            unpack(pushed_groups[i], _push_wait("gather_wait_l%d" % i, gathers[i], xs, False))
        kind, args = mixer_args(i)
        layer_args.append((kind, args))
        gain = vec(p + "mix_norm") + started if i == 0 else vec(p + "mix_norm")
        h = _rmsnorm_fwd("mix_norm", xs, gain)
        x_mid, sv_mix = fwd[kind](xs, h, *args)
        if i == 0:
            unpack(pushed_groups[0], _push_wait("gather_wait_l0", gathers[0], x_mid, False))
        x_out, sv_ffn = _ffn_fwd(x_mid, vec(p + "ffn_norm"), full[p + "ffn_w_gate_up"], full[p + "ffn_w_down"])
        saved.append((xs, sv_mix, sv_ffn))
        xs = x_out
    dy, loss_part = _loss_head(xs, target)

    grads, big_out = {}, {}

    def grad_shards(n):
        g, (rs, cs) = grads[n], a[n].shape
        if g.ndim == 3:
            return g
        if _BIG_AXIS[n[3:]] == 0:
            return g.reshape(N_DEV, rs, cs)
        return _shards_from_cols(g, cs)

    def push_grads(tag, groups):
        owns, sends = [], []
        for grp in groups:
            parts = [jnp.pad(grad_shards(n), ((0, 0), (0, 0), (0, grp[0] - a[n].shape[1]))) for n in grp[1]]
            g_all = jnp.concatenate(parts, axis=1)
            owns.append(lax.dynamic_index_in_dim(g_all, dev, 0, keepdims=False))
            sends.append(g_all.astype(BF16))
        lands = [lax.empty(s.shape, s.dtype) for s in sends]
        return tag, groups, owns, _push_start("grads_start_" + tag, sends, lands, True)

    def finish_grads(push, after):
        tag, groups, owns, pushed = push
        recvs = _push_wait("grads_wait_" + tag, pushed, after, True)
        for grp, own, recv in zip(groups, owns, recvs):
            packs = [_stack_group(grp, lambda n, pre=pre: a[pre + n]) for pre in ("", "m_", "v_")]
            outs = [_unstack_group(grp, o, lambda n: a[n].shape) for o in _sum_adam_devices(own, recv, dev_arr, *packs)]
            for j, n in enumerate(grp[1]):
                big_out[n] = [o[j] for o in outs]

    mixer_push = None
    for i in reversed(range(N_LAYERS)):
        p = "l%d_" % i
        kind, args = layer_args[i]
        x_in, sv_mix, sv_ffn = saved[i]
        gain = vec(p + "ffn_norm") if mixer_push is None else vec(p + "ffn_norm") + mixer_push[3][-1][0, 0]
        dx_mid, grads[p + "ffn_norm"], grads[p + "ffn_w_gate_up"], grads[p + "ffn_w_down"] = _ffn_bwd(
            sv_ffn, dy, gain, full[p + "ffn_w_gate_up"], full[p + "ffn_w_down"])
        ffn_push = push_grads("l%d_ffn" % i, LAYER_GROUPS[i][:N_FFN_GROUPS])
        if mixer_push is not None:
            finish_grads(mixer_push, dx_mid)
        res = bwd[kind](sv_mix, dx_mid, *args)
        dh = res[0]
        if kind == "mla":
            res = list(res)
            res[1], res[4], res[5] = _mla_unlayout(res[1], res[4], res[5])
        for n, g in zip(_MIXER_PARAMS[kind], res[1:]):
            grads[p + n] = g
        dy, grads[p + "mix_norm"] = _rmsnorm_bwd("mix_norm_bwd", x_in, vec(p + "mix_norm") + ffn_push[3][-1][0, 0], dh, dx_mid)
        mixer_push = push_grads("l%d_mix" % i, LAYER_GROUPS[i][N_FFN_GROUPS:])
        finish_grads(ffn_push, dy)
    grad_x = dy.reshape(a["x"].shape)

    small_full_shapes = [full[n].shape if n in CONV else a[n].shape for n in SMALL]
    small_grads = []
    for n, s in zip(SMALL, small_full_shapes):
        g = grads[n].reshape(-1)
        small_grads.append(g[:math.prod(s)])
    small_pack = _pack(small_grads + [loss_part.reshape(-1)], LANES, 8)
    small_sum = _sum_devices(_all_gather("gather_small_grads", small_pack))
    small_red = _unpack(small_sum, small_full_shapes + [(LANES,)])
    loss = small_red[-1][0]
    g_small = {}
    for n, g in zip(SMALL, small_red[:-1]):
        if n in CONV:
            cs = a[n].shape[1]
            g = lax.dynamic_slice_in_dim(g, dev * cs, cs, axis=1)
        g_small[n] = g
    small_shapes = [a[n].shape for n in SMALL]
    packs = [_pack([src[n] for n in SMALL], LANES, 8) for src in
             ({n: a[n] for n in SMALL}, g_small, {n: a["m_" + n] for n in SMALL}, {n: a["v_" + n] for n in SMALL})]
    d_small, m_small, v_small = (_unpack(o, small_shapes) for o in _adam_small(*packs))
    finish_grads(mixer_push, small_sum)

    small_out = dict(zip(SMALL, zip([g_small[n] for n in SMALL], d_small, m_small, v_small)))

    def out(k):
        return [small_out[n][k] if n in small_out else big_out[n][k] for n in WEIGHTS]

    return (loss, grad_x, *out(0), *out(1), *out(2), *out(3))
```

```python
import math

import jax
import jax.numpy as jnp
from jax import lax
from jax.experimental import pallas as pl
from jax.experimental.pallas import tpu as pltpu

F32 = jnp.float32
BF16 = jnp.bfloat16
GRAD_DTYPE = BF16
F32X3 = lax.Precision.HIGH

LANES = 128
N_DEV = 8
N_HEADS = 8
HEAD = 128
NORM_EPS = 1e-6
DN_CHUNK = 64
ATT_BLOCK = 512
ATT_Q = 512
MLA_ROPE = 64
MLA_QK = 192
ROPE_THETA = 10000.0
VMEM_LIMIT = 56 * 1024 * 1024
MM_TILE_BYTES = 32 * 1024 * 1024

ADAM_LR = 0.001
ADAM_B1 = 0.9
ADAM_B2 = 0.999
ADAM_EPS = 1e-08
ADAM_WD = 0.01
ADAM_STEP = 10


def _cparams(**kw):
    return pltpu.CompilerParams(vmem_limit_bytes=VMEM_LIMIT, **kw)


def _pick(n, cands):
    for c in cands:
        if c <= n and n % c == 0:
            return c
    return n


def _mm(name, a, b, *, ta=False, tb=False, out_dtype=F32, add=None, tm=None, tn=None, tk=None):
    if ta:
        K, M = a.shape
    else:
        M, K = a.shape
    N = b.shape[0] if tb else b.shape[1]
    tm = tm or _pick(M, (1024, 512, 256, 128))
    tn = tn or _pick(N, (1024, 512, 384, 256, 128))
    if tk is None:
        fits = [c for c in (4096, 2048, 1408, 1024, 512, 384, 256, 128)
                if c <= K and K % c == 0 and 2 * c * (tm * a.dtype.itemsize + tn * b.dtype.itemsize) <= MM_TILE_BYTES]
        tk = fits[0] if fits else K
    return _mm_raw(
        name, a, b, ta=ta, tb=tb, out_dtype=out_dtype, add=add, grid=(M // tm, N // tn, K // tk), out_shape=(M, N),
        a_block=(tk, tm) if ta else (tm, tk), a_map=(lambda i, j, k: (k, i)) if ta else (lambda i, j, k: (i, k)),
        b_block=(tn, tk) if tb else (tk, tn), b_map=(lambda i, j, k: (j, k)) if tb else (lambda i, j, k: (k, j)),
        o_block=(tm, tn), o_map=lambda i, j, k: (i, j))


def _mm_raw(name, a, b, *, ta, tb, out_dtype, add, grid, out_shape, a_block, a_map, b_block, b_map, o_block, o_map):
    nk = grid[2]
    tm, tn = o_block
    dn = (((0 if ta else 1,), (1 if tb else 0,)), ((), ()))
    has_add = add is not None

    def kern(*refs):
        if has_add:
            a_ref, b_ref, add_ref, o_ref, acc_ref = refs
        else:
            a_ref, b_ref, o_ref, acc_ref = refs
        k = pl.program_id(2)
        part = lax.dot_general(a_ref[...].astype(BF16), b_ref[...].astype(BF16), dn, preferred_element_type=F32)

        @pl.when(k == 0)
        def _():
            acc_ref[...] = part

        @pl.when(k > 0)
        def _():
            acc_ref[...] += part

        @pl.when(k == nk - 1)
        def _():
            r = acc_ref[...]
            if has_add:
                r = r + add_ref[...]
            o_ref[...] = r.astype(out_dtype)

    in_specs = [pl.BlockSpec(a_block, a_map), pl.BlockSpec(b_block, b_map)]
    args = [a, b]
    if has_add:
        in_specs.append(pl.BlockSpec(o_block, o_map))
        args.append(add)
    return pl.pallas_call(
        kern, name=name,
        grid=grid,
        in_specs=in_specs,
        out_specs=pl.BlockSpec(o_block, o_map),
        out_shape=jax.ShapeDtypeStruct(out_shape, out_dtype),
        scratch_shapes=[pltpu.VMEM((tm, tn), F32)],
        compiler_params=_cparams(dimension_semantics=("parallel", "parallel", "arbitrary")),
    )(*args)


def _rows(name, body, ins, outs, *, tt, consts=(), accs=()):
    in_specs, args = [], []
    first = ins[0][0] if isinstance(ins[0], tuple) else ins[0]
    t = first.shape[-2]
    tt = min(tt, t)
    for x in ins:
        if isinstance(x, tuple):
            arr, bs, im = x
            in_specs.append(pl.BlockSpec(bs, im))
            args.append(arr)
        else:
            in_specs.append(_row_spec(x.shape, tt))
            args.append(x)
    for c in consts:
        in_specs.append(pl.BlockSpec(c.shape, lambda i, _n=c.ndim: (0,) * _n))
        args.append(c)
    out_specs = [_row_spec(o.shape, tt) for o in outs]
    out_specs += [pl.BlockSpec(a.shape, lambda i, _n=len(a.shape): (0,) * _n) for a in accs]
    res = pl.pallas_call(
        body, name=name, grid=(t // tt,),
        in_specs=in_specs, out_specs=out_specs, out_shape=list(outs) + list(accs),
        compiler_params=_cparams(dimension_semantics=("arbitrary",)),
    )(*args)
    return res


def _row_spec(shape, tt):
    if len(shape) == 2:
        return pl.BlockSpec((tt, shape[1]), lambda i: (i, 0))
    return pl.BlockSpec((shape[0], tt, shape[2]), lambda i: (0, i, 0))


def _sds(shape, dtype=F32):
    return jax.ShapeDtypeStruct(tuple(shape), dtype)


def _acc(ref, val):
    i = pl.program_id(0)

    @pl.when(i == 0)
    def _():
        ref[...] = val

    @pl.when(i > 0)
    def _():
        ref[...] += val


def _rms(x, g):
    return x * lax.rsqrt(jnp.mean(x * x, axis=-1, keepdims=True) + NORM_EPS) * g


def _silu(x):
    return x / (1.0 + jnp.exp(-x))


def _softplus(x):
    return jnp.maximum(x, 0.0) + jnp.log(1.0 + jnp.exp(-jnp.abs(x)))


def _sigmoid(x):
    return 1.0 / (1.0 + jnp.exp(-x))


def _rmsnorm_fwd(name, x, g, tt=512):
    def body(x_ref, g_ref, h_ref):
        h_ref[...] = _rms(x_ref[...], g_ref[...]).astype(BF16)

    return _rows(name, body, [x], [_sds(x.shape, BF16)], tt=tt, consts=[g])[0]


def _rmsnorm_bwd(name, x, g, dh, dres, tt=512):
    def body(x_ref, dh_ref, dres_ref, g_ref, dx_ref, dg_ref):
        _, vjp = jax.vjp(_rms, x_ref[...], g_ref[...])
        dx, dg = vjp(dh_ref[...])
        dx_ref[...] = dx + dres_ref[...]
        _acc(dg_ref, dg)

    return _rows(name, body, [x, dh, dres], [_sds(x.shape)], tt=tt, consts=[g], accs=[_sds(g.shape)])


def _ffn_fwd(x, norm_g, w3, w_down):
    t, d = x.shape
    ns, _, cs = w3.shape
    half = ns // 2
    w2 = w3.reshape(ns * d, cs)
    h = _rmsnorm_fwd("ffn_norm", x, norm_g)
    tm = _pick(t, (1024, 512, 256, 128))
    nm = t // tm

    def gate_up(h_ref, wg_ref, wu_ref, g_ref, u_ref, a_ref):
        hv = h_ref[...]
        g = _dot(hv, wg_ref[...])
        u = _dot(hv, wu_ref[...])
        g_ref[...] = g
        u_ref[...] = u
        a_ref[...] = (_silu(g) * u).astype(BF16)

    hid = pl.BlockSpec((tm, cs), lambda j, i: (j * nm + i, 0))
    g, u, act = pl.pallas_call(
        gate_up, name="ffn_gate_up", grid=(half, nm),
        in_specs=[pl.BlockSpec((tm, d), lambda j, i: (i, 0)), pl.BlockSpec((d, cs), lambda j, i: (j, 0)),
                  pl.BlockSpec((d, cs), lambda j, i: (j + half, 0))],
        out_specs=[hid, hid, hid], out_shape=[_sds((half * t, cs)), _sds((half * t, cs)), _sds((half * t, cs), BF16)],
        compiler_params=_cparams(dimension_semantics=("parallel", "arbitrary")),
    )(h, w2, w2)
    def down(a_ref, w_ref, x_ref, y_ref):
        y = x_ref[...]
        for j in range(half):
            y = y + _dot(a_ref[j], w_ref[j])
        y_ref[...] = y

    y = pl.pallas_call(
        down, name="ffn_down", grid=(nm,),
        in_specs=[pl.BlockSpec((half, tm, cs), lambda i: (0, i, 0)), pl.BlockSpec((half, cs, d), lambda i: (0, 0, 0)),
                  pl.BlockSpec((tm, d), lambda i: (i, 0))],
        out_specs=pl.BlockSpec((tm, d), lambda i: (i, 0)), out_shape=_sds((t, d)),
        compiler_params=_cparams(dimension_semantics=("parallel",)),
    )(act.reshape(half, t, cs), w_down.reshape(half, cs, d), x)
    return y, (x, h, g, u, act)


def _ffn_bwd(saved, dy, norm_g, w3, w_down):
    x, h, g, u, act = saved
    t, d = x.shape
    ns, _, cs = w3.shape
    half = ns // 2
    w2 = w3.reshape(ns * d, cs)
    tm = _pick(t, (1024, 512, 256, 128))
    nm = t // tm
    tk = _pick(t, (4096, 2048, 1024, 512, 256, 128))
    nk = t // tk
    d_wdown = _mm_raw("ffn_down_wgrad", act, dy, ta=True, tb=False, out_dtype=GRAD_DTYPE, add=None, grid=(half, 1, nk),
                      out_shape=(half * cs, d), a_block=(tk, cs), a_map=lambda i, j, k: (i * nk + k, 0),
                      b_block=(tk, d), b_map=lambda i, j, k: (k, 0), o_block=(cs, d), o_map=lambda i, j, k: (i, 0))
    def down_dgrad(dy_ref, wd_ref, g_ref, u_ref, dg_ref, du_ref):
        da = _dot_nt(dy_ref[...].astype(BF16), wd_ref[...])
        gv, uv = g_ref[...], u_ref[...]
        s = _sigmoid(gv)
        dg_ref[...] = (da * uv * s * (1.0 + gv * (1.0 - s))).astype(BF16)
        du_ref[...] = (da * gv * s).astype(BF16)

    hid = pl.BlockSpec((tm, cs), lambda j, i: (j * nm + i, 0))
    dg, du = pl.pallas_call(
        down_dgrad, name="ffn_down_dgrad", grid=(half, nm),
        in_specs=[pl.BlockSpec((tm, d), lambda j, i: (i, 0)), pl.BlockSpec((cs, d), lambda j, i: (j, 0)), hid, hid],
        out_specs=[hid, hid], out_shape=[_sds((half * t, cs), BF16), _sds((half * t, cs), BF16)],
        compiler_params=_cparams(dimension_semantics=("parallel", "arbitrary")),
    )(dy, w_down, g, u)

    def wgrad(name, dd):
        return _mm_raw(name, h, dd, ta=True, tb=False, out_dtype=GRAD_DTYPE, add=None, grid=(1, half, nk), out_shape=(half * d, cs),
                       a_block=(tk, d), a_map=lambda i, j, k: (k, 0), b_block=(tk, cs), b_map=lambda i, j, k: (j * nk + k, 0),
                       o_block=(d, cs), o_map=lambda i, j, k: (j, 0))

    def gate_up_dgrad(dg_ref, du_ref, w_ref, dh_ref):
        dh = _dot_nt(dg_ref[0], w_ref[0]) + _dot_nt(du_ref[0], w_ref[half])
        for j in range(1, half):
            dh = dh + _dot_nt(dg_ref[j], w_ref[j]) + _dot_nt(du_ref[j], w_ref[half + j])
        dh_ref[...] = dh

    th = _pick(t, (512, 256, 128))
    hid3 = pl.BlockSpec((half, th, cs), lambda i: (0, i, 0))
    dh = pl.pallas_call(
        gate_up_dgrad, name="ffn_gate_up_dgrad", grid=(t // th,),
        in_specs=[hid3, hid3, pl.BlockSpec((ns, d, cs), lambda i: (0, 0, 0))],
        out_specs=pl.BlockSpec((th, d), lambda i: (i, 0)), out_shape=_sds((t, d)),
        compiler_params=_cparams(dimension_semantics=("parallel",)),
    )(dg.reshape(half, t, cs), du.reshape(half, t, cs), w3)
    d_w3 = jnp.concatenate([wgrad("ffn_gate_wgrad", dg), wgrad("ffn_up_wgrad", du)], axis=0).reshape(ns, d, cs)
    dx, dgain = _rmsnorm_bwd("ffn_norm_bwd", x, norm_g, dh, dy)
    return dx, dgain, d_w3, d_wdown


def _dot_nt(a, b):
    return lax.dot_general(a, b, (((1,), (1,)), ((), ())), preferred_element_type=F32)


def _dot_tn(a, b):
    return lax.dot_general(a, b, (((0,), (0,)), ((), ())), preferred_element_type=F32)


def _dot(a, b):
    return jnp.dot(a, b, preferred_element_type=F32)


CUM_BLOCK = 128


def _tri2(lower):
    r = lax.broadcasted_iota(jnp.int32, (CUM_BLOCK, CUM_BLOCK), 0)
    c = lax.broadcasted_iota(jnp.int32, (CUM_BLOCK, CUM_BLOCK), 1)
    tri = ((r > c) if lower else (r < c)).astype(BF16)
    return jnp.concatenate([tri, tri], axis=0)


def _run_sums(x, tri2, run, reverse):
    nb = x.shape[1] // CUM_BLOCK
    outs = [None] * nb
    for j in (reversed(range(nb)) if reverse else range(nb)):
        xj = x[:, j * CUM_BLOCK:(j + 1) * CUM_BLOCK]
        hi = xj.astype(BF16)
        lo = (xj - hi.astype(F32)).astype(BF16)
        outs[j] = _dot(jnp.concatenate([hi, lo], axis=1), tri2) + run
        run = run + jnp.sum(xj, axis=1, keepdims=True)
    return jnp.concatenate(outs, axis=1), run


def _log_sigmoid(z):
    return jnp.minimum(z, 0.0) - jnp.log(1.0 + jnp.exp(-jnp.abs(z)))


def _heads_in(ref, h, width=HEAD):
    return ref[:, h * width:(h + 1) * width]


def _sb_qk(q, k, gq, gk):
    return _rms(q, gq) * (HEAD ** -0.5), _rms(k, gk)


def _sb_prep_fwd(qkv, gq, gk):
    t = qkv.shape[0]

    def body(x_ref, gq_ref, gk_ref, q_ref, k_ref, v_ref):
        for h in range(N_HEADS):
            q, k = _sb_qk(_heads_in(x_ref, h), _heads_in(x_ref, N_HEADS + h), gq_ref[...], gk_ref[...])
            q_ref[h] = q.astype(BF16)
            k_ref[h] = k.astype(BF16)
            v_ref[h] = _heads_in(x_ref, 2 * N_HEADS + h).astype(BF16)

    hm = _sds((N_HEADS, t, HEAD), BF16)
    return _rows("sb_prep", body, [qkv], [hm, hm, hm], tt=256, consts=[gq, gk])


def _sb_prep_bwd(qkv, gq, gk, dq, dk, dv):
    def body(x_ref, dq_ref, dk_ref, dv_ref, gq_ref, gk_ref, dx_ref, dgq_ref, dgk_ref):
        dgq = jnp.zeros(gq_ref.shape, F32)
        dgk = jnp.zeros(gk_ref.shape, F32)
        for h in range(N_HEADS):
            _, vjp = jax.vjp(_sb_qk, _heads_in(x_ref, h), _heads_in(x_ref, N_HEADS + h), gq_ref[...], gk_ref[...])
            a, b, c, d = vjp((dq_ref[h], dk_ref[h]))
            dx_ref[:, h * HEAD:(h + 1) * HEAD] = a.astype(BF16)
            dx_ref[:, (N_HEADS + h) * HEAD:(N_HEADS + h + 1) * HEAD] = b.astype(BF16)
            dx_ref[:, (2 * N_HEADS + h) * HEAD:(2 * N_HEADS + h + 1) * HEAD] = dv_ref[h].astype(BF16)
            dgq, dgk = dgq + c, dgk + d
        _acc(dgq_ref, dgq)
        _acc(dgk_ref, dgk)

    return _rows("sb_prep_bwd", body, [qkv, dq, dk, dv], [_sds(qkv.shape, BF16)], tt=256, consts=[gq, gk],
                 accs=[_sds(gq.shape), _sds(gk.shape)])


def _q_block(t):
    return min(ATT_Q, t)


def _key_order(bq, qb):
    rows = lax.broadcasted_iota(jnp.int32, (bq, ATT_BLOCK), 0)
    cols = lax.broadcasted_iota(jnp.int32, (bq, ATT_BLOCK), 1)
    return rows - cols + qb * bq


def _sb_attn_fwd(q, k, v):
    nh, t, _ = q.shape
    bq = _q_block(t)
    per = bq // ATT_BLOCK

    def kern(q_ref, k_ref, v_ref, o_ref):
        qb = pl.program_id(1)
        qv = q_ref[0]
        after = _tri2(True)
        order = _key_order(bq, qb)
        nkb = (qb + 1) * per

        def body(i, carry, diagonal):
            o_acc, run = carry
            kb = nkb - 1 - i
            off = pl.multiple_of(kb * ATT_BLOCK, ATT_BLOCK)
            kv = k_ref[0, pl.ds(off, ATT_BLOCK), :]
            vv = v_ref[0, pl.ds(off, ATT_BLOCK), :]
            z = _dot_nt(qv, kv)
            lsz = _log_sigmoid(z)
            lsn = lsz - z
            if diagonal:
                past = order > kb * ATT_BLOCK
                lsn = jnp.where(past, lsn, 0.0)
            la, run = _run_sums(lsn, after, run, True)
            a = jnp.exp(lsz + la)
            if diagonal:
                a = jnp.where(past, a, 0.0)
            o_acc = o_acc + _dot(a.astype(BF16), vv)
            return o_acc, run

        carry = lax.fori_loop(0, per, lambda i, c: body(i, c, True), (jnp.zeros((bq, HEAD), F32), jnp.zeros((bq, 1), F32)))
        o, _ = lax.fori_loop(per, nkb, lambda i, c: body(i, c, False), carry)
        o_ref[...] = o

    return pl.pallas_call(
        kern, name="sb_attn_fwd", grid=(nh, t // bq),
        in_specs=[pl.BlockSpec((1, bq, HEAD), lambda h, i: (h, i, 0)),
                  pl.BlockSpec((1, t, HEAD), lambda h, i: (h, 0, 0)),
                  pl.BlockSpec((1, t, HEAD), lambda h, i: (h, 0, 0))],
        out_specs=pl.BlockSpec((bq, HEAD), lambda h, i: (i, h)),
        out_shape=_sds((t, nh * HEAD)),
        compiler_params=_cparams(dimension_semantics=("parallel", "arbitrary")),
    )(q, k, v)


def _sb_attn_bwd(q, k, v, do):
    nh, t, _ = q.shape
    bq = _q_block(t)
    per = bq // ATT_BLOCK

    def kern(q_ref, k_ref, v_ref, do_ref, dq_ref, dk_ref, dv_ref, g_s, ls_s):
        qb = pl.program_id(1)

        @pl.when(qb == 0)
        def _():
            dk_ref[...] = jnp.zeros(dk_ref.shape, F32)
            dv_ref[...] = jnp.zeros(dv_ref.shape, F32)

        qv = q_ref[0]
        dob = do_ref[...].astype(BF16)
        after, before = _tri2(True), _tri2(False)
        order = _key_order(bq, qb)
        nkb = (qb + 1) * per

        def sweep_left(i, run, diagonal):
            kb = nkb - 1 - i
            off = pl.multiple_of(kb * ATT_BLOCK, ATT_BLOCK)
            kv = k_ref[0, pl.ds(off, ATT_BLOCK), :]
            vv = v_ref[0, pl.ds(off, ATT_BLOCK), :]
            z = _dot_nt(qv, kv)
            lsz = _log_sigmoid(z)
            lsn = lsz - z
            if diagonal:
                past = order > kb * ATT_BLOCK
                lsn = jnp.where(past, lsn, 0.0)
            la, run = _run_sums(lsn, after, run, True)
            a = jnp.exp(lsz + la)
            if diagonal:
                a = jnp.where(past, a, 0.0)
            g_s[kb] = _dot_nt(dob, vv) * a
            ls_s[kb] = lsz
            dv_ref[0, pl.ds(off, ATT_BLOCK), :] += _dot_tn(a.astype(BF16), dob)
            return run

        zero = jnp.zeros((bq, 1), F32)
        run = lax.fori_loop(0, per, lambda i, c: sweep_left(i, c, True), zero)
        lax.fori_loop(per, nkb, lambda i, c: sweep_left(i, c, False), run)

        def sweep_right(kb, carry, diagonal):
            dq_acc, run_g = carry
            off = pl.multiple_of(kb * ATT_BLOCK, ATT_BLOCK)
            kv = k_ref[0, pl.ds(off, ATT_BLOCK), :]
            g = g_s[kb]
            sg = jnp.exp(ls_s[kb])
            dls, run_g = _run_sums(g, before, run_g, False)
            dz = g * (1.0 - sg) - dls * sg
            if diagonal:
                dz = jnp.where(order > kb * ATT_BLOCK, dz, 0.0)
            dzb = dz.astype(BF16)
            dk_ref[0, pl.ds(off, ATT_BLOCK), :] += _dot_tn(dzb, qv)
            return dq_acc + _dot(dzb, kv), run_g

        carry = lax.fori_loop(0, nkb - per, lambda i, c: sweep_right(i, c, False), (jnp.zeros((bq, HEAD), F32), zero))
        dq, _ = lax.fori_loop(nkb - per, nkb, lambda i, c: sweep_right(i, c, True), carry)
        dq_ref[0] = dq

    hm = _sds((nh, t, HEAD))
    full = pl.BlockSpec((1, t, HEAD), lambda h, i: (h, 0, 0))
    tok = pl.BlockSpec((bq, HEAD), lambda h, i: (i, h))
    nkb_max = t // ATT_BLOCK
    return pl.pallas_call(
        kern, name="sb_attn_bwd", grid=(nh, t // bq),
        in_specs=[pl.BlockSpec((1, bq, HEAD), lambda h, i: (h, i, 0)), full, full, tok],
        out_specs=[pl.BlockSpec((1, bq, HEAD), lambda h, i: (h, i, 0)), full, full],
        out_shape=[hm, hm, hm],
        scratch_shapes=[pltpu.VMEM((nkb_max, bq, ATT_BLOCK), F32), pltpu.VMEM((nkb_max, bq, ATT_BLOCK), F32)],
        compiler_params=_cparams(dimension_semantics=("parallel", "arbitrary")),
    )(q, k, v, do)


def _sb_fwd(x, h, w_qkv, gq, gk, w_out):
    qkv = _mm("sb_qkv", h, w_qkv)
    q, k, v = _sb_prep_fwd(qkv, gq, gk)
    o = _sb_attn_fwd(q, k, v)
    y = _mm("sb_out", o, w_out, add=x)
    return y, (h, qkv, q, k, v, o)


def _sb_bwd(saved, dy, w_qkv, gq, gk, w_out):
    h, qkv, q, k, v, o = saved
    d_wout = _mm("sb_out_wgrad", o, dy, ta=True, out_dtype=GRAD_DTYPE)
    do = _mm("sb_out_dgrad", dy, w_out, tb=True)
    dq, dk, dv = _sb_attn_bwd(q, k, v, do)
    dqkv, dgq, dgk = _sb_prep_bwd(qkv, gq, gk, dq, dk, dv)
    d_wqkv = _mm("sb_qkv_wgrad", h, dqkv, ta=True, out_dtype=GRAD_DTYPE)
    dh = _mm("sb_qkv_dgrad", dqkv, w_qkv, tb=True)
    return dh, d_wqkv, dgq, dgk, d_wout


DN_QKV = 3 * N_HEADS * HEAD
DN_PROJ = DN_QKV + N_HEADS * HEAD + LANES
DN_CONV = 4
HALO = 8
CONV_COLS = 512


def _dn_conv_fwd(proj, conv_w, tt=256):
    t = proj.shape[0]
    tt = min(tt, t)

    def body(u_ref, prev_ref, w_ref, c_ref):
        i = pl.program_id(0)
        for cc in range(DN_QKV // CONV_COLS):
            cs = slice(cc * CONV_COLS, (cc + 1) * CONV_COLS)
            cur = u_ref[:, cs]
            prev = jnp.where(i > 0, prev_ref[:, cs], 0.0)
            ext = jnp.concatenate([prev, cur], axis=0)
            y = cur * w_ref[DN_CONV - 1:DN_CONV, cs]
            for j in range(DN_CONV - 1):
                y = y + pltpu.roll(ext, DN_CONV - 1 - j, 0)[HALO:] * w_ref[j:j + 1, cs]
            c_ref[:, cs] = y

    return _rows("dn_conv", body,
                 [(proj, (tt, DN_QKV), lambda i: (i, 0)),
                  (proj, (HALO, DN_QKV), lambda i: (jnp.maximum(i * (tt // HALO) - 1, 0), 0))],
                 [_sds((t, DN_QKV))], tt=tt, consts=[conv_w])[0]


def _dn_conv_bwd(proj, conv_w, dc, dz, dab, tt=256):
    t = proj.shape[0]
    tt = min(tt, t)
    nblk = t // tt

    def body(u_ref, prev_ref, dc_ref, next_ref, dz_ref, dab_ref, w_ref, dp_ref, dw_ref):
        i = pl.program_id(0)
        dws = []
        for cc in range(DN_QKV // CONV_COLS):
            cs = slice(cc * CONV_COLS, (cc + 1) * CONV_COLS)
            cur = u_ref[:, cs]
            prev = jnp.where(i > 0, prev_ref[:, cs], 0.0)
            ext_u = jnp.concatenate([prev, cur], axis=0)
            d = dc_ref[:, cs]
            nxt = jnp.where(i < nblk - 1, next_ref[:, cs], 0.0)
            ext_d = jnp.concatenate([d, nxt], axis=0)
            du = d * w_ref[DN_CONV - 1:DN_CONV, cs]
            rows = [jnp.sum(d * cur, axis=0, keepdims=True)]
            for j in range(DN_CONV - 2, -1, -1):
                sh = DN_CONV - 1 - j
                du = du + pltpu.roll(ext_d, tt + HALO - sh, 0)[:tt] * w_ref[j:j + 1, cs]
                rows.insert(0, jnp.sum(d * pltpu.roll(ext_u, sh, 0)[HALO:], axis=0, keepdims=True))
            dp_ref[:, cs] = du.astype(BF16)
            dws.append(jnp.concatenate(rows, axis=0))
        dp_ref[:, DN_QKV:DN_QKV + N_HEADS * HEAD] = dz_ref[...].astype(BF16)
        dp_ref[:, DN_QKV + N_HEADS * HEAD:] = dab_ref[...].astype(BF16)
        _acc(dw_ref, jnp.concatenate(dws, axis=1))

    return _rows("dn_conv_bwd", body,
                 [(proj, (tt, DN_QKV), lambda i: (i, 0)),
                  (proj, (HALO, DN_QKV), lambda i: (jnp.maximum(i * (tt // HALO) - 1, 0), 0)),
                  dc,
                  (dc, (HALO, DN_QKV), lambda i: (jnp.minimum((i + 1) * (tt // HALO), t // HALO - 1), 0)),
                  dz, dab],
                 [_sds((t, DN_PROJ), BF16)], tt=tt, consts=[conv_w], accs=[_sds(conv_w.shape)])


def _l2n(x):
    return x * lax.rsqrt(jnp.sum(x * x, axis=-1, keepdims=True) + NORM_EPS)


def _dn_qkv(cq, ck, cv):
    return _l2n(_silu(cq)) * (HEAD ** -0.5), _l2n(_silu(ck)), _silu(cv)


def _dn_gates(ab, a_log, dt_bias):
    lane = lax.broadcasted_iota(jnp.int32, ab.shape, 1)
    g = -jnp.exp(a_log) * _softplus(ab + dt_bias)
    return jnp.where(lane < N_HEADS, g, jnp.where(lane < 2 * N_HEADS, _sigmoid(ab), 0.0))


def _ab_spec(tt):
    return (tt, LANES), lambda i: (i, DN_PROJ // LANES - 1)


def _dn_prep_fwd(c, proj, a_log, dt_bias, tt=256):
    t = c.shape[0]
    tt = min(tt, t)

    def body(c_ref, ab_ref, al_ref, dt_ref, q_ref, k_ref, v_ref, g_ref):
        for h in range(N_HEADS):
            q_ref[h], k_ref[h], v_ref[h] = _dn_qkv(_heads_in(c_ref, h), _heads_in(c_ref, N_HEADS + h), _heads_in(c_ref, 2 * N_HEADS + h))
        g_ref[...] = _dn_gates(ab_ref[...], al_ref[...], dt_ref[...])

    hm = _sds((N_HEADS, t, HEAD))
    return _rows("dn_prep", body, [c, (proj,) + _ab_spec(tt)], [hm, hm, hm, _sds((t, LANES))], tt=tt, consts=[a_log, dt_bias])


def _dn_prep_bwd(c, proj, a_log, dt_bias, dq, dk, dv, dgates, tt=256):
    t = c.shape[0]
    tt = min(tt, t)

    def body(c_ref, ab_ref, dq_ref, dk_ref, dv_ref, dg_ref, al_ref, dt_ref, dc_ref, dab_ref, dal_ref, ddt_ref):
        for h in range(N_HEADS):
            _, vjp = jax.vjp(_dn_qkv, _heads_in(c_ref, h), _heads_in(c_ref, N_HEADS + h), _heads_in(c_ref, 2 * N_HEADS + h))
            a, b, d = vjp((dq_ref[h], dk_ref[h], dv_ref[h]))
            dc_ref[:, h * HEAD:(h + 1) * HEAD] = a
            dc_ref[:, (N_HEADS + h) * HEAD:(N_HEADS + h + 1) * HEAD] = b
            dc_ref[:, (2 * N_HEADS + h) * HEAD:(2 * N_HEADS + h + 1) * HEAD] = d
        _, vjp = jax.vjp(_dn_gates, ab_ref[...], al_ref[...], dt_ref[...])
        dab, dal, ddt = vjp(dg_ref[...])
        dab_ref[...] = dab
        _acc(dal_ref, dal)
        _acc(ddt_ref, ddt)

    return _rows("dn_prep_bwd", body, [c, (proj,) + _ab_spec(tt), dq, dk, dv, dgates], [_sds(c.shape), _sds((t, LANES))],
                 tt=tt, consts=[a_log, dt_bias], accs=[_sds(a_log.shape), _sds(dt_bias.shape)])


def _bdot(a, b, prec=None):
    return lax.dot_general(a, b, (((2,), (1,)), ((0,), (0,))), precision=prec, preferred_element_type=F32)


def _bdot_nt(a, b, prec=None):
    return lax.dot_general(a, b, (((2,), (2,)), ((0,), (0,))), precision=prec, preferred_element_type=F32)


def _bdot_tn(a, b, prec=None):
    return lax.dot_general(a, b, (((1,), (1,)), ((0,), (0,))), precision=prec, preferred_element_type=F32)


def _inv_raw(low):
    c = low.shape[-1]
    r = lax.broadcasted_iota(jnp.int32, (c, c), 0)
    s = lax.broadcasted_iota(jnp.int32, (c, c), 1)
    m = jnp.where(r == s, 1.0, 0.0) - low
    p = _bdot(low, low, F32X3)
    n_fac = int(math.log2(c)) - 1
    for i in range(n_fac):
        m = m + _bdot(m, p, F32X3)
        if i < n_fac - 1:
            p = _bdot(p, p, F32X3)
    return m


@jax.custom_vjp
def _inv_unit_lower(low):
    return _inv_raw(low)


def _inv_fwd(low):
    m = _inv_raw(low)
    return m, m


def _inv_bwd(m, dm):
    return (-_bdot_nt(_bdot_tn(m, dm, F32X3), m, F32X3),)


_inv_unit_lower.defvjp(_inv_fwd, _inv_bwd)


def _dn_chunk(q, k, v, gates, s):
    nh, c, _ = q.shape
    lane = lax.broadcasted_iota(jnp.int32, gates.shape, 1)
    def column(j):
        return jnp.sum(jnp.where(lane == j, gates, 0.0), axis=1, keepdims=True)[None]

    g_col = jnp.concatenate([column(h) for h in range(nh)], axis=0)
    b_col = jnp.concatenate([column(h + nh) for h in range(nh)], axis=0)
    r = lax.broadcasted_iota(jnp.int32, (c, c), 0)
    cc = lax.broadcasted_iota(jnp.int32, (c, c), 1)
    causal, strict = r >= cc, r > cc
    incl = jnp.broadcast_to(jnp.where(causal, 1.0, 0.0), (nh, c, c))
    upper = jnp.broadcast_to(jnp.where(r <= cc, 1.0, 0.0), (nh, c, c))
    gb = jnp.broadcast_to(g_col, (nh, c, LANES))
    gbc = jnp.broadcast_to(g_col, (nh, c, c))
    gc = _bdot(incl, gb, F32X3)
    gc_r = _bdot(incl, gbc, F32X3)
    gc_c = _bdot_tn(gbc, upper, F32X3)
    decay = jnp.where(causal, jnp.exp(jnp.where(causal, gc_r - gc_c, 0.0)), 0.0)
    kb = k * b_col
    low = jnp.where(strict, _bdot_nt(kb, k) * decay, 0.0)
    m = _inv_unit_lower(low)
    egc = jnp.exp(gc)
    u = _bdot(m, v * b_col, F32X3)
    w = _bdot(m, kb * egc, F32X3)
    attn = _bdot_nt(q, k) * decay
    gl = jnp.sum(gb, axis=1, keepdims=True)
    v_new = u - _bdot(w, s)
    o = _bdot(q * egc, s) + _bdot(attn, v_new)
    s_new = s * jnp.exp(gl) + _bdot_tn(k * jnp.exp(gl - gc), v_new)
    return o, s_new


def _dn_chunks_fwd(q, k, v, gates):
    nh, t, _ = q.shape
    n = t // DN_CHUNK

    def kern(q_ref, k_ref, v_ref, g_ref, o_ref, sin_ref, s_scr):
        @pl.when(pl.program_id(0) == 0)
        def _():
            s_scr[...] = jnp.zeros(s_scr.shape, F32)

        s = s_scr[...]
        sin_ref[0] = s
        o_ref[...], s_scr[...] = _dn_chunk(q_ref[...], k_ref[...], v_ref[...], g_ref[...], s)

    blk = pl.BlockSpec((nh, DN_CHUNK, HEAD), lambda i: (0, i, 0))
    return pl.pallas_call(
        kern, name="dn_chunks_fwd", grid=(n,),
        in_specs=[blk, blk, blk, pl.BlockSpec((DN_CHUNK, LANES), lambda i: (i, 0))],
        out_specs=[blk, pl.BlockSpec((1, nh, HEAD, HEAD), lambda i: (i, 0, 0, 0))],
        out_shape=[_sds((nh, t, HEAD)), _sds((n, nh, HEAD, HEAD))],
        scratch_shapes=[pltpu.VMEM((nh, HEAD, HEAD), F32)],
        compiler_params=_cparams(dimension_semantics=("arbitrary",)),
    )(q, k, v, gates)


def _dn_chunks_bwd(q, k, v, gates, s_in, do):
    nh, t, _ = q.shape
    n = t // DN_CHUNK

    def kern(q_ref, k_ref, v_ref, g_ref, sin_ref, do_ref, dq_ref, dk_ref, dv_ref, dg_ref, ds_scr):
        @pl.when(pl.program_id(0) == 0)
        def _():
            ds_scr[...] = jnp.zeros(ds_scr.shape, F32)

        _, vjp = jax.vjp(_dn_chunk, q_ref[...], k_ref[...], v_ref[...], g_ref[...], sin_ref[0])
        dq_ref[...], dk_ref[...], dv_ref[...], dg_ref[...], ds_scr[...] = vjp((do_ref[...], ds_scr[...]))

    blk = pl.BlockSpec((nh, DN_CHUNK, HEAD), lambda i: (0, n - 1 - i, 0))
    gblk = pl.BlockSpec((DN_CHUNK, LANES), lambda i: (n - 1 - i, 0))
    hm = _sds((nh, t, HEAD))
    return pl.pallas_call(
        kern, name="dn_chunks_bwd", grid=(n,),
        in_specs=[blk, blk, blk, gblk, pl.BlockSpec((1, nh, HEAD, HEAD), lambda i: (n - 1 - i, 0, 0, 0)), blk],
        out_specs=[blk, blk, blk, gblk],
        out_shape=[hm, hm, hm, _sds((t, LANES))],
        scratch_shapes=[pltpu.VMEM((nh, HEAD, HEAD), F32)],
        compiler_params=_cparams(dimension_semantics=("arbitrary",)),
    )(q, k, v, gates, s_in, do)


def _dn_gate_out(o, z, g):
    return _rms(o, g) * _silu(z)


def _z_spec(tt):
    return (tt, N_HEADS * HEAD), lambda i: (i, DN_QKV // (N_HEADS * HEAD))


def _dn_post_fwd(o, proj, out_norm, tt=256):
    t = o.shape[1]
    tt = min(tt, t)

    def body(o_ref, z_ref, g_ref, y_ref):
        for h in range(N_HEADS):
            y_ref[:, h * HEAD:(h + 1) * HEAD] = _dn_gate_out(o_ref[h], _heads_in(z_ref, h), g_ref[...]).astype(BF16)

    return _rows("dn_post", body, [o, (proj,) + _z_spec(tt)], [_sds((t, N_HEADS * HEAD), BF16)], tt=tt, consts=[out_norm])[0]


def _dn_post_bwd(o, proj, out_norm, dy, tt=256):
    t = o.shape[1]
    tt = min(tt, t)

    def body(o_ref, z_ref, dy_ref, g_ref, do_ref, dz_ref, dg_ref):
        dg = jnp.zeros(g_ref.shape, F32)
        for h in range(N_HEADS):
            _, vjp = jax.vjp(_dn_gate_out, o_ref[h], _heads_in(z_ref, h), g_ref[...])
            a, b, d = vjp(_heads_in(dy_ref, h))
            do_ref[h] = a
            dz_ref[:, h * HEAD:(h + 1) * HEAD] = b
            dg = dg + d
        _acc(dg_ref, dg)

    return _rows("dn_post_bwd", body, [o, (proj,) + _z_spec(tt), dy], [_sds(o.shape), _sds((t, N_HEADS * HEAD))], tt=tt,
                 consts=[out_norm], accs=[_sds(out_norm.shape)])


def _dn_fwd(x, h, w_in, conv_w, a_log, dt_bias, out_norm, w_out):
    proj = _mm("dn_in", h, w_in)
    c = _dn_conv_fwd(proj, conv_w)
    q, k, v, gates = _dn_prep_fwd(c, proj, a_log, dt_bias)
    o, s_in = _dn_chunks_fwd(q, k, v, gates)
    on = _dn_post_fwd(o, proj, out_norm)
    y = _mm("dn_out", on, w_out, add=x)
    return y, (h, proj, c, q, k, v, gates, o, s_in, on)


def _dn_bwd(saved, dy, w_in, conv_w, a_log, dt_bias, out_norm, w_out):
    h, proj, c, q, k, v, gates, o, s_in, on = saved
    d_wout = _mm("dn_out_wgrad", on, dy, ta=True, out_dtype=GRAD_DTYPE)
    don = _mm("dn_out_dgrad", dy, w_out, tb=True)
    do, dz, d_out_norm = _dn_post_bwd(o, proj, out_norm, don)
    dq, dk, dv, dgates = _dn_chunks_bwd(q, k, v, gates, s_in, do)
    dc, dab, d_a_log, d_dt_bias = _dn_prep_bwd(c, proj, a_log, dt_bias, dq, dk, dv, dgates)
    dproj, d_conv_w = _dn_conv_bwd(proj, conv_w, dc, dz, dab)
    d_win = _mm("dn_in_wgrad", h, dproj, ta=True, out_dtype=GRAD_DTYPE)
    dh = _mm("dn_in_dgrad", dproj, w_in, tb=True)
    return dh, d_win, d_conv_w, d_a_log, d_dt_bias, d_out_norm, d_wout


MLA_SCALE = MLA_QK ** -0.5
MLA_C = 512


def _swap_raw(x):
    lane = lax.broadcasted_iota(jnp.int32, x.shape, 1)
    half = MLA_ROPE // 2
    y = jnp.where(lane < half, pltpu.roll(x, LANES - half, 1), pltpu.roll(x, half, 1))
    return jnp.where(lane < MLA_ROPE, y, 0.0)


@jax.custom_vjp
def _swap_halves(x):
    return _swap_raw(x)


_swap_halves.defvjp(lambda x: (_swap_raw(x), None), lambda _, d: (_swap_raw(d),))


def _rms_rope(x, g, cos, sin):
    y = x * lax.rsqrt(jnp.sum(x * x, axis=-1, keepdims=True) * (1.0 / MLA_ROPE) + NORM_EPS) * g
    return y * cos + _swap_halves(y) * sin


def _mla_latent(cq, ckv, kr, gq, gkv, gkr, cos, sin):
    return _rms(cq, gq), _rms(ckv, gkv), _rms_rope(kr, gkr, cos, sin)


def _mla_prep1_fwd(c, gq, gkv, gkr, cos, sin):
    t = c.shape[0]

    def body(c_ref, cos_ref, sin_ref, gq_ref, gkv_ref, gkr_ref, cq_ref, ckv_ref, kr_ref):
        a, b, r = _mla_latent(c_ref[:, :256], c_ref[:, 256:384], c_ref[:, 384:], gq_ref[...], gkv_ref[...], gkr_ref[...],
                              cos_ref[...], sin_ref[...])
        cq_ref[...] = a.astype(BF16)
        ckv_ref[...] = b.astype(BF16)
        kr_ref[...] = r.astype(BF16)

    return _rows("mla_prep1", body, [c, cos, sin], [_sds((t, 256), BF16), _sds((t, HEAD), BF16), _sds((t, HEAD), BF16)],
                 tt=512, consts=[gq, gkv, gkr])


def _mla_prep1_bwd(c, gq, gkv, gkr, cos, sin, dcq, dckv, dkr_heads):
    def body(c_ref, cos_ref, sin_ref, dcq_ref, dckv_ref, dkr_ref, gq_ref, gkv_ref, gkr_ref, dc_ref, dgq_ref, dgkv_ref, dgkr_ref):
        dkr = dkr_ref[0]
        for h in range(1, N_HEADS):
            dkr = dkr + dkr_ref[h]
        _, vjp = jax.vjp(_mla_latent, c_ref[:, :256], c_ref[:, 256:384], c_ref[:, 384:], gq_ref[...], gkv_ref[...], gkr_ref[...],
                         cos_ref[...], sin_ref[...])
        a, b, r, d1, d2, d3, _, _ = vjp((dcq_ref[...], dckv_ref[...], dkr))
        dc_ref[:, :256] = a.astype(BF16)
        dc_ref[:, 256:384] = b.astype(BF16)
        dc_ref[:, 384:] = r.astype(BF16)
        _acc(dgq_ref, d1)
        _acc(dgkv_ref, d2)
        _acc(dgkr_ref, d3)

    return _rows("mla_prep1_bwd", body, [c, cos, sin, dcq, dckv, dkr_heads], [_sds(c.shape, BF16)], tt=512,
                 consts=[gq, gkv, gkr], accs=[_sds(gq.shape), _sds(gkv.shape), _sds(gkr.shape)])


def _mla_heads(qn, qr, kn, gqn, gqr, gkn, cos, sin):
    return _rms(qn, gqn) * MLA_SCALE, _rms_rope(qr, gqr, cos, sin) * MLA_SCALE, _rms(kn, gkn)


def _mla_prep2_fwd(qa, kv, gqn, gqr, gkn, cos, sin):
    t = qa.shape[0]

    def body(qa_ref, kv_ref, cos_ref, sin_ref, gqn_ref, gqr_ref, gkn_ref, qn_ref, qr_ref, kn_ref, v_ref):
        for h in range(N_HEADS):
            a, b, c = _mla_heads(_heads_in(qa_ref, h), _heads_in(qa_ref, N_HEADS + h), _heads_in(kv_ref, h),
                                 gqn_ref[...], gqr_ref[...], gkn_ref[...], cos_ref[...], sin_ref[...])
            qn_ref[h] = a.astype(BF16)
            qr_ref[h] = b.astype(BF16)
            kn_ref[h] = c.astype(BF16)
            v_ref[h] = _heads_in(kv_ref, N_HEADS + h).astype(BF16)

    hm = _sds((N_HEADS, t, HEAD), BF16)
    return _rows("mla_prep2", body, [qa, kv, cos, sin], [hm, hm, hm, hm], tt=256, consts=[gqn, gqr, gkn])


def _mla_prep2_bwd(qa, kv, gqn, gqr, gkn, cos, sin, dqn, dqr, dkn, dv):
    def body(qa_ref, kv_ref, cos_ref, sin_ref, dqn_ref, dqr_ref, dkn_ref, dv_ref, gqn_ref, gqr_ref, gkn_ref,
             dqa_ref, dkv_ref, d1_ref, d2_ref, d3_ref):
        d1 = jnp.zeros(gqn_ref.shape, F32)
        d2 = jnp.zeros(gqr_ref.shape, F32)
        d3 = jnp.zeros(gkn_ref.shape, F32)
        for h in range(N_HEADS):
            _, vjp = jax.vjp(_mla_heads, _heads_in(qa_ref, h), _heads_in(qa_ref, N_HEADS + h), _heads_in(kv_ref, h),
                             gqn_ref[...], gqr_ref[...], gkn_ref[...], cos_ref[...], sin_ref[...])
            a, b, c, e1, e2, e3, _, _ = vjp((dqn_ref[h], dqr_ref[h], dkn_ref[h]))
            dqa_ref[:, h * HEAD:(h + 1) * HEAD] = a.astype(BF16)
            dqa_ref[:, (N_HEADS + h) * HEAD:(N_HEADS + h + 1) * HEAD] = b.astype(BF16)
            dkv_ref[:, h * HEAD:(h + 1) * HEAD] = c.astype(BF16)
            dkv_ref[:, (N_HEADS + h) * HEAD:(N_HEADS + h + 1) * HEAD] = dv_ref[h].astype(BF16)
            d1, d2, d3 = d1 + e1, d2 + e2, d3 + e3
        _acc(d1_ref, d1)
        _acc(d2_ref, d2)
        _acc(d3_ref, d3)

    return _rows("mla_prep2_bwd", body, [qa, kv, cos, sin, dqn, dqr, dkn, dv], [_sds(qa.shape, BF16), _sds(kv.shape, BF16)],
                 tt=256, consts=[gqn, gqr, gkn], accs=[_sds(gqn.shape), _sds(gqr.shape), _sds(gkn.shape)])


def _mla_attn_fwd(qn, qr, kn, kr, v):
    nh, t, _ = qn.shape
    bq = _q_block(t)
    per = bq // ATT_BLOCK

    def kern(qn_ref, qr_ref, kn_ref, kr_ref, v_ref, o_ref, lse_ref):
        qb = pl.program_id(1)
        qv = jnp.concatenate([qn_ref[0], qr_ref[0]], axis=1)
        order = _key_order(bq, qb)

        def body(kb, carry, diagonal):
            acc, m, l = carry
            off = pl.multiple_of(kb * ATT_BLOCK, ATT_BLOCK)
            kv = jnp.concatenate([kn_ref[0, pl.ds(off, ATT_BLOCK), :], kr_ref[pl.ds(off, ATT_BLOCK), :]], axis=1)
            s = _dot_nt(qv, kv)
            if diagonal:
                s = jnp.where(order >= kb * ATT_BLOCK, s, -jnp.inf)
            m_new = jnp.maximum(m, jnp.max(s, axis=1, keepdims=True))
            alpha = jnp.exp(m - m_new)
            p = jnp.exp(s - m_new)
            acc = acc * alpha + _dot(p.astype(BF16), v_ref[0, pl.ds(off, ATT_BLOCK), :])
            return acc, m_new, l * alpha + jnp.sum(p, axis=1, keepdims=True)

        init = (jnp.zeros((bq, HEAD), F32), jnp.full((bq, 1), -jnp.inf, F32), jnp.zeros((bq, 1), F32))
        carry = lax.fori_loop(0, qb * per, lambda i, c: body(i, c, False), init)
        acc, m, l = lax.fori_loop(qb * per, (qb + 1) * per, lambda i, c: body(i, c, True), carry)
        o_ref[...] = acc / l
        lse_ref[...] = jnp.broadcast_to(m + jnp.log(l), (bq, HEAD))

    blk = pl.BlockSpec((1, bq, HEAD), lambda h, i: (h, i, 0))
    full = pl.BlockSpec((1, t, HEAD), lambda h, i: (h, 0, 0))
    tok = pl.BlockSpec((bq, HEAD), lambda h, i: (i, h))
    return pl.pallas_call(
        kern, name="mla_attn_fwd", grid=(nh, t // bq),
        in_specs=[blk, blk, full, pl.BlockSpec((t, HEAD), lambda h, i: (0, 0)), full],
        out_specs=[tok, tok], out_shape=[_sds((t, nh * HEAD)), _sds((t, nh * HEAD))],
        compiler_params=_cparams(dimension_semantics=("parallel", "arbitrary")),
    )(qn, qr, kn, kr, v)


def _mla_attn_bwd(qn, qr, kn, kr, v, o, lse, do):
    nh, t, _ = qn.shape
    bq = _q_block(t)
    per = bq // ATT_BLOCK

    def kern(qn_ref, qr_ref, kn_ref, kr_ref, v_ref, o_ref, lse_ref, do_ref, dqn_ref, dqr_ref, dkn_ref, dkr_ref, dv_ref):
        qb = pl.program_id(1)

        @pl.when(qb == 0)
        def _():
            dkn_ref[...] = jnp.zeros(dkn_ref.shape, F32)
            dkr_ref[...] = jnp.zeros(dkr_ref.shape, F32)
            dv_ref[...] = jnp.zeros(dv_ref.shape, F32)

        qv = jnp.concatenate([qn_ref[0], qr_ref[0]], axis=1)
        dov = do_ref[...]
        dob = dov.astype(BF16)
        delta = jnp.sum(dov * o_ref[...], axis=1, keepdims=True)
        lse_col = lse_ref[:, :1]
        order = _key_order(bq, qb)

        def body(kb, dq, diagonal):
            off = pl.multiple_of(kb * ATT_BLOCK, ATT_BLOCK)
            kv = jnp.concatenate([kn_ref[0, pl.ds(off, ATT_BLOCK), :], kr_ref[pl.ds(off, ATT_BLOCK), :]], axis=1)
            vv = v_ref[0, pl.ds(off, ATT_BLOCK), :]
            p = jnp.exp(_dot_nt(qv, kv) - lse_col)
            if diagonal:
                p = jnp.where(order >= kb * ATT_BLOCK, p, 0.0)
            ds = (p * (_dot_nt(dob, vv) - delta)).astype(BF16)
            dk = _dot_tn(ds, qv)
            dkn_ref[0, pl.ds(off, ATT_BLOCK), :] += dk[:, :HEAD]
            dkr_ref[0, pl.ds(off, ATT_BLOCK), :] += dk[:, HEAD:]
            dv_ref[0, pl.ds(off, ATT_BLOCK), :] += _dot_tn(p.astype(BF16), dob)
            return dq + _dot(ds, kv)

        dq = lax.fori_loop(0, qb * per, lambda i, c: body(i, c, False), jnp.zeros((bq, 2 * HEAD), F32))
        dq = lax.fori_loop(qb * per, (qb + 1) * per, lambda i, c: body(i, c, True), dq)
        dqn_ref[0] = dq[:, :HEAD]
        dqr_ref[0] = dq[:, HEAD:]

    hm = _sds((nh, t, HEAD))
    blk = pl.BlockSpec((1, bq, HEAD), lambda h, i: (h, i, 0))
    full = pl.BlockSpec((1, t, HEAD), lambda h, i: (h, 0, 0))
    tok = pl.BlockSpec((bq, HEAD), lambda h, i: (i, h))
    return pl.pallas_call(
        kern, name="mla_attn_bwd", grid=(nh, t // bq),
        in_specs=[blk, blk, full, pl.BlockSpec((t, HEAD), lambda h, i: (0, 0)), full, tok, tok, tok],
        out_specs=[blk, blk, full, full, full], out_shape=[hm, hm, hm, hm, hm],
        compiler_params=_cparams(dimension_semantics=("parallel", "arbitrary")),
    )(qn, qr, kn, kr, v, o, lse, do)


def _rope_tables(t):
    inv_freq = ROPE_THETA ** (-jnp.arange(0, MLA_ROPE, 2, dtype=F32) / MLA_ROPE)
    ang = jnp.arange(t, dtype=F32)[:, None] * inv_freq[None, :]
    c, s = jnp.cos(ang), jnp.sin(ang)
    pad = ((0, 0), (0, LANES - MLA_ROPE))
    return jnp.pad(jnp.concatenate([c, c], axis=1), pad), jnp.pad(jnp.concatenate([-s, s], axis=1), pad)


def _pad_lanes(v, n=LANES):
    return jnp.pad(v, (0, n - v.shape[0])).reshape(1, n)


def _mla_layout(w_down, w_uq, w_ukv):
    w_down_p = jnp.pad(w_down, ((0, 0), (0, MLA_C - w_down.shape[1])))
    uq = w_uq.reshape(w_uq.shape[0], N_HEADS, MLA_QK)
    rope = jnp.pad(uq[:, :, HEAD:], ((0, 0), (0, 0), (0, LANES - MLA_ROPE)))
    w_uq_p = jnp.concatenate([uq[:, :, :HEAD].reshape(-1, N_HEADS * HEAD), rope.reshape(-1, N_HEADS * LANES)], axis=1)
    ukv = w_ukv.reshape(w_ukv.shape[0], N_HEADS, 2 * HEAD)
    w_ukv_p = jnp.concatenate([ukv[:, :, :HEAD].reshape(-1, N_HEADS * HEAD), ukv[:, :, HEAD:].reshape(-1, N_HEADS * HEAD)], axis=1)
    return w_down_p, w_uq_p, w_ukv_p


def _mla_unlayout(d_down_p, d_uq_p, d_ukv_p):
    d_down = d_down_p[:, :256 + HEAD + MLA_ROPE]
    nope = d_uq_p[:, :N_HEADS * HEAD].reshape(-1, N_HEADS, HEAD)
    rope = d_uq_p[:, N_HEADS * HEAD:].reshape(-1, N_HEADS, LANES)[:, :, :MLA_ROPE]
    d_uq = jnp.concatenate([nope, rope], axis=2).reshape(-1, N_HEADS * MLA_QK)
    kn = d_ukv_p[:, :N_HEADS * HEAD].reshape(-1, N_HEADS, HEAD)
    vv = d_ukv_p[:, N_HEADS * HEAD:].reshape(-1, N_HEADS, HEAD)
    d_ukv = jnp.concatenate([kn, vv], axis=2).reshape(-1, N_HEADS * 2 * HEAD)
    return d_down, d_uq, d_ukv


def _mla_weight_shapes():
    return (_sds((1024, MLA_C), BF16), _sds((1, 256)), _sds((1, HEAD)), _sds((256, 2048), BF16), _sds((HEAD, 2048), BF16),
            _sds((1, HEAD)), _sds((1, HEAD)), _sds((1, HEAD)), _sds((1, HEAD)), _sds((1024, 1024), BF16),
            _sds((4096, HEAD)), _sds((4096, HEAD)))


def _mla_fwd(x, h, w_down, gq, gkv, w_uq, w_ukv, gqn, gqr, gkn, gkr, w_out, cos, sin):
    c = _mm("mla_down", h, w_down)
    cq, ckv, kr = _mla_prep1_fwd(c, gq, gkv, gkr, cos, sin)
    qa = _mm("mla_uq", cq, w_uq)
    kv = _mm("mla_ukv", ckv, w_ukv)
    qn, qr, kn, v = _mla_prep2_fwd(qa, kv, gqn, gqr, gkn, cos, sin)
    o, lse = _mla_attn_fwd(qn, qr, kn, kr, v)
    y = _mm("mla_out", o, w_out, add=x)
    return y, (h, c, cq, ckv, kr, qa, kv, qn, qr, kn, v, o, lse)


def _mla_bwd(saved, dy, w_down, gq, gkv, w_uq, w_ukv, gqn, gqr, gkn, gkr, w_out, cos, sin):
    h, c, cq, ckv, kr, qa, kv, qn, qr, kn, v, o, lse = saved
    d_wout = _mm("mla_out_wgrad", o, dy, ta=True, out_dtype=GRAD_DTYPE)
    do = _mm("mla_out_dgrad", dy, w_out, tb=True)
    dqn, dqr, dkn, dkr, dv = _mla_attn_bwd(qn, qr, kn, kr, v, o, lse, do)
    dqa, dkv, dgqn, dgqr, dgkn = _mla_prep2_bwd(qa, kv, gqn, gqr, gkn, cos, sin, dqn, dqr, dkn, dv)
    d_wuq = _mm("mla_uq_wgrad", cq, dqa, ta=True, out_dtype=GRAD_DTYPE)
    d_wukv = _mm("mla_ukv_wgrad", ckv, dkv, ta=True, out_dtype=GRAD_DTYPE)
    dcq = _mm("mla_uq_dgrad", dqa, w_uq, tb=True)
    dckv = _mm("mla_ukv_dgrad", dkv, w_ukv, tb=True)
    dc, dgq, dgkv, dgkr = _mla_prep1_bwd(c, gq, gkv, gkr, cos, sin, dcq, dckv, dkr)
    d_wdown = _mm("mla_down_wgrad", h, dc, ta=True, out_dtype=GRAD_DTYPE)
    dh = _mm("mla_down_dgrad", dc, w_down, tb=True)
    return dh, d_wdown, dgq, dgkv, d_wuq, d_wukv, dgqn, dgqr, dgkn, dgkr, d_wout


def _loss_head(y, target):
    d = y.shape[1]

    def body(y_ref, t_ref, dy_ref, l_ref):
        err = y_ref[...] - t_ref[...]
        dy_ref[...] = err * (1.0 / d)
        part = 0.5 * jnp.sum(jnp.sum(err * err, axis=1, keepdims=True) * (1.0 / d), axis=0, keepdims=True)
        _acc(l_ref, jnp.broadcast_to(part, (1, LANES)))

    return _rows("loss_head", body, [y, target], [_sds(y.shape)], tt=512, accs=[_sds((1, LANES))])


MESH_ID = pl.DeviceIdType.MESH
HBM_SPEC = pl.BlockSpec(memory_space=pltpu.HBM)


def _all_gather(name, x):
    m_per, n = x.shape

    def body(x_ref, out_ref, send_sems, recv_sems, local_sem):
        x, y, c = lax.axis_index("x"), lax.axis_index("y"), lax.axis_index("c")
        me, sibling = (x, y, c), (x, y, 1 - c)
        chips = [(1 - x, y), (x, 1 - y), (1 - x, 1 - y)]

        def rows(px, py, pc):
            return out_ref.at[pl.ds((4 * px + 2 * py + pc) * m_per, m_per), :]

        def copy(k, block, to, src=None):
            return pltpu.make_async_remote_copy(
                src_ref=rows(*block) if src is None else src, dst_ref=rows(*block),
                send_sem=send_sems.at[k], recv_sem=recv_sems.at[k], device_id=to, device_id_type=MESH_ID)

        mine = pltpu.make_async_copy(x_ref, rows(*me), local_sem)
        mine.start()
        first = [copy(0, me, sibling, src=x_ref)]
        first += [copy(1 + j, me, (*chip, c), src=x_ref) for j, chip in enumerate(chips)]
        for cp in first:
            cp.start()
        passed = [copy(4 + j, (*chip, c), sibling) for j, chip in enumerate(chips)]
        for j, chip in enumerate(chips):
            copy(1 + j, (*chip, c), me).wait_recv()
            passed[j].start()
        copy(0, sibling, me).wait_recv()
        for j, chip in enumerate(chips):
            copy(4 + j, (*chip, 1 - c), me).wait_recv()
        for cp in first + passed:
            cp.wait_send()
        mine.wait()

    return pl.pallas_call(
        body, name=name,
        out_shape=jax.ShapeDtypeStruct((N_DEV * m_per, n), x.dtype),
        in_specs=[HBM_SPEC], out_specs=HBM_SPEC,
        scratch_shapes=[pltpu.SemaphoreType.DMA((7,)), pltpu.SemaphoreType.DMA((7,)), pltpu.SemaphoreType.DMA],
    )(x)


def _all_gather_groups(name, xs):
    ng = len(xs)

    def body(*refs):
        x_refs, out_refs, token = refs[:ng], refs[ng:2 * ng], refs[2 * ng]
        send_sems, recv_sems, local_sems = refs[2 * ng + 1:]
        token[...] = jnp.zeros(token.shape, F32)
        x, y, c = lax.axis_index("x"), lax.axis_index("y"), lax.axis_index("c")
        me, sibling = (x, y, c), (x, y, 1 - c)
        chips = [(1 - x, y), (x, 1 - y), (1 - x, 1 - y)]

        def copy(g, k, block, to, src=None):
            px, py, pc = block
            dst = out_refs[g].at[4 * px + 2 * py + pc]
            return pltpu.make_async_remote_copy(
                src_ref=dst if src is None else src, dst_ref=dst,
                send_sem=send_sems.at[g, k], recv_sem=recv_sems.at[g, k], device_id=to, device_id_type=MESH_ID)

        mine = [pltpu.make_async_copy(x_refs[g], out_refs[g].at[4 * x + 2 * y + c], local_sems.at[g]) for g in range(ng)]
        for cp in mine:
            cp.start()
        first = []
        for g in range(ng):
            first.append(copy(g, 0, me, sibling, src=x_refs[g]))
            first += [copy(g, 1 + j, me, (*chip, c), src=x_refs[g]) for j, chip in enumerate(chips)]
        for cp in first:
            cp.start()
        passed = []
        for j, chip in enumerate(chips):
            for g in range(ng):
                copy(g, 1 + j, (*chip, c), me).wait_recv()
                passed.append(copy(g, 4 + j, (*chip, c), sibling))
                passed[-1].start()
        for g in range(ng):
            copy(g, 0, sibling, me).wait_recv()
            for j, chip in enumerate(chips):
                copy(g, 4 + j, (*chip, 1 - c), me).wait_recv()
        for cp in first + passed:
            cp.wait_send()
        for cp in mine:
            cp.wait()

    return pl.pallas_call(
        body, name=name,
        out_shape=[jax.ShapeDtypeStruct((N_DEV,) + x.shape, x.dtype) for x in xs] + [_sds((8, LANES))],
        in_specs=[HBM_SPEC] * ng, out_specs=[HBM_SPEC] * ng + [pl.BlockSpec(memory_space=pltpu.VMEM)],
        scratch_shapes=[pltpu.SemaphoreType.DMA((ng, 7)), pltpu.SemaphoreType.DMA((ng, 7)), pltpu.SemaphoreType.DMA((ng,))],
    )(*xs)


EFFECT = pltpu.SideEffectType.DATAFLOW_SIDE_EFFECTING
SEM_SPEC = pl.BlockSpec(memory_space=pltpu.SEMAPHORE)


def _push_copies(src_refs, land_refs, send_sems, recv_sems, chunked):
    x, y, c = lax.axis_index("x"), lax.axis_index("y"), lax.axis_index("c")
    me = 4 * x + 2 * y + c
    copies = []
    for g, (src, land) in enumerate(zip(src_refs, land_refs)):
        for k in range(1, N_DEV):
            px = 1 - x if k & 4 else x
            py = 1 - y if k & 2 else y
            pc = 1 - c if k & 1 else c
            copies.append(pltpu.make_async_remote_copy(
                src_ref=src.at[4 * px + 2 * py + pc] if chunked else src, dst_ref=land.at[me],
                send_sem=send_sems.at[g * (N_DEV - 1) + k - 1], recv_sem=recv_sems.at[g * (N_DEV - 1) + k - 1],
                device_id=(px, py, pc), device_id_type=MESH_ID))
    return copies


def _hbm(x):
    return pltpu.with_memory_space_constraint(x, pltpu.HBM)


def _push_start(name, srcs, lands, chunked):
    ng = len(srcs)

    def body(*refs):
        for cp in _push_copies(refs[:ng], refs[ng:2 * ng], refs[2 * ng], refs[2 * ng + 1], chunked):
            cp.start()
        refs[-1][...] = jnp.zeros(refs[-1].shape, F32)

    bufs = list(srcs) + list(lands)
    outs = pl.pallas_call(
        body, name=name,
        out_shape=(pltpu.SemaphoreType.DMA((ng * (N_DEV - 1),)), pltpu.SemaphoreType.DMA((ng * (N_DEV - 1),)),
                   *[pltpu.HBM(b.shape, b.dtype) for b in bufs], jax.ShapeDtypeStruct((8, LANES), F32)),
        in_specs=[HBM_SPEC] * (2 * ng),
        out_specs=(SEM_SPEC, SEM_SPEC, *[HBM_SPEC] * (2 * ng), pl.BlockSpec(memory_space=pltpu.VMEM)),
        input_output_aliases={i: 2 + i for i in range(2 * ng)},
        compiler_params=pltpu.CompilerParams(has_side_effects=EFFECT),
    )(*[_hbm(b) for b in bufs])
    return outs[0], outs[1], list(outs[2:2 + ng]), list(outs[2 + ng:2 + 2 * ng]), outs[-1]


def _push_wait(name, started, after, chunked):
    send_sems, recv_sems, srcs, lands, _ = started
    ng = len(srcs)

    def body(*refs):
        copies = _push_copies(refs[:ng], refs[ng:2 * ng], refs[2 * ng], refs[2 * ng + 1], chunked)
        for cp in copies:
            cp.wait_send()
        for cp in copies:
            cp.wait_recv()

    bufs = srcs + lands
    outs = pl.pallas_call(
        body, name=name,
        out_shape=tuple(pltpu.HBM(b.shape, b.dtype) for b in bufs),
        in_specs=[HBM_SPEC] * (2 * ng) + [SEM_SPEC, SEM_SPEC, pl.BlockSpec(memory_space=pl.ANY)],
        out_specs=tuple([HBM_SPEC] * (2 * ng)),
        input_output_aliases={i: i for i in range(2 * ng)},
        compiler_params=pltpu.CompilerParams(has_side_effects=EFFECT),
    )(*bufs, send_sems, recv_sems, after)
    return list(outs[:ng]), list(outs[ng:])


def _sum_adam_devices(sent, recv, dev, w, m, v):
    ndev, r, c_ = recv.shape
    tr = _pick(r, (256, 128, 96, 32))

    def body(dev_ref, own_ref, r_ref, w_ref, m_ref, v_ref, g_ref, d_ref, mo_ref, vo_ref):
        me = dev_ref[0]
        g = jnp.where(me == 0, own_ref[0], r_ref[0]).astype(F32)
        for j in range(1, ndev):
            g = g + jnp.where(me == j, own_ref[0], r_ref[j]).astype(F32)
        g_ref[...] = g
        d_ref[...], mo_ref[...], vo_ref[...] = _adam(w_ref[...], g, m_ref[...], v_ref[...])

    row = pl.BlockSpec((tr, c_), lambda i, dev_ref: (i, 0))
    return pl.pallas_call(
        body, name="sum_adam",
        grid_spec=pltpu.PrefetchScalarGridSpec(
            num_scalar_prefetch=1, grid=(r // tr,),
            in_specs=[pl.BlockSpec((1, tr, c_), lambda i, dev_ref: (dev_ref[0], i, 0)),
                      pl.BlockSpec((ndev, tr, c_), lambda i, dev_ref: (0, i, 0)), row, row, row],
            out_specs=[row, row, row, row]),
        out_shape=[_sds((r, c_))] * 4,
        compiler_params=_cparams(dimension_semantics=("arbitrary",)),
    )(dev, sent, recv, w, m, v)


def _cols_from_shards(w, width):
    ns, r, cs = w.shape
    tr = _pick(r, (256, 128))

    def body(w_ref, o_ref):
        parts = [w_ref[j] for j in range(ns)]
        if width > ns * cs:
            parts.append(jnp.zeros((tr, width - ns * cs), w.dtype))
        o_ref[...] = jnp.concatenate(parts, axis=1)

    return pl.pallas_call(
        body, name="cols_from_shards", grid=(r // tr,),
        in_specs=[pl.BlockSpec((ns, tr, cs), lambda i: (0, i, 0))], out_specs=pl.BlockSpec((tr, width), lambda i: (i, 0)),
        out_shape=jax.ShapeDtypeStruct((r, width), w.dtype), compiler_params=_cparams(dimension_semantics=("arbitrary",)),
    )(w)


def _shards_from_cols(g, cs):
    r, width = g.shape
    tr = _pick(r, (256, 128))

    def body(g_ref, o_ref):
        for j in range(N_DEV):
            o_ref[j] = g_ref[:, j * cs:(j + 1) * cs]

    return pl.pallas_call(
        body, name="shards_from_cols", grid=(r // tr,),
        in_specs=[pl.BlockSpec((tr, width), lambda i: (i, 0))], out_specs=pl.BlockSpec((N_DEV, tr, cs), lambda i: (0, i, 0)),
        out_shape=jax.ShapeDtypeStruct((N_DEV, r, cs), g.dtype), compiler_params=_cparams(dimension_semantics=("arbitrary",)),
    )(g)


def _adam(w, g, m, v):
    m = ADAM_B1 * m + (1.0 - ADAM_B1) * g
    v = ADAM_B2 * v + (1.0 - ADAM_B2) * (g * g)
    m_hat = m / (1.0 - ADAM_B1 ** ADAM_STEP)
    v_hat = v / (1.0 - ADAM_B2 ** ADAM_STEP)
    return -ADAM_LR * (m_hat / (jnp.sqrt(v_hat) + ADAM_EPS) + ADAM_WD * w), m, v


def _sum_devices(gathered):
    m_all, n = gathered.shape
    m_per = m_all // N_DEV

    def body(x_ref, o_ref):
        s = x_ref[0:m_per, :]
        for j in range(1, N_DEV):
            s = s + x_ref[j * m_per:(j + 1) * m_per, :]
        o_ref[...] = s

    return pl.pallas_call(body, name="sum_devices", out_shape=_sds((m_per, n)), compiler_params=_cparams())(gathered)


def _adam_small(w, g, m, v):
    def body(w_ref, g_ref, m_ref, v_ref, d_ref, mo_ref, vo_ref):
        d_ref[...], mo_ref[...], vo_ref[...] = _adam(w_ref[...], g_ref[...], m_ref[...], v_ref[...])

    return pl.pallas_call(body, name="adam_small", out_shape=[_sds(w.shape)] * 3, compiler_params=_cparams())(w, g, m, v)


N_LAYERS = 4
_MIXER = ("dn", "sb", "mla")
_MIXER_PARAMS = {
    "dn": ("dn_w_in", "dn_conv_w", "dn_a_log", "dn_dt_bias", "dn_out_norm", "dn_w_out"),
    "sb": ("sb_w_qkv", "sb_q_norm", "sb_k_norm", "sb_w_out"),
    "mla": ("mla_w_down", "mla_q_a_norm", "mla_kv_a_norm", "mla_w_uq", "mla_w_ukv", "mla_q_nope_norm", "mla_q_rope_norm",
            "mla_k_nope_norm", "mla_k_rope_norm", "mla_w_out"),
}
_BIG_AXIS = {"dn_w_in": 1, "dn_w_out": 0, "sb_w_qkv": 1, "sb_w_out": 0, "mla_w_down": 0, "mla_w_uq": 1, "mla_w_ukv": 1,
             "mla_w_out": 0, "ffn_w_gate_up": 1, "ffn_w_down": 0}


def _weight_names():
    names = []
    for i in range(N_LAYERS):
        p = "l%d_" % i
        names += [p + "mix_norm"] + [p + n for n in _MIXER_PARAMS[_MIXER[i % 3]]] + [p + "ffn_norm", p + "ffn_w_gate_up", p + "ffn_w_down"]
    return names


WEIGHTS = _weight_names()
BIG = [n for n in WEIGHTS if n[3:] in _BIG_AXIS]
SMALL = [n for n in WEIGHTS if n[3:] not in _BIG_AXIS]
CONV = [n for n in SMALL if n.endswith("conv_w")]


def _ceil_to(n, k):
    return -(-n // k) * k


def _pack(arrs, cols, row_mult):
    parts = []
    for a in arrs:
        f = a.reshape(-1)
        parts.append(jnp.pad(f, (0, _ceil_to(f.shape[0], cols) - f.shape[0])))
    flat = jnp.concatenate(parts)
    rows = _ceil_to(flat.shape[0] // cols, row_mult)
    return jnp.pad(flat, (0, rows * cols - flat.shape[0])).reshape(rows, cols)


def _unpack(buf, shapes):
    cols = buf.shape[-1]
    out, r0 = [], 0
    for s in shapes:
        n = math.prod(s)
        nr = _ceil_to(n, cols) // cols
        out.append(buf[r0:r0 + nr].reshape(-1)[:n].reshape(s))
        r0 += nr
    return out


def _layer_groups(i):
    by = {"gu": (704, []), "down": (1024, []), "out": (1024, []), "dn_in": (514, []), "sb_qkv": (384, []), "mla": (512, [])}
    key = {"ffn_w_gate_up": "gu", "ffn_w_down": "down", "dn_w_in": "dn_in", "sb_w_qkv": "sb_qkv", "mla_w_down": "mla",
           "mla_w_uq": "mla", "mla_w_ukv": "mla"}
    for n in BIG:
        if n.startswith("l%d_" % i):
            by[key.get(n[3:], "out")][1].append(n)
    return [g for g in by.values() if g[1]]


LAYER_GROUPS = [_layer_groups(i) for i in range(N_LAYERS)]
N_FFN_GROUPS = 2


def _stack_group(grp, get):
    width, names = grp
    return jnp.concatenate([jnp.pad(get(n), ((0, 0), (0, width - get(n).shape[1]))) for n in names], axis=0)


def _unstack_group(grp, buf, shape_of):
    out, r0 = [], 0
    for n in grp[1]:
        rs, cs = shape_of(n)
        out.append(buf[..., r0:r0 + rs, :cs])
        r0 += rs
    return out


def kernel(x, l0_mix_norm, l0_dn_w_in, l0_dn_conv_w, l0_dn_a_log, l0_dn_dt_bias, l0_dn_out_norm, l0_dn_w_out, l0_ffn_norm, l0_ffn_w_gate_up, l0_ffn_w_down, l1_mix_norm, l1_sb_w_qkv, l1_sb_q_norm, l1_sb_k_norm, l1_sb_w_out, l1_ffn_norm, l1_ffn_w_gate_up, l1_ffn_w_down, l2_mix_norm, l2_mla_w_down, l2_mla_q_a_norm, l2_mla_kv_a_norm, l2_mla_w_uq, l2_mla_w_ukv, l2_mla_q_nope_norm, l2_mla_q_rope_norm, l2_mla_k_nope_norm, l2_mla_k_rope_norm, l2_mla_w_out, l2_ffn_norm, l2_ffn_w_gate_up, l2_ffn_w_down, l3_mix_norm, l3_dn_w_in, l3_dn_conv_w, l3_dn_a_log, l3_dn_dt_bias, l3_dn_out_norm, l3_dn_w_out, l3_ffn_norm, l3_ffn_w_gate_up, l3_ffn_w_down, loss_target, m_l0_mix_norm, m_l0_dn_w_in, m_l0_dn_conv_w, m_l0_dn_a_log, m_l0_dn_dt_bias, m_l0_dn_out_norm, m_l0_dn_w_out, m_l0_ffn_norm, m_l0_ffn_w_gate_up, m_l0_ffn_w_down, m_l1_mix_norm, m_l1_sb_w_qkv, m_l1_sb_q_norm, m_l1_sb_k_norm, m_l1_sb_w_out, m_l1_ffn_norm, m_l1_ffn_w_gate_up, m_l1_ffn_w_down, m_l2_mix_norm, m_l2_mla_w_down, m_l2_mla_q_a_norm, m_l2_mla_kv_a_norm, m_l2_mla_w_uq, m_l2_mla_w_ukv, m_l2_mla_q_nope_norm, m_l2_mla_q_rope_norm, m_l2_mla_k_nope_norm, m_l2_mla_k_rope_norm, m_l2_mla_w_out, m_l2_ffn_norm, m_l2_ffn_w_gate_up, m_l2_ffn_w_down, m_l3_mix_norm, m_l3_dn_w_in, m_l3_dn_conv_w, m_l3_dn_a_log, m_l3_dn_dt_bias, m_l3_dn_out_norm, m_l3_dn_w_out, m_l3_ffn_norm, m_l3_ffn_w_gate_up, m_l3_ffn_w_down, v_l0_mix_norm, v_l0_dn_w_in, v_l0_dn_conv_w, v_l0_dn_a_log, v_l0_dn_dt_bias, v_l0_dn_out_norm, v_l0_dn_w_out, v_l0_ffn_norm, v_l0_ffn_w_gate_up, v_l0_ffn_w_down, v_l1_mix_norm, v_l1_sb_w_qkv, v_l1_sb_q_norm, v_l1_sb_k_norm, v_l1_sb_w_out, v_l1_ffn_norm, v_l1_ffn_w_gate_up, v_l1_ffn_w_down, v_l2_mix_norm, v_l2_mla_w_down, v_l2_mla_q_a_norm, v_l2_mla_kv_a_norm, v_l2_mla_w_uq, v_l2_mla_w_ukv, v_l2_mla_q_nope_norm, v_l2_mla_q_rope_norm, v_l2_mla_k_nope_norm, v_l2_mla_k_rope_norm, v_l2_mla_w_out, v_l2_ffn_norm, v_l2_ffn_w_gate_up, v_l2_ffn_w_down, v_l3_mix_norm, v_l3_dn_w_in, v_l3_dn_conv_w, v_l3_dn_a_log, v_l3_dn_dt_bias, v_l3_dn_out_norm, v_l3_dn_w_out, v_l3_ffn_norm, v_l3_ffn_w_gate_up, v_l3_ffn_w_down):
    a = dict(locals())
    return _train_step(a)


def _train_step(a):
    mx, my, mc = lax.axis_index("x"), lax.axis_index("y"), lax.axis_index("c")
    dev = 4 * mx + 2 * my + mc
    dev_arr = jnp.reshape(dev, (1,)).astype(jnp.int32)
    t, d = a["x"].shape[1], a["x"].shape[2]
    xs = a["x"].reshape(t, d)
    target = a["loss_target"].reshape(t, d)

    full = {}

    def unpack(groups, bufs):
        for grp, buf in zip(groups, bufs):
            for n, shards in zip(grp[1], _unstack_group(grp, buf, lambda n: a[n].shape)):
                kind = n[3:]
                if kind == "ffn_w_gate_up":
                    full[n] = shards
                elif _BIG_AXIS[kind] == 0:
                    full[n] = shards.reshape(N_DEV * shards.shape[1], shards.shape[2])
                else:
                    width = DN_PROJ if kind == "dn_w_in" else N_DEV * shards.shape[2]
                    full[n] = _cols_from_shards(shards, width)

    def local_shards(groups):
        return [_stack_group(grp, lambda n: a[n].astype(BF16)) for grp in groups]

    pushed_groups = [LAYER_GROUPS[0][:N_FFN_GROUPS]] + LAYER_GROUPS[1:]
    conv_pack = _pack([a[n] for n in CONV], LANES, 8)
    first_groups = LAYER_GROUPS[0][N_FFN_GROUPS:]
    *first_bufs, conv_all, gathered = _all_gather_groups("gather_weights", local_shards(first_groups) + [conv_pack])
    unpack(first_groups, first_bufs)
    for n, parts in zip(CONV, zip(*[_unpack(conv_all[j], [a[n].shape for n in CONV]) for j in range(N_DEV)])):
        full[n] = jnp.concatenate(parts, axis=1)
    after_first = gathered[0, 0].astype(BF16)
    gathers, started = {}, jnp.zeros((), F32)
    for i in range(N_LAYERS):
        srcs = [s + after_first for s in local_shards(pushed_groups[i])]
        lands = [lax.dynamic_update_index_in_dim(lax.empty((N_DEV,) + s.shape, s.dtype), s, dev, 0) for s in srcs]
        gathers[i] = _push_start("gather_start_l%d" % i, srcs, lands, False)
        started = started + gathers[i][-1][0, 0]

    def vec(n):
        return a[n].reshape(1, -1)

    cos, sin = _rope_tables(t)

    def mixer_args(i):
        p = "l%d_" % i
        kind = _MIXER[i % 3]
        if kind == "dn":
            args = (full[p + "dn_w_in"], full[p + "dn_conv_w"], _pad_lanes(a[p + "dn_a_log"]), _pad_lanes(a[p + "dn_dt_bias"]),
                    vec(p + "dn_out_norm"), full[p + "dn_w_out"])
        elif kind == "sb":
            args = (full[p + "sb_w_qkv"], vec(p + "sb_q_norm"), vec(p + "sb_k_norm"), full[p + "sb_w_out"])
        else:
            w_down, w_uq, w_ukv = _mla_layout(full[p + "mla_w_down"], full[p + "mla_w_uq"], full[p + "mla_w_ukv"])
            args = (w_down, vec(p + "mla_q_a_norm"), vec(p + "mla_kv_a_norm"), w_uq, w_ukv, vec(p + "mla_q_nope_norm"),
                    _pad_lanes(a[p + "mla_q_rope_norm"]), vec(p + "mla_k_nope_norm"), _pad_lanes(a[p + "mla_k_rope_norm"]),
                    full[p + "mla_w_out"], cos, sin)
        return kind, args

    fwd = {"dn": _dn_fwd, "sb": _sb_fwd, "mla": _mla_fwd}
    bwd = {"dn": _dn_bwd, "sb": _sb_bwd, "mla": _mla_bwd}
    saved, layer_args = [], []
    for i in range(N_LAYERS):
        p = "l%d_" % i
        if i > 0:
            unpack(pushed_groups[i], _push_wait("gather_wait_l%d" % i, gathers[i], xs, False)[1])
        kind, args = mixer_args(i)
        layer_args.append((kind, args))
        gain = vec(p + "mix_norm") + started if i == 0 else vec(p + "mix_norm")
        h = _rmsnorm_fwd("mix_norm", xs, gain)
        x_mid, sv_mix = fwd[kind](xs, h, *args)
        if i == 0:
            unpack(pushed_groups[0], _push_wait("gather_wait_l0", gathers[0], x_mid, False)[1])
        x_out, sv_ffn = _ffn_fwd(x_mid, vec(p + "ffn_norm"), full[p + "ffn_w_gate_up"], full[p + "ffn_w_down"])
        saved.append((xs, sv_mix, sv_ffn))
        xs = x_out
    dy, loss_part = _loss_head(xs, target)

    grads, big_out = {}, {}

    def grad_shards(n):
        g, (rs, cs) = grads[n], a[n].shape
        if g.ndim == 3:
            return g
        if _BIG_AXIS[n[3:]] == 0:
            return g.reshape(N_DEV, rs, cs)
        return _shards_from_cols(g, cs)

    def push_grads(tag, groups):
        sends = []
        for grp in groups:
            parts = [jnp.pad(grad_shards(n), ((0, 0), (0, 0), (0, grp[0] - a[n].shape[1]))) for n in grp[1]]
            sends.append(jnp.concatenate(parts, axis=1))
        lands = [lax.empty(s.shape, s.dtype) for s in sends]
        return tag, groups, _push_start("grads_start_" + tag, sends, lands, True)

    def finish_grads(push, after):
        tag, groups, pushed = push
        sents, recvs = _push_wait("grads_wait_" + tag, pushed, after, True)
        for grp, sent, recv in zip(groups, sents, recvs):
            packs = [_stack_group(grp, lambda n, pre=pre: a[pre + n]) for pre in ("", "m_", "v_")]
            outs = [_unstack_group(grp, o, lambda n: a[n].shape) for o in _sum_adam_devices(sent, recv, dev_arr, *packs)]
            for j, n in enumerate(grp[1]):
                big_out[n] = [o[j] for o in outs]

    mixer_push = None
    for i in reversed(range(N_LAYERS)):
        p = "l%d_" % i
        kind, args = layer_args[i]
        x_in, sv_mix, sv_ffn = saved[i]
        gain = vec(p + "ffn_norm") if mixer_push is None else vec(p + "ffn_norm") + mixer_push[2][-1][0, 0]
        dx_mid, grads[p + "ffn_norm"], grads[p + "ffn_w_gate_up"], grads[p + "ffn_w_down"] = _ffn_bwd(
            sv_ffn, dy, gain, full[p + "ffn_w_gate_up"], full[p + "ffn_w_down"])
        ffn_push = push_grads("l%d_ffn" % i, LAYER_GROUPS[i][:N_FFN_GROUPS])
        if mixer_push is not None:
            finish_grads(mixer_push, dx_mid)
        res = bwd[kind](sv_mix, dx_mid, *args)
        dh = res[0]
        if kind == "mla":
            res = list(res)
            res[1], res[4], res[5] = _mla_unlayout(res[1], res[4], res[5])
        for n, g in zip(_MIXER_PARAMS[kind], res[1:]):
            grads[p + n] = g
        dy, grads[p + "mix_norm"] = _rmsnorm_bwd("mix_norm_bwd", x_in, vec(p + "mix_norm") + ffn_push[2][-1][0, 0], dh, dx_mid)
        mixer_push = push_grads("l%d_mix" % i, LAYER_GROUPS[i][N_FFN_GROUPS:])
        finish_grads(ffn_push, dy)
    grad_x = dy.reshape(a["x"].shape)

    small_full_shapes = [full[n].shape if n in CONV else a[n].shape for n in SMALL]
    small_grads = []
    for n, s in zip(SMALL, small_full_shapes):
        g = grads[n].reshape(-1)
        small_grads.append(g[:math.prod(s)])
    small_pack = _pack(small_grads + [loss_part.reshape(-1)], LANES, 8)
    small_sum = _sum_devices(_all_gather("gather_small_grads", small_pack))
    small_red = _unpack(small_sum, small_full_shapes + [(LANES,)])
    loss = small_red[-1][0]
    g_small = {}
    for n, g in zip(SMALL, small_red[:-1]):
        if n in CONV:
            cs = a[n].shape[1]
            g = lax.dynamic_slice_in_dim(g, dev * cs, cs, axis=1)
        g_small[n] = g
    small_shapes = [a[n].shape for n in SMALL]
    packs = [_pack([src[n] for n in SMALL], LANES, 8) for src in
             ({n: a[n] for n in SMALL}, g_small, {n: a["m_" + n] for n in SMALL}, {n: a["v_" + n] for n in SMALL})]
    d_small, m_small, v_small = (_unpack(o, small_shapes) for o in _adam_small(*packs))
    finish_grads(mixer_push, small_sum)

    small_out = dict(zip(SMALL, zip([g_small[n] for n in SMALL], d_small, m_small, v_small)))

    def out(k):
        return [small_out[n][k] if n in small_out else big_out[n][k] for n in WEIGHTS]

    return (loss, grad_x, *out(0), *out(1), *out(2), *out(3))
```

```python
import math

import jax
import jax.numpy as jnp
from jax import lax
from jax.experimental import pallas as pl
from jax.experimental.pallas import tpu as pltpu

F32 = jnp.float32
BF16 = jnp.bfloat16
GRAD_DTYPE = BF16
F32X3 = lax.Precision.HIGH

LANES = 128
N_DEV = 8
N_HEADS = 8
HEAD = 128
NORM_EPS = 1e-6
DN_CHUNK = 64
ATT_BLOCK = 512
ATT_Q = 512
MLA_ROPE = 64
MLA_QK = 192
ROPE_THETA = 10000.0
VMEM_LIMIT = 56 * 1024 * 1024
MM_TILE_BYTES = 32 * 1024 * 1024

ADAM_LR = 0.001
ADAM_B1 = 0.9
ADAM_B2 = 0.999
ADAM_EPS = 1e-08
ADAM_WD = 0.01
ADAM_STEP = 10


def _cparams(**kw):
    return pltpu.CompilerParams(vmem_limit_bytes=VMEM_LIMIT, **kw)


def _pick(n, cands):
    for c in cands:
        if c <= n and n % c == 0:
            return c
    return n


def _mm(name, a, b, *, ta=False, tb=False, out_dtype=F32, add=None, tm=None, tn=None, tk=None):
    if ta:
        K, M = a.shape
    else:
        M, K = a.shape
    N = b.shape[0] if tb else b.shape[1]
    tm = tm or _pick(M, (1024, 512, 256, 128))
    tn = tn or _pick(N, (1024, 512, 384, 256, 128))
    if tk is None:
        fits = [c for c in (4096, 2048, 1408, 1024, 512, 384, 256, 128)
                if c <= K and K % c == 0 and 2 * c * (tm * a.dtype.itemsize + tn * b.dtype.itemsize) <= MM_TILE_BYTES]
        tk = fits[0] if fits else K
    return _mm_raw(
        name, a, b, ta=ta, tb=tb, out_dtype=out_dtype, add=add, grid=(M // tm, N // tn, K // tk), out_shape=(M, N),
        a_block=(tk, tm) if ta else (tm, tk), a_map=(lambda i, j, k: (k, i)) if ta else (lambda i, j, k: (i, k)),
        b_block=(tn, tk) if tb else (tk, tn), b_map=(lambda i, j, k: (j, k)) if tb else (lambda i, j, k: (k, j)),
        o_block=(tm, tn), o_map=lambda i, j, k: (i, j))


def _mm_raw(name, a, b, *, ta, tb, out_dtype, add, grid, out_shape, a_block, a_map, b_block, b_map, o_block, o_map):
    nk = grid[2]
    tm, tn = o_block
    dn = (((0 if ta else 1,), (1 if tb else 0,)), ((), ()))
    has_add = add is not None

    def kern(*refs):
        if has_add:
            a_ref, b_ref, add_ref, o_ref, acc_ref = refs
        else:
            a_ref, b_ref, o_ref, acc_ref = refs
        k = pl.program_id(2)
        part = lax.dot_general(a_ref[...].astype(BF16), b_ref[...].astype(BF16), dn, preferred_element_type=F32)

        @pl.when(k == 0)
        def _():
            acc_ref[...] = part

        @pl.when(k > 0)
        def _():
            acc_ref[...] += part

        @pl.when(k == nk - 1)
        def _():
            r = acc_ref[...]
            if has_add:
                r = r + add_ref[...]
            o_ref[...] = r.astype(out_dtype)

    in_specs = [pl.BlockSpec(a_block, a_map), pl.BlockSpec(b_block, b_map)]
    args = [a, b]
    if has_add:
        in_specs.append(pl.BlockSpec(o_block, o_map))
        args.append(add)
    return pl.pallas_call(
        kern, name=name,
        grid=grid,
        in_specs=in_specs,
        out_specs=pl.BlockSpec(o_block, o_map),
        out_shape=jax.ShapeDtypeStruct(out_shape, out_dtype),
        scratch_shapes=[pltpu.VMEM((tm, tn), F32)],
        compiler_params=_cparams(dimension_semantics=("parallel", "parallel", "arbitrary")),
    )(*args)


def _rows(name, body, ins, outs, *, tt, consts=(), accs=()):
    in_specs, args = [], []
    first = ins[0][0] if isinstance(ins[0], tuple) else ins[0]
    t = first.shape[-2]
    tt = min(tt, t)
    for x in ins:
        if isinstance(x, tuple):
            arr, bs, im = x
            in_specs.append(pl.BlockSpec(bs, im))
            args.append(arr)
        else:
            in_specs.append(_row_spec(x.shape, tt))
            args.append(x)
    for c in consts:
        in_specs.append(pl.BlockSpec(c.shape, lambda i, _n=c.ndim: (0,) * _n))
        args.append(c)
    out_specs = [_row_spec(o.shape, tt) for o in outs]
    out_specs += [pl.BlockSpec(a.shape, lambda i, _n=len(a.shape): (0,) * _n) for a in accs]
    res = pl.pallas_call(
        body, name=name, grid=(t // tt,),
        in_specs=in_specs, out_specs=out_specs, out_shape=list(outs) + list(accs),
        compiler_params=_cparams(dimension_semantics=("arbitrary",)),
    )(*args)
    return res


def _row_spec(shape, tt):
    if len(shape) == 2:
        return pl.BlockSpec((tt, shape[1]), lambda i: (i, 0))
    return pl.BlockSpec((shape[0], tt, shape[2]), lambda i: (0, i, 0))


def _sds(shape, dtype=F32):
    return jax.ShapeDtypeStruct(tuple(shape), dtype)


def _acc(ref, val):
    i = pl.program_id(0)

    @pl.when(i == 0)
    def _():
        ref[...] = val

    @pl.when(i > 0)
    def _():
        ref[...] += val


def _rms(x, g):
    return x * lax.rsqrt(jnp.mean(x * x, axis=-1, keepdims=True) + NORM_EPS) * g


def _silu(x):
    return x / (1.0 + jnp.exp(-x))


def _softplus(x):
    return jnp.maximum(x, 0.0) + jnp.log(1.0 + jnp.exp(-jnp.abs(x)))


def _sigmoid(x):
    return 1.0 / (1.0 + jnp.exp(-x))


def _rmsnorm_fwd(name, x, g, tt=512):
    def body(x_ref, g_ref, h_ref):
        h_ref[...] = _rms(x_ref[...], g_ref[...]).astype(BF16)

    return _rows(name, body, [x], [_sds(x.shape, BF16)], tt=tt, consts=[g])[0]


def _rmsnorm_bwd(name, x, g, dh, dres, tt=512):
    def body(x_ref, dh_ref, dres_ref, g_ref, dx_ref, dg_ref):
        _, vjp = jax.vjp(_rms, x_ref[...], g_ref[...])
        dx, dg = vjp(dh_ref[...])
        dx_ref[...] = dx + dres_ref[...]
        _acc(dg_ref, dg)

    return _rows(name, body, [x, dh, dres], [_sds(x.shape)], tt=tt, consts=[g], accs=[_sds(g.shape)])


def _ffn_fwd(x, norm_g, w3, w_down):
    t, d = x.shape
    ns, _, cs = w3.shape
    half = ns // 2
    w2 = w3.reshape(ns * d, cs)
    h = _rmsnorm_fwd("ffn_norm", x, norm_g)
    tm = _pick(t, (1024, 512, 256, 128))
    nm = t // tm

    def gate_up(h_ref, wg_ref, wu_ref, g_ref, u_ref, a_ref):
        hv = h_ref[...]
        g = _dot(hv, wg_ref[...])
        u = _dot(hv, wu_ref[...])
        g_ref[...] = g
        u_ref[...] = u
        a_ref[...] = (_silu(g) * u).astype(BF16)

    hid = pl.BlockSpec((tm, cs), lambda j, i: (j * nm + i, 0))
    g, u, act = pl.pallas_call(
        gate_up, name="ffn_gate_up", grid=(half, nm),
        in_specs=[pl.BlockSpec((tm, d), lambda j, i: (i, 0)), pl.BlockSpec((d, cs), lambda j, i: (j, 0)),
                  pl.BlockSpec((d, cs), lambda j, i: (j + half, 0))],
        out_specs=[hid, hid, hid], out_shape=[_sds((half * t, cs)), _sds((half * t, cs)), _sds((half * t, cs), BF16)],
        compiler_params=_cparams(dimension_semantics=("parallel", "arbitrary")),
    )(h, w2, w2)
    def down(a_ref, w_ref, x_ref, y_ref):
        y = x_ref[...]
        for j in range(half):
            y = y + _dot(a_ref[j], w_ref[j])
        y_ref[...] = y

    y = pl.pallas_call(
        down, name="ffn_down", grid=(nm,),
        in_specs=[pl.BlockSpec((half, tm, cs), lambda i: (0, i, 0)), pl.BlockSpec((half, cs, d), lambda i: (0, 0, 0)),
                  pl.BlockSpec((tm, d), lambda i: (i, 0))],
        out_specs=pl.BlockSpec((tm, d), lambda i: (i, 0)), out_shape=_sds((t, d)),
        compiler_params=_cparams(dimension_semantics=("parallel",)),
    )(act.reshape(half, t, cs), w_down.reshape(half, cs, d), x)
    return y, (x, h, g, u, act)


def _ffn_bwd(saved, dy, norm_g, w3, w_down):
    x, h, g, u, act = saved
    t, d = x.shape
    ns, _, cs = w3.shape
    half = ns // 2
    w2 = w3.reshape(ns * d, cs)
    tm = _pick(t, (1024, 512, 256, 128))
    nm = t // tm
    tk = _pick(t, (4096, 2048, 1024, 512, 256, 128))
    nk = t // tk
    d_wdown = _mm_raw("ffn_down_wgrad", act, dy, ta=True, tb=False, out_dtype=GRAD_DTYPE, add=None, grid=(half, 1, nk),
                      out_shape=(half * cs, d), a_block=(tk, cs), a_map=lambda i, j, k: (i * nk + k, 0),
                      b_block=(tk, d), b_map=lambda i, j, k: (k, 0), o_block=(cs, d), o_map=lambda i, j, k: (i, 0))
    def down_dgrad(dy_ref, wd_ref, g_ref, u_ref, dg_ref, du_ref):
        da = _dot_nt(dy_ref[...].astype(BF16), wd_ref[...])
        gv, uv = g_ref[...], u_ref[...]
        s = _sigmoid(gv)
        dg_ref[...] = (da * uv * s * (1.0 + gv * (1.0 - s))).astype(BF16)
        du_ref[...] = (da * gv * s).astype(BF16)

    hid = pl.BlockSpec((tm, cs), lambda j, i: (j * nm + i, 0))
    dg, du = pl.pallas_call(
        down_dgrad, name="ffn_down_dgrad", grid=(half, nm),
        in_specs=[pl.BlockSpec((tm, d), lambda j, i: (i, 0)), pl.BlockSpec((cs, d), lambda j, i: (j, 0)), hid, hid],
        out_specs=[hid, hid], out_shape=[_sds((half * t, cs), BF16), _sds((half * t, cs), BF16)],
        compiler_params=_cparams(dimension_semantics=("parallel", "arbitrary")),
    )(dy, w_down, g, u)

    def wgrad(name, dd):
        return _mm_raw(name, h, dd, ta=True, tb=False, out_dtype=GRAD_DTYPE, add=None, grid=(1, half, nk), out_shape=(half * d, cs),
                       a_block=(tk, d), a_map=lambda i, j, k: (k, 0), b_block=(tk, cs), b_map=lambda i, j, k: (j * nk + k, 0),
                       o_block=(d, cs), o_map=lambda i, j, k: (j, 0))

    def gate_up_dgrad(dg_ref, du_ref, w_ref, dh_ref):
        dh = _dot_nt(dg_ref[0], w_ref[0]) + _dot_nt(du_ref[0], w_ref[half])
        for j in range(1, half):
            dh = dh + _dot_nt(dg_ref[j], w_ref[j]) + _dot_nt(du_ref[j], w_ref[half + j])
        dh_ref[...] = dh

    th = _pick(t, (512, 256, 128))
    hid3 = pl.BlockSpec((half, th, cs), lambda i: (0, i, 0))
    dh = pl.pallas_call(
        gate_up_dgrad, name="ffn_gate_up_dgrad", grid=(t // th,),
        in_specs=[hid3, hid3, pl.BlockSpec((ns, d, cs), lambda i: (0, 0, 0))],
        out_specs=pl.BlockSpec((th, d), lambda i: (i, 0)), out_shape=_sds((t, d)),
        compiler_params=_cparams(dimension_semantics=("parallel",)),
    )(dg.reshape(half, t, cs), du.reshape(half, t, cs), w3)
    d_w3 = jnp.concatenate([wgrad("ffn_gate_wgrad", dg), wgrad("ffn_up_wgrad", du)], axis=0).reshape(ns, d, cs)
    dx, dgain = _rmsnorm_bwd("ffn_norm_bwd", x, norm_g, dh, dy)
    return dx, dgain, d_w3, d_wdown


def _dot_nt(a, b):
    return lax.dot_general(a, b, (((1,), (1,)), ((), ())), preferred_element_type=F32)


def _dot_tn(a, b):
    return lax.dot_general(a, b, (((0,), (0,)), ((), ())), preferred_element_type=F32)


def _dot(a, b):
    return jnp.dot(a, b, preferred_element_type=F32)


CUM_BLOCK = 128


def _tri2(lower):
    r = lax.broadcasted_iota(jnp.int32, (CUM_BLOCK, CUM_BLOCK), 0)
    c = lax.broadcasted_iota(jnp.int32, (CUM_BLOCK, CUM_BLOCK), 1)
    tri = ((r > c) if lower else (r < c)).astype(BF16)
    return jnp.concatenate([tri, tri], axis=0)


def _run_sums(x, tri2, run, reverse):
    nb = x.shape[1] // CUM_BLOCK
    outs = [None] * nb
    for j in (reversed(range(nb)) if reverse else range(nb)):
        xj = x[:, j * CUM_BLOCK:(j + 1) * CUM_BLOCK]
        hi = xj.astype(BF16)
        lo = (xj - hi.astype(F32)).astype(BF16)
        outs[j] = _dot(jnp.concatenate([hi, lo], axis=1), tri2) + run
        run = run + jnp.sum(xj, axis=1, keepdims=True)
    return jnp.concatenate(outs, axis=1), run


def _log_sigmoid(z):
    return jnp.minimum(z, 0.0) - jnp.log(1.0 + jnp.exp(-jnp.abs(z)))


def _heads_in(ref, h, width=HEAD):
    return ref[:, h * width:(h + 1) * width]


def _sb_qk(q, k, gq, gk):
    return _rms(q, gq) * (HEAD ** -0.5), _rms(k, gk)


def _sb_prep_fwd(qkv, gq, gk):
    t = qkv.shape[0]

    def body(x_ref, gq_ref, gk_ref, q_ref, k_ref, v_ref):
        for h in range(N_HEADS):
            q, k = _sb_qk(_heads_in(x_ref, h), _heads_in(x_ref, N_HEADS + h), gq_ref[...], gk_ref[...])
            q_ref[h] = q.astype(BF16)
            k_ref[h] = k.astype(BF16)
            v_ref[h] = _heads_in(x_ref, 2 * N_HEADS + h).astype(BF16)

    hm = _sds((N_HEADS, t, HEAD), BF16)
    return _rows("sb_prep", body, [qkv], [hm, hm, hm], tt=256, consts=[gq, gk])


def _sb_prep_bwd(qkv, gq, gk, dq, dk, dv):
    def body(x_ref, dq_ref, dk_ref, dv_ref, gq_ref, gk_ref, dx_ref, dgq_ref, dgk_ref):
        dgq = jnp.zeros(gq_ref.shape, F32)
        dgk = jnp.zeros(gk_ref.shape, F32)
        for h in range(N_HEADS):
            _, vjp = jax.vjp(_sb_qk, _heads_in(x_ref, h), _heads_in(x_ref, N_HEADS + h), gq_ref[...], gk_ref[...])
            a, b, c, d = vjp((dq_ref[h], dk_ref[h]))
            dx_ref[:, h * HEAD:(h + 1) * HEAD] = a.astype(BF16)
            dx_ref[:, (N_HEADS + h) * HEAD:(N_HEADS + h + 1) * HEAD] = b.astype(BF16)
            dx_ref[:, (2 * N_HEADS + h) * HEAD:(2 * N_HEADS + h + 1) * HEAD] = dv_ref[h].astype(BF16)
            dgq, dgk = dgq + c, dgk + d
        _acc(dgq_ref, dgq)
        _acc(dgk_ref, dgk)

    return _rows("sb_prep_bwd", body, [qkv, dq, dk, dv], [_sds(qkv.shape, BF16)], tt=256, consts=[gq, gk],
                 accs=[_sds(gq.shape), _sds(gk.shape)])


def _q_block(t):
    return min(ATT_Q, t)


def _key_order(bq, qb):
    rows = lax.broadcasted_iota(jnp.int32, (bq, ATT_BLOCK), 0)
    cols = lax.broadcasted_iota(jnp.int32, (bq, ATT_BLOCK), 1)
    return rows - cols + qb * bq


def _sb_attn_fwd(q, k, v):
    nh, t, _ = q.shape
    bq = _q_block(t)
    per = bq // ATT_BLOCK

    def kern(q_ref, k_ref, v_ref, o_ref):
        qb = pl.program_id(1)
        qv = q_ref[0]
        after = _tri2(True)
        order = _key_order(bq, qb)
        nkb = (qb + 1) * per

        def body(i, carry, diagonal):
            o_acc, run = carry
            kb = nkb - 1 - i
            off = pl.multiple_of(kb * ATT_BLOCK, ATT_BLOCK)
            kv = k_ref[0, pl.ds(off, ATT_BLOCK), :]
            vv = v_ref[0, pl.ds(off, ATT_BLOCK), :]
            z = _dot_nt(qv, kv)
            lsz = _log_sigmoid(z)
            lsn = lsz - z
            if diagonal:
                past = order > kb * ATT_BLOCK
                lsn = jnp.where(past, lsn, 0.0)
            la, run = _run_sums(lsn, after, run, True)
            a = jnp.exp(lsz + la)
            if diagonal:
                a = jnp.where(past, a, 0.0)
            o_acc = o_acc + _dot(a.astype(BF16), vv)
            return o_acc, run

        carry = lax.fori_loop(0, per, lambda i, c: body(i, c, True), (jnp.zeros((bq, HEAD), F32), jnp.zeros((bq, 1), F32)))
        o, _ = lax.fori_loop(per, nkb, lambda i, c: body(i, c, False), carry)
        o_ref[...] = o

    return pl.pallas_call(
        kern, name="sb_attn_fwd", grid=(nh, t // bq),
        in_specs=[pl.BlockSpec((1, bq, HEAD), lambda h, i: (h, i, 0)),
                  pl.BlockSpec((1, t, HEAD), lambda h, i: (h, 0, 0)),
                  pl.BlockSpec((1, t, HEAD), lambda h, i: (h, 0, 0))],
        out_specs=pl.BlockSpec((bq, HEAD), lambda h, i: (i, h)),
        out_shape=_sds((t, nh * HEAD)),
        compiler_params=_cparams(dimension_semantics=("parallel", "arbitrary")),
    )(q, k, v)


def _sb_attn_bwd(q, k, v, do):
    nh, t, _ = q.shape
    bq = _q_block(t)
    per = bq // ATT_BLOCK

    def kern(q_ref, k_ref, v_ref, do_ref, dq_ref, dk_ref, dv_ref, g_s, ls_s):
        qb = pl.program_id(1)

        @pl.when(qb == 0)
        def _():
            dk_ref[...] = jnp.zeros(dk_ref.shape, F32)
            dv_ref[...] = jnp.zeros(dv_ref.shape, F32)

        qv = q_ref[0]
        dob = do_ref[...].astype(BF16)
        after, before = _tri2(True), _tri2(False)
        order = _key_order(bq, qb)
        nkb = (qb + 1) * per

        def sweep_left(i, run, diagonal):
            kb = nkb - 1 - i
            off = pl.multiple_of(kb * ATT_BLOCK, ATT_BLOCK)
            kv = k_ref[0, pl.ds(off, ATT_BLOCK), :]
            vv = v_ref[0, pl.ds(off, ATT_BLOCK), :]
            z = _dot_nt(qv, kv)
            lsz = _log_sigmoid(z)
            lsn = lsz - z
            if diagonal:
                past = order > kb * ATT_BLOCK
                lsn = jnp.where(past, lsn, 0.0)
            la, run = _run_sums(lsn, after, run, True)
            a = jnp.exp(lsz + la)
            if diagonal:
                a = jnp.where(past, a, 0.0)
            g_s[kb] = _dot_nt(dob, vv) * a
            ls_s[kb] = lsz
            dv_ref[0, pl.ds(off, ATT_BLOCK), :] += _dot_tn(a.astype(BF16), dob)
            return run

        zero = jnp.zeros((bq, 1), F32)
        run = lax.fori_loop(0, per, lambda i, c: sweep_left(i, c, True), zero)
        lax.fori_loop(per, nkb, lambda i, c: sweep_left(i, c, False), run)

        def sweep_right(kb, carry, diagonal):
            dq_acc, run_g = carry
            off = pl.multiple_of(kb * ATT_BLOCK, ATT_BLOCK)
            kv = k_ref[0, pl.ds(off, ATT_BLOCK), :]
            g = g_s[kb]
            sg = jnp.exp(ls_s[kb])
            dls, run_g = _run_sums(g, before, run_g, False)
            dz = g * (1.0 - sg) - dls * sg
            if diagonal:
                dz = jnp.where(order > kb * ATT_BLOCK, dz, 0.0)
            dzb = dz.astype(BF16)
            dk_ref[0, pl.ds(off, ATT_BLOCK), :] += _dot_tn(dzb, qv)
            return dq_acc + _dot(dzb, kv), run_g

        carry = lax.fori_loop(0, nkb - per, lambda i, c: sweep_right(i, c, False), (jnp.zeros((bq, HEAD), F32), zero))
        dq, _ = lax.fori_loop(nkb - per, nkb, lambda i, c: sweep_right(i, c, True), carry)
        dq_ref[0] = dq

    hm = _sds((nh, t, HEAD))
    full = pl.BlockSpec((1, t, HEAD), lambda h, i: (h, 0, 0))
    tok = pl.BlockSpec((bq, HEAD), lambda h, i: (i, h))
    nkb_max = t // ATT_BLOCK
    return pl.pallas_call(
        kern, name="sb_attn_bwd", grid=(nh, t // bq),
        in_specs=[pl.BlockSpec((1, bq, HEAD), lambda h, i: (h, i, 0)), full, full, tok],
        out_specs=[pl.BlockSpec((1, bq, HEAD), lambda h, i: (h, i, 0)), full, full],
        out_shape=[hm, hm, hm],
        scratch_shapes=[pltpu.VMEM((nkb_max, bq, ATT_BLOCK), F32), pltpu.VMEM((nkb_max, bq, ATT_BLOCK), F32)],
        compiler_params=_cparams(dimension_semantics=("parallel", "arbitrary")),
    )(q, k, v, do)


def _sb_fwd(x, h, w_qkv, gq, gk, w_out):
    qkv = _mm("sb_qkv", h, w_qkv)
    q, k, v = _sb_prep_fwd(qkv, gq, gk)
    o = _sb_attn_fwd(q, k, v)
    y = _mm("sb_out", o, w_out, add=x)
    return y, (h, qkv, q, k, v, o)


def _sb_bwd(saved, dy, w_qkv, gq, gk, w_out):
    h, qkv, q, k, v, o = saved
    d_wout = _mm("sb_out_wgrad", o, dy, ta=True, out_dtype=GRAD_DTYPE)
    do = _mm("sb_out_dgrad", dy, w_out, tb=True)
    dq, dk, dv = _sb_attn_bwd(q, k, v, do)
    dqkv, dgq, dgk = _sb_prep_bwd(qkv, gq, gk, dq, dk, dv)
    d_wqkv = _mm("sb_qkv_wgrad", h, dqkv, ta=True, out_dtype=GRAD_DTYPE)
    dh = _mm("sb_qkv_dgrad", dqkv, w_qkv, tb=True)
    return dh, d_wqkv, dgq, dgk, d_wout


DN_QKV = 3 * N_HEADS * HEAD
DN_PROJ = DN_QKV + N_HEADS * HEAD + LANES
DN_CONV = 4
HALO = 8
CONV_COLS = 512


def _dn_conv_fwd(proj, conv_w, tt=256):
    t = proj.shape[0]
    tt = min(tt, t)

    def body(u_ref, prev_ref, w_ref, c_ref):
        i = pl.program_id(0)
        for cc in range(DN_QKV // CONV_COLS):
            cs = slice(cc * CONV_COLS, (cc + 1) * CONV_COLS)
            cur = u_ref[:, cs]
            prev = jnp.where(i > 0, prev_ref[:, cs], 0.0)
            ext = jnp.concatenate([prev, cur], axis=0)
            y = cur * w_ref[DN_CONV - 1:DN_CONV, cs]
            for j in range(DN_CONV - 1):
                y = y + pltpu.roll(ext, DN_CONV - 1 - j, 0)[HALO:] * w_ref[j:j + 1, cs]
            c_ref[:, cs] = y

    return _rows("dn_conv", body,
                 [(proj, (tt, DN_QKV), lambda i: (i, 0)),
                  (proj, (HALO, DN_QKV), lambda i: (jnp.maximum(i * (tt // HALO) - 1, 0), 0))],
                 [_sds((t, DN_QKV))], tt=tt, consts=[conv_w])[0]


def _dn_conv_bwd(proj, conv_w, dc, dz, dab, tt=256):
    t = proj.shape[0]
    tt = min(tt, t)
    nblk = t // tt

    def body(u_ref, prev_ref, dc_ref, next_ref, dz_ref, dab_ref, w_ref, dp_ref, dw_ref):
        i = pl.program_id(0)
        dws = []
        for cc in range(DN_QKV // CONV_COLS):
            cs = slice(cc * CONV_COLS, (cc + 1) * CONV_COLS)
            cur = u_ref[:, cs]
            prev = jnp.where(i > 0, prev_ref[:, cs], 0.0)
            ext_u = jnp.concatenate([prev, cur], axis=0)
            d = dc_ref[:, cs]
            nxt = jnp.where(i < nblk - 1, next_ref[:, cs], 0.0)
            ext_d = jnp.concatenate([d, nxt], axis=0)
            du = d * w_ref[DN_CONV - 1:DN_CONV, cs]
            rows = [jnp.sum(d * cur, axis=0, keepdims=True)]
            for j in range(DN_CONV - 2, -1, -1):
                sh = DN_CONV - 1 - j
                du = du + pltpu.roll(ext_d, tt + HALO - sh, 0)[:tt] * w_ref[j:j + 1, cs]
                rows.insert(0, jnp.sum(d * pltpu.roll(ext_u, sh, 0)[HALO:], axis=0, keepdims=True))
            dp_ref[:, cs] = du.astype(BF16)
            dws.append(jnp.concatenate(rows, axis=0))
        dp_ref[:, DN_QKV:DN_QKV + N_HEADS * HEAD] = dz_ref[...].astype(BF16)
        dp_ref[:, DN_QKV + N_HEADS * HEAD:] = dab_ref[...].astype(BF16)
        _acc(dw_ref, jnp.concatenate(dws, axis=1))

    return _rows("dn_conv_bwd", body,
                 [(proj, (tt, DN_QKV), lambda i: (i, 0)),
                  (proj, (HALO, DN_QKV), lambda i: (jnp.maximum(i * (tt // HALO) - 1, 0), 0)),
                  dc,
                  (dc, (HALO, DN_QKV), lambda i: (jnp.minimum((i + 1) * (tt // HALO), t // HALO - 1), 0)),
                  dz, dab],
                 [_sds((t, DN_PROJ), BF16)], tt=tt, consts=[conv_w], accs=[_sds(conv_w.shape)])


def _l2n(x):
    return x * lax.rsqrt(jnp.sum(x * x, axis=-1, keepdims=True) + NORM_EPS)


def _dn_qkv(cq, ck, cv):
    return _l2n(_silu(cq)) * (HEAD ** -0.5), _l2n(_silu(ck)), _silu(cv)


def _dn_gates(ab, a_log, dt_bias):
    lane = lax.broadcasted_iota(jnp.int32, ab.shape, 1)
    g = -jnp.exp(a_log) * _softplus(ab + dt_bias)
    return jnp.where(lane < N_HEADS, g, jnp.where(lane < 2 * N_HEADS, _sigmoid(ab), 0.0))


def _ab_spec(tt):
    return (tt, LANES), lambda i: (i, DN_PROJ // LANES - 1)


def _dn_prep_fwd(c, proj, a_log, dt_bias, tt=256):
    t = c.shape[0]
    tt = min(tt, t)

    def body(c_ref, ab_ref, al_ref, dt_ref, q_ref, k_ref, v_ref, g_ref):
        for h in range(N_HEADS):
            q_ref[h], k_ref[h], v_ref[h] = _dn_qkv(_heads_in(c_ref, h), _heads_in(c_ref, N_HEADS + h), _heads_in(c_ref, 2 * N_HEADS + h))
        g_ref[...] = _dn_gates(ab_ref[...], al_ref[...], dt_ref[...])

    hm = _sds((N_HEADS, t, HEAD))
    return _rows("dn_prep", body, [c, (proj,) + _ab_spec(tt)], [hm, hm, hm, _sds((t, LANES))], tt=tt, consts=[a_log, dt_bias])


def _dn_prep_bwd(c, proj, a_log, dt_bias, dq, dk, dv, dgates, tt=256):
    t = c.shape[0]
    tt = min(tt, t)

    def body(c_ref, ab_ref, dq_ref, dk_ref, dv_ref, dg_ref, al_ref, dt_ref, dc_ref, dab_ref, dal_ref, ddt_ref):
        for h in range(N_HEADS):
            _, vjp = jax.vjp(_dn_qkv, _heads_in(c_ref, h), _heads_in(c_ref, N_HEADS + h), _heads_in(c_ref, 2 * N_HEADS + h))
            a, b, d = vjp((dq_ref[h], dk_ref[h], dv_ref[h]))
            dc_ref[:, h * HEAD:(h + 1) * HEAD] = a
            dc_ref[:, (N_HEADS + h) * HEAD:(N_HEADS + h + 1) * HEAD] = b
            dc_ref[:, (2 * N_HEADS + h) * HEAD:(2 * N_HEADS + h + 1) * HEAD] = d
        _, vjp = jax.vjp(_dn_gates, ab_ref[...], al_ref[...], dt_ref[...])
        dab, dal, ddt = vjp(dg_ref[...])
        dab_ref[...] = dab
        _acc(dal_ref, dal)
        _acc(ddt_ref, ddt)

    return _rows("dn_prep_bwd", body, [c, (proj,) + _ab_spec(tt), dq, dk, dv, dgates], [_sds(c.shape), _sds((t, LANES))],
                 tt=tt, consts=[a_log, dt_bias], accs=[_sds(a_log.shape), _sds(dt_bias.shape)])


def _bdot(a, b, prec=None):
    return lax.dot_general(a, b, (((2,), (1,)), ((0,), (0,))), precision=prec, preferred_element_type=F32)


def _bdot_nt(a, b, prec=None):
    return lax.dot_general(a, b, (((2,), (2,)), ((0,), (0,))), precision=prec, preferred_element_type=F32)


def _bdot_tn(a, b, prec=None):
    return lax.dot_general(a, b, (((1,), (1,)), ((0,), (0,))), precision=prec, preferred_element_type=F32)


def _inv_raw(low):
    c = low.shape[-1]
    r = lax.broadcasted_iota(jnp.int32, (c, c), 0)
    s = lax.broadcasted_iota(jnp.int32, (c, c), 1)
    m = jnp.where(r == s, 1.0, 0.0) - low
    p = _bdot(low, low, F32X3)
    n_fac = int(math.log2(c)) - 1
    for i in range(n_fac):
        m = m + _bdot(m, p, F32X3)
        if i < n_fac - 1:
            p = _bdot(p, p, F32X3)
    return m


@jax.custom_vjp
def _inv_unit_lower(low):
    return _inv_raw(low)


def _inv_fwd(low):
    m = _inv_raw(low)
    return m, m


def _inv_bwd(m, dm):
    return (-_bdot_nt(_bdot_tn(m, dm, F32X3), m, F32X3),)


_inv_unit_lower.defvjp(_inv_fwd, _inv_bwd)


def _dn_chunk(q, k, v, gates, s):
    nh, c, _ = q.shape
    lane = lax.broadcasted_iota(jnp.int32, gates.shape, 1)
    def column(j):
        return jnp.sum(jnp.where(lane == j, gates, 0.0), axis=1, keepdims=True)[None]

    g_col = jnp.concatenate([column(h) for h in range(nh)], axis=0)
    b_col = jnp.concatenate([column(h + nh) for h in range(nh)], axis=0)
    r = lax.broadcasted_iota(jnp.int32, (c, c), 0)
    cc = lax.broadcasted_iota(jnp.int32, (c, c), 1)
    causal, strict = r >= cc, r > cc
    incl = jnp.broadcast_to(jnp.where(causal, 1.0, 0.0), (nh, c, c))
    upper = jnp.broadcast_to(jnp.where(r <= cc, 1.0, 0.0), (nh, c, c))
    gb = jnp.broadcast_to(g_col, (nh, c, LANES))
    gbc = jnp.broadcast_to(g_col, (nh, c, c))
    gc = _bdot(incl, gb, F32X3)
    gc_r = _bdot(incl, gbc, F32X3)
    gc_c = _bdot_tn(gbc, upper, F32X3)
    decay = jnp.where(causal, jnp.exp(jnp.where(causal, gc_r - gc_c, 0.0)), 0.0)
    kb = k * b_col
    low = jnp.where(strict, _bdot_nt(kb, k) * decay, 0.0)
    m = _inv_unit_lower(low)
    egc = jnp.exp(gc)
    u = _bdot(m, v * b_col, F32X3)
    w = _bdot(m, kb * egc, F32X3)
    attn = _bdot_nt(q, k) * decay
    gl = jnp.sum(gb, axis=1, keepdims=True)
    v_new = u - _bdot(w, s)
    o = _bdot(q * egc, s) + _bdot(attn, v_new)
    s_new = s * jnp.exp(gl) + _bdot_tn(k * jnp.exp(gl - gc), v_new)
    return o, s_new


def _dn_chunks_fwd(q, k, v, gates):
    nh, t, _ = q.shape
    n = t // DN_CHUNK

    def kern(q_ref, k_ref, v_ref, g_ref, o_ref, sin_ref, s_scr):
        @pl.when(pl.program_id(0) == 0)
        def _():
            s_scr[...] = jnp.zeros(s_scr.shape, F32)

        s = s_scr[...]
        sin_ref[0] = s
        o_ref[...], s_scr[...] = _dn_chunk(q_ref[...], k_ref[...], v_ref[...], g_ref[...], s)

    blk = pl.BlockSpec((nh, DN_CHUNK, HEAD), lambda i: (0, i, 0))
    return pl.pallas_call(
        kern, name="dn_chunks_fwd", grid=(n,),
        in_specs=[blk, blk, blk, pl.BlockSpec((DN_CHUNK, LANES), lambda i: (i, 0))],
        out_specs=[blk, pl.BlockSpec((1, nh, HEAD, HEAD), lambda i: (i, 0, 0, 0))],
        out_shape=[_sds((nh, t, HEAD)), _sds((n, nh, HEAD, HEAD))],
        scratch_shapes=[pltpu.VMEM((nh, HEAD, HEAD), F32)],
        compiler_params=_cparams(dimension_semantics=("arbitrary",)),
    )(q, k, v, gates)


def _dn_chunks_bwd(q, k, v, gates, s_in, do):
    nh, t, _ = q.shape
    n = t // DN_CHUNK

    def kern(q_ref, k_ref, v_ref, g_ref, sin_ref, do_ref, dq_ref, dk_ref, dv_ref, dg_ref, ds_scr):
        @pl.when(pl.program_id(0) == 0)
        def _():
            ds_scr[...] = jnp.zeros(ds_scr.shape, F32)

        _, vjp = jax.vjp(_dn_chunk, q_ref[...], k_ref[...], v_ref[...], g_ref[...], sin_ref[0])
        dq_ref[...], dk_ref[...], dv_ref[...], dg_ref[...], ds_scr[...] = vjp((do_ref[...], ds_scr[...]))

    blk = pl.BlockSpec((nh, DN_CHUNK, HEAD), lambda i: (0, n - 1 - i, 0))
    gblk = pl.BlockSpec((DN_CHUNK, LANES), lambda i: (n - 1 - i, 0))
    hm = _sds((nh, t, HEAD))
    return pl.pallas_call(
        kern, name="dn_chunks_bwd", grid=(n,),
        in_specs=[blk, blk, blk, gblk, pl.BlockSpec((1, nh, HEAD, HEAD), lambda i: (n - 1 - i, 0, 0, 0)), blk],
        out_specs=[blk, blk, blk, gblk],
        out_shape=[hm, hm, hm, _sds((t, LANES))],
        scratch_shapes=[pltpu.VMEM((nh, HEAD, HEAD), F32)],
        compiler_params=_cparams(dimension_semantics=("arbitrary",)),
    )(q, k, v, gates, s_in, do)


def _dn_gate_out(o, z, g):
    return _rms(o, g) * _silu(z)


def _z_spec(tt):
    return (tt, N_HEADS * HEAD), lambda i: (i, DN_QKV // (N_HEADS * HEAD))


def _dn_post_fwd(o, proj, out_norm, tt=256):
    t = o.shape[1]
    tt = min(tt, t)

    def body(o_ref, z_ref, g_ref, y_ref):
        for h in range(N_HEADS):
            y_ref[:, h * HEAD:(h + 1) * HEAD] = _dn_gate_out(o_ref[h], _heads_in(z_ref, h), g_ref[...]).astype(BF16)

    return _rows("dn_post", body, [o, (proj,) + _z_spec(tt)], [_sds((t, N_HEADS * HEAD), BF16)], tt=tt, consts=[out_norm])[0]


def _dn_post_bwd(o, proj, out_norm, dy, tt=256):
    t = o.shape[1]
    tt = min(tt, t)

    def body(o_ref, z_ref, dy_ref, g_ref, do_ref, dz_ref, dg_ref):
        dg = jnp.zeros(g_ref.shape, F32)
        for h in range(N_HEADS):
            _, vjp = jax.vjp(_dn_gate_out, o_ref[h], _heads_in(z_ref, h), g_ref[...])
            a, b, d = vjp(_heads_in(dy_ref, h))
            do_ref[h] = a
            dz_ref[:, h * HEAD:(h + 1) * HEAD] = b
            dg = dg + d
        _acc(dg_ref, dg)

    return _rows("dn_post_bwd", body, [o, (proj,) + _z_spec(tt), dy], [_sds(o.shape), _sds((t, N_HEADS * HEAD))], tt=tt,
                 consts=[out_norm], accs=[_sds(out_norm.shape)])


def _dn_fwd(x, h, w_in, conv_w, a_log, dt_bias, out_norm, w_out):
    proj = _mm("dn_in", h, w_in)
    c = _dn_conv_fwd(proj, conv_w)
    q, k, v, gates = _dn_prep_fwd(c, proj, a_log, dt_bias)
    o, s_in = _dn_chunks_fwd(q, k, v, gates)
    on = _dn_post_fwd(o, proj, out_norm)
    y = _mm("dn_out", on, w_out, add=x)
    return y, (h, proj, c, q, k, v, gates, o, s_in, on)


def _dn_bwd(saved, dy, w_in, conv_w, a_log, dt_bias, out_norm, w_out):
    h, proj, c, q, k, v, gates, o, s_in, on = saved
    d_wout = _mm("dn_out_wgrad", on, dy, ta=True, out_dtype=GRAD_DTYPE)
    don = _mm("dn_out_dgrad", dy, w_out, tb=True)
    do, dz, d_out_norm = _dn_post_bwd(o, proj, out_norm, don)
    dq, dk, dv, dgates = _dn_chunks_bwd(q, k, v, gates, s_in, do)
    dc, dab, d_a_log, d_dt_bias = _dn_prep_bwd(c, proj, a_log, dt_bias, dq, dk, dv, dgates)
    dproj, d_conv_w = _dn_conv_bwd(proj, conv_w, dc, dz, dab)
    d_win = _mm("dn_in_wgrad", h, dproj, ta=True, out_dtype=GRAD_DTYPE)
    dh = _mm("dn_in_dgrad", dproj, w_in, tb=True)
    return dh, d_win, d_conv_w, d_a_log, d_dt_bias, d_out_norm, d_wout


MLA_SCALE = MLA_QK ** -0.5
MLA_C = 512


def _swap_raw(x):
    lane = lax.broadcasted_iota(jnp.int32, x.shape, 1)
    half = MLA_ROPE // 2
    y = jnp.where(lane < half, pltpu.roll(x, LANES - half, 1), pltpu.roll(x, half, 1))
    return jnp.where(lane < MLA_ROPE, y, 0.0)


@jax.custom_vjp
def _swap_halves(x):
    return _swap_raw(x)


_swap_halves.defvjp(lambda x: (_swap_raw(x), None), lambda _, d: (_swap_raw(d),))


def _rms_rope(x, g, cos, sin):
    y = x * lax.rsqrt(jnp.sum(x * x, axis=-1, keepdims=True) * (1.0 / MLA_ROPE) + NORM_EPS) * g
    return y * cos + _swap_halves(y) * sin


def _mla_latent(cq, ckv, kr, gq, gkv, gkr, cos, sin):
    return _rms(cq, gq), _rms(ckv, gkv), _rms_rope(kr, gkr, cos, sin)


def _mla_prep1_fwd(c, gq, gkv, gkr, cos, sin):
    t = c.shape[0]

    def body(c_ref, cos_ref, sin_ref, gq_ref, gkv_ref, gkr_ref, cq_ref, ckv_ref, kr_ref):
        a, b, r = _mla_latent(c_ref[:, :256], c_ref[:, 256:384], c_ref[:, 384:], gq_ref[...], gkv_ref[...], gkr_ref[...],
                              cos_ref[...], sin_ref[...])
        cq_ref[...] = a.astype(BF16)
        ckv_ref[...] = b.astype(BF16)
        kr_ref[...] = r.astype(BF16)

    return _rows("mla_prep1", body, [c, cos, sin], [_sds((t, 256), BF16), _sds((t, HEAD), BF16), _sds((t, HEAD), BF16)],
                 tt=512, consts=[gq, gkv, gkr])


def _mla_prep1_bwd(c, gq, gkv, gkr, cos, sin, dcq, dckv, dkr_heads):
    def body(c_ref, cos_ref, sin_ref, dcq_ref, dckv_ref, dkr_ref, gq_ref, gkv_ref, gkr_ref, dc_ref, dgq_ref, dgkv_ref, dgkr_ref):
        dkr = dkr_ref[0]
        for h in range(1, N_HEADS):
            dkr = dkr + dkr_ref[h]
        _, vjp = jax.vjp(_mla_latent, c_ref[:, :256], c_ref[:, 256:384], c_ref[:, 384:], gq_ref[...], gkv_ref[...], gkr_ref[...],
                         cos_ref[...], sin_ref[...])
        a, b, r, d1, d2, d3, _, _ = vjp((dcq_ref[...], dckv_ref[...], dkr))
        dc_ref[:, :256] = a.astype(BF16)
        dc_ref[:, 256:384] = b.astype(BF16)
        dc_ref[:, 384:] = r.astype(BF16)
        _acc(dgq_ref, d1)
        _acc(dgkv_ref, d2)
        _acc(dgkr_ref, d3)

    return _rows("mla_prep1_bwd", body, [c, cos, sin, dcq, dckv, dkr_heads], [_sds(c.shape, BF16)], tt=512,
                 consts=[gq, gkv, gkr], accs=[_sds(gq.shape), _sds(gkv.shape), _sds(gkr.shape)])


def _mla_heads(qn, qr, kn, gqn, gqr, gkn, cos, sin):
    return _rms(qn, gqn) * MLA_SCALE, _rms_rope(qr, gqr, cos, sin) * MLA_SCALE, _rms(kn, gkn)


def _mla_prep2_fwd(qa, kv, gqn, gqr, gkn, cos, sin):
    t = qa.shape[0]

    def body(qa_ref, kv_ref, cos_ref, sin_ref, gqn_ref, gqr_ref, gkn_ref, qn_ref, qr_ref, kn_ref, v_ref):
        for h in range(N_HEADS):
            a, b, c = _mla_heads(_heads_in(qa_ref, h), _heads_in(qa_ref, N_HEADS + h), _heads_in(kv_ref, h),
                                 gqn_ref[...], gqr_ref[...], gkn_ref[...], cos_ref[...], sin_ref[...])
            qn_ref[h] = a.astype(BF16)
            qr_ref[h] = b.astype(BF16)
            kn_ref[h] = c.astype(BF16)
            v_ref[h] = _heads_in(kv_ref, N_HEADS + h).astype(BF16)

    hm = _sds((N_HEADS, t, HEAD), BF16)
    return _rows("mla_prep2", body, [qa, kv, cos, sin], [hm, hm, hm, hm], tt=256, consts=[gqn, gqr, gkn])


def _mla_prep2_bwd(qa, kv, gqn, gqr, gkn, cos, sin, dqn, dqr, dkn, dv):
    def body(qa_ref, kv_ref, cos_ref, sin_ref, dqn_ref, dqr_ref, dkn_ref, dv_ref, gqn_ref, gqr_ref, gkn_ref,
             dqa_ref, dkv_ref, d1_ref, d2_ref, d3_ref):
        d1 = jnp.zeros(gqn_ref.shape, F32)
        d2 = jnp.zeros(gqr_ref.shape, F32)
        d3 = jnp.zeros(gkn_ref.shape, F32)
        for h in range(N_HEADS):
            _, vjp = jax.vjp(_mla_heads, _heads_in(qa_ref, h), _heads_in(qa_ref, N_HEADS + h), _heads_in(kv_ref, h),
                             gqn_ref[...], gqr_ref[...], gkn_ref[...], cos_ref[...], sin_ref[...])
            a, b, c, e1, e2, e3, _, _ = vjp((dqn_ref[h], dqr_ref[h], dkn_ref[h]))
            dqa_ref[:, h * HEAD:(h + 1) * HEAD] = a.astype(BF16)
            dqa_ref[:, (N_HEADS + h) * HEAD:(N_HEADS + h + 1) * HEAD] = b.astype(BF16)
            dkv_ref[:, h * HEAD:(h + 1) * HEAD] = c.astype(BF16)
            dkv_ref[:, (N_HEADS + h) * HEAD:(N_HEADS + h + 1) * HEAD] = dv_ref[h].astype(BF16)
            d1, d2, d3 = d1 + e1, d2 + e2, d3 + e3
        _acc(d1_ref, d1)
        _acc(d2_ref, d2)
        _acc(d3_ref, d3)

    return _rows("mla_prep2_bwd", body, [qa, kv, cos, sin, dqn, dqr, dkn, dv], [_sds(qa.shape, BF16), _sds(kv.shape, BF16)],
                 tt=256, consts=[gqn, gqr, gkn], accs=[_sds(gqn.shape), _sds(gqr.shape), _sds(gkn.shape)])


def _mla_attn_fwd(qn, qr, kn, kr, v):
    nh, t, _ = qn.shape
    bq = _q_block(t)
    per = bq // ATT_BLOCK

    def kern(qn_ref, qr_ref, kn_ref, kr_ref, v_ref, o_ref, lse_ref):
        qb = pl.program_id(1)
        qv = jnp.concatenate([qn_ref[0], qr_ref[0]], axis=1)
        order = _key_order(bq, qb)

        def body(kb, carry, diagonal):
            acc, m, l = carry
            off = pl.multiple_of(kb * ATT_BLOCK, ATT_BLOCK)
            kv = jnp.concatenate([kn_ref[0, pl.ds(off, ATT_BLOCK), :], kr_ref[pl.ds(off, ATT_BLOCK), :]], axis=1)
            s = _dot_nt(qv, kv)
            if diagonal:
                s = jnp.where(order >= kb * ATT_BLOCK, s, -jnp.inf)
            m_new = jnp.maximum(m, jnp.max(s, axis=1, keepdims=True))
            alpha = jnp.exp(m - m_new)
            p = jnp.exp(s - m_new)
            acc = acc * alpha + _dot(p.astype(BF16), v_ref[0, pl.ds(off, ATT_BLOCK), :])
            return acc, m_new, l * alpha + jnp.sum(p, axis=1, keepdims=True)

        init = (jnp.zeros((bq, HEAD), F32), jnp.full((bq, 1), -jnp.inf, F32), jnp.zeros((bq, 1), F32))
        carry = lax.fori_loop(0, qb * per, lambda i, c: body(i, c, False), init)
        acc, m, l = lax.fori_loop(qb * per, (qb + 1) * per, lambda i, c: body(i, c, True), carry)
        o_ref[...] = acc / l
        lse_ref[...] = jnp.broadcast_to(m + jnp.log(l), (bq, HEAD))

    blk = pl.BlockSpec((1, bq, HEAD), lambda h, i: (h, i, 0))
    full = pl.BlockSpec((1, t, HEAD), lambda h, i: (h, 0, 0))
    tok = pl.BlockSpec((bq, HEAD), lambda h, i: (i, h))
    return pl.pallas_call(
        kern, name="mla_attn_fwd", grid=(nh, t // bq),
        in_specs=[blk, blk, full, pl.BlockSpec((t, HEAD), lambda h, i: (0, 0)), full],
        out_specs=[tok, tok], out_shape=[_sds((t, nh * HEAD)), _sds((t, nh * HEAD))],
        compiler_params=_cparams(dimension_semantics=("parallel", "arbitrary")),
    )(qn, qr, kn, kr, v)


def _mla_attn_bwd(qn, qr, kn, kr, v, o, lse, do):
    nh, t, _ = qn.shape
    bq = _q_block(t)
    per = bq // ATT_BLOCK

    def kern(qn_ref, qr_ref, kn_ref, kr_ref, v_ref, o_ref, lse_ref, do_ref, dqn_ref, dqr_ref, dkn_ref, dkr_ref, dv_ref):
        qb = pl.program_id(1)

        @pl.when(qb == 0)
        def _():
            dkn_ref[...] = jnp.zeros(dkn_ref.shape, F32)
            dkr_ref[...] = jnp.zeros(dkr_ref.shape, F32)
            dv_ref[...] = jnp.zeros(dv_ref.shape, F32)

        qv = jnp.concatenate([qn_ref[0], qr_ref[0]], axis=1)
        dov = do_ref[...]
        dob = dov.astype(BF16)
        delta = jnp.sum(dov * o_ref[...], axis=1, keepdims=True)
        lse_col = lse_ref[:, :1]
        order = _key_order(bq, qb)

        def body(kb, dq, diagonal):
            off = pl.multiple_of(kb * ATT_BLOCK, ATT_BLOCK)
            kv = jnp.concatenate([kn_ref[0, pl.ds(off, ATT_BLOCK), :], kr_ref[pl.ds(off, ATT_BLOCK), :]], axis=1)
            vv = v_ref[0, pl.ds(off, ATT_BLOCK), :]
            p = jnp.exp(_dot_nt(qv, kv) - lse_col)
            if diagonal:
                p = jnp.where(order >= kb * ATT_BLOCK, p, 0.0)
            ds = (p * (_dot_nt(dob, vv) - delta)).astype(BF16)
            dk = _dot_tn(ds, qv)
            dkn_ref[0, pl.ds(off, ATT_BLOCK), :] += dk[:, :HEAD]
            dkr_ref[0, pl.ds(off, ATT_BLOCK), :] += dk[:, HEAD:]
            dv_ref[0, pl.ds(off, ATT_BLOCK), :] += _dot_tn(p.astype(BF16), dob)
            return dq + _dot(ds, kv)

        dq = lax.fori_loop(0, qb * per, lambda i, c: body(i, c, False), jnp.zeros((bq, 2 * HEAD), F32))
        dq = lax.fori_loop(qb * per, (qb + 1) * per, lambda i, c: body(i, c, True), dq)
        dqn_ref[0] = dq[:, :HEAD]
        dqr_ref[0] = dq[:, HEAD:]

    hm = _sds((nh, t, HEAD))
    blk = pl.BlockSpec((1, bq, HEAD), lambda h, i: (h, i, 0))
    full = pl.BlockSpec((1, t, HEAD), lambda h, i: (h, 0, 0))
    tok = pl.BlockSpec((bq, HEAD), lambda h, i: (i, h))
    return pl.pallas_call(
        kern, name="mla_attn_bwd", grid=(nh, t // bq),
        in_specs=[blk, blk, full, pl.BlockSpec((t, HEAD), lambda h, i: (0, 0)), full, tok, tok, tok],
        out_specs=[blk, blk, full, full, full], out_shape=[hm, hm, hm, hm, hm],
        compiler_params=_cparams(dimension_semantics=("parallel", "arbitrary")),
    )(qn, qr, kn, kr, v, o, lse, do)


def _rope_tables(t):
    inv_freq = ROPE_THETA ** (-jnp.arange(0, MLA_ROPE, 2, dtype=F32) / MLA_ROPE)
    ang = jnp.arange(t, dtype=F32)[:, None] * inv_freq[None, :]
    c, s = jnp.cos(ang), jnp.sin(ang)
    pad = ((0, 0), (0, LANES - MLA_ROPE))
    return jnp.pad(jnp.concatenate([c, c], axis=1), pad), jnp.pad(jnp.concatenate([-s, s], axis=1), pad)


def _pad_lanes(v, n=LANES):
    return jnp.pad(v, (0, n - v.shape[0])).reshape(1, n)


def _mla_layout(w_down, w_uq, w_ukv):
    w_down_p = jnp.pad(w_down, ((0, 0), (0, MLA_C - w_down.shape[1])))
    uq = w_uq.reshape(w_uq.shape[0], N_HEADS, MLA_QK)
    rope = jnp.pad(uq[:, :, HEAD:], ((0, 0), (0, 0), (0, LANES - MLA_ROPE)))
    w_uq_p = jnp.concatenate([uq[:, :, :HEAD].reshape(-1, N_HEADS * HEAD), rope.reshape(-1, N_HEADS * LANES)], axis=1)
    ukv = w_ukv.reshape(w_ukv.shape[0], N_HEADS, 2 * HEAD)
    w_ukv_p = jnp.concatenate([ukv[:, :, :HEAD].reshape(-1, N_HEADS * HEAD), ukv[:, :, HEAD:].reshape(-1, N_HEADS * HEAD)], axis=1)
    return w_down_p, w_uq_p, w_ukv_p


def _mla_unlayout(d_down_p, d_uq_p, d_ukv_p):
    d_down = d_down_p[:, :256 + HEAD + MLA_ROPE]
    nope = d_uq_p[:, :N_HEADS * HEAD].reshape(-1, N_HEADS, HEAD)
    rope = d_uq_p[:, N_HEADS * HEAD:].reshape(-1, N_HEADS, LANES)[:, :, :MLA_ROPE]
    d_uq = jnp.concatenate([nope, rope], axis=2).reshape(-1, N_HEADS * MLA_QK)
    kn = d_ukv_p[:, :N_HEADS * HEAD].reshape(-1, N_HEADS, HEAD)
    vv = d_ukv_p[:, N_HEADS * HEAD:].reshape(-1, N_HEADS, HEAD)
    d_ukv = jnp.concatenate([kn, vv], axis=2).reshape(-1, N_HEADS * 2 * HEAD)
    return d_down, d_uq, d_ukv


def _mla_weight_shapes():
    return (_sds((1024, MLA_C), BF16), _sds((1, 256)), _sds((1, HEAD)), _sds((256, 2048), BF16), _sds((HEAD, 2048), BF16),
            _sds((1, HEAD)), _sds((1, HEAD)), _sds((1, HEAD)), _sds((1, HEAD)), _sds((1024, 1024), BF16),
            _sds((4096, HEAD)), _sds((4096, HEAD)))


def _mla_fwd(x, h, w_down, gq, gkv, w_uq, w_ukv, gqn, gqr, gkn, gkr, w_out, cos, sin):
    c = _mm("mla_down", h, w_down)
    cq, ckv, kr = _mla_prep1_fwd(c, gq, gkv, gkr, cos, sin)
    qa = _mm("mla_uq", cq, w_uq)
    kv = _mm("mla_ukv", ckv, w_ukv)
    qn, qr, kn, v = _mla_prep2_fwd(qa, kv, gqn, gqr, gkn, cos, sin)
    o, lse = _mla_attn_fwd(qn, qr, kn, kr, v)
    y = _mm("mla_out", o, w_out, add=x)
    return y, (h, c, cq, ckv, kr, qa, kv, qn, qr, kn, v, o, lse)


def _mla_bwd(saved, dy, w_down, gq, gkv, w_uq, w_ukv, gqn, gqr, gkn, gkr, w_out, cos, sin):
    h, c, cq, ckv, kr, qa, kv, qn, qr, kn, v, o, lse = saved
    d_wout = _mm("mla_out_wgrad", o, dy, ta=True, out_dtype=GRAD_DTYPE)
    do = _mm("mla_out_dgrad", dy, w_out, tb=True)
    dqn, dqr, dkn, dkr, dv = _mla_attn_bwd(qn, qr, kn, kr, v, o, lse, do)
    dqa, dkv, dgqn, dgqr, dgkn = _mla_prep2_bwd(qa, kv, gqn, gqr, gkn, cos, sin, dqn, dqr, dkn, dv)
    d_wuq = _mm("mla_uq_wgrad", cq, dqa, ta=True, out_dtype=GRAD_DTYPE)
    d_wukv = _mm("mla_ukv_wgrad", ckv, dkv, ta=True, out_dtype=GRAD_DTYPE)
    dcq = _mm("mla_uq_dgrad", dqa, w_uq, tb=True)
    dckv = _mm("mla_ukv_dgrad", dkv, w_ukv, tb=True)
    dc, dgq, dgkv, dgkr = _mla_prep1_bwd(c, gq, gkv, gkr, cos, sin, dcq, dckv, dkr)
    d_wdown = _mm("mla_down_wgrad", h, dc, ta=True, out_dtype=GRAD_DTYPE)
    dh = _mm("mla_down_dgrad", dc, w_down, tb=True)
    return dh, d_wdown, dgq, dgkv, d_wuq, d_wukv, dgqn, dgqr, dgkn, dgkr, d_wout


def _loss_head(y, target):
    d = y.shape[1]

    def body(y_ref, t_ref, dy_ref, l_ref):
        err = y_ref[...] - t_ref[...]
        dy_ref[...] = err * (1.0 / d)
        part = 0.5 * jnp.sum(jnp.sum(err * err, axis=1, keepdims=True) * (1.0 / d), axis=0, keepdims=True)
        _acc(l_ref, jnp.broadcast_to(part, (1, LANES)))

    return _rows("loss_head", body, [y, target], [_sds(y.shape)], tt=512, accs=[_sds((1, LANES))])


MESH_ID = pl.DeviceIdType.MESH
HBM_SPEC = pl.BlockSpec(memory_space=pltpu.HBM)


def _all_gather(name, x):
    m_per, n = x.shape

    def body(x_ref, out_ref, send_sems, recv_sems, local_sem):
        x, y, c = lax.axis_index("x"), lax.axis_index("y"), lax.axis_index("c")
        me, sibling = (x, y, c), (x, y, 1 - c)
        chips = [(1 - x, y), (x, 1 - y), (1 - x, 1 - y)]

        def rows(px, py, pc):
            return out_ref.at[pl.ds((4 * px + 2 * py + pc) * m_per, m_per), :]

        def copy(k, block, to, src=None):
            return pltpu.make_async_remote_copy(
                src_ref=rows(*block) if src is None else src, dst_ref=rows(*block),
                send_sem=send_sems.at[k], recv_sem=recv_sems.at[k], device_id=to, device_id_type=MESH_ID)

        mine = pltpu.make_async_copy(x_ref, rows(*me), local_sem)
        mine.start()
        first = [copy(0, me, sibling, src=x_ref)]
        first += [copy(1 + j, me, (*chip, c), src=x_ref) for j, chip in enumerate(chips)]
        for cp in first:
            cp.start()
        passed = [copy(4 + j, (*chip, c), sibling) for j, chip in enumerate(chips)]
        for j, chip in enumerate(chips):
            copy(1 + j, (*chip, c), me).wait_recv()
            passed[j].start()
        copy(0, sibling, me).wait_recv()
        for j, chip in enumerate(chips):
            copy(4 + j, (*chip, 1 - c), me).wait_recv()
        for cp in first + passed:
            cp.wait_send()
        mine.wait()

    return pl.pallas_call(
        body, name=name,
        out_shape=jax.ShapeDtypeStruct((N_DEV * m_per, n), x.dtype),
        in_specs=[HBM_SPEC], out_specs=HBM_SPEC,
        scratch_shapes=[pltpu.SemaphoreType.DMA((7,)), pltpu.SemaphoreType.DMA((7,)), pltpu.SemaphoreType.DMA],
    )(x)


def _all_gather_groups(name, xs):
    ng = len(xs)

    def body(*refs):
        x_refs, out_refs, token = refs[:ng], refs[ng:2 * ng], refs[2 * ng]
        send_sems, recv_sems, local_sems = refs[2 * ng + 1:]
        token[...] = jnp.zeros(token.shape, F32)
        x, y, c = lax.axis_index("x"), lax.axis_index("y"), lax.axis_index("c")
        me, sibling = (x, y, c), (x, y, 1 - c)
        chips = [(1 - x, y), (x, 1 - y), (1 - x, 1 - y)]

        def copy(g, k, block, to, src=None):
            px, py, pc = block
            dst = out_refs[g].at[4 * px + 2 * py + pc]
            return pltpu.make_async_remote_copy(
                src_ref=dst if src is None else src, dst_ref=dst,
                send_sem=send_sems.at[g, k], recv_sem=recv_sems.at[g, k], device_id=to, device_id_type=MESH_ID)

        mine = [pltpu.make_async_copy(x_refs[g], out_refs[g].at[4 * x + 2 * y + c], local_sems.at[g]) for g in range(ng)]
        for cp in mine:
            cp.start()
        first = []
        for g in range(ng):
            first.append(copy(g, 0, me, sibling, src=x_refs[g]))
            first += [copy(g, 1 + j, me, (*chip, c), src=x_refs[g]) for j, chip in enumerate(chips)]
        for cp in first:
            cp.start()
        passed = []
        for j, chip in enumerate(chips):
            for g in range(ng):
                copy(g, 1 + j, (*chip, c), me).wait_recv()
                passed.append(copy(g, 4 + j, (*chip, c), sibling))
                passed[-1].start()
        for g in range(ng):
            copy(g, 0, sibling, me).wait_recv()
            for j, chip in enumerate(chips):
                copy(g, 4 + j, (*chip, 1 - c), me).wait_recv()
        for cp in first + passed:
            cp.wait_send()
        for cp in mine:
            cp.wait()

    return pl.pallas_call(
        body, name=name,
        out_shape=[jax.ShapeDtypeStruct((N_DEV,) + x.shape, x.dtype) for x in xs] + [_sds((8, LANES))],
        in_specs=[HBM_SPEC] * ng, out_specs=[HBM_SPEC] * ng + [pl.BlockSpec(memory_space=pltpu.VMEM)],
        scratch_shapes=[pltpu.SemaphoreType.DMA((ng, 7)), pltpu.SemaphoreType.DMA((ng, 7)), pltpu.SemaphoreType.DMA((ng,))],
    )(*xs)


EFFECT = pltpu.SideEffectType.DATAFLOW_SIDE_EFFECTING
SEM_SPEC = pl.BlockSpec(memory_space=pltpu.SEMAPHORE)


def _push_copies(src_refs, land_refs, send_sems, recv_sems, chunked):
    x, y, c = lax.axis_index("x"), lax.axis_index("y"), lax.axis_index("c")
    me = 4 * x + 2 * y + c
    copies = []
    for g, (src, land) in enumerate(zip(src_refs, land_refs)):
        for k in range(1, N_DEV):
            px = 1 - x if k & 4 else x
            py = 1 - y if k & 2 else y
            pc = 1 - c if k & 1 else c
            copies.append(pltpu.make_async_remote_copy(
                src_ref=src.at[4 * px + 2 * py + pc] if chunked else src, dst_ref=land.at[me],
                send_sem=send_sems.at[g * (N_DEV - 1) + k - 1], recv_sem=recv_sems.at[g * (N_DEV - 1) + k - 1],
                device_id=(px, py, pc), device_id_type=MESH_ID))
    return copies


def _hbm(x):
    return pltpu.with_memory_space_constraint(x, pltpu.HBM)


def _push_start(name, srcs, lands, chunked):
    ng = len(srcs)

    def body(*refs):
        for cp in _push_copies(refs[:ng], refs[ng:2 * ng], refs[2 * ng], refs[2 * ng + 1], chunked):
            cp.start()
        refs[-1][...] = jnp.zeros(refs[-1].shape, F32)

    bufs = list(srcs) + list(lands)
    outs = pl.pallas_call(
        body, name=name,
        out_shape=(pltpu.SemaphoreType.DMA((ng * (N_DEV - 1),)), pltpu.SemaphoreType.DMA((ng * (N_DEV - 1),)),
                   *[pltpu.HBM(b.shape, b.dtype) for b in bufs], jax.ShapeDtypeStruct((8, LANES), F32)),
        in_specs=[HBM_SPEC] * (2 * ng),
        out_specs=(SEM_SPEC, SEM_SPEC, *[HBM_SPEC] * (2 * ng), pl.BlockSpec(memory_space=pltpu.VMEM)),
        input_output_aliases={i: 2 + i for i in range(2 * ng)},
        compiler_params=pltpu.CompilerParams(has_side_effects=EFFECT),
    )(*[_hbm(b) for b in bufs])
    return outs[0], outs[1], list(outs[2:2 + ng]), list(outs[2 + ng:2 + 2 * ng]), outs[-1]


def _push_wait(name, started, after, chunked):
    send_sems, recv_sems, srcs, lands, _ = started
    ng = len(srcs)

    def body(*refs):
        copies = _push_copies(refs[:ng], refs[ng:2 * ng], refs[2 * ng], refs[2 * ng + 1], chunked)
        for cp in copies:
            cp.wait_send()
        for cp in copies:
            cp.wait_recv()

    bufs = srcs + lands
    outs = pl.pallas_call(
        body, name=name,
        out_shape=tuple(pltpu.HBM(b.shape, b.dtype) for b in bufs),
        in_specs=[HBM_SPEC] * (2 * ng) + [SEM_SPEC, SEM_SPEC, pl.BlockSpec(memory_space=pl.ANY)],
        out_specs=tuple([HBM_SPEC] * (2 * ng)),
        input_output_aliases={i: i for i in range(2 * ng)},
        compiler_params=pltpu.CompilerParams(has_side_effects=EFFECT),
    )(*bufs, send_sems, recv_sems, after)
    return list(outs[:ng]), list(outs[ng:])


def _sum_adam_devices(sent, recv, dev, w, m, v):
    ndev, r, c_ = recv.shape
    tr = _pick(r, (256, 128, 96, 32))

    def body(dev_ref, own_ref, r_ref, w_ref, m_ref, v_ref, g_ref, d_ref, mo_ref, vo_ref):
        me = dev_ref[0]
        g = jnp.where(me == 0, own_ref[0], r_ref[0]).astype(F32)
        for j in range(1, ndev):
            g = g + jnp.where(me == j, own_ref[0], r_ref[j]).astype(F32)
        g_ref[...] = g
        d_ref[...], mo_ref[...], vo_ref[...] = _adam(w_ref[...], g, m_ref[...], v_ref[...])

    row = pl.BlockSpec((tr, c_), lambda i, dev_ref: (i, 0))
    return pl.pallas_call(
        body, name="sum_adam",
        grid_spec=pltpu.PrefetchScalarGridSpec(
            num_scalar_prefetch=1, grid=(r // tr,),
            in_specs=[pl.BlockSpec((1, tr, c_), lambda i, dev_ref: (dev_ref[0], i, 0)),
                      pl.BlockSpec((ndev, tr, c_), lambda i, dev_ref: (0, i, 0)), row, row, row],
            out_specs=[row, row, row, row]),
        out_shape=[_sds((r, c_))] * 4,
        compiler_params=_cparams(dimension_semantics=("arbitrary",)),
    )(dev, sent, recv, w, m, v)


def _cols_from_shards(w, width):
    ns, r, cs = w.shape
    tr = _pick(r, (256, 128))

    def body(w_ref, o_ref):
        parts = [w_ref[j] for j in range(ns)]
        if width > ns * cs:
            parts.append(jnp.zeros((tr, width - ns * cs), w.dtype))
        o_ref[...] = jnp.concatenate(parts, axis=1)

    return pl.pallas_call(
        body, name="cols_from_shards", grid=(r // tr,),
        in_specs=[pl.BlockSpec((ns, tr, cs), lambda i: (0, i, 0))], out_specs=pl.BlockSpec((tr, width), lambda i: (i, 0)),
        out_shape=jax.ShapeDtypeStruct((r, width), w.dtype), compiler_params=_cparams(dimension_semantics=("arbitrary",)),
    )(w)


def _shards_from_cols(g, cs):
    r, width = g.shape
    tr = _pick(r, (256, 128))

    def body(g_ref, o_ref):
        for j in range(N_DEV):
            o_ref[j] = g_ref[:, j * cs:(j + 1) * cs]

    return pl.pallas_call(
        body, name="shards_from_cols", grid=(r // tr,),
        in_specs=[pl.BlockSpec((tr, width), lambda i: (i, 0))], out_specs=pl.BlockSpec((N_DEV, tr, cs), lambda i: (0, i, 0)),
        out_shape=jax.ShapeDtypeStruct((N_DEV, r, cs), g.dtype), compiler_params=_cparams(dimension_semantics=("arbitrary",)),
    )(g)


def _adam(w, g, m, v):
    m = ADAM_B1 * m + (1.0 - ADAM_B1) * g
    v = ADAM_B2 * v + (1.0 - ADAM_B2) * (g * g)
    m_hat = m / (1.0 - ADAM_B1 ** ADAM_STEP)
    v_hat = v / (1.0 - ADAM_B2 ** ADAM_STEP)
    return -ADAM_LR * (m_hat / (jnp.sqrt(v_hat) + ADAM_EPS) + ADAM_WD * w), m, v


def _sum_devices(gathered):
    m_all, n = gathered.shape
    m_per = m_all // N_DEV

    def body(x_ref, o_ref):
        s = x_ref[0:m_per, :]
        for j in range(1, N_DEV):
            s = s + x_ref[j * m_per:(j + 1) * m_per, :]
        o_ref[...] = s

    return pl.pallas_call(body, name="sum_devices", out_shape=_sds((m_per, n)), compiler_params=_cparams())(gathered)


def _adam_small(w, g, m, v):
    def body(w_ref, g_ref, m_ref, v_ref, d_ref, mo_ref, vo_ref):
        d_ref[...], mo_ref[...], vo_ref[...] = _adam(w_ref[...], g_ref[...], m_ref[...], v_ref[...])

    return pl.pallas_call(body, name="adam_small", out_shape=[_sds(w.shape)] * 3, compiler_params=_cparams())(w, g, m, v)


N_LAYERS = 4
_MIXER = ("dn", "sb", "mla")
_MIXER_PARAMS = {
    "dn": ("dn_w_in", "dn_conv_w", "dn_a_log", "dn_dt_bias", "dn_out_norm", "dn_w_out"),
    "sb": ("sb_w_qkv", "sb_q_norm", "sb_k_norm", "sb_w_out"),
    "mla": ("mla_w_down", "mla_q_a_norm", "mla_kv_a_norm", "mla_w_uq", "mla_w_ukv", "mla_q_nope_norm", "mla_q_rope_norm",
            "mla_k_nope_norm", "mla_k_rope_norm", "mla_w_out"),
}
_BIG_AXIS = {"dn_w_in": 1, "dn_w_out": 0, "sb_w_qkv": 1, "sb_w_out": 0, "mla_w_down": 0, "mla_w_uq": 1, "mla_w_ukv": 1,
             "mla_w_out": 0, "ffn_w_gate_up": 1, "ffn_w_down": 0}


def _weight_names():
    names = []
    for i in range(N_LAYERS):
        p = "l%d_" % i
        names += [p + "mix_norm"] + [p + n for n in _MIXER_PARAMS[_MIXER[i % 3]]] + [p + "ffn_norm", p + "ffn_w_gate_up", p + "ffn_w_down"]
    return names


WEIGHTS = _weight_names()
BIG = [n for n in WEIGHTS if n[3:] in _BIG_AXIS]
SMALL = [n for n in WEIGHTS if n[3:] not in _BIG_AXIS]
CONV = [n for n in SMALL if n.endswith("conv_w")]


def _ceil_to(n, k):
    return -(-n // k) * k


def _pack(arrs, cols, row_mult):
    parts = []
    for a in arrs:
        f = a.reshape(-1)
        parts.append(jnp.pad(f, (0, _ceil_to(f.shape[0], cols) - f.shape[0])))
    flat = jnp.concatenate(parts)
    rows = _ceil_to(flat.shape[0] // cols, row_mult)
    return jnp.pad(flat, (0, rows * cols - flat.shape[0])).reshape(rows, cols)


def _unpack(buf, shapes):
    cols = buf.shape[-1]
    out, r0 = [], 0
    for s in shapes:
        n = math.prod(s)
        nr = _ceil_to(n, cols) // cols
        out.append(buf[r0:r0 + nr].reshape(-1)[:n].reshape(s))
        r0 += nr
    return out


def _layer_groups(i):
    by = {"gu": (704, []), "down": (1024, []), "out": (1024, []), "dn_in": (514, []), "sb_qkv": (384, []), "mla": (512, [])}
    key = {"ffn_w_gate_up": "gu", "ffn_w_down": "down", "dn_w_in": "dn_in", "sb_w_qkv": "sb_qkv", "mla_w_down": "mla",
           "mla_w_uq": "mla", "mla_w_ukv": "mla"}
    for n in BIG:
        if n.startswith("l%d_" % i):
            by[key.get(n[3:], "out")][1].append(n)
    return [g for g in by.values() if g[1]]


LAYER_GROUPS = [_layer_groups(i) for i in range(N_LAYERS)]
N_FFN_GROUPS = 2


def _stack_group(grp, get):
    width, names = grp
    if len(names) == 1 and get(names[0]).shape[1] == width:
        return get(names[0])
    return jnp.concatenate([jnp.pad(get(n), ((0, 0), (0, width - get(n).shape[1]))) for n in names], axis=0)


def _unstack_group(grp, buf, shape_of):
    if len(grp[1]) == 1 and tuple(shape_of(grp[1][0])) == tuple(buf.shape[-2:]):
        return [buf]
    out, r0 = [], 0
    for n in grp[1]:
        rs, cs = shape_of(n)
        out.append(buf[..., r0:r0 + rs, :cs])
        r0 += rs
    return out


def kernel(x, l0_mix_norm, l0_dn_w_in, l0_dn_conv_w, l0_dn_a_log, l0_dn_dt_bias, l0_dn_out_norm, l0_dn_w_out, l0_ffn_norm, l0_ffn_w_gate_up, l0_ffn_w_down, l1_mix_norm, l1_sb_w_qkv, l1_sb_q_norm, l1_sb_k_norm, l1_sb_w_out, l1_ffn_norm, l1_ffn_w_gate_up, l1_ffn_w_down, l2_mix_norm, l2_mla_w_down, l2_mla_q_a_norm, l2_mla_kv_a_norm, l2_mla_w_uq, l2_mla_w_ukv, l2_mla_q_nope_norm, l2_mla_q_rope_norm, l2_mla_k_nope_norm, l2_mla_k_rope_norm, l2_mla_w_out, l2_ffn_norm, l2_ffn_w_gate_up, l2_ffn_w_down, l3_mix_norm, l3_dn_w_in, l3_dn_conv_w, l3_dn_a_log, l3_dn_dt_bias, l3_dn_out_norm, l3_dn_w_out, l3_ffn_norm, l3_ffn_w_gate_up, l3_ffn_w_down, loss_target, m_l0_mix_norm, m_l0_dn_w_in, m_l0_dn_conv_w, m_l0_dn_a_log, m_l0_dn_dt_bias, m_l0_dn_out_norm, m_l0_dn_w_out, m_l0_ffn_norm, m_l0_ffn_w_gate_up, m_l0_ffn_w_down, m_l1_mix_norm, m_l1_sb_w_qkv, m_l1_sb_q_norm, m_l1_sb_k_norm, m_l1_sb_w_out, m_l1_ffn_norm, m_l1_ffn_w_gate_up, m_l1_ffn_w_down, m_l2_mix_norm, m_l2_mla_w_down, m_l2_mla_q_a_norm, m_l2_mla_kv_a_norm, m_l2_mla_w_uq, m_l2_mla_w_ukv, m_l2_mla_q_nope_norm, m_l2_mla_q_rope_norm, m_l2_mla_k_nope_norm, m_l2_mla_k_rope_norm, m_l2_mla_w_out, m_l2_ffn_norm, m_l2_ffn_w_gate_up, m_l2_ffn_w_down, m_l3_mix_norm, m_l3_dn_w_in, m_l3_dn_conv_w, m_l3_dn_a_log, m_l3_dn_dt_bias, m_l3_dn_out_norm, m_l3_dn_w_out, m_l3_ffn_norm, m_l3_ffn_w_gate_up, m_l3_ffn_w_down, v_l0_mix_norm, v_l0_dn_w_in, v_l0_dn_conv_w, v_l0_dn_a_log, v_l0_dn_dt_bias, v_l0_dn_out_norm, v_l0_dn_w_out, v_l0_ffn_norm, v_l0_ffn_w_gate_up, v_l0_ffn_w_down, v_l1_mix_norm, v_l1_sb_w_qkv, v_l1_sb_q_norm, v_l1_sb_k_norm, v_l1_sb_w_out, v_l1_ffn_norm, v_l1_ffn_w_gate_up, v_l1_ffn_w_down, v_l2_mix_norm, v_l2_mla_w_down, v_l2_mla_q_a_norm, v_l2_mla_kv_a_norm, v_l2_mla_w_uq, v_l2_mla_w_ukv, v_l2_mla_q_nope_norm, v_l2_mla_q_rope_norm, v_l2_mla_k_nope_norm, v_l2_mla_k_rope_norm, v_l2_mla_w_out, v_l2_ffn_norm, v_l2_ffn_w_gate_up, v_l2_ffn_w_down, v_l3_mix_norm, v_l3_dn_w_in, v_l3_dn_conv_w, v_l3_dn_a_log, v_l3_dn_dt_bias, v_l3_dn_out_norm, v_l3_dn_w_out, v_l3_ffn_norm, v_l3_ffn_w_gate_up, v_l3_ffn_w_down):
    a = dict(locals())
    return _train_step(a)


def _train_step(a):
    mx, my, mc = lax.axis_index("x"), lax.axis_index("y"), lax.axis_index("c")
    dev = 4 * mx + 2 * my + mc
    dev_arr = jnp.reshape(dev, (1,)).astype(jnp.int32)
    t, d = a["x"].shape[1], a["x"].shape[2]
    xs = a["x"].reshape(t, d)
    target = a["loss_target"].reshape(t, d)

    full = {}

    def unpack(groups, bufs):
        for grp, buf in zip(groups, bufs):
            for n, shards in zip(grp[1], _unstack_group(grp, buf, lambda n: a[n].shape)):
                kind = n[3:]
                if kind == "ffn_w_gate_up":
                    full[n] = shards
                elif _BIG_AXIS[kind] == 0:
                    full[n] = shards.reshape(N_DEV * shards.shape[1], shards.shape[2])
                else:
                    width = DN_PROJ if kind == "dn_w_in" else N_DEV * shards.shape[2]
                    full[n] = _cols_from_shards(shards, width)

    def local_shards(groups):
        return [_stack_group(grp, lambda n: a[n].astype(BF16)) for grp in groups]

    pushed_groups = [LAYER_GROUPS[0][:N_FFN_GROUPS]] + LAYER_GROUPS[1:]
    conv_pack = _pack([a[n] for n in CONV], LANES, 8)
    first_groups = LAYER_GROUPS[0][N_FFN_GROUPS:]
    *first_bufs, conv_all, gathered = _all_gather_groups("gather_weights", local_shards(first_groups) + [conv_pack])
    unpack(first_groups, first_bufs)
    for n, parts in zip(CONV, zip(*[_unpack(conv_all[j], [a[n].shape for n in CONV]) for j in range(N_DEV)])):
        full[n] = jnp.concatenate(parts, axis=1)
    after_first = gathered[0, 0].astype(BF16)
    gathers, started = {}, jnp.zeros((), F32)
    for i in range(N_LAYERS):
        srcs = local_shards(pushed_groups[i])
        srcs[-1] = srcs[-1] + after_first
        lands = [lax.dynamic_update_index_in_dim(lax.empty((N_DEV,) + s.shape, s.dtype), s, dev, 0) for s in srcs]
        gathers[i] = _push_start("gather_start_l%d" % i, srcs, lands, False)
        started = started + gathers[i][-1][0, 0]

    def vec(n):
        return a[n].reshape(1, -1)

    cos, sin = _rope_tables(t)

    def mixer_args(i):
        p = "l%d_" % i
        kind = _MIXER[i % 3]
        if kind == "dn":
            args = (full[p + "dn_w_in"], full[p + "dn_conv_w"], _pad_lanes(a[p + "dn_a_log"]), _pad_lanes(a[p + "dn_dt_bias"]),
                    vec(p + "dn_out_norm"), full[p + "dn_w_out"])
        elif kind == "sb":
            args = (full[p + "sb_w_qkv"], vec(p + "sb_q_norm"), vec(p + "sb_k_norm"), full[p + "sb_w_out"])
        else:
            w_down, w_uq, w_ukv = _mla_layout(full[p + "mla_w_down"], full[p + "mla_w_uq"], full[p + "mla_w_ukv"])
            args = (w_down, vec(p + "mla_q_a_norm"), vec(p + "mla_kv_a_norm"), w_uq, w_ukv, vec(p + "mla_q_nope_norm"),
                    _pad_lanes(a[p + "mla_q_rope_norm"]), vec(p + "mla_k_nope_norm"), _pad_lanes(a[p + "mla_k_rope_norm"]),
                    full[p + "mla_w_out"], cos, sin)
        return kind, args

    fwd = {"dn": _dn_fwd, "sb": _sb_fwd, "mla": _mla_fwd}
    bwd = {"dn": _dn_bwd, "sb": _sb_bwd, "mla": _mla_bwd}
    saved, layer_args = [], []
    for i in range(N_LAYERS):
        p = "l%d_" % i
        if i > 0:
            unpack(pushed_groups[i], _push_wait("gather_wait_l%d" % i, gathers[i], xs, False)[1])
        kind, args = mixer_args(i)
        layer_args.append((kind, args))
        gain = vec(p + "mix_norm") + started if i == 0 else vec(p + "mix_norm")
        h = _rmsnorm_fwd("mix_norm", xs, gain)
        x_mid, sv_mix = fwd[kind](xs, h, *args)
        if i == 0:
            unpack(pushed_groups[0], _push_wait("gather_wait_l0", gathers[0], x_mid, False)[1])
        x_out, sv_ffn = _ffn_fwd(x_mid, vec(p + "ffn_norm"), full[p + "ffn_w_gate_up"], full[p + "ffn_w_down"])
        saved.append((xs, sv_mix, sv_ffn))
        xs = x_out
    dy, loss_part = _loss_head(xs, target)

    grads, big_out = {}, {}

    def grad_shards(n):
        g, (rs, cs) = grads[n], a[n].shape
        if g.ndim == 3:
            return g
        if _BIG_AXIS[n[3:]] == 0:
            return g.reshape(N_DEV, rs, cs)
        return _shards_from_cols(g, cs)

    def push_grads(tag, groups):
        sends = []
        for grp in groups:
            if len(grp[1]) == 1 and a[grp[1][0]].shape[1] == grp[0]:
                sends.append(grad_shards(grp[1][0]))
                continue
            parts = [jnp.pad(grad_shards(n), ((0, 0), (0, 0), (0, grp[0] - a[n].shape[1]))) for n in grp[1]]
            sends.append(jnp.concatenate(parts, axis=1))
        lands = [lax.empty(s.shape, s.dtype) for s in sends]
        return tag, groups, _push_start("grads_start_" + tag, sends, lands, True)

    def finish_grads(push, after):
        tag, groups, pushed = push
        sents, recvs = _push_wait("grads_wait_" + tag, pushed, after, True)
        for grp, sent, recv in zip(groups, sents, recvs):
            packs = [_stack_group(grp, lambda n, pre=pre: a[pre + n]) for pre in ("", "m_", "v_")]
            outs = [_unstack_group(grp, o, lambda n: a[n].shape) for o in _sum_adam_devices(sent, recv, dev_arr, *packs)]
            for j, n in enumerate(grp[1]):
                big_out[n] = [o[j] for o in outs]

    mixer_push = None
    for i in reversed(range(N_LAYERS)):
        p = "l%d_" % i
        kind, args = layer_args[i]
        x_in, sv_mix, sv_ffn = saved[i]
        gain = vec(p + "ffn_norm") if mixer_push is None else vec(p + "ffn_norm") + mixer_push[2][-1][0, 0]
        dx_mid, grads[p + "ffn_norm"], grads[p + "ffn_w_gate_up"], grads[p + "ffn_w_down"] = _ffn_bwd(
            sv_ffn, dy, gain, full[p + "ffn_w_gate_up"], full[p + "ffn_w_down"])
        ffn_push = push_grads("l%d_ffn" % i, LAYER_GROUPS[i][:N_FFN_GROUPS])
        if mixer_push is not None:
            finish_grads(mixer_push, dx_mid)
        res = bwd[kind](sv_mix, dx_mid, *args)
        dh = res[0]
        if kind == "mla":
            res = list(res)
            res[1], res[4], res[5] = _mla_unlayout(res[1], res[4], res[5])
        for n, g in zip(_MIXER_PARAMS[kind], res[1:]):
            grads[p + n] = g
        dy, grads[p + "mix_norm"] = _rmsnorm_bwd("mix_norm_bwd", x_in, vec(p + "mix_norm") + ffn_push[2][-1][0, 0], dh, dx_mid)
        mixer_push = push_grads("l%d_mix" % i, LAYER_GROUPS[i][N_FFN_GROUPS:])
        finish_grads(ffn_push, dy)
    grad_x = dy.reshape(a["x"].shape)

    small_full_shapes = [full[n].shape if n in CONV else a[n].shape for n in SMALL]
    small_grads = []
    for n, s in zip(SMALL, small_full_shapes):
        g = grads[n].reshape(-1)
        small_grads.append(g[:math.prod(s)])
    small_pack = _pack(small_grads + [loss_part.reshape(-1)], LANES, 8)
    small_sum = _sum_devices(_all_gather("gather_small_grads", small_pack))
    small_red = _unpack(small_sum, small_full_shapes + [(LANES,)])
    loss = small_red[-1][0]
    g_small = {}
    for n, g in zip(SMALL, small_red[:-1]):
        if n in CONV:
            cs = a[n].shape[1]
            g = lax.dynamic_slice_in_dim(g, dev * cs, cs, axis=1)
        g_small[n] = g
    small_shapes = [a[n].shape for n in SMALL]
    packs = [_pack([src[n] for n in SMALL], LANES, 8) for src in
             ({n: a[n] for n in SMALL}, g_small, {n: a["m_" + n] for n in SMALL}, {n: a["v_" + n] for n in SMALL})]
    d_small, m_small, v_small = (_unpack(o, small_shapes) for o in _adam_small(*packs))
    finish_grads(mixer_push, small_sum)

    small_out = dict(zip(SMALL, zip([g_small[n] for n in SMALL], d_small, m_small, v_small)))

    def out(k):
        return [small_out[n][k] if n in small_out else big_out[n][k] for n in WEIGHTS]

    return (loss, grad_x, *out(0), *out(1), *out(2), *out(3))
```

```python
import math

import jax
import jax.numpy as jnp
from jax import lax
from jax.experimental import pallas as pl
from jax.experimental.pallas import tpu as pltpu

F32 = jnp.float32
BF16 = jnp.bfloat16
GRAD_DTYPE = BF16
F32X3 = lax.Precision.HIGH

LANES = 128
N_DEV = 8
N_HEADS = 8
HEAD = 128
NORM_EPS = 1e-6
DN_CHUNK = 64
DN_STEP = 2 * DN_CHUNK
ATT_BLOCK = 512
ATT_Q = 512
MLA_ROPE = 64
MLA_QK = 192
ROPE_THETA = 10000.0
VMEM_LIMIT = 56 * 1024 * 1024
MM_TILE_BYTES = 32 * 1024 * 1024

ADAM_LR = 0.001
ADAM_B1 = 0.9
ADAM_B2 = 0.999
ADAM_EPS = 1e-08
ADAM_WD = 0.01
ADAM_STEP = 10


def _cparams(**kw):
    return pltpu.CompilerParams(vmem_limit_bytes=VMEM_LIMIT, **kw)


def _pick(n, cands):
    for c in cands:
        if c <= n and n % c == 0:
            return c
    return n


def _mm(name, a, b, *, ta=False, tb=False, out_dtype=F32, add=None, tm=None, tn=None, tk=None):
    if ta:
        K, M = a.shape
    else:
        M, K = a.shape
    N = b.shape[0] if tb else b.shape[1]
    tm = tm or _pick(M, (1024, 512, 256, 128))
    tn = tn or _pick(N, (1024, 512, 384, 256, 128))
    if tk is None:
        fits = [c for c in (4096, 2048, 1408, 1024, 512, 384, 256, 128)
                if c <= K and K % c == 0 and 2 * c * (tm * a.dtype.itemsize + tn * b.dtype.itemsize) <= MM_TILE_BYTES]
        tk = fits[0] if fits else K
    return _mm_raw(
        name, a, b, ta=ta, tb=tb, out_dtype=out_dtype, add=add, grid=(M // tm, N // tn, K // tk), out_shape=(M, N),
        a_block=(tk, tm) if ta else (tm, tk), a_map=(lambda i, j, k: (k, i)) if ta else (lambda i, j, k: (i, k)),
        b_block=(tn, tk) if tb else (tk, tn), b_map=(lambda i, j, k: (j, k)) if tb else (lambda i, j, k: (k, j)),
        o_block=(tm, tn), o_map=lambda i, j, k: (i, j))


def _mm_raw(name, a, b, *, ta, tb, out_dtype, add, grid, out_shape, a_block, a_map, b_block, b_map, o_block, o_map):
    nk = grid[2]
    tm, tn = o_block
    dn = (((0 if ta else 1,), (1 if tb else 0,)), ((), ()))
    has_add = add is not None

    def kern(*refs):
        if has_add:
            a_ref, b_ref, add_ref, o_ref, acc_ref = refs
        else:
            a_ref, b_ref, o_ref, acc_ref = refs
        k = pl.program_id(2)
        part = lax.dot_general(a_ref[...].astype(BF16), b_ref[...].astype(BF16), dn, preferred_element_type=F32)

        @pl.when(k == 0)
        def _():
            acc_ref[...] = part

        @pl.when(k > 0)
        def _():
            acc_ref[...] += part

        @pl.when(k == nk - 1)
        def _():
            r = acc_ref[...]
            if has_add:
                r = r + add_ref[...]
            o_ref[...] = r.astype(out_dtype)

    in_specs = [pl.BlockSpec(a_block, a_map), pl.BlockSpec(b_block, b_map)]
    args = [a, b]
    if has_add:
        in_specs.append(pl.BlockSpec(o_block, o_map))
        args.append(add)
    return pl.pallas_call(
        kern, name=name,
        grid=grid,
        in_specs=in_specs,
        out_specs=pl.BlockSpec(o_block, o_map),
        out_shape=jax.ShapeDtypeStruct(out_shape, out_dtype),
        scratch_shapes=[pltpu.VMEM((tm, tn), F32)],
        compiler_params=_cparams(dimension_semantics=("parallel", "parallel", "arbitrary")),
    )(*args)


def _rows(name, body, ins, outs, *, tt, consts=(), accs=()):
    in_specs, args = [], []
    first = ins[0][0] if isinstance(ins[0], tuple) else ins[0]
    t = first.shape[-2]
    tt = min(tt, t)
    for x in ins:
        if isinstance(x, tuple):
            arr, bs, im = x
            in_specs.append(pl.BlockSpec(bs, im))
            args.append(arr)
        else:
            in_specs.append(_row_spec(x.shape, tt))
            args.append(x)
    for c in consts:
        in_specs.append(pl.BlockSpec(c.shape, lambda i, _n=c.ndim: (0,) * _n))
        args.append(c)
    out_specs = [_row_spec(o.shape, tt) for o in outs]
    out_specs += [pl.BlockSpec(a.shape, lambda i, _n=len(a.shape): (0,) * _n) for a in accs]
    res = pl.pallas_call(
        body, name=name, grid=(t // tt,),
        in_specs=in_specs, out_specs=out_specs, out_shape=list(outs) + list(accs),
        compiler_params=_cparams(dimension_semantics=("arbitrary",)),
    )(*args)
    return res


def _row_spec(shape, tt):
    if len(shape) == 2:
        return pl.BlockSpec((tt, shape[1]), lambda i: (i, 0))
    return pl.BlockSpec((shape[0], tt, shape[2]), lambda i: (0, i, 0))


def _sds(shape, dtype=F32):
    return jax.ShapeDtypeStruct(tuple(shape), dtype)


def _acc(ref, val):
    i = pl.program_id(0)

    @pl.when(i == 0)
    def _():
        ref[...] = val

    @pl.when(i > 0)
    def _():
        ref[...] += val


def _rms(x, g):
    return x * lax.rsqrt(jnp.mean(x * x, axis=-1, keepdims=True) + NORM_EPS) * g


def _silu(x):
    return x / (1.0 + jnp.exp(-x))


def _softplus(x):
    return jnp.maximum(x, 0.0) + jnp.log(1.0 + jnp.exp(-jnp.abs(x)))


def _sigmoid(x):
    return 1.0 / (1.0 + jnp.exp(-x))


def _rmsnorm_fwd(name, x, g, tt=512):
    def body(x_ref, g_ref, h_ref):
        h_ref[...] = _rms(x_ref[...], g_ref[...]).astype(BF16)

    return _rows(name, body, [x], [_sds(x.shape, BF16)], tt=tt, consts=[g])[0]


def _rmsnorm_bwd(name, x, g, dh, dres, tt=512):
    def body(x_ref, dh_ref, dres_ref, g_ref, dx_ref, dg_ref):
        _, vjp = jax.vjp(_rms, x_ref[...], g_ref[...])
        dx, dg = vjp(dh_ref[...])
        dx_ref[...] = dx + dres_ref[...]
        _acc(dg_ref, dg)

    return _rows(name, body, [x, dh, dres], [_sds(x.shape)], tt=tt, consts=[g], accs=[_sds(g.shape)])


def _ffn_fwd(x, norm_g, w3, w_down):
    t, d = x.shape
    ns, _, cs = w3.shape
    half = ns // 2
    w2 = w3.reshape(ns * d, cs)
    h = _rmsnorm_fwd("ffn_norm", x, norm_g)
    tm = _pick(t, (1024, 512, 256, 128))
    nm = t // tm

    def gate_up(h_ref, wg_ref, wu_ref, g_ref, u_ref, a_ref):
        hv = h_ref[...]
        g = _dot(hv, wg_ref[...])
        u = _dot(hv, wu_ref[...])
        g_ref[...] = g
        u_ref[...] = u
        a_ref[...] = (_silu(g) * u).astype(BF16)

    hid = pl.BlockSpec((tm, cs), lambda j, i: (j * nm + i, 0))
    g, u, act = pl.pallas_call(
        gate_up, name="ffn_gate_up", grid=(half, nm),
        in_specs=[pl.BlockSpec((tm, d), lambda j, i: (i, 0)), pl.BlockSpec((d, cs), lambda j, i: (j, 0)),
                  pl.BlockSpec((d, cs), lambda j, i: (j + half, 0))],
        out_specs=[hid, hid, hid], out_shape=[_sds((half * t, cs)), _sds((half * t, cs)), _sds((half * t, cs), BF16)],
        compiler_params=_cparams(dimension_semantics=("parallel", "arbitrary")),
    )(h, w2, w2)
    def down(a_ref, w_ref, x_ref, y_ref):
        y = x_ref[...]
        for j in range(half):
            y = y + _dot(a_ref[j], w_ref[j])
        y_ref[...] = y

    y = pl.pallas_call(
        down, name="ffn_down", grid=(nm,),
        in_specs=[pl.BlockSpec((half, tm, cs), lambda i: (0, i, 0)), pl.BlockSpec((half, cs, d), lambda i: (0, 0, 0)),
                  pl.BlockSpec((tm, d), lambda i: (i, 0))],
        out_specs=pl.BlockSpec((tm, d), lambda i: (i, 0)), out_shape=_sds((t, d)),
        compiler_params=_cparams(dimension_semantics=("parallel",)),
    )(act.reshape(half, t, cs), w_down.reshape(half, cs, d), x)
    return y, (x, h, g, u, act)


def _ffn_bwd(saved, dy, norm_g, w3, w_down):
    x, h, g, u, act = saved
    t, d = x.shape
    ns, _, cs = w3.shape
    half = ns // 2
    w2 = w3.reshape(ns * d, cs)
    tm = _pick(t, (1024, 512, 256, 128))
    nm = t // tm
    tk = _pick(t, (4096, 2048, 1024, 512, 256, 128))
    nk = t // tk
    d_wdown = _mm_raw("ffn_down_wgrad", act, dy, ta=True, tb=False, out_dtype=GRAD_DTYPE, add=None, grid=(half, 1, nk),
                      out_shape=(half * cs, d), a_block=(tk, cs), a_map=lambda i, j, k: (i * nk + k, 0),
                      b_block=(tk, d), b_map=lambda i, j, k: (k, 0), o_block=(cs, d), o_map=lambda i, j, k: (i, 0))
    def down_dgrad(dy_ref, wd_ref, g_ref, u_ref, dg_ref, du_ref):
        da = _dot_nt(dy_ref[...].astype(BF16), wd_ref[...])
        gv, uv = g_ref[...], u_ref[...]
        s = _sigmoid(gv)
        dg_ref[...] = (da * uv * s * (1.0 + gv * (1.0 - s))).astype(BF16)
        du_ref[...] = (da * gv * s).astype(BF16)

    hid = pl.BlockSpec((tm, cs), lambda j, i: (j * nm + i, 0))
    dg, du = pl.pallas_call(
        down_dgrad, name="ffn_down_dgrad", grid=(half, nm),
        in_specs=[pl.BlockSpec((tm, d), lambda j, i: (i, 0)), pl.BlockSpec((cs, d), lambda j, i: (j, 0)), hid, hid],
        out_specs=[hid, hid], out_shape=[_sds((half * t, cs), BF16), _sds((half * t, cs), BF16)],
        compiler_params=_cparams(dimension_semantics=("parallel", "arbitrary")),
    )(dy, w_down, g, u)

    def wgrad(name, dd):
        return _mm_raw(name, h, dd, ta=True, tb=False, out_dtype=GRAD_DTYPE, add=None, grid=(1, half, nk), out_shape=(half * d, cs),
                       a_block=(tk, d), a_map=lambda i, j, k: (k, 0), b_block=(tk, cs), b_map=lambda i, j, k: (j * nk + k, 0),
                       o_block=(d, cs), o_map=lambda i, j, k: (j, 0))

    def gate_up_dgrad(dg_ref, du_ref, w_ref, dh_ref):
        dh = _dot_nt(dg_ref[0], w_ref[0]) + _dot_nt(du_ref[0], w_ref[half])
        for j in range(1, half):
            dh = dh + _dot_nt(dg_ref[j], w_ref[j]) + _dot_nt(du_ref[j], w_ref[half + j])
        dh_ref[...] = dh

    th = _pick(t, (512, 256, 128))
    hid3 = pl.BlockSpec((half, th, cs), lambda i: (0, i, 0))
    dh = pl.pallas_call(
        gate_up_dgrad, name="ffn_gate_up_dgrad", grid=(t // th,),
        in_specs=[hid3, hid3, pl.BlockSpec((ns, d, cs), lambda i: (0, 0, 0))],
        out_specs=pl.BlockSpec((th, d), lambda i: (i, 0)), out_shape=_sds((t, d)),
        compiler_params=_cparams(dimension_semantics=("parallel",)),
    )(dg.reshape(half, t, cs), du.reshape(half, t, cs), w3)
    d_w3 = jnp.concatenate([wgrad("ffn_gate_wgrad", dg), wgrad("ffn_up_wgrad", du)], axis=0).reshape(ns, d, cs)
    dx, dgain = _rmsnorm_bwd("ffn_norm_bwd", x, norm_g, dh, dy)
    return dx, dgain, d_w3, d_wdown


def _dot_nt(a, b):
    return lax.dot_general(a, b, (((1,), (1,)), ((), ())), preferred_element_type=F32)


def _dot_tn(a, b):
    return lax.dot_general(a, b, (((0,), (0,)), ((), ())), preferred_element_type=F32)


def _dot(a, b):
    return jnp.dot(a, b, preferred_element_type=F32)


CUM_BLOCK = 128


def _tri2(lower):
    r = lax.broadcasted_iota(jnp.int32, (CUM_BLOCK, CUM_BLOCK), 0)
    c = lax.broadcasted_iota(jnp.int32, (CUM_BLOCK, CUM_BLOCK), 1)
    tri = ((r > c) if lower else (r < c)).astype(BF16)
    return jnp.concatenate([tri, tri], axis=0)


def _run_sums(x, tri2, run, reverse):
    nb = x.shape[1] // CUM_BLOCK
    outs = [None] * nb
    for j in (reversed(range(nb)) if reverse else range(nb)):
        xj = x[:, j * CUM_BLOCK:(j + 1) * CUM_BLOCK]
        hi = xj.astype(BF16)
        lo = (xj - hi.astype(F32)).astype(BF16)
        outs[j] = _dot(jnp.concatenate([hi, lo], axis=1), tri2) + run
        run = run + jnp.sum(xj, axis=1, keepdims=True)
    return jnp.concatenate(outs, axis=1), run


def _log_sigmoid(z):
    return jnp.minimum(z, 0.0) - jnp.log(1.0 + jnp.exp(-jnp.abs(z)))


def _heads_in(ref, h, width=HEAD):
    return ref[:, h * width:(h + 1) * width]


def _sb_qk(q, k, gq, gk):
    return _rms(q, gq) * (HEAD ** -0.5), _rms(k, gk)


def _sb_prep_fwd(qkv, gq, gk):
    t = qkv.shape[0]

    def body(x_ref, gq_ref, gk_ref, q_ref, k_ref, v_ref):
        for h in range(N_HEADS):
            q, k = _sb_qk(_heads_in(x_ref, h), _heads_in(x_ref, N_HEADS + h), gq_ref[...], gk_ref[...])
            q_ref[h] = q.astype(BF16)
            k_ref[h] = k.astype(BF16)
            v_ref[h] = _heads_in(x_ref, 2 * N_HEADS + h).astype(BF16)

    hm = _sds((N_HEADS, t, HEAD), BF16)
    return _rows("sb_prep", body, [qkv], [hm, hm, hm], tt=256, consts=[gq, gk])


def _sb_prep_bwd(qkv, gq, gk, dq, dk, dv):
    def body(x_ref, dq_ref, dk_ref, dv_ref, gq_ref, gk_ref, dx_ref, dgq_ref, dgk_ref):
        dgq = jnp.zeros(gq_ref.shape, F32)
        dgk = jnp.zeros(gk_ref.shape, F32)
        for h in range(N_HEADS):
            _, vjp = jax.vjp(_sb_qk, _heads_in(x_ref, h), _heads_in(x_ref, N_HEADS + h), gq_ref[...], gk_ref[...])
            a, b, c, d = vjp((dq_ref[h], dk_ref[h]))
            dx_ref[:, h * HEAD:(h + 1) * HEAD] = a.astype(BF16)
            dx_ref[:, (N_HEADS + h) * HEAD:(N_HEADS + h + 1) * HEAD] = b.astype(BF16)
            dx_ref[:, (2 * N_HEADS + h) * HEAD:(2 * N_HEADS + h + 1) * HEAD] = dv_ref[h].astype(BF16)
            dgq, dgk = dgq + c, dgk + d
        _acc(dgq_ref, dgq)
        _acc(dgk_ref, dgk)

    return _rows("sb_prep_bwd", body, [qkv, dq, dk, dv], [_sds(qkv.shape, BF16)], tt=256, consts=[gq, gk],
                 accs=[_sds(gq.shape), _sds(gk.shape)])


def _q_block(t):
    return min(ATT_Q, t)


def _key_order(bq, qb):
    rows = lax.broadcasted_iota(jnp.int32, (bq, ATT_BLOCK), 0)
    cols = lax.broadcasted_iota(jnp.int32, (bq, ATT_BLOCK), 1)
    return rows - cols + qb * bq


def _sb_attn_fwd(q, k, v):
    nh, t, _ = q.shape
    bq = _q_block(t)
    per = bq // ATT_BLOCK

    def kern(q_ref, k_ref, v_ref, o_ref):
        qb = pl.program_id(1)
        qv = q_ref[0]
        after = _tri2(True)
        order = _key_order(bq, qb)
        nkb = (qb + 1) * per

        def body(i, carry, diagonal):
            o_acc, run = carry
            kb = nkb - 1 - i
            off = pl.multiple_of(kb * ATT_BLOCK, ATT_BLOCK)
            kv = k_ref[0, pl.ds(off, ATT_BLOCK), :]
            vv = v_ref[0, pl.ds(off, ATT_BLOCK), :]
            z = _dot_nt(qv, kv)
            lsz = _log_sigmoid(z)
            lsn = lsz - z
            if diagonal:
                past = order > kb * ATT_BLOCK
                lsn = jnp.where(past, lsn, 0.0)
            la, run = _run_sums(lsn, after, run, True)
            a = jnp.exp(lsz + la)
            if diagonal:
                a = jnp.where(past, a, 0.0)
            o_acc = o_acc + _dot(a.astype(BF16), vv)
            return o_acc, run

        carry = lax.fori_loop(0, per, lambda i, c: body(i, c, True), (jnp.zeros((bq, HEAD), F32), jnp.zeros((bq, 1), F32)))
        o, _ = lax.fori_loop(per, nkb, lambda i, c: body(i, c, False), carry)
        o_ref[...] = o

    return pl.pallas_call(
        kern, name="sb_attn_fwd", grid=(nh, t // bq),
        in_specs=[pl.BlockSpec((1, bq, HEAD), lambda h, i: (h, i, 0)),
                  pl.BlockSpec((1, t, HEAD), lambda h, i: (h, 0, 0)),
                  pl.BlockSpec((1, t, HEAD), lambda h, i: (h, 0, 0))],
        out_specs=pl.BlockSpec((bq, HEAD), lambda h, i: (i, h)),
        out_shape=_sds((t, nh * HEAD)),
        compiler_params=_cparams(dimension_semantics=("parallel", "arbitrary")),
    )(q, k, v)


def _sb_attn_bwd(q, k, v, do):
    nh, t, _ = q.shape
    bq = _q_block(t)
    per = bq // ATT_BLOCK

    def kern(q_ref, k_ref, v_ref, do_ref, dq_ref, dk_ref, dv_ref, g_s, ls_s):
        qb = pl.program_id(1)

        @pl.when(qb == 0)
        def _():
            dk_ref[...] = jnp.zeros(dk_ref.shape, F32)
            dv_ref[...] = jnp.zeros(dv_ref.shape, F32)

        qv = q_ref[0]
        dob = do_ref[...].astype(BF16)
        after, before = _tri2(True), _tri2(False)
        order = _key_order(bq, qb)
        nkb = (qb + 1) * per

        def sweep_left(i, run, diagonal):
            kb = nkb - 1 - i
            off = pl.multiple_of(kb * ATT_BLOCK, ATT_BLOCK)
            kv = k_ref[0, pl.ds(off, ATT_BLOCK), :]
            vv = v_ref[0, pl.ds(off, ATT_BLOCK), :]
            z = _dot_nt(qv, kv)
            lsz = _log_sigmoid(z)
            lsn = lsz - z
            if diagonal:
                past = order > kb * ATT_BLOCK
                lsn = jnp.where(past, lsn, 0.0)
            la, run = _run_sums(lsn, after, run, True)
            a = jnp.exp(lsz + la)
            if diagonal:
                a = jnp.where(past, a, 0.0)
            g_s[kb] = _dot_nt(dob, vv) * a
            ls_s[kb] = lsz
            dv_ref[0, pl.ds(off, ATT_BLOCK), :] += _dot_tn(a.astype(BF16), dob)
            return run

        zero = jnp.zeros((bq, 1), F32)
        run = lax.fori_loop(0, per, lambda i, c: sweep_left(i, c, True), zero)
        lax.fori_loop(per, nkb, lambda i, c: sweep_left(i, c, False), run)

        def sweep_right(kb, carry, diagonal):
            dq_acc, run_g = carry
            off = pl.multiple_of(kb * ATT_BLOCK, ATT_BLOCK)
            kv = k_ref[0, pl.ds(off, ATT_BLOCK), :]
            g = g_s[kb]
            sg = jnp.exp(ls_s[kb])
            dls, run_g = _run_sums(g, before, run_g, False)
            dz = g * (1.0 - sg) - dls * sg
            if diagonal:
                dz = jnp.where(order > kb * ATT_BLOCK, dz, 0.0)
            dzb = dz.astype(BF16)
            dk_ref[0, pl.ds(off, ATT_BLOCK), :] += _dot_tn(dzb, qv)
            return dq_acc + _dot(dzb, kv), run_g

        carry = lax.fori_loop(0, nkb - per, lambda i, c: sweep_right(i, c, False), (jnp.zeros((bq, HEAD), F32), zero))
        dq, _ = lax.fori_loop(nkb - per, nkb, lambda i, c: sweep_right(i, c, True), carry)
        dq_ref[0] = dq

    hm = _sds((nh, t, HEAD))
    full = pl.BlockSpec((1, t, HEAD), lambda h, i: (h, 0, 0))
    tok = pl.BlockSpec((bq, HEAD), lambda h, i: (i, h))
    nkb_max = t // ATT_BLOCK
    return pl.pallas_call(
        kern, name="sb_attn_bwd", grid=(nh, t // bq),
        in_specs=[pl.BlockSpec((1, bq, HEAD), lambda h, i: (h, i, 0)), full, full, tok],
        out_specs=[pl.BlockSpec((1, bq, HEAD), lambda h, i: (h, i, 0)), full, full],
        out_shape=[hm, hm, hm],
        scratch_shapes=[pltpu.VMEM((nkb_max, bq, ATT_BLOCK), F32), pltpu.VMEM((nkb_max, bq, ATT_BLOCK), F32)],
        compiler_params=_cparams(dimension_semantics=("parallel", "arbitrary")),
    )(q, k, v, do)


def _sb_fwd(x, h, w_qkv, gq, gk, w_out):
    qkv = _mm("sb_qkv", h, w_qkv)
    q, k, v = _sb_prep_fwd(qkv, gq, gk)
    o = _sb_attn_fwd(q, k, v)
    y = _mm("sb_out", o, w_out, add=x)
    return y, (h, qkv, q, k, v, o)


def _sb_bwd(saved, dy, w_qkv, gq, gk, w_out):
    h, qkv, q, k, v, o = saved
    d_wout = _mm("sb_out_wgrad", o, dy, ta=True, out_dtype=GRAD_DTYPE)
    do = _mm("sb_out_dgrad", dy, w_out, tb=True)
    dq, dk, dv = _sb_attn_bwd(q, k, v, do)
    dqkv, dgq, dgk = _sb_prep_bwd(qkv, gq, gk, dq, dk, dv)
    d_wqkv = _mm("sb_qkv_wgrad", h, dqkv, ta=True, out_dtype=GRAD_DTYPE)
    dh = _mm("sb_qkv_dgrad", dqkv, w_qkv, tb=True)
    return dh, d_wqkv, dgq, dgk, d_wout


DN_QKV = 3 * N_HEADS * HEAD
DN_PROJ = DN_QKV + N_HEADS * HEAD + LANES
DN_CONV = 4
HALO = 8
CONV_COLS = 512


def _dn_conv_fwd(proj, conv_w, tt=256):
    t = proj.shape[0]
    tt = min(tt, t)

    def body(u_ref, prev_ref, w_ref, c_ref):
        i = pl.program_id(0)
        for cc in range(DN_QKV // CONV_COLS):
            cs = slice(cc * CONV_COLS, (cc + 1) * CONV_COLS)
            cur = u_ref[:, cs]
            prev = jnp.where(i > 0, prev_ref[:, cs], 0.0)
            ext = jnp.concatenate([prev, cur], axis=0)
            y = cur * w_ref[DN_CONV - 1:DN_CONV, cs]
            for j in range(DN_CONV - 1):
                y = y + pltpu.roll(ext, DN_CONV - 1 - j, 0)[HALO:] * w_ref[j:j + 1, cs]
            c_ref[:, cs] = y

    return _rows("dn_conv", body,
                 [(proj, (tt, DN_QKV), lambda i: (i, 0)),
                  (proj, (HALO, DN_QKV), lambda i: (jnp.maximum(i * (tt // HALO) - 1, 0), 0))],
                 [_sds((t, DN_QKV))], tt=tt, consts=[conv_w])[0]


def _dn_conv_bwd(proj, conv_w, dc, dz, dab, tt=256):
    t = proj.shape[0]
    tt = min(tt, t)
    nblk = t // tt

    def body(u_ref, prev_ref, dc_ref, next_ref, dz_ref, dab_ref, w_ref, dp_ref, dw_ref):
        i = pl.program_id(0)
        dws = []
        for cc in range(DN_QKV // CONV_COLS):
            cs = slice(cc * CONV_COLS, (cc + 1) * CONV_COLS)
            cur = u_ref[:, cs]
            prev = jnp.where(i > 0, prev_ref[:, cs], 0.0)
            ext_u = jnp.concatenate([prev, cur], axis=0)
            d = dc_ref[:, cs]
            nxt = jnp.where(i < nblk - 1, next_ref[:, cs], 0.0)
            ext_d = jnp.concatenate([d, nxt], axis=0)
            du = d * w_ref[DN_CONV - 1:DN_CONV, cs]
            rows = [jnp.sum(d * cur, axis=0, keepdims=True)]
            for j in range(DN_CONV - 2, -1, -1):
                sh = DN_CONV - 1 - j
                du = du + pltpu.roll(ext_d, tt + HALO - sh, 0)[:tt] * w_ref[j:j + 1, cs]
                rows.insert(0, jnp.sum(d * pltpu.roll(ext_u, sh, 0)[HALO:], axis=0, keepdims=True))
            dp_ref[:, cs] = du.astype(BF16)
            dws.append(jnp.concatenate(rows, axis=0))
        dp_ref[:, DN_QKV:DN_QKV + N_HEADS * HEAD] = dz_ref[...].astype(BF16)
        dp_ref[:, DN_QKV + N_HEADS * HEAD:] = dab_ref[...].astype(BF16)
        _acc(dw_ref, jnp.concatenate(dws, axis=1))

    return _rows("dn_conv_bwd", body,
                 [(proj, (tt, DN_QKV), lambda i: (i, 0)),
                  (proj, (HALO, DN_QKV), lambda i: (jnp.maximum(i * (tt // HALO) - 1, 0), 0)),
                  dc,
                  (dc, (HALO, DN_QKV), lambda i: (jnp.minimum((i + 1) * (tt // HALO), t // HALO - 1), 0)),
                  dz, dab],
                 [_sds((t, DN_PROJ), BF16)], tt=tt, consts=[conv_w], accs=[_sds(conv_w.shape)])


def _l2n(x):
    return x * lax.rsqrt(jnp.sum(x * x, axis=-1, keepdims=True) + NORM_EPS)


def _dn_qkv(cq, ck, cv):
    return _l2n(_silu(cq)) * (HEAD ** -0.5), _l2n(_silu(ck)), _silu(cv)


def _dn_gates(ab, a_log, dt_bias):
    lane = lax.broadcasted_iota(jnp.int32, ab.shape, 1)
    g = -jnp.exp(a_log) * _softplus(ab + dt_bias)
    return jnp.where(lane < N_HEADS, g, jnp.where(lane < 2 * N_HEADS, _sigmoid(ab), 0.0))


def _ab_spec(tt):
    return (tt, LANES), lambda i: (i, DN_PROJ // LANES - 1)


def _dn_prep_fwd(c, proj, a_log, dt_bias, tt=256):
    t = c.shape[0]
    tt = min(tt, t)

    def body(c_ref, ab_ref, al_ref, dt_ref, q_ref, k_ref, v_ref, g_ref):
        for h in range(N_HEADS):
            q_ref[h], k_ref[h], v_ref[h] = _dn_qkv(_heads_in(c_ref, h), _heads_in(c_ref, N_HEADS + h), _heads_in(c_ref, 2 * N_HEADS + h))
        g_ref[...] = _dn_gates(ab_ref[...], al_ref[...], dt_ref[...])

    hm = _sds((N_HEADS, t, HEAD))
    return _rows("dn_prep", body, [c, (proj,) + _ab_spec(tt)], [hm, hm, hm, _sds((t, LANES))], tt=tt, consts=[a_log, dt_bias])


def _dn_prep_bwd(c, proj, a_log, dt_bias, dq, dk, dv, dgates, tt=256):
    t = c.shape[0]
    tt = min(tt, t)

    def body(c_ref, ab_ref, dq_ref, dk_ref, dv_ref, dg_ref, al_ref, dt_ref, dc_ref, dab_ref, dal_ref, ddt_ref):
        for h in range(N_HEADS):
            _, vjp = jax.vjp(_dn_qkv, _heads_in(c_ref, h), _heads_in(c_ref, N_HEADS + h), _heads_in(c_ref, 2 * N_HEADS + h))
            a, b, d = vjp((dq_ref[h], dk_ref[h], dv_ref[h]))
            dc_ref[:, h * HEAD:(h + 1) * HEAD] = a
            dc_ref[:, (N_HEADS + h) * HEAD:(N_HEADS + h + 1) * HEAD] = b
            dc_ref[:, (2 * N_HEADS + h) * HEAD:(2 * N_HEADS + h + 1) * HEAD] = d
        _, vjp = jax.vjp(_dn_gates, ab_ref[...], al_ref[...], dt_ref[...])
        dab, dal, ddt = vjp(dg_ref[...])
        dab_ref[...] = dab
        _acc(dal_ref, dal)
        _acc(ddt_ref, ddt)

    return _rows("dn_prep_bwd", body, [c, (proj,) + _ab_spec(tt), dq, dk, dv, dgates], [_sds(c.shape), _sds((t, LANES))],
                 tt=tt, consts=[a_log, dt_bias], accs=[_sds(a_log.shape), _sds(dt_bias.shape)])


def _bdot(a, b, prec=None):
    return lax.dot_general(a, b, (((2,), (1,)), ((0,), (0,))), precision=prec, preferred_element_type=F32)


def _bdot_nt(a, b, prec=None):
    return lax.dot_general(a, b, (((2,), (2,)), ((0,), (0,))), precision=prec, preferred_element_type=F32)


def _bdot_tn(a, b, prec=None):
    return lax.dot_general(a, b, (((1,), (1,)), ((0,), (0,))), precision=prec, preferred_element_type=F32)


def _inv_raw(low):
    c = low.shape[-1]
    r = lax.broadcasted_iota(jnp.int32, (c, c), 0)
    s = lax.broadcasted_iota(jnp.int32, (c, c), 1)
    m = jnp.where(r == s, 1.0, 0.0) - low
    p = _bdot(low, low, F32X3)
    n_fac = int(math.log2(c)) - 1
    for i in range(n_fac):
        m = m + _bdot(m, p, F32X3)
        if i < n_fac - 1:
            p = _bdot(p, p, F32X3)
    return m


@jax.custom_vjp
def _inv_unit_lower(low):
    return _inv_raw(low)


def _inv_fwd(low):
    m = _inv_raw(low)
    return m, m


def _inv_bwd(m, dm):
    return (-_bdot_nt(_bdot_tn(m, dm, F32X3), m, F32X3),)


_inv_unit_lower.defvjp(_inv_fwd, _inv_bwd)


def _dn_chunk(q, k, v, gates, s):
    nh, rows, _ = q.shape
    c = DN_CHUNK
    nc = rows // c
    nb = nh * nc
    lane = lax.broadcasted_iota(jnp.int32, gates.shape, 1)

    def column(j):
        return jnp.sum(jnp.where(lane == j, gates, 0.0), axis=1, keepdims=True)[None]

    def fold(x):
        return x.reshape((nb, c) + x.shape[2:])

    g_col = fold(jnp.concatenate([column(h) for h in range(nh)], axis=0))
    b_col = fold(jnp.concatenate([column(h + nh) for h in range(nh)], axis=0))
    q, k, v = fold(q), fold(k), fold(v)
    r = lax.broadcasted_iota(jnp.int32, (c, c), 0)
    cc = lax.broadcasted_iota(jnp.int32, (c, c), 1)
    causal, strict = r >= cc, r > cc
    incl = jnp.broadcast_to(jnp.where(causal, 1.0, 0.0), (nb, c, c))
    upper = jnp.broadcast_to(jnp.where(r <= cc, 1.0, 0.0), (nb, c, c))
    gb = jnp.broadcast_to(g_col, (nb, c, LANES))
    gbc = jnp.broadcast_to(g_col, (nb, c, c))
    gc = _bdot(incl, gb, F32X3)
    gc_r = _bdot(incl, gbc, F32X3)
    gc_c = _bdot_tn(gbc, upper, F32X3)
    decay = jnp.where(causal, jnp.exp(jnp.where(causal, gc_r - gc_c, 0.0)), 0.0)
    kb = k * b_col
    low = jnp.where(strict, _bdot_nt(kb, k) * decay, 0.0)
    m = _inv_unit_lower(low)
    egc = jnp.exp(gc)
    gl = jnp.sum(gb, axis=1, keepdims=True)
    local = (_bdot(m, v * b_col, F32X3), _bdot(m, kb * egc, F32X3), _bdot_nt(q, k) * decay, q * egc,
             k * jnp.exp(gl - gc), jnp.exp(gl))
    outs = []
    for i in range(nc):
        u, w, attn, q_dec, k_dec, cd = (x.reshape((nh, nc) + x.shape[1:])[:, i] for x in local)
        v_new = u - _bdot(w, s)
        outs.append(_bdot(q_dec, s) + _bdot(attn, v_new))
        s = s * cd + _bdot_tn(k_dec, v_new)
    return jnp.concatenate(outs, axis=1), s


def _dn_chunks_fwd(q, k, v, gates):
    nh, t, _ = q.shape
    n = t // DN_STEP

    def kern(q_ref, k_ref, v_ref, g_ref, o_ref, sin_ref, s_scr):
        @pl.when(pl.program_id(0) == 0)
        def _():
            s_scr[...] = jnp.zeros(s_scr.shape, F32)

        s = s_scr[...]
        sin_ref[0] = s
        o_ref[...], s_scr[...] = _dn_chunk(q_ref[...], k_ref[...], v_ref[...], g_ref[...], s)

    blk = pl.BlockSpec((nh, DN_STEP, HEAD), lambda i: (0, i, 0))
    return pl.pallas_call(
        kern, name="dn_chunks_fwd", grid=(n,),
        in_specs=[blk, blk, blk, pl.BlockSpec((DN_STEP, LANES), lambda i: (i, 0))],
        out_specs=[blk, pl.BlockSpec((1, nh, HEAD, HEAD), lambda i: (i, 0, 0, 0))],
        out_shape=[_sds((nh, t, HEAD)), _sds((n, nh, HEAD, HEAD))],
        scratch_shapes=[pltpu.VMEM((nh, HEAD, HEAD), F32)],
        compiler_params=_cparams(dimension_semantics=("arbitrary",)),
    )(q, k, v, gates)


def _dn_chunks_bwd(q, k, v, gates, s_in, do):
    nh, t, _ = q.shape
    n = t // DN_STEP

    def kern(q_ref, k_ref, v_ref, g_ref, sin_ref, do_ref, dq_ref, dk_ref, dv_ref, dg_ref, ds_scr):
        @pl.when(pl.program_id(0) == 0)
        def _():
            ds_scr[...] = jnp.zeros(ds_scr.shape, F32)

        _, vjp = jax.vjp(_dn_chunk, q_ref[...], k_ref[...], v_ref[...], g_ref[...], sin_ref[0])
        dq_ref[...], dk_ref[...], dv_ref[...], dg_ref[...], ds_scr[...] = vjp((do_ref[...], ds_scr[...]))

    blk = pl.BlockSpec((nh, DN_STEP, HEAD), lambda i: (0, n - 1 - i, 0))
    gblk = pl.BlockSpec((DN_STEP, LANES), lambda i: (n - 1 - i, 0))
    hm = _sds((nh, t, HEAD))
    return pl.pallas_call(
        kern, name="dn_chunks_bwd", grid=(n,),
        in_specs=[blk, blk, blk, gblk, pl.BlockSpec((1, nh, HEAD, HEAD), lambda i: (n - 1 - i, 0, 0, 0)), blk],
        out_specs=[blk, blk, blk, gblk],
        out_shape=[hm, hm, hm, _sds((t, LANES))],
        scratch_shapes=[pltpu.VMEM((nh, HEAD, HEAD), F32)],
        compiler_params=_cparams(dimension_semantics=("arbitrary",)),
    )(q, k, v, gates, s_in, do)


def _dn_gate_out(o, z, g):
    return _rms(o, g) * _silu(z)


def _z_spec(tt):
    return (tt, N_HEADS * HEAD), lambda i: (i, DN_QKV // (N_HEADS * HEAD))


def _dn_post_fwd(o, proj, out_norm, tt=256):
    t = o.shape[1]
    tt = min(tt, t)

    def body(o_ref, z_ref, g_ref, y_ref):
        for h in range(N_HEADS):
            y_ref[:, h * HEAD:(h + 1) * HEAD] = _dn_gate_out(o_ref[h], _heads_in(z_ref, h), g_ref[...]).astype(BF16)

    return _rows("dn_post", body, [o, (proj,) + _z_spec(tt)], [_sds((t, N_HEADS * HEAD), BF16)], tt=tt, consts=[out_norm])[0]


def _dn_post_bwd(o, proj, out_norm, dy, tt=256):
    t = o.shape[1]
    tt = min(tt, t)

    def body(o_ref, z_ref, dy_ref, g_ref, do_ref, dz_ref, dg_ref):
        dg = jnp.zeros(g_ref.shape, F32)
        for h in range(N_HEADS):
            _, vjp = jax.vjp(_dn_gate_out, o_ref[h], _heads_in(z_ref, h), g_ref[...])
            a, b, d = vjp(_heads_in(dy_ref, h))
            do_ref[h] = a
            dz_ref[:, h * HEAD:(h + 1) * HEAD] = b
            dg = dg + d
        _acc(dg_ref, dg)

    return _rows("dn_post_bwd", body, [o, (proj,) + _z_spec(tt), dy], [_sds(o.shape), _sds((t, N_HEADS * HEAD))], tt=tt,
                 consts=[out_norm], accs=[_sds(out_norm.shape)])


def _dn_fwd(x, h, w_in, conv_w, a_log, dt_bias, out_norm, w_out):
    proj = _mm("dn_in", h, w_in)
    c = _dn_conv_fwd(proj, conv_w)
    q, k, v, gates = _dn_prep_fwd(c, proj, a_log, dt_bias)
    o, s_in = _dn_chunks_fwd(q, k, v, gates)
    on = _dn_post_fwd(o, proj, out_norm)
    y = _mm("dn_out", on, w_out, add=x)
    return y, (h, proj, c, q, k, v, gates, o, s_in, on)


def _dn_bwd(saved, dy, w_in, conv_w, a_log, dt_bias, out_norm, w_out):
    h, proj, c, q, k, v, gates, o, s_in, on = saved
    d_wout = _mm("dn_out_wgrad", on, dy, ta=True, out_dtype=GRAD_DTYPE)
    don = _mm("dn_out_dgrad", dy, w_out, tb=True)
    do, dz, d_out_norm = _dn_post_bwd(o, proj, out_norm, don)
    dq, dk, dv, dgates = _dn_chunks_bwd(q, k, v, gates, s_in, do)
    dc, dab, d_a_log, d_dt_bias = _dn_prep_bwd(c, proj, a_log, dt_bias, dq, dk, dv, dgates)
    dproj, d_conv_w = _dn_conv_bwd(proj, conv_w, dc, dz, dab)
    d_win = _mm("dn_in_wgrad", h, dproj, ta=True, out_dtype=GRAD_DTYPE)
    dh = _mm("dn_in_dgrad", dproj, w_in, tb=True)
    return dh, d_win, d_conv_w, d_a_log, d_dt_bias, d_out_norm, d_wout


MLA_SCALE = MLA_QK ** -0.5
MLA_C = 512


def _swap_raw(x):
    lane = lax.broadcasted_iota(jnp.int32, x.shape, 1)
    half = MLA_ROPE // 2
    y = jnp.where(lane < half, pltpu.roll(x, LANES - half, 1), pltpu.roll(x, half, 1))
    return jnp.where(lane < MLA_ROPE, y, 0.0)


@jax.custom_vjp
def _swap_halves(x):
    return _swap_raw(x)


_swap_halves.defvjp(lambda x: (_swap_raw(x), None), lambda _, d: (_swap_raw(d),))


def _rms_rope(x, g, cos, sin):
    y = x * lax.rsqrt(jnp.sum(x * x, axis=-1, keepdims=True) * (1.0 / MLA_ROPE) + NORM_EPS) * g
    return y * cos + _swap_halves(y) * sin


def _mla_latent(cq, ckv, kr, gq, gkv, gkr, cos, sin):
    return _rms(cq, gq), _rms(ckv, gkv), _rms_rope(kr, gkr, cos, sin)


def _mla_prep1_fwd(c, gq, gkv, gkr, cos, sin):
    t = c.shape[0]

    def body(c_ref, cos_ref, sin_ref, gq_ref, gkv_ref, gkr_ref, cq_ref, ckv_ref, kr_ref):
        a, b, r = _mla_latent(c_ref[:, :256], c_ref[:, 256:384], c_ref[:, 384:], gq_ref[...], gkv_ref[...], gkr_ref[...],
                              cos_ref[...], sin_ref[...])
        cq_ref[...] = a.astype(BF16)
        ckv_ref[...] = b.astype(BF16)
        kr_ref[...] = r.astype(BF16)

    return _rows("mla_prep1", body, [c, cos, sin], [_sds((t, 256), BF16), _sds((t, HEAD), BF16), _sds((t, HEAD), BF16)],
                 tt=512, consts=[gq, gkv, gkr])


def _mla_prep1_bwd(c, gq, gkv, gkr, cos, sin, dcq, dckv, dkr_heads):
    def body(c_ref, cos_ref, sin_ref, dcq_ref, dckv_ref, dkr_ref, gq_ref, gkv_ref, gkr_ref, dc_ref, dgq_ref, dgkv_ref, dgkr_ref):
        dkr = dkr_ref[0]
        for h in range(1, N_HEADS):
            dkr = dkr + dkr_ref[h]
        _, vjp = jax.vjp(_mla_latent, c_ref[:, :256], c_ref[:, 256:384], c_ref[:, 384:], gq_ref[...], gkv_ref[...], gkr_ref[...],
                         cos_ref[...], sin_ref[...])
        a, b, r, d1, d2, d3, _, _ = vjp((dcq_ref[...], dckv_ref[...], dkr))
        dc_ref[:, :256] = a.astype(BF16)
        dc_ref[:, 256:384] = b.astype(BF16)
        dc_ref[:, 384:] = r.astype(BF16)
        _acc(dgq_ref, d1)
        _acc(dgkv_ref, d2)
        _acc(dgkr_ref, d3)

    return _rows("mla_prep1_bwd", body, [c, cos, sin, dcq, dckv, dkr_heads], [_sds(c.shape, BF16)], tt=512,
                 consts=[gq, gkv, gkr], accs=[_sds(gq.shape), _sds(gkv.shape), _sds(gkr.shape)])


def _mla_heads(qn, qr, kn, gqn, gqr, gkn, cos, sin):
    return _rms(qn, gqn) * MLA_SCALE, _rms_rope(qr, gqr, cos, sin) * MLA_SCALE, _rms(kn, gkn)


def _mla_prep2_fwd(qa, kv, gqn, gqr, gkn, cos, sin):
    t = qa.shape[0]

    def body(qa_ref, kv_ref, cos_ref, sin_ref, gqn_ref, gqr_ref, gkn_ref, qn_ref, qr_ref, kn_ref, v_ref):
        for h in range(N_HEADS):
            a, b, c = _mla_heads(_heads_in(qa_ref, h), _heads_in(qa_ref, N_HEADS + h), _heads_in(kv_ref, h),
                                 gqn_ref[...], gqr_ref[...], gkn_ref[...], cos_ref[...], sin_ref[...])
            qn_ref[h] = a.astype(BF16)
            qr_ref[h] = b.astype(BF16)
            kn_ref[h] = c.astype(BF16)
            v_ref[h] = _heads_in(kv_ref, N_HEADS + h).astype(BF16)

    hm = _sds((N_HEADS, t, HEAD), BF16)
    return _rows("mla_prep2", body, [qa, kv, cos, sin], [hm, hm, hm, hm], tt=256, consts=[gqn, gqr, gkn])


def _mla_prep2_bwd(qa, kv, gqn, gqr, gkn, cos, sin, dqn, dqr, dkn, dv):
    def body(qa_ref, kv_ref, cos_ref, sin_ref, dqn_ref, dqr_ref, dkn_ref, dv_ref, gqn_ref, gqr_ref, gkn_ref,
             dqa_ref, dkv_ref, d1_ref, d2_ref, d3_ref):
        d1 = jnp.zeros(gqn_ref.shape, F32)
        d2 = jnp.zeros(gqr_ref.shape, F32)
        d3 = jnp.zeros(gkn_ref.shape, F32)
        for h in range(N_HEADS):
            _, vjp = jax.vjp(_mla_heads, _heads_in(qa_ref, h), _heads_in(qa_ref, N_HEADS + h), _heads_in(kv_ref, h),
                             gqn_ref[...], gqr_ref[...], gkn_ref[...], cos_ref[...], sin_ref[...])
            a, b, c, e1, e2, e3, _, _ = vjp((dqn_ref[h], dqr_ref[h], dkn_ref[h]))
            dqa_ref[:, h * HEAD:(h + 1) * HEAD] = a.astype(BF16)
            dqa_ref[:, (N_HEADS + h) * HEAD:(N_HEADS + h + 1) * HEAD] = b.astype(BF16)
            dkv_ref[:, h * HEAD:(h + 1) * HEAD] = c.astype(BF16)
            dkv_ref[:, (N_HEADS + h) * HEAD:(N_HEADS + h + 1) * HEAD] = dv_ref[h].astype(BF16)
            d1, d2, d3 = d1 + e1, d2 + e2, d3 + e3
        _acc(d1_ref, d1)
        _acc(d2_ref, d2)
        _acc(d3_ref, d3)

    return _rows("mla_prep2_bwd", body, [qa, kv, cos, sin, dqn, dqr, dkn, dv], [_sds(qa.shape, BF16), _sds(kv.shape, BF16)],
                 tt=256, consts=[gqn, gqr, gkn], accs=[_sds(gqn.shape), _sds(gqr.shape), _sds(gkn.shape)])


def _mla_attn_fwd(qn, qr, kn, kr, v):
    nh, t, _ = qn.shape
    bq = _q_block(t)
    per = bq // ATT_BLOCK

    def kern(qn_ref, qr_ref, kn_ref, kr_ref, v_ref, o_ref, lse_ref):
        qb = pl.program_id(1)
        qv = jnp.concatenate([qn_ref[0], qr_ref[0]], axis=1)
        order = _key_order(bq, qb)

        def body(kb, carry, diagonal):
            acc, m, l = carry
            off = pl.multiple_of(kb * ATT_BLOCK, ATT_BLOCK)
            kv = jnp.concatenate([kn_ref[0, pl.ds(off, ATT_BLOCK), :], kr_ref[pl.ds(off, ATT_BLOCK), :]], axis=1)
            s = _dot_nt(qv, kv)
            if diagonal:
                s = jnp.where(order >= kb * ATT_BLOCK, s, -jnp.inf)
            m_new = jnp.maximum(m, jnp.max(s, axis=1, keepdims=True))
            alpha = jnp.exp(m - m_new)
            p = jnp.exp(s - m_new)
            acc = acc * alpha + _dot(p.astype(BF16), v_ref[0, pl.ds(off, ATT_BLOCK), :])
            return acc, m_new, l * alpha + jnp.sum(p, axis=1, keepdims=True)

        init = (jnp.zeros((bq, HEAD), F32), jnp.full((bq, 1), -jnp.inf, F32), jnp.zeros((bq, 1), F32))
        carry = lax.fori_loop(0, qb * per, lambda i, c: body(i, c, False), init)
        acc, m, l = lax.fori_loop(qb * per, (qb + 1) * per, lambda i, c: body(i, c, True), carry)
        o_ref[...] = acc / l
        lse_ref[...] = jnp.broadcast_to(m + jnp.log(l), (bq, HEAD))

    blk = pl.BlockSpec((1, bq, HEAD), lambda h, i: (h, i, 0))
    full = pl.BlockSpec((1, t, HEAD), lambda h, i: (h, 0, 0))
    tok = pl.BlockSpec((bq, HEAD), lambda h, i: (i, h))
    return pl.pallas_call(
        kern, name="mla_attn_fwd", grid=(nh, t // bq),
        in_specs=[blk, blk, full, pl.BlockSpec((t, HEAD), lambda h, i: (0, 0)), full],
        out_specs=[tok, tok], out_shape=[_sds((t, nh * HEAD)), _sds((t, nh * HEAD))],
        compiler_params=_cparams(dimension_semantics=("parallel", "arbitrary")),
    )(qn, qr, kn, kr, v)


def _mla_attn_bwd(qn, qr, kn, kr, v, o, lse, do):
    nh, t, _ = qn.shape
    bq = _q_block(t)
    per = bq // ATT_BLOCK

    def kern(qn_ref, qr_ref, kn_ref, kr_ref, v_ref, o_ref, lse_ref, do_ref, dqn_ref, dqr_ref, dkn_ref, dkr_ref, dv_ref):
        qb = pl.program_id(1)

        @pl.when(qb == 0)
        def _():
            dkn_ref[...] = jnp.zeros(dkn_ref.shape, F32)
            dkr_ref[...] = jnp.zeros(dkr_ref.shape, F32)
            dv_ref[...] = jnp.zeros(dv_ref.shape, F32)

        qv = jnp.concatenate([qn_ref[0], qr_ref[0]], axis=1)
        dov = do_ref[...]
        dob = dov.astype(BF16)
        delta = jnp.sum(dov * o_ref[...], axis=1, keepdims=True)
        lse_col = lse_ref[:, :1]
        order = _key_order(bq, qb)

        def body(kb, dq, diagonal):
            off = pl.multiple_of(kb * ATT_BLOCK, ATT_BLOCK)
            kv = jnp.concatenate([kn_ref[0, pl.ds(off, ATT_BLOCK), :], kr_ref[pl.ds(off, ATT_BLOCK), :]], axis=1)
            vv = v_ref[0, pl.ds(off, ATT_BLOCK), :]
            p = jnp.exp(_dot_nt(qv, kv) - lse_col)
            if diagonal:
                p = jnp.where(order >= kb * ATT_BLOCK, p, 0.0)
            ds = (p * (_dot_nt(dob, vv) - delta)).astype(BF16)
            dk = _dot_tn(ds, qv)
            dkn_ref[0, pl.ds(off, ATT_BLOCK), :] += dk[:, :HEAD]
            dkr_ref[0, pl.ds(off, ATT_BLOCK), :] += dk[:, HEAD:]
            dv_ref[0, pl.ds(off, ATT_BLOCK), :] += _dot_tn(p.astype(BF16), dob)
            return dq + _dot(ds, kv)

        dq = lax.fori_loop(0, qb * per, lambda i, c: body(i, c, False), jnp.zeros((bq, 2 * HEAD), F32))
        dq = lax.fori_loop(qb * per, (qb + 1) * per, lambda i, c: body(i, c, True), dq)
        dqn_ref[0] = dq[:, :HEAD]
        dqr_ref[0] = dq[:, HEAD:]

    hm = _sds((nh, t, HEAD))
    blk = pl.BlockSpec((1, bq, HEAD), lambda h, i: (h, i, 0))
    full = pl.BlockSpec((1, t, HEAD), lambda h, i: (h, 0, 0))
    tok = pl.BlockSpec((bq, HEAD), lambda h, i: (i, h))
    return pl.pallas_call(
        kern, name="mla_attn_bwd", grid=(nh, t // bq),
        in_specs=[blk, blk, full, pl.BlockSpec((t, HEAD), lambda h, i: (0, 0)), full, tok, tok, tok],
        out_specs=[blk, blk, full, full, full], out_shape=[hm, hm, hm, hm, hm],
        compiler_params=_cparams(dimension_semantics=("parallel", "arbitrary")),
    )(qn, qr, kn, kr, v, o, lse, do)


def _rope_tables(t):
    inv_freq = ROPE_THETA ** (-jnp.arange(0, MLA_ROPE, 2, dtype=F32) / MLA_ROPE)
    ang = jnp.arange(t, dtype=F32)[:, None] * inv_freq[None, :]
    c, s = jnp.cos(ang), jnp.sin(ang)
    pad = ((0, 0), (0, LANES - MLA_ROPE))
    return jnp.pad(jnp.concatenate([c, c], axis=1), pad), jnp.pad(jnp.concatenate([-s, s], axis=1), pad)


def _pad_lanes(v, n=LANES):
    return jnp.pad(v, (0, n - v.shape[0])).reshape(1, n)


def _mla_layout(w_down, w_uq, w_ukv):
    w_down_p = jnp.pad(w_down, ((0, 0), (0, MLA_C - w_down.shape[1])))
    uq = w_uq.reshape(w_uq.shape[0], N_HEADS, MLA_QK)
    rope = jnp.pad(uq[:, :, HEAD:], ((0, 0), (0, 0), (0, LANES - MLA_ROPE)))
    w_uq_p = jnp.concatenate([uq[:, :, :HEAD].reshape(-1, N_HEADS * HEAD), rope.reshape(-1, N_HEADS * LANES)], axis=1)
    ukv = w_ukv.reshape(w_ukv.shape[0], N_HEADS, 2 * HEAD)
    w_ukv_p = jnp.concatenate([ukv[:, :, :HEAD].reshape(-1, N_HEADS * HEAD), ukv[:, :, HEAD:].reshape(-1, N_HEADS * HEAD)], axis=1)
    return w_down_p, w_uq_p, w_ukv_p


def _mla_unlayout(d_down_p, d_uq_p, d_ukv_p):
    d_down = d_down_p[:, :256 + HEAD + MLA_ROPE]
    nope = d_uq_p[:, :N_HEADS * HEAD].reshape(-1, N_HEADS, HEAD)
    rope = d_uq_p[:, N_HEADS * HEAD:].reshape(-1, N_HEADS, LANES)[:, :, :MLA_ROPE]
    d_uq = jnp.concatenate([nope, rope], axis=2).reshape(-1, N_HEADS * MLA_QK)
    kn = d_ukv_p[:, :N_HEADS * HEAD].reshape(-1, N_HEADS, HEAD)
    vv = d_ukv_p[:, N_HEADS * HEAD:].reshape(-1, N_HEADS, HEAD)
    d_ukv = jnp.concatenate([kn, vv], axis=2).reshape(-1, N_HEADS * 2 * HEAD)
    return d_down, d_uq, d_ukv


def _mla_weight_shapes():
    return (_sds((1024, MLA_C), BF16), _sds((1, 256)), _sds((1, HEAD)), _sds((256, 2048), BF16), _sds((HEAD, 2048), BF16),
            _sds((1, HEAD)), _sds((1, HEAD)), _sds((1, HEAD)), _sds((1, HEAD)), _sds((1024, 1024), BF16),
            _sds((4096, HEAD)), _sds((4096, HEAD)))


def _mla_fwd(x, h, w_down, gq, gkv, w_uq, w_ukv, gqn, gqr, gkn, gkr, w_out, cos, sin):
    c = _mm("mla_down", h, w_down)
    cq, ckv, kr = _mla_prep1_fwd(c, gq, gkv, gkr, cos, sin)
    qa = _mm("mla_uq", cq, w_uq)
    kv = _mm("mla_ukv", ckv, w_ukv)
    qn, qr, kn, v = _mla_prep2_fwd(qa, kv, gqn, gqr, gkn, cos, sin)
    o, lse = _mla_attn_fwd(qn, qr, kn, kr, v)
    y = _mm("mla_out", o, w_out, add=x)
    return y, (h, c, cq, ckv, kr, qa, kv, qn, qr, kn, v, o, lse)


def _mla_bwd(saved, dy, w_down, gq, gkv, w_uq, w_ukv, gqn, gqr, gkn, gkr, w_out, cos, sin):
    h, c, cq, ckv, kr, qa, kv, qn, qr, kn, v, o, lse = saved
    d_wout = _mm("mla_out_wgrad", o, dy, ta=True, out_dtype=GRAD_DTYPE)
    do = _mm("mla_out_dgrad", dy, w_out, tb=True)
    dqn, dqr, dkn, dkr, dv = _mla_attn_bwd(qn, qr, kn, kr, v, o, lse, do)
    dqa, dkv, dgqn, dgqr, dgkn = _mla_prep2_bwd(qa, kv, gqn, gqr, gkn, cos, sin, dqn, dqr, dkn, dv)
    d_wuq = _mm("mla_uq_wgrad", cq, dqa, ta=True, out_dtype=GRAD_DTYPE)
    d_wukv = _mm("mla_ukv_wgrad", ckv, dkv, ta=True, out_dtype=GRAD_DTYPE)
    dcq = _mm("mla_uq_dgrad", dqa, w_uq, tb=True)
    dckv = _mm("mla_ukv_dgrad", dkv, w_ukv, tb=True)
    dc, dgq, dgkv, dgkr = _mla_prep1_bwd(c, gq, gkv, gkr, cos, sin, dcq, dckv, dkr)
    d_wdown = _mm("mla_down_wgrad", h, dc, ta=True, out_dtype=GRAD_DTYPE)
    dh = _mm("mla_down_dgrad", dc, w_down, tb=True)
    return dh, d_wdown, dgq, dgkv, d_wuq, d_wukv, dgqn, dgqr, dgkn, dgkr, d_wout


def _loss_head(y, target):
    d = y.shape[1]

    def body(y_ref, t_ref, dy_ref, l_ref):
        err = y_ref[...] - t_ref[...]
        dy_ref[...] = err * (1.0 / d)
        part = 0.5 * jnp.sum(jnp.sum(err * err, axis=1, keepdims=True) * (1.0 / d), axis=0, keepdims=True)
        _acc(l_ref, jnp.broadcast_to(part, (1, LANES)))

    return _rows("loss_head", body, [y, target], [_sds(y.shape)], tt=512, accs=[_sds((1, LANES))])


MESH_ID = pl.DeviceIdType.MESH
HBM_SPEC = pl.BlockSpec(memory_space=pltpu.HBM)


def _all_gather(name, x):
    m_per, n = x.shape

    def body(x_ref, out_ref, send_sems, recv_sems, local_sem):
        x, y, c = lax.axis_index("x"), lax.axis_index("y"), lax.axis_index("c")
        me, sibling = (x, y, c), (x, y, 1 - c)
        chips = [(1 - x, y), (x, 1 - y), (1 - x, 1 - y)]

        def rows(px, py, pc):
            return out_ref.at[pl.ds((4 * px + 2 * py + pc) * m_per, m_per), :]

        def copy(k, block, to, src=None):
            return pltpu.make_async_remote_copy(
                src_ref=rows(*block) if src is None else src, dst_ref=rows(*block),
                send_sem=send_sems.at[k], recv_sem=recv_sems.at[k], device_id=to, device_id_type=MESH_ID)

        mine = pltpu.make_async_copy(x_ref, rows(*me), local_sem)
        mine.start()
        first = [copy(0, me, sibling, src=x_ref)]
        first += [copy(1 + j, me, (*chip, c), src=x_ref) for j, chip in enumerate(chips)]
        for cp in first:
            cp.start()
        passed = [copy(4 + j, (*chip, c), sibling) for j, chip in enumerate(chips)]
        for j, chip in enumerate(chips):
            copy(1 + j, (*chip, c), me).wait_recv()
            passed[j].start()
        copy(0, sibling, me).wait_recv()
        for j, chip in enumerate(chips):
            copy(4 + j, (*chip, 1 - c), me).wait_recv()
        for cp in first + passed:
            cp.wait_send()
        mine.wait()

    return pl.pallas_call(
        body, name=name,
        out_shape=jax.ShapeDtypeStruct((N_DEV * m_per, n), x.dtype),
        in_specs=[HBM_SPEC], out_specs=HBM_SPEC,
        scratch_shapes=[pltpu.SemaphoreType.DMA((7,)), pltpu.SemaphoreType.DMA((7,)), pltpu.SemaphoreType.DMA],
    )(x)


def _all_gather_groups(name, xs):
    ng = len(xs)

    def body(*refs):
        x_refs, out_refs, token = refs[:ng], refs[ng:2 * ng], refs[2 * ng]
        send_sems, recv_sems, local_sems = refs[2 * ng + 1:]
        token[...] = jnp.zeros(token.shape, F32)
        x, y, c = lax.axis_index("x"), lax.axis_index("y"), lax.axis_index("c")
        me, sibling = (x, y, c), (x, y, 1 - c)
        chips = [(1 - x, y), (x, 1 - y), (1 - x, 1 - y)]

        def copy(g, k, block, to, src=None):
            px, py, pc = block
            dst = out_refs[g].at[4 * px + 2 * py + pc]
            return pltpu.make_async_remote_copy(
                src_ref=dst if src is None else src, dst_ref=dst,
                send_sem=send_sems.at[g, k], recv_sem=recv_sems.at[g, k], device_id=to, device_id_type=MESH_ID)

        mine = [pltpu.make_async_copy(x_refs[g], out_refs[g].at[4 * x + 2 * y + c], local_sems.at[g]) for g in range(ng)]
        for cp in mine:
            cp.start()
        first = []
        for g in range(ng):
            first.append(copy(g, 0, me, sibling, src=x_refs[g]))
            first += [copy(g, 1 + j, me, (*chip, c), src=x_refs[g]) for j, chip in enumerate(chips)]
        for cp in first:
            cp.start()
        passed = []
        for j, chip in enumerate(chips):
            for g in range(ng):
                copy(g, 1 + j, (*chip, c), me).wait_recv()
                passed.append(copy(g, 4 + j, (*chip, c), sibling))
                passed[-1].start()
        for g in range(ng):
            copy(g, 0, sibling, me).wait_recv()
            for j, chip in enumerate(chips):
                copy(g, 4 + j, (*chip, 1 - c), me).wait_recv()
        for cp in first + passed:
            cp.wait_send()
        for cp in mine:
            cp.wait()

    return pl.pallas_call(
        body, name=name,
        out_shape=[jax.ShapeDtypeStruct((N_DEV,) + x.shape, x.dtype) for x in xs] + [_sds((8, LANES))],
        in_specs=[HBM_SPEC] * ng, out_specs=[HBM_SPEC] * ng + [pl.BlockSpec(memory_space=pltpu.VMEM)],
        scratch_shapes=[pltpu.SemaphoreType.DMA((ng, 7)), pltpu.SemaphoreType.DMA((ng, 7)), pltpu.SemaphoreType.DMA((ng,))],
    )(*xs)


EFFECT = pltpu.SideEffectType.DATAFLOW_SIDE_EFFECTING
SEM_SPEC = pl.BlockSpec(memory_space=pltpu.SEMAPHORE)


def _push_copies(src_refs, land_refs, send_sems, recv_sems, chunked):
    x, y, c = lax.axis_index("x"), lax.axis_index("y"), lax.axis_index("c")
    me = 4 * x + 2 * y + c
    copies = []
    for g, (src, land) in enumerate(zip(src_refs, land_refs)):
        for k in range(1, N_DEV):
            px = 1 - x if k & 4 else x
            py = 1 - y if k & 2 else y
            pc = 1 - c if k & 1 else c
            copies.append(pltpu.make_async_remote_copy(
                src_ref=src.at[4 * px + 2 * py + pc] if chunked else src, dst_ref=land.at[me],
                send_sem=send_sems.at[g * (N_DEV - 1) + k - 1], recv_sem=recv_sems.at[g * (N_DEV - 1) + k - 1],
                device_id=(px, py, pc), device_id_type=MESH_ID))
    return copies


def _hbm(x):
    return pltpu.with_memory_space_constraint(x, pltpu.HBM)


def _push_start(name, srcs, lands, chunked):
    ng = len(srcs)

    def body(*refs):
        for cp in _push_copies(refs[:ng], refs[ng:2 * ng], refs[2 * ng], refs[2 * ng + 1], chunked):
            cp.start()
        refs[-1][...] = jnp.zeros(refs[-1].shape, F32)

    bufs = list(srcs) + list(lands)
    outs = pl.pallas_call(
        body, name=name,
        out_shape=(pltpu.SemaphoreType.DMA((ng * (N_DEV - 1),)), pltpu.SemaphoreType.DMA((ng * (N_DEV - 1),)),
                   *[pltpu.HBM(b.shape, b.dtype) for b in bufs], jax.ShapeDtypeStruct((8, LANES), F32)),
        in_specs=[HBM_SPEC] * (2 * ng),
        out_specs=(SEM_SPEC, SEM_SPEC, *[HBM_SPEC] * (2 * ng), pl.BlockSpec(memory_space=pltpu.VMEM)),
        input_output_aliases={i: 2 + i for i in range(2 * ng)},
        compiler_params=pltpu.CompilerParams(has_side_effects=EFFECT),
    )(*[_hbm(b) for b in bufs])
    return outs[0], outs[1], list(outs[2:2 + ng]), list(outs[2 + ng:2 + 2 * ng]), outs[-1]


def _push_wait(name, started, after, chunked):
    send_sems, recv_sems, srcs, lands, _ = started
    ng = len(srcs)

    def body(*refs):
        copies = _push_copies(refs[:ng], refs[ng:2 * ng], refs[2 * ng], refs[2 * ng + 1], chunked)
        for cp in copies:
            cp.wait_send()
        for cp in copies:
            cp.wait_recv()

    bufs = srcs + lands
    outs = pl.pallas_call(
        body, name=name,
        out_shape=tuple(pltpu.HBM(b.shape, b.dtype) for b in bufs),
        in_specs=[HBM_SPEC] * (2 * ng) + [SEM_SPEC, SEM_SPEC, pl.BlockSpec(memory_space=pl.ANY)],
        out_specs=tuple([HBM_SPEC] * (2 * ng)),
        input_output_aliases={i: i for i in range(2 * ng)},
        compiler_params=pltpu.CompilerParams(has_side_effects=EFFECT),
    )(*bufs, send_sems, recv_sems, after)
    return list(outs[:ng]), list(outs[ng:])


def _sum_adam_devices(sent, recv, dev, w, m, v):
    ndev, r, c_ = recv.shape
    tr = _pick(r, (256, 128, 96, 32))

    def body(dev_ref, own_ref, r_ref, w_ref, m_ref, v_ref, g_ref, d_ref, mo_ref, vo_ref):
        me = dev_ref[0]
        g = jnp.where(me == 0, own_ref[0], r_ref[0]).astype(F32)
        for j in range(1, ndev):
            g = g + jnp.where(me == j, own_ref[0], r_ref[j]).astype(F32)
        g_ref[...] = g
        d_ref[...], mo_ref[...], vo_ref[...] = _adam(w_ref[...], g, m_ref[...], v_ref[...])

    row = pl.BlockSpec((tr, c_), lambda i, dev_ref: (i, 0))
    return pl.pallas_call(
        body, name="sum_adam",
        grid_spec=pltpu.PrefetchScalarGridSpec(
            num_scalar_prefetch=1, grid=(r // tr,),
            in_specs=[pl.BlockSpec((1, tr, c_), lambda i, dev_ref: (dev_ref[0], i, 0)),
                      pl.BlockSpec((ndev, tr, c_), lambda i, dev_ref: (0, i, 0)), row, row, row],
            out_specs=[row, row, row, row]),
        out_shape=[_sds((r, c_))] * 4,
        compiler_params=_cparams(dimension_semantics=("arbitrary",)),
    )(dev, sent, recv, w, m, v)


def _cols_from_shards(w, width):
    ns, r, cs = w.shape
    tr = _pick(r, (256, 128))

    def body(w_ref, o_ref):
        parts = [w_ref[j] for j in range(ns)]
        if width > ns * cs:
            parts.append(jnp.zeros((tr, width - ns * cs), w.dtype))
        o_ref[...] = jnp.concatenate(parts, axis=1)

    return pl.pallas_call(
        body, name="cols_from_shards", grid=(r // tr,),
        in_specs=[pl.BlockSpec((ns, tr, cs), lambda i: (0, i, 0))], out_specs=pl.BlockSpec((tr, width), lambda i: (i, 0)),
        out_shape=jax.ShapeDtypeStruct((r, width), w.dtype), compiler_params=_cparams(dimension_semantics=("arbitrary",)),
    )(w)


def _shards_from_cols(g, cs):
    r, width = g.shape
    tr = _pick(r, (256, 128))

    def body(g_ref, o_ref):
        for j in range(N_DEV):
            o_ref[j] = g_ref[:, j * cs:(j + 1) * cs]

    return pl.pallas_call(
        body, name="shards_from_cols", grid=(r // tr,),
        in_specs=[pl.BlockSpec((tr, width), lambda i: (i, 0))], out_specs=pl.BlockSpec((N_DEV, tr, cs), lambda i: (0, i, 0)),
        out_shape=jax.ShapeDtypeStruct((N_DEV, r, cs), g.dtype), compiler_params=_cparams(dimension_semantics=("arbitrary",)),
    )(g)


def _adam(w, g, m, v):
    m = ADAM_B1 * m + (1.0 - ADAM_B1) * g
    v = ADAM_B2 * v + (1.0 - ADAM_B2) * (g * g)
    m_hat = m / (1.0 - ADAM_B1 ** ADAM_STEP)
    v_hat = v / (1.0 - ADAM_B2 ** ADAM_STEP)
    return -ADAM_LR * (m_hat / (jnp.sqrt(v_hat) + ADAM_EPS) + ADAM_WD * w), m, v


def _sum_devices(gathered):
    m_all, n = gathered.shape
    m_per = m_all // N_DEV

    def body(x_ref, o_ref):
        s = x_ref[0:m_per, :]
        for j in range(1, N_DEV):
            s = s + x_ref[j * m_per:(j + 1) * m_per, :]
        o_ref[...] = s

    return pl.pallas_call(body, name="sum_devices", out_shape=_sds((m_per, n)), compiler_params=_cparams())(gathered)


def _adam_small(w, g, m, v):
    def body(w_ref, g_ref, m_ref, v_ref, d_ref, mo_ref, vo_ref):
        d_ref[...], mo_ref[...], vo_ref[...] = _adam(w_ref[...], g_ref[...], m_ref[...], v_ref[...])

    return pl.pallas_call(body, name="adam_small", out_shape=[_sds(w.shape)] * 3, compiler_params=_cparams())(w, g, m, v)


N_LAYERS = 4
_MIXER = ("dn", "sb", "mla")
_MIXER_PARAMS = {
    "dn": ("dn_w_in", "dn_conv_w", "dn_a_log", "dn_dt_bias", "dn_out_norm", "dn_w_out"),
    "sb": ("sb_w_qkv", "sb_q_norm", "sb_k_norm", "sb_w_out"),
    "mla": ("mla_w_down", "mla_q_a_norm", "mla_kv_a_norm", "mla_w_uq", "mla_w_ukv", "mla_q_nope_norm", "mla_q_rope_norm",
            "mla_k_nope_norm", "mla_k_rope_norm", "mla_w_out"),
}
_BIG_AXIS = {"dn_w_in": 1, "dn_w_out": 0, "sb_w_qkv": 1, "sb_w_out": 0, "mla_w_down": 0, "mla_w_uq": 1, "mla_w_ukv": 1,
             "mla_w_out": 0, "ffn_w_gate_up": 1, "ffn_w_down": 0}


def _weight_names():
    names = []
    for i in range(N_LAYERS):
        p = "l%d_" % i
        names += [p + "mix_norm"] + [p + n for n in _MIXER_PARAMS[_MIXER[i % 3]]] + [p + "ffn_norm", p + "ffn_w_gate_up", p + "ffn_w_down"]
    return names


WEIGHTS = _weight_names()
BIG = [n for n in WEIGHTS if n[3:] in _BIG_AXIS]
SMALL = [n for n in WEIGHTS if n[3:] not in _BIG_AXIS]
CONV = [n for n in SMALL if n.endswith("conv_w")]


def _ceil_to(n, k):
    return -(-n // k) * k


def _pack(arrs, cols, row_mult):
    parts = []
    for a in arrs:
        f = a.reshape(-1)
        parts.append(jnp.pad(f, (0, _ceil_to(f.shape[0], cols) - f.shape[0])))
    flat = jnp.concatenate(parts)
    rows = _ceil_to(flat.shape[0] // cols, row_mult)
    return jnp.pad(flat, (0, rows * cols - flat.shape[0])).reshape(rows, cols)


def _unpack(buf, shapes):
    cols = buf.shape[-1]
    out, r0 = [], 0
    for s in shapes:
        n = math.prod(s)
        nr = _ceil_to(n, cols) // cols
        out.append(buf[r0:r0 + nr].reshape(-1)[:n].reshape(s))
        r0 += nr
    return out


def _layer_groups(i):
    by = {"gu": (704, []), "down": (1024, []), "out": (1024, []), "dn_in": (514, []), "sb_qkv": (384, []), "mla": (512, [])}
    key = {"ffn_w_gate_up": "gu", "ffn_w_down": "down", "dn_w_in": "dn_in", "sb_w_qkv": "sb_qkv", "mla_w_down": "mla",
           "mla_w_uq": "mla", "mla_w_ukv": "mla"}
    for n in BIG:
        if n.startswith("l%d_" % i):
            by[key.get(n[3:], "out")][1].append(n)
    return [g for g in by.values() if g[1]]


LAYER_GROUPS = [_layer_groups(i) for i in range(N_LAYERS)]
N_FFN_GROUPS = 2


def _stack_group(grp, get):
    width, names = grp
    if len(names) == 1 and get(names[0]).shape[1] == width:
        return get(names[0])
    return jnp.concatenate([jnp.pad(get(n), ((0, 0), (0, width - get(n).shape[1]))) for n in names], axis=0)


def _unstack_group(grp, buf, shape_of):
    if len(grp[1]) == 1 and tuple(shape_of(grp[1][0])) == tuple(buf.shape[-2:]):
        return [buf]
    out, r0 = [], 0
    for n in grp[1]:
        rs, cs = shape_of(n)
        out.append(buf[..., r0:r0 + rs, :cs])
        r0 += rs
    return out


def kernel(x, l0_mix_norm, l0_dn_w_in, l0_dn_conv_w, l0_dn_a_log, l0_dn_dt_bias, l0_dn_out_norm, l0_dn_w_out, l0_ffn_norm, l0_ffn_w_gate_up, l0_ffn_w_down, l1_mix_norm, l1_sb_w_qkv, l1_sb_q_norm, l1_sb_k_norm, l1_sb_w_out, l1_ffn_norm, l1_ffn_w_gate_up, l1_ffn_w_down, l2_mix_norm, l2_mla_w_down, l2_mla_q_a_norm, l2_mla_kv_a_norm, l2_mla_w_uq, l2_mla_w_ukv, l2_mla_q_nope_norm, l2_mla_q_rope_norm, l2_mla_k_nope_norm, l2_mla_k_rope_norm, l2_mla_w_out, l2_ffn_norm, l2_ffn_w_gate_up, l2_ffn_w_down, l3_mix_norm, l3_dn_w_in, l3_dn_conv_w, l3_dn_a_log, l3_dn_dt_bias, l3_dn_out_norm, l3_dn_w_out, l3_ffn_norm, l3_ffn_w_gate_up, l3_ffn_w_down, loss_target, m_l0_mix_norm, m_l0_dn_w_in, m_l0_dn_conv_w, m_l0_dn_a_log, m_l0_dn_dt_bias, m_l0_dn_out_norm, m_l0_dn_w_out, m_l0_ffn_norm, m_l0_ffn_w_gate_up, m_l0_ffn_w_down, m_l1_mix_norm, m_l1_sb_w_qkv, m_l1_sb_q_norm, m_l1_sb_k_norm, m_l1_sb_w_out, m_l1_ffn_norm, m_l1_ffn_w_gate_up, m_l1_ffn_w_down, m_l2_mix_norm, m_l2_mla_w_down, m_l2_mla_q_a_norm, m_l2_mla_kv_a_norm, m_l2_mla_w_uq, m_l2_mla_w_ukv, m_l2_mla_q_nope_norm, m_l2_mla_q_rope_norm, m_l2_mla_k_nope_norm, m_l2_mla_k_rope_norm, m_l2_mla_w_out, m_l2_ffn_norm, m_l2_ffn_w_gate_up, m_l2_ffn_w_down, m_l3_mix_norm, m_l3_dn_w_in, m_l3_dn_conv_w, m_l3_dn_a_log, m_l3_dn_dt_bias, m_l3_dn_out_norm, m_l3_dn_w_out, m_l3_ffn_norm, m_l3_ffn_w_gate_up, m_l3_ffn_w_down, v_l0_mix_norm, v_l0_dn_w_in, v_l0_dn_conv_w, v_l0_dn_a_log, v_l0_dn_dt_bias, v_l0_dn_out_norm, v_l0_dn_w_out, v_l0_ffn_norm, v_l0_ffn_w_gate_up, v_l0_ffn_w_down, v_l1_mix_norm, v_l1_sb_w_qkv, v_l1_sb_q_norm, v_l1_sb_k_norm, v_l1_sb_w_out, v_l1_ffn_norm, v_l1_ffn_w_gate_up, v_l1_ffn_w_down, v_l2_mix_norm, v_l2_mla_w_down, v_l2_mla_q_a_norm, v_l2_mla_kv_a_norm, v_l2_mla_w_uq, v_l2_mla_w_ukv, v_l2_mla_q_nope_norm, v_l2_mla_q_rope_norm, v_l2_mla_k_nope_norm, v_l2_mla_k_rope_norm, v_l2_mla_w_out, v_l2_ffn_norm, v_l2_ffn_w_gate_up, v_l2_ffn_w_down, v_l3_mix_norm, v_l3_dn_w_in, v_l3_dn_conv_w, v_l3_dn_a_log, v_l3_dn_dt_bias, v_l3_dn_out_norm, v_l3_dn_w_out, v_l3_ffn_norm, v_l3_ffn_w_gate_up, v_l3_ffn_w_down):
    a = dict(locals())
    return _train_step(a)


def _train_step(a):
    mx, my, mc = lax.axis_index("x"), lax.axis_index("y"), lax.axis_index("c")
    dev = 4 * mx + 2 * my + mc
    dev_arr = jnp.reshape(dev, (1,)).astype(jnp.int32)
    t, d = a["x"].shape[1], a["x"].shape[2]
    xs = a["x"].reshape(t, d)
    target = a["loss_target"].reshape(t, d)

    full = {}

    def unpack(groups, bufs):
        for grp, buf in zip(groups, bufs):
            for n, shards in zip(grp[1], _unstack_group(grp, buf, lambda n: a[n].shape)):
                kind = n[3:]
                if kind == "ffn_w_gate_up":
                    full[n] = shards
                elif _BIG_AXIS[kind] == 0:
                    full[n] = shards.reshape(N_DEV * shards.shape[1], shards.shape[2])
                else:
                    width = DN_PROJ if kind == "dn_w_in" else N_DEV * shards.shape[2]
                    full[n] = _cols_from_shards(shards, width)

    def local_shards(groups):
        return [_stack_group(grp, lambda n: a[n].astype(BF16)) for grp in groups]

    pushed_groups = [LAYER_GROUPS[0][:N_FFN_GROUPS]] + LAYER_GROUPS[1:]
    conv_pack = _pack([a[n] for n in CONV], LANES, 8)
    first_groups = LAYER_GROUPS[0][N_FFN_GROUPS:]
    *first_bufs, conv_all, gathered = _all_gather_groups("gather_weights", local_shards(first_groups) + [conv_pack])
    unpack(first_groups, first_bufs)
    for n, parts in zip(CONV, zip(*[_unpack(conv_all[j], [a[n].shape for n in CONV]) for j in range(N_DEV)])):
        full[n] = jnp.concatenate(parts, axis=1)
    after_first = gathered[0, 0].astype(BF16)
    gathers, started = {}, jnp.zeros((), F32)
    for i in range(N_LAYERS):
        srcs = local_shards(pushed_groups[i])
        srcs[-1] = srcs[-1] + after_first
        lands = [lax.dynamic_update_index_in_dim(lax.empty((N_DEV,) + s.shape, s.dtype), s, dev, 0) for s in srcs]
        gathers[i] = _push_start("gather_start_l%d" % i, srcs, lands, False)
        started = started + gathers[i][-1][0, 0]

    def vec(n):
        return a[n].reshape(1, -1)

    cos, sin = _rope_tables(t)

    def mixer_args(i):
        p = "l%d_" % i
        kind = _MIXER[i % 3]
        if kind == "dn":
            args = (full[p + "dn_w_in"], full[p + "dn_conv_w"], _pad_lanes(a[p + "dn_a_log"]), _pad_lanes(a[p + "dn_dt_bias"]),
                    vec(p + "dn_out_norm"), full[p + "dn_w_out"])
        elif kind == "sb":
            args = (full[p + "sb_w_qkv"], vec(p + "sb_q_norm"), vec(p + "sb_k_norm"), full[p + "sb_w_out"])
        else:
            w_down, w_uq, w_ukv = _mla_layout(full[p + "mla_w_down"], full[p + "mla_w_uq"], full[p + "mla_w_ukv"])
            args = (w_down, vec(p + "mla_q_a_norm"), vec(p + "mla_kv_a_norm"), w_uq, w_ukv, vec(p + "mla_q_nope_norm"),
                    _pad_lanes(a[p + "mla_q_rope_norm"]), vec(p + "mla_k_nope_norm"), _pad_lanes(a[p + "mla_k_rope_norm"]),
                    full[p + "mla_w_out"], cos, sin)
        return kind, args

    fwd = {"dn": _dn_fwd, "sb": _sb_fwd, "mla": _mla_fwd}
    bwd = {"dn": _dn_bwd, "sb": _sb_bwd, "mla": _mla_bwd}
    saved, layer_args = [], []
    for i in range(N_LAYERS):
        p = "l%d_" % i
        if i > 0:
            unpack(pushed_groups[i], _push_wait("gather_wait_l%d" % i, gathers[i], xs, False)[1])
        kind, args = mixer_args(i)
        layer_args.append((kind, args))
        gain = vec(p + "mix_norm") + started if i == 0 else vec(p + "mix_norm")
        h = _rmsnorm_fwd("mix_norm", xs, gain)
        x_mid, sv_mix = fwd[kind](xs, h, *args)
        if i == 0:
            unpack(pushed_groups[0], _push_wait("gather_wait_l0", gathers[0], x_mid, False)[1])
        x_out, sv_ffn = _ffn_fwd(x_mid, vec(p + "ffn_norm"), full[p + "ffn_w_gate_up"], full[p + "ffn_w_down"])
        saved.append((xs, sv_mix, sv_ffn))
        xs = x_out
    dy, loss_part = _loss_head(xs, target)

    grads, big_out = {}, {}

    def grad_shards(n):
        g, (rs, cs) = grads[n], a[n].shape
        if g.ndim == 3:
            return g
        if _BIG_AXIS[n[3:]] == 0:
            return g.reshape(N_DEV, rs, cs)
        return _shards_from_cols(g, cs)

    def push_grads(tag, groups):
        sends = []
        for grp in groups:
            if len(grp[1]) == 1 and a[grp[1][0]].shape[1] == grp[0]:
                sends.append(grad_shards(grp[1][0]))
                continue
            parts = [jnp.pad(grad_shards(n), ((0, 0), (0, 0), (0, grp[0] - a[n].shape[1]))) for n in grp[1]]
            sends.append(jnp.concatenate(parts, axis=1))
        lands = [lax.empty(s.shape, s.dtype) for s in sends]
        return tag, groups, _push_start("grads_start_" + tag, sends, lands, True)

    def finish_grads(push, after):
        tag, groups, pushed = push
        sents, recvs = _push_wait("grads_wait_" + tag, pushed, after, True)
        for grp, sent, recv in zip(groups, sents, recvs):
            packs = [_stack_group(grp, lambda n, pre=pre: a[pre + n]) for pre in ("", "m_", "v_")]
            outs = [_unstack_group(grp, o, lambda n: a[n].shape) for o in _sum_adam_devices(sent, recv, dev_arr, *packs)]
            for j, n in enumerate(grp[1]):
                big_out[n] = [o[j] for o in outs]

    mixer_push = None
    for i in reversed(range(N_LAYERS)):
        p = "l%d_" % i
        kind, args = layer_args[i]
        x_in, sv_mix, sv_ffn = saved[i]
        gain = vec(p + "ffn_norm") if mixer_push is None else vec(p + "ffn_norm") + mixer_push[2][-1][0, 0]
        dx_mid, grads[p + "ffn_norm"], grads[p + "ffn_w_gate_up"], grads[p + "ffn_w_down"] = _ffn_bwd(
            sv_ffn, dy, gain, full[p + "ffn_w_gate_up"], full[p + "ffn_w_down"])
        ffn_push = push_grads("l%d_ffn" % i, LAYER_GROUPS[i][:N_FFN_GROUPS])
        if mixer_push is not None:
            finish_grads(mixer_push, dx_mid)
        res = bwd[kind](sv_mix, dx_mid, *args)
        dh = res[0]
        if kind == "mla":
            res = list(res)
            res[1], res[4], res[5] = _mla_unlayout(res[1], res[4], res[5])
        for n, g in zip(_MIXER_PARAMS[kind], res[1:]):
            grads[p + n] = g
        dy, grads[p + "mix_norm"] = _rmsnorm_bwd("mix_norm_bwd", x_in, vec(p + "mix_norm") + ffn_push[2][-1][0, 0], dh, dx_mid)
        mixer_push = push_grads("l%d_mix" % i, LAYER_GROUPS[i][N_FFN_GROUPS:])
        finish_grads(ffn_push, dy)
    grad_x = dy.reshape(a["x"].shape)

    small_full_shapes = [full[n].shape if n in CONV else a[n].shape for n in SMALL]
    small_grads = []
    for n, s in zip(SMALL, small_full_shapes):
        g = grads[n].reshape(-1)
        small_grads.append(g[:math.prod(s)])
    small_pack = _pack(small_grads + [loss_part.reshape(-1)], LANES, 8)
    small_sum = _sum_devices(_all_gather("gather_small_grads", small_pack))
    small_red = _unpack(small_sum, small_full_shapes + [(LANES,)])
    loss = small_red[-1][0]
    g_small = {}
    for n, g in zip(SMALL, small_red[:-1]):
        if n in CONV:
            cs = a[n].shape[1]
            g = lax.dynamic_slice_in_dim(g, dev * cs, cs, axis=1)
        g_small[n] = g
    small_shapes = [a[n].shape for n in SMALL]
    packs = [_pack([src[n] for n in SMALL], LANES, 8) for src in
             ({n: a[n] for n in SMALL}, g_small, {n: a["m_" + n] for n in SMALL}, {n: a["v_" + n] for n in SMALL})]
    d_small, m_small, v_small = (_unpack(o, small_shapes) for o in _adam_small(*packs))
    finish_grads(mixer_push, small_sum)

    small_out = dict(zip(SMALL, zip([g_small[n] for n in SMALL], d_small, m_small, v_small)))

    def out(k):
        return [small_out[n][k] if n in small_out else big_out[n][k] for n in WEIGHTS]

    return (loss, grad_x, *out(0), *out(1), *out(2), *out(3))
```

```python
import math

import jax
import jax.numpy as jnp
from jax import lax
from jax.experimental import pallas as pl
from jax.experimental.pallas import tpu as pltpu

F32 = jnp.float32
BF16 = jnp.bfloat16
GRAD_DTYPE = BF16
F32X3 = lax.Precision.HIGH

LANES = 128
N_DEV = 8
N_HEADS = 8
HEAD = 128
NORM_EPS = 1e-6
DN_CHUNK = 64
DN_STEP = 2 * DN_CHUNK
ATT_BLOCK = 512
ATT_Q = 512
MLA_ROPE = 64
MLA_QK = 192
ROPE_THETA = 10000.0
VMEM_LIMIT = 56 * 1024 * 1024
MM_TILE_BYTES = 32 * 1024 * 1024

ADAM_LR = 0.001
ADAM_B1 = 0.9
ADAM_B2 = 0.999
ADAM_EPS = 1e-08
ADAM_WD = 0.01
ADAM_STEP = 10


def _cparams(**kw):
    return pltpu.CompilerParams(vmem_limit_bytes=VMEM_LIMIT, **kw)


def _pick(n, cands):
    for c in cands:
        if c <= n and n % c == 0:
            return c
    return n


def _mm(name, a, b, *, ta=False, tb=False, out_dtype=F32, add=None, tm=None, tn=None, tk=None):
    if ta:
        K, M = a.shape
    else:
        M, K = a.shape
    N = b.shape[0] if tb else b.shape[1]
    tm = tm or _pick(M, (1024, 512, 256, 128))
    tn = tn or _pick(N, (1024, 512, 384, 256, 128))
    if tk is None:
        fits = [c for c in (4096, 2048, 1408, 1024, 512, 384, 256, 128)
                if c <= K and K % c == 0 and 2 * c * (tm * a.dtype.itemsize + tn * b.dtype.itemsize) <= MM_TILE_BYTES]
        tk = fits[0] if fits else K
    return _mm_raw(
        name, a, b, ta=ta, tb=tb, out_dtype=out_dtype, add=add, grid=(M // tm, N // tn, K // tk), out_shape=(M, N),
        a_block=(tk, tm) if ta else (tm, tk), a_map=(lambda i, j, k: (k, i)) if ta else (lambda i, j, k: (i, k)),
        b_block=(tn, tk) if tb else (tk, tn), b_map=(lambda i, j, k: (j, k)) if tb else (lambda i, j, k: (k, j)),
        o_block=(tm, tn), o_map=lambda i, j, k: (i, j))


def _mm_raw(name, a, b, *, ta, tb, out_dtype, add, grid, out_shape, a_block, a_map, b_block, b_map, o_block, o_map):
    nk = grid[2]
    tm, tn = o_block
    dn = (((0 if ta else 1,), (1 if tb else 0,)), ((), ()))
    has_add = add is not None

    def kern(*refs):
        if has_add:
            a_ref, b_ref, add_ref, o_ref, acc_ref = refs
        else:
            a_ref, b_ref, o_ref, acc_ref = refs
        k = pl.program_id(2)
        part = lax.dot_general(a_ref[...].astype(BF16), b_ref[...].astype(BF16), dn, preferred_element_type=F32)

        @pl.when(k == 0)
        def _():
            acc_ref[...] = part

        @pl.when(k > 0)
        def _():
            acc_ref[...] += part

        @pl.when(k == nk - 1)
        def _():
            r = acc_ref[...]
            if has_add:
                r = r + add_ref[...]
            o_ref[...] = r.astype(out_dtype)

    in_specs = [pl.BlockSpec(a_block, a_map), pl.BlockSpec(b_block, b_map)]
    args = [a, b]
    if has_add:
        in_specs.append(pl.BlockSpec(o_block, o_map))
        args.append(add)
    return pl.pallas_call(
        kern, name=name,
        grid=grid,
        in_specs=in_specs,
        out_specs=pl.BlockSpec(o_block, o_map),
        out_shape=jax.ShapeDtypeStruct(out_shape, out_dtype),
        scratch_shapes=[pltpu.VMEM((tm, tn), F32)],
        compiler_params=_cparams(dimension_semantics=("parallel", "parallel", "arbitrary")),
    )(*args)


def _rows(name, body, ins, outs, *, tt, consts=(), accs=()):
    in_specs, args = [], []
    first = ins[0][0] if isinstance(ins[0], tuple) else ins[0]
    t = first.shape[-2]
    tt = min(tt, t)
    for x in ins:
        if isinstance(x, tuple):
            arr, bs, im = x
            in_specs.append(pl.BlockSpec(bs, im))
            args.append(arr)
        else:
            in_specs.append(_row_spec(x.shape, tt))
            args.append(x)
    for c in consts:
        in_specs.append(pl.BlockSpec(c.shape, lambda i, _n=c.ndim: (0,) * _n))
        args.append(c)
    out_specs = [_row_spec(o.shape, tt) for o in outs]
    out_specs += [pl.BlockSpec(a.shape, lambda i, _n=len(a.shape): (0,) * _n) for a in accs]
    res = pl.pallas_call(
        body, name=name, grid=(t // tt,),
        in_specs=in_specs, out_specs=out_specs, out_shape=list(outs) + list(accs),
        compiler_params=_cparams(dimension_semantics=("arbitrary",)),
    )(*args)
    return res


def _row_spec(shape, tt):
    if len(shape) == 2:
        return pl.BlockSpec((tt, shape[1]), lambda i: (i, 0))
    return pl.BlockSpec((shape[0], tt, shape[2]), lambda i: (0, i, 0))


def _sds(shape, dtype=F32):
    return jax.ShapeDtypeStruct(tuple(shape), dtype)


def _acc(ref, val):
    i = pl.program_id(0)

    @pl.when(i == 0)
    def _():
        ref[...] = val

    @pl.when(i > 0)
    def _():
        ref[...] += val


def _rms(x, g):
    return x * lax.rsqrt(jnp.mean(x * x, axis=-1, keepdims=True) + NORM_EPS) * g


def _silu(x):
    return x / (1.0 + jnp.exp(-x))


def _softplus(x):
    return jnp.maximum(x, 0.0) + jnp.log(1.0 + jnp.exp(-jnp.abs(x)))


def _sigmoid(x):
    return 1.0 / (1.0 + jnp.exp(-x))


def _rmsnorm_fwd(name, x, g, tt=512):
    def body(x_ref, g_ref, h_ref):
        h_ref[...] = _rms(x_ref[...], g_ref[...]).astype(BF16)

    return _rows(name, body, [x], [_sds(x.shape, BF16)], tt=tt, consts=[g])[0]


def _rmsnorm_bwd(name, x, g, dh, dres, tt=512):
    def body(x_ref, dh_ref, dres_ref, g_ref, dx_ref, dg_ref):
        _, vjp = jax.vjp(_rms, x_ref[...], g_ref[...])
        dx, dg = vjp(dh_ref[...])
        dx_ref[...] = dx + dres_ref[...]
        _acc(dg_ref, dg)

    return _rows(name, body, [x, dh, dres], [_sds(x.shape)], tt=tt, consts=[g], accs=[_sds(g.shape)])


def _ffn_fwd(x, norm_g, w3, down_weight):
    t, d = x.shape
    ns, _, cs = w3.shape
    half = ns // 2
    w2 = w3.reshape(ns * d, cs)
    h = _rmsnorm_fwd("ffn_norm", x, norm_g)
    tm = _pick(t, (1024, 512, 256, 128))
    nm = t // tm

    def gate_up(h_ref, wg_ref, wu_ref, g_ref, u_ref, a_ref):
        hv = h_ref[...]
        g = _dot(hv, wg_ref[...])
        u = _dot(hv, wu_ref[...])
        g_ref[...] = g
        u_ref[...] = u
        a_ref[...] = (_silu(g) * u).astype(BF16)

    hid = pl.BlockSpec((tm, cs), lambda j, i: (j * nm + i, 0))
    g, u, act = pl.pallas_call(
        gate_up, name="ffn_gate_up", grid=(half, nm),
        in_specs=[pl.BlockSpec((tm, d), lambda j, i: (i, 0)), pl.BlockSpec((d, cs), lambda j, i: (j, 0)),
                  pl.BlockSpec((d, cs), lambda j, i: (j + half, 0))],
        out_specs=[hid, hid, hid], out_shape=[_sds((half * t, cs)), _sds((half * t, cs)), _sds((half * t, cs), BF16)],
        compiler_params=_cparams(dimension_semantics=("parallel", "arbitrary")),
    )(h, w2, w2)
    w_down = down_weight(act)

    def down(a_ref, w_ref, x_ref, y_ref):
        y = x_ref[...]
        for j in range(half):
            y = y + _dot(a_ref[j], w_ref[j])
        y_ref[...] = y

    y = pl.pallas_call(
        down, name="ffn_down", grid=(nm,),
        in_specs=[pl.BlockSpec((half, tm, cs), lambda i: (0, i, 0)), pl.BlockSpec((half, cs, d), lambda i: (0, 0, 0)),
                  pl.BlockSpec((tm, d), lambda i: (i, 0))],
        out_specs=pl.BlockSpec((tm, d), lambda i: (i, 0)), out_shape=_sds((t, d)),
        compiler_params=_cparams(dimension_semantics=("parallel",)),
    )(act.reshape(half, t, cs), w_down.reshape(half, cs, d), x)
    return y, (x, h, g, u, act)


def _ffn_bwd(saved, dy, norm_g, w3, w_down):
    x, h, g, u, act = saved
    t, d = x.shape
    ns, _, cs = w3.shape
    half = ns // 2
    w2 = w3.reshape(ns * d, cs)
    tm = _pick(t, (1024, 512, 256, 128))
    nm = t // tm
    tk = _pick(t, (4096, 2048, 1024, 512, 256, 128))
    nk = t // tk
    d_wdown = _mm_raw("ffn_down_wgrad", act, dy, ta=True, tb=False, out_dtype=GRAD_DTYPE, add=None, grid=(half, 1, nk),
                      out_shape=(half * cs, d), a_block=(tk, cs), a_map=lambda i, j, k: (i * nk + k, 0),
                      b_block=(tk, d), b_map=lambda i, j, k: (k, 0), o_block=(cs, d), o_map=lambda i, j, k: (i, 0))
    def down_dgrad(dy_ref, wd_ref, g_ref, u_ref, dg_ref, du_ref):
        da = _dot_nt(dy_ref[...].astype(BF16), wd_ref[...])
        gv, uv = g_ref[...], u_ref[...]
        s = _sigmoid(gv)
        dg_ref[...] = (da * uv * s * (1.0 + gv * (1.0 - s))).astype(BF16)
        du_ref[...] = (da * gv * s).astype(BF16)

    hid = pl.BlockSpec((tm, cs), lambda j, i: (j * nm + i, 0))
    dg, du = pl.pallas_call(
        down_dgrad, name="ffn_down_dgrad", grid=(half, nm),
        in_specs=[pl.BlockSpec((tm, d), lambda j, i: (i, 0)), pl.BlockSpec((cs, d), lambda j, i: (j, 0)), hid, hid],
        out_specs=[hid, hid], out_shape=[_sds((half * t, cs), BF16), _sds((half * t, cs), BF16)],
        compiler_params=_cparams(dimension_semantics=("parallel", "arbitrary")),
    )(dy, w_down, g, u)

    def wgrad(name, dd):
        return _mm_raw(name, h, dd, ta=True, tb=False, out_dtype=GRAD_DTYPE, add=None, grid=(1, half, nk), out_shape=(half * d, cs),
                       a_block=(tk, d), a_map=lambda i, j, k: (k, 0), b_block=(tk, cs), b_map=lambda i, j, k: (j * nk + k, 0),
                       o_block=(d, cs), o_map=lambda i, j, k: (j, 0))

    def gate_up_dgrad(dg_ref, du_ref, w_ref, dh_ref):
        dh = _dot_nt(dg_ref[0], w_ref[0]) + _dot_nt(du_ref[0], w_ref[half])
        for j in range(1, half):
            dh = dh + _dot_nt(dg_ref[j], w_ref[j]) + _dot_nt(du_ref[j], w_ref[half + j])
        dh_ref[...] = dh

    th = _pick(t, (512, 256, 128))
    hid3 = pl.BlockSpec((half, th, cs), lambda i: (0, i, 0))
    dh = pl.pallas_call(
        gate_up_dgrad, name="ffn_gate_up_dgrad", grid=(t // th,),
        in_specs=[hid3, hid3, pl.BlockSpec((ns, d, cs), lambda i: (0, 0, 0))],
        out_specs=pl.BlockSpec((th, d), lambda i: (i, 0)), out_shape=_sds((t, d)),
        compiler_params=_cparams(dimension_semantics=("parallel",)),
    )(dg.reshape(half, t, cs), du.reshape(half, t, cs), w3)
    d_w3 = jnp.concatenate([wgrad("ffn_gate_wgrad", dg), wgrad("ffn_up_wgrad", du)], axis=0).reshape(ns, d, cs)
    dx, dgain = _rmsnorm_bwd("ffn_norm_bwd", x, norm_g, dh, dy)
    return dx, dgain, d_w3, d_wdown


def _dot_nt(a, b):
    return lax.dot_general(a, b, (((1,), (1,)), ((), ())), preferred_element_type=F32)


def _dot_tn(a, b):
    return lax.dot_general(a, b, (((0,), (0,)), ((), ())), preferred_element_type=F32)


def _dot(a, b):
    return jnp.dot(a, b, preferred_element_type=F32)


CUM_BLOCK = 128


def _tri2(lower):
    r = lax.broadcasted_iota(jnp.int32, (CUM_BLOCK, CUM_BLOCK), 0)
    c = lax.broadcasted_iota(jnp.int32, (CUM_BLOCK, CUM_BLOCK), 1)
    tri = ((r > c) if lower else (r < c)).astype(BF16)
    return jnp.concatenate([tri, tri], axis=0)


def _run_sums(x, tri2, run, reverse):
    nb = x.shape[1] // CUM_BLOCK
    outs = [None] * nb
    for j in (reversed(range(nb)) if reverse else range(nb)):
        xj = x[:, j * CUM_BLOCK:(j + 1) * CUM_BLOCK]
        hi = xj.astype(BF16)
        lo = (xj - hi.astype(F32)).astype(BF16)
        outs[j] = _dot(jnp.concatenate([hi, lo], axis=1), tri2) + run
        run = run + jnp.sum(xj, axis=1, keepdims=True)
    return jnp.concatenate(outs, axis=1), run


def _log_sigmoid(z):
    return jnp.minimum(z, 0.0) - jnp.log(1.0 + jnp.exp(-jnp.abs(z)))


def _heads_in(ref, h, width=HEAD):
    return ref[:, h * width:(h + 1) * width]


def _sb_qk(q, k, gq, gk):
    return _rms(q, gq) * (HEAD ** -0.5), _rms(k, gk)


def _sb_prep_fwd(qkv, gq, gk):
    t = qkv.shape[0]

    def body(x_ref, gq_ref, gk_ref, q_ref, k_ref, v_ref):
        for h in range(N_HEADS):
            q, k = _sb_qk(_heads_in(x_ref, h), _heads_in(x_ref, N_HEADS + h), gq_ref[...], gk_ref[...])
            q_ref[h] = q.astype(BF16)
            k_ref[h] = k.astype(BF16)
            v_ref[h] = _heads_in(x_ref, 2 * N_HEADS + h).astype(BF16)

    hm = _sds((N_HEADS, t, HEAD), BF16)
    return _rows("sb_prep", body, [qkv], [hm, hm, hm], tt=256, consts=[gq, gk])


def _sb_prep_bwd(qkv, gq, gk, dq, dk, dv):
    def body(x_ref, dq_ref, dk_ref, dv_ref, gq_ref, gk_ref, dx_ref, dgq_ref, dgk_ref):
        dgq = jnp.zeros(gq_ref.shape, F32)
        dgk = jnp.zeros(gk_ref.shape, F32)
        for h in range(N_HEADS):
            _, vjp = jax.vjp(_sb_qk, _heads_in(x_ref, h), _heads_in(x_ref, N_HEADS + h), gq_ref[...], gk_ref[...])
            a, b, c, d = vjp((dq_ref[h], dk_ref[h]))
            dx_ref[:, h * HEAD:(h + 1) * HEAD] = a.astype(BF16)
            dx_ref[:, (N_HEADS + h) * HEAD:(N_HEADS + h + 1) * HEAD] = b.astype(BF16)
            dx_ref[:, (2 * N_HEADS + h) * HEAD:(2 * N_HEADS + h + 1) * HEAD] = dv_ref[h].astype(BF16)
            dgq, dgk = dgq + c, dgk + d
        _acc(dgq_ref, dgq)
        _acc(dgk_ref, dgk)

    return _rows("sb_prep_bwd", body, [qkv, dq, dk, dv], [_sds(qkv.shape, BF16)], tt=256, consts=[gq, gk],
                 accs=[_sds(gq.shape), _sds(gk.shape)])


def _q_block(t):
    return min(ATT_Q, t)


def _key_order(bq, qb):
    rows = lax.broadcasted_iota(jnp.int32, (bq, ATT_BLOCK), 0)
    cols = lax.broadcasted_iota(jnp.int32, (bq, ATT_BLOCK), 1)
    return rows - cols + qb * bq


def _sb_attn_fwd(q, k, v):
    nh, t, _ = q.shape
    bq = _q_block(t)
    per = bq // ATT_BLOCK

    def kern(q_ref, k_ref, v_ref, o_ref):
        qb = pl.program_id(1)
        qv = q_ref[0]
        after = _tri2(True)
        order = _key_order(bq, qb)
        nkb = (qb + 1) * per

        def body(i, carry, diagonal):
            o_acc, run = carry
            kb = nkb - 1 - i
            off = pl.multiple_of(kb * ATT_BLOCK, ATT_BLOCK)
            kv = k_ref[0, pl.ds(off, ATT_BLOCK), :]
            vv = v_ref[0, pl.ds(off, ATT_BLOCK), :]
            z = _dot_nt(qv, kv)
            lsz = _log_sigmoid(z)
            lsn = lsz - z
            if diagonal:
                past = order > kb * ATT_BLOCK
                lsn = jnp.where(past, lsn, 0.0)
            la, run = _run_sums(lsn, after, run, True)
            a = jnp.exp(lsz + la)
            if diagonal:
                a = jnp.where(past, a, 0.0)
            o_acc = o_acc + _dot(a.astype(BF16), vv)
            return o_acc, run

        carry = lax.fori_loop(0, per, lambda i, c: body(i, c, True), (jnp.zeros((bq, HEAD), F32), jnp.zeros((bq, 1), F32)))
        o, _ = lax.fori_loop(per, nkb, lambda i, c: body(i, c, False), carry)
        o_ref[...] = o

    return pl.pallas_call(
        kern, name="sb_attn_fwd", grid=(nh, t // bq),
        in_specs=[pl.BlockSpec((1, bq, HEAD), lambda h, i: (h, i, 0)),
                  pl.BlockSpec((1, t, HEAD), lambda h, i: (h, 0, 0)),
                  pl.BlockSpec((1, t, HEAD), lambda h, i: (h, 0, 0))],
        out_specs=pl.BlockSpec((bq, HEAD), lambda h, i: (i, h)),
        out_shape=_sds((t, nh * HEAD)),
        compiler_params=_cparams(dimension_semantics=("parallel", "arbitrary")),
    )(q, k, v)


def _sb_attn_bwd(q, k, v, do):
    nh, t, _ = q.shape
    bq = _q_block(t)
    per = bq // ATT_BLOCK

    def kern(q_ref, k_ref, v_ref, do_ref, dq_ref, dk_ref, dv_ref, g_s, ls_s):
        qb = pl.program_id(1)

        @pl.when(qb == 0)
        def _():
            dk_ref[...] = jnp.zeros(dk_ref.shape, F32)
            dv_ref[...] = jnp.zeros(dv_ref.shape, F32)

        qv = q_ref[0]
        dob = do_ref[...].astype(BF16)
        after, before = _tri2(True), _tri2(False)
        order = _key_order(bq, qb)
        nkb = (qb + 1) * per

        def sweep_left(i, run, diagonal):
            kb = nkb - 1 - i
            off = pl.multiple_of(kb * ATT_BLOCK, ATT_BLOCK)
            kv = k_ref[0, pl.ds(off, ATT_BLOCK), :]
            vv = v_ref[0, pl.ds(off, ATT_BLOCK), :]
            z = _dot_nt(qv, kv)
            lsz = _log_sigmoid(z)
            lsn = lsz - z
            if diagonal:
                past = order > kb * ATT_BLOCK
                lsn = jnp.where(past, lsn, 0.0)
            la, run = _run_sums(lsn, after, run, True)
            a = jnp.exp(lsz + la)
            if diagonal:
                a = jnp.where(past, a, 0.0)
            g_s[kb] = _dot_nt(dob, vv) * a
            ls_s[kb] = lsz
            dv_ref[0, pl.ds(off, ATT_BLOCK), :] += _dot_tn(a.astype(BF16), dob)
            return run

        zero = jnp.zeros((bq, 1), F32)
        run = lax.fori_loop(0, per, lambda i, c: sweep_left(i, c, True), zero)
        lax.fori_loop(per, nkb, lambda i, c: sweep_left(i, c, False), run)

        def sweep_right(kb, carry, diagonal):
            dq_acc, run_g = carry
            off = pl.multiple_of(kb * ATT_BLOCK, ATT_BLOCK)
            kv = k_ref[0, pl.ds(off, ATT_BLOCK), :]
            g = g_s[kb]
            sg = jnp.exp(ls_s[kb])
            dls, run_g = _run_sums(g, before, run_g, False)
            dz = g * (1.0 - sg) - dls * sg
            if diagonal:
                dz = jnp.where(order > kb * ATT_BLOCK, dz, 0.0)
            dzb = dz.astype(BF16)
            dk_ref[0, pl.ds(off, ATT_BLOCK), :] += _dot_tn(dzb, qv)
            return dq_acc + _dot(dzb, kv), run_g

        carry = lax.fori_loop(0, nkb - per, lambda i, c: sweep_right(i, c, False), (jnp.zeros((bq, HEAD), F32), zero))
        dq, _ = lax.fori_loop(nkb - per, nkb, lambda i, c: sweep_right(i, c, True), carry)
        dq_ref[0] = dq

    hm = _sds((nh, t, HEAD))
    full = pl.BlockSpec((1, t, HEAD), lambda h, i: (h, 0, 0))
    tok = pl.BlockSpec((bq, HEAD), lambda h, i: (i, h))
    nkb_max = t // ATT_BLOCK
    return pl.pallas_call(
        kern, name="sb_attn_bwd", grid=(nh, t // bq),
        in_specs=[pl.BlockSpec((1, bq, HEAD), lambda h, i: (h, i, 0)), full, full, tok],
        out_specs=[pl.BlockSpec((1, bq, HEAD), lambda h, i: (h, i, 0)), full, full],
        out_shape=[hm, hm, hm],
        scratch_shapes=[pltpu.VMEM((nkb_max, bq, ATT_BLOCK), F32), pltpu.VMEM((nkb_max, bq, ATT_BLOCK), F32)],
        compiler_params=_cparams(dimension_semantics=("parallel", "arbitrary")),
    )(q, k, v, do)


def _sb_fwd(x, h, w_qkv, gq, gk, w_out):
    qkv = _mm("sb_qkv", h, w_qkv)
    q, k, v = _sb_prep_fwd(qkv, gq, gk)
    o = _sb_attn_fwd(q, k, v)
    y = _mm("sb_out", o, w_out, add=x)
    return y, (h, qkv, q, k, v, o)


def _sb_bwd(saved, dy, w_qkv, gq, gk, w_out):
    h, qkv, q, k, v, o = saved
    d_wout = _mm("sb_out_wgrad", o, dy, ta=True, out_dtype=GRAD_DTYPE)
    do = _mm("sb_out_dgrad", dy, w_out, tb=True)
    dq, dk, dv = _sb_attn_bwd(q, k, v, do)
    dqkv, dgq, dgk = _sb_prep_bwd(qkv, gq, gk, dq, dk, dv)
    d_wqkv = _mm("sb_qkv_wgrad", h, dqkv, ta=True, out_dtype=GRAD_DTYPE)
    dh = _mm("sb_qkv_dgrad", dqkv, w_qkv, tb=True)
    return dh, d_wqkv, dgq, dgk, d_wout


DN_QKV = 3 * N_HEADS * HEAD
DN_PROJ = DN_QKV + N_HEADS * HEAD + LANES
DN_CONV = 4
HALO = 8
CONV_COLS = 512


def _dn_conv_fwd(proj, conv_w, tt=256):
    t = proj.shape[0]
    tt = min(tt, t)

    def body(u_ref, prev_ref, w_ref, c_ref):
        i = pl.program_id(0)
        for cc in range(DN_QKV // CONV_COLS):
            cs = slice(cc * CONV_COLS, (cc + 1) * CONV_COLS)
            cur = u_ref[:, cs]
            prev = jnp.where(i > 0, prev_ref[:, cs], 0.0)
            ext = jnp.concatenate([prev, cur], axis=0)
            y = cur * w_ref[DN_CONV - 1:DN_CONV, cs]
            for j in range(DN_CONV - 1):
                y = y + pltpu.roll(ext, DN_CONV - 1 - j, 0)[HALO:] * w_ref[j:j + 1, cs]
            c_ref[:, cs] = y

    return _rows("dn_conv", body,
                 [(proj, (tt, DN_QKV), lambda i: (i, 0)),
                  (proj, (HALO, DN_QKV), lambda i: (jnp.maximum(i * (tt // HALO) - 1, 0), 0))],
                 [_sds((t, DN_QKV))], tt=tt, consts=[conv_w])[0]


def _dn_conv_bwd(proj, conv_w, dc, dz, dab, tt=256):
    t = proj.shape[0]
    tt = min(tt, t)
    nblk = t // tt

    def body(u_ref, prev_ref, dc_ref, next_ref, dz_ref, dab_ref, w_ref, dp_ref, dw_ref):
        i = pl.program_id(0)
        dws = []
        for cc in range(DN_QKV // CONV_COLS):
            cs = slice(cc * CONV_COLS, (cc + 1) * CONV_COLS)
            cur = u_ref[:, cs]
            prev = jnp.where(i > 0, prev_ref[:, cs], 0.0)
            ext_u = jnp.concatenate([prev, cur], axis=0)
            d = dc_ref[:, cs]
            nxt = jnp.where(i < nblk - 1, next_ref[:, cs], 0.0)
            ext_d = jnp.concatenate([d, nxt], axis=0)
            du = d * w_ref[DN_CONV - 1:DN_CONV, cs]
            rows = [jnp.sum(d * cur, axis=0, keepdims=True)]
            for j in range(DN_CONV - 2, -1, -1):
                sh = DN_CONV - 1 - j
                du = du + pltpu.roll(ext_d, tt + HALO - sh, 0)[:tt] * w_ref[j:j + 1, cs]
                rows.insert(0, jnp.sum(d * pltpu.roll(ext_u, sh, 0)[HALO:], axis=0, keepdims=True))
            dp_ref[:, cs] = du.astype(BF16)
            dws.append(jnp.concatenate(rows, axis=0))
        dp_ref[:, DN_QKV:DN_QKV + N_HEADS * HEAD] = dz_ref[...].astype(BF16)
        dp_ref[:, DN_QKV + N_HEADS * HEAD:] = dab_ref[...].astype(BF16)
        _acc(dw_ref, jnp.concatenate(dws, axis=1))

    return _rows("dn_conv_bwd", body,
                 [(proj, (tt, DN_QKV), lambda i: (i, 0)),
                  (proj, (HALO, DN_QKV), lambda i: (jnp.maximum(i * (tt // HALO) - 1, 0), 0)),
                  dc,
                  (dc, (HALO, DN_QKV), lambda i: (jnp.minimum((i + 1) * (tt // HALO), t // HALO - 1), 0)),
                  dz, dab],
                 [_sds((t, DN_PROJ), BF16)], tt=tt, consts=[conv_w], accs=[_sds(conv_w.shape)])


def _l2n(x):
    return x * lax.rsqrt(jnp.sum(x * x, axis=-1, keepdims=True) + NORM_EPS)


def _dn_qkv(cq, ck, cv):
    return _l2n(_silu(cq)) * (HEAD ** -0.5), _l2n(_silu(ck)), _silu(cv)


def _dn_gates(ab, a_log, dt_bias):
    lane = lax.broadcasted_iota(jnp.int32, ab.shape, 1)
    g = -jnp.exp(a_log) * _softplus(ab + dt_bias)
    return jnp.where(lane < N_HEADS, g, jnp.where(lane < 2 * N_HEADS, _sigmoid(ab), 0.0))


def _ab_spec(tt):
    return (tt, LANES), lambda i: (i, DN_PROJ // LANES - 1)


def _dn_prep_fwd(c, proj, a_log, dt_bias, tt=256):
    t = c.shape[0]
    tt = min(tt, t)

    def body(c_ref, ab_ref, al_ref, dt_ref, q_ref, k_ref, v_ref, g_ref):
        for h in range(N_HEADS):
            q_ref[h], k_ref[h], v_ref[h] = _dn_qkv(_heads_in(c_ref, h), _heads_in(c_ref, N_HEADS + h), _heads_in(c_ref, 2 * N_HEADS + h))
        g_ref[...] = _dn_gates(ab_ref[...], al_ref[...], dt_ref[...])

    hm = _sds((N_HEADS, t, HEAD))
    return _rows("dn_prep", body, [c, (proj,) + _ab_spec(tt)], [hm, hm, hm, _sds((t, LANES))], tt=tt, consts=[a_log, dt_bias])


def _dn_prep_bwd(c, proj, a_log, dt_bias, dq, dk, dv, dgates, tt=256):
    t = c.shape[0]
    tt = min(tt, t)

    def body(c_ref, ab_ref, dq_ref, dk_ref, dv_ref, dg_ref, al_ref, dt_ref, dc_ref, dab_ref, dal_ref, ddt_ref):
        for h in range(N_HEADS):
            _, vjp = jax.vjp(_dn_qkv, _heads_in(c_ref, h), _heads_in(c_ref, N_HEADS + h), _heads_in(c_ref, 2 * N_HEADS + h))
            a, b, d = vjp((dq_ref[h], dk_ref[h], dv_ref[h]))
            dc_ref[:, h * HEAD:(h + 1) * HEAD] = a
            dc_ref[:, (N_HEADS + h) * HEAD:(N_HEADS + h + 1) * HEAD] = b
            dc_ref[:, (2 * N_HEADS + h) * HEAD:(2 * N_HEADS + h + 1) * HEAD] = d
        _, vjp = jax.vjp(_dn_gates, ab_ref[...], al_ref[...], dt_ref[...])
        dab, dal, ddt = vjp(dg_ref[...])
        dab_ref[...] = dab
        _acc(dal_ref, dal)
        _acc(ddt_ref, ddt)

    return _rows("dn_prep_bwd", body, [c, (proj,) + _ab_spec(tt), dq, dk, dv, dgates], [_sds(c.shape), _sds((t, LANES))],
                 tt=tt, consts=[a_log, dt_bias], accs=[_sds(a_log.shape), _sds(dt_bias.shape)])


def _bdot(a, b, prec=None):
    return lax.dot_general(a, b, (((2,), (1,)), ((0,), (0,))), precision=prec, preferred_element_type=F32)


def _bdot_nt(a, b, prec=None):
    return lax.dot_general(a, b, (((2,), (2,)), ((0,), (0,))), precision=prec, preferred_element_type=F32)


def _bdot_tn(a, b, prec=None):
    return lax.dot_general(a, b, (((1,), (1,)), ((0,), (0,))), precision=prec, preferred_element_type=F32)


def _inv_raw(low):
    c = low.shape[-1]
    r = lax.broadcasted_iota(jnp.int32, (c, c), 0)
    s = lax.broadcasted_iota(jnp.int32, (c, c), 1)
    m = jnp.where(r == s, 1.0, 0.0) - low
    p = _bdot(low, low, F32X3)
    n_fac = int(math.log2(c)) - 1
    for i in range(n_fac):
        m = m + _bdot(m, p, F32X3)
        if i < n_fac - 1:
            p = _bdot(p, p, F32X3)
    return m


@jax.custom_vjp
def _inv_unit_lower(low):
    return _inv_raw(low)


def _inv_fwd(low):
    m = _inv_raw(low)
    return m, m


def _inv_bwd(m, dm):
    return (-_bdot_nt(_bdot_tn(m, dm, F32X3), m, F32X3),)


_inv_unit_lower.defvjp(_inv_fwd, _inv_bwd)


def _dn_chunk(q, k, v, gates, s):
    nh, rows, _ = q.shape
    c = DN_CHUNK
    nc = rows // c
    nb = nh * nc
    lane = lax.broadcasted_iota(jnp.int32, gates.shape, 1)

    def column(j):
        return jnp.sum(jnp.where(lane == j, gates, 0.0), axis=1, keepdims=True)[None]

    def fold(x):
        return x.reshape((nb, c) + x.shape[2:])

    g_col = fold(jnp.concatenate([column(h) for h in range(nh)], axis=0))
    b_col = fold(jnp.concatenate([column(h + nh) for h in range(nh)], axis=0))
    q, k, v = fold(q), fold(k), fold(v)
    r = lax.broadcasted_iota(jnp.int32, (c, c), 0)
    cc = lax.broadcasted_iota(jnp.int32, (c, c), 1)
    causal, strict = r >= cc, r > cc
    incl = jnp.broadcast_to(jnp.where(causal, 1.0, 0.0), (nb, c, c))
    upper = jnp.broadcast_to(jnp.where(r <= cc, 1.0, 0.0), (nb, c, c))
    gb = jnp.broadcast_to(g_col, (nb, c, LANES))
    gbc = jnp.broadcast_to(g_col, (nb, c, c))
    gc = _bdot(incl, gb, F32X3)
    gc_r = _bdot(incl, gbc, F32X3)
    gc_c = _bdot_tn(gbc, upper, F32X3)
    decay = jnp.where(causal, jnp.exp(jnp.where(causal, gc_r - gc_c, 0.0)), 0.0)
    kb = k * b_col
    low = jnp.where(strict, _bdot_nt(kb, k) * decay, 0.0)
    m = _inv_unit_lower(low)
    egc = jnp.exp(gc)
    gl = jnp.sum(gb, axis=1, keepdims=True)
    local = (_bdot(m, v * b_col, F32X3), _bdot(m, kb * egc, F32X3), _bdot_nt(q, k) * decay, q * egc,
             k * jnp.exp(gl - gc), jnp.exp(gl))
    outs = []
    for i in range(nc):
        u, w, attn, q_dec, k_dec, cd = (x.reshape((nh, nc) + x.shape[1:])[:, i] for x in local)
        v_new = u - _bdot(w, s)
        outs.append(_bdot(q_dec, s) + _bdot(attn, v_new))
        s = s * cd + _bdot_tn(k_dec, v_new)
    return jnp.concatenate(outs, axis=1), s


def _dn_chunks_fwd(q, k, v, gates):
    nh, t, _ = q.shape
    n = t // DN_STEP

    def kern(q_ref, k_ref, v_ref, g_ref, o_ref, sin_ref, s_scr):
        @pl.when(pl.program_id(0) == 0)
        def _():
            s_scr[...] = jnp.zeros(s_scr.shape, F32)

        s = s_scr[...]
        sin_ref[0] = s
        o_ref[...], s_scr[...] = _dn_chunk(q_ref[...], k_ref[...], v_ref[...], g_ref[...], s)

    blk = pl.BlockSpec((nh, DN_STEP, HEAD), lambda i: (0, i, 0))
    return pl.pallas_call(
        kern, name="dn_chunks_fwd", grid=(n,),
        in_specs=[blk, blk, blk, pl.BlockSpec((DN_STEP, LANES), lambda i: (i, 0))],
        out_specs=[blk, pl.BlockSpec((1, nh, HEAD, HEAD), lambda i: (i, 0, 0, 0))],
        out_shape=[_sds((nh, t, HEAD)), _sds((n, nh, HEAD, HEAD))],
        scratch_shapes=[pltpu.VMEM((nh, HEAD, HEAD), F32)],
        compiler_params=_cparams(dimension_semantics=("arbitrary",)),
    )(q, k, v, gates)


def _dn_chunks_bwd(q, k, v, gates, s_in, do):
    nh, t, _ = q.shape
    n = t // DN_STEP

    def kern(q_ref, k_ref, v_ref, g_ref, sin_ref, do_ref, dq_ref, dk_ref, dv_ref, dg_ref, ds_scr):
        @pl.when(pl.program_id(0) == 0)
        def _():
            ds_scr[...] = jnp.zeros(ds_scr.shape, F32)

        _, vjp = jax.vjp(_dn_chunk, q_ref[...], k_ref[...], v_ref[...], g_ref[...], sin_ref[0])
        dq_ref[...], dk_ref[...], dv_ref[...], dg_ref[...], ds_scr[...] = vjp((do_ref[...], ds_scr[...]))

    blk = pl.BlockSpec((nh, DN_STEP, HEAD), lambda i: (0, n - 1 - i, 0))
    gblk = pl.BlockSpec((DN_STEP, LANES), lambda i: (n - 1 - i, 0))
    hm = _sds((nh, t, HEAD))
    return pl.pallas_call(
        kern, name="dn_chunks_bwd", grid=(n,),
        in_specs=[blk, blk, blk, gblk, pl.BlockSpec((1, nh, HEAD, HEAD), lambda i: (n - 1 - i, 0, 0, 0)), blk],
        out_specs=[blk, blk, blk, gblk],
        out_shape=[hm, hm, hm, _sds((t, LANES))],
        scratch_shapes=[pltpu.VMEM((nh, HEAD, HEAD), F32)],
        compiler_params=_cparams(dimension_semantics=("arbitrary",)),
    )(q, k, v, gates, s_in, do)


def _dn_gate_out(o, z, g):
    return _rms(o, g) * _silu(z)


def _z_spec(tt):
    return (tt, N_HEADS * HEAD), lambda i: (i, DN_QKV // (N_HEADS * HEAD))


def _dn_post_fwd(o, proj, out_norm, tt=256):
    t = o.shape[1]
    tt = min(tt, t)

    def body(o_ref, z_ref, g_ref, y_ref):
        for h in range(N_HEADS):
            y_ref[:, h * HEAD:(h + 1) * HEAD] = _dn_gate_out(o_ref[h], _heads_in(z_ref, h), g_ref[...]).astype(BF16)

    return _rows("dn_post", body, [o, (proj,) + _z_spec(tt)], [_sds((t, N_HEADS * HEAD), BF16)], tt=tt, consts=[out_norm])[0]


def _dn_post_bwd(o, proj, out_norm, dy, tt=256):
    t = o.shape[1]
    tt = min(tt, t)

    def body(o_ref, z_ref, dy_ref, g_ref, do_ref, dz_ref, dg_ref):
        dg = jnp.zeros(g_ref.shape, F32)
        for h in range(N_HEADS):
            _, vjp = jax.vjp(_dn_gate_out, o_ref[h], _heads_in(z_ref, h), g_ref[...])
            a, b, d = vjp(_heads_in(dy_ref, h))
            do_ref[h] = a
            dz_ref[:, h * HEAD:(h + 1) * HEAD] = b
            dg = dg + d
        _acc(dg_ref, dg)

    return _rows("dn_post_bwd", body, [o, (proj,) + _z_spec(tt), dy], [_sds(o.shape), _sds((t, N_HEADS * HEAD))], tt=tt,
                 consts=[out_norm], accs=[_sds(out_norm.shape)])


def _dn_fwd(x, h, w_in, conv_w, a_log, dt_bias, out_norm, w_out):
    proj = _mm("dn_in", h, w_in)
    c = _dn_conv_fwd(proj, conv_w)
    q, k, v, gates = _dn_prep_fwd(c, proj, a_log, dt_bias)
    o, s_in = _dn_chunks_fwd(q, k, v, gates)
    on = _dn_post_fwd(o, proj, out_norm)
    y = _mm("dn_out", on, w_out, add=x)
    return y, (h, proj, c, q, k, v, gates, o, s_in, on)


def _dn_bwd(saved, dy, w_in, conv_w, a_log, dt_bias, out_norm, w_out):
    h, proj, c, q, k, v, gates, o, s_in, on = saved
    d_wout = _mm("dn_out_wgrad", on, dy, ta=True, out_dtype=GRAD_DTYPE)
    don = _mm("dn_out_dgrad", dy, w_out, tb=True)
    do, dz, d_out_norm = _dn_post_bwd(o, proj, out_norm, don)
    dq, dk, dv, dgates = _dn_chunks_bwd(q, k, v, gates, s_in, do)
    dc, dab, d_a_log, d_dt_bias = _dn_prep_bwd(c, proj, a_log, dt_bias, dq, dk, dv, dgates)
    dproj, d_conv_w = _dn_conv_bwd(proj, conv_w, dc, dz, dab)
    d_win = _mm("dn_in_wgrad", h, dproj, ta=True, out_dtype=GRAD_DTYPE)
    dh = _mm("dn_in_dgrad", dproj, w_in, tb=True)
    return dh, d_win, d_conv_w, d_a_log, d_dt_bias, d_out_norm, d_wout


MLA_SCALE = MLA_QK ** -0.5
MLA_C = 512


def _swap_raw(x):
    lane = lax.broadcasted_iota(jnp.int32, x.shape, 1)
    half = MLA_ROPE // 2
    y = jnp.where(lane < half, pltpu.roll(x, LANES - half, 1), pltpu.roll(x, half, 1))
    return jnp.where(lane < MLA_ROPE, y, 0.0)


@jax.custom_vjp
def _swap_halves(x):
    return _swap_raw(x)


_swap_halves.defvjp(lambda x: (_swap_raw(x), None), lambda _, d: (_swap_raw(d),))


def _rms_rope(x, g, cos, sin):
    y = x * lax.rsqrt(jnp.sum(x * x, axis=-1, keepdims=True) * (1.0 / MLA_ROPE) + NORM_EPS) * g
    return y * cos + _swap_halves(y) * sin


def _mla_latent(cq, ckv, kr, gq, gkv, gkr, cos, sin):
    return _rms(cq, gq), _rms(ckv, gkv), _rms_rope(kr, gkr, cos, sin)


def _mla_prep1_fwd(c, gq, gkv, gkr, cos, sin):
    t = c.shape[0]

    def body(c_ref, cos_ref, sin_ref, gq_ref, gkv_ref, gkr_ref, cq_ref, ckv_ref, kr_ref):
        a, b, r = _mla_latent(c_ref[:, :256], c_ref[:, 256:384], c_ref[:, 384:], gq_ref[...], gkv_ref[...], gkr_ref[...],
                              cos_ref[...], sin_ref[...])
        cq_ref[...] = a.astype(BF16)
        ckv_ref[...] = b.astype(BF16)
        kr_ref[...] = r.astype(BF16)

    return _rows("mla_prep1", body, [c, cos, sin], [_sds((t, 256), BF16), _sds((t, HEAD), BF16), _sds((t, HEAD), BF16)],
                 tt=512, consts=[gq, gkv, gkr])


def _mla_prep1_bwd(c, gq, gkv, gkr, cos, sin, dcq, dckv, dkr_heads):
    def body(c_ref, cos_ref, sin_ref, dcq_ref, dckv_ref, dkr_ref, gq_ref, gkv_ref, gkr_ref, dc_ref, dgq_ref, dgkv_ref, dgkr_ref):
        dkr = dkr_ref[0]
        for h in range(1, N_HEADS):
            dkr = dkr + dkr_ref[h]
        _, vjp = jax.vjp(_mla_latent, c_ref[:, :256], c_ref[:, 256:384], c_ref[:, 384:], gq_ref[...], gkv_ref[...], gkr_ref[...],
                         cos_ref[...], sin_ref[...])
        a, b, r, d1, d2, d3, _, _ = vjp((dcq_ref[...], dckv_ref[...], dkr))
        dc_ref[:, :256] = a.astype(BF16)
        dc_ref[:, 256:384] = b.astype(BF16)
        dc_ref[:, 384:] = r.astype(BF16)
        _acc(dgq_ref, d1)
        _acc(dgkv_ref, d2)
        _acc(dgkr_ref, d3)

    return _rows("mla_prep1_bwd", body, [c, cos, sin, dcq, dckv, dkr_heads], [_sds(c.shape, BF16)], tt=512,
                 consts=[gq, gkv, gkr], accs=[_sds(gq.shape), _sds(gkv.shape), _sds(gkr.shape)])


def _mla_heads(qn, qr, kn, gqn, gqr, gkn, cos, sin):
    return _rms(qn, gqn) * MLA_SCALE, _rms_rope(qr, gqr, cos, sin) * MLA_SCALE, _rms(kn, gkn)


def _mla_prep2_fwd(qa, kv, gqn, gqr, gkn, cos, sin):
    t = qa.shape[0]

    def body(qa_ref, kv_ref, cos_ref, sin_ref, gqn_ref, gqr_ref, gkn_ref, qn_ref, qr_ref, kn_ref, v_ref):
        for h in range(N_HEADS):
            a, b, c = _mla_heads(_heads_in(qa_ref, h), _heads_in(qa_ref, N_HEADS + h), _heads_in(kv_ref, h),
                                 gqn_ref[...], gqr_ref[...], gkn_ref[...], cos_ref[...], sin_ref[...])
            qn_ref[h] = a.astype(BF16)
            qr_ref[h] = b.astype(BF16)
            kn_ref[h] = c.astype(BF16)
            v_ref[h] = _heads_in(kv_ref, N_HEADS + h).astype(BF16)

    hm = _sds((N_HEADS, t, HEAD), BF16)
    return _rows("mla_prep2", body, [qa, kv, cos, sin], [hm, hm, hm, hm], tt=256, consts=[gqn, gqr, gkn])


def _mla_prep2_bwd(qa, kv, gqn, gqr, gkn, cos, sin, dqn, dqr, dkn, dv):
    def body(qa_ref, kv_ref, cos_ref, sin_ref, dqn_ref, dqr_ref, dkn_ref, dv_ref, gqn_ref, gqr_ref, gkn_ref,
             dqa_ref, dkv_ref, d1_ref, d2_ref, d3_ref):
        d1 = jnp.zeros(gqn_ref.shape, F32)
        d2 = jnp.zeros(gqr_ref.shape, F32)
        d3 = jnp.zeros(gkn_ref.shape, F32)
        for h in range(N_HEADS):
            _, vjp = jax.vjp(_mla_heads, _heads_in(qa_ref, h), _heads_in(qa_ref, N_HEADS + h), _heads_in(kv_ref, h),
                             gqn_ref[...], gqr_ref[...], gkn_ref[...], cos_ref[...], sin_ref[...])
            a, b, c, e1, e2, e3, _, _ = vjp((dqn_ref[h], dqr_ref[h], dkn_ref[h]))
            dqa_ref[:, h * HEAD:(h + 1) * HEAD] = a.astype(BF16)
            dqa_ref[:, (N_HEADS + h) * HEAD:(N_HEADS + h + 1) * HEAD] = b.astype(BF16)
            dkv_ref[:, h * HEAD:(h + 1) * HEAD] = c.astype(BF16)
            dkv_ref[:, (N_HEADS + h) * HEAD:(N_HEADS + h + 1) * HEAD] = dv_ref[h].astype(BF16)
            d1, d2, d3 = d1 + e1, d2 + e2, d3 + e3
        _acc(d1_ref, d1)
        _acc(d2_ref, d2)
        _acc(d3_ref, d3)

    return _rows("mla_prep2_bwd", body, [qa, kv, cos, sin, dqn, dqr, dkn, dv], [_sds(qa.shape, BF16), _sds(kv.shape, BF16)],
                 tt=256, consts=[gqn, gqr, gkn], accs=[_sds(gqn.shape), _sds(gqr.shape), _sds(gkn.shape)])


def _mla_attn_fwd(qn, qr, kn, kr, v):
    nh, t, _ = qn.shape
    bq = _q_block(t)
    per = bq // ATT_BLOCK

    def kern(qn_ref, qr_ref, kn_ref, kr_ref, v_ref, o_ref, lse_ref):
        qb = pl.program_id(1)
        qv = jnp.concatenate([qn_ref[0], qr_ref[0]], axis=1)
        order = _key_order(bq, qb)

        def body(kb, carry, diagonal):
            acc, m, l = carry
            off = pl.multiple_of(kb * ATT_BLOCK, ATT_BLOCK)
            kv = jnp.concatenate([kn_ref[0, pl.ds(off, ATT_BLOCK), :], kr_ref[pl.ds(off, ATT_BLOCK), :]], axis=1)
            s = _dot_nt(qv, kv)
            if diagonal:
                s = jnp.where(order >= kb * ATT_BLOCK, s, -jnp.inf)
            m_new = jnp.maximum(m, jnp.max(s, axis=1, keepdims=True))
            alpha = jnp.exp(m - m_new)
            p = jnp.exp(s - m_new)
            acc = acc * alpha + _dot(p.astype(BF16), v_ref[0, pl.ds(off, ATT_BLOCK), :])
            return acc, m_new, l * alpha + jnp.sum(p, axis=1, keepdims=True)

        init = (jnp.zeros((bq, HEAD), F32), jnp.full((bq, 1), -jnp.inf, F32), jnp.zeros((bq, 1), F32))
        carry = lax.fori_loop(0, qb * per, lambda i, c: body(i, c, False), init)
        acc, m, l = lax.fori_loop(qb * per, (qb + 1) * per, lambda i, c: body(i, c, True), carry)
        o_ref[...] = acc / l
        lse_ref[...] = jnp.broadcast_to(m + jnp.log(l), (bq, HEAD))

    blk = pl.BlockSpec((1, bq, HEAD), lambda h, i: (h, i, 0))
    full = pl.BlockSpec((1, t, HEAD), lambda h, i: (h, 0, 0))
    tok = pl.BlockSpec((bq, HEAD), lambda h, i: (i, h))
    return pl.pallas_call(
        kern, name="mla_attn_fwd", grid=(nh, t // bq),
        in_specs=[blk, blk, full, pl.BlockSpec((t, HEAD), lambda h, i: (0, 0)), full],
        out_specs=[tok, tok], out_shape=[_sds((t, nh * HEAD)), _sds((t, nh * HEAD))],
        compiler_params=_cparams(dimension_semantics=("parallel", "arbitrary")),
    )(qn, qr, kn, kr, v)


def _mla_attn_bwd(qn, qr, kn, kr, v, o, lse, do):
    nh, t, _ = qn.shape
    bq = _q_block(t)
    per = bq // ATT_BLOCK

    def kern(qn_ref, qr_ref, kn_ref, kr_ref, v_ref, o_ref, lse_ref, do_ref, dqn_ref, dqr_ref, dkn_ref, dkr_ref, dv_ref):
        qb = pl.program_id(1)

        @pl.when(qb == 0)
        def _():
            dkn_ref[...] = jnp.zeros(dkn_ref.shape, F32)
            dkr_ref[...] = jnp.zeros(dkr_ref.shape, F32)
            dv_ref[...] = jnp.zeros(dv_ref.shape, F32)

        qv = jnp.concatenate([qn_ref[0], qr_ref[0]], axis=1)
        dov = do_ref[...]
        dob = dov.astype(BF16)
        delta = jnp.sum(dov * o_ref[...], axis=1, keepdims=True)
        lse_col = lse_ref[:, :1]
        order = _key_order(bq, qb)

        def body(kb, dq, diagonal):
            off = pl.multiple_of(kb * ATT_BLOCK, ATT_BLOCK)
            kv = jnp.concatenate([kn_ref[0, pl.ds(off, ATT_BLOCK), :], kr_ref[pl.ds(off, ATT_BLOCK), :]], axis=1)
            vv = v_ref[0, pl.ds(off, ATT_BLOCK), :]
            p = jnp.exp(_dot_nt(qv, kv) - lse_col)
            if diagonal:
                p = jnp.where(order >= kb * ATT_BLOCK, p, 0.0)
            ds = (p * (_dot_nt(dob, vv) - delta)).astype(BF16)
            dk = _dot_tn(ds, qv)
            dkn_ref[0, pl.ds(off, ATT_BLOCK), :] += dk[:, :HEAD]
            dkr_ref[0, pl.ds(off, ATT_BLOCK), :] += dk[:, HEAD:]
            dv_ref[0, pl.ds(off, ATT_BLOCK), :] += _dot_tn(p.astype(BF16), dob)
            return dq + _dot(ds, kv)

        dq = lax.fori_loop(0, qb * per, lambda i, c: body(i, c, False), jnp.zeros((bq, 2 * HEAD), F32))
        dq = lax.fori_loop(qb * per, (qb + 1) * per, lambda i, c: body(i, c, True), dq)
        dqn_ref[0] = dq[:, :HEAD]
        dqr_ref[0] = dq[:, HEAD:]

    hm = _sds((nh, t, HEAD))
    blk = pl.BlockSpec((1, bq, HEAD), lambda h, i: (h, i, 0))
    full = pl.BlockSpec((1, t, HEAD), lambda h, i: (h, 0, 0))
    tok = pl.BlockSpec((bq, HEAD), lambda h, i: (i, h))
    return pl.pallas_call(
        kern, name="mla_attn_bwd", grid=(nh, t // bq),
        in_specs=[blk, blk, full, pl.BlockSpec((t, HEAD), lambda h, i: (0, 0)), full, tok, tok, tok],
        out_specs=[blk, blk, full, full, full], out_shape=[hm, hm, hm, hm, hm],
        compiler_params=_cparams(dimension_semantics=("parallel", "arbitrary")),
    )(qn, qr, kn, kr, v, o, lse, do)


def _rope_tables(t):
    inv_freq = ROPE_THETA ** (-jnp.arange(0, MLA_ROPE, 2, dtype=F32) / MLA_ROPE)
    ang = jnp.arange(t, dtype=F32)[:, None] * inv_freq[None, :]
    c, s = jnp.cos(ang), jnp.sin(ang)
    pad = ((0, 0), (0, LANES - MLA_ROPE))
    return jnp.pad(jnp.concatenate([c, c], axis=1), pad), jnp.pad(jnp.concatenate([-s, s], axis=1), pad)


def _pad_lanes(v, n=LANES):
    return jnp.pad(v, (0, n - v.shape[0])).reshape(1, n)


def _mla_layout(w_down, w_uq, w_ukv):
    w_down_p = jnp.pad(w_down, ((0, 0), (0, MLA_C - w_down.shape[1])))
    uq = w_uq.reshape(w_uq.shape[0], N_HEADS, MLA_QK)
    rope = jnp.pad(uq[:, :, HEAD:], ((0, 0), (0, 0), (0, LANES - MLA_ROPE)))
    w_uq_p = jnp.concatenate([uq[:, :, :HEAD].reshape(-1, N_HEADS * HEAD), rope.reshape(-1, N_HEADS * LANES)], axis=1)
    ukv = w_ukv.reshape(w_ukv.shape[0], N_HEADS, 2 * HEAD)
    w_ukv_p = jnp.concatenate([ukv[:, :, :HEAD].reshape(-1, N_HEADS * HEAD), ukv[:, :, HEAD:].reshape(-1, N_HEADS * HEAD)], axis=1)
    return w_down_p, w_uq_p, w_ukv_p


def _mla_unlayout(d_down_p, d_uq_p, d_ukv_p):
    d_down = d_down_p[:, :256 + HEAD + MLA_ROPE]
    nope = d_uq_p[:, :N_HEADS * HEAD].reshape(-1, N_HEADS, HEAD)
    rope = d_uq_p[:, N_HEADS * HEAD:].reshape(-1, N_HEADS, LANES)[:, :, :MLA_ROPE]
    d_uq = jnp.concatenate([nope, rope], axis=2).reshape(-1, N_HEADS * MLA_QK)
    kn = d_ukv_p[:, :N_HEADS * HEAD].reshape(-1, N_HEADS, HEAD)
    vv = d_ukv_p[:, N_HEADS * HEAD:].reshape(-1, N_HEADS, HEAD)
    d_ukv = jnp.concatenate([kn, vv], axis=2).reshape(-1, N_HEADS * 2 * HEAD)
    return d_down, d_uq, d_ukv


def _mla_weight_shapes():
    return (_sds((1024, MLA_C), BF16), _sds((1, 256)), _sds((1, HEAD)), _sds((256, 2048), BF16), _sds((HEAD, 2048), BF16),
            _sds((1, HEAD)), _sds((1, HEAD)), _sds((1, HEAD)), _sds((1, HEAD)), _sds((1024, 1024), BF16),
            _sds((4096, HEAD)), _sds((4096, HEAD)))


def _mla_fwd(x, h, w_down, gq, gkv, w_uq, w_ukv, gqn, gqr, gkn, gkr, w_out, cos, sin):
    c = _mm("mla_down", h, w_down)
    cq, ckv, kr = _mla_prep1_fwd(c, gq, gkv, gkr, cos, sin)
    qa = _mm("mla_uq", cq, w_uq)
    kv = _mm("mla_ukv", ckv, w_ukv)
    qn, qr, kn, v = _mla_prep2_fwd(qa, kv, gqn, gqr, gkn, cos, sin)
    o, lse = _mla_attn_fwd(qn, qr, kn, kr, v)
    y = _mm("mla_out", o, w_out, add=x)
    return y, (h, c, cq, ckv, kr, qa, kv, qn, qr, kn, v, o, lse)


def _mla_bwd(saved, dy, w_down, gq, gkv, w_uq, w_ukv, gqn, gqr, gkn, gkr, w_out, cos, sin):
    h, c, cq, ckv, kr, qa, kv, qn, qr, kn, v, o, lse = saved
    d_wout = _mm("mla_out_wgrad", o, dy, ta=True, out_dtype=GRAD_DTYPE)
    do = _mm("mla_out_dgrad", dy, w_out, tb=True)
    dqn, dqr, dkn, dkr, dv = _mla_attn_bwd(qn, qr, kn, kr, v, o, lse, do)
    dqa, dkv, dgqn, dgqr, dgkn = _mla_prep2_bwd(qa, kv, gqn, gqr, gkn, cos, sin, dqn, dqr, dkn, dv)
    d_wuq = _mm("mla_uq_wgrad", cq, dqa, ta=True, out_dtype=GRAD_DTYPE)
    d_wukv = _mm("mla_ukv_wgrad", ckv, dkv, ta=True, out_dtype=GRAD_DTYPE)
    dcq = _mm("mla_uq_dgrad", dqa, w_uq, tb=True)
    dckv = _mm("mla_ukv_dgrad", dkv, w_ukv, tb=True)
    dc, dgq, dgkv, dgkr = _mla_prep1_bwd(c, gq, gkv, gkr, cos, sin, dcq, dckv, dkr)
    d_wdown = _mm("mla_down_wgrad", h, dc, ta=True, out_dtype=GRAD_DTYPE)
    dh = _mm("mla_down_dgrad", dc, w_down, tb=True)
    return dh, d_wdown, dgq, dgkv, d_wuq, d_wukv, dgqn, dgqr, dgkn, dgkr, d_wout


def _loss_head(y, target):
    d = y.shape[1]

    def body(y_ref, t_ref, dy_ref, l_ref):
        err = y_ref[...] - t_ref[...]
        dy_ref[...] = err * (1.0 / d)
        part = 0.5 * jnp.sum(jnp.sum(err * err, axis=1, keepdims=True) * (1.0 / d), axis=0, keepdims=True)
        _acc(l_ref, jnp.broadcast_to(part, (1, LANES)))

    return _rows("loss_head", body, [y, target], [_sds(y.shape)], tt=512, accs=[_sds((1, LANES))])


MESH_ID = pl.DeviceIdType.MESH
HBM_SPEC = pl.BlockSpec(memory_space=pltpu.HBM)


def _all_gather(name, x):
    m_per, n = x.shape

    def body(x_ref, out_ref, send_sems, recv_sems, local_sem):
        x, y, c = lax.axis_index("x"), lax.axis_index("y"), lax.axis_index("c")
        me, sibling = (x, y, c), (x, y, 1 - c)
        chips = [(1 - x, y), (x, 1 - y), (1 - x, 1 - y)]

        def rows(px, py, pc):
            return out_ref.at[pl.ds((4 * px + 2 * py + pc) * m_per, m_per), :]

        def copy(k, block, to, src=None):
            return pltpu.make_async_remote_copy(
                src_ref=rows(*block) if src is None else src, dst_ref=rows(*block),
                send_sem=send_sems.at[k], recv_sem=recv_sems.at[k], device_id=to, device_id_type=MESH_ID)

        mine = pltpu.make_async_copy(x_ref, rows(*me), local_sem)
        mine.start()
        first = [copy(0, me, sibling, src=x_ref)]
        first += [copy(1 + j, me, (*chip, c), src=x_ref) for j, chip in enumerate(chips)]
        for cp in first:
            cp.start()
        passed = [copy(4 + j, (*chip, c), sibling) for j, chip in enumerate(chips)]
        for j, chip in enumerate(chips):
            copy(1 + j, (*chip, c), me).wait_recv()
            passed[j].start()
        copy(0, sibling, me).wait_recv()
        for j, chip in enumerate(chips):
            copy(4 + j, (*chip, 1 - c), me).wait_recv()
        for cp in first + passed:
            cp.wait_send()
        mine.wait()

    return pl.pallas_call(
        body, name=name,
        out_shape=jax.ShapeDtypeStruct((N_DEV * m_per, n), x.dtype),
        in_specs=[HBM_SPEC], out_specs=HBM_SPEC,
        scratch_shapes=[pltpu.SemaphoreType.DMA((7,)), pltpu.SemaphoreType.DMA((7,)), pltpu.SemaphoreType.DMA],
    )(x)


def _all_gather_groups(name, xs):
    ng = len(xs)

    def body(*refs):
        x_refs, out_refs, token = refs[:ng], refs[ng:2 * ng], refs[2 * ng]
        send_sems, recv_sems, local_sems = refs[2 * ng + 1:]
        token[...] = jnp.zeros(token.shape, F32)
        x, y, c = lax.axis_index("x"), lax.axis_index("y"), lax.axis_index("c")
        me, sibling = (x, y, c), (x, y, 1 - c)
        chips = [(1 - x, y), (x, 1 - y), (1 - x, 1 - y)]

        def copy(g, k, block, to, src=None):
            px, py, pc = block
            dst = out_refs[g].at[4 * px + 2 * py + pc]
            return pltpu.make_async_remote_copy(
                src_ref=dst if src is None else src, dst_ref=dst,
                send_sem=send_sems.at[g, k], recv_sem=recv_sems.at[g, k], device_id=to, device_id_type=MESH_ID)

        mine = [pltpu.make_async_copy(x_refs[g], out_refs[g].at[4 * x + 2 * y + c], local_sems.at[g]) for g in range(ng)]
        for cp in mine:
            cp.start()
        first = []
        for g in range(ng):
            first.append(copy(g, 0, me, sibling, src=x_refs[g]))
            first += [copy(g, 1 + j, me, (*chip, c), src=x_refs[g]) for j, chip in enumerate(chips)]
        for cp in first:
            cp.start()
        passed = []
        for j, chip in enumerate(chips):
            for g in range(ng):
                copy(g, 1 + j, (*chip, c), me).wait_recv()
                passed.append(copy(g, 4 + j, (*chip, c), sibling))
                passed[-1].start()
        for g in range(ng):
            copy(g, 0, sibling, me).wait_recv()
            for j, chip in enumerate(chips):
                copy(g, 4 + j, (*chip, 1 - c), me).wait_recv()
        for cp in first + passed:
            cp.wait_send()
        for cp in mine:
            cp.wait()

    return pl.pallas_call(
        body, name=name,
        out_shape=[jax.ShapeDtypeStruct((N_DEV,) + x.shape, x.dtype) for x in xs] + [_sds((8, LANES))],
        in_specs=[HBM_SPEC] * ng, out_specs=[HBM_SPEC] * ng + [pl.BlockSpec(memory_space=pltpu.VMEM)],
        scratch_shapes=[pltpu.SemaphoreType.DMA((ng, 7)), pltpu.SemaphoreType.DMA((ng, 7)), pltpu.SemaphoreType.DMA((ng,))],
    )(*xs)


EFFECT = pltpu.SideEffectType.DATAFLOW_SIDE_EFFECTING
SEM_SPEC = pl.BlockSpec(memory_space=pltpu.SEMAPHORE)


def _push_copies(src_refs, land_refs, send_sems, recv_sems, chunked):
    x, y, c = lax.axis_index("x"), lax.axis_index("y"), lax.axis_index("c")
    me = 4 * x + 2 * y + c
    copies = []
    for g, (src, land) in enumerate(zip(src_refs, land_refs)):
        for k in range(1, N_DEV):
            px = 1 - x if k & 4 else x
            py = 1 - y if k & 2 else y
            pc = 1 - c if k & 1 else c
            copies.append(pltpu.make_async_remote_copy(
                src_ref=src.at[4 * px + 2 * py + pc] if chunked else src, dst_ref=land.at[me],
                send_sem=send_sems.at[g * (N_DEV - 1) + k - 1], recv_sem=recv_sems.at[g * (N_DEV - 1) + k - 1],
                device_id=(px, py, pc), device_id_type=MESH_ID))
    return copies


def _hbm(x):
    return pltpu.with_memory_space_constraint(x, pltpu.HBM)


def _push_start(name, srcs, lands, chunked):
    ng = len(srcs)

    def body(*refs):
        for cp in _push_copies(refs[:ng], refs[ng:2 * ng], refs[2 * ng], refs[2 * ng + 1], chunked):
            cp.start()
        refs[-1][...] = jnp.zeros(refs[-1].shape, F32)

    bufs = list(srcs) + list(lands)
    outs = pl.pallas_call(
        body, name=name,
        out_shape=(pltpu.SemaphoreType.DMA((ng * (N_DEV - 1),)), pltpu.SemaphoreType.DMA((ng * (N_DEV - 1),)),
                   *[pltpu.HBM(b.shape, b.dtype) for b in bufs], jax.ShapeDtypeStruct((8, LANES), F32)),
        in_specs=[HBM_SPEC] * (2 * ng),
        out_specs=(SEM_SPEC, SEM_SPEC, *[HBM_SPEC] * (2 * ng), pl.BlockSpec(memory_space=pltpu.VMEM)),
        input_output_aliases={i: 2 + i for i in range(2 * ng)},
        compiler_params=pltpu.CompilerParams(has_side_effects=EFFECT),
    )(*[_hbm(b) for b in bufs])
    return outs[0], outs[1], list(outs[2:2 + ng]), list(outs[2 + ng:2 + 2 * ng]), outs[-1]


def _push_wait(name, started, after, chunked):
    send_sems, recv_sems, srcs, lands, _ = started
    ng = len(srcs)

    def body(*refs):
        copies = _push_copies(refs[:ng], refs[ng:2 * ng], refs[2 * ng], refs[2 * ng + 1], chunked)
        for cp in copies:
            cp.wait_send()
        for cp in copies:
            cp.wait_recv()

    bufs = srcs + lands
    outs = pl.pallas_call(
        body, name=name,
        out_shape=tuple(pltpu.HBM(b.shape, b.dtype) for b in bufs),
        in_specs=[HBM_SPEC] * (2 * ng) + [SEM_SPEC, SEM_SPEC, pl.BlockSpec(memory_space=pl.ANY)],
        out_specs=tuple([HBM_SPEC] * (2 * ng)),
        input_output_aliases={i: i for i in range(2 * ng)},
        compiler_params=pltpu.CompilerParams(has_side_effects=EFFECT),
    )(*bufs, send_sems, recv_sems, after)
    return list(outs[:ng]), list(outs[ng:])


def _sum_adam_devices(sent, recv, dev, w, m, v):
    ndev, r, c_ = recv.shape
    tr = _pick(r, (256, 128, 96, 32))

    def body(dev_ref, own_ref, r_ref, w_ref, m_ref, v_ref, g_ref, d_ref, mo_ref, vo_ref):
        me = dev_ref[0]
        g = jnp.where(me == 0, own_ref[0], r_ref[0]).astype(F32)
        for j in range(1, ndev):
            g = g + jnp.where(me == j, own_ref[0], r_ref[j]).astype(F32)
        g_ref[...] = g
        d_ref[...], mo_ref[...], vo_ref[...] = _adam(w_ref[...], g, m_ref[...], v_ref[...])

    row = pl.BlockSpec((tr, c_), lambda i, dev_ref: (i, 0))
    return pl.pallas_call(
        body, name="sum_adam",
        grid_spec=pltpu.PrefetchScalarGridSpec(
            num_scalar_prefetch=1, grid=(r // tr,),
            in_specs=[pl.BlockSpec((1, tr, c_), lambda i, dev_ref: (dev_ref[0], i, 0)),
                      pl.BlockSpec((ndev, tr, c_), lambda i, dev_ref: (0, i, 0)), row, row, row],
            out_specs=[row, row, row, row]),
        out_shape=[_sds((r, c_))] * 4,
        compiler_params=_cparams(dimension_semantics=("arbitrary",)),
    )(dev, sent, recv, w, m, v)


def _cols_from_shards(w, width):
    ns, r, cs = w.shape
    tr = _pick(r, (256, 128))

    def body(w_ref, o_ref):
        parts = [w_ref[j] for j in range(ns)]
        if width > ns * cs:
            parts.append(jnp.zeros((tr, width - ns * cs), w.dtype))
        o_ref[...] = jnp.concatenate(parts, axis=1)

    return pl.pallas_call(
        body, name="cols_from_shards", grid=(r // tr,),
        in_specs=[pl.BlockSpec((ns, tr, cs), lambda i: (0, i, 0))], out_specs=pl.BlockSpec((tr, width), lambda i: (i, 0)),
        out_shape=jax.ShapeDtypeStruct((r, width), w.dtype), compiler_params=_cparams(dimension_semantics=("arbitrary",)),
    )(w)


def _shards_from_cols(g, cs):
    r, width = g.shape
    tr = _pick(r, (256, 128))

    def body(g_ref, o_ref):
        for j in range(N_DEV):
            o_ref[j] = g_ref[:, j * cs:(j + 1) * cs]

    return pl.pallas_call(
        body, name="shards_from_cols", grid=(r // tr,),
        in_specs=[pl.BlockSpec((tr, width), lambda i: (i, 0))], out_specs=pl.BlockSpec((N_DEV, tr, cs), lambda i: (0, i, 0)),
        out_shape=jax.ShapeDtypeStruct((N_DEV, r, cs), g.dtype), compiler_params=_cparams(dimension_semantics=("arbitrary",)),
    )(g)


def _adam(w, g, m, v):
    m = ADAM_B1 * m + (1.0 - ADAM_B1) * g
    v = ADAM_B2 * v + (1.0 - ADAM_B2) * (g * g)
    m_hat = m / (1.0 - ADAM_B1 ** ADAM_STEP)
    v_hat = v / (1.0 - ADAM_B2 ** ADAM_STEP)
    return -ADAM_LR * (m_hat / (jnp.sqrt(v_hat) + ADAM_EPS) + ADAM_WD * w), m, v


def _sum_devices(gathered):
    m_all, n = gathered.shape
    m_per = m_all // N_DEV

    def body(x_ref, o_ref):
        s = x_ref[0:m_per, :]
        for j in range(1, N_DEV):
            s = s + x_ref[j * m_per:(j + 1) * m_per, :]
        o_ref[...] = s

    return pl.pallas_call(body, name="sum_devices", out_shape=_sds((m_per, n)), compiler_params=_cparams())(gathered)


def _adam_small(w, g, m, v):
    def body(w_ref, g_ref, m_ref, v_ref, d_ref, mo_ref, vo_ref):
        d_ref[...], mo_ref[...], vo_ref[...] = _adam(w_ref[...], g_ref[...], m_ref[...], v_ref[...])

    return pl.pallas_call(body, name="adam_small", out_shape=[_sds(w.shape)] * 3, compiler_params=_cparams())(w, g, m, v)


N_LAYERS = 4
_MIXER = ("dn", "sb", "mla")
_MIXER_PARAMS = {
    "dn": ("dn_w_in", "dn_conv_w", "dn_a_log", "dn_dt_bias", "dn_out_norm", "dn_w_out"),
    "sb": ("sb_w_qkv", "sb_q_norm", "sb_k_norm", "sb_w_out"),
    "mla": ("mla_w_down", "mla_q_a_norm", "mla_kv_a_norm", "mla_w_uq", "mla_w_ukv", "mla_q_nope_norm", "mla_q_rope_norm",
            "mla_k_nope_norm", "mla_k_rope_norm", "mla_w_out"),
}
_BIG_AXIS = {"dn_w_in": 1, "dn_w_out": 0, "sb_w_qkv": 1, "sb_w_out": 0, "mla_w_down": 0, "mla_w_uq": 1, "mla_w_ukv": 1,
             "mla_w_out": 0, "ffn_w_gate_up": 1, "ffn_w_down": 0}


def _weight_names():
    names = []
    for i in range(N_LAYERS):
        p = "l%d_" % i
        names += [p + "mix_norm"] + [p + n for n in _MIXER_PARAMS[_MIXER[i % 3]]] + [p + "ffn_norm", p + "ffn_w_gate_up", p + "ffn_w_down"]
    return names


WEIGHTS = _weight_names()
BIG = [n for n in WEIGHTS if n[3:] in _BIG_AXIS]
SMALL = [n for n in WEIGHTS if n[3:] not in _BIG_AXIS]
CONV = [n for n in SMALL if n.endswith("conv_w")]


def _ceil_to(n, k):
    return -(-n // k) * k


def _pack(arrs, cols, row_mult):
    parts = []
    for a in arrs:
        f = a.reshape(-1)
        parts.append(jnp.pad(f, (0, _ceil_to(f.shape[0], cols) - f.shape[0])))
    flat = jnp.concatenate(parts)
    rows = _ceil_to(flat.shape[0] // cols, row_mult)
    return jnp.pad(flat, (0, rows * cols - flat.shape[0])).reshape(rows, cols)


def _unpack(buf, shapes):
    cols = buf.shape[-1]
    out, r0 = [], 0
    for s in shapes:
        n = math.prod(s)
        nr = _ceil_to(n, cols) // cols
        out.append(buf[r0:r0 + nr].reshape(-1)[:n].reshape(s))
        r0 += nr
    return out


def _layer_groups(i):
    by = {"gu": (704, []), "down": (1024, []), "out": (1024, []), "dn_in": (514, []), "sb_qkv": (384, []), "mla": (512, [])}
    key = {"ffn_w_gate_up": "gu", "ffn_w_down": "down", "dn_w_in": "dn_in", "sb_w_qkv": "sb_qkv", "mla_w_down": "mla",
           "mla_w_uq": "mla", "mla_w_ukv": "mla"}
    for n in BIG:
        if n.startswith("l%d_" % i):
            by[key.get(n[3:], "out")][1].append(n)
    return [g for g in by.values() if g[1]]


LAYER_GROUPS = [_layer_groups(i) for i in range(N_LAYERS)]
N_FFN_GROUPS = 2


def _stack_group(grp, get):
    width, names = grp
    if len(names) == 1 and get(names[0]).shape[1] == width:
        return get(names[0])
    return jnp.concatenate([jnp.pad(get(n), ((0, 0), (0, width - get(n).shape[1]))) for n in names], axis=0)


def _unstack_group(grp, buf, shape_of):
    if len(grp[1]) == 1 and tuple(shape_of(grp[1][0])) == tuple(buf.shape[-2:]):
        return [buf]
    out, r0 = [], 0
    for n in grp[1]:
        rs, cs = shape_of(n)
        out.append(buf[..., r0:r0 + rs, :cs])
        r0 += rs
    return out


def kernel(x, l0_mix_norm, l0_dn_w_in, l0_dn_conv_w, l0_dn_a_log, l0_dn_dt_bias, l0_dn_out_norm, l0_dn_w_out, l0_ffn_norm, l0_ffn_w_gate_up, l0_ffn_w_down, l1_mix_norm, l1_sb_w_qkv, l1_sb_q_norm, l1_sb_k_norm, l1_sb_w_out, l1_ffn_norm, l1_ffn_w_gate_up, l1_ffn_w_down, l2_mix_norm, l2_mla_w_down, l2_mla_q_a_norm, l2_mla_kv_a_norm, l2_mla_w_uq, l2_mla_w_ukv, l2_mla_q_nope_norm, l2_mla_q_rope_norm, l2_mla_k_nope_norm, l2_mla_k_rope_norm, l2_mla_w_out, l2_ffn_norm, l2_ffn_w_gate_up, l2_ffn_w_down, l3_mix_norm, l3_dn_w_in, l3_dn_conv_w, l3_dn_a_log, l3_dn_dt_bias, l3_dn_out_norm, l3_dn_w_out, l3_ffn_norm, l3_ffn_w_gate_up, l3_ffn_w_down, loss_target, m_l0_mix_norm, m_l0_dn_w_in, m_l0_dn_conv_w, m_l0_dn_a_log, m_l0_dn_dt_bias, m_l0_dn_out_norm, m_l0_dn_w_out, m_l0_ffn_norm, m_l0_ffn_w_gate_up, m_l0_ffn_w_down, m_l1_mix_norm, m_l1_sb_w_qkv, m_l1_sb_q_norm, m_l1_sb_k_norm, m_l1_sb_w_out, m_l1_ffn_norm, m_l1_ffn_w_gate_up, m_l1_ffn_w_down, m_l2_mix_norm, m_l2_mla_w_down, m_l2_mla_q_a_norm, m_l2_mla_kv_a_norm, m_l2_mla_w_uq, m_l2_mla_w_ukv, m_l2_mla_q_nope_norm, m_l2_mla_q_rope_norm, m_l2_mla_k_nope_norm, m_l2_mla_k_rope_norm, m_l2_mla_w_out, m_l2_ffn_norm, m_l2_ffn_w_gate_up, m_l2_ffn_w_down, m_l3_mix_norm, m_l3_dn_w_in, m_l3_dn_conv_w, m_l3_dn_a_log, m_l3_dn_dt_bias, m_l3_dn_out_norm, m_l3_dn_w_out, m_l3_ffn_norm, m_l3_ffn_w_gate_up, m_l3_ffn_w_down, v_l0_mix_norm, v_l0_dn_w_in, v_l0_dn_conv_w, v_l0_dn_a_log, v_l0_dn_dt_bias, v_l0_dn_out_norm, v_l0_dn_w_out, v_l0_ffn_norm, v_l0_ffn_w_gate_up, v_l0_ffn_w_down, v_l1_mix_norm, v_l1_sb_w_qkv, v_l1_sb_q_norm, v_l1_sb_k_norm, v_l1_sb_w_out, v_l1_ffn_norm, v_l1_ffn_w_gate_up, v_l1_ffn_w_down, v_l2_mix_norm, v_l2_mla_w_down, v_l2_mla_q_a_norm, v_l2_mla_kv_a_norm, v_l2_mla_w_uq, v_l2_mla_w_ukv, v_l2_mla_q_nope_norm, v_l2_mla_q_rope_norm, v_l2_mla_k_nope_norm, v_l2_mla_k_rope_norm, v_l2_mla_w_out, v_l2_ffn_norm, v_l2_ffn_w_gate_up, v_l2_ffn_w_down, v_l3_mix_norm, v_l3_dn_w_in, v_l3_dn_conv_w, v_l3_dn_a_log, v_l3_dn_dt_bias, v_l3_dn_out_norm, v_l3_dn_w_out, v_l3_ffn_norm, v_l3_ffn_w_gate_up, v_l3_ffn_w_down):
    a = dict(locals())
    return _train_step(a)


def _train_step(a):
    mx, my, mc = lax.axis_index("x"), lax.axis_index("y"), lax.axis_index("c")
    dev = 4 * mx + 2 * my + mc
    dev_arr = jnp.reshape(dev, (1,)).astype(jnp.int32)
    t, d = a["x"].shape[1], a["x"].shape[2]
    xs = a["x"].reshape(t, d)
    target = a["loss_target"].reshape(t, d)

    full = {}

    def unpack(groups, bufs):
        for grp, buf in zip(groups, bufs):
            for n, shards in zip(grp[1], _unstack_group(grp, buf, lambda n: a[n].shape)):
                kind = n[3:]
                if kind == "ffn_w_gate_up":
                    full[n] = shards
                elif _BIG_AXIS[kind] == 0:
                    full[n] = shards.reshape(N_DEV * shards.shape[1], shards.shape[2])
                else:
                    width = DN_PROJ if kind == "dn_w_in" else N_DEV * shards.shape[2]
                    full[n] = _cols_from_shards(shards, width)

    def local_shards(groups):
        return [_stack_group(grp, lambda n: a[n].astype(BF16)) for grp in groups]

    pushed_groups = {"l0_gate_up": LAYER_GROUPS[0][:1], "l0_down": LAYER_GROUPS[0][1:N_FFN_GROUPS]}
    pushed_groups.update({"l%d" % i: LAYER_GROUPS[i] for i in range(1, N_LAYERS)})
    conv_pack = _pack([a[n] for n in CONV], LANES, 8)
    first_groups = LAYER_GROUPS[0][N_FFN_GROUPS:]
    *first_bufs, conv_all, gathered = _all_gather_groups("gather_weights", local_shards(first_groups) + [conv_pack])
    unpack(first_groups, first_bufs)
    for n, parts in zip(CONV, zip(*[_unpack(conv_all[j], [a[n].shape for n in CONV]) for j in range(N_DEV)])):
        full[n] = jnp.concatenate(parts, axis=1)
    after_first = gathered[0, 0].astype(BF16)
    gathers, started = {}, jnp.zeros((), F32)
    for tag, groups in pushed_groups.items():
        srcs = local_shards(groups)
        srcs[-1] = srcs[-1] + after_first
        lands = [lax.dynamic_update_index_in_dim(lax.empty((N_DEV,) + s.shape, s.dtype), s, dev, 0) for s in srcs]
        gathers[tag] = _push_start("gather_start_" + tag, srcs, lands, False)
        started = started + gathers[tag][-1][0, 0]

    def arrived(tag, after):
        unpack(pushed_groups[tag], _push_wait("gather_wait_" + tag, gathers[tag], after, False)[1])

    def vec(n):
        return a[n].reshape(1, -1)

    cos, sin = _rope_tables(t)

    def mixer_args(i):
        p = "l%d_" % i
        kind = _MIXER[i % 3]
        if kind == "dn":
            args = (full[p + "dn_w_in"], full[p + "dn_conv_w"], _pad_lanes(a[p + "dn_a_log"]), _pad_lanes(a[p + "dn_dt_bias"]),
                    vec(p + "dn_out_norm"), full[p + "dn_w_out"])
        elif kind == "sb":
            args = (full[p + "sb_w_qkv"], vec(p + "sb_q_norm"), vec(p + "sb_k_norm"), full[p + "sb_w_out"])
        else:
            w_down, w_uq, w_ukv = _mla_layout(full[p + "mla_w_down"], full[p + "mla_w_uq"], full[p + "mla_w_ukv"])
            args = (w_down, vec(p + "mla_q_a_norm"), vec(p + "mla_kv_a_norm"), w_uq, w_ukv, vec(p + "mla_q_nope_norm"),
                    _pad_lanes(a[p + "mla_q_rope_norm"]), vec(p + "mla_k_nope_norm"), _pad_lanes(a[p + "mla_k_rope_norm"]),
                    full[p + "mla_w_out"], cos, sin)
        return kind, args

    fwd = {"dn": _dn_fwd, "sb": _sb_fwd, "mla": _mla_fwd}
    bwd = {"dn": _dn_bwd, "sb": _sb_bwd, "mla": _mla_bwd}
    saved, layer_args = [], []
    for i in range(N_LAYERS):
        p = "l%d_" % i
        if i > 0:
            arrived("l%d" % i, xs)
        kind, args = mixer_args(i)
        layer_args.append((kind, args))
        gain = vec(p + "mix_norm") + started if i == 0 else vec(p + "mix_norm")
        h = _rmsnorm_fwd("mix_norm", xs, gain)
        x_mid, sv_mix = fwd[kind](xs, h, *args)
        if i == 0:
            arrived("l0_gate_up", x_mid)

        def down_weight(act, p=p, first=(i == 0)):
            if first:
                arrived("l0_down", act)
            return full[p + "ffn_w_down"]

        x_out, sv_ffn = _ffn_fwd(x_mid, vec(p + "ffn_norm"), full[p + "ffn_w_gate_up"], down_weight)
        saved.append((xs, sv_mix, sv_ffn))
        xs = x_out
    dy, loss_part = _loss_head(xs, target)

    grads, big_out = {}, {}

    def grad_shards(n):
        g, (rs, cs) = grads[n], a[n].shape
        if g.ndim == 3:
            return g
        if _BIG_AXIS[n[3:]] == 0:
            return g.reshape(N_DEV, rs, cs)
        return _shards_from_cols(g, cs)

    def push_grads(tag, groups):
        sends = []
        for grp in groups:
            if len(grp[1]) == 1 and a[grp[1][0]].shape[1] == grp[0]:
                sends.append(grad_shards(grp[1][0]))
                continue
            parts = [jnp.pad(grad_shards(n), ((0, 0), (0, 0), (0, grp[0] - a[n].shape[1]))) for n in grp[1]]
            sends.append(jnp.concatenate(parts, axis=1))
        lands = [lax.empty(s.shape, s.dtype) for s in sends]
        return tag, groups, _push_start("grads_start_" + tag, sends, lands, True)

    def finish_grads(push, after):
        tag, groups, pushed = push
        sents, recvs = _push_wait("grads_wait_" + tag, pushed, after, True)
        for grp, sent, recv in zip(groups, sents, recvs):
            packs = [_stack_group(grp, lambda n, pre=pre: a[pre + n]) for pre in ("", "m_", "v_")]
            outs = [_unstack_group(grp, o, lambda n: a[n].shape) for o in _sum_adam_devices(sent, recv, dev_arr, *packs)]
            for j, n in enumerate(grp[1]):
                big_out[n] = [o[j] for o in outs]

    mixer_push = None
    for i in reversed(range(N_LAYERS)):
        p = "l%d_" % i
        kind, args = layer_args[i]
        x_in, sv_mix, sv_ffn = saved[i]
        gain = vec(p + "ffn_norm") if mixer_push is None else vec(p + "ffn_norm") + mixer_push[2][-1][0, 0]
        dx_mid, grads[p + "ffn_norm"], grads[p + "ffn_w_gate_up"], grads[p + "ffn_w_down"] = _ffn_bwd(
            sv_ffn, dy, gain, full[p + "ffn_w_gate_up"], full[p + "ffn_w_down"])
        ffn_push = push_grads("l%d_ffn" % i, LAYER_GROUPS[i][:N_FFN_GROUPS])
        if mixer_push is not None:
            finish_grads(mixer_push, dx_mid)
        res = bwd[kind](sv_mix, dx_mid, *args)
        dh = res[0]
        if kind == "mla":
            res = list(res)
            res[1], res[4], res[5] = _mla_unlayout(res[1], res[4], res[5])
        for n, g in zip(_MIXER_PARAMS[kind], res[1:]):
            grads[p + n] = g
        mixer_push = push_grads("l%d_mix" % i, LAYER_GROUPS[i][N_FFN_GROUPS:])
        pushed = ffn_push[2][-1][0, 0] + mixer_push[2][-1][0, 0]
        dy, grads[p + "mix_norm"] = _rmsnorm_bwd("mix_norm_bwd", x_in, vec(p + "mix_norm") + pushed, dh, dx_mid)
        finish_grads(ffn_push, dy)
    grad_x = dy.reshape(a["x"].shape)

    small_full_shapes = [full[n].shape if n in CONV else a[n].shape for n in SMALL]
    small_grads = []
    for n, s in zip(SMALL, small_full_shapes):
        g = grads[n].reshape(-1)
        small_grads.append(g[:math.prod(s)])
    small_pack = _pack(small_grads + [loss_part.reshape(-1)], LANES, 8)
    small_sum = _sum_devices(_all_gather("gather_small_grads", small_pack))
    small_red = _unpack(small_sum, small_full_shapes + [(LANES,)])
    loss = small_red[-1][0]
    g_small = {}
    for n, g in zip(SMALL, small_red[:-1]):
        if n in CONV:
            cs = a[n].shape[1]
            g = lax.dynamic_slice_in_dim(g, dev * cs, cs, axis=1)
        g_small[n] = g
    small_shapes = [a[n].shape for n in SMALL]
    packs = [_pack([src[n] for n in SMALL], LANES, 8) for src in
             ({n: a[n] for n in SMALL}, g_small, {n: a["m_" + n] for n in SMALL}, {n: a["v_" + n] for n in SMALL})]
    d_small, m_small, v_small = (_unpack(o, small_shapes) for o in _adam_small(*packs))
    finish_grads(mixer_push, small_sum)

    small_out = dict(zip(SMALL, zip([g_small[n] for n in SMALL], d_small, m_small, v_small)))

    def out(k):
        return [small_out[n][k] if n in small_out else big_out[n][k] for n in WEIGHTS]

    return (loss, grad_x, *out(0), *out(1), *out(2), *out(3))
```

```python
import math

import jax
import jax.numpy as jnp
from jax import lax
from jax.experimental import pallas as pl
from jax.experimental.pallas import tpu as pltpu

F32 = jnp.float32
BF16 = jnp.bfloat16
GRAD_DTYPE = BF16
F32X3 = lax.Precision.HIGH

LANES = 128
N_DEV = 8
N_HEADS = 8
HEAD = 128
NORM_EPS = 1e-6
DN_CHUNK = 64
DN_STEP = 2 * DN_CHUNK
ATT_BLOCK = 512
ATT_Q = 512
MLA_ROPE = 64
MLA_QK = 192
ROPE_THETA = 10000.0
VMEM_LIMIT = 56 * 1024 * 1024
MM_TILE_BYTES = 32 * 1024 * 1024

ADAM_LR = 0.001
ADAM_B1 = 0.9
ADAM_B2 = 0.999
ADAM_EPS = 1e-08
ADAM_WD = 0.01
ADAM_STEP = 10


def _cparams(**kw):
    return pltpu.CompilerParams(vmem_limit_bytes=VMEM_LIMIT, **kw)


def _pick(n, cands):
    for c in cands:
        if c <= n and n % c == 0:
            return c
    return n


def _mm(name, a, b, *, ta=False, tb=False, out_dtype=F32, add=None, tm=None, tn=None, tk=None):
    if ta:
        K, M = a.shape
    else:
        M, K = a.shape
    N = b.shape[0] if tb else b.shape[1]
    tm = tm or _pick(M, (1024, 512, 256, 128))
    tn = tn or _pick(N, (1024, 512, 384, 256, 128))
    if tk is None:
        fits = [c for c in (4096, 2048, 1408, 1024, 512, 384, 256, 128)
                if c <= K and K % c == 0 and 2 * c * (tm * a.dtype.itemsize + tn * b.dtype.itemsize) <= MM_TILE_BYTES]
        tk = fits[0] if fits else K
    return _mm_raw(
        name, a, b, ta=ta, tb=tb, out_dtype=out_dtype, add=add, grid=(M // tm, N // tn, K // tk), out_shape=(M, N),
        a_block=(tk, tm) if ta else (tm, tk), a_map=(lambda i, j, k: (k, i)) if ta else (lambda i, j, k: (i, k)),
        b_block=(tn, tk) if tb else (tk, tn), b_map=(lambda i, j, k: (j, k)) if tb else (lambda i, j, k: (k, j)),
        o_block=(tm, tn), o_map=lambda i, j, k: (i, j))


def _mm_raw(name, a, b, *, ta, tb, out_dtype, add, grid, out_shape, a_block, a_map, b_block, b_map, o_block, o_map):
    nk = grid[2]
    tm, tn = o_block
    dn = (((0 if ta else 1,), (1 if tb else 0,)), ((), ()))
    has_add = add is not None

    def kern(*refs):
        if has_add:
            a_ref, b_ref, add_ref, o_ref, acc_ref = refs
        else:
            a_ref, b_ref, o_ref, acc_ref = refs
        k = pl.program_id(2)
        part = lax.dot_general(a_ref[...].astype(BF16), b_ref[...].astype(BF16), dn, preferred_element_type=F32)

        @pl.when(k == 0)
        def _():
            acc_ref[...] = part

        @pl.when(k > 0)
        def _():
            acc_ref[...] += part

        @pl.when(k == nk - 1)
        def _():
            r = acc_ref[...]
            if has_add:
                r = r + add_ref[...]
            o_ref[...] = r.astype(out_dtype)

    in_specs = [pl.BlockSpec(a_block, a_map), pl.BlockSpec(b_block, b_map)]
    args = [a, b]
    if has_add:
        in_specs.append(pl.BlockSpec(o_block, o_map))
        args.append(add)
    return pl.pallas_call(
        kern, name=name,
        grid=grid,
        in_specs=in_specs,
        out_specs=pl.BlockSpec(o_block, o_map),
        out_shape=jax.ShapeDtypeStruct(out_shape, out_dtype),
        scratch_shapes=[pltpu.VMEM((tm, tn), F32)],
        compiler_params=_cparams(dimension_semantics=("parallel", "parallel", "arbitrary")),
    )(*args)


def _rows(name, body, ins, outs, *, tt, consts=(), accs=()):
    in_specs, args = [], []
    first = ins[0][0] if isinstance(ins[0], tuple) else ins[0]
    t = first.shape[-2]
    tt = min(tt, t)
    for x in ins:
        if isinstance(x, tuple):
            arr, bs, im = x
            in_specs.append(pl.BlockSpec(bs, im))
            args.append(arr)
        else:
            in_specs.append(_row_spec(x.shape, tt))
            args.append(x)
    for c in consts:
        in_specs.append(pl.BlockSpec(c.shape, lambda i, _n=c.ndim: (0,) * _n))
        args.append(c)
    out_specs = [_row_spec(o.shape, tt) for o in outs]
    out_specs += [pl.BlockSpec(a.shape, lambda i, _n=len(a.shape): (0,) * _n) for a in accs]
    res = pl.pallas_call(
        body, name=name, grid=(t // tt,),
        in_specs=in_specs, out_specs=out_specs, out_shape=list(outs) + list(accs),
        compiler_params=_cparams(dimension_semantics=("arbitrary",)),
    )(*args)
    return res


def _row_spec(shape, tt):
    if len(shape) == 2:
        return pl.BlockSpec((tt, shape[1]), lambda i: (i, 0))
    return pl.BlockSpec((shape[0], tt, shape[2]), lambda i: (0, i, 0))


def _sds(shape, dtype=F32):
    return jax.ShapeDtypeStruct(tuple(shape), dtype)


def _acc(ref, val):
    i = pl.program_id(0)

    @pl.when(i == 0)
    def _():
        ref[...] = val

    @pl.when(i > 0)
    def _():
        ref[...] += val


def _rms(x, g):
    return x * lax.rsqrt(jnp.mean(x * x, axis=-1, keepdims=True) + NORM_EPS) * g


def _silu(x):
    return x / (1.0 + jnp.exp(-x))


def _softplus(x):
    return jnp.maximum(x, 0.0) + jnp.log(1.0 + jnp.exp(-jnp.abs(x)))


def _sigmoid(x):
    return 1.0 / (1.0 + jnp.exp(-x))


def _rmsnorm_fwd(name, x, g, tt=512):
    def body(x_ref, g_ref, h_ref):
        h_ref[...] = _rms(x_ref[...], g_ref[...]).astype(BF16)

    return _rows(name, body, [x], [_sds(x.shape, BF16)], tt=tt, consts=[g])[0]


def _rmsnorm_bwd(name, x, g, dh, dres, tt=512):
    def body(x_ref, dh_ref, dres_ref, g_ref, dx_ref, dg_ref):
        _, vjp = jax.vjp(_rms, x_ref[...], g_ref[...])
        dx, dg = vjp(dh_ref[...])
        dx_ref[...] = dx + dres_ref[...]
        _acc(dg_ref, dg)

    return _rows(name, body, [x, dh, dres], [_sds(x.shape)], tt=tt, consts=[g], accs=[_sds(g.shape)])


def _ffn_fwd(x, norm_g, w3, down_weight):
    t, d = x.shape
    ns, _, cs = w3.shape
    half = ns // 2
    w2 = w3.reshape(ns * d, cs)
    h = _rmsnorm_fwd("ffn_norm", x, norm_g)
    tm = _pick(t, (1024, 512, 256, 128))
    nm = t // tm

    def gate_up(h_ref, wg_ref, wu_ref, g_ref, u_ref, a_ref):
        hv = h_ref[...]
        g = _dot(hv, wg_ref[...])
        u = _dot(hv, wu_ref[...])
        g_ref[...] = g
        u_ref[...] = u
        a_ref[...] = (_silu(g) * u).astype(BF16)

    hid = pl.BlockSpec((tm, cs), lambda j, i: (j * nm + i, 0))
    g, u, act = pl.pallas_call(
        gate_up, name="ffn_gate_up", grid=(half, nm),
        in_specs=[pl.BlockSpec((tm, d), lambda j, i: (i, 0)), pl.BlockSpec((d, cs), lambda j, i: (j, 0)),
                  pl.BlockSpec((d, cs), lambda j, i: (j + half, 0))],
        out_specs=[hid, hid, hid], out_shape=[_sds((half * t, cs)), _sds((half * t, cs)), _sds((half * t, cs), BF16)],
        compiler_params=_cparams(dimension_semantics=("parallel", "arbitrary")),
    )(h, w2, w2)
    w_down = down_weight(act)

    def down(a_ref, w_ref, x_ref, y_ref):
        y = x_ref[...]
        for j in range(half):
            y = y + _dot(a_ref[j], w_ref[j])
        y_ref[...] = y

    y = pl.pallas_call(
        down, name="ffn_down", grid=(nm,),
        in_specs=[pl.BlockSpec((half, tm, cs), lambda i: (0, i, 0)), pl.BlockSpec((half, cs, d), lambda i: (0, 0, 0)),
                  pl.BlockSpec((tm, d), lambda i: (i, 0))],
        out_specs=pl.BlockSpec((tm, d), lambda i: (i, 0)), out_shape=_sds((t, d)),
        compiler_params=_cparams(dimension_semantics=("parallel",)),
    )(act.reshape(half, t, cs), w_down.reshape(half, cs, d), x)
    return y, (x, h, g, u, act)


def _ffn_bwd(saved, dy, norm_g, w3, w_down):
    x, h, g, u, act = saved
    t, d = x.shape
    ns, _, cs = w3.shape
    half = ns // 2
    w2 = w3.reshape(ns * d, cs)
    tm = _pick(t, (1024, 512, 256, 128))
    nm = t // tm
    tk = _pick(t, (4096, 2048, 1024, 512, 256, 128))
    nk = t // tk
    d_wdown = _mm_raw("ffn_down_wgrad", act, dy, ta=True, tb=False, out_dtype=GRAD_DTYPE, add=None, grid=(half, 1, nk),
                      out_shape=(half * cs, d), a_block=(tk, cs), a_map=lambda i, j, k: (i * nk + k, 0),
                      b_block=(tk, d), b_map=lambda i, j, k: (k, 0), o_block=(cs, d), o_map=lambda i, j, k: (i, 0))
    def down_dgrad(dy_ref, wd_ref, g_ref, u_ref, dg_ref, du_ref):
        da = _dot_nt(dy_ref[...].astype(BF16), wd_ref[...])
        gv, uv = g_ref[...], u_ref[...]
        s = _sigmoid(gv)
        dg_ref[...] = (da * uv * s * (1.0 + gv * (1.0 - s))).astype(BF16)
        du_ref[...] = (da * gv * s).astype(BF16)

    hid = pl.BlockSpec((tm, cs), lambda j, i: (j * nm + i, 0))
    dg, du = pl.pallas_call(
        down_dgrad, name="ffn_down_dgrad", grid=(half, nm),
        in_specs=[pl.BlockSpec((tm, d), lambda j, i: (i, 0)), pl.BlockSpec((cs, d), lambda j, i: (j, 0)), hid, hid],
        out_specs=[hid, hid], out_shape=[_sds((half * t, cs), BF16), _sds((half * t, cs), BF16)],
        compiler_params=_cparams(dimension_semantics=("parallel", "arbitrary")),
    )(dy, w_down, g, u)

    def wgrad(name, dd):
        return _mm_raw(name, h, dd, ta=True, tb=False, out_dtype=GRAD_DTYPE, add=None, grid=(1, half, nk), out_shape=(half * d, cs),
                       a_block=(tk, d), a_map=lambda i, j, k: (k, 0), b_block=(tk, cs), b_map=lambda i, j, k: (j * nk + k, 0),
                       o_block=(d, cs), o_map=lambda i, j, k: (j, 0))

    def gate_up_dgrad(dg_ref, du_ref, w_ref, dh_ref):
        dh = _dot_nt(dg_ref[0], w_ref[0]) + _dot_nt(du_ref[0], w_ref[half])
        for j in range(1, half):
            dh = dh + _dot_nt(dg_ref[j], w_ref[j]) + _dot_nt(du_ref[j], w_ref[half + j])
        dh_ref[...] = dh

    th = _pick(t, (512, 256, 128))
    hid3 = pl.BlockSpec((half, th, cs), lambda i: (0, i, 0))
    dh = pl.pallas_call(
        gate_up_dgrad, name="ffn_gate_up_dgrad", grid=(t // th,),
        in_specs=[hid3, hid3, pl.BlockSpec((ns, d, cs), lambda i: (0, 0, 0))],
        out_specs=pl.BlockSpec((th, d), lambda i: (i, 0)), out_shape=_sds((t, d)),
        compiler_params=_cparams(dimension_semantics=("parallel",)),
    )(dg.reshape(half, t, cs), du.reshape(half, t, cs), w3)
    d_w3 = jnp.concatenate([wgrad("ffn_gate_wgrad", dg), wgrad("ffn_up_wgrad", du)], axis=0).reshape(ns, d, cs)
    dx, dgain = _rmsnorm_bwd("ffn_norm_bwd", x, norm_g, dh, dy)
    return dx, dgain, d_w3, d_wdown


def _dot_nt(a, b):
    return lax.dot_general(a, b, (((1,), (1,)), ((), ())), preferred_element_type=F32)


def _dot_tn(a, b):
    return lax.dot_general(a, b, (((0,), (0,)), ((), ())), preferred_element_type=F32)


def _dot(a, b):
    return jnp.dot(a, b, preferred_element_type=F32)


CUM_BLOCK = 128


def _tri2(lower):
    r = lax.broadcasted_iota(jnp.int32, (CUM_BLOCK, CUM_BLOCK), 0)
    c = lax.broadcasted_iota(jnp.int32, (CUM_BLOCK, CUM_BLOCK), 1)
    tri = ((r > c) if lower else (r < c)).astype(BF16)
    return jnp.concatenate([tri, tri], axis=0)


def _run_sums(x, tri2, run, reverse):
    nb = x.shape[1] // CUM_BLOCK
    outs = [None] * nb
    for j in (reversed(range(nb)) if reverse else range(nb)):
        xj = x[:, j * CUM_BLOCK:(j + 1) * CUM_BLOCK]
        hi = xj.astype(BF16)
        lo = (xj - hi.astype(F32)).astype(BF16)
        outs[j] = _dot(jnp.concatenate([hi, lo], axis=1), tri2) + run
        run = run + jnp.sum(xj, axis=1, keepdims=True)
    return jnp.concatenate(outs, axis=1), run


def _log_sigmoid(z):
    return jnp.minimum(z, 0.0) - jnp.log(1.0 + jnp.exp(-jnp.abs(z)))


def _heads_in(ref, h, width=HEAD):
    return ref[:, h * width:(h + 1) * width]


def _sb_qk(q, k, gq, gk):
    return _rms(q, gq) * (HEAD ** -0.5), _rms(k, gk)


def _sb_prep_fwd(qkv, gq, gk):
    t = qkv.shape[0]

    def body(x_ref, gq_ref, gk_ref, q_ref, k_ref, v_ref):
        for h in range(N_HEADS):
            q, k = _sb_qk(_heads_in(x_ref, h), _heads_in(x_ref, N_HEADS + h), gq_ref[...], gk_ref[...])
            q_ref[h] = q.astype(BF16)
            k_ref[h] = k.astype(BF16)
            v_ref[h] = _heads_in(x_ref, 2 * N_HEADS + h).astype(BF16)

    hm = _sds((N_HEADS, t, HEAD), BF16)
    return _rows("sb_prep", body, [qkv], [hm, hm, hm], tt=256, consts=[gq, gk])


def _sb_prep_bwd(qkv, gq, gk, dq, dk, dv):
    def body(x_ref, dq_ref, dk_ref, dv_ref, gq_ref, gk_ref, dx_ref, dgq_ref, dgk_ref):
        dgq = jnp.zeros(gq_ref.shape, F32)
        dgk = jnp.zeros(gk_ref.shape, F32)
        for h in range(N_HEADS):
            _, vjp = jax.vjp(_sb_qk, _heads_in(x_ref, h), _heads_in(x_ref, N_HEADS + h), gq_ref[...], gk_ref[...])
            a, b, c, d = vjp((dq_ref[h], dk_ref[h]))
            dx_ref[:, h * HEAD:(h + 1) * HEAD] = a.astype(BF16)
            dx_ref[:, (N_HEADS + h) * HEAD:(N_HEADS + h + 1) * HEAD] = b.astype(BF16)
            dx_ref[:, (2 * N_HEADS + h) * HEAD:(2 * N_HEADS + h + 1) * HEAD] = dv_ref[h].astype(BF16)
            dgq, dgk = dgq + c, dgk + d
        _acc(dgq_ref, dgq)
        _acc(dgk_ref, dgk)

    return _rows("sb_prep_bwd", body, [qkv, dq, dk, dv], [_sds(qkv.shape, BF16)], tt=256, consts=[gq, gk],
                 accs=[_sds(gq.shape), _sds(gk.shape)])


def _q_block(t):
    return min(ATT_Q, t)


def _key_order(bq, qb):
    rows = lax.broadcasted_iota(jnp.int32, (bq, ATT_BLOCK), 0)
    cols = lax.broadcasted_iota(jnp.int32, (bq, ATT_BLOCK), 1)
    return rows - cols + qb * bq


def _sb_attn_fwd(q, k, v):
    nh, t, _ = q.shape
    bq = _q_block(t)
    per = bq // ATT_BLOCK

    def kern(q_ref, k_ref, v_ref, o_ref):
        qb = pl.program_id(1)
        qv = q_ref[0]
        after = _tri2(True)
        order = _key_order(bq, qb)
        nkb = (qb + 1) * per

        def body(i, carry, diagonal):
            o_acc, run = carry
            kb = nkb - 1 - i
            off = pl.multiple_of(kb * ATT_BLOCK, ATT_BLOCK)
            kv = k_ref[0, pl.ds(off, ATT_BLOCK), :]
            vv = v_ref[0, pl.ds(off, ATT_BLOCK), :]
            z = _dot_nt(qv, kv)
            lsz = _log_sigmoid(z)
            lsn = lsz - z
            if diagonal:
                past = order > kb * ATT_BLOCK
                lsn = jnp.where(past, lsn, 0.0)
            la, run = _run_sums(lsn, after, run, True)
            a = jnp.exp(lsz + la)
            if diagonal:
                a = jnp.where(past, a, 0.0)
            o_acc = o_acc + _dot(a.astype(BF16), vv)
            return o_acc, run

        carry = lax.fori_loop(0, per, lambda i, c: body(i, c, True), (jnp.zeros((bq, HEAD), F32), jnp.zeros((bq, 1), F32)))
        o, _ = lax.fori_loop(per, nkb, lambda i, c: body(i, c, False), carry)
        o_ref[...] = o

    return pl.pallas_call(
        kern, name="sb_attn_fwd", grid=(nh, t // bq),
        in_specs=[pl.BlockSpec((1, bq, HEAD), lambda h, i: (h, i, 0)),
                  pl.BlockSpec((1, t, HEAD), lambda h, i: (h, 0, 0)),
                  pl.BlockSpec((1, t, HEAD), lambda h, i: (h, 0, 0))],
        out_specs=pl.BlockSpec((bq, HEAD), lambda h, i: (i, h)),
        out_shape=_sds((t, nh * HEAD)),
        compiler_params=_cparams(dimension_semantics=("parallel", "arbitrary")),
    )(q, k, v)


def _sb_attn_bwd(q, k, v, do):
    nh, t, _ = q.shape
    bq = _q_block(t)
    per = bq // ATT_BLOCK

    def kern(q_ref, k_ref, v_ref, do_ref, dq_ref, dk_ref, dv_ref, g_s, ls_s):
        qb = pl.program_id(1)

        @pl.when(qb == 0)
        def _():
            dk_ref[...] = jnp.zeros(dk_ref.shape, F32)
            dv_ref[...] = jnp.zeros(dv_ref.shape, F32)

        qv = q_ref[0]
        dob = do_ref[...].astype(BF16)
        after, before = _tri2(True), _tri2(False)
        order = _key_order(bq, qb)
        nkb = (qb + 1) * per

        def sweep_left(i, run, diagonal):
            kb = nkb - 1 - i
            off = pl.multiple_of(kb * ATT_BLOCK, ATT_BLOCK)
            kv = k_ref[0, pl.ds(off, ATT_BLOCK), :]
            vv = v_ref[0, pl.ds(off, ATT_BLOCK), :]
            z = _dot_nt(qv, kv)
            lsz = _log_sigmoid(z)
            lsn = lsz - z
            if diagonal:
                past = order > kb * ATT_BLOCK
                lsn = jnp.where(past, lsn, 0.0)
            la, run = _run_sums(lsn, after, run, True)
            a = jnp.exp(lsz + la)
            if diagonal:
                a = jnp.where(past, a, 0.0)
            g_s[kb] = _dot_nt(dob, vv) * a
            ls_s[kb] = lsz
            dv_ref[0, pl.ds(off, ATT_BLOCK), :] += _dot_tn(a.astype(BF16), dob)
            return run

        zero = jnp.zeros((bq, 1), F32)
        run = lax.fori_loop(0, per, lambda i, c: sweep_left(i, c, True), zero)
        lax.fori_loop(per, nkb, lambda i, c: sweep_left(i, c, False), run)

        def sweep_right(kb, carry, diagonal):
            dq_acc, run_g = carry
            off = pl.multiple_of(kb * ATT_BLOCK, ATT_BLOCK)
            kv = k_ref[0, pl.ds(off, ATT_BLOCK), :]
            g = g_s[kb]
            sg = jnp.exp(ls_s[kb])
            dls, run_g = _run_sums(g, before, run_g, False)
            dz = g * (1.0 - sg) - dls * sg
            if diagonal:
                dz = jnp.where(order > kb * ATT_BLOCK, dz, 0.0)
            dzb = dz.astype(BF16)
            dk_ref[0, pl.ds(off, ATT_BLOCK), :] += _dot_tn(dzb, qv)
            return dq_acc + _dot(dzb, kv), run_g

        carry = lax.fori_loop(0, nkb - per, lambda i, c: sweep_right(i, c, False), (jnp.zeros((bq, HEAD), F32), zero))
        dq, _ = lax.fori_loop(nkb - per, nkb, lambda i, c: sweep_right(i, c, True), carry)
        dq_ref[0] = dq

    hm = _sds((nh, t, HEAD))
    full = pl.BlockSpec((1, t, HEAD), lambda h, i: (h, 0, 0))
    tok = pl.BlockSpec((bq, HEAD), lambda h, i: (i, h))
    nkb_max = t // ATT_BLOCK
    return pl.pallas_call(
        kern, name="sb_attn_bwd", grid=(nh, t // bq),
        in_specs=[pl.BlockSpec((1, bq, HEAD), lambda h, i: (h, i, 0)), full, full, tok],
        out_specs=[pl.BlockSpec((1, bq, HEAD), lambda h, i: (h, i, 0)), full, full],
        out_shape=[hm, hm, hm],
        scratch_shapes=[pltpu.VMEM((nkb_max, bq, ATT_BLOCK), F32), pltpu.VMEM((nkb_max, bq, ATT_BLOCK), F32)],
        compiler_params=_cparams(dimension_semantics=("parallel", "arbitrary")),
    )(q, k, v, do)


def _sb_fwd(x, h, w_qkv, gq, gk, w_out):
    qkv = _mm("sb_qkv", h, w_qkv)
    q, k, v = _sb_prep_fwd(qkv, gq, gk)
    o = _sb_attn_fwd(q, k, v)
    y = _mm("sb_out", o, w_out, add=x)
    return y, (h, qkv, q, k, v, o)


def _sb_bwd(saved, dy, w_qkv, gq, gk, w_out):
    h, qkv, q, k, v, o = saved
    d_wout = _mm("sb_out_wgrad", o, dy, ta=True, out_dtype=GRAD_DTYPE)
    do = _mm("sb_out_dgrad", dy, w_out, tb=True)
    dq, dk, dv = _sb_attn_bwd(q, k, v, do)
    dqkv, dgq, dgk = _sb_prep_bwd(qkv, gq, gk, dq, dk, dv)
    d_wqkv = _mm("sb_qkv_wgrad", h, dqkv, ta=True, out_dtype=GRAD_DTYPE)
    dh = _mm("sb_qkv_dgrad", dqkv, w_qkv, tb=True)
    return dh, d_wqkv, dgq, dgk, d_wout


DN_QKV = 3 * N_HEADS * HEAD
DN_PROJ = DN_QKV + N_HEADS * HEAD + LANES
DN_CONV = 4
HALO = 8
CONV_COLS = 512


def _dn_conv_fwd(proj, conv_w, tt=256):
    t = proj.shape[0]
    tt = min(tt, t)

    def body(u_ref, prev_ref, w_ref, c_ref):
        i = pl.program_id(0)
        for cc in range(DN_QKV // CONV_COLS):
            cs = slice(cc * CONV_COLS, (cc + 1) * CONV_COLS)
            cur = u_ref[:, cs]
            prev = jnp.where(i > 0, prev_ref[:, cs], 0.0)
            ext = jnp.concatenate([prev, cur], axis=0)
            y = cur * w_ref[DN_CONV - 1:DN_CONV, cs]
            for j in range(DN_CONV - 1):
                y = y + pltpu.roll(ext, DN_CONV - 1 - j, 0)[HALO:] * w_ref[j:j + 1, cs]
            c_ref[:, cs] = y

    return _rows("dn_conv", body,
                 [(proj, (tt, DN_QKV), lambda i: (i, 0)),
                  (proj, (HALO, DN_QKV), lambda i: (jnp.maximum(i * (tt // HALO) - 1, 0), 0))],
                 [_sds((t, DN_QKV))], tt=tt, consts=[conv_w])[0]


def _dn_conv_bwd(proj, conv_w, dc, dz, dab, tt=256):
    t = proj.shape[0]
    tt = min(tt, t)
    nblk = t // tt

    def body(u_ref, prev_ref, dc_ref, next_ref, dz_ref, dab_ref, w_ref, dp_ref, dw_ref):
        i = pl.program_id(0)
        dws = []
        for cc in range(DN_QKV // CONV_COLS):
            cs = slice(cc * CONV_COLS, (cc + 1) * CONV_COLS)
            cur = u_ref[:, cs]
            prev = jnp.where(i > 0, prev_ref[:, cs], 0.0)
            ext_u = jnp.concatenate([prev, cur], axis=0)
            d = dc_ref[:, cs]
            nxt = jnp.where(i < nblk - 1, next_ref[:, cs], 0.0)
            ext_d = jnp.concatenate([d, nxt], axis=0)
            du = d * w_ref[DN_CONV - 1:DN_CONV, cs]
            rows = [jnp.sum(d * cur, axis=0, keepdims=True)]
            for j in range(DN_CONV - 2, -1, -1):
                sh = DN_CONV - 1 - j
                du = du + pltpu.roll(ext_d, tt + HALO - sh, 0)[:tt] * w_ref[j:j + 1, cs]
                rows.insert(0, jnp.sum(d * pltpu.roll(ext_u, sh, 0)[HALO:], axis=0, keepdims=True))
            dp_ref[:, cs] = du.astype(BF16)
            dws.append(jnp.concatenate(rows, axis=0))
        dp_ref[:, DN_QKV:DN_QKV + N_HEADS * HEAD] = dz_ref[...].astype(BF16)
        dp_ref[:, DN_QKV + N_HEADS * HEAD:] = dab_ref[...].astype(BF16)
        _acc(dw_ref, jnp.concatenate(dws, axis=1))

    return _rows("dn_conv_bwd", body,
                 [(proj, (tt, DN_QKV), lambda i: (i, 0)),
                  (proj, (HALO, DN_QKV), lambda i: (jnp.maximum(i * (tt // HALO) - 1, 0), 0)),
                  dc,
                  (dc, (HALO, DN_QKV), lambda i: (jnp.minimum((i + 1) * (tt // HALO), t // HALO - 1), 0)),
                  dz, dab],
                 [_sds((t, DN_PROJ), BF16)], tt=tt, consts=[conv_w], accs=[_sds(conv_w.shape)])


def _l2n(x):
    return x * lax.rsqrt(jnp.sum(x * x, axis=-1, keepdims=True) + NORM_EPS)


def _dn_qkv(cq, ck, cv):
    return _l2n(_silu(cq)) * (HEAD ** -0.5), _l2n(_silu(ck)), _silu(cv)


def _dn_gates(ab, a_log, dt_bias):
    lane = lax.broadcasted_iota(jnp.int32, ab.shape, 1)
    g = -jnp.exp(a_log) * _softplus(ab + dt_bias)
    return jnp.where(lane < N_HEADS, g, jnp.where(lane < 2 * N_HEADS, _sigmoid(ab), 0.0))


def _ab_spec(tt):
    return (tt, LANES), lambda i: (i, DN_PROJ // LANES - 1)


def _dn_prep_fwd(c, proj, a_log, dt_bias, tt=256):
    t = c.shape[0]
    tt = min(tt, t)

    def body(c_ref, ab_ref, al_ref, dt_ref, q_ref, k_ref, v_ref, g_ref):
        for h in range(N_HEADS):
            q_ref[h], k_ref[h], v_ref[h] = _dn_qkv(_heads_in(c_ref, h), _heads_in(c_ref, N_HEADS + h), _heads_in(c_ref, 2 * N_HEADS + h))
        g_ref[...] = _dn_gates(ab_ref[...], al_ref[...], dt_ref[...])

    hm = _sds((N_HEADS, t, HEAD))
    return _rows("dn_prep", body, [c, (proj,) + _ab_spec(tt)], [hm, hm, hm, _sds((t, LANES))], tt=tt, consts=[a_log, dt_bias])


def _dn_prep_bwd(c, proj, a_log, dt_bias, dq, dk, dv, dgates, tt=256):
    t = c.shape[0]
    tt = min(tt, t)

    def body(c_ref, ab_ref, dq_ref, dk_ref, dv_ref, dg_ref, al_ref, dt_ref, dc_ref, dab_ref, dal_ref, ddt_ref):
        for h in range(N_HEADS):
            _, vjp = jax.vjp(_dn_qkv, _heads_in(c_ref, h), _heads_in(c_ref, N_HEADS + h), _heads_in(c_ref, 2 * N_HEADS + h))
            a, b, d = vjp((dq_ref[h], dk_ref[h], dv_ref[h]))
            dc_ref[:, h * HEAD:(h + 1) * HEAD] = a
            dc_ref[:, (N_HEADS + h) * HEAD:(N_HEADS + h + 1) * HEAD] = b
            dc_ref[:, (2 * N_HEADS + h) * HEAD:(2 * N_HEADS + h + 1) * HEAD] = d
        _, vjp = jax.vjp(_dn_gates, ab_ref[...], al_ref[...], dt_ref[...])
        dab, dal, ddt = vjp(dg_ref[...])
        dab_ref[...] = dab
        _acc(dal_ref, dal)
        _acc(ddt_ref, ddt)

    return _rows("dn_prep_bwd", body, [c, (proj,) + _ab_spec(tt), dq, dk, dv, dgates], [_sds(c.shape), _sds((t, LANES))],
                 tt=tt, consts=[a_log, dt_bias], accs=[_sds(a_log.shape), _sds(dt_bias.shape)])


def _bdot(a, b, prec=None):
    return lax.dot_general(a, b, (((2,), (1,)), ((0,), (0,))), precision=prec, preferred_element_type=F32)


def _bdot_nt(a, b, prec=None):
    return lax.dot_general(a, b, (((2,), (2,)), ((0,), (0,))), precision=prec, preferred_element_type=F32)


def _bdot_tn(a, b, prec=None):
    return lax.dot_general(a, b, (((1,), (1,)), ((0,), (0,))), precision=prec, preferred_element_type=F32)


def _inv_raw(low):
    c = low.shape[-1]
    r = lax.broadcasted_iota(jnp.int32, (c, c), 0)
    s = lax.broadcasted_iota(jnp.int32, (c, c), 1)
    m = jnp.where(r == s, 1.0, 0.0) - low
    p = _bdot(low, low, F32X3)
    n_fac = int(math.log2(c)) - 1
    for i in range(n_fac):
        m = m + _bdot(m, p, F32X3)
        if i < n_fac - 1:
            p = _bdot(p, p, F32X3)
    return m


@jax.custom_vjp
def _inv_unit_lower(low):
    return _inv_raw(low)


def _inv_fwd(low):
    m = _inv_raw(low)
    return m, m


def _inv_bwd(m, dm):
    return (-_bdot_nt(_bdot_tn(m, dm, F32X3), m, F32X3),)


_inv_unit_lower.defvjp(_inv_fwd, _inv_bwd)


def _dn_chunk(q, k, v, gates, s):
    nh, rows, _ = q.shape
    c = DN_CHUNK
    nc = rows // c
    nb = nh * nc
    lane = lax.broadcasted_iota(jnp.int32, gates.shape, 1)

    def column(j):
        return jnp.sum(jnp.where(lane == j, gates, 0.0), axis=1, keepdims=True)[None]

    def fold(x):
        return x.reshape((nb, c) + x.shape[2:])

    g_col = fold(jnp.concatenate([column(h) for h in range(nh)], axis=0))
    b_col = fold(jnp.concatenate([column(h + nh) for h in range(nh)], axis=0))
    q, k, v = fold(q), fold(k), fold(v)
    r = lax.broadcasted_iota(jnp.int32, (c, c), 0)
    cc = lax.broadcasted_iota(jnp.int32, (c, c), 1)
    causal, strict = r >= cc, r > cc
    incl = jnp.broadcast_to(jnp.where(causal, 1.0, 0.0), (nb, c, c))
    upper = jnp.broadcast_to(jnp.where(r <= cc, 1.0, 0.0), (nb, c, c))
    gb = jnp.broadcast_to(g_col, (nb, c, LANES))
    gbc = jnp.broadcast_to(g_col, (nb, c, c))
    gc = _bdot(incl, gb, F32X3)
    gc_r = _bdot(incl, gbc, F32X3)
    gc_c = _bdot_tn(gbc, upper, F32X3)
    decay = jnp.where(causal, jnp.exp(jnp.where(causal, gc_r - gc_c, 0.0)), 0.0)
    kb = k * b_col
    low = jnp.where(strict, _bdot_nt(kb, k) * decay, 0.0)
    m = _inv_unit_lower(low)
    egc = jnp.exp(gc)
    gl = jnp.sum(gb, axis=1, keepdims=True)
    local = (_bdot(m, v * b_col, F32X3), _bdot(m, kb * egc, F32X3), _bdot_nt(q, k) * decay, q * egc,
             k * jnp.exp(gl - gc), jnp.exp(gl))
    outs = []
    for i in range(nc):
        u, w, attn, q_dec, k_dec, cd = (x.reshape((nh, nc) + x.shape[1:])[:, i] for x in local)
        v_new = u - _bdot(w, s)
        outs.append(_bdot(q_dec, s) + _bdot(attn, v_new))
        s = s * cd + _bdot_tn(k_dec, v_new)
    return jnp.concatenate(outs, axis=1), s


def _dn_chunks_fwd(q, k, v, gates):
    nh, t, _ = q.shape
    n = t // DN_STEP

    def kern(q_ref, k_ref, v_ref, g_ref, o_ref, sin_ref, s_scr):
        @pl.when(pl.program_id(0) == 0)
        def _():
            s_scr[...] = jnp.zeros(s_scr.shape, F32)

        s = s_scr[...]
        sin_ref[0] = s
        o_ref[...], s_scr[...] = _dn_chunk(q_ref[...], k_ref[...], v_ref[...], g_ref[...], s)

    blk = pl.BlockSpec((nh, DN_STEP, HEAD), lambda i: (0, i, 0))
    return pl.pallas_call(
        kern, name="dn_chunks_fwd", grid=(n,),
        in_specs=[blk, blk, blk, pl.BlockSpec((DN_STEP, LANES), lambda i: (i, 0))],
        out_specs=[blk, pl.BlockSpec((1, nh, HEAD, HEAD), lambda i: (i, 0, 0, 0))],
        out_shape=[_sds((nh, t, HEAD)), _sds((n, nh, HEAD, HEAD))],
        scratch_shapes=[pltpu.VMEM((nh, HEAD, HEAD), F32)],
        compiler_params=_cparams(dimension_semantics=("arbitrary",)),
    )(q, k, v, gates)


def _dn_chunks_bwd(q, k, v, gates, s_in, do):
    nh, t, _ = q.shape
    n = t // DN_STEP

    def kern(q_ref, k_ref, v_ref, g_ref, sin_ref, do_ref, dq_ref, dk_ref, dv_ref, dg_ref, ds_scr):
        @pl.when(pl.program_id(0) == 0)
        def _():
            ds_scr[...] = jnp.zeros(ds_scr.shape, F32)

        _, vjp = jax.vjp(_dn_chunk, q_ref[...], k_ref[...], v_ref[...], g_ref[...], sin_ref[0])
        dq_ref[...], dk_ref[...], dv_ref[...], dg_ref[...], ds_scr[...] = vjp((do_ref[...], ds_scr[...]))

    blk = pl.BlockSpec((nh, DN_STEP, HEAD), lambda i: (0, n - 1 - i, 0))
    gblk = pl.BlockSpec((DN_STEP, LANES), lambda i: (n - 1 - i, 0))
    hm = _sds((nh, t, HEAD))
    return pl.pallas_call(
        kern, name="dn_chunks_bwd", grid=(n,),
        in_specs=[blk, blk, blk, gblk, pl.BlockSpec((1, nh, HEAD, HEAD), lambda i: (n - 1 - i, 0, 0, 0)), blk],
        out_specs=[blk, blk, blk, gblk],
        out_shape=[hm, hm, hm, _sds((t, LANES))],
        scratch_shapes=[pltpu.VMEM((nh, HEAD, HEAD), F32)],
        compiler_params=_cparams(dimension_semantics=("arbitrary",)),
    )(q, k, v, gates, s_in, do)


def _dn_gate_out(o, z, g):
    return _rms(o, g) * _silu(z)


def _z_spec(tt):
    return (tt, N_HEADS * HEAD), lambda i: (i, DN_QKV // (N_HEADS * HEAD))


def _dn_post_fwd(o, proj, out_norm, tt=256):
    t = o.shape[1]
    tt = min(tt, t)

    def body(o_ref, z_ref, g_ref, y_ref):
        for h in range(N_HEADS):
            y_ref[:, h * HEAD:(h + 1) * HEAD] = _dn_gate_out(o_ref[h], _heads_in(z_ref, h), g_ref[...]).astype(BF16)

    return _rows("dn_post", body, [o, (proj,) + _z_spec(tt)], [_sds((t, N_HEADS * HEAD), BF16)], tt=tt, consts=[out_norm])[0]


def _dn_post_bwd(o, proj, out_norm, dy, tt=256):
    t = o.shape[1]
    tt = min(tt, t)

    def body(o_ref, z_ref, dy_ref, g_ref, do_ref, dz_ref, dg_ref):
        dg = jnp.zeros(g_ref.shape, F32)
        for h in range(N_HEADS):
            _, vjp = jax.vjp(_dn_gate_out, o_ref[h], _heads_in(z_ref, h), g_ref[...])
            a, b, d = vjp(_heads_in(dy_ref, h))
            do_ref[h] = a
            dz_ref[:, h * HEAD:(h + 1) * HEAD] = b
            dg = dg + d
        _acc(dg_ref, dg)

    return _rows("dn_post_bwd", body, [o, (proj,) + _z_spec(tt), dy], [_sds(o.shape), _sds((t, N_HEADS * HEAD))], tt=tt,
                 consts=[out_norm], accs=[_sds(out_norm.shape)])


def _dn_fwd(x, h, w_in, conv_w, a_log, dt_bias, out_norm, w_out):
    proj = _mm("dn_in", h, w_in)
    c = _dn_conv_fwd(proj, conv_w)
    q, k, v, gates = _dn_prep_fwd(c, proj, a_log, dt_bias)
    o, s_in = _dn_chunks_fwd(q, k, v, gates)
    on = _dn_post_fwd(o, proj, out_norm)
    y = _mm("dn_out", on, w_out, add=x)
    return y, (h, proj, c, q, k, v, gates, o, s_in, on)


def _dn_bwd(saved, dy, w_in, conv_w, a_log, dt_bias, out_norm, w_out):
    h, proj, c, q, k, v, gates, o, s_in, on = saved
    d_wout = _mm("dn_out_wgrad", on, dy, ta=True, out_dtype=GRAD_DTYPE)
    don = _mm("dn_out_dgrad", dy, w_out, tb=True)
    do, dz, d_out_norm = _dn_post_bwd(o, proj, out_norm, don)
    dq, dk, dv, dgates = _dn_chunks_bwd(q, k, v, gates, s_in, do)
    dc, dab, d_a_log, d_dt_bias = _dn_prep_bwd(c, proj, a_log, dt_bias, dq, dk, dv, dgates)
    dproj, d_conv_w = _dn_conv_bwd(proj, conv_w, dc, dz, dab)
    d_win = _mm("dn_in_wgrad", h, dproj, ta=True, out_dtype=GRAD_DTYPE)
    dh = _mm("dn_in_dgrad", dproj, w_in, tb=True)
    return dh, d_win, d_conv_w, d_a_log, d_dt_bias, d_out_norm, d_wout


MLA_SCALE = MLA_QK ** -0.5
MLA_C = 512


def _swap_raw(x):
    lane = lax.broadcasted_iota(jnp.int32, x.shape, 1)
    half = MLA_ROPE // 2
    y = jnp.where(lane < half, pltpu.roll(x, LANES - half, 1), pltpu.roll(x, half, 1))
    return jnp.where(lane < MLA_ROPE, y, 0.0)


@jax.custom_vjp
def _swap_halves(x):
    return _swap_raw(x)


_swap_halves.defvjp(lambda x: (_swap_raw(x), None), lambda _, d: (_swap_raw(d),))


def _rms_rope(x, g, cos, sin):
    y = x * lax.rsqrt(jnp.sum(x * x, axis=-1, keepdims=True) * (1.0 / MLA_ROPE) + NORM_EPS) * g
    return y * cos + _swap_halves(y) * sin


def _mla_latent(cq, ckv, kr, gq, gkv, gkr, cos, sin):
    return _rms(cq, gq), _rms(ckv, gkv), _rms_rope(kr, gkr, cos, sin)


def _mla_prep1_fwd(c, gq, gkv, gkr, cos, sin):
    t = c.shape[0]

    def body(c_ref, cos_ref, sin_ref, gq_ref, gkv_ref, gkr_ref, cq_ref, ckv_ref, kr_ref):
        a, b, r = _mla_latent(c_ref[:, :256], c_ref[:, 256:384], c_ref[:, 384:], gq_ref[...], gkv_ref[...], gkr_ref[...],
                              cos_ref[...], sin_ref[...])
        cq_ref[...] = a.astype(BF16)
        ckv_ref[...] = b.astype(BF16)
        kr_ref[...] = r.astype(BF16)

    return _rows("mla_prep1", body, [c, cos, sin], [_sds((t, 256), BF16), _sds((t, HEAD), BF16), _sds((t, HEAD), BF16)],
                 tt=512, consts=[gq, gkv, gkr])


def _mla_prep1_bwd(c, gq, gkv, gkr, cos, sin, dcq, dckv, dkr_heads):
    def body(c_ref, cos_ref, sin_ref, dcq_ref, dckv_ref, dkr_ref, gq_ref, gkv_ref, gkr_ref, dc_ref, dgq_ref, dgkv_ref, dgkr_ref):
        dkr = dkr_ref[0]
        for h in range(1, N_HEADS):
            dkr = dkr + dkr_ref[h]
        _, vjp = jax.vjp(_mla_latent, c_ref[:, :256], c_ref[:, 256:384], c_ref[:, 384:], gq_ref[...], gkv_ref[...], gkr_ref[...],
                         cos_ref[...], sin_ref[...])
        a, b, r, d1, d2, d3, _, _ = vjp((dcq_ref[...], dckv_ref[...], dkr))
        dc_ref[:, :256] = a.astype(BF16)
        dc_ref[:, 256:384] = b.astype(BF16)
        dc_ref[:, 384:] = r.astype(BF16)
        _acc(dgq_ref, d1)
        _acc(dgkv_ref, d2)
        _acc(dgkr_ref, d3)

    return _rows("mla_prep1_bwd", body, [c, cos, sin, dcq, dckv, dkr_heads], [_sds(c.shape, BF16)], tt=512,
                 consts=[gq, gkv, gkr], accs=[_sds(gq.shape), _sds(gkv.shape), _sds(gkr.shape)])


def _mla_heads(qn, qr, kn, gqn, gqr, gkn, cos, sin):
    return _rms(qn, gqn) * MLA_SCALE, _rms_rope(qr, gqr, cos, sin) * MLA_SCALE, _rms(kn, gkn)


def _mla_prep2_fwd(qa, kv, gqn, gqr, gkn, cos, sin):
    t = qa.shape[0]

    def body(qa_ref, kv_ref, cos_ref, sin_ref, gqn_ref, gqr_ref, gkn_ref, qn_ref, qr_ref, kn_ref, v_ref):
        for h in range(N_HEADS):
            a, b, c = _mla_heads(_heads_in(qa_ref, h), _heads_in(qa_ref, N_HEADS + h), _heads_in(kv_ref, h),
                                 gqn_ref[...], gqr_ref[...], gkn_ref[...], cos_ref[...], sin_ref[...])
            qn_ref[h] = a.astype(BF16)
            qr_ref[h] = b.astype(BF16)
            kn_ref[h] = c.astype(BF16)
            v_ref[h] = _heads_in(kv_ref, N_HEADS + h).astype(BF16)

    hm = _sds((N_HEADS, t, HEAD), BF16)
    return _rows("mla_prep2", body, [qa, kv, cos, sin], [hm, hm, hm, hm], tt=256, consts=[gqn, gqr, gkn])


def _mla_prep2_bwd(qa, kv, gqn, gqr, gkn, cos, sin, dqn, dqr, dkn, dv):
    def body(qa_ref, kv_ref, cos_ref, sin_ref, dqn_ref, dqr_ref, dkn_ref, dv_ref, gqn_ref, gqr_ref, gkn_ref,
             dqa_ref, dkv_ref, d1_ref, d2_ref, d3_ref):
        d1 = jnp.zeros(gqn_ref.shape, F32)
        d2 = jnp.zeros(gqr_ref.shape, F32)
        d3 = jnp.zeros(gkn_ref.shape, F32)
        for h in range(N_HEADS):
            _, vjp = jax.vjp(_mla_heads, _heads_in(qa_ref, h), _heads_in(qa_ref, N_HEADS + h), _heads_in(kv_ref, h),
                             gqn_ref[...], gqr_ref[...], gkn_ref[...], cos_ref[...], sin_ref[...])
            a, b, c, e1, e2, e3, _, _ = vjp((dqn_ref[h], dqr_ref[h], dkn_ref[h]))
            dqa_ref[:, h * HEAD:(h + 1) * HEAD] = a.astype(BF16)
            dqa_ref[:, (N_HEADS + h) * HEAD:(N_HEADS + h + 1) * HEAD] = b.astype(BF16)
            dkv_ref[:, h * HEAD:(h + 1) * HEAD] = c.astype(BF16)
            dkv_ref[:, (N_HEADS + h) * HEAD:(N_HEADS + h + 1) * HEAD] = dv_ref[h].astype(BF16)
            d1, d2, d3 = d1 + e1, d2 + e2, d3 + e3
        _acc(d1_ref, d1)
        _acc(d2_ref, d2)
        _acc(d3_ref, d3)

    return _rows("mla_prep2_bwd", body, [qa, kv, cos, sin, dqn, dqr, dkn, dv], [_sds(qa.shape, BF16), _sds(kv.shape, BF16)],
                 tt=256, consts=[gqn, gqr, gkn], accs=[_sds(gqn.shape), _sds(gqr.shape), _sds(gkn.shape)])


def _mla_attn_fwd(qn, qr, kn, kr, v):
    nh, t, _ = qn.shape
    bq = _q_block(t)
    per = bq // ATT_BLOCK

    def kern(qn_ref, qr_ref, kn_ref, kr_ref, v_ref, o_ref, lse_ref):
        qb = pl.program_id(1)
        qv = jnp.concatenate([qn_ref[0], qr_ref[0]], axis=1)
        order = _key_order(bq, qb)

        def body(kb, carry, diagonal):
            acc, m, l = carry
            off = pl.multiple_of(kb * ATT_BLOCK, ATT_BLOCK)
            kv = jnp.concatenate([kn_ref[0, pl.ds(off, ATT_BLOCK), :], kr_ref[pl.ds(off, ATT_BLOCK), :]], axis=1)
            s = _dot_nt(qv, kv)
            if diagonal:
                s = jnp.where(order >= kb * ATT_BLOCK, s, -jnp.inf)
            m_new = jnp.maximum(m, jnp.max(s, axis=1, keepdims=True))
            alpha = jnp.exp(m - m_new)
            p = jnp.exp(s - m_new)
            acc = acc * alpha + _dot(p.astype(BF16), v_ref[0, pl.ds(off, ATT_BLOCK), :])
            return acc, m_new, l * alpha + jnp.sum(p, axis=1, keepdims=True)

        init = (jnp.zeros((bq, HEAD), F32), jnp.full((bq, 1), -jnp.inf, F32), jnp.zeros((bq, 1), F32))
        carry = lax.fori_loop(0, qb * per, lambda i, c: body(i, c, False), init)
        acc, m, l = lax.fori_loop(qb * per, (qb + 1) * per, lambda i, c: body(i, c, True), carry)
        o_ref[...] = acc / l
        lse_ref[...] = jnp.broadcast_to(m + jnp.log(l), (bq, HEAD))

    blk = pl.BlockSpec((1, bq, HEAD), lambda h, i: (h, i, 0))
    full = pl.BlockSpec((1, t, HEAD), lambda h, i: (h, 0, 0))
    tok = pl.BlockSpec((bq, HEAD), lambda h, i: (i, h))
    return pl.pallas_call(
        kern, name="mla_attn_fwd", grid=(nh, t // bq),
        in_specs=[blk, blk, full, pl.BlockSpec((t, HEAD), lambda h, i: (0, 0)), full],
        out_specs=[tok, tok], out_shape=[_sds((t, nh * HEAD)), _sds((t, nh * HEAD))],
        compiler_params=_cparams(dimension_semantics=("parallel", "arbitrary")),
    )(qn, qr, kn, kr, v)


def _mla_attn_bwd(qn, qr, kn, kr, v, o, lse, do):
    nh, t, _ = qn.shape
    bq = _q_block(t)
    per = bq // ATT_BLOCK

    def kern(qn_ref, qr_ref, kn_ref, kr_ref, v_ref, o_ref, lse_ref, do_ref, dqn_ref, dqr_ref, dkn_ref, dkr_ref, dv_ref):
        qb = pl.program_id(1)

        @pl.when(qb == 0)
        def _():
            dkn_ref[...] = jnp.zeros(dkn_ref.shape, F32)
            dkr_ref[...] = jnp.zeros(dkr_ref.shape, F32)
            dv_ref[...] = jnp.zeros(dv_ref.shape, F32)

        qv = jnp.concatenate([qn_ref[0], qr_ref[0]], axis=1)
        dov = do_ref[...]
        dob = dov.astype(BF16)
        delta = jnp.sum(dov * o_ref[...], axis=1, keepdims=True)
        lse_col = lse_ref[:, :1]
        order = _key_order(bq, qb)

        def body(kb, dq, diagonal):
            off = pl.multiple_of(kb * ATT_BLOCK, ATT_BLOCK)
            kv = jnp.concatenate([kn_ref[0, pl.ds(off, ATT_BLOCK), :], kr_ref[pl.ds(off, ATT_BLOCK), :]], axis=1)
            vv = v_ref[0, pl.ds(off, ATT_BLOCK), :]
            p = jnp.exp(_dot_nt(qv, kv) - lse_col)
            if diagonal:
                p = jnp.where(order >= kb * ATT_BLOCK, p, 0.0)
            ds = (p * (_dot_nt(dob, vv) - delta)).astype(BF16)
            dk = _dot_tn(ds, qv)
            dkn_ref[0, pl.ds(off, ATT_BLOCK), :] += dk[:, :HEAD]
            dkr_ref[0, pl.ds(off, ATT_BLOCK), :] += dk[:, HEAD:]
            dv_ref[0, pl.ds(off, ATT_BLOCK), :] += _dot_tn(p.astype(BF16), dob)
            return dq + _dot(ds, kv)

        dq = lax.fori_loop(0, qb * per, lambda i, c: body(i, c, False), jnp.zeros((bq, 2 * HEAD), F32))
        dq = lax.fori_loop(qb * per, (qb + 1) * per, lambda i, c: body(i, c, True), dq)
        dqn_ref[0] = dq[:, :HEAD]
        dqr_ref[0] = dq[:, HEAD:]

    hm = _sds((nh, t, HEAD))
    blk = pl.BlockSpec((1, bq, HEAD), lambda h, i: (h, i, 0))
    full = pl.BlockSpec((1, t, HEAD), lambda h, i: (h, 0, 0))
    tok = pl.BlockSpec((bq, HEAD), lambda h, i: (i, h))
    return pl.pallas_call(
        kern, name="mla_attn_bwd", grid=(nh, t // bq),
        in_specs=[blk, blk, full, pl.BlockSpec((t, HEAD), lambda h, i: (0, 0)), full, tok, tok, tok],
        out_specs=[blk, blk, full, full, full], out_shape=[hm, hm, hm, hm, hm],
        compiler_params=_cparams(dimension_semantics=("parallel", "arbitrary")),
    )(qn, qr, kn, kr, v, o, lse, do)


def _rope_tables(t):
    inv_freq = ROPE_THETA ** (-jnp.arange(0, MLA_ROPE, 2, dtype=F32) / MLA_ROPE)
    ang = jnp.arange(t, dtype=F32)[:, None] * inv_freq[None, :]
    c, s = jnp.cos(ang), jnp.sin(ang)
    pad = ((0, 0), (0, LANES - MLA_ROPE))
    return jnp.pad(jnp.concatenate([c, c], axis=1), pad), jnp.pad(jnp.concatenate([-s, s], axis=1), pad)


def _pad_lanes(v, n=LANES):
    return jnp.pad(v, (0, n - v.shape[0])).reshape(1, n)


def _mla_layout(w_down, w_uq, w_ukv):
    w_down_p = jnp.pad(w_down, ((0, 0), (0, MLA_C - w_down.shape[1])))
    uq = w_uq.reshape(w_uq.shape[0], N_HEADS, MLA_QK)
    rope = jnp.pad(uq[:, :, HEAD:], ((0, 0), (0, 0), (0, LANES - MLA_ROPE)))
    w_uq_p = jnp.concatenate([uq[:, :, :HEAD].reshape(-1, N_HEADS * HEAD), rope.reshape(-1, N_HEADS * LANES)], axis=1)
    ukv = w_ukv.reshape(w_ukv.shape[0], N_HEADS, 2 * HEAD)
    w_ukv_p = jnp.concatenate([ukv[:, :, :HEAD].reshape(-1, N_HEADS * HEAD), ukv[:, :, HEAD:].reshape(-1, N_HEADS * HEAD)], axis=1)
    return w_down_p, w_uq_p, w_ukv_p


def _mla_unlayout(d_down_p, d_uq_p, d_ukv_p):
    d_down = d_down_p[:, :256 + HEAD + MLA_ROPE]
    nope = d_uq_p[:, :N_HEADS * HEAD].reshape(-1, N_HEADS, HEAD)
    rope = d_uq_p[:, N_HEADS * HEAD:].reshape(-1, N_HEADS, LANES)[:, :, :MLA_ROPE]
    d_uq = jnp.concatenate([nope, rope], axis=2).reshape(-1, N_HEADS * MLA_QK)
    kn = d_ukv_p[:, :N_HEADS * HEAD].reshape(-1, N_HEADS, HEAD)
    vv = d_ukv_p[:, N_HEADS * HEAD:].reshape(-1, N_HEADS, HEAD)
    d_ukv = jnp.concatenate([kn, vv], axis=2).reshape(-1, N_HEADS * 2 * HEAD)
    return d_down, d_uq, d_ukv


def _mla_weight_shapes():
    return (_sds((1024, MLA_C), BF16), _sds((1, 256)), _sds((1, HEAD)), _sds((256, 2048), BF16), _sds((HEAD, 2048), BF16),
            _sds((1, HEAD)), _sds((1, HEAD)), _sds((1, HEAD)), _sds((1, HEAD)), _sds((1024, 1024), BF16),
            _sds((4096, HEAD)), _sds((4096, HEAD)))


def _mla_fwd(x, h, w_down, gq, gkv, w_uq, w_ukv, gqn, gqr, gkn, gkr, w_out, cos, sin):
    c = _mm("mla_down", h, w_down)
    cq, ckv, kr = _mla_prep1_fwd(c, gq, gkv, gkr, cos, sin)
    qa = _mm("mla_uq", cq, w_uq)
    kv = _mm("mla_ukv", ckv, w_ukv)
    qn, qr, kn, v = _mla_prep2_fwd(qa, kv, gqn, gqr, gkn, cos, sin)
    o, lse = _mla_attn_fwd(qn, qr, kn, kr, v)
    y = _mm("mla_out", o, w_out, add=x)
    return y, (h, c, cq, ckv, kr, qa, kv, qn, qr, kn, v, o, lse)


def _mla_bwd(saved, dy, w_down, gq, gkv, w_uq, w_ukv, gqn, gqr, gkn, gkr, w_out, cos, sin):
    h, c, cq, ckv, kr, qa, kv, qn, qr, kn, v, o, lse = saved
    d_wout = _mm("mla_out_wgrad", o, dy, ta=True, out_dtype=GRAD_DTYPE)
    do = _mm("mla_out_dgrad", dy, w_out, tb=True)
    dqn, dqr, dkn, dkr, dv = _mla_attn_bwd(qn, qr, kn, kr, v, o, lse, do)
    dqa, dkv, dgqn, dgqr, dgkn = _mla_prep2_bwd(qa, kv, gqn, gqr, gkn, cos, sin, dqn, dqr, dkn, dv)
    d_wuq = _mm("mla_uq_wgrad", cq, dqa, ta=True, out_dtype=GRAD_DTYPE)
    d_wukv = _mm("mla_ukv_wgrad", ckv, dkv, ta=True, out_dtype=GRAD_DTYPE)
    dcq = _mm("mla_uq_dgrad", dqa, w_uq, tb=True)
    dckv = _mm("mla_ukv_dgrad", dkv, w_ukv, tb=True)
    dc, dgq, dgkv, dgkr = _mla_prep1_bwd(c, gq, gkv, gkr, cos, sin, dcq, dckv, dkr)
    d_wdown = _mm("mla_down_wgrad", h, dc, ta=True, out_dtype=GRAD_DTYPE)
    dh = _mm("mla_down_dgrad", dc, w_down, tb=True)
    return dh, d_wdown, dgq, dgkv, d_wuq, d_wukv, dgqn, dgqr, dgkn, dgkr, d_wout


def _loss_head(y, target):
    d = y.shape[1]

    def body(y_ref, t_ref, dy_ref, l_ref):
        err = y_ref[...] - t_ref[...]
        dy_ref[...] = err * (1.0 / d)
        part = 0.5 * jnp.sum(jnp.sum(err * err, axis=1, keepdims=True) * (1.0 / d), axis=0, keepdims=True)
        _acc(l_ref, jnp.broadcast_to(part, (1, LANES)))

    return _rows("loss_head", body, [y, target], [_sds(y.shape)], tt=512, accs=[_sds((1, LANES))])


MESH_ID = pl.DeviceIdType.MESH
HBM_SPEC = pl.BlockSpec(memory_space=pltpu.HBM)


def _all_gather(name, x):
    m_per, n = x.shape

    def body(x_ref, out_ref, send_sems, recv_sems, local_sem):
        x, y, c = lax.axis_index("x"), lax.axis_index("y"), lax.axis_index("c")
        me, sibling = (x, y, c), (x, y, 1 - c)
        chips = [(1 - x, y), (x, 1 - y), (1 - x, 1 - y)]

        def rows(px, py, pc):
            return out_ref.at[pl.ds((4 * px + 2 * py + pc) * m_per, m_per), :]

        def copy(k, block, to, src=None):
            return pltpu.make_async_remote_copy(
                src_ref=rows(*block) if src is None else src, dst_ref=rows(*block),
                send_sem=send_sems.at[k], recv_sem=recv_sems.at[k], device_id=to, device_id_type=MESH_ID)

        mine = pltpu.make_async_copy(x_ref, rows(*me), local_sem)
        mine.start()
        first = [copy(0, me, sibling, src=x_ref)]
        first += [copy(1 + j, me, (*chip, c), src=x_ref) for j, chip in enumerate(chips)]
        for cp in first:
            cp.start()
        passed = [copy(4 + j, (*chip, c), sibling) for j, chip in enumerate(chips)]
        for j, chip in enumerate(chips):
            copy(1 + j, (*chip, c), me).wait_recv()
            passed[j].start()
        copy(0, sibling, me).wait_recv()
        for j, chip in enumerate(chips):
            copy(4 + j, (*chip, 1 - c), me).wait_recv()
        for cp in first + passed:
            cp.wait_send()
        mine.wait()

    return pl.pallas_call(
        body, name=name,
        out_shape=jax.ShapeDtypeStruct((N_DEV * m_per, n), x.dtype),
        in_specs=[HBM_SPEC], out_specs=HBM_SPEC,
        scratch_shapes=[pltpu.SemaphoreType.DMA((7,)), pltpu.SemaphoreType.DMA((7,)), pltpu.SemaphoreType.DMA],
    )(x)


def _all_gather_groups(name, xs):
    ng = len(xs)

    def body(*refs):
        x_refs, out_refs, token = refs[:ng], refs[ng:2 * ng], refs[2 * ng]
        send_sems, recv_sems, local_sems = refs[2 * ng + 1:]
        token[...] = jnp.zeros(token.shape, F32)
        x, y, c = lax.axis_index("x"), lax.axis_index("y"), lax.axis_index("c")
        me, sibling = (x, y, c), (x, y, 1 - c)
        chips = [(1 - x, y), (x, 1 - y), (1 - x, 1 - y)]

        def copy(g, k, block, to, src=None):
            px, py, pc = block
            dst = out_refs[g].at[4 * px + 2 * py + pc]
            return pltpu.make_async_remote_copy(
                src_ref=dst if src is None else src, dst_ref=dst,
                send_sem=send_sems.at[g, k], recv_sem=recv_sems.at[g, k], device_id=to, device_id_type=MESH_ID)

        mine = [pltpu.make_async_copy(x_refs[g], out_refs[g].at[4 * x + 2 * y + c], local_sems.at[g]) for g in range(ng)]
        for cp in mine:
            cp.start()
        first = []
        for g in range(ng):
            first.append(copy(g, 0, me, sibling, src=x_refs[g]))
            first += [copy(g, 1 + j, me, (*chip, c), src=x_refs[g]) for j, chip in enumerate(chips)]
        for cp in first:
            cp.start()
        passed = []
        for j, chip in enumerate(chips):
            for g in range(ng):
                copy(g, 1 + j, (*chip, c), me).wait_recv()
                passed.append(copy(g, 4 + j, (*chip, c), sibling))
                passed[-1].start()
        for g in range(ng):
            copy(g, 0, sibling, me).wait_recv()
            for j, chip in enumerate(chips):
                copy(g, 4 + j, (*chip, 1 - c), me).wait_recv()
        for cp in first + passed:
            cp.wait_send()
        for cp in mine:
            cp.wait()

    return pl.pallas_call(
        body, name=name,
        out_shape=[jax.ShapeDtypeStruct((N_DEV,) + x.shape, x.dtype) for x in xs] + [_sds((8, LANES))],
        in_specs=[HBM_SPEC] * ng, out_specs=[HBM_SPEC] * ng + [pl.BlockSpec(memory_space=pltpu.VMEM)],
        scratch_shapes=[pltpu.SemaphoreType.DMA((ng, 7)), pltpu.SemaphoreType.DMA((ng, 7)), pltpu.SemaphoreType.DMA((ng,))],
    )(*xs)


EFFECT = pltpu.SideEffectType.DATAFLOW_SIDE_EFFECTING
SEM_SPEC = pl.BlockSpec(memory_space=pltpu.SEMAPHORE)


def _push_copies(src_refs, land_refs, send_sems, recv_sems, chunked):
    x, y, c = lax.axis_index("x"), lax.axis_index("y"), lax.axis_index("c")
    me = 4 * x + 2 * y + c
    copies = []
    for g, (src, land) in enumerate(zip(src_refs, land_refs)):
        for k in range(1, N_DEV):
            px = 1 - x if k & 4 else x
            py = 1 - y if k & 2 else y
            pc = 1 - c if k & 1 else c
            copies.append(pltpu.make_async_remote_copy(
                src_ref=src.at[4 * px + 2 * py + pc] if chunked else src, dst_ref=land.at[me],
                send_sem=send_sems.at[g * (N_DEV - 1) + k - 1], recv_sem=recv_sems.at[g * (N_DEV - 1) + k - 1],
                device_id=(px, py, pc), device_id_type=MESH_ID))
    return copies


def _hbm(x):
    return pltpu.with_memory_space_constraint(x, pltpu.HBM)


def _push_start(name, srcs, lands, chunked):
    ng = len(srcs)

    def body(*refs):
        for cp in _push_copies(refs[:ng], refs[ng:2 * ng], refs[2 * ng], refs[2 * ng + 1], chunked):
            cp.start()
        refs[-1][...] = jnp.zeros(refs[-1].shape, F32)

    bufs = list(srcs) + list(lands)
    outs = pl.pallas_call(
        body, name=name,
        out_shape=(pltpu.SemaphoreType.DMA((ng * (N_DEV - 1),)), pltpu.SemaphoreType.DMA((ng * (N_DEV - 1),)),
                   *[pltpu.HBM(b.shape, b.dtype) for b in bufs], jax.ShapeDtypeStruct((8, LANES), F32)),
        in_specs=[HBM_SPEC] * (2 * ng),
        out_specs=(SEM_SPEC, SEM_SPEC, *[HBM_SPEC] * (2 * ng), pl.BlockSpec(memory_space=pltpu.VMEM)),
        input_output_aliases={i: 2 + i for i in range(2 * ng)},
        compiler_params=pltpu.CompilerParams(has_side_effects=EFFECT),
    )(*[_hbm(b) for b in bufs])
    return outs[0], outs[1], list(outs[2:2 + ng]), list(outs[2 + ng:2 + 2 * ng]), outs[-1]


def _push_wait(name, started, after, chunked):
    send_sems, recv_sems, srcs, lands, _ = started
    ng = len(srcs)

    def body(*refs):
        copies = _push_copies(refs[:ng], refs[ng:2 * ng], refs[2 * ng], refs[2 * ng + 1], chunked)
        for cp in copies:
            cp.wait_send()
        for cp in copies:
            cp.wait_recv()

    bufs = srcs + lands
    outs = pl.pallas_call(
        body, name=name,
        out_shape=tuple(pltpu.HBM(b.shape, b.dtype) for b in bufs),
        in_specs=[HBM_SPEC] * (2 * ng) + [SEM_SPEC, SEM_SPEC, pl.BlockSpec(memory_space=pl.ANY)],
        out_specs=tuple([HBM_SPEC] * (2 * ng)),
        input_output_aliases={i: i for i in range(2 * ng)},
        compiler_params=pltpu.CompilerParams(has_side_effects=EFFECT),
    )(*bufs, send_sems, recv_sems, after)
    return list(outs[:ng]), list(outs[ng:])


def _sum_adam_devices(sent, recv, dev, w, m, v):
    ndev, r, c_ = recv.shape
    tr = _pick(r, (256, 128, 96, 32))

    def body(dev_ref, own_ref, r_ref, w_ref, m_ref, v_ref, g_ref, d_ref, mo_ref, vo_ref):
        me = dev_ref[0]
        g = jnp.where(me == 0, own_ref[0], r_ref[0]).astype(F32)
        for j in range(1, ndev):
            g = g + jnp.where(me == j, own_ref[0], r_ref[j]).astype(F32)
        g_ref[...] = g
        d_ref[...], mo_ref[...], vo_ref[...] = _adam(w_ref[...], g, m_ref[...], v_ref[...])

    row = pl.BlockSpec((tr, c_), lambda i, dev_ref: (i, 0))
    return pl.pallas_call(
        body, name="sum_adam",
        grid_spec=pltpu.PrefetchScalarGridSpec(
            num_scalar_prefetch=1, grid=(r // tr,),
            in_specs=[pl.BlockSpec((1, tr, c_), lambda i, dev_ref: (dev_ref[0], i, 0)),
                      pl.BlockSpec((ndev, tr, c_), lambda i, dev_ref: (0, i, 0)), row, row, row],
            out_specs=[row, row, row, row]),
        out_shape=[_sds((r, c_))] * 4,
        compiler_params=_cparams(dimension_semantics=("arbitrary",)),
    )(dev, sent, recv, w, m, v)


def _cols_from_shards(w, width):
    ns, r, cs = w.shape
    tr = _pick(r, (256, 128))

    def body(w_ref, o_ref):
        parts = [w_ref[j] for j in range(ns)]
        if width > ns * cs:
            parts.append(jnp.zeros((tr, width - ns * cs), w.dtype))
        o_ref[...] = jnp.concatenate(parts, axis=1)

    return pl.pallas_call(
        body, name="cols_from_shards", grid=(r // tr,),
        in_specs=[pl.BlockSpec((ns, tr, cs), lambda i: (0, i, 0))], out_specs=pl.BlockSpec((tr, width), lambda i: (i, 0)),
        out_shape=jax.ShapeDtypeStruct((r, width), w.dtype), compiler_params=_cparams(dimension_semantics=("arbitrary",)),
    )(w)


def _shards_from_cols(g, cs):
    r, width = g.shape
    tr = _pick(r, (256, 128))

    def body(g_ref, o_ref):
        for j in range(N_DEV):
            o_ref[j] = g_ref[:, j * cs:(j + 1) * cs]

    return pl.pallas_call(
        body, name="shards_from_cols", grid=(r // tr,),
        in_specs=[pl.BlockSpec((tr, width), lambda i: (i, 0))], out_specs=pl.BlockSpec((N_DEV, tr, cs), lambda i: (0, i, 0)),
        out_shape=jax.ShapeDtypeStruct((N_DEV, r, cs), g.dtype), compiler_params=_cparams(dimension_semantics=("arbitrary",)),
    )(g)


def _adam(w, g, m, v):
    m = ADAM_B1 * m + (1.0 - ADAM_B1) * g
    v = ADAM_B2 * v + (1.0 - ADAM_B2) * (g * g)
    m_hat = m / (1.0 - ADAM_B1 ** ADAM_STEP)
    v_hat = v / (1.0 - ADAM_B2 ** ADAM_STEP)
    return -ADAM_LR * (m_hat / (jnp.sqrt(v_hat) + ADAM_EPS) + ADAM_WD * w), m, v


def _sum_devices(gathered):
    m_all, n = gathered.shape
    m_per = m_all // N_DEV

    def body(x_ref, o_ref):
        s = x_ref[0:m_per, :]
        for j in range(1, N_DEV):
            s = s + x_ref[j * m_per:(j + 1) * m_per, :]
        o_ref[...] = s

    return pl.pallas_call(body, name="sum_devices", out_shape=_sds((m_per, n)), compiler_params=_cparams())(gathered)


def _adam_small(w, g, m, v):
    def body(w_ref, g_ref, m_ref, v_ref, d_ref, mo_ref, vo_ref):
        d_ref[...], mo_ref[...], vo_ref[...] = _adam(w_ref[...], g_ref[...], m_ref[...], v_ref[...])

    return pl.pallas_call(body, name="adam_small", out_shape=[_sds(w.shape)] * 3, compiler_params=_cparams())(w, g, m, v)


N_LAYERS = 4
_MIXER = ("dn", "sb", "mla")
_MIXER_PARAMS = {
    "dn": ("dn_w_in", "dn_conv_w", "dn_a_log", "dn_dt_bias", "dn_out_norm", "dn_w_out"),
    "sb": ("sb_w_qkv", "sb_q_norm", "sb_k_norm", "sb_w_out"),
    "mla": ("mla_w_down", "mla_q_a_norm", "mla_kv_a_norm", "mla_w_uq", "mla_w_ukv", "mla_q_nope_norm", "mla_q_rope_norm",
            "mla_k_nope_norm", "mla_k_rope_norm", "mla_w_out"),
}
_BIG_AXIS = {"dn_w_in": 1, "dn_w_out": 0, "sb_w_qkv": 1, "sb_w_out": 0, "mla_w_down": 0, "mla_w_uq": 1, "mla_w_ukv": 1,
             "mla_w_out": 0, "ffn_w_gate_up": 1, "ffn_w_down": 0}


def _weight_names():
    names = []
    for i in range(N_LAYERS):
        p = "l%d_" % i
        names += [p + "mix_norm"] + [p + n for n in _MIXER_PARAMS[_MIXER[i % 3]]] + [p + "ffn_norm", p + "ffn_w_gate_up", p + "ffn_w_down"]
    return names


WEIGHTS = _weight_names()
BIG = [n for n in WEIGHTS if n[3:] in _BIG_AXIS]
SMALL = [n for n in WEIGHTS if n[3:] not in _BIG_AXIS]
CONV = [n for n in SMALL if n.endswith("conv_w")]


def _ceil_to(n, k):
    return -(-n // k) * k


def _pack(arrs, cols, row_mult):
    parts = []
    for a in arrs:
        f = a.reshape(-1)
        parts.append(jnp.pad(f, (0, _ceil_to(f.shape[0], cols) - f.shape[0])))
    flat = jnp.concatenate(parts)
    rows = _ceil_to(flat.shape[0] // cols, row_mult)
    return jnp.pad(flat, (0, rows * cols - flat.shape[0])).reshape(rows, cols)


def _unpack(buf, shapes):
    cols = buf.shape[-1]
    out, r0 = [], 0
    for s in shapes:
        n = math.prod(s)
        nr = _ceil_to(n, cols) // cols
        out.append(buf[r0:r0 + nr].reshape(-1)[:n].reshape(s))
        r0 += nr
    return out


def _layer_groups(i):
    by = {"gu": (704, []), "down": (1024, []), "out": (1024, []), "dn_in": (514, []), "sb_qkv": (384, []), "mla": (512, [])}
    key = {"ffn_w_gate_up": "gu", "ffn_w_down": "down", "dn_w_in": "dn_in", "sb_w_qkv": "sb_qkv", "mla_w_down": "mla",
           "mla_w_uq": "mla", "mla_w_ukv": "mla"}
    for n in BIG:
        if n.startswith("l%d_" % i):
            by[key.get(n[3:], "out")][1].append(n)
    return [g for g in by.values() if g[1]]


LAYER_GROUPS = [_layer_groups(i) for i in range(N_LAYERS)]
N_FFN_GROUPS = 2


def _stack_group(grp, get):
    width, names = grp
    if len(names) == 1 and get(names[0]).shape[1] == width:
        return get(names[0])
    return jnp.concatenate([jnp.pad(get(n), ((0, 0), (0, width - get(n).shape[1]))) for n in names], axis=0)


def _unstack_group(grp, buf, shape_of):
    if len(grp[1]) == 1 and tuple(shape_of(grp[1][0])) == tuple(buf.shape[-2:]):
        return [buf]
    out, r0 = [], 0
    for n in grp[1]:
        rs, cs = shape_of(n)
        out.append(buf[..., r0:r0 + rs, :cs])
        r0 += rs
    return out


def kernel(x, l0_mix_norm, l0_dn_w_in, l0_dn_conv_w, l0_dn_a_log, l0_dn_dt_bias, l0_dn_out_norm, l0_dn_w_out, l0_ffn_norm, l0_ffn_w_gate_up, l0_ffn_w_down, l1_mix_norm, l1_sb_w_qkv, l1_sb_q_norm, l1_sb_k_norm, l1_sb_w_out, l1_ffn_norm, l1_ffn_w_gate_up, l1_ffn_w_down, l2_mix_norm, l2_mla_w_down, l2_mla_q_a_norm, l2_mla_kv_a_norm, l2_mla_w_uq, l2_mla_w_ukv, l2_mla_q_nope_norm, l2_mla_q_rope_norm, l2_mla_k_nope_norm, l2_mla_k_rope_norm, l2_mla_w_out, l2_ffn_norm, l2_ffn_w_gate_up, l2_ffn_w_down, l3_mix_norm, l3_dn_w_in, l3_dn_conv_w, l3_dn_a_log, l3_dn_dt_bias, l3_dn_out_norm, l3_dn_w_out, l3_ffn_norm, l3_ffn_w_gate_up, l3_ffn_w_down, loss_target, m_l0_mix_norm, m_l0_dn_w_in, m_l0_dn_conv_w, m_l0_dn_a_log, m_l0_dn_dt_bias, m_l0_dn_out_norm, m_l0_dn_w_out, m_l0_ffn_norm, m_l0_ffn_w_gate_up, m_l0_ffn_w_down, m_l1_mix_norm, m_l1_sb_w_qkv, m_l1_sb_q_norm, m_l1_sb_k_norm, m_l1_sb_w_out, m_l1_ffn_norm, m_l1_ffn_w_gate_up, m_l1_ffn_w_down, m_l2_mix_norm, m_l2_mla_w_down, m_l2_mla_q_a_norm, m_l2_mla_kv_a_norm, m_l2_mla_w_uq, m_l2_mla_w_ukv, m_l2_mla_q_nope_norm, m_l2_mla_q_rope_norm, m_l2_mla_k_nope_norm, m_l2_mla_k_rope_norm, m_l2_mla_w_out, m_l2_ffn_norm, m_l2_ffn_w_gate_up, m_l2_ffn_w_down, m_l3_mix_norm, m_l3_dn_w_in, m_l3_dn_conv_w, m_l3_dn_a_log, m_l3_dn_dt_bias, m_l3_dn_out_norm, m_l3_dn_w_out, m_l3_ffn_norm, m_l3_ffn_w_gate_up, m_l3_ffn_w_down, v_l0_mix_norm, v_l0_dn_w_in, v_l0_dn_conv_w, v_l0_dn_a_log, v_l0_dn_dt_bias, v_l0_dn_out_norm, v_l0_dn_w_out, v_l0_ffn_norm, v_l0_ffn_w_gate_up, v_l0_ffn_w_down, v_l1_mix_norm, v_l1_sb_w_qkv, v_l1_sb_q_norm, v_l1_sb_k_norm, v_l1_sb_w_out, v_l1_ffn_norm, v_l1_ffn_w_gate_up, v_l1_ffn_w_down, v_l2_mix_norm, v_l2_mla_w_down, v_l2_mla_q_a_norm, v_l2_mla_kv_a_norm, v_l2_mla_w_uq, v_l2_mla_w_ukv, v_l2_mla_q_nope_norm, v_l2_mla_q_rope_norm, v_l2_mla_k_nope_norm, v_l2_mla_k_rope_norm, v_l2_mla_w_out, v_l2_ffn_norm, v_l2_ffn_w_gate_up, v_l2_ffn_w_down, v_l3_mix_norm, v_l3_dn_w_in, v_l3_dn_conv_w, v_l3_dn_a_log, v_l3_dn_dt_bias, v_l3_dn_out_norm, v_l3_dn_w_out, v_l3_ffn_norm, v_l3_ffn_w_gate_up, v_l3_ffn_w_down):
    a = dict(locals())
    return _train_step(a)


def _train_step(a):
    mx, my, mc = lax.axis_index("x"), lax.axis_index("y"), lax.axis_index("c")
    dev = 4 * mx + 2 * my + mc
    dev_arr = jnp.reshape(dev, (1,)).astype(jnp.int32)
    t, d = a["x"].shape[1], a["x"].shape[2]
    xs = a["x"].reshape(t, d)
    target = a["loss_target"].reshape(t, d)

    full = {}

    def unpack(groups, bufs):
        for grp, buf in zip(groups, bufs):
            for n, shards in zip(grp[1], _unstack_group(grp, buf, lambda n: a[n].shape)):
                kind = n[3:]
                if kind == "ffn_w_gate_up":
                    full[n] = shards
                elif _BIG_AXIS[kind] == 0:
                    full[n] = shards.reshape(N_DEV * shards.shape[1], shards.shape[2])
                else:
                    width = DN_PROJ if kind == "dn_w_in" else N_DEV * shards.shape[2]
                    full[n] = _cols_from_shards(shards, width)

    def local_shards(groups):
        return [_stack_group(grp, lambda n: a[n].astype(BF16)) for grp in groups]

    pushed_groups = {"l0_gate_up": LAYER_GROUPS[0][:1], "l0_down": LAYER_GROUPS[0][1:N_FFN_GROUPS]}
    pushed_groups.update({"l%d" % i: LAYER_GROUPS[i] for i in range(1, N_LAYERS)})
    conv_pack = _pack([a[n] for n in CONV], LANES, 8)
    first_groups = LAYER_GROUPS[0][N_FFN_GROUPS:]
    *first_bufs, conv_all, gathered = _all_gather_groups("gather_weights", local_shards(first_groups) + [conv_pack])
    unpack(first_groups, first_bufs)
    for n, parts in zip(CONV, zip(*[_unpack(conv_all[j], [a[n].shape for n in CONV]) for j in range(N_DEV)])):
        full[n] = jnp.concatenate(parts, axis=1)
    after_first = gathered[0, 0].astype(BF16)
    gathers, started = {}, jnp.zeros((), F32)
    for tag, groups in pushed_groups.items():
        srcs = local_shards(groups)
        srcs[-1] = srcs[-1] + after_first
        lands = [lax.dynamic_update_index_in_dim(lax.empty((N_DEV,) + s.shape, s.dtype), s, dev, 0) for s in srcs]
        gathers[tag] = _push_start("gather_start_" + tag, srcs, lands, False)
        started = started + gathers[tag][-1][0, 0]

    def arrived(tag, after):
        unpack(pushed_groups[tag], _push_wait("gather_wait_" + tag, gathers[tag], after, False)[1])

    def vec(n):
        return a[n].reshape(1, -1)

    cos, sin = _rope_tables(t)

    def mixer_args(i):
        p = "l%d_" % i
        kind = _MIXER[i % 3]
        if kind == "dn":
            args = (full[p + "dn_w_in"], full[p + "dn_conv_w"], _pad_lanes(a[p + "dn_a_log"]), _pad_lanes(a[p + "dn_dt_bias"]),
                    vec(p + "dn_out_norm"), full[p + "dn_w_out"])
        elif kind == "sb":
            args = (full[p + "sb_w_qkv"], vec(p + "sb_q_norm"), vec(p + "sb_k_norm"), full[p + "sb_w_out"])
        else:
            w_down, w_uq, w_ukv = _mla_layout(full[p + "mla_w_down"], full[p + "mla_w_uq"], full[p + "mla_w_ukv"])
            args = (w_down, vec(p + "mla_q_a_norm"), vec(p + "mla_kv_a_norm"), w_uq, w_ukv, vec(p + "mla_q_nope_norm"),
                    _pad_lanes(a[p + "mla_q_rope_norm"]), vec(p + "mla_k_nope_norm"), _pad_lanes(a[p + "mla_k_rope_norm"]),
                    full[p + "mla_w_out"], cos, sin)
        return kind, args

    fwd = {"dn": _dn_fwd, "sb": _sb_fwd, "mla": _mla_fwd}
    bwd = {"dn": _dn_bwd, "sb": _sb_bwd, "mla": _mla_bwd}
    saved, layer_args = [], []
    for i in range(N_LAYERS):
        p = "l%d_" % i
        if i > 0:
            arrived("l%d" % i, xs)
        kind, args = mixer_args(i)
        layer_args.append((kind, args))
        gain = vec(p + "mix_norm") + started if i == 0 else vec(p + "mix_norm")
        h = _rmsnorm_fwd("mix_norm", xs, gain)
        x_mid, sv_mix = fwd[kind](xs, h, *args)
        if i == 0:
            arrived("l0_gate_up", x_mid)

        def down_weight(act, p=p, first=(i == 0)):
            if first:
                arrived("l0_down", act)
            return full[p + "ffn_w_down"]

        x_out, sv_ffn = _ffn_fwd(x_mid, vec(p + "ffn_norm"), full[p + "ffn_w_gate_up"], down_weight)
        saved.append((xs, sv_mix, sv_ffn))
        xs = x_out
    dy, loss_part = _loss_head(xs, target)

    grads, big_out = {}, {}

    def grad_shards(n):
        g, (rs, cs) = grads[n], a[n].shape
        if g.ndim == 3:
            return g
        if _BIG_AXIS[n[3:]] == 0:
            return g.reshape(N_DEV, rs, cs)
        return _shards_from_cols(g, cs)

    def push_grads(tag, groups):
        sends = []
        for grp in groups:
            if len(grp[1]) == 1 and a[grp[1][0]].shape[1] == grp[0]:
                sends.append(grad_shards(grp[1][0]))
                continue
            parts = [jnp.pad(grad_shards(n), ((0, 0), (0, 0), (0, grp[0] - a[n].shape[1]))) for n in grp[1]]
            sends.append(jnp.concatenate(parts, axis=1))
        lands = [lax.empty(s.shape, s.dtype) for s in sends]
        return tag, groups, _push_start("grads_start_" + tag, sends, lands, True)

    def finish_grads(push, after):
        tag, groups, pushed = push
        sents, recvs = _push_wait("grads_wait_" + tag, pushed, after, True)
        for grp, sent, recv in zip(groups, sents, recvs):
            packs = [_stack_group(grp, lambda n, pre=pre: a[pre + n]) for pre in ("", "m_", "v_")]
            outs = [_unstack_group(grp, o, lambda n: a[n].shape) for o in _sum_adam_devices(sent, recv, dev_arr, *packs)]
            for j, n in enumerate(grp[1]):
                big_out[n] = [o[j] for o in outs]

    mixer_push, pushes = None, []
    for i in reversed(range(N_LAYERS)):
        p = "l%d_" % i
        kind, args = layer_args[i]
        x_in, sv_mix, sv_ffn = saved[i]
        gain = vec(p + "ffn_norm") if mixer_push is None else vec(p + "ffn_norm") + mixer_push[2][-1][0, 0]
        dx_mid, grads[p + "ffn_norm"], grads[p + "ffn_w_gate_up"], grads[p + "ffn_w_down"] = _ffn_bwd(
            sv_ffn, dy, gain, full[p + "ffn_w_gate_up"], full[p + "ffn_w_down"])
        ffn_push = push_grads("l%d_ffn" % i, LAYER_GROUPS[i][:N_FFN_GROUPS])
        res = bwd[kind](sv_mix, dx_mid, *args)
        dh = res[0]
        if kind == "mla":
            res = list(res)
            res[1], res[4], res[5] = _mla_unlayout(res[1], res[4], res[5])
        for n, g in zip(_MIXER_PARAMS[kind], res[1:]):
            grads[p + n] = g
        mixer_push = push_grads("l%d_mix" % i, LAYER_GROUPS[i][N_FFN_GROUPS:])
        pushed = ffn_push[2][-1][0, 0] + mixer_push[2][-1][0, 0]
        dy, grads[p + "mix_norm"] = _rmsnorm_bwd("mix_norm_bwd", x_in, vec(p + "mix_norm") + pushed, dh, dx_mid)
        pushes += [ffn_push, mixer_push]
    for push in pushes[:-1]:
        finish_grads(push, dy)
    grad_x = dy.reshape(a["x"].shape)

    small_full_shapes = [full[n].shape if n in CONV else a[n].shape for n in SMALL]
    small_grads = []
    for n, s in zip(SMALL, small_full_shapes):
        g = grads[n].reshape(-1)
        small_grads.append(g[:math.prod(s)])
    small_pack = _pack(small_grads + [loss_part.reshape(-1)], LANES, 8)
    small_sum = _sum_devices(_all_gather("gather_small_grads", small_pack))
    small_red = _unpack(small_sum, small_full_shapes + [(LANES,)])
    loss = small_red[-1][0]
    g_small = {}
    for n, g in zip(SMALL, small_red[:-1]):
        if n in CONV:
            cs = a[n].shape[1]
            g = lax.dynamic_slice_in_dim(g, dev * cs, cs, axis=1)
        g_small[n] = g
    small_shapes = [a[n].shape for n in SMALL]
    packs = [_pack([src[n] for n in SMALL], LANES, 8) for src in
             ({n: a[n] for n in SMALL}, g_small, {n: a["m_" + n] for n in SMALL}, {n: a["v_" + n] for n in SMALL})]
    d_small, m_small, v_small = (_unpack(o, small_shapes) for o in _adam_small(*packs))
    finish_grads(mixer_push, small_sum)

    small_out = dict(zip(SMALL, zip([g_small[n] for n in SMALL], d_small, m_small, v_small)))

    def out(k):
        return [small_out[n][k] if n in small_out else big_out[n][k] for n in WEIGHTS]

    return (loss, grad_x, *out(0), *out(1), *out(2), *out(3))
```

```python
import math

import jax
import jax.numpy as jnp
from jax import lax
from jax.experimental import pallas as pl
from jax.experimental.pallas import tpu as pltpu

F32 = jnp.float32
BF16 = jnp.bfloat16
GRAD_DTYPE = BF16
F32X3 = lax.Precision.HIGH

LANES = 128
N_DEV = 8
N_HEADS = 8
HEAD = 128
NORM_EPS = 1e-6
DN_CHUNK = 64
DN_STEP = 2 * DN_CHUNK
ATT_BLOCK = 512
ATT_Q = 512
MLA_ROPE = 64
MLA_QK = 192
ROPE_THETA = 10000.0
VMEM_LIMIT = 56 * 1024 * 1024
MM_TILE_BYTES = 32 * 1024 * 1024

ADAM_LR = 0.001
ADAM_B1 = 0.9
ADAM_B2 = 0.999
ADAM_EPS = 1e-08
ADAM_WD = 0.01
ADAM_STEP = 10


def _cparams(**kw):
    return pltpu.CompilerParams(vmem_limit_bytes=VMEM_LIMIT, **kw)


def _pick(n, cands):
    for c in cands:
        if c <= n and n % c == 0:
            return c
    return n


def _mm(name, a, b, *, ta=False, tb=False, out_dtype=F32, add=None, tm=None, tn=None, tk=None):
    if ta:
        K, M = a.shape
    else:
        M, K = a.shape
    N = b.shape[0] if tb else b.shape[1]
    tm = tm or _pick(M, (1024, 512, 256, 128))
    tn = tn or _pick(N, (1024, 512, 384, 256, 128))
    if tk is None:
        fits = [c for c in (4096, 2048, 1408, 1024, 512, 384, 256, 128)
                if c <= K and K % c == 0 and 2 * c * (tm * a.dtype.itemsize + tn * b.dtype.itemsize) <= MM_TILE_BYTES]
        tk = fits[0] if fits else K
    return _mm_raw(
        name, a, b, ta=ta, tb=tb, out_dtype=out_dtype, add=add, grid=(M // tm, N // tn, K // tk), out_shape=(M, N),
        a_block=(tk, tm) if ta else (tm, tk), a_map=(lambda i, j, k: (k, i)) if ta else (lambda i, j, k: (i, k)),
        b_block=(tn, tk) if tb else (tk, tn), b_map=(lambda i, j, k: (j, k)) if tb else (lambda i, j, k: (k, j)),
        o_block=(tm, tn), o_map=lambda i, j, k: (i, j))


def _mm_raw(name, a, b, *, ta, tb, out_dtype, add, grid, out_shape, a_block, a_map, b_block, b_map, o_block, o_map):
    nk = grid[2]
    tm, tn = o_block
    dn = (((0 if ta else 1,), (1 if tb else 0,)), ((), ()))
    has_add = add is not None

    def kern(*refs):
        if has_add:
            a_ref, b_ref, add_ref, o_ref, acc_ref = refs
        else:
            a_ref, b_ref, o_ref, acc_ref = refs
        k = pl.program_id(2)
        part = lax.dot_general(a_ref[...].astype(BF16), b_ref[...].astype(BF16), dn, preferred_element_type=F32)

        @pl.when(k == 0)
        def _():
            acc_ref[...] = part

        @pl.when(k > 0)
        def _():
            acc_ref[...] += part

        @pl.when(k == nk - 1)
        def _():
            r = acc_ref[...]
            if has_add:
                r = r + add_ref[...]
            o_ref[...] = r.astype(out_dtype)

    in_specs = [pl.BlockSpec(a_block, a_map), pl.BlockSpec(b_block, b_map)]
    args = [a, b]
    if has_add:
        in_specs.append(pl.BlockSpec(o_block, o_map))
        args.append(add)
    return pl.pallas_call(
        kern, name=name,
        grid=grid,
        in_specs=in_specs,
        out_specs=pl.BlockSpec(o_block, o_map),
        out_shape=jax.ShapeDtypeStruct(out_shape, out_dtype),
        scratch_shapes=[pltpu.VMEM((tm, tn), F32)],
        compiler_params=_cparams(dimension_semantics=("parallel", "parallel", "arbitrary")),
    )(*args)


def _rows(name, body, ins, outs, *, tt, consts=(), accs=()):
    in_specs, args = [], []
    first = ins[0][0] if isinstance(ins[0], tuple) else ins[0]
    t = first.shape[-2]
    tt = min(tt, t)
    for x in ins:
        if isinstance(x, tuple):
            arr, bs, im = x
            in_specs.append(pl.BlockSpec(bs, im))
            args.append(arr)
        else:
            in_specs.append(_row_spec(x.shape, tt))
            args.append(x)
    for c in consts:
        in_specs.append(pl.BlockSpec(c.shape, lambda i, _n=c.ndim: (0,) * _n))
        args.append(c)
    out_specs = [_row_spec(o.shape, tt) for o in outs]
    out_specs += [pl.BlockSpec(a.shape, lambda i, _n=len(a.shape): (0,) * _n) for a in accs]
    res = pl.pallas_call(
        body, name=name, grid=(t // tt,),
        in_specs=in_specs, out_specs=out_specs, out_shape=list(outs) + list(accs),
        compiler_params=_cparams(dimension_semantics=("arbitrary",)),
    )(*args)
    return res


def _row_spec(shape, tt):
    if len(shape) == 2:
        return pl.BlockSpec((tt, shape[1]), lambda i: (i, 0))
    return pl.BlockSpec((shape[0], tt, shape[2]), lambda i: (0, i, 0))


def _sds(shape, dtype=F32):
    return jax.ShapeDtypeStruct(tuple(shape), dtype)


def _acc(ref, val):
    i = pl.program_id(0)

    @pl.when(i == 0)
    def _():
        ref[...] = val

    @pl.when(i > 0)
    def _():
        ref[...] += val


def _rms(x, g):
    return x * lax.rsqrt(jnp.mean(x * x, axis=-1, keepdims=True) + NORM_EPS) * g


def _silu(x):
    return x / (1.0 + jnp.exp(-x))


def _softplus(x):
    return jnp.maximum(x, 0.0) + jnp.log(1.0 + jnp.exp(-jnp.abs(x)))


def _sigmoid(x):
    return 1.0 / (1.0 + jnp.exp(-x))


def _rmsnorm_fwd(name, x, g, tt=512):
    def body(x_ref, g_ref, h_ref):
        h_ref[...] = _rms(x_ref[...], g_ref[...]).astype(BF16)

    return _rows(name, body, [x], [_sds(x.shape, BF16)], tt=tt, consts=[g])[0]


def _rmsnorm_bwd(name, x, g, dh, dres, tt=512):
    def body(x_ref, dh_ref, dres_ref, g_ref, dx_ref, dg_ref):
        _, vjp = jax.vjp(_rms, x_ref[...], g_ref[...])
        dx, dg = vjp(dh_ref[...])
        dx_ref[...] = dx + dres_ref[...]
        _acc(dg_ref, dg)

    return _rows(name, body, [x, dh, dres], [_sds(x.shape)], tt=tt, consts=[g], accs=[_sds(g.shape)])


def _ffn_fwd(x, norm_g, w3, down_weight):
    t, d = x.shape
    ns, _, cs = w3.shape
    half = ns // 2
    w2 = w3.reshape(ns * d, cs)
    h = _rmsnorm_fwd("ffn_norm", x, norm_g)
    tm = _pick(t, (1024, 512, 256, 128))
    nm = t // tm

    def gate_up(h_ref, wg_ref, wu_ref, g_ref, u_ref, a_ref):
        hv = h_ref[...]
        g = _dot(hv, wg_ref[...])
        u = _dot(hv, wu_ref[...])
        g_ref[...] = g
        u_ref[...] = u
        a_ref[...] = (_silu(g) * u).astype(BF16)

    hid = pl.BlockSpec((tm, cs), lambda j, i: (j * nm + i, 0))
    g, u, act = pl.pallas_call(
        gate_up, name="ffn_gate_up", grid=(half, nm),
        in_specs=[pl.BlockSpec((tm, d), lambda j, i: (i, 0)), pl.BlockSpec((d, cs), lambda j, i: (j, 0)),
                  pl.BlockSpec((d, cs), lambda j, i: (j + half, 0))],
        out_specs=[hid, hid, hid], out_shape=[_sds((half * t, cs)), _sds((half * t, cs)), _sds((half * t, cs), BF16)],
        compiler_params=_cparams(dimension_semantics=("parallel", "arbitrary")),
    )(h, w2, w2)
    w_down = down_weight(act)

    def down(a_ref, w_ref, x_ref, y_ref):
        y = x_ref[...]
        for j in range(half):
            y = y + _dot(a_ref[j], w_ref[j])
        y_ref[...] = y

    y = pl.pallas_call(
        down, name="ffn_down", grid=(nm,),
        in_specs=[pl.BlockSpec((half, tm, cs), lambda i: (0, i, 0)), pl.BlockSpec((half, cs, d), lambda i: (0, 0, 0)),
                  pl.BlockSpec((tm, d), lambda i: (i, 0))],
        out_specs=pl.BlockSpec((tm, d), lambda i: (i, 0)), out_shape=_sds((t, d)),
        compiler_params=_cparams(dimension_semantics=("parallel",)),
    )(act.reshape(half, t, cs), w_down.reshape(half, cs, d), x)
    return y, (x, h, g, u, act)


def _ffn_bwd(saved, dy, norm_g, w3, w_down):
    x, h, g, u, act = saved
    t, d = x.shape
    ns, _, cs = w3.shape
    half = ns // 2
    w2 = w3.reshape(ns * d, cs)
    tm = _pick(t, (1024, 512, 256, 128))
    nm = t // tm
    tk = _pick(t, (4096, 2048, 1024, 512, 256, 128))
    nk = t // tk
    d_wdown = _mm_raw("ffn_down_wgrad", act, dy, ta=True, tb=False, out_dtype=GRAD_DTYPE, add=None, grid=(half, 1, nk),
                      out_shape=(half * cs, d), a_block=(tk, cs), a_map=lambda i, j, k: (i * nk + k, 0),
                      b_block=(tk, d), b_map=lambda i, j, k: (k, 0), o_block=(cs, d), o_map=lambda i, j, k: (i, 0))
    def down_dgrad(dy_ref, wd_ref, g_ref, u_ref, dg_ref, du_ref):
        da = _dot_nt(dy_ref[...].astype(BF16), wd_ref[...])
        gv, uv = g_ref[...], u_ref[...]
        s = _sigmoid(gv)
        dg_ref[...] = (da * uv * s * (1.0 + gv * (1.0 - s))).astype(BF16)
        du_ref[...] = (da * gv * s).astype(BF16)

    hid = pl.BlockSpec((tm, cs), lambda j, i: (j * nm + i, 0))
    dg, du = pl.pallas_call(
        down_dgrad, name="ffn_down_dgrad", grid=(half, nm),
        in_specs=[pl.BlockSpec((tm, d), lambda j, i: (i, 0)), pl.BlockSpec((cs, d), lambda j, i: (j, 0)), hid, hid],
        out_specs=[hid, hid], out_shape=[_sds((half * t, cs), BF16), _sds((half * t, cs), BF16)],
        compiler_params=_cparams(dimension_semantics=("parallel", "arbitrary")),
    )(dy, w_down, g, u)

    def wgrad(name, dd):
        return _mm_raw(name, h, dd, ta=True, tb=False, out_dtype=GRAD_DTYPE, add=None, grid=(1, half, nk), out_shape=(half * d, cs),
                       a_block=(tk, d), a_map=lambda i, j, k: (k, 0), b_block=(tk, cs), b_map=lambda i, j, k: (j * nk + k, 0),
                       o_block=(d, cs), o_map=lambda i, j, k: (j, 0))

    def gate_up_dgrad(dg_ref, du_ref, w_ref, dh_ref):
        dh = _dot_nt(dg_ref[0], w_ref[0]) + _dot_nt(du_ref[0], w_ref[half])
        for j in range(1, half):
            dh = dh + _dot_nt(dg_ref[j], w_ref[j]) + _dot_nt(du_ref[j], w_ref[half + j])
        dh_ref[...] = dh

    th = _pick(t, (512, 256, 128))
    hid3 = pl.BlockSpec((half, th, cs), lambda i: (0, i, 0))
    dh = pl.pallas_call(
        gate_up_dgrad, name="ffn_gate_up_dgrad", grid=(t // th,),
        in_specs=[hid3, hid3, pl.BlockSpec((ns, d, cs), lambda i: (0, 0, 0))],
        out_specs=pl.BlockSpec((th, d), lambda i: (i, 0)), out_shape=_sds((t, d)),
        compiler_params=_cparams(dimension_semantics=("parallel",)),
    )(dg.reshape(half, t, cs), du.reshape(half, t, cs), w3)
    d_w3 = jnp.concatenate([wgrad("ffn_gate_wgrad", dg), wgrad("ffn_up_wgrad", du)], axis=0).reshape(ns, d, cs)
    dx, dgain = _rmsnorm_bwd("ffn_norm_bwd", x, norm_g, dh, dy)
    return dx, dgain, d_w3, d_wdown


def _dot_nt(a, b):
    return lax.dot_general(a, b, (((1,), (1,)), ((), ())), preferred_element_type=F32)


def _dot_tn(a, b):
    return lax.dot_general(a, b, (((0,), (0,)), ((), ())), preferred_element_type=F32)


def _dot(a, b):
    return jnp.dot(a, b, preferred_element_type=F32)


CUM_BLOCK = 128


def _tri2(lower):
    r = lax.broadcasted_iota(jnp.int32, (CUM_BLOCK, CUM_BLOCK), 0)
    c = lax.broadcasted_iota(jnp.int32, (CUM_BLOCK, CUM_BLOCK), 1)
    tri = ((r > c) if lower else (r < c)).astype(BF16)
    return jnp.concatenate([tri, tri], axis=0)


def _run_sums(x, tri2, run, reverse):
    nb = x.shape[1] // CUM_BLOCK
    outs = [None] * nb
    for j in (reversed(range(nb)) if reverse else range(nb)):
        xj = x[:, j * CUM_BLOCK:(j + 1) * CUM_BLOCK]
        hi = xj.astype(BF16)
        lo = (xj - hi.astype(F32)).astype(BF16)
        outs[j] = _dot(jnp.concatenate([hi, lo], axis=1), tri2) + run
        run = run + jnp.sum(xj, axis=1, keepdims=True)
    return jnp.concatenate(outs, axis=1), run


def _log_sigmoid(z):
    return jnp.minimum(z, 0.0) - jnp.log(1.0 + jnp.exp(-jnp.abs(z)))


def _heads_in(ref, h, width=HEAD):
    return ref[:, h * width:(h + 1) * width]


def _sb_qk(q, k, gq, gk):
    return _rms(q, gq) * (HEAD ** -0.5), _rms(k, gk)


def _sb_prep_fwd(qkv, gq, gk):
    t = qkv.shape[0]

    def body(x_ref, gq_ref, gk_ref, q_ref, k_ref, v_ref):
        for h in range(N_HEADS):
            q, k = _sb_qk(_heads_in(x_ref, h), _heads_in(x_ref, N_HEADS + h), gq_ref[...], gk_ref[...])
            q_ref[h] = q.astype(BF16)
            k_ref[h] = k.astype(BF16)
            v_ref[h] = _heads_in(x_ref, 2 * N_HEADS + h).astype(BF16)

    hm = _sds((N_HEADS, t, HEAD), BF16)
    return _rows("sb_prep", body, [qkv], [hm, hm, hm], tt=256, consts=[gq, gk])


def _sb_prep_bwd(qkv, gq, gk, dq, dk, dv):
    def body(x_ref, dq_ref, dk_ref, dv_ref, gq_ref, gk_ref, dx_ref, dgq_ref, dgk_ref):
        dgq = jnp.zeros(gq_ref.shape, F32)
        dgk = jnp.zeros(gk_ref.shape, F32)
        for h in range(N_HEADS):
            _, vjp = jax.vjp(_sb_qk, _heads_in(x_ref, h), _heads_in(x_ref, N_HEADS + h), gq_ref[...], gk_ref[...])
            a, b, c, d = vjp((dq_ref[h], dk_ref[h]))
            dx_ref[:, h * HEAD:(h + 1) * HEAD] = a.astype(BF16)
            dx_ref[:, (N_HEADS + h) * HEAD:(N_HEADS + h + 1) * HEAD] = b.astype(BF16)
            dx_ref[:, (2 * N_HEADS + h) * HEAD:(2 * N_HEADS + h + 1) * HEAD] = dv_ref[h].astype(BF16)
            dgq, dgk = dgq + c, dgk + d
        _acc(dgq_ref, dgq)
        _acc(dgk_ref, dgk)

    return _rows("sb_prep_bwd", body, [qkv, dq, dk, dv], [_sds(qkv.shape, BF16)], tt=256, consts=[gq, gk],
                 accs=[_sds(gq.shape), _sds(gk.shape)])


def _q_block(t):
    return min(ATT_Q, t)


def _key_order(bq, qb):
    rows = lax.broadcasted_iota(jnp.int32, (bq, ATT_BLOCK), 0)
    cols = lax.broadcasted_iota(jnp.int32, (bq, ATT_BLOCK), 1)
    return rows - cols + qb * bq


def _sb_attn_fwd(q, k, v):
    nh, t, _ = q.shape
    bq = _q_block(t)
    per = bq // ATT_BLOCK

    def kern(q_ref, k_ref, v_ref, o_ref):
        qb = pl.program_id(1)
        qv = q_ref[0]
        after = _tri2(True)
        order = _key_order(bq, qb)
        nkb = (qb + 1) * per

        def body(i, carry, diagonal):
            o_acc, run = carry
            kb = nkb - 1 - i
            off = pl.multiple_of(kb * ATT_BLOCK, ATT_BLOCK)
            kv = k_ref[0, pl.ds(off, ATT_BLOCK), :]
            vv = v_ref[0, pl.ds(off, ATT_BLOCK), :]
            z = _dot_nt(qv, kv)
            lsz = _log_sigmoid(z)
            lsn = lsz - z
            if diagonal:
                past = order > kb * ATT_BLOCK
                lsn = jnp.where(past, lsn, 0.0)
            la, run = _run_sums(lsn, after, run, True)
            a = jnp.exp(lsz + la)
            if diagonal:
                a = jnp.where(past, a, 0.0)
            o_acc = o_acc + _dot(a.astype(BF16), vv)
            return o_acc, run

        carry = lax.fori_loop(0, per, lambda i, c: body(i, c, True), (jnp.zeros((bq, HEAD), F32), jnp.zeros((bq, 1), F32)))
        o, _ = lax.fori_loop(per, nkb, lambda i, c: body(i, c, False), carry)
        o_ref[...] = o

    return pl.pallas_call(
        kern, name="sb_attn_fwd", grid=(nh, t // bq),
        in_specs=[pl.BlockSpec((1, bq, HEAD), lambda h, i: (h, i, 0)),
                  pl.BlockSpec((1, t, HEAD), lambda h, i: (h, 0, 0)),
                  pl.BlockSpec((1, t, HEAD), lambda h, i: (h, 0, 0))],
        out_specs=pl.BlockSpec((bq, HEAD), lambda h, i: (i, h)),
        out_shape=_sds((t, nh * HEAD)),
        compiler_params=_cparams(dimension_semantics=("parallel", "arbitrary")),
    )(q, k, v)


def _sb_attn_bwd(q, k, v, do):
    nh, t, _ = q.shape
    bq = _q_block(t)
    per = bq // ATT_BLOCK

    def kern(q_ref, k_ref, v_ref, do_ref, dq_ref, dk_ref, dv_ref, g_s, ls_s):
        qb = pl.program_id(1)

        @pl.when(qb == 0)
        def _():
            dk_ref[...] = jnp.zeros(dk_ref.shape, F32)
            dv_ref[...] = jnp.zeros(dv_ref.shape, F32)

        qv = q_ref[0]
        dob = do_ref[...].astype(BF16)
        after, before = _tri2(True), _tri2(False)
        order = _key_order(bq, qb)
        nkb = (qb + 1) * per

        def sweep_left(i, run, diagonal):
            kb = nkb - 1 - i
            off = pl.multiple_of(kb * ATT_BLOCK, ATT_BLOCK)
            kv = k_ref[0, pl.ds(off, ATT_BLOCK), :]
            vv = v_ref[0, pl.ds(off, ATT_BLOCK), :]
            z = _dot_nt(qv, kv)
            lsz = _log_sigmoid(z)
            lsn = lsz - z
            if diagonal:
                past = order > kb * ATT_BLOCK
                lsn = jnp.where(past, lsn, 0.0)
            la, run = _run_sums(lsn, after, run, True)
            a = jnp.exp(lsz + la)
            if diagonal:
                a = jnp.where(past, a, 0.0)
            g_s[kb] = _dot_nt(dob, vv) * a
            ls_s[kb] = lsz
            dv_ref[0, pl.ds(off, ATT_BLOCK), :] += _dot_tn(a.astype(BF16), dob)
            return run

        zero = jnp.zeros((bq, 1), F32)
        run = lax.fori_loop(0, per, lambda i, c: sweep_left(i, c, True), zero)
        lax.fori_loop(per, nkb, lambda i, c: sweep_left(i, c, False), run)

        def sweep_right(kb, carry, diagonal):
            dq_acc, run_g = carry
            off = pl.multiple_of(kb * ATT_BLOCK, ATT_BLOCK)
            kv = k_ref[0, pl.ds(off, ATT_BLOCK), :]
            g = g_s[kb]
            sg = jnp.exp(ls_s[kb])
            dls, run_g = _run_sums(g, before, run_g, False)
            dz = g * (1.0 - sg) - dls * sg
            if diagonal:
                dz = jnp.where(order > kb * ATT_BLOCK, dz, 0.0)
            dzb = dz.astype(BF16)
            dk_ref[0, pl.ds(off, ATT_BLOCK), :] += _dot_tn(dzb, qv)
            return dq_acc + _dot(dzb, kv), run_g

        carry = lax.fori_loop(0, nkb - per, lambda i, c: sweep_right(i, c, False), (jnp.zeros((bq, HEAD), F32), zero))
        dq, _ = lax.fori_loop(nkb - per, nkb, lambda i, c: sweep_right(i, c, True), carry)
        dq_ref[0] = dq

    hm = _sds((nh, t, HEAD))
    full = pl.BlockSpec((1, t, HEAD), lambda h, i: (h, 0, 0))
    tok = pl.BlockSpec((bq, HEAD), lambda h, i: (i, h))
    nkb_max = t // ATT_BLOCK
    return pl.pallas_call(
        kern, name="sb_attn_bwd", grid=(nh, t // bq),
        in_specs=[pl.BlockSpec((1, bq, HEAD), lambda h, i: (h, i, 0)), full, full, tok],
        out_specs=[pl.BlockSpec((1, bq, HEAD), lambda h, i: (h, i, 0)), full, full],
        out_shape=[hm, hm, hm],
        scratch_shapes=[pltpu.VMEM((nkb_max, bq, ATT_BLOCK), F32), pltpu.VMEM((nkb_max, bq, ATT_BLOCK), F32)],
        compiler_params=_cparams(dimension_semantics=("parallel", "arbitrary")),
    )(q, k, v, do)


def _sb_fwd(x, h, w_qkv, gq, gk, w_out):
    qkv = _mm("sb_qkv", h, w_qkv)
    q, k, v = _sb_prep_fwd(qkv, gq, gk)
    o = _sb_attn_fwd(q, k, v)
    y = _mm("sb_out", o, w_out, add=x)
    return y, (h, qkv, q, k, v, o)


def _sb_bwd(saved, dy, w_qkv, gq, gk, w_out):
    h, qkv, q, k, v, o = saved
    d_wout = _mm("sb_out_wgrad", o, dy, ta=True, out_dtype=GRAD_DTYPE)
    do = _mm("sb_out_dgrad", dy, w_out, tb=True)
    dq, dk, dv = _sb_attn_bwd(q, k, v, do)
    dqkv, dgq, dgk = _sb_prep_bwd(qkv, gq, gk, dq, dk, dv)
    d_wqkv = _mm("sb_qkv_wgrad", h, dqkv, ta=True, out_dtype=GRAD_DTYPE)
    dh = _mm("sb_qkv_dgrad", dqkv, w_qkv, tb=True)
    return dh, d_wqkv, dgq, dgk, d_wout


DN_QKV = 3 * N_HEADS * HEAD
DN_PROJ = DN_QKV + N_HEADS * HEAD + LANES
DN_CONV = 4
HALO = 8
CONV_COLS = 512


def _dn_conv_fwd(proj, conv_w, tt=256):
    t = proj.shape[0]
    tt = min(tt, t)

    def body(u_ref, prev_ref, w_ref, c_ref):
        i = pl.program_id(0)
        for cc in range(DN_QKV // CONV_COLS):
            cs = slice(cc * CONV_COLS, (cc + 1) * CONV_COLS)
            cur = u_ref[:, cs]
            prev = jnp.where(i > 0, prev_ref[:, cs], 0.0)
            ext = jnp.concatenate([prev, cur], axis=0)
            y = cur * w_ref[DN_CONV - 1:DN_CONV, cs]
            for j in range(DN_CONV - 1):
                y = y + pltpu.roll(ext, DN_CONV - 1 - j, 0)[HALO:] * w_ref[j:j + 1, cs]
            c_ref[:, cs] = y

    return _rows("dn_conv", body,
                 [(proj, (tt, DN_QKV), lambda i: (i, 0)),
                  (proj, (HALO, DN_QKV), lambda i: (jnp.maximum(i * (tt // HALO) - 1, 0), 0))],
                 [_sds((t, DN_QKV))], tt=tt, consts=[conv_w])[0]


def _dn_conv_bwd(proj, conv_w, dc, dz, dab, tt=256):
    t = proj.shape[0]
    tt = min(tt, t)
    nblk = t // tt

    def body(u_ref, prev_ref, dc_ref, next_ref, dz_ref, dab_ref, w_ref, dp_ref, dw_ref):
        i = pl.program_id(0)
        dws = []
        for cc in range(DN_QKV // CONV_COLS):
            cs = slice(cc * CONV_COLS, (cc + 1) * CONV_COLS)
            cur = u_ref[:, cs]
            prev = jnp.where(i > 0, prev_ref[:, cs], 0.0)
            ext_u = jnp.concatenate([prev, cur], axis=0)
            d = dc_ref[:, cs]
            nxt = jnp.where(i < nblk - 1, next_ref[:, cs], 0.0)
            ext_d = jnp.concatenate([d, nxt], axis=0)
            du = d * w_ref[DN_CONV - 1:DN_CONV, cs]
            rows = [jnp.sum(d * cur, axis=0, keepdims=True)]
            for j in range(DN_CONV - 2, -1, -1):
                sh = DN_CONV - 1 - j
                du = du + pltpu.roll(ext_d, tt + HALO - sh, 0)[:tt] * w_ref[j:j + 1, cs]
                rows.insert(0, jnp.sum(d * pltpu.roll(ext_u, sh, 0)[HALO:], axis=0, keepdims=True))
            dp_ref[:, cs] = du.astype(BF16)
            dws.append(jnp.concatenate(rows, axis=0))
        dp_ref[:, DN_QKV:DN_QKV + N_HEADS * HEAD] = dz_ref[...].astype(BF16)
        dp_ref[:, DN_QKV + N_HEADS * HEAD:] = dab_ref[...].astype(BF16)
        _acc(dw_ref, jnp.concatenate(dws, axis=1))

    return _rows("dn_conv_bwd", body,
                 [(proj, (tt, DN_QKV), lambda i: (i, 0)),
                  (proj, (HALO, DN_QKV), lambda i: (jnp.maximum(i * (tt // HALO) - 1, 0), 0)),
                  dc,
                  (dc, (HALO, DN_QKV), lambda i: (jnp.minimum((i + 1) * (tt // HALO), t // HALO - 1), 0)),
                  dz, dab],
                 [_sds((t, DN_PROJ), BF16)], tt=tt, consts=[conv_w], accs=[_sds(conv_w.shape)])


def _l2n(x):
    return x * lax.rsqrt(jnp.sum(x * x, axis=-1, keepdims=True) + NORM_EPS)


def _dn_qkv(cq, ck, cv):
    return _l2n(_silu(cq)) * (HEAD ** -0.5), _l2n(_silu(ck)), _silu(cv)


def _dn_gates(ab, a_log, dt_bias):
    lane = lax.broadcasted_iota(jnp.int32, ab.shape, 1)
    g = -jnp.exp(a_log) * _softplus(ab + dt_bias)
    return jnp.where(lane < N_HEADS, g, jnp.where(lane < 2 * N_HEADS, _sigmoid(ab), 0.0))


def _ab_spec(tt):
    return (tt, LANES), lambda i: (i, DN_PROJ // LANES - 1)


def _dn_prep_fwd(c, proj, a_log, dt_bias, tt=256):
    t = c.shape[0]
    tt = min(tt, t)

    def body(c_ref, ab_ref, al_ref, dt_ref, q_ref, k_ref, v_ref, g_ref):
        for h in range(N_HEADS):
            q_ref[h], k_ref[h], v_ref[h] = _dn_qkv(_heads_in(c_ref, h), _heads_in(c_ref, N_HEADS + h), _heads_in(c_ref, 2 * N_HEADS + h))
        g_ref[...] = _dn_gates(ab_ref[...], al_ref[...], dt_ref[...])

    hm = _sds((N_HEADS, t, HEAD))
    return _rows("dn_prep", body, [c, (proj,) + _ab_spec(tt)], [hm, hm, hm, _sds((t, LANES))], tt=tt, consts=[a_log, dt_bias])


def _dn_prep_bwd(c, proj, a_log, dt_bias, dq, dk, dv, dgates, tt=256):
    t = c.shape[0]
    tt = min(tt, t)

    def body(c_ref, ab_ref, dq_ref, dk_ref, dv_ref, dg_ref, al_ref, dt_ref, dc_ref, dab_ref, dal_ref, ddt_ref):
        for h in range(N_HEADS):
            _, vjp = jax.vjp(_dn_qkv, _heads_in(c_ref, h), _heads_in(c_ref, N_HEADS + h), _heads_in(c_ref, 2 * N_HEADS + h))
            a, b, d = vjp((dq_ref[h], dk_ref[h], dv_ref[h]))
            dc_ref[:, h * HEAD:(h + 1) * HEAD] = a
            dc_ref[:, (N_HEADS + h) * HEAD:(N_HEADS + h + 1) * HEAD] = b
            dc_ref[:, (2 * N_HEADS + h) * HEAD:(2 * N_HEADS + h + 1) * HEAD] = d
        _, vjp = jax.vjp(_dn_gates, ab_ref[...], al_ref[...], dt_ref[...])
        dab, dal, ddt = vjp(dg_ref[...])
        dab_ref[...] = dab
        _acc(dal_ref, dal)
        _acc(ddt_ref, ddt)

    return _rows("dn_prep_bwd", body, [c, (proj,) + _ab_spec(tt), dq, dk, dv, dgates], [_sds(c.shape), _sds((t, LANES))],
                 tt=tt, consts=[a_log, dt_bias], accs=[_sds(a_log.shape), _sds(dt_bias.shape)])


def _bdot(a, b, prec=None):
    return lax.dot_general(a, b, (((2,), (1,)), ((0,), (0,))), precision=prec, preferred_element_type=F32)


def _bdot_nt(a, b, prec=None):
    return lax.dot_general(a, b, (((2,), (2,)), ((0,), (0,))), precision=prec, preferred_element_type=F32)


def _bdot_tn(a, b, prec=None):
    return lax.dot_general(a, b, (((1,), (1,)), ((0,), (0,))), precision=prec, preferred_element_type=F32)


def _inv_raw(low):
    c = low.shape[-1]
    r = lax.broadcasted_iota(jnp.int32, (c, c), 0)
    s = lax.broadcasted_iota(jnp.int32, (c, c), 1)
    m = jnp.where(r == s, 1.0, 0.0) - low
    p = _bdot(low, low, F32X3)
    n_fac = int(math.log2(c)) - 1
    for i in range(n_fac):
        m = m + _bdot(m, p, F32X3)
        if i < n_fac - 1:
            p = _bdot(p, p, F32X3)
    return m


@jax.custom_vjp
def _inv_unit_lower(low):
    return _inv_raw(low)


def _inv_fwd(low):
    m = _inv_raw(low)
    return m, m


def _inv_bwd(m, dm):
    return (-_bdot_nt(_bdot_tn(m, dm, F32X3), m, F32X3),)


_inv_unit_lower.defvjp(_inv_fwd, _inv_bwd)


def _dn_chunk(q, k, v, gates, s):
    nh, rows, _ = q.shape
    c = DN_CHUNK
    nc = rows // c
    nb = nh * nc
    lane = lax.broadcasted_iota(jnp.int32, gates.shape, 1)

    def column(j):
        return jnp.sum(jnp.where(lane == j, gates, 0.0), axis=1, keepdims=True)[None]

    def fold(x):
        return x.reshape((nb, c) + x.shape[2:])

    g_col = fold(jnp.concatenate([column(h) for h in range(nh)], axis=0))
    b_col = fold(jnp.concatenate([column(h + nh) for h in range(nh)], axis=0))
    q, k, v = fold(q), fold(k), fold(v)
    r = lax.broadcasted_iota(jnp.int32, (c, c), 0)
    cc = lax.broadcasted_iota(jnp.int32, (c, c), 1)
    causal, strict = r >= cc, r > cc
    incl = jnp.broadcast_to(jnp.where(causal, 1.0, 0.0), (nb, c, c))
    upper = jnp.broadcast_to(jnp.where(r <= cc, 1.0, 0.0), (nb, c, c))
    gb = jnp.broadcast_to(g_col, (nb, c, LANES))
    gbc = jnp.broadcast_to(g_col, (nb, c, c))
    gc = _bdot(incl, gb, F32X3)
    gc_r = _bdot(incl, gbc, F32X3)
    gc_c = _bdot_tn(gbc, upper, F32X3)
    decay = jnp.where(causal, jnp.exp(jnp.where(causal, gc_r - gc_c, 0.0)), 0.0)
    kb = k * b_col
    low = jnp.where(strict, _bdot_nt(kb, k) * decay, 0.0)
    m = _inv_unit_lower(low)
    egc = jnp.exp(gc)
    gl = jnp.sum(gb, axis=1, keepdims=True)
    local = (_bdot(m, v * b_col, F32X3), _bdot(m, kb * egc, F32X3), _bdot_nt(q, k) * decay, q * egc,
             k * jnp.exp(gl - gc), jnp.exp(gl))
    outs = []
    for i in range(nc):
        u, w, attn, q_dec, k_dec, cd = (x.reshape((nh, nc) + x.shape[1:])[:, i] for x in local)
        v_new = u - _bdot(w, s)
        outs.append(_bdot(q_dec, s) + _bdot(attn, v_new))
        s = s * cd + _bdot_tn(k_dec, v_new)
    return jnp.concatenate(outs, axis=1), s


def _dn_chunks_fwd(q, k, v, gates):
    nh, t, _ = q.shape
    n = t // DN_STEP

    def kern(q_ref, k_ref, v_ref, g_ref, o_ref, sin_ref, s_scr):
        @pl.when(pl.program_id(0) == 0)
        def _():
            s_scr[...] = jnp.zeros(s_scr.shape, F32)

        s = s_scr[...]
        sin_ref[0] = s
        o_ref[...], s_scr[...] = _dn_chunk(q_ref[...], k_ref[...], v_ref[...], g_ref[...], s)

    blk = pl.BlockSpec((nh, DN_STEP, HEAD), lambda i: (0, i, 0))
    return pl.pallas_call(
        kern, name="dn_chunks_fwd", grid=(n,),
        in_specs=[blk, blk, blk, pl.BlockSpec((DN_STEP, LANES), lambda i: (i, 0))],
        out_specs=[blk, pl.BlockSpec((1, nh, HEAD, HEAD), lambda i: (i, 0, 0, 0))],
        out_shape=[_sds((nh, t, HEAD)), _sds((n, nh, HEAD, HEAD))],
        scratch_shapes=[pltpu.VMEM((nh, HEAD, HEAD), F32)],
        compiler_params=_cparams(dimension_semantics=("arbitrary",)),
    )(q, k, v, gates)


def _dn_chunks_bwd(q, k, v, gates, s_in, do):
    nh, t, _ = q.shape
    n = t // DN_STEP

    def kern(q_ref, k_ref, v_ref, g_ref, sin_ref, do_ref, dq_ref, dk_ref, dv_ref, dg_ref, ds_scr):
        @pl.when(pl.program_id(0) == 0)
        def _():
            ds_scr[...] = jnp.zeros(ds_scr.shape, F32)

        _, vjp = jax.vjp(_dn_chunk, q_ref[...], k_ref[...], v_ref[...], g_ref[...], sin_ref[0])
        dq_ref[...], dk_ref[...], dv_ref[...], dg_ref[...], ds_scr[...] = vjp((do_ref[...], ds_scr[...]))

    blk = pl.BlockSpec((nh, DN_STEP, HEAD), lambda i: (0, n - 1 - i, 0))
    gblk = pl.BlockSpec((DN_STEP, LANES), lambda i: (n - 1 - i, 0))
    hm = _sds((nh, t, HEAD))
    return pl.pallas_call(
        kern, name="dn_chunks_bwd", grid=(n,),
        in_specs=[blk, blk, blk, gblk, pl.BlockSpec((1, nh, HEAD, HEAD), lambda i: (n - 1 - i, 0, 0, 0)), blk],
        out_specs=[blk, blk, blk, gblk],
        out_shape=[hm, hm, hm, _sds((t, LANES))],
        scratch_shapes=[pltpu.VMEM((nh, HEAD, HEAD), F32)],
        compiler_params=_cparams(dimension_semantics=("arbitrary",)),
    )(q, k, v, gates, s_in, do)


def _dn_gate_out(o, z, g):
    return _rms(o, g) * _silu(z)


def _z_spec(tt):
    return (tt, N_HEADS * HEAD), lambda i: (i, DN_QKV // (N_HEADS * HEAD))


def _dn_post_fwd(o, proj, out_norm, tt=256):
    t = o.shape[1]
    tt = min(tt, t)

    def body(o_ref, z_ref, g_ref, y_ref):
        for h in range(N_HEADS):
            y_ref[:, h * HEAD:(h + 1) * HEAD] = _dn_gate_out(o_ref[h], _heads_in(z_ref, h), g_ref[...]).astype(BF16)

    return _rows("dn_post", body, [o, (proj,) + _z_spec(tt)], [_sds((t, N_HEADS * HEAD), BF16)], tt=tt, consts=[out_norm])[0]


def _dn_post_bwd(o, proj, out_norm, dy, tt=256):
    t = o.shape[1]
    tt = min(tt, t)

    def body(o_ref, z_ref, dy_ref, g_ref, do_ref, dz_ref, dg_ref):
        dg = jnp.zeros(g_ref.shape, F32)
        for h in range(N_HEADS):
            _, vjp = jax.vjp(_dn_gate_out, o_ref[h], _heads_in(z_ref, h), g_ref[...])
            a, b, d = vjp(_heads_in(dy_ref, h))
            do_ref[h] = a
            dz_ref[:, h * HEAD:(h + 1) * HEAD] = b
            dg = dg + d
        _acc(dg_ref, dg)

    return _rows("dn_post_bwd", body, [o, (proj,) + _z_spec(tt), dy], [_sds(o.shape), _sds((t, N_HEADS * HEAD))], tt=tt,
                 consts=[out_norm], accs=[_sds(out_norm.shape)])


def _dn_fwd(x, h, w_in, conv_w, a_log, dt_bias, out_norm, w_out):
    proj = _mm("dn_in", h, w_in)
    c = _dn_conv_fwd(proj, conv_w)
    q, k, v, gates = _dn_prep_fwd(c, proj, a_log, dt_bias)
    o, s_in = _dn_chunks_fwd(q, k, v, gates)
    on = _dn_post_fwd(o, proj, out_norm)
    y = _mm("dn_out", on, w_out, add=x)
    return y, (h, proj, c, q, k, v, gates, o, s_in, on)


def _dn_bwd(saved, dy, w_in, conv_w, a_log, dt_bias, out_norm, w_out):
    h, proj, c, q, k, v, gates, o, s_in, on = saved
    d_wout = _mm("dn_out_wgrad", on, dy, ta=True, out_dtype=GRAD_DTYPE)
    don = _mm("dn_out_dgrad", dy, w_out, tb=True)
    do, dz, d_out_norm = _dn_post_bwd(o, proj, out_norm, don)
    dq, dk, dv, dgates = _dn_chunks_bwd(q, k, v, gates, s_in, do)
    dc, dab, d_a_log, d_dt_bias = _dn_prep_bwd(c, proj, a_log, dt_bias, dq, dk, dv, dgates)
    dproj, d_conv_w = _dn_conv_bwd(proj, conv_w, dc, dz, dab)
    d_win = _mm("dn_in_wgrad", h, dproj, ta=True, out_dtype=GRAD_DTYPE)
    dh = _mm("dn_in_dgrad", dproj, w_in, tb=True)
    return dh, d_win, d_conv_w, d_a_log, d_dt_bias, d_out_norm, d_wout


MLA_SCALE = MLA_QK ** -0.5
MLA_C = 512


def _swap_raw(x):
    lane = lax.broadcasted_iota(jnp.int32, x.shape, 1)
    half = MLA_ROPE // 2
    y = jnp.where(lane < half, pltpu.roll(x, LANES - half, 1), pltpu.roll(x, half, 1))
    return jnp.where(lane < MLA_ROPE, y, 0.0)


@jax.custom_vjp
def _swap_halves(x):
    return _swap_raw(x)


_swap_halves.defvjp(lambda x: (_swap_raw(x), None), lambda _, d: (_swap_raw(d),))


def _rms_rope(x, g, cos, sin):
    y = x * lax.rsqrt(jnp.sum(x * x, axis=-1, keepdims=True) * (1.0 / MLA_ROPE) + NORM_EPS) * g
    return y * cos + _swap_halves(y) * sin


def _mla_latent(cq, ckv, kr, gq, gkv, gkr, cos, sin):
    return _rms(cq, gq), _rms(ckv, gkv), _rms_rope(kr, gkr, cos, sin)


def _mla_prep1_fwd(c, gq, gkv, gkr, cos, sin):
    t = c.shape[0]

    def body(c_ref, cos_ref, sin_ref, gq_ref, gkv_ref, gkr_ref, cq_ref, ckv_ref, kr_ref):
        a, b, r = _mla_latent(c_ref[:, :256], c_ref[:, 256:384], c_ref[:, 384:], gq_ref[...], gkv_ref[...], gkr_ref[...],
                              cos_ref[...], sin_ref[...])
        cq_ref[...] = a.astype(BF16)
        ckv_ref[...] = b.astype(BF16)
        kr_ref[...] = r.astype(BF16)

    return _rows("mla_prep1", body, [c, cos, sin], [_sds((t, 256), BF16), _sds((t, HEAD), BF16), _sds((t, HEAD), BF16)],
                 tt=512, consts=[gq, gkv, gkr])


def _mla_prep1_bwd(c, gq, gkv, gkr, cos, sin, dcq, dckv, dkr_heads):
    def body(c_ref, cos_ref, sin_ref, dcq_ref, dckv_ref, dkr_ref, gq_ref, gkv_ref, gkr_ref, dc_ref, dgq_ref, dgkv_ref, dgkr_ref):
        dkr = dkr_ref[0]
        for h in range(1, N_HEADS):
            dkr = dkr + dkr_ref[h]
        _, vjp = jax.vjp(_mla_latent, c_ref[:, :256], c_ref[:, 256:384], c_ref[:, 384:], gq_ref[...], gkv_ref[...], gkr_ref[...],
                         cos_ref[...], sin_ref[...])
        a, b, r, d1, d2, d3, _, _ = vjp((dcq_ref[...], dckv_ref[...], dkr))
        dc_ref[:, :256] = a.astype(BF16)
        dc_ref[:, 256:384] = b.astype(BF16)
        dc_ref[:, 384:] = r.astype(BF16)
        _acc(dgq_ref, d1)
        _acc(dgkv_ref, d2)
        _acc(dgkr_ref, d3)

    return _rows("mla_prep1_bwd", body, [c, cos, sin, dcq, dckv, dkr_heads], [_sds(c.shape, BF16)], tt=512,
                 consts=[gq, gkv, gkr], accs=[_sds(gq.shape), _sds(gkv.shape), _sds(gkr.shape)])


def _mla_heads(qn, qr, kn, gqn, gqr, gkn, cos, sin):
    return _rms(qn, gqn) * MLA_SCALE, _rms_rope(qr, gqr, cos, sin) * MLA_SCALE, _rms(kn, gkn)


def _mla_prep2_fwd(qa, kv, gqn, gqr, gkn, cos, sin):
    t = qa.shape[0]

    def body(qa_ref, kv_ref, cos_ref, sin_ref, gqn_ref, gqr_ref, gkn_ref, qn_ref, qr_ref, kn_ref, v_ref):
        for h in range(N_HEADS):
            a, b, c = _mla_heads(_heads_in(qa_ref, h), _heads_in(qa_ref, N_HEADS + h), _heads_in(kv_ref, h),
                                 gqn_ref[...], gqr_ref[...], gkn_ref[...], cos_ref[...], sin_ref[...])
            qn_ref[h] = a.astype(BF16)
            qr_ref[h] = b.astype(BF16)
            kn_ref[h] = c.astype(BF16)
            v_ref[h] = _heads_in(kv_ref, N_HEADS + h).astype(BF16)

    hm = _sds((N_HEADS, t, HEAD), BF16)
    return _rows("mla_prep2", body, [qa, kv, cos, sin], [hm, hm, hm, hm], tt=256, consts=[gqn, gqr, gkn])


def _mla_prep2_bwd(qa, kv, gqn, gqr, gkn, cos, sin, dqn, dqr, dkn, dv):
    def body(qa_ref, kv_ref, cos_ref, sin_ref, dqn_ref, dqr_ref, dkn_ref, dv_ref, gqn_ref, gqr_ref, gkn_ref,
             dqa_ref, dkv_ref, d1_ref, d2_ref, d3_ref):
        d1 = jnp.zeros(gqn_ref.shape, F32)
        d2 = jnp.zeros(gqr_ref.shape, F32)
        d3 = jnp.zeros(gkn_ref.shape, F32)
        for h in range(N_HEADS):
            _, vjp = jax.vjp(_mla_heads, _heads_in(qa_ref, h), _heads_in(qa_ref, N_HEADS + h), _heads_in(kv_ref, h),
                             gqn_ref[...], gqr_ref[...], gkn_ref[...], cos_ref[...], sin_ref[...])
            a, b, c, e1, e2, e3, _, _ = vjp((dqn_ref[h], dqr_ref[h], dkn_ref[h]))
            dqa_ref[:, h * HEAD:(h + 1) * HEAD] = a.astype(BF16)
            dqa_ref[:, (N_HEADS + h) * HEAD:(N_HEADS + h + 1) * HEAD] = b.astype(BF16)
            dkv_ref[:, h * HEAD:(h + 1) * HEAD] = c.astype(BF16)
            dkv_ref[:, (N_HEADS + h) * HEAD:(N_HEADS + h + 1) * HEAD] = dv_ref[h].astype(BF16)
            d1, d2, d3 = d1 + e1, d2 + e2, d3 + e3
        _acc(d1_ref, d1)
        _acc(d2_ref, d2)
        _acc(d3_ref, d3)

    return _rows("mla_prep2_bwd", body, [qa, kv, cos, sin, dqn, dqr, dkn, dv], [_sds(qa.shape, BF16), _sds(kv.shape, BF16)],
                 tt=256, consts=[gqn, gqr, gkn], accs=[_sds(gqn.shape), _sds(gqr.shape), _sds(gkn.shape)])


def _mla_attn_fwd(qn, qr, kn, kr, v):
    nh, t, _ = qn.shape
    bq = _q_block(t)
    per = bq // ATT_BLOCK

    def kern(qn_ref, qr_ref, kn_ref, kr_ref, v_ref, o_ref, lse_ref):
        qb = pl.program_id(1)
        qv = jnp.concatenate([qn_ref[0], qr_ref[0]], axis=1)
        order = _key_order(bq, qb)

        def body(kb, carry, diagonal):
            acc, m, l = carry
            off = pl.multiple_of(kb * ATT_BLOCK, ATT_BLOCK)
            kv = jnp.concatenate([kn_ref[0, pl.ds(off, ATT_BLOCK), :], kr_ref[pl.ds(off, ATT_BLOCK), :]], axis=1)
            s = _dot_nt(qv, kv)
            if diagonal:
                s = jnp.where(order >= kb * ATT_BLOCK, s, -jnp.inf)
            m_new = jnp.maximum(m, jnp.max(s, axis=1, keepdims=True))
            alpha = jnp.exp(m - m_new)
            p = jnp.exp(s - m_new)
            acc = acc * alpha + _dot(p.astype(BF16), v_ref[0, pl.ds(off, ATT_BLOCK), :])
            return acc, m_new, l * alpha + jnp.sum(p, axis=1, keepdims=True)

        init = (jnp.zeros((bq, HEAD), F32), jnp.full((bq, 1), -jnp.inf, F32), jnp.zeros((bq, 1), F32))
        carry = lax.fori_loop(0, qb * per, lambda i, c: body(i, c, False), init)
        acc, m, l = lax.fori_loop(qb * per, (qb + 1) * per, lambda i, c: body(i, c, True), carry)
        o_ref[...] = acc / l
        lse_ref[...] = jnp.broadcast_to(m + jnp.log(l), (bq, HEAD))

    blk = pl.BlockSpec((1, bq, HEAD), lambda h, i: (h, i, 0))
    full = pl.BlockSpec((1, t, HEAD), lambda h, i: (h, 0, 0))
    tok = pl.BlockSpec((bq, HEAD), lambda h, i: (i, h))
    return pl.pallas_call(
        kern, name="mla_attn_fwd", grid=(nh, t // bq),
        in_specs=[blk, blk, full, pl.BlockSpec((t, HEAD), lambda h, i: (0, 0)), full],
        out_specs=[tok, tok], out_shape=[_sds((t, nh * HEAD)), _sds((t, nh * HEAD))],
        compiler_params=_cparams(dimension_semantics=("parallel", "arbitrary")),
    )(qn, qr, kn, kr, v)


def _mla_attn_bwd(qn, qr, kn, kr, v, o, lse, do):
    nh, t, _ = qn.shape
    bq = _q_block(t)
    per = bq // ATT_BLOCK

    def kern(qn_ref, qr_ref, kn_ref, kr_ref, v_ref, o_ref, lse_ref, do_ref, dqn_ref, dqr_ref, dkn_ref, dkr_ref, dv_ref):
        qb = pl.program_id(1)

        @pl.when(qb == 0)
        def _():
            dkn_ref[...] = jnp.zeros(dkn_ref.shape, F32)
            dkr_ref[...] = jnp.zeros(dkr_ref.shape, F32)
            dv_ref[...] = jnp.zeros(dv_ref.shape, F32)

        qv = jnp.concatenate([qn_ref[0], qr_ref[0]], axis=1)
        dov = do_ref[...]
        dob = dov.astype(BF16)
        delta = jnp.sum(dov * o_ref[...], axis=1, keepdims=True)
        lse_col = lse_ref[:, :1]
        order = _key_order(bq, qb)

        def body(kb, dq, diagonal):
            off = pl.multiple_of(kb * ATT_BLOCK, ATT_BLOCK)
            kv = jnp.concatenate([kn_ref[0, pl.ds(off, ATT_BLOCK), :], kr_ref[pl.ds(off, ATT_BLOCK), :]], axis=1)
            vv = v_ref[0, pl.ds(off, ATT_BLOCK), :]
            p = jnp.exp(_dot_nt(qv, kv) - lse_col)
            if diagonal:
                p = jnp.where(order >= kb * ATT_BLOCK, p, 0.0)
            ds = (p * (_dot_nt(dob, vv) - delta)).astype(BF16)
            dk = _dot_tn(ds, qv)
            dkn_ref[0, pl.ds(off, ATT_BLOCK), :] += dk[:, :HEAD]
            dkr_ref[0, pl.ds(off, ATT_BLOCK), :] += dk[:, HEAD:]
            dv_ref[0, pl.ds(off, ATT_BLOCK), :] += _dot_tn(p.astype(BF16), dob)
            return dq + _dot(ds, kv)

        dq = lax.fori_loop(0, qb * per, lambda i, c: body(i, c, False), jnp.zeros((bq, 2 * HEAD), F32))
        dq = lax.fori_loop(qb * per, (qb + 1) * per, lambda i, c: body(i, c, True), dq)
        dqn_ref[0] = dq[:, :HEAD]
        dqr_ref[0] = dq[:, HEAD:]

    hm = _sds((nh, t, HEAD))
    blk = pl.BlockSpec((1, bq, HEAD), lambda h, i: (h, i, 0))
    full = pl.BlockSpec((1, t, HEAD), lambda h, i: (h, 0, 0))
    tok = pl.BlockSpec((bq, HEAD), lambda h, i: (i, h))
    return pl.pallas_call(
        kern, name="mla_attn_bwd", grid=(nh, t // bq),
        in_specs=[blk, blk, full, pl.BlockSpec((t, HEAD), lambda h, i: (0, 0)), full, tok, tok, tok],
        out_specs=[blk, blk, full, full, full], out_shape=[hm, hm, hm, hm, hm],
        compiler_params=_cparams(dimension_semantics=("parallel", "arbitrary")),
    )(qn, qr, kn, kr, v, o, lse, do)


def _rope_tables(t):
    inv_freq = ROPE_THETA ** (-jnp.arange(0, MLA_ROPE, 2, dtype=F32) / MLA_ROPE)
    ang = jnp.arange(t, dtype=F32)[:, None] * inv_freq[None, :]
    c, s = jnp.cos(ang), jnp.sin(ang)
    pad = ((0, 0), (0, LANES - MLA_ROPE))
    return jnp.pad(jnp.concatenate([c, c], axis=1), pad), jnp.pad(jnp.concatenate([-s, s], axis=1), pad)


def _pad_lanes(v, n=LANES):
    return jnp.pad(v, (0, n - v.shape[0])).reshape(1, n)


def _mla_layout(w_down, w_uq, w_ukv):
    w_down_p = jnp.pad(w_down, ((0, 0), (0, MLA_C - w_down.shape[1])))
    uq = w_uq.reshape(w_uq.shape[0], N_HEADS, MLA_QK)
    rope = jnp.pad(uq[:, :, HEAD:], ((0, 0), (0, 0), (0, LANES - MLA_ROPE)))
    w_uq_p = jnp.concatenate([uq[:, :, :HEAD].reshape(-1, N_HEADS * HEAD), rope.reshape(-1, N_HEADS * LANES)], axis=1)
    ukv = w_ukv.reshape(w_ukv.shape[0], N_HEADS, 2 * HEAD)
    w_ukv_p = jnp.concatenate([ukv[:, :, :HEAD].reshape(-1, N_HEADS * HEAD), ukv[:, :, HEAD:].reshape(-1, N_HEADS * HEAD)], axis=1)
    return w_down_p, w_uq_p, w_ukv_p


def _mla_unlayout(d_down_p, d_uq_p, d_ukv_p):
    d_down = d_down_p[:, :256 + HEAD + MLA_ROPE]
    nope = d_uq_p[:, :N_HEADS * HEAD].reshape(-1, N_HEADS, HEAD)
    rope = d_uq_p[:, N_HEADS * HEAD:].reshape(-1, N_HEADS, LANES)[:, :, :MLA_ROPE]
    d_uq = jnp.concatenate([nope, rope], axis=2).reshape(-1, N_HEADS * MLA_QK)
    kn = d_ukv_p[:, :N_HEADS * HEAD].reshape(-1, N_HEADS, HEAD)
    vv = d_ukv_p[:, N_HEADS * HEAD:].reshape(-1, N_HEADS, HEAD)
    d_ukv = jnp.concatenate([kn, vv], axis=2).reshape(-1, N_HEADS * 2 * HEAD)
    return d_down, d_uq, d_ukv


def _mla_weight_shapes():
    return (_sds((1024, MLA_C), BF16), _sds((1, 256)), _sds((1, HEAD)), _sds((256, 2048), BF16), _sds((HEAD, 2048), BF16),
            _sds((1, HEAD)), _sds((1, HEAD)), _sds((1, HEAD)), _sds((1, HEAD)), _sds((1024, 1024), BF16),
            _sds((4096, HEAD)), _sds((4096, HEAD)))


def _mla_fwd(x, h, w_down, gq, gkv, w_uq, w_ukv, gqn, gqr, gkn, gkr, w_out, cos, sin):
    c = _mm("mla_down", h, w_down)
    cq, ckv, kr = _mla_prep1_fwd(c, gq, gkv, gkr, cos, sin)
    qa = _mm("mla_uq", cq, w_uq)
    kv = _mm("mla_ukv", ckv, w_ukv)
    qn, qr, kn, v = _mla_prep2_fwd(qa, kv, gqn, gqr, gkn, cos, sin)
    o, lse = _mla_attn_fwd(qn, qr, kn, kr, v)
    y = _mm("mla_out", o, w_out, add=x)
    return y, (h, c, cq, ckv, kr, qa, kv, qn, qr, kn, v, o, lse)


def _mla_bwd(saved, dy, w_down, gq, gkv, w_uq, w_ukv, gqn, gqr, gkn, gkr, w_out, cos, sin):
    h, c, cq, ckv, kr, qa, kv, qn, qr, kn, v, o, lse = saved
    d_wout = _mm("mla_out_wgrad", o, dy, ta=True, out_dtype=GRAD_DTYPE)
    do = _mm("mla_out_dgrad", dy, w_out, tb=True)
    dqn, dqr, dkn, dkr, dv = _mla_attn_bwd(qn, qr, kn, kr, v, o, lse, do)
    dqa, dkv, dgqn, dgqr, dgkn = _mla_prep2_bwd(qa, kv, gqn, gqr, gkn, cos, sin, dqn, dqr, dkn, dv)
    d_wuq = _mm("mla_uq_wgrad", cq, dqa, ta=True, out_dtype=GRAD_DTYPE)
    d_wukv = _mm("mla_ukv_wgrad", ckv, dkv, ta=True, out_dtype=GRAD_DTYPE)
    dcq = _mm("mla_uq_dgrad", dqa, w_uq, tb=True)
    dckv = _mm("mla_ukv_dgrad", dkv, w_ukv, tb=True)
    dc, dgq, dgkv, dgkr = _mla_prep1_bwd(c, gq, gkv, gkr, cos, sin, dcq, dckv, dkr)
    d_wdown = _mm("mla_down_wgrad", h, dc, ta=True, out_dtype=GRAD_DTYPE)
    dh = _mm("mla_down_dgrad", dc, w_down, tb=True)
    return dh, d_wdown, dgq, dgkv, d_wuq, d_wukv, dgqn, dgqr, dgkn, dgkr, d_wout


def _loss_head(y, target):
    d = y.shape[1]

    def body(y_ref, t_ref, dy_ref, l_ref):
        err = y_ref[...] - t_ref[...]
        dy_ref[...] = err * (1.0 / d)
        part = 0.5 * jnp.sum(jnp.sum(err * err, axis=1, keepdims=True) * (1.0 / d), axis=0, keepdims=True)
        _acc(l_ref, jnp.broadcast_to(part, (1, LANES)))

    return _rows("loss_head", body, [y, target], [_sds(y.shape)], tt=512, accs=[_sds((1, LANES))])


MESH_ID = pl.DeviceIdType.MESH
HBM_SPEC = pl.BlockSpec(memory_space=pltpu.HBM)


def _all_gather(name, x):
    m_per, n = x.shape

    def body(x_ref, out_ref, send_sems, recv_sems, local_sem):
        x, y, c = lax.axis_index("x"), lax.axis_index("y"), lax.axis_index("c")
        me, sibling = (x, y, c), (x, y, 1 - c)
        chips = [(1 - x, y), (x, 1 - y), (1 - x, 1 - y)]

        def rows(px, py, pc):
            return out_ref.at[pl.ds((4 * px + 2 * py + pc) * m_per, m_per), :]

        def copy(k, block, to, src=None):
            return pltpu.make_async_remote_copy(
                src_ref=rows(*block) if src is None else src, dst_ref=rows(*block),
                send_sem=send_sems.at[k], recv_sem=recv_sems.at[k], device_id=to, device_id_type=MESH_ID)

        mine = pltpu.make_async_copy(x_ref, rows(*me), local_sem)
        mine.start()
        first = [copy(0, me, sibling, src=x_ref)]
        first += [copy(1 + j, me, (*chip, c), src=x_ref) for j, chip in enumerate(chips)]
        for cp in first:
            cp.start()
        passed = [copy(4 + j, (*chip, c), sibling) for j, chip in enumerate(chips)]
        for j, chip in enumerate(chips):
            copy(1 + j, (*chip, c), me).wait_recv()
            passed[j].start()
        copy(0, sibling, me).wait_recv()
        for j, chip in enumerate(chips):
            copy(4 + j, (*chip, 1 - c), me).wait_recv()
        for cp in first + passed:
            cp.wait_send()
        mine.wait()

    return pl.pallas_call(
        body, name=name,
        out_shape=jax.ShapeDtypeStruct((N_DEV * m_per, n), x.dtype),
        in_specs=[HBM_SPEC], out_specs=HBM_SPEC,
        scratch_shapes=[pltpu.SemaphoreType.DMA((7,)), pltpu.SemaphoreType.DMA((7,)), pltpu.SemaphoreType.DMA],
    )(x)


def _all_gather_groups(name, xs):
    ng = len(xs)

    def body(*refs):
        x_refs, out_refs, token = refs[:ng], refs[ng:2 * ng], refs[2 * ng]
        send_sems, recv_sems, local_sems = refs[2 * ng + 1:]
        token[...] = jnp.zeros(token.shape, F32)
        x, y, c = lax.axis_index("x"), lax.axis_index("y"), lax.axis_index("c")
        me, sibling = (x, y, c), (x, y, 1 - c)
        chips = [(1 - x, y), (x, 1 - y), (1 - x, 1 - y)]

        def copy(g, k, block, to, src=None):
            px, py, pc = block
            dst = out_refs[g].at[4 * px + 2 * py + pc]
            return pltpu.make_async_remote_copy(
                src_ref=dst if src is None else src, dst_ref=dst,
                send_sem=send_sems.at[g, k], recv_sem=recv_sems.at[g, k], device_id=to, device_id_type=MESH_ID)

        mine = [pltpu.make_async_copy(x_refs[g], out_refs[g].at[4 * x + 2 * y + c], local_sems.at[g]) for g in range(ng)]
        for cp in mine:
            cp.start()
        first = []
        for g in range(ng):
            first.append(copy(g, 0, me, sibling, src=x_refs[g]))
            first += [copy(g, 1 + j, me, (*chip, c), src=x_refs[g]) for j, chip in enumerate(chips)]
        for cp in first:
            cp.start()
        passed = []
        for j, chip in enumerate(chips):
            for g in range(ng):
                copy(g, 1 + j, (*chip, c), me).wait_recv()
                passed.append(copy(g, 4 + j, (*chip, c), sibling))
                passed[-1].start()
        for g in range(ng):
            copy(g, 0, sibling, me).wait_recv()
            for j, chip in enumerate(chips):
                copy(g, 4 + j, (*chip, 1 - c), me).wait_recv()
        for cp in first + passed:
            cp.wait_send()
        for cp in mine:
            cp.wait()

    return pl.pallas_call(
        body, name=name,
        out_shape=[jax.ShapeDtypeStruct((N_DEV,) + x.shape, x.dtype) for x in xs] + [_sds((8, LANES))],
        in_specs=[HBM_SPEC] * ng, out_specs=[HBM_SPEC] * ng + [pl.BlockSpec(memory_space=pltpu.VMEM)],
        scratch_shapes=[pltpu.SemaphoreType.DMA((ng, 7)), pltpu.SemaphoreType.DMA((ng, 7)), pltpu.SemaphoreType.DMA((ng,))],
    )(*xs)


EFFECT = pltpu.SideEffectType.DATAFLOW_SIDE_EFFECTING
SEM_SPEC = pl.BlockSpec(memory_space=pltpu.SEMAPHORE)


def _push_copies(src_refs, land_refs, send_sems, recv_sems, chunked):
    x, y, c = lax.axis_index("x"), lax.axis_index("y"), lax.axis_index("c")
    me = 4 * x + 2 * y + c
    copies = []
    for g, (src, land) in enumerate(zip(src_refs, land_refs)):
        for k in range(1, N_DEV):
            px = 1 - x if k & 4 else x
            py = 1 - y if k & 2 else y
            pc = 1 - c if k & 1 else c
            copies.append(pltpu.make_async_remote_copy(
                src_ref=src.at[4 * px + 2 * py + pc] if chunked else src, dst_ref=land.at[me],
                send_sem=send_sems.at[g * (N_DEV - 1) + k - 1], recv_sem=recv_sems.at[g * (N_DEV - 1) + k - 1],
                device_id=(px, py, pc), device_id_type=MESH_ID))
    return copies


def _hbm(x):
    return pltpu.with_memory_space_constraint(x, pltpu.HBM)


def _push_start(name, srcs, lands, chunked):
    ng = len(srcs)

    def body(*refs):
        for cp in _push_copies(refs[:ng], refs[ng:2 * ng], refs[2 * ng], refs[2 * ng + 1], chunked):
            cp.start()
        refs[-1][...] = jnp.zeros(refs[-1].shape, F32)

    bufs = list(srcs) + list(lands)
    outs = pl.pallas_call(
        body, name=name,
        out_shape=(pltpu.SemaphoreType.DMA((ng * (N_DEV - 1),)), pltpu.SemaphoreType.DMA((ng * (N_DEV - 1),)),
                   *[pltpu.HBM(b.shape, b.dtype) for b in bufs], jax.ShapeDtypeStruct((8, LANES), F32)),
        in_specs=[HBM_SPEC] * (2 * ng),
        out_specs=(SEM_SPEC, SEM_SPEC, *[HBM_SPEC] * (2 * ng), pl.BlockSpec(memory_space=pltpu.VMEM)),
        input_output_aliases={i: 2 + i for i in range(2 * ng)},
        compiler_params=pltpu.CompilerParams(has_side_effects=EFFECT),
    )(*[_hbm(b) for b in bufs])
    return outs[0], outs[1], list(outs[2:2 + ng]), list(outs[2 + ng:2 + 2 * ng]), outs[-1]


def _push_wait(name, started, after, chunked):
    send_sems, recv_sems, srcs, lands, _ = started
    ng = len(srcs)

    def body(*refs):
        copies = _push_copies(refs[:ng], refs[ng:2 * ng], refs[2 * ng], refs[2 * ng + 1], chunked)
        for cp in copies:
            cp.wait_send()
        for cp in copies:
            cp.wait_recv()

    bufs = srcs + lands
    outs = pl.pallas_call(
        body, name=name,
        out_shape=tuple(pltpu.HBM(b.shape, b.dtype) for b in bufs),
        in_specs=[HBM_SPEC] * (2 * ng) + [SEM_SPEC, SEM_SPEC, pl.BlockSpec(memory_space=pl.ANY)],
        out_specs=tuple([HBM_SPEC] * (2 * ng)),
        input_output_aliases={i: i for i in range(2 * ng)},
        compiler_params=pltpu.CompilerParams(has_side_effects=EFFECT),
    )(*bufs, send_sems, recv_sems, after)
    return list(outs[:ng]), list(outs[ng:])


def _sum_adam_devices(sent, recv, dev, w, m, v):
    ndev, r, c_ = recv.shape
    tr = _pick(r, (256, 128, 96, 32))

    def body(dev_ref, own_ref, r_ref, w_ref, m_ref, v_ref, g_ref, d_ref, mo_ref, vo_ref):
        me = dev_ref[0]
        g = jnp.where(me == 0, own_ref[0], r_ref[0]).astype(F32)
        for j in range(1, ndev):
            g = g + jnp.where(me == j, own_ref[0], r_ref[j]).astype(F32)
        g_ref[...] = g
        d_ref[...], mo_ref[...], vo_ref[...] = _adam(w_ref[...], g, m_ref[...], v_ref[...])

    row = pl.BlockSpec((tr, c_), lambda i, dev_ref: (i, 0))
    return pl.pallas_call(
        body, name="sum_adam",
        grid_spec=pltpu.PrefetchScalarGridSpec(
            num_scalar_prefetch=1, grid=(r // tr,),
            in_specs=[pl.BlockSpec((1, tr, c_), lambda i, dev_ref: (dev_ref[0], i, 0)),
                      pl.BlockSpec((ndev, tr, c_), lambda i, dev_ref: (0, i, 0)), row, row, row],
            out_specs=[row, row, row, row]),
        out_shape=[_sds((r, c_))] * 4,
        compiler_params=_cparams(dimension_semantics=("arbitrary",)),
    )(dev, sent, recv, w, m, v)


def _cols_from_shards(w, width):
    ns, r, cs = w.shape
    tr = _pick(r, (256, 128))

    def body(w_ref, o_ref):
        parts = [w_ref[j] for j in range(ns)]
        if width > ns * cs:
            parts.append(jnp.zeros((tr, width - ns * cs), w.dtype))
        o_ref[...] = jnp.concatenate(parts, axis=1)

    return pl.pallas_call(
        body, name="cols_from_shards", grid=(r // tr,),
        in_specs=[pl.BlockSpec((ns, tr, cs), lambda i: (0, i, 0))], out_specs=pl.BlockSpec((tr, width), lambda i: (i, 0)),
        out_shape=jax.ShapeDtypeStruct((r, width), w.dtype), compiler_params=_cparams(dimension_semantics=("arbitrary",)),
    )(w)


def _shards_from_cols(g, cs):
    r, width = g.shape
    tr = _pick(r, (256, 128))

    def body(g_ref, o_ref):
        for j in range(N_DEV):
            o_ref[j] = g_ref[:, j * cs:(j + 1) * cs]

    return pl.pallas_call(
        body, name="shards_from_cols", grid=(r // tr,),
        in_specs=[pl.BlockSpec((tr, width), lambda i: (i, 0))], out_specs=pl.BlockSpec((N_DEV, tr, cs), lambda i: (0, i, 0)),
        out_shape=jax.ShapeDtypeStruct((N_DEV, r, cs), g.dtype), compiler_params=_cparams(dimension_semantics=("arbitrary",)),
    )(g)


def _adam(w, g, m, v):
    m = ADAM_B1 * m + (1.0 - ADAM_B1) * g
    v = ADAM_B2 * v + (1.0 - ADAM_B2) * (g * g)
    m_hat = m / (1.0 - ADAM_B1 ** ADAM_STEP)
    v_hat = v / (1.0 - ADAM_B2 ** ADAM_STEP)
    return -ADAM_LR * (m_hat / (jnp.sqrt(v_hat) + ADAM_EPS) + ADAM_WD * w), m, v


def _sum_devices(gathered):
    m_all, n = gathered.shape
    m_per = m_all // N_DEV

    def body(x_ref, o_ref):
        s = x_ref[0:m_per, :]
        for j in range(1, N_DEV):
            s = s + x_ref[j * m_per:(j + 1) * m_per, :]
        o_ref[...] = s

    return pl.pallas_call(body, name="sum_devices", out_shape=_sds((m_per, n)), compiler_params=_cparams())(gathered)


def _adam_small(w, g, m, v):
    def body(w_ref, g_ref, m_ref, v_ref, d_ref, mo_ref, vo_ref):
        d_ref[...], mo_ref[...], vo_ref[...] = _adam(w_ref[...], g_ref[...], m_ref[...], v_ref[...])

    return pl.pallas_call(body, name="adam_small", out_shape=[_sds(w.shape)] * 3, compiler_params=_cparams())(w, g, m, v)


N_LAYERS = 4
_MIXER = ("dn", "sb", "mla")
_MIXER_PARAMS = {
    "dn": ("dn_w_in", "dn_conv_w", "dn_a_log", "dn_dt_bias", "dn_out_norm", "dn_w_out"),
    "sb": ("sb_w_qkv", "sb_q_norm", "sb_k_norm", "sb_w_out"),
    "mla": ("mla_w_down", "mla_q_a_norm", "mla_kv_a_norm", "mla_w_uq", "mla_w_ukv", "mla_q_nope_norm", "mla_q_rope_norm",
            "mla_k_nope_norm", "mla_k_rope_norm", "mla_w_out"),
}
_BIG_AXIS = {"dn_w_in": 1, "dn_w_out": 0, "sb_w_qkv": 1, "sb_w_out": 0, "mla_w_down": 0, "mla_w_uq": 1, "mla_w_ukv": 1,
             "mla_w_out": 0, "ffn_w_gate_up": 1, "ffn_w_down": 0}


def _weight_names():
    names = []
    for i in range(N_LAYERS):
        p = "l%d_" % i
        names += [p + "mix_norm"] + [p + n for n in _MIXER_PARAMS[_MIXER[i % 3]]] + [p + "ffn_norm", p + "ffn_w_gate_up", p + "ffn_w_down"]
    return names


WEIGHTS = _weight_names()
BIG = [n for n in WEIGHTS if n[3:] in _BIG_AXIS]
SMALL = [n for n in WEIGHTS if n[3:] not in _BIG_AXIS]
CONV = [n for n in SMALL if n.endswith("conv_w")]


def _ceil_to(n, k):
    return -(-n // k) * k


def _pack(arrs, cols, row_mult):
    parts = []
    for a in arrs:
        f = a.reshape(-1)
        parts.append(jnp.pad(f, (0, _ceil_to(f.shape[0], cols) - f.shape[0])))
    flat = jnp.concatenate(parts)
    rows = _ceil_to(flat.shape[0] // cols, row_mult)
    return jnp.pad(flat, (0, rows * cols - flat.shape[0])).reshape(rows, cols)


def _unpack(buf, shapes):
    cols = buf.shape[-1]
    out, r0 = [], 0
    for s in shapes:
        n = math.prod(s)
        nr = _ceil_to(n, cols) // cols
        out.append(buf[r0:r0 + nr].reshape(-1)[:n].reshape(s))
        r0 += nr
    return out


def _layer_groups(i):
    by = {"gu": (704, []), "down": (1024, []), "out": (1024, []), "dn_in": (514, []), "sb_qkv": (384, []), "mla": (512, [])}
    key = {"ffn_w_gate_up": "gu", "ffn_w_down": "down", "dn_w_in": "dn_in", "sb_w_qkv": "sb_qkv", "mla_w_down": "mla",
           "mla_w_uq": "mla", "mla_w_ukv": "mla"}
    for n in BIG:
        if n.startswith("l%d_" % i):
            by[key.get(n[3:], "out")][1].append(n)
    return [g for g in by.values() if g[1]]


LAYER_GROUPS = [_layer_groups(i) for i in range(N_LAYERS)]
N_FFN_GROUPS = 2


def _stack_group(grp, get):
    width, names = grp
    if len(names) == 1 and get(names[0]).shape[1] == width:
        return get(names[0])
    return jnp.concatenate([jnp.pad(get(n), ((0, 0), (0, width - get(n).shape[1]))) for n in names], axis=0)


def _unstack_group(grp, buf, shape_of):
    if len(grp[1]) == 1 and tuple(shape_of(grp[1][0])) == tuple(buf.shape[-2:]):
        return [buf]
    out, r0 = [], 0
    for n in grp[1]:
        rs, cs = shape_of(n)
        out.append(buf[..., r0:r0 + rs, :cs])
        r0 += rs
    return out


def kernel(x, l0_mix_norm, l0_dn_w_in, l0_dn_conv_w, l0_dn_a_log, l0_dn_dt_bias, l0_dn_out_norm, l0_dn_w_out, l0_ffn_norm, l0_ffn_w_gate_up, l0_ffn_w_down, l1_mix_norm, l1_sb_w_qkv, l1_sb_q_norm, l1_sb_k_norm, l1_sb_w_out, l1_ffn_norm, l1_ffn_w_gate_up, l1_ffn_w_down, l2_mix_norm, l2_mla_w_down, l2_mla_q_a_norm, l2_mla_kv_a_norm, l2_mla_w_uq, l2_mla_w_ukv, l2_mla_q_nope_norm, l2_mla_q_rope_norm, l2_mla_k_nope_norm, l2_mla_k_rope_norm, l2_mla_w_out, l2_ffn_norm, l2_ffn_w_gate_up, l2_ffn_w_down, l3_mix_norm, l3_dn_w_in, l3_dn_conv_w, l3_dn_a_log, l3_dn_dt_bias, l3_dn_out_norm, l3_dn_w_out, l3_ffn_norm, l3_ffn_w_gate_up, l3_ffn_w_down, loss_target, m_l0_mix_norm, m_l0_dn_w_in, m_l0_dn_conv_w, m_l0_dn_a_log, m_l0_dn_dt_bias, m_l0_dn_out_norm, m_l0_dn_w_out, m_l0_ffn_norm, m_l0_ffn_w_gate_up, m_l0_ffn_w_down, m_l1_mix_norm, m_l1_sb_w_qkv, m_l1_sb_q_norm, m_l1_sb_k_norm, m_l1_sb_w_out, m_l1_ffn_norm, m_l1_ffn_w_gate_up, m_l1_ffn_w_down, m_l2_mix_norm, m_l2_mla_w_down, m_l2_mla_q_a_norm, m_l2_mla_kv_a_norm, m_l2_mla_w_uq, m_l2_mla_w_ukv, m_l2_mla_q_nope_norm, m_l2_mla_q_rope_norm, m_l2_mla_k_nope_norm, m_l2_mla_k_rope_norm, m_l2_mla_w_out, m_l2_ffn_norm, m_l2_ffn_w_gate_up, m_l2_ffn_w_down, m_l3_mix_norm, m_l3_dn_w_in, m_l3_dn_conv_w, m_l3_dn_a_log, m_l3_dn_dt_bias, m_l3_dn_out_norm, m_l3_dn_w_out, m_l3_ffn_norm, m_l3_ffn_w_gate_up, m_l3_ffn_w_down, v_l0_mix_norm, v_l0_dn_w_in, v_l0_dn_conv_w, v_l0_dn_a_log, v_l0_dn_dt_bias, v_l0_dn_out_norm, v_l0_dn_w_out, v_l0_ffn_norm, v_l0_ffn_w_gate_up, v_l0_ffn_w_down, v_l1_mix_norm, v_l1_sb_w_qkv, v_l1_sb_q_norm, v_l1_sb_k_norm, v_l1_sb_w_out, v_l1_ffn_norm, v_l1_ffn_w_gate_up, v_l1_ffn_w_down, v_l2_mix_norm, v_l2_mla_w_down, v_l2_mla_q_a_norm, v_l2_mla_kv_a_norm, v_l2_mla_w_uq, v_l2_mla_w_ukv, v_l2_mla_q_nope_norm, v_l2_mla_q_rope_norm, v_l2_mla_k_nope_norm, v_l2_mla_k_rope_norm, v_l2_mla_w_out, v_l2_ffn_norm, v_l2_ffn_w_gate_up, v_l2_ffn_w_down, v_l3_mix_norm, v_l3_dn_w_in, v_l3_dn_conv_w, v_l3_dn_a_log, v_l3_dn_dt_bias, v_l3_dn_out_norm, v_l3_dn_w_out, v_l3_ffn_norm, v_l3_ffn_w_gate_up, v_l3_ffn_w_down):
    a = dict(locals())
    return _train_step(a)


def _train_step(a):
    mx, my, mc = lax.axis_index("x"), lax.axis_index("y"), lax.axis_index("c")
    dev = 4 * mx + 2 * my + mc
    dev_arr = jnp.reshape(dev, (1,)).astype(jnp.int32)
    t, d = a["x"].shape[1], a["x"].shape[2]
    xs = a["x"].reshape(t, d)
    target = a["loss_target"].reshape(t, d)

    full = {}

    def unpack(groups, bufs):
        for grp, buf in zip(groups, bufs):
            for n, shards in zip(grp[1], _unstack_group(grp, buf, lambda n: a[n].shape)):
                kind = n[3:]
                if kind == "ffn_w_gate_up":
                    full[n] = shards
                elif _BIG_AXIS[kind] == 0:
                    full[n] = shards.reshape(N_DEV * shards.shape[1], shards.shape[2])
                else:
                    width = DN_PROJ if kind == "dn_w_in" else N_DEV * shards.shape[2]
                    full[n] = _cols_from_shards(shards, width)

    def local_shards(groups):
        return [_stack_group(grp, lambda n: a[n].astype(BF16)) for grp in groups]

    pushed_groups = {"l0_gate_up": LAYER_GROUPS[0][:1], "l0_down": LAYER_GROUPS[0][1:N_FFN_GROUPS]}
    pushed_groups.update({"l%d" % i: LAYER_GROUPS[i] for i in range(1, N_LAYERS)})
    conv_pack = _pack([a[n] for n in CONV], LANES, 8)
    first_groups = LAYER_GROUPS[0][N_FFN_GROUPS:]
    *first_bufs, conv_all, gathered = _all_gather_groups("gather_weights", local_shards(first_groups) + [conv_pack])
    unpack(first_groups, first_bufs)
    for n, parts in zip(CONV, zip(*[_unpack(conv_all[j], [a[n].shape for n in CONV]) for j in range(N_DEV)])):
        full[n] = jnp.concatenate(parts, axis=1)
    after_first = gathered[0, 0].astype(BF16)
    gathers, started = {}, jnp.zeros((), F32)
    for tag, groups in pushed_groups.items():
        srcs = local_shards(groups)
        srcs[-1] = srcs[-1] + after_first
        lands = [lax.dynamic_update_index_in_dim(lax.empty((N_DEV,) + s.shape, s.dtype), s, dev, 0) for s in srcs]
        gathers[tag] = _push_start("gather_start_" + tag, srcs, lands, False)
        started = started + gathers[tag][-1][0, 0]

    def arrived(tag, after):
        unpack(pushed_groups[tag], _push_wait("gather_wait_" + tag, gathers[tag], after, False)[1])

    def vec(n):
        return a[n].reshape(1, -1)

    cos, sin = _rope_tables(t)

    def mixer_args(i):
        p = "l%d_" % i
        kind = _MIXER[i % 3]
        if kind == "dn":
            args = (full[p + "dn_w_in"], full[p + "dn_conv_w"], _pad_lanes(a[p + "dn_a_log"]), _pad_lanes(a[p + "dn_dt_bias"]),
                    vec(p + "dn_out_norm"), full[p + "dn_w_out"])
        elif kind == "sb":
            args = (full[p + "sb_w_qkv"], vec(p + "sb_q_norm"), vec(p + "sb_k_norm"), full[p + "sb_w_out"])
        else:
            w_down, w_uq, w_ukv = _mla_layout(full[p + "mla_w_down"], full[p + "mla_w_uq"], full[p + "mla_w_ukv"])
            args = (w_down, vec(p + "mla_q_a_norm"), vec(p + "mla_kv_a_norm"), w_uq, w_ukv, vec(p + "mla_q_nope_norm"),
                    _pad_lanes(a[p + "mla_q_rope_norm"]), vec(p + "mla_k_nope_norm"), _pad_lanes(a[p + "mla_k_rope_norm"]),
                    full[p + "mla_w_out"], cos, sin)
        return kind, args

    fwd = {"dn": _dn_fwd, "sb": _sb_fwd, "mla": _mla_fwd}
    bwd = {"dn": _dn_bwd, "sb": _sb_bwd, "mla": _mla_bwd}
    saved, layer_args = [], []
    for i in range(N_LAYERS):
        p = "l%d_" % i
        if i > 0:
            arrived("l%d" % i, xs)
        kind, args = mixer_args(i)
        layer_args.append((kind, args))
        gain = vec(p + "mix_norm") + started if i == 0 else vec(p + "mix_norm")
        h = _rmsnorm_fwd("mix_norm", xs, gain)
        x_mid, sv_mix = fwd[kind](xs, h, *args)
        if i == 0:
            arrived("l0_gate_up", x_mid)

        def down_weight(act, p=p, first=(i == 0)):
            if first:
                arrived("l0_down", act)
            return full[p + "ffn_w_down"]

        x_out, sv_ffn = _ffn_fwd(x_mid, vec(p + "ffn_norm"), full[p + "ffn_w_gate_up"], down_weight)
        saved.append((xs, sv_mix, sv_ffn))
        xs = x_out
    dy, loss_part = _loss_head(xs, target)

    grads, big_out = {}, {}

    def grad_shards(n):
        g, (rs, cs) = grads[n], a[n].shape
        if g.ndim == 3:
            return g
        if _BIG_AXIS[n[3:]] == 0:
            return g.reshape(N_DEV, rs, cs)
        return _shards_from_cols(g, cs)

    def push_grads(tag, groups):
        sends = []
        for grp in groups:
            if len(grp[1]) == 1 and a[grp[1][0]].shape[1] == grp[0]:
                sends.append(grad_shards(grp[1][0]))
                continue
            parts = [jnp.pad(grad_shards(n), ((0, 0), (0, 0), (0, grp[0] - a[n].shape[1]))) for n in grp[1]]
            sends.append(jnp.concatenate(parts, axis=1))
        lands = [lax.empty(s.shape, s.dtype) for s in sends]
        return tag, groups, _push_start("grads_start_" + tag, sends, lands, True)

    def finish_grads(push, after):
        tag, groups, pushed = push
        sents, recvs = _push_wait("grads_wait_" + tag, pushed, after, True)
        for grp, sent, recv in zip(groups, sents, recvs):
            packs = [_stack_group(grp, lambda n, pre=pre: a[pre + n]) for pre in ("", "m_", "v_")]
            outs = [_unstack_group(grp, o, lambda n: a[n].shape) for o in _sum_adam_devices(sent, recv, dev_arr, *packs)]
            for j, n in enumerate(grp[1]):
                big_out[n] = [o[j] for o in outs]

    mixer_push, pushes = None, []
    for i in reversed(range(N_LAYERS)):
        p = "l%d_" % i
        kind, args = layer_args[i]
        x_in, sv_mix, sv_ffn = saved[i]
        gain = vec(p + "ffn_norm") if mixer_push is None else vec(p + "ffn_norm") + mixer_push[2][-1][0, 0]
        dx_mid, grads[p + "ffn_norm"], grads[p + "ffn_w_gate_up"], grads[p + "ffn_w_down"] = _ffn_bwd(
            sv_ffn, dy, gain, full[p + "ffn_w_gate_up"], full[p + "ffn_w_down"])
        ffn_push = push_grads("l%d_ffn" % i, LAYER_GROUPS[i][:N_FFN_GROUPS])
        res = bwd[kind](sv_mix, dx_mid, *args)
        dh = res[0]
        if kind == "mla":
            res = list(res)
            res[1], res[4], res[5] = _mla_unlayout(res[1], res[4], res[5])
        for n, g in zip(_MIXER_PARAMS[kind], res[1:]):
            grads[p + n] = g
        mixer_push = push_grads("l%d_mix" % i, LAYER_GROUPS[i][N_FFN_GROUPS:])
        pushed = ffn_push[2][-1][0, 0] + mixer_push[2][-1][0, 0]
        dy, grads[p + "mix_norm"] = _rmsnorm_bwd("mix_norm_bwd", x_in, vec(p + "mix_norm") + pushed, dh, dx_mid)
        pushes += [ffn_push, mixer_push]
    for push in pushes[:-1]:
        finish_grads(push, dy)
    grad_x = dy.reshape(a["x"].shape)

    small_full_shapes = [full[n].shape if n in CONV else a[n].shape for n in SMALL]
    small_grads = []
    for n, s in zip(SMALL, small_full_shapes):
        g = grads[n].reshape(-1)
        small_grads.append(g[:math.prod(s)])
    last_summed = big_out[pushes[-2][1][-1][1][-1]][0]
    small_pack = _pack(small_grads + [loss_part.reshape(-1)], LANES, 8) + jnp.minimum(jnp.abs(last_summed[0, 0]), 0.0)
    small_sum = _sum_devices(_all_gather("gather_small_grads", small_pack))
    small_red = _unpack(small_sum, small_full_shapes + [(LANES,)])
    loss = small_red[-1][0]
    g_small = {}
    for n, g in zip(SMALL, small_red[:-1]):
        if n in CONV:
            cs = a[n].shape[1]
            g = lax.dynamic_slice_in_dim(g, dev * cs, cs, axis=1)
        g_small[n] = g
    small_shapes = [a[n].shape for n in SMALL]
    packs = [_pack([src[n] for n in SMALL], LANES, 8) for src in
             ({n: a[n] for n in SMALL}, g_small, {n: a["m_" + n] for n in SMALL}, {n: a["v_" + n] for n in SMALL})]
    d_small, m_small, v_small = (_unpack(o, small_shapes) for o in _adam_small(*packs))
    finish_grads(mixer_push, small_sum)

    small_out = dict(zip(SMALL, zip([g_small[n] for n in SMALL], d_small, m_small, v_small)))

    def out(k):
        return [small_out[n][k] if n in small_out else big_out[n][k] for n in WEIGHTS]

    return (loss, grad_x, *out(0), *out(1), *out(2), *out(3))
```

```python
import math

import jax
import jax.numpy as jnp
from jax import lax
from jax.experimental import pallas as pl
from jax.experimental.pallas import tpu as pltpu

F32 = jnp.float32
BF16 = jnp.bfloat16
GRAD_DTYPE = BF16
F32X3 = lax.Precision.HIGH

LANES = 128
N_DEV = 8
N_HEADS = 8
HEAD = 128
NORM_EPS = 1e-6
DN_CHUNK = 64
DN_STEP = 4 * DN_CHUNK
ATT_BLOCK = 512
ATT_Q = 512
MLA_ROPE = 64
MLA_QK = 192
ROPE_THETA = 10000.0
VMEM_LIMIT = 56 * 1024 * 1024
MM_TILE_BYTES = 32 * 1024 * 1024

ADAM_LR = 0.001
ADAM_B1 = 0.9
ADAM_B2 = 0.999
ADAM_EPS = 1e-08
ADAM_WD = 0.01
ADAM_STEP = 10


def _cparams(**kw):
    return pltpu.CompilerParams(vmem_limit_bytes=VMEM_LIMIT, **kw)


def _pick(n, cands):
    for c in cands:
        if c <= n and n % c == 0:
            return c
    return n


def _mm(name, a, b, *, ta=False, tb=False, out_dtype=F32, add=None, tm=None, tn=None, tk=None):
    if ta:
        K, M = a.shape
    else:
        M, K = a.shape
    N = b.shape[0] if tb else b.shape[1]
    tm = tm or _pick(M, (1024, 512, 256, 128))
    tn = tn or _pick(N, (1024, 512, 384, 256, 128))
    if tk is None:
        fits = [c for c in (4096, 2048, 1408, 1024, 512, 384, 256, 128)
                if c <= K and K % c == 0 and 2 * c * (tm * a.dtype.itemsize + tn * b.dtype.itemsize) <= MM_TILE_BYTES]
        tk = fits[0] if fits else K
    return _mm_raw(
        name, a, b, ta=ta, tb=tb, out_dtype=out_dtype, add=add, grid=(M // tm, N // tn, K // tk), out_shape=(M, N),
        a_block=(tk, tm) if ta else (tm, tk), a_map=(lambda i, j, k: (k, i)) if ta else (lambda i, j, k: (i, k)),
        b_block=(tn, tk) if tb else (tk, tn), b_map=(lambda i, j, k: (j, k)) if tb else (lambda i, j, k: (k, j)),
        o_block=(tm, tn), o_map=lambda i, j, k: (i, j))


def _mm_raw(name, a, b, *, ta, tb, out_dtype, add, grid, out_shape, a_block, a_map, b_block, b_map, o_block, o_map):
    nk = grid[2]
    tm, tn = o_block
    dn = (((0 if ta else 1,), (1 if tb else 0,)), ((), ()))
    has_add = add is not None

    def kern(*refs):
        if has_add:
            a_ref, b_ref, add_ref, o_ref, acc_ref = refs
        else:
            a_ref, b_ref, o_ref, acc_ref = refs
        k = pl.program_id(2)
        part = lax.dot_general(a_ref[...].astype(BF16), b_ref[...].astype(BF16), dn, preferred_element_type=F32)

        @pl.when(k == 0)
        def _():
            acc_ref[...] = part

        @pl.when(k > 0)
        def _():
            acc_ref[...] += part

        @pl.when(k == nk - 1)
        def _():
            r = acc_ref[...]
            if has_add:
                r = r + add_ref[...]
            o_ref[...] = r.astype(out_dtype)

    in_specs = [pl.BlockSpec(a_block, a_map), pl.BlockSpec(b_block, b_map)]
    args = [a, b]
    if has_add:
        in_specs.append(pl.BlockSpec(o_block, o_map))
        args.append(add)
    return pl.pallas_call(
        kern, name=name,
        grid=grid,
        in_specs=in_specs,
        out_specs=pl.BlockSpec(o_block, o_map),
        out_shape=jax.ShapeDtypeStruct(out_shape, out_dtype),
        scratch_shapes=[pltpu.VMEM((tm, tn), F32)],
        compiler_params=_cparams(dimension_semantics=("parallel", "parallel", "arbitrary")),
    )(*args)


def _rows(name, body, ins, outs, *, tt, consts=(), accs=()):
    in_specs, args = [], []
    first = ins[0][0] if isinstance(ins[0], tuple) else ins[0]
    t = first.shape[-2]
    tt = min(tt, t)
    for x in ins:
        if isinstance(x, tuple):
            arr, bs, im = x
            in_specs.append(pl.BlockSpec(bs, im))
            args.append(arr)
        else:
            in_specs.append(_row_spec(x.shape, tt))
            args.append(x)
    for c in consts:
        in_specs.append(pl.BlockSpec(c.shape, lambda i, _n=c.ndim: (0,) * _n))
        args.append(c)
    out_specs = [_row_spec(o.shape, tt) for o in outs]
    out_specs += [pl.BlockSpec(a.shape, lambda i, _n=len(a.shape): (0,) * _n) for a in accs]
    res = pl.pallas_call(
        body, name=name, grid=(t // tt,),
        in_specs=in_specs, out_specs=out_specs, out_shape=list(outs) + list(accs),
        compiler_params=_cparams(dimension_semantics=("arbitrary",)),
    )(*args)
    return res


def _row_spec(shape, tt):
    if len(shape) == 2:
        return pl.BlockSpec((tt, shape[1]), lambda i: (i, 0))
    return pl.BlockSpec((shape[0], tt, shape[2]), lambda i: (0, i, 0))


def _sds(shape, dtype=F32):
    return jax.ShapeDtypeStruct(tuple(shape), dtype)


def _acc(ref, val):
    i = pl.program_id(0)

    @pl.when(i == 0)
    def _():
        ref[...] = val

    @pl.when(i > 0)
    def _():
        ref[...] += val


def _rms(x, g):
    return x * lax.rsqrt(jnp.mean(x * x, axis=-1, keepdims=True) + NORM_EPS) * g


def _silu(x):
    return x / (1.0 + jnp.exp(-x))


def _softplus(x):
    return jnp.maximum(x, 0.0) + jnp.log(1.0 + jnp.exp(-jnp.abs(x)))


def _sigmoid(x):
    return 1.0 / (1.0 + jnp.exp(-x))


def _rmsnorm_fwd(name, x, g, tt=512):
    def body(x_ref, g_ref, h_ref):
        h_ref[...] = _rms(x_ref[...], g_ref[...]).astype(BF16)

    return _rows(name, body, [x], [_sds(x.shape, BF16)], tt=tt, consts=[g])[0]


def _rmsnorm_bwd(name, x, g, dh, dres, tt=512):
    def body(x_ref, dh_ref, dres_ref, g_ref, dx_ref, dg_ref):
        _, vjp = jax.vjp(_rms, x_ref[...], g_ref[...])
        dx, dg = vjp(dh_ref[...])
        dx_ref[...] = dx + dres_ref[...]
        _acc(dg_ref, dg)

    return _rows(name, body, [x, dh, dres], [_sds(x.shape)], tt=tt, consts=[g], accs=[_sds(g.shape)])


def _ffn_fwd(x, norm_g, w3, down_weight):
    t, d = x.shape
    ns, _, cs = w3.shape
    half = ns // 2
    w2 = w3.reshape(ns * d, cs)
    h = _rmsnorm_fwd("ffn_norm", x, norm_g)
    tm = _pick(t, (1024, 512, 256, 128))
    nm = t // tm

    def gate_up(h_ref, wg_ref, wu_ref, g_ref, u_ref, a_ref):
        hv = h_ref[...]
        g = _dot(hv, wg_ref[...])
        u = _dot(hv, wu_ref[...])
        g_ref[...] = g
        u_ref[...] = u
        a_ref[...] = (_silu(g) * u).astype(BF16)

    hid = pl.BlockSpec((tm, cs), lambda j, i: (j * nm + i, 0))
    g, u, act = pl.pallas_call(
        gate_up, name="ffn_gate_up", grid=(half, nm),
        in_specs=[pl.BlockSpec((tm, d), lambda j, i: (i, 0)), pl.BlockSpec((d, cs), lambda j, i: (j, 0)),
                  pl.BlockSpec((d, cs), lambda j, i: (j + half, 0))],
        out_specs=[hid, hid, hid], out_shape=[_sds((half * t, cs)), _sds((half * t, cs)), _sds((half * t, cs), BF16)],
        compiler_params=_cparams(dimension_semantics=("parallel", "arbitrary")),
    )(h, w2, w2)
    w_down = down_weight(act)

    def down(a_ref, w_ref, x_ref, y_ref):
        y = x_ref[...]
        for j in range(half):
            y = y + _dot(a_ref[j], w_ref[j])
        y_ref[...] = y

    y = pl.pallas_call(
        down, name="ffn_down", grid=(nm,),
        in_specs=[pl.BlockSpec((half, tm, cs), lambda i: (0, i, 0)), pl.BlockSpec((half, cs, d), lambda i: (0, 0, 0)),
                  pl.BlockSpec((tm, d), lambda i: (i, 0))],
        out_specs=pl.BlockSpec((tm, d), lambda i: (i, 0)), out_shape=_sds((t, d)),
        compiler_params=_cparams(dimension_semantics=("parallel",)),
    )(act.reshape(half, t, cs), w_down.reshape(half, cs, d), x)
    return y, (x, h, g, u, act)


def _ffn_bwd(saved, dy, norm_g, w3, w_down):
    x, h, g, u, act = saved
    t, d = x.shape
    ns, _, cs = w3.shape
    half = ns // 2
    w2 = w3.reshape(ns * d, cs)
    tm = _pick(t, (1024, 512, 256, 128))
    nm = t // tm
    tk = _pick(t, (4096, 2048, 1024, 512, 256, 128))
    nk = t // tk
    d_wdown = _mm_raw("ffn_down_wgrad", act, dy, ta=True, tb=False, out_dtype=GRAD_DTYPE, add=None, grid=(half, 1, nk),
                      out_shape=(half * cs, d), a_block=(tk, cs), a_map=lambda i, j, k: (i * nk + k, 0),
                      b_block=(tk, d), b_map=lambda i, j, k: (k, 0), o_block=(cs, d), o_map=lambda i, j, k: (i, 0))
    def down_dgrad(dy_ref, wd_ref, g_ref, u_ref, dg_ref, du_ref):
        da = _dot_nt(dy_ref[...].astype(BF16), wd_ref[...])
        gv, uv = g_ref[...], u_ref[...]
        s = _sigmoid(gv)
        dg_ref[...] = (da * uv * s * (1.0 + gv * (1.0 - s))).astype(BF16)
        du_ref[...] = (da * gv * s).astype(BF16)

    hid = pl.BlockSpec((tm, cs), lambda j, i: (j * nm + i, 0))
    dg, du = pl.pallas_call(
        down_dgrad, name="ffn_down_dgrad", grid=(half, nm),
        in_specs=[pl.BlockSpec((tm, d), lambda j, i: (i, 0)), pl.BlockSpec((cs, d), lambda j, i: (j, 0)), hid, hid],
        out_specs=[hid, hid], out_shape=[_sds((half * t, cs), BF16), _sds((half * t, cs), BF16)],
        compiler_params=_cparams(dimension_semantics=("parallel", "arbitrary")),
    )(dy, w_down, g, u)

    def wgrad(name, dd):
        return _mm_raw(name, h, dd, ta=True, tb=False, out_dtype=GRAD_DTYPE, add=None, grid=(1, half, nk), out_shape=(half * d, cs),
                       a_block=(tk, d), a_map=lambda i, j, k: (k, 0), b_block=(tk, cs), b_map=lambda i, j, k: (j * nk + k, 0),
                       o_block=(d, cs), o_map=lambda i, j, k: (j, 0))

    def gate_up_dgrad(dg_ref, du_ref, w_ref, dh_ref):
        dh = _dot_nt(dg_ref[0], w_ref[0]) + _dot_nt(du_ref[0], w_ref[half])
        for j in range(1, half):
            dh = dh + _dot_nt(dg_ref[j], w_ref[j]) + _dot_nt(du_ref[j], w_ref[half + j])
        dh_ref[...] = dh

    th = _pick(t, (512, 256, 128))
    hid3 = pl.BlockSpec((half, th, cs), lambda i: (0, i, 0))
    dh = pl.pallas_call(
        gate_up_dgrad, name="ffn_gate_up_dgrad", grid=(t // th,),
        in_specs=[hid3, hid3, pl.BlockSpec((ns, d, cs), lambda i: (0, 0, 0))],
        out_specs=pl.BlockSpec((th, d), lambda i: (i, 0)), out_shape=_sds((t, d)),
        compiler_params=_cparams(dimension_semantics=("parallel",)),
    )(dg.reshape(half, t, cs), du.reshape(half, t, cs), w3)
    d_w3 = jnp.concatenate([wgrad("ffn_gate_wgrad", dg), wgrad("ffn_up_wgrad", du)], axis=0).reshape(ns, d, cs)
    dx, dgain = _rmsnorm_bwd("ffn_norm_bwd", x, norm_g, dh, dy)
    return dx, dgain, d_w3, d_wdown


def _dot_nt(a, b):
    return lax.dot_general(a, b, (((1,), (1,)), ((), ())), preferred_element_type=F32)


def _dot_tn(a, b):
    return lax.dot_general(a, b, (((0,), (0,)), ((), ())), preferred_element_type=F32)


def _dot(a, b):
    return jnp.dot(a, b, preferred_element_type=F32)


CUM_BLOCK = 128


def _tri2(lower):
    r = lax.broadcasted_iota(jnp.int32, (CUM_BLOCK, CUM_BLOCK), 0)
    c = lax.broadcasted_iota(jnp.int32, (CUM_BLOCK, CUM_BLOCK), 1)
    tri = ((r > c) if lower else (r < c)).astype(BF16)
    return jnp.concatenate([tri, tri], axis=0)


def _run_sums(x, tri2, run, reverse):
    nb = x.shape[1] // CUM_BLOCK
    outs = [None] * nb
    for j in (reversed(range(nb)) if reverse else range(nb)):
        xj = x[:, j * CUM_BLOCK:(j + 1) * CUM_BLOCK]
        hi = xj.astype(BF16)
        lo = (xj - hi.astype(F32)).astype(BF16)
        outs[j] = _dot(jnp.concatenate([hi, lo], axis=1), tri2) + run
        run = run + jnp.sum(xj, axis=1, keepdims=True)
    return jnp.concatenate(outs, axis=1), run


def _log_sigmoid(z):
    return jnp.minimum(z, 0.0) - jnp.log(1.0 + jnp.exp(-jnp.abs(z)))


def _heads_in(ref, h, width=HEAD):
    return ref[:, h * width:(h + 1) * width]


def _sb_qk(q, k, gq, gk):
    return _rms(q, gq) * (HEAD ** -0.5), _rms(k, gk)


def _sb_prep_fwd(qkv, gq, gk):
    t = qkv.shape[0]

    def body(x_ref, gq_ref, gk_ref, q_ref, k_ref, v_ref):
        for h in range(N_HEADS):
            q, k = _sb_qk(_heads_in(x_ref, h), _heads_in(x_ref, N_HEADS + h), gq_ref[...], gk_ref[...])
            q_ref[h] = q.astype(BF16)
            k_ref[h] = k.astype(BF16)
            v_ref[h] = _heads_in(x_ref, 2 * N_HEADS + h).astype(BF16)

    hm = _sds((N_HEADS, t, HEAD), BF16)
    return _rows("sb_prep", body, [qkv], [hm, hm, hm], tt=256, consts=[gq, gk])


def _sb_prep_bwd(qkv, gq, gk, dq, dk, dv):
    def body(x_ref, dq_ref, dk_ref, dv_ref, gq_ref, gk_ref, dx_ref, dgq_ref, dgk_ref):
        dgq = jnp.zeros(gq_ref.shape, F32)
        dgk = jnp.zeros(gk_ref.shape, F32)
        for h in range(N_HEADS):
            _, vjp = jax.vjp(_sb_qk, _heads_in(x_ref, h), _heads_in(x_ref, N_HEADS + h), gq_ref[...], gk_ref[...])
            a, b, c, d = vjp((dq_ref[h], dk_ref[h]))
            dx_ref[:, h * HEAD:(h + 1) * HEAD] = a.astype(BF16)
            dx_ref[:, (N_HEADS + h) * HEAD:(N_HEADS + h + 1) * HEAD] = b.astype(BF16)
            dx_ref[:, (2 * N_HEADS + h) * HEAD:(2 * N_HEADS + h + 1) * HEAD] = dv_ref[h].astype(BF16)
            dgq, dgk = dgq + c, dgk + d
        _acc(dgq_ref, dgq)
        _acc(dgk_ref, dgk)

    return _rows("sb_prep_bwd", body, [qkv, dq, dk, dv], [_sds(qkv.shape, BF16)], tt=256, consts=[gq, gk],
                 accs=[_sds(gq.shape), _sds(gk.shape)])


def _q_block(t):
    return min(ATT_Q, t)


def _key_order(bq, qb):
    rows = lax.broadcasted_iota(jnp.int32, (bq, ATT_BLOCK), 0)
    cols = lax.broadcasted_iota(jnp.int32, (bq, ATT_BLOCK), 1)
    return rows - cols + qb * bq


def _sb_attn_fwd(q, k, v):
    nh, t, _ = q.shape
    bq = _q_block(t)
    per = bq // ATT_BLOCK

    def kern(q_ref, k_ref, v_ref, o_ref):
        qb = pl.program_id(1)
        qv = q_ref[0]
        after = _tri2(True)
        order = _key_order(bq, qb)
        nkb = (qb + 1) * per

        def body(i, carry, diagonal):
            o_acc, run = carry
            kb = nkb - 1 - i
            off = pl.multiple_of(kb * ATT_BLOCK, ATT_BLOCK)
            kv = k_ref[0, pl.ds(off, ATT_BLOCK), :]
            vv = v_ref[0, pl.ds(off, ATT_BLOCK), :]
            z = _dot_nt(qv, kv)
            lsz = _log_sigmoid(z)
            lsn = lsz - z
            if diagonal:
                past = order > kb * ATT_BLOCK
                lsn = jnp.where(past, lsn, 0.0)
            la, run = _run_sums(lsn, after, run, True)
            a = jnp.exp(lsz + la)
            if diagonal:
                a = jnp.where(past, a, 0.0)
            o_acc = o_acc + _dot(a.astype(BF16), vv)
            return o_acc, run

        carry = lax.fori_loop(0, per, lambda i, c: body(i, c, True), (jnp.zeros((bq, HEAD), F32), jnp.zeros((bq, 1), F32)))
        o, _ = lax.fori_loop(per, nkb, lambda i, c: body(i, c, False), carry)
        o_ref[...] = o

    return pl.pallas_call(
        kern, name="sb_attn_fwd", grid=(nh, t // bq),
        in_specs=[pl.BlockSpec((1, bq, HEAD), lambda h, i: (h, i, 0)),
                  pl.BlockSpec((1, t, HEAD), lambda h, i: (h, 0, 0)),
                  pl.BlockSpec((1, t, HEAD), lambda h, i: (h, 0, 0))],
        out_specs=pl.BlockSpec((bq, HEAD), lambda h, i: (i, h)),
        out_shape=_sds((t, nh * HEAD)),
        compiler_params=_cparams(dimension_semantics=("parallel", "arbitrary")),
    )(q, k, v)


def _sb_attn_bwd(q, k, v, do):
    nh, t, _ = q.shape
    bq = _q_block(t)
    per = bq // ATT_BLOCK

    def kern(q_ref, k_ref, v_ref, do_ref, dq_ref, dk_ref, dv_ref, g_s, ls_s):
        qb = pl.program_id(1)

        @pl.when(qb == 0)
        def _():
            dk_ref[...] = jnp.zeros(dk_ref.shape, F32)
            dv_ref[...] = jnp.zeros(dv_ref.shape, F32)

        qv = q_ref[0]
        dob = do_ref[...].astype(BF16)
        after, before = _tri2(True), _tri2(False)
        order = _key_order(bq, qb)
        nkb = (qb + 1) * per

        def sweep_left(i, run, diagonal):
            kb = nkb - 1 - i
            off = pl.multiple_of(kb * ATT_BLOCK, ATT_BLOCK)
            kv = k_ref[0, pl.ds(off, ATT_BLOCK), :]
            vv = v_ref[0, pl.ds(off, ATT_BLOCK), :]
            z = _dot_nt(qv, kv)
            lsz = _log_sigmoid(z)
            lsn = lsz - z
            if diagonal:
                past = order > kb * ATT_BLOCK
                lsn = jnp.where(past, lsn, 0.0)
            la, run = _run_sums(lsn, after, run, True)
            a = jnp.exp(lsz + la)
            if diagonal:
                a = jnp.where(past, a, 0.0)
            g_s[kb] = _dot_nt(dob, vv) * a
            ls_s[kb] = lsz
            dv_ref[0, pl.ds(off, ATT_BLOCK), :] += _dot_tn(a.astype(BF16), dob)
            return run

        zero = jnp.zeros((bq, 1), F32)
        run = lax.fori_loop(0, per, lambda i, c: sweep_left(i, c, True), zero)
        lax.fori_loop(per, nkb, lambda i, c: sweep_left(i, c, False), run)

        def sweep_right(kb, carry, diagonal):
            dq_acc, run_g = carry
            off = pl.multiple_of(kb * ATT_BLOCK, ATT_BLOCK)
            kv = k_ref[0, pl.ds(off, ATT_BLOCK), :]
            g = g_s[kb]
            sg = jnp.exp(ls_s[kb])
            dls, run_g = _run_sums(g, before, run_g, False)
            dz = g * (1.0 - sg) - dls * sg
            if diagonal:
                dz = jnp.where(order > kb * ATT_BLOCK, dz, 0.0)
            dzb = dz.astype(BF16)
            dk_ref[0, pl.ds(off, ATT_BLOCK), :] += _dot_tn(dzb, qv)
            return dq_acc + _dot(dzb, kv), run_g

        carry = lax.fori_loop(0, nkb - per, lambda i, c: sweep_right(i, c, False), (jnp.zeros((bq, HEAD), F32), zero))
        dq, _ = lax.fori_loop(nkb - per, nkb, lambda i, c: sweep_right(i, c, True), carry)
        dq_ref[0] = dq

    hm = _sds((nh, t, HEAD))
    full = pl.BlockSpec((1, t, HEAD), lambda h, i: (h, 0, 0))
    tok = pl.BlockSpec((bq, HEAD), lambda h, i: (i, h))
    nkb_max = t // ATT_BLOCK
    return pl.pallas_call(
        kern, name="sb_attn_bwd", grid=(nh, t // bq),
        in_specs=[pl.BlockSpec((1, bq, HEAD), lambda h, i: (h, i, 0)), full, full, tok],
        out_specs=[pl.BlockSpec((1, bq, HEAD), lambda h, i: (h, i, 0)), full, full],
        out_shape=[hm, hm, hm],
        scratch_shapes=[pltpu.VMEM((nkb_max, bq, ATT_BLOCK), F32), pltpu.VMEM((nkb_max, bq, ATT_BLOCK), F32)],
        compiler_params=_cparams(dimension_semantics=("parallel", "arbitrary")),
    )(q, k, v, do)


def _sb_fwd(x, h, w_qkv, gq, gk, w_out):
    qkv = _mm("sb_qkv", h, w_qkv)
    q, k, v = _sb_prep_fwd(qkv, gq, gk)
    o = _sb_attn_fwd(q, k, v)
    y = _mm("sb_out", o, w_out, add=x)
    return y, (h, qkv, q, k, v, o)


def _sb_bwd(saved, dy, w_qkv, gq, gk, w_out):
    h, qkv, q, k, v, o = saved
    d_wout = _mm("sb_out_wgrad", o, dy, ta=True, out_dtype=GRAD_DTYPE)
    do = _mm("sb_out_dgrad", dy, w_out, tb=True)
    dq, dk, dv = _sb_attn_bwd(q, k, v, do)
    dqkv, dgq, dgk = _sb_prep_bwd(qkv, gq, gk, dq, dk, dv)
    d_wqkv = _mm("sb_qkv_wgrad", h, dqkv, ta=True, out_dtype=GRAD_DTYPE)
    dh = _mm("sb_qkv_dgrad", dqkv, w_qkv, tb=True)
    return dh, d_wqkv, dgq, dgk, d_wout


DN_QKV = 3 * N_HEADS * HEAD
DN_PROJ = DN_QKV + N_HEADS * HEAD + LANES
DN_CONV = 4
HALO = 8
CONV_COLS = 512


def _dn_conv_fwd(proj, conv_w, tt=256):
    t = proj.shape[0]
    tt = min(tt, t)

    def body(u_ref, prev_ref, w_ref, c_ref):
        i = pl.program_id(0)
        for cc in range(DN_QKV // CONV_COLS):
            cs = slice(cc * CONV_COLS, (cc + 1) * CONV_COLS)
            cur = u_ref[:, cs]
            prev = jnp.where(i > 0, prev_ref[:, cs], 0.0)
            ext = jnp.concatenate([prev, cur], axis=0)
            y = cur * w_ref[DN_CONV - 1:DN_CONV, cs]
            for j in range(DN_CONV - 1):
                y = y + pltpu.roll(ext, DN_CONV - 1 - j, 0)[HALO:] * w_ref[j:j + 1, cs]
            c_ref[:, cs] = y

    return _rows("dn_conv", body,
                 [(proj, (tt, DN_QKV), lambda i: (i, 0)),
                  (proj, (HALO, DN_QKV), lambda i: (jnp.maximum(i * (tt // HALO) - 1, 0), 0))],
                 [_sds((t, DN_QKV))], tt=tt, consts=[conv_w])[0]


def _dn_conv_bwd(proj, conv_w, dc, dz, dab, tt=256):
    t = proj.shape[0]
    tt = min(tt, t)
    nblk = t // tt

    def body(u_ref, prev_ref, dc_ref, next_ref, dz_ref, dab_ref, w_ref, dp_ref, dw_ref):
        i = pl.program_id(0)
        dws = []
        for cc in range(DN_QKV // CONV_COLS):
            cs = slice(cc * CONV_COLS, (cc + 1) * CONV_COLS)
            cur = u_ref[:, cs]
            prev = jnp.where(i > 0, prev_ref[:, cs], 0.0)
            ext_u = jnp.concatenate([prev, cur], axis=0)
            d = dc_ref[:, cs]
            nxt = jnp.where(i < nblk - 1, next_ref[:, cs], 0.0)
            ext_d = jnp.concatenate([d, nxt], axis=0)
            du = d * w_ref[DN_CONV - 1:DN_CONV, cs]
            rows = [jnp.sum(d * cur, axis=0, keepdims=True)]
            for j in range(DN_CONV - 2, -1, -1):
                sh = DN_CONV - 1 - j
                du = du + pltpu.roll(ext_d, tt + HALO - sh, 0)[:tt] * w_ref[j:j + 1, cs]
                rows.insert(0, jnp.sum(d * pltpu.roll(ext_u, sh, 0)[HALO:], axis=0, keepdims=True))
            dp_ref[:, cs] = du.astype(BF16)
            dws.append(jnp.concatenate(rows, axis=0))
        dp_ref[:, DN_QKV:DN_QKV + N_HEADS * HEAD] = dz_ref[...].astype(BF16)
        dp_ref[:, DN_QKV + N_HEADS * HEAD:] = dab_ref[...].astype(BF16)
        _acc(dw_ref, jnp.concatenate(dws, axis=1))

    return _rows("dn_conv_bwd", body,
                 [(proj, (tt, DN_QKV), lambda i: (i, 0)),
                  (proj, (HALO, DN_QKV), lambda i: (jnp.maximum(i * (tt // HALO) - 1, 0), 0)),
                  dc,
                  (dc, (HALO, DN_QKV), lambda i: (jnp.minimum((i + 1) * (tt // HALO), t // HALO - 1), 0)),
                  dz, dab],
                 [_sds((t, DN_PROJ), BF16)], tt=tt, consts=[conv_w], accs=[_sds(conv_w.shape)])


def _l2n(x):
    return x * lax.rsqrt(jnp.sum(x * x, axis=-1, keepdims=True) + NORM_EPS)


def _dn_qkv(cq, ck, cv):
    return _l2n(_silu(cq)) * (HEAD ** -0.5), _l2n(_silu(ck)), _silu(cv)


def _dn_gates(ab, a_log, dt_bias):
    lane = lax.broadcasted_iota(jnp.int32, ab.shape, 1)
    g = -jnp.exp(a_log) * _softplus(ab + dt_bias)
    return jnp.where(lane < N_HEADS, g, jnp.where(lane < 2 * N_HEADS, _sigmoid(ab), 0.0))


def _ab_spec(tt):
    return (tt, LANES), lambda i: (i, DN_PROJ // LANES - 1)


def _dn_prep_fwd(c, proj, a_log, dt_bias, tt=256):
    t = c.shape[0]
    tt = min(tt, t)

    def body(c_ref, ab_ref, al_ref, dt_ref, q_ref, k_ref, v_ref, g_ref):
        for h in range(N_HEADS):
            q_ref[h], k_ref[h], v_ref[h] = _dn_qkv(_heads_in(c_ref, h), _heads_in(c_ref, N_HEADS + h), _heads_in(c_ref, 2 * N_HEADS + h))
        g_ref[...] = _dn_gates(ab_ref[...], al_ref[...], dt_ref[...])

    hm = _sds((N_HEADS, t, HEAD))
    return _rows("dn_prep", body, [c, (proj,) + _ab_spec(tt)], [hm, hm, hm, _sds((t, LANES))], tt=tt, consts=[a_log, dt_bias])


def _dn_prep_bwd(c, proj, a_log, dt_bias, dq, dk, dv, dgates, tt=256):
    t = c.shape[0]
    tt = min(tt, t)

    def body(c_ref, ab_ref, dq_ref, dk_ref, dv_ref, dg_ref, al_ref, dt_ref, dc_ref, dab_ref, dal_ref, ddt_ref):
        for h in range(N_HEADS):
            _, vjp = jax.vjp(_dn_qkv, _heads_in(c_ref, h), _heads_in(c_ref, N_HEADS + h), _heads_in(c_ref, 2 * N_HEADS + h))
            a, b, d = vjp((dq_ref[h], dk_ref[h], dv_ref[h]))
            dc_ref[:, h * HEAD:(h + 1) * HEAD] = a
            dc_ref[:, (N_HEADS + h) * HEAD:(N_HEADS + h + 1) * HEAD] = b
            dc_ref[:, (2 * N_HEADS + h) * HEAD:(2 * N_HEADS + h + 1) * HEAD] = d
        _, vjp = jax.vjp(_dn_gates, ab_ref[...], al_ref[...], dt_ref[...])
        dab, dal, ddt = vjp(dg_ref[...])
        dab_ref[...] = dab
        _acc(dal_ref, dal)
        _acc(ddt_ref, ddt)

    return _rows("dn_prep_bwd", body, [c, (proj,) + _ab_spec(tt), dq, dk, dv, dgates], [_sds(c.shape), _sds((t, LANES))],
                 tt=tt, consts=[a_log, dt_bias], accs=[_sds(a_log.shape), _sds(dt_bias.shape)])


def _bdot(a, b, prec=None):
    return lax.dot_general(a, b, (((2,), (1,)), ((0,), (0,))), precision=prec, preferred_element_type=F32)


def _bdot_nt(a, b, prec=None):
    return lax.dot_general(a, b, (((2,), (2,)), ((0,), (0,))), precision=prec, preferred_element_type=F32)


def _bdot_tn(a, b, prec=None):
    return lax.dot_general(a, b, (((1,), (1,)), ((0,), (0,))), precision=prec, preferred_element_type=F32)


def _inv_raw(low):
    c = low.shape[-1]
    r = lax.broadcasted_iota(jnp.int32, (c, c), 0)
    s = lax.broadcasted_iota(jnp.int32, (c, c), 1)
    m = jnp.where(r == s, 1.0, 0.0) - low
    p = _bdot(low, low, F32X3)
    n_fac = int(math.log2(c)) - 1
    for i in range(n_fac):
        m = m + _bdot(m, p, F32X3)
        if i < n_fac - 1:
            p = _bdot(p, p, F32X3)
    return m


@jax.custom_vjp
def _inv_unit_lower(low):
    return _inv_raw(low)


def _inv_fwd(low):
    m = _inv_raw(low)
    return m, m


def _inv_bwd(m, dm):
    return (-_bdot_nt(_bdot_tn(m, dm, F32X3), m, F32X3),)


_inv_unit_lower.defvjp(_inv_fwd, _inv_bwd)


def _dn_chunk(q, k, v, gates, s):
    nh, rows, _ = q.shape
    c = DN_CHUNK
    nc = rows // c
    nb = nh * nc
    lane = lax.broadcasted_iota(jnp.int32, gates.shape, 1)

    def column(j):
        return jnp.sum(jnp.where(lane == j, gates, 0.0), axis=1, keepdims=True)[None]

    def fold(x):
        return x.reshape((nb, c) + x.shape[2:])

    g_col = fold(jnp.concatenate([column(h) for h in range(nh)], axis=0))
    b_col = fold(jnp.concatenate([column(h + nh) for h in range(nh)], axis=0))
    q, k, v = fold(q), fold(k), fold(v)
    r = lax.broadcasted_iota(jnp.int32, (c, c), 0)
    cc = lax.broadcasted_iota(jnp.int32, (c, c), 1)
    causal, strict = r >= cc, r > cc
    incl = jnp.broadcast_to(jnp.where(causal, 1.0, 0.0), (nb, c, c))
    upper = jnp.broadcast_to(jnp.where(r <= cc, 1.0, 0.0), (nb, c, c))
    gb = jnp.broadcast_to(g_col, (nb, c, LANES))
    gbc = jnp.broadcast_to(g_col, (nb, c, c))
    gc = _bdot(incl, gb, F32X3)
    gc_r = _bdot(incl, gbc, F32X3)
    gc_c = _bdot_tn(gbc, upper, F32X3)
    decay = jnp.where(causal, jnp.exp(jnp.where(causal, gc_r - gc_c, 0.0)), 0.0)
    kb = k * b_col
    low = jnp.where(strict, _bdot_nt(kb, k) * decay, 0.0)
    m = _inv_unit_lower(low)
    egc = jnp.exp(gc)
    gl = jnp.sum(gb, axis=1, keepdims=True)
    local = (_bdot(m, v * b_col, F32X3), _bdot(m, kb * egc, F32X3), _bdot_nt(q, k) * decay, q * egc,
             k * jnp.exp(gl - gc), jnp.exp(gl))
    outs = []
    for i in range(nc):
        u, w, attn, q_dec, k_dec, cd = (x.reshape((nh, nc) + x.shape[1:])[:, i] for x in local)
        v_new = u - _bdot(w, s)
        outs.append(_bdot(q_dec, s) + _bdot(attn, v_new))
        s = s * cd + _bdot_tn(k_dec, v_new)
    return jnp.concatenate(outs, axis=1), s


def _dn_chunks_fwd(q, k, v, gates):
    nh, t, _ = q.shape
    n = t // DN_STEP

    def kern(q_ref, k_ref, v_ref, g_ref, o_ref, sin_ref, s_scr):
        @pl.when(pl.program_id(0) == 0)
        def _():
            s_scr[...] = jnp.zeros(s_scr.shape, F32)

        s = s_scr[...]
        sin_ref[0] = s
        o_ref[...], s_scr[...] = _dn_chunk(q_ref[...], k_ref[...], v_ref[...], g_ref[...], s)

    blk = pl.BlockSpec((nh, DN_STEP, HEAD), lambda i: (0, i, 0))
    return pl.pallas_call(
        kern, name="dn_chunks_fwd", grid=(n,),
        in_specs=[blk, blk, blk, pl.BlockSpec((DN_STEP, LANES), lambda i: (i, 0))],
        out_specs=[blk, pl.BlockSpec((1, nh, HEAD, HEAD), lambda i: (i, 0, 0, 0))],
        out_shape=[_sds((nh, t, HEAD)), _sds((n, nh, HEAD, HEAD))],
        scratch_shapes=[pltpu.VMEM((nh, HEAD, HEAD), F32)],
        compiler_params=_cparams(dimension_semantics=("arbitrary",)),
    )(q, k, v, gates)


def _dn_chunks_bwd(q, k, v, gates, s_in, do):
    nh, t, _ = q.shape
    n = t // DN_STEP

    def kern(q_ref, k_ref, v_ref, g_ref, sin_ref, do_ref, dq_ref, dk_ref, dv_ref, dg_ref, ds_scr):
        @pl.when(pl.program_id(0) == 0)
        def _():
            ds_scr[...] = jnp.zeros(ds_scr.shape, F32)

        _, vjp = jax.vjp(_dn_chunk, q_ref[...], k_ref[...], v_ref[...], g_ref[...], sin_ref[0])
        dq_ref[...], dk_ref[...], dv_ref[...], dg_ref[...], ds_scr[...] = vjp((do_ref[...], ds_scr[...]))

    blk = pl.BlockSpec((nh, DN_STEP, HEAD), lambda i: (0, n - 1 - i, 0))
    gblk = pl.BlockSpec((DN_STEP, LANES), lambda i: (n - 1 - i, 0))
    hm = _sds((nh, t, HEAD))
    return pl.pallas_call(
        kern, name="dn_chunks_bwd", grid=(n,),
        in_specs=[blk, blk, blk, gblk, pl.BlockSpec((1, nh, HEAD, HEAD), lambda i: (n - 1 - i, 0, 0, 0)), blk],
        out_specs=[blk, blk, blk, gblk],
        out_shape=[hm, hm, hm, _sds((t, LANES))],
        scratch_shapes=[pltpu.VMEM((nh, HEAD, HEAD), F32)],
        compiler_params=_cparams(dimension_semantics=("arbitrary",)),
    )(q, k, v, gates, s_in, do)


def _dn_gate_out(o, z, g):
    return _rms(o, g) * _silu(z)


def _z_spec(tt):
    return (tt, N_HEADS * HEAD), lambda i: (i, DN_QKV // (N_HEADS * HEAD))


def _dn_post_fwd(o, proj, out_norm, tt=256):
    t = o.shape[1]
    tt = min(tt, t)

    def body(o_ref, z_ref, g_ref, y_ref):
        for h in range(N_HEADS):
            y_ref[:, h * HEAD:(h + 1) * HEAD] = _dn_gate_out(o_ref[h], _heads_in(z_ref, h), g_ref[...]).astype(BF16)

    return _rows("dn_post", body, [o, (proj,) + _z_spec(tt)], [_sds((t, N_HEADS * HEAD), BF16)], tt=tt, consts=[out_norm])[0]


def _dn_post_bwd(o, proj, out_norm, dy, tt=256):
    t = o.shape[1]
    tt = min(tt, t)

    def body(o_ref, z_ref, dy_ref, g_ref, do_ref, dz_ref, dg_ref):
        dg = jnp.zeros(g_ref.shape, F32)
        for h in range(N_HEADS):
            _, vjp = jax.vjp(_dn_gate_out, o_ref[h], _heads_in(z_ref, h), g_ref[...])
            a, b, d = vjp(_heads_in(dy_ref, h))
            do_ref[h] = a
            dz_ref[:, h * HEAD:(h + 1) * HEAD] = b
            dg = dg + d
        _acc(dg_ref, dg)

    return _rows("dn_post_bwd", body, [o, (proj,) + _z_spec(tt), dy], [_sds(o.shape), _sds((t, N_HEADS * HEAD))], tt=tt,
                 consts=[out_norm], accs=[_sds(out_norm.shape)])


def _dn_fwd(x, h, w_in, conv_w, a_log, dt_bias, out_norm, w_out):
    proj = _mm("dn_in", h, w_in)
    c = _dn_conv_fwd(proj, conv_w)
    q, k, v, gates = _dn_prep_fwd(c, proj, a_log, dt_bias)
    o, s_in = _dn_chunks_fwd(q, k, v, gates)
    on = _dn_post_fwd(o, proj, out_norm)
    y = _mm("dn_out", on, w_out, add=x)
    return y, (h, proj, c, q, k, v, gates, o, s_in, on)


def _dn_bwd(saved, dy, w_in, conv_w, a_log, dt_bias, out_norm, w_out):
    h, proj, c, q, k, v, gates, o, s_in, on = saved
    d_wout = _mm("dn_out_wgrad", on, dy, ta=True, out_dtype=GRAD_DTYPE)
    don = _mm("dn_out_dgrad", dy, w_out, tb=True)
    do, dz, d_out_norm = _dn_post_bwd(o, proj, out_norm, don)
    dq, dk, dv, dgates = _dn_chunks_bwd(q, k, v, gates, s_in, do)
    dc, dab, d_a_log, d_dt_bias = _dn_prep_bwd(c, proj, a_log, dt_bias, dq, dk, dv, dgates)
    dproj, d_conv_w = _dn_conv_bwd(proj, conv_w, dc, dz, dab)
    d_win = _mm("dn_in_wgrad", h, dproj, ta=True, out_dtype=GRAD_DTYPE)
    dh = _mm("dn_in_dgrad", dproj, w_in, tb=True)
    return dh, d_win, d_conv_w, d_a_log, d_dt_bias, d_out_norm, d_wout


MLA_SCALE = MLA_QK ** -0.5
MLA_C = 512


def _swap_raw(x):
    lane = lax.broadcasted_iota(jnp.int32, x.shape, 1)
    half = MLA_ROPE // 2
    y = jnp.where(lane < half, pltpu.roll(x, LANES - half, 1), pltpu.roll(x, half, 1))
    return jnp.where(lane < MLA_ROPE, y, 0.0)


@jax.custom_vjp
def _swap_halves(x):
    return _swap_raw(x)


_swap_halves.defvjp(lambda x: (_swap_raw(x), None), lambda _, d: (_swap_raw(d),))


def _rms_rope(x, g, cos, sin):
    y = x * lax.rsqrt(jnp.sum(x * x, axis=-1, keepdims=True) * (1.0 / MLA_ROPE) + NORM_EPS) * g
    return y * cos + _swap_halves(y) * sin


def _mla_latent(cq, ckv, kr, gq, gkv, gkr, cos, sin):
    return _rms(cq, gq), _rms(ckv, gkv), _rms_rope(kr, gkr, cos, sin)


def _mla_prep1_fwd(c, gq, gkv, gkr, cos, sin):
    t = c.shape[0]

    def body(c_ref, cos_ref, sin_ref, gq_ref, gkv_ref, gkr_ref, cq_ref, ckv_ref, kr_ref):
        a, b, r = _mla_latent(c_ref[:, :256], c_ref[:, 256:384], c_ref[:, 384:], gq_ref[...], gkv_ref[...], gkr_ref[...],
                              cos_ref[...], sin_ref[...])
        cq_ref[...] = a.astype(BF16)
        ckv_ref[...] = b.astype(BF16)
        kr_ref[...] = r.astype(BF16)

    return _rows("mla_prep1", body, [c, cos, sin], [_sds((t, 256), BF16), _sds((t, HEAD), BF16), _sds((t, HEAD), BF16)],
                 tt=512, consts=[gq, gkv, gkr])


def _mla_prep1_bwd(c, gq, gkv, gkr, cos, sin, dcq, dckv, dkr_heads):
    def body(c_ref, cos_ref, sin_ref, dcq_ref, dckv_ref, dkr_ref, gq_ref, gkv_ref, gkr_ref, dc_ref, dgq_ref, dgkv_ref, dgkr_ref):
        dkr = dkr_ref[0]
        for h in range(1, N_HEADS):
            dkr = dkr + dkr_ref[h]
        _, vjp = jax.vjp(_mla_latent, c_ref[:, :256], c_ref[:, 256:384], c_ref[:, 384:], gq_ref[...], gkv_ref[...], gkr_ref[...],
                         cos_ref[...], sin_ref[...])
        a, b, r, d1, d2, d3, _, _ = vjp((dcq_ref[...], dckv_ref[...], dkr))
        dc_ref[:, :256] = a.astype(BF16)
        dc_ref[:, 256:384] = b.astype(BF16)
        dc_ref[:, 384:] = r.astype(BF16)
        _acc(dgq_ref, d1)
        _acc(dgkv_ref, d2)
        _acc(dgkr_ref, d3)

    return _rows("mla_prep1_bwd", body, [c, cos, sin, dcq, dckv, dkr_heads], [_sds(c.shape, BF16)], tt=512,
                 consts=[gq, gkv, gkr], accs=[_sds(gq.shape), _sds(gkv.shape), _sds(gkr.shape)])


def _mla_heads(qn, qr, kn, gqn, gqr, gkn, cos, sin):
    return _rms(qn, gqn) * MLA_SCALE, _rms_rope(qr, gqr, cos, sin) * MLA_SCALE, _rms(kn, gkn)


def _mla_prep2_fwd(qa, kv, gqn, gqr, gkn, cos, sin):
    t = qa.shape[0]

    def body(qa_ref, kv_ref, cos_ref, sin_ref, gqn_ref, gqr_ref, gkn_ref, qn_ref, qr_ref, kn_ref, v_ref):
        for h in range(N_HEADS):
            a, b, c = _mla_heads(_heads_in(qa_ref, h), _heads_in(qa_ref, N_HEADS + h), _heads_in(kv_ref, h),
                                 gqn_ref[...], gqr_ref[...], gkn_ref[...], cos_ref[...], sin_ref[...])
            qn_ref[h] = a.astype(BF16)
            qr_ref[h] = b.astype(BF16)
            kn_ref[h] = c.astype(BF16)
            v_ref[h] = _heads_in(kv_ref, N_HEADS + h).astype(BF16)

    hm = _sds((N_HEADS, t, HEAD), BF16)
    return _rows("mla_prep2", body, [qa, kv, cos, sin], [hm, hm, hm, hm], tt=256, consts=[gqn, gqr, gkn])


def _mla_prep2_bwd(qa, kv, gqn, gqr, gkn, cos, sin, dqn, dqr, dkn, dv):
    def body(qa_ref, kv_ref, cos_ref, sin_ref, dqn_ref, dqr_ref, dkn_ref, dv_ref, gqn_ref, gqr_ref, gkn_ref,
             dqa_ref, dkv_ref, d1_ref, d2_ref, d3_ref):
        d1 = jnp.zeros(gqn_ref.shape, F32)
        d2 = jnp.zeros(gqr_ref.shape, F32)
        d3 = jnp.zeros(gkn_ref.shape, F32)
        for h in range(N_HEADS):
            _, vjp = jax.vjp(_mla_heads, _heads_in(qa_ref, h), _heads_in(qa_ref, N_HEADS + h), _heads_in(kv_ref, h),
                             gqn_ref[...], gqr_ref[...], gkn_ref[...], cos_ref[...], sin_ref[...])
            a, b, c, e1, e2, e3, _, _ = vjp((dqn_ref[h], dqr_ref[h], dkn_ref[h]))
            dqa_ref[:, h * HEAD:(h + 1) * HEAD] = a.astype(BF16)
            dqa_ref[:, (N_HEADS + h) * HEAD:(N_HEADS + h + 1) * HEAD] = b.astype(BF16)
            dkv_ref[:, h * HEAD:(h + 1) * HEAD] = c.astype(BF16)
            dkv_ref[:, (N_HEADS + h) * HEAD:(N_HEADS + h + 1) * HEAD] = dv_ref[h].astype(BF16)
            d1, d2, d3 = d1 + e1, d2 + e2, d3 + e3
        _acc(d1_ref, d1)
        _acc(d2_ref, d2)
        _acc(d3_ref, d3)

    return _rows("mla_prep2_bwd", body, [qa, kv, cos, sin, dqn, dqr, dkn, dv], [_sds(qa.shape, BF16), _sds(kv.shape, BF16)],
                 tt=256, consts=[gqn, gqr, gkn], accs=[_sds(gqn.shape), _sds(gqr.shape), _sds(gkn.shape)])


def _mla_attn_fwd(qn, qr, kn, kr, v):
    nh, t, _ = qn.shape
    bq = _q_block(t)
    per = bq // ATT_BLOCK

    def kern(qn_ref, qr_ref, kn_ref, kr_ref, v_ref, o_ref, lse_ref):
        qb = pl.program_id(1)
        qv = jnp.concatenate([qn_ref[0], qr_ref[0]], axis=1)
        order = _key_order(bq, qb)

        def body(kb, carry, diagonal):
            acc, m, l = carry
            off = pl.multiple_of(kb * ATT_BLOCK, ATT_BLOCK)
            kv = jnp.concatenate([kn_ref[0, pl.ds(off, ATT_BLOCK), :], kr_ref[pl.ds(off, ATT_BLOCK), :]], axis=1)
            s = _dot_nt(qv, kv)
            if diagonal:
                s = jnp.where(order >= kb * ATT_BLOCK, s, -jnp.inf)
            m_new = jnp.maximum(m, jnp.max(s, axis=1, keepdims=True))
            alpha = jnp.exp(m - m_new)
            p = jnp.exp(s - m_new)
            acc = acc * alpha + _dot(p.astype(BF16), v_ref[0, pl.ds(off, ATT_BLOCK), :])
            return acc, m_new, l * alpha + jnp.sum(p, axis=1, keepdims=True)

        init = (jnp.zeros((bq, HEAD), F32), jnp.full((bq, 1), -jnp.inf, F32), jnp.zeros((bq, 1), F32))
        carry = lax.fori_loop(0, qb * per, lambda i, c: body(i, c, False), init)
        acc, m, l = lax.fori_loop(qb * per, (qb + 1) * per, lambda i, c: body(i, c, True), carry)
        o_ref[...] = acc / l
        lse_ref[...] = jnp.broadcast_to(m + jnp.log(l), (bq, HEAD))

    blk = pl.BlockSpec((1, bq, HEAD), lambda h, i: (h, i, 0))
    full = pl.BlockSpec((1, t, HEAD), lambda h, i: (h, 0, 0))
    tok = pl.BlockSpec((bq, HEAD), lambda h, i: (i, h))
    return pl.pallas_call(
        kern, name="mla_attn_fwd", grid=(nh, t // bq),
        in_specs=[blk, blk, full, pl.BlockSpec((t, HEAD), lambda h, i: (0, 0)), full],
        out_specs=[tok, tok], out_shape=[_sds((t, nh * HEAD)), _sds((t, nh * HEAD))],
        compiler_params=_cparams(dimension_semantics=("parallel", "arbitrary")),
    )(qn, qr, kn, kr, v)


def _mla_attn_bwd(qn, qr, kn, kr, v, o, lse, do):
    nh, t, _ = qn.shape
    bq = _q_block(t)
    per = bq // ATT_BLOCK

    def kern(qn_ref, qr_ref, kn_ref, kr_ref, v_ref, o_ref, lse_ref, do_ref, dqn_ref, dqr_ref, dkn_ref, dkr_ref, dv_ref):
        qb = pl.program_id(1)

        @pl.when(qb == 0)
        def _():
            dkn_ref[...] = jnp.zeros(dkn_ref.shape, F32)
            dkr_ref[...] = jnp.zeros(dkr_ref.shape, F32)
            dv_ref[...] = jnp.zeros(dv_ref.shape, F32)

        qv = jnp.concatenate([qn_ref[0], qr_ref[0]], axis=1)
        dov = do_ref[...]
        dob = dov.astype(BF16)
        delta = jnp.sum(dov * o_ref[...], axis=1, keepdims=True)
        lse_col = lse_ref[:, :1]
        order = _key_order(bq, qb)

        def body(kb, dq, diagonal):
            off = pl.multiple_of(kb * ATT_BLOCK, ATT_BLOCK)
            kv = jnp.concatenate([kn_ref[0, pl.ds(off, ATT_BLOCK), :], kr_ref[pl.ds(off, ATT_BLOCK), :]], axis=1)
            vv = v_ref[0, pl.ds(off, ATT_BLOCK), :]
            p = jnp.exp(_dot_nt(qv, kv) - lse_col)
            if diagonal:
                p = jnp.where(order >= kb * ATT_BLOCK, p, 0.0)
            ds = (p * (_dot_nt(dob, vv) - delta)).astype(BF16)
            dk = _dot_tn(ds, qv)
            dkn_ref[0, pl.ds(off, ATT_BLOCK), :] += dk[:, :HEAD]
            dkr_ref[0, pl.ds(off, ATT_BLOCK), :] += dk[:, HEAD:]
            dv_ref[0, pl.ds(off, ATT_BLOCK), :] += _dot_tn(p.astype(BF16), dob)
            return dq + _dot(ds, kv)

        dq = lax.fori_loop(0, qb * per, lambda i, c: body(i, c, False), jnp.zeros((bq, 2 * HEAD), F32))
        dq = lax.fori_loop(qb * per, (qb + 1) * per, lambda i, c: body(i, c, True), dq)
        dqn_ref[0] = dq[:, :HEAD]
        dqr_ref[0] = dq[:, HEAD:]

    hm = _sds((nh, t, HEAD))
    blk = pl.BlockSpec((1, bq, HEAD), lambda h, i: (h, i, 0))
    full = pl.BlockSpec((1, t, HEAD), lambda h, i: (h, 0, 0))
    tok = pl.BlockSpec((bq, HEAD), lambda h, i: (i, h))
    return pl.pallas_call(
        kern, name="mla_attn_bwd", grid=(nh, t // bq),
        in_specs=[blk, blk, full, pl.BlockSpec((t, HEAD), lambda h, i: (0, 0)), full, tok, tok, tok],
        out_specs=[blk, blk, full, full, full], out_shape=[hm, hm, hm, hm, hm],
        compiler_params=_cparams(dimension_semantics=("parallel", "arbitrary")),
    )(qn, qr, kn, kr, v, o, lse, do)


def _rope_tables(t):
    inv_freq = ROPE_THETA ** (-jnp.arange(0, MLA_ROPE, 2, dtype=F32) / MLA_ROPE)
    ang = jnp.arange(t, dtype=F32)[:, None] * inv_freq[None, :]
    c, s = jnp.cos(ang), jnp.sin(ang)
    pad = ((0, 0), (0, LANES - MLA_ROPE))
    return jnp.pad(jnp.concatenate([c, c], axis=1), pad), jnp.pad(jnp.concatenate([-s, s], axis=1), pad)


def _pad_lanes(v, n=LANES):
    return jnp.pad(v, (0, n - v.shape[0])).reshape(1, n)


def _mla_layout(w_down, w_uq, w_ukv):
    w_down_p = jnp.pad(w_down, ((0, 0), (0, MLA_C - w_down.shape[1])))
    uq = w_uq.reshape(w_uq.shape[0], N_HEADS, MLA_QK)
    rope = jnp.pad(uq[:, :, HEAD:], ((0, 0), (0, 0), (0, LANES - MLA_ROPE)))
    w_uq_p = jnp.concatenate([uq[:, :, :HEAD].reshape(-1, N_HEADS * HEAD), rope.reshape(-1, N_HEADS * LANES)], axis=1)
    ukv = w_ukv.reshape(w_ukv.shape[0], N_HEADS, 2 * HEAD)
    w_ukv_p = jnp.concatenate([ukv[:, :, :HEAD].reshape(-1, N_HEADS * HEAD), ukv[:, :, HEAD:].reshape(-1, N_HEADS * HEAD)], axis=1)
    return w_down_p, w_uq_p, w_ukv_p


def _mla_unlayout(d_down_p, d_uq_p, d_ukv_p):
    d_down = d_down_p[:, :256 + HEAD + MLA_ROPE]
    nope = d_uq_p[:, :N_HEADS * HEAD].reshape(-1, N_HEADS, HEAD)
    rope = d_uq_p[:, N_HEADS * HEAD:].reshape(-1, N_HEADS, LANES)[:, :, :MLA_ROPE]
    d_uq = jnp.concatenate([nope, rope], axis=2).reshape(-1, N_HEADS * MLA_QK)
    kn = d_ukv_p[:, :N_HEADS * HEAD].reshape(-1, N_HEADS, HEAD)
    vv = d_ukv_p[:, N_HEADS * HEAD:].reshape(-1, N_HEADS, HEAD)
    d_ukv = jnp.concatenate([kn, vv], axis=2).reshape(-1, N_HEADS * 2 * HEAD)
    return d_down, d_uq, d_ukv


def _mla_weight_shapes():
    return (_sds((1024, MLA_C), BF16), _sds((1, 256)), _sds((1, HEAD)), _sds((256, 2048), BF16), _sds((HEAD, 2048), BF16),
            _sds((1, HEAD)), _sds((1, HEAD)), _sds((1, HEAD)), _sds((1, HEAD)), _sds((1024, 1024), BF16),
            _sds((4096, HEAD)), _sds((4096, HEAD)))


def _mla_fwd(x, h, w_down, gq, gkv, w_uq, w_ukv, gqn, gqr, gkn, gkr, w_out, cos, sin):
    c = _mm("mla_down", h, w_down)
    cq, ckv, kr = _mla_prep1_fwd(c, gq, gkv, gkr, cos, sin)
    qa = _mm("mla_uq", cq, w_uq)
    kv = _mm("mla_ukv", ckv, w_ukv)
    qn, qr, kn, v = _mla_prep2_fwd(qa, kv, gqn, gqr, gkn, cos, sin)
    o, lse = _mla_attn_fwd(qn, qr, kn, kr, v)
    y = _mm("mla_out", o, w_out, add=x)
    return y, (h, c, cq, ckv, kr, qa, kv, qn, qr, kn, v, o, lse)


def _mla_bwd(saved, dy, w_down, gq, gkv, w_uq, w_ukv, gqn, gqr, gkn, gkr, w_out, cos, sin):
    h, c, cq, ckv, kr, qa, kv, qn, qr, kn, v, o, lse = saved
    d_wout = _mm("mla_out_wgrad", o, dy, ta=True, out_dtype=GRAD_DTYPE)
    do = _mm("mla_out_dgrad", dy, w_out, tb=True)
    dqn, dqr, dkn, dkr, dv = _mla_attn_bwd(qn, qr, kn, kr, v, o, lse, do)
    dqa, dkv, dgqn, dgqr, dgkn = _mla_prep2_bwd(qa, kv, gqn, gqr, gkn, cos, sin, dqn, dqr, dkn, dv)
    d_wuq = _mm("mla_uq_wgrad", cq, dqa, ta=True, out_dtype=GRAD_DTYPE)
    d_wukv = _mm("mla_ukv_wgrad", ckv, dkv, ta=True, out_dtype=GRAD_DTYPE)
    dcq = _mm("mla_uq_dgrad", dqa, w_uq, tb=True)
    dckv = _mm("mla_ukv_dgrad", dkv, w_ukv, tb=True)
    dc, dgq, dgkv, dgkr = _mla_prep1_bwd(c, gq, gkv, gkr, cos, sin, dcq, dckv, dkr)
    d_wdown = _mm("mla_down_wgrad", h, dc, ta=True, out_dtype=GRAD_DTYPE)
    dh = _mm("mla_down_dgrad", dc, w_down, tb=True)
    return dh, d_wdown, dgq, dgkv, d_wuq, d_wukv, dgqn, dgqr, dgkn, dgkr, d_wout


def _loss_head(y, target):
    d = y.shape[1]

    def body(y_ref, t_ref, dy_ref, l_ref):
        err = y_ref[...] - t_ref[...]
        dy_ref[...] = err * (1.0 / d)
        part = 0.5 * jnp.sum(jnp.sum(err * err, axis=1, keepdims=True) * (1.0 / d), axis=0, keepdims=True)
        _acc(l_ref, jnp.broadcast_to(part, (1, LANES)))

    return _rows("loss_head", body, [y, target], [_sds(y.shape)], tt=512, accs=[_sds((1, LANES))])


MESH_ID = pl.DeviceIdType.MESH
HBM_SPEC = pl.BlockSpec(memory_space=pltpu.HBM)


def _all_gather(name, x):
    m_per, n = x.shape

    def body(x_ref, out_ref, send_sems, recv_sems, local_sem):
        x, y, c = lax.axis_index("x"), lax.axis_index("y"), lax.axis_index("c")
        me, sibling = (x, y, c), (x, y, 1 - c)
        chips = [(1 - x, y), (x, 1 - y), (1 - x, 1 - y)]

        def rows(px, py, pc):
            return out_ref.at[pl.ds((4 * px + 2 * py + pc) * m_per, m_per), :]

        def copy(k, block, to, src=None):
            return pltpu.make_async_remote_copy(
                src_ref=rows(*block) if src is None else src, dst_ref=rows(*block),
                send_sem=send_sems.at[k], recv_sem=recv_sems.at[k], device_id=to, device_id_type=MESH_ID)

        mine = pltpu.make_async_copy(x_ref, rows(*me), local_sem)
        mine.start()
        first = [copy(0, me, sibling, src=x_ref)]
        first += [copy(1 + j, me, (*chip, c), src=x_ref) for j, chip in enumerate(chips)]
        for cp in first:
            cp.start()
        passed = [copy(4 + j, (*chip, c), sibling) for j, chip in enumerate(chips)]
        for j, chip in enumerate(chips):
            copy(1 + j, (*chip, c), me).wait_recv()
            passed[j].start()
        copy(0, sibling, me).wait_recv()
        for j, chip in enumerate(chips):
            copy(4 + j, (*chip, 1 - c), me).wait_recv()
        for cp in first + passed:
            cp.wait_send()
        mine.wait()

    return pl.pallas_call(
        body, name=name,
        out_shape=jax.ShapeDtypeStruct((N_DEV * m_per, n), x.dtype),
        in_specs=[HBM_SPEC], out_specs=HBM_SPEC,
        scratch_shapes=[pltpu.SemaphoreType.DMA((7,)), pltpu.SemaphoreType.DMA((7,)), pltpu.SemaphoreType.DMA],
    )(x)


def _all_gather_groups(name, xs):
    ng = len(xs)

    def body(*refs):
        x_refs, out_refs, token = refs[:ng], refs[ng:2 * ng], refs[2 * ng]
        send_sems, recv_sems, local_sems = refs[2 * ng + 1:]
        token[...] = jnp.zeros(token.shape, F32)
        x, y, c = lax.axis_index("x"), lax.axis_index("y"), lax.axis_index("c")
        me, sibling = (x, y, c), (x, y, 1 - c)
        chips = [(1 - x, y), (x, 1 - y), (1 - x, 1 - y)]

        def copy(g, k, block, to, src=None):
            px, py, pc = block
            dst = out_refs[g].at[4 * px + 2 * py + pc]
            return pltpu.make_async_remote_copy(
                src_ref=dst if src is None else src, dst_ref=dst,
                send_sem=send_sems.at[g, k], recv_sem=recv_sems.at[g, k], device_id=to, device_id_type=MESH_ID)

        mine = [pltpu.make_async_copy(x_refs[g], out_refs[g].at[4 * x + 2 * y + c], local_sems.at[g]) for g in range(ng)]
        for cp in mine:
            cp.start()
        first = []
        for g in range(ng):
            first.append(copy(g, 0, me, sibling, src=x_refs[g]))
            first += [copy(g, 1 + j, me, (*chip, c), src=x_refs[g]) for j, chip in enumerate(chips)]
        for cp in first:
            cp.start()
        passed = []
        for j, chip in enumerate(chips):
            for g in range(ng):
                copy(g, 1 + j, (*chip, c), me).wait_recv()
                passed.append(copy(g, 4 + j, (*chip, c), sibling))
                passed[-1].start()
        for g in range(ng):
            copy(g, 0, sibling, me).wait_recv()
            for j, chip in enumerate(chips):
                copy(g, 4 + j, (*chip, 1 - c), me).wait_recv()
        for cp in first + passed:
            cp.wait_send()
        for cp in mine:
            cp.wait()

    return pl.pallas_call(
        body, name=name,
        out_shape=[jax.ShapeDtypeStruct((N_DEV,) + x.shape, x.dtype) for x in xs] + [_sds((8, LANES))],
        in_specs=[HBM_SPEC] * ng, out_specs=[HBM_SPEC] * ng + [pl.BlockSpec(memory_space=pltpu.VMEM)],
        scratch_shapes=[pltpu.SemaphoreType.DMA((ng, 7)), pltpu.SemaphoreType.DMA((ng, 7)), pltpu.SemaphoreType.DMA((ng,))],
    )(*xs)


EFFECT = pltpu.SideEffectType.DATAFLOW_SIDE_EFFECTING
SEM_SPEC = pl.BlockSpec(memory_space=pltpu.SEMAPHORE)


def _push_copies(src_refs, land_refs, send_sems, recv_sems, chunked):
    x, y, c = lax.axis_index("x"), lax.axis_index("y"), lax.axis_index("c")
    me = 4 * x + 2 * y + c
    copies = []
    for g, (src, land) in enumerate(zip(src_refs, land_refs)):
        for k in range(1, N_DEV):
            px = 1 - x if k & 4 else x
            py = 1 - y if k & 2 else y
            pc = 1 - c if k & 1 else c
            copies.append(pltpu.make_async_remote_copy(
                src_ref=src.at[4 * px + 2 * py + pc] if chunked else src, dst_ref=land.at[me],
                send_sem=send_sems.at[g * (N_DEV - 1) + k - 1], recv_sem=recv_sems.at[g * (N_DEV - 1) + k - 1],
                device_id=(px, py, pc), device_id_type=MESH_ID))
    return copies


def _hbm(x):
    return pltpu.with_memory_space_constraint(x, pltpu.HBM)


def _push_start(name, srcs, lands, chunked):
    ng = len(srcs)

    def body(*refs):
        for cp in _push_copies(refs[:ng], refs[ng:2 * ng], refs[2 * ng], refs[2 * ng + 1], chunked):
            cp.start()
        refs[-1][...] = jnp.zeros(refs[-1].shape, F32)

    bufs = list(srcs) + list(lands)
    outs = pl.pallas_call(
        body, name=name,
        out_shape=(pltpu.SemaphoreType.DMA((ng * (N_DEV - 1),)), pltpu.SemaphoreType.DMA((ng * (N_DEV - 1),)),
                   *[pltpu.HBM(b.shape, b.dtype) for b in bufs], jax.ShapeDtypeStruct((8, LANES), F32)),
        in_specs=[HBM_SPEC] * (2 * ng),
        out_specs=(SEM_SPEC, SEM_SPEC, *[HBM_SPEC] * (2 * ng), pl.BlockSpec(memory_space=pltpu.VMEM)),
        input_output_aliases={i: 2 + i for i in range(2 * ng)},
        compiler_params=pltpu.CompilerParams(has_side_effects=EFFECT),
    )(*[_hbm(b) for b in bufs])
    return outs[0], outs[1], list(outs[2:2 + ng]), list(outs[2 + ng:2 + 2 * ng]), outs[-1]


def _push_wait(name, started, after, chunked):
    send_sems, recv_sems, srcs, lands, _ = started
    ng = len(srcs)

    def body(*refs):
        copies = _push_copies(refs[:ng], refs[ng:2 * ng], refs[2 * ng], refs[2 * ng + 1], chunked)
        for cp in copies:
            cp.wait_send()
        for cp in copies:
            cp.wait_recv()

    bufs = srcs + lands
    outs = pl.pallas_call(
        body, name=name,
        out_shape=tuple(pltpu.HBM(b.shape, b.dtype) for b in bufs),
        in_specs=[HBM_SPEC] * (2 * ng) + [SEM_SPEC, SEM_SPEC, pl.BlockSpec(memory_space=pl.ANY)],
        out_specs=tuple([HBM_SPEC] * (2 * ng)),
        input_output_aliases={i: i for i in range(2 * ng)},
        compiler_params=pltpu.CompilerParams(has_side_effects=EFFECT),
    )(*bufs, send_sems, recv_sems, after)
    return list(outs[:ng]), list(outs[ng:])


def _sum_adam_devices(sent, recv, dev, w, m, v):
    ndev, r, c_ = recv.shape
    tr = _pick(r, (256, 128, 96, 32))

    def body(dev_ref, own_ref, r_ref, w_ref, m_ref, v_ref, g_ref, d_ref, mo_ref, vo_ref):
        me = dev_ref[0]
        g = jnp.where(me == 0, own_ref[0], r_ref[0]).astype(F32)
        for j in range(1, ndev):
            g = g + jnp.where(me == j, own_ref[0], r_ref[j]).astype(F32)
        g_ref[...] = g
        d_ref[...], mo_ref[...], vo_ref[...] = _adam(w_ref[...], g, m_ref[...], v_ref[...])

    row = pl.BlockSpec((tr, c_), lambda i, dev_ref: (i, 0))
    return pl.pallas_call(
        body, name="sum_adam",
        grid_spec=pltpu.PrefetchScalarGridSpec(
            num_scalar_prefetch=1, grid=(r // tr,),
            in_specs=[pl.BlockSpec((1, tr, c_), lambda i, dev_ref: (dev_ref[0], i, 0)),
                      pl.BlockSpec((ndev, tr, c_), lambda i, dev_ref: (0, i, 0)), row, row, row],
            out_specs=[row, row, row, row]),
        out_shape=[_sds((r, c_))] * 4,
        compiler_params=_cparams(dimension_semantics=("arbitrary",)),
    )(dev, sent, recv, w, m, v)


def _cols_from_shards(w, width):
    ns, r, cs = w.shape
    tr = _pick(r, (256, 128))

    def body(w_ref, o_ref):
        parts = [w_ref[j] for j in range(ns)]
        if width > ns * cs:
            parts.append(jnp.zeros((tr, width - ns * cs), w.dtype))
        o_ref[...] = jnp.concatenate(parts, axis=1)

    return pl.pallas_call(
        body, name="cols_from_shards", grid=(r // tr,),
        in_specs=[pl.BlockSpec((ns, tr, cs), lambda i: (0, i, 0))], out_specs=pl.BlockSpec((tr, width), lambda i: (i, 0)),
        out_shape=jax.ShapeDtypeStruct((r, width), w.dtype), compiler_params=_cparams(dimension_semantics=("arbitrary",)),
    )(w)


def _shards_from_cols(g, cs):
    r, width = g.shape
    tr = _pick(r, (256, 128))

    def body(g_ref, o_ref):
        for j in range(N_DEV):
            o_ref[j] = g_ref[:, j * cs:(j + 1) * cs]

    return pl.pallas_call(
        body, name="shards_from_cols", grid=(r // tr,),
        in_specs=[pl.BlockSpec((tr, width), lambda i: (i, 0))], out_specs=pl.BlockSpec((N_DEV, tr, cs), lambda i: (0, i, 0)),
        out_shape=jax.ShapeDtypeStruct((N_DEV, r, cs), g.dtype), compiler_params=_cparams(dimension_semantics=("arbitrary",)),
    )(g)


def _adam(w, g, m, v):
    m = ADAM_B1 * m + (1.0 - ADAM_B1) * g
    v = ADAM_B2 * v + (1.0 - ADAM_B2) * (g * g)
    m_hat = m / (1.0 - ADAM_B1 ** ADAM_STEP)
    v_hat = v / (1.0 - ADAM_B2 ** ADAM_STEP)
    return -ADAM_LR * (m_hat / (jnp.sqrt(v_hat) + ADAM_EPS) + ADAM_WD * w), m, v


def _sum_devices(gathered):
    m_all, n = gathered.shape
    m_per = m_all // N_DEV

    def body(x_ref, o_ref):
        s = x_ref[0:m_per, :]
        for j in range(1, N_DEV):
            s = s + x_ref[j * m_per:(j + 1) * m_per, :]
        o_ref[...] = s

    return pl.pallas_call(body, name="sum_devices", out_shape=_sds((m_per, n)), compiler_params=_cparams())(gathered)


def _adam_small(w, g, m, v):
    def body(w_ref, g_ref, m_ref, v_ref, d_ref, mo_ref, vo_ref):
        d_ref[...], mo_ref[...], vo_ref[...] = _adam(w_ref[...], g_ref[...], m_ref[...], v_ref[...])

    return pl.pallas_call(body, name="adam_small", out_shape=[_sds(w.shape)] * 3, compiler_params=_cparams())(w, g, m, v)


N_LAYERS = 4
_MIXER = ("dn", "sb", "mla")
_MIXER_PARAMS = {
    "dn": ("dn_w_in", "dn_conv_w", "dn_a_log", "dn_dt_bias", "dn_out_norm", "dn_w_out"),
    "sb": ("sb_w_qkv", "sb_q_norm", "sb_k_norm", "sb_w_out"),
    "mla": ("mla_w_down", "mla_q_a_norm", "mla_kv_a_norm", "mla_w_uq", "mla_w_ukv", "mla_q_nope_norm", "mla_q_rope_norm",
            "mla_k_nope_norm", "mla_k_rope_norm", "mla_w_out"),
}
_BIG_AXIS = {"dn_w_in": 1, "dn_w_out": 0, "sb_w_qkv": 1, "sb_w_out": 0, "mla_w_down": 0, "mla_w_uq": 1, "mla_w_ukv": 1,
             "mla_w_out": 0, "ffn_w_gate_up": 1, "ffn_w_down": 0}


def _weight_names():
    names = []
    for i in range(N_LAYERS):
        p = "l%d_" % i
        names += [p + "mix_norm"] + [p + n for n in _MIXER_PARAMS[_MIXER[i % 3]]] + [p + "ffn_norm", p + "ffn_w_gate_up", p + "ffn_w_down"]
    return names


WEIGHTS = _weight_names()
BIG = [n for n in WEIGHTS if n[3:] in _BIG_AXIS]
SMALL = [n for n in WEIGHTS if n[3:] not in _BIG_AXIS]
CONV = [n for n in SMALL if n.endswith("conv_w")]


def _ceil_to(n, k):
    return -(-n // k) * k


def _pack(arrs, cols, row_mult):
    parts = []
    for a in arrs:
        f = a.reshape(-1)
        parts.append(jnp.pad(f, (0, _ceil_to(f.shape[0], cols) - f.shape[0])))
    flat = jnp.concatenate(parts)
    rows = _ceil_to(flat.shape[0] // cols, row_mult)
    return jnp.pad(flat, (0, rows * cols - flat.shape[0])).reshape(rows, cols)


def _unpack(buf, shapes):
    cols = buf.shape[-1]
    out, r0 = [], 0
    for s in shapes:
        n = math.prod(s)
        nr = _ceil_to(n, cols) // cols
        out.append(buf[r0:r0 + nr].reshape(-1)[:n].reshape(s))
        r0 += nr
    return out


def _layer_groups(i):
    by = {"gu": (704, []), "down": (1024, []), "out": (1024, []), "dn_in": (514, []), "sb_qkv": (384, []), "mla": (512, [])}
    key = {"ffn_w_gate_up": "gu", "ffn_w_down": "down", "dn_w_in": "dn_in", "sb_w_qkv": "sb_qkv", "mla_w_down": "mla",
           "mla_w_uq": "mla", "mla_w_ukv": "mla"}
    for n in BIG:
        if n.startswith("l%d_" % i):
            by[key.get(n[3:], "out")][1].append(n)
    return [g for g in by.values() if g[1]]


LAYER_GROUPS = [_layer_groups(i) for i in range(N_LAYERS)]
N_FFN_GROUPS = 2


def _stack_group(grp, get):
    width, names = grp
    if len(names) == 1 and get(names[0]).shape[1] == width:
        return get(names[0])
    return jnp.concatenate([jnp.pad(get(n), ((0, 0), (0, width - get(n).shape[1]))) for n in names], axis=0)


def _unstack_group(grp, buf, shape_of):
    if len(grp[1]) == 1 and tuple(shape_of(grp[1][0])) == tuple(buf.shape[-2:]):
        return [buf]
    out, r0 = [], 0
    for n in grp[1]:
        rs, cs = shape_of(n)
        out.append(buf[..., r0:r0 + rs, :cs])
        r0 += rs
    return out


def kernel(x, l0_mix_norm, l0_dn_w_in, l0_dn_conv_w, l0_dn_a_log, l0_dn_dt_bias, l0_dn_out_norm, l0_dn_w_out, l0_ffn_norm, l0_ffn_w_gate_up, l0_ffn_w_down, l1_mix_norm, l1_sb_w_qkv, l1_sb_q_norm, l1_sb_k_norm, l1_sb_w_out, l1_ffn_norm, l1_ffn_w_gate_up, l1_ffn_w_down, l2_mix_norm, l2_mla_w_down, l2_mla_q_a_norm, l2_mla_kv_a_norm, l2_mla_w_uq, l2_mla_w_ukv, l2_mla_q_nope_norm, l2_mla_q_rope_norm, l2_mla_k_nope_norm, l2_mla_k_rope_norm, l2_mla_w_out, l2_ffn_norm, l2_ffn_w_gate_up, l2_ffn_w_down, l3_mix_norm, l3_dn_w_in, l3_dn_conv_w, l3_dn_a_log, l3_dn_dt_bias, l3_dn_out_norm, l3_dn_w_out, l3_ffn_norm, l3_ffn_w_gate_up, l3_ffn_w_down, loss_target, m_l0_mix_norm, m_l0_dn_w_in, m_l0_dn_conv_w, m_l0_dn_a_log, m_l0_dn_dt_bias, m_l0_dn_out_norm, m_l0_dn_w_out, m_l0_ffn_norm, m_l0_ffn_w_gate_up, m_l0_ffn_w_down, m_l1_mix_norm, m_l1_sb_w_qkv, m_l1_sb_q_norm, m_l1_sb_k_norm, m_l1_sb_w_out, m_l1_ffn_norm, m_l1_ffn_w_gate_up, m_l1_ffn_w_down, m_l2_mix_norm, m_l2_mla_w_down, m_l2_mla_q_a_norm, m_l2_mla_kv_a_norm, m_l2_mla_w_uq, m_l2_mla_w_ukv, m_l2_mla_q_nope_norm, m_l2_mla_q_rope_norm, m_l2_mla_k_nope_norm, m_l2_mla_k_rope_norm, m_l2_mla_w_out, m_l2_ffn_norm, m_l2_ffn_w_gate_up, m_l2_ffn_w_down, m_l3_mix_norm, m_l3_dn_w_in, m_l3_dn_conv_w, m_l3_dn_a_log, m_l3_dn_dt_bias, m_l3_dn_out_norm, m_l3_dn_w_out, m_l3_ffn_norm, m_l3_ffn_w_gate_up, m_l3_ffn_w_down, v_l0_mix_norm, v_l0_dn_w_in, v_l0_dn_conv_w, v_l0_dn_a_log, v_l0_dn_dt_bias, v_l0_dn_out_norm, v_l0_dn_w_out, v_l0_ffn_norm, v_l0_ffn_w_gate_up, v_l0_ffn_w_down, v_l1_mix_norm, v_l1_sb_w_qkv, v_l1_sb_q_norm, v_l1_sb_k_norm, v_l1_sb_w_out, v_l1_ffn_norm, v_l1_ffn_w_gate_up, v_l1_ffn_w_down, v_l2_mix_norm, v_l2_mla_w_down, v_l2_mla_q_a_norm, v_l2_mla_kv_a_norm, v_l2_mla_w_uq, v_l2_mla_w_ukv, v_l2_mla_q_nope_norm, v_l2_mla_q_rope_norm, v_l2_mla_k_nope_norm, v_l2_mla_k_rope_norm, v_l2_mla_w_out, v_l2_ffn_norm, v_l2_ffn_w_gate_up, v_l2_ffn_w_down, v_l3_mix_norm, v_l3_dn_w_in, v_l3_dn_conv_w, v_l3_dn_a_log, v_l3_dn_dt_bias, v_l3_dn_out_norm, v_l3_dn_w_out, v_l3_ffn_norm, v_l3_ffn_w_gate_up, v_l3_ffn_w_down):
    a = dict(locals())
    return _train_step(a)


def _train_step(a):
    mx, my, mc = lax.axis_index("x"), lax.axis_index("y"), lax.axis_index("c")
    dev = 4 * mx + 2 * my + mc
    dev_arr = jnp.reshape(dev, (1,)).astype(jnp.int32)
    t, d = a["x"].shape[1], a["x"].shape[2]
    xs = a["x"].reshape(t, d)
    target = a["loss_target"].reshape(t, d)

    full = {}

    def unpack(groups, bufs):
        for grp, buf in zip(groups, bufs):
            for n, shards in zip(grp[1], _unstack_group(grp, buf, lambda n: a[n].shape)):
                kind = n[3:]
                if kind == "ffn_w_gate_up":
                    full[n] = shards
                elif _BIG_AXIS[kind] == 0:
                    full[n] = shards.reshape(N_DEV * shards.shape[1], shards.shape[2])
                else:
                    width = DN_PROJ if kind == "dn_w_in" else N_DEV * shards.shape[2]
                    full[n] = _cols_from_shards(shards, width)

    def local_shards(groups):
        return [_stack_group(grp, lambda n: a[n].astype(BF16)) for grp in groups]

    pushed_groups = {"l0_gate_up": LAYER_GROUPS[0][:1], "l0_down": LAYER_GROUPS[0][1:N_FFN_GROUPS]}
    pushed_groups.update({"l%d" % i: LAYER_GROUPS[i] for i in range(1, N_LAYERS)})
    conv_pack = _pack([a[n] for n in CONV], LANES, 8)
    first_groups = LAYER_GROUPS[0][N_FFN_GROUPS:]
    *first_bufs, conv_all, gathered = _all_gather_groups("gather_weights", local_shards(first_groups) + [conv_pack])
    unpack(first_groups, first_bufs)
    for n, parts in zip(CONV, zip(*[_unpack(conv_all[j], [a[n].shape for n in CONV]) for j in range(N_DEV)])):
        full[n] = jnp.concatenate(parts, axis=1)
    after_first = gathered[0, 0].astype(BF16)
    gathers, started = {}, jnp.zeros((), F32)
    for tag, groups in pushed_groups.items():
        srcs = local_shards(groups)
        srcs[-1] = srcs[-1] + after_first
        lands = [lax.dynamic_update_index_in_dim(lax.empty((N_DEV,) + s.shape, s.dtype), s, dev, 0) for s in srcs]
        gathers[tag] = _push_start("gather_start_" + tag, srcs, lands, False)
        started = started + gathers[tag][-1][0, 0]

    def arrived(tag, after):
        unpack(pushed_groups[tag], _push_wait("gather_wait_" + tag, gathers[tag], after, False)[1])

    def vec(n):
        return a[n].reshape(1, -1)

    cos, sin = _rope_tables(t)

    def mixer_args(i):
        p = "l%d_" % i
        kind = _MIXER[i % 3]
        if kind == "dn":
            args = (full[p + "dn_w_in"], full[p + "dn_conv_w"], _pad_lanes(a[p + "dn_a_log"]), _pad_lanes(a[p + "dn_dt_bias"]),
                    vec(p + "dn_out_norm"), full[p + "dn_w_out"])
        elif kind == "sb":
            args = (full[p + "sb_w_qkv"], vec(p + "sb_q_norm"), vec(p + "sb_k_norm"), full[p + "sb_w_out"])
        else:
            w_down, w_uq, w_ukv = _mla_layout(full[p + "mla_w_down"], full[p + "mla_w_uq"], full[p + "mla_w_ukv"])
            args = (w_down, vec(p + "mla_q_a_norm"), vec(p + "mla_kv_a_norm"), w_uq, w_ukv, vec(p + "mla_q_nope_norm"),
                    _pad_lanes(a[p + "mla_q_rope_norm"]), vec(p + "mla_k_nope_norm"), _pad_lanes(a[p + "mla_k_rope_norm"]),
                    full[p + "mla_w_out"], cos, sin)
        return kind, args

    fwd = {"dn": _dn_fwd, "sb": _sb_fwd, "mla": _mla_fwd}
    bwd = {"dn": _dn_bwd, "sb": _sb_bwd, "mla": _mla_bwd}
    saved, layer_args = [], []
    for i in range(N_LAYERS):
        p = "l%d_" % i
        if i > 0:
            arrived("l%d" % i, xs)
        kind, args = mixer_args(i)
        layer_args.append((kind, args))
        gain = vec(p + "mix_norm") + started if i == 0 else vec(p + "mix_norm")
        h = _rmsnorm_fwd("mix_norm", xs, gain)
        x_mid, sv_mix = fwd[kind](xs, h, *args)
        if i == 0:
            arrived("l0_gate_up", x_mid)

        def down_weight(act, p=p, first=(i == 0)):
            if first:
                arrived("l0_down", act)
            return full[p + "ffn_w_down"]

        x_out, sv_ffn = _ffn_fwd(x_mid, vec(p + "ffn_norm"), full[p + "ffn_w_gate_up"], down_weight)
        saved.append((xs, sv_mix, sv_ffn))
        xs = x_out
    dy, loss_part = _loss_head(xs, target)

    grads, big_out = {}, {}

    def grad_shards(n):
        g, (rs, cs) = grads[n], a[n].shape
        if g.ndim == 3:
            return g
        if _BIG_AXIS[n[3:]] == 0:
            return g.reshape(N_DEV, rs, cs)
        return _shards_from_cols(g, cs)

    def push_grads(tag, groups):
        sends = []
        for grp in groups:
            if len(grp[1]) == 1 and a[grp[1][0]].shape[1] == grp[0]:
                sends.append(grad_shards(grp[1][0]))
                continue
            parts = [jnp.pad(grad_shards(n), ((0, 0), (0, 0), (0, grp[0] - a[n].shape[1]))) for n in grp[1]]
            sends.append(jnp.concatenate(parts, axis=1))
        lands = [lax.empty(s.shape, s.dtype) for s in sends]
        return tag, groups, _push_start("grads_start_" + tag, sends, lands, True)

    def finish_grads(push, after):
        tag, groups, pushed = push
        sents, recvs = _push_wait("grads_wait_" + tag, pushed, after, True)
        for grp, sent, recv in zip(groups, sents, recvs):
            packs = [_stack_group(grp, lambda n, pre=pre: a[pre + n]) for pre in ("", "m_", "v_")]
            outs = [_unstack_group(grp, o, lambda n: a[n].shape) for o in _sum_adam_devices(sent, recv, dev_arr, *packs)]
            for j, n in enumerate(grp[1]):
                big_out[n] = [o[j] for o in outs]

    mixer_push = None
    for i in reversed(range(N_LAYERS)):
        p = "l%d_" % i
        kind, args = layer_args[i]
        x_in, sv_mix, sv_ffn = saved[i]
        gain = vec(p + "ffn_norm") if mixer_push is None else vec(p + "ffn_norm") + mixer_push[2][-1][0, 0]
        dx_mid, grads[p + "ffn_norm"], grads[p + "ffn_w_gate_up"], grads[p + "ffn_w_down"] = _ffn_bwd(
            sv_ffn, dy, gain, full[p + "ffn_w_gate_up"], full[p + "ffn_w_down"])
        ffn_push = push_grads("l%d_ffn" % i, LAYER_GROUPS[i][:N_FFN_GROUPS])
        if mixer_push is not None:
            finish_grads(mixer_push, dx_mid)
        res = bwd[kind](sv_mix, dx_mid, *args)
        dh = res[0]
        if kind == "mla":
            res = list(res)
            res[1], res[4], res[5] = _mla_unlayout(res[1], res[4], res[5])
        for n, g in zip(_MIXER_PARAMS[kind], res[1:]):
            grads[p + n] = g
        mixer_push = push_grads("l%d_mix" % i, LAYER_GROUPS[i][N_FFN_GROUPS:])
        pushed = ffn_push[2][-1][0, 0] + mixer_push[2][-1][0, 0]
        dy, grads[p + "mix_norm"] = _rmsnorm_bwd("mix_norm_bwd", x_in, vec(p + "mix_norm") + pushed, dh, dx_mid)
        finish_grads(ffn_push, dy)
    grad_x = dy.reshape(a["x"].shape)

    small_full_shapes = [full[n].shape if n in CONV else a[n].shape for n in SMALL]
    small_grads = []
    for n, s in zip(SMALL, small_full_shapes):
        g = grads[n].reshape(-1)
        small_grads.append(g[:math.prod(s)])
    small_pack = _pack(small_grads + [loss_part.reshape(-1)], LANES, 8)
    small_sum = _sum_devices(_all_gather("gather_small_grads", small_pack))
    small_red = _unpack(small_sum, small_full_shapes + [(LANES,)])
    loss = small_red[-1][0]
    g_small = {}
    for n, g in zip(SMALL, small_red[:-1]):
        if n in CONV:
            cs = a[n].shape[1]
            g = lax.dynamic_slice_in_dim(g, dev * cs, cs, axis=1)
        g_small[n] = g
    small_shapes = [a[n].shape for n in SMALL]
    packs = [_pack([src[n] for n in SMALL], LANES, 8) for src in
             ({n: a[n] for n in SMALL}, g_small, {n: a["m_" + n] for n in SMALL}, {n: a["v_" + n] for n in SMALL})]
    d_small, m_small, v_small = (_unpack(o, small_shapes) for o in _adam_small(*packs))
    finish_grads(mixer_push, small_sum)

    small_out = dict(zip(SMALL, zip([g_small[n] for n in SMALL], d_small, m_small, v_small)))

    def out(k):
        return [small_out[n][k] if n in small_out else big_out[n][k] for n in WEIGHTS]

    return (loss, grad_x, *out(0), *out(1), *out(2), *out(3))
```

```python
import math

import jax
import jax.numpy as jnp
from jax import lax
from jax.experimental import pallas as pl
from jax.experimental.pallas import tpu as pltpu

F32 = jnp.float32
BF16 = jnp.bfloat16
GRAD_DTYPE = BF16
F32X3 = lax.Precision.HIGH

LANES = 128
N_DEV = 8
N_HEADS = 8
HEAD = 128
NORM_EPS = 1e-6
DN_CHUNK = 64
DN_STEP = 2 * DN_CHUNK
ATT_BLOCK = 512
ATT_Q = 512
MLA_ROPE = 64
MLA_QK = 192
ROPE_THETA = 10000.0
VMEM_LIMIT = 56 * 1024 * 1024
MM_TILE_BYTES = 32 * 1024 * 1024

ADAM_LR = 0.001
ADAM_B1 = 0.9
ADAM_B2 = 0.999
ADAM_EPS = 1e-08
ADAM_WD = 0.01
ADAM_STEP = 10


def _cparams(**kw):
    return pltpu.CompilerParams(vmem_limit_bytes=VMEM_LIMIT, **kw)


def _pick(n, cands):
    for c in cands:
        if c <= n and n % c == 0:
            return c
    return n


def _mm(name, a, b, *, ta=False, tb=False, out_dtype=F32, add=None, tm=None, tn=None, tk=None):
    if ta:
        K, M = a.shape
    else:
        M, K = a.shape
    N = b.shape[0] if tb else b.shape[1]
    tm = tm or _pick(M, (1024, 512, 256, 128))
    tn = tn or _pick(N, (1024, 512, 384, 256, 128))
    if tk is None:
        fits = [c for c in (4096, 2048, 1408, 1024, 512, 384, 256, 128)
                if c <= K and K % c == 0 and 2 * c * (tm * a.dtype.itemsize + tn * b.dtype.itemsize) <= MM_TILE_BYTES]
        tk = fits[0] if fits else K
    return _mm_raw(
        name, a, b, ta=ta, tb=tb, out_dtype=out_dtype, add=add, grid=(M // tm, N // tn, K // tk), out_shape=(M, N),
        a_block=(tk, tm) if ta else (tm, tk), a_map=(lambda i, j, k: (k, i)) if ta else (lambda i, j, k: (i, k)),
        b_block=(tn, tk) if tb else (tk, tn), b_map=(lambda i, j, k: (j, k)) if tb else (lambda i, j, k: (k, j)),
        o_block=(tm, tn), o_map=lambda i, j, k: (i, j))


def _mm_raw(name, a, b, *, ta, tb, out_dtype, add, grid, out_shape, a_block, a_map, b_block, b_map, o_block, o_map):
    nk = grid[2]
    tm, tn = o_block
    dn = (((0 if ta else 1,), (1 if tb else 0,)), ((), ()))
    has_add = add is not None

    def kern(*refs):
        if has_add:
            a_ref, b_ref, add_ref, o_ref, acc_ref = refs
        else:
            a_ref, b_ref, o_ref, acc_ref = refs
        k = pl.program_id(2)
        part = lax.dot_general(a_ref[...].astype(BF16), b_ref[...].astype(BF16), dn, preferred_element_type=F32)

        @pl.when(k == 0)
        def _():
            acc_ref[...] = part

        @pl.when(k > 0)
        def _():
            acc_ref[...] += part

        @pl.when(k == nk - 1)
        def _():
            r = acc_ref[...]
            if has_add:
                r = r + add_ref[...]
            o_ref[...] = r.astype(out_dtype)

    in_specs = [pl.BlockSpec(a_block, a_map), pl.BlockSpec(b_block, b_map)]
    args = [a, b]
    if has_add:
        in_specs.append(pl.BlockSpec(o_block, o_map))
        args.append(add)
    return pl.pallas_call(
        kern, name=name,
        grid=grid,
        in_specs=in_specs,
        out_specs=pl.BlockSpec(o_block, o_map),
        out_shape=jax.ShapeDtypeStruct(out_shape, out_dtype),
        scratch_shapes=[pltpu.VMEM((tm, tn), F32)],
        compiler_params=_cparams(dimension_semantics=("parallel", "parallel", "arbitrary")),
    )(*args)


def _rows(name, body, ins, outs, *, tt, consts=(), accs=()):
    in_specs, args = [], []
    first = ins[0][0] if isinstance(ins[0], tuple) else ins[0]
    t = first.shape[-2]
    tt = min(tt, t)
    for x in ins:
        if isinstance(x, tuple):
            arr, bs, im = x
            in_specs.append(pl.BlockSpec(bs, im))
            args.append(arr)
        else:
            in_specs.append(_row_spec(x.shape, tt))
            args.append(x)
    for c in consts:
        in_specs.append(pl.BlockSpec(c.shape, lambda i, _n=c.ndim: (0,) * _n))
        args.append(c)
    out_specs = [_row_spec(o.shape, tt) for o in outs]
    out_specs += [pl.BlockSpec(a.shape, lambda i, _n=len(a.shape): (0,) * _n) for a in accs]
    res = pl.pallas_call(
        body, name=name, grid=(t // tt,),
        in_specs=in_specs, out_specs=out_specs, out_shape=list(outs) + list(accs),
        compiler_params=_cparams(dimension_semantics=("arbitrary",)),
    )(*args)
    return res


def _row_spec(shape, tt):
    if len(shape) == 2:
        return pl.BlockSpec((tt, shape[1]), lambda i: (i, 0))
    return pl.BlockSpec((shape[0], tt, shape[2]), lambda i: (0, i, 0))


def _sds(shape, dtype=F32):
    return jax.ShapeDtypeStruct(tuple(shape), dtype)


def _acc(ref, val):
    i = pl.program_id(0)

    @pl.when(i == 0)
    def _():
        ref[...] = val

    @pl.when(i > 0)
    def _():
        ref[...] += val


def _rms(x, g):
    return x * lax.rsqrt(jnp.mean(x * x, axis=-1, keepdims=True) + NORM_EPS) * g


def _silu(x):
    return x / (1.0 + jnp.exp(-x))


def _softplus(x):
    return jnp.maximum(x, 0.0) + jnp.log(1.0 + jnp.exp(-jnp.abs(x)))


def _sigmoid(x):
    return 1.0 / (1.0 + jnp.exp(-x))


def _rmsnorm_fwd(name, x, g, tt=512):
    def body(x_ref, g_ref, h_ref):
        h_ref[...] = _rms(x_ref[...], g_ref[...]).astype(BF16)

    return _rows(name, body, [x], [_sds(x.shape, BF16)], tt=tt, consts=[g])[0]


def _rmsnorm_bwd(name, x, g, dh, dres, tt=512):
    def body(x_ref, dh_ref, dres_ref, g_ref, dx_ref, dg_ref):
        _, vjp = jax.vjp(_rms, x_ref[...], g_ref[...])
        dx, dg = vjp(dh_ref[...])
        dx_ref[...] = dx + dres_ref[...]
        _acc(dg_ref, dg)

    return _rows(name, body, [x, dh, dres], [_sds(x.shape)], tt=tt, consts=[g], accs=[_sds(g.shape)])


def _ffn_fwd(x, norm_g, w3, down_weight):
    t, d = x.shape
    ns, _, cs = w3.shape
    half = ns // 2
    w2 = w3.reshape(ns * d, cs)
    h = _rmsnorm_fwd("ffn_norm", x, norm_g)
    tm = _pick(t, (1024, 512, 256, 128))
    nm = t // tm

    def gate_up(h_ref, wg_ref, wu_ref, g_ref, u_ref, a_ref):
        hv = h_ref[...]
        g = _dot(hv, wg_ref[...])
        u = _dot(hv, wu_ref[...])
        g_ref[...] = g
        u_ref[...] = u
        a_ref[...] = (_silu(g) * u).astype(BF16)

    hid = pl.BlockSpec((tm, cs), lambda j, i: (j * nm + i, 0))
    g, u, act = pl.pallas_call(
        gate_up, name="ffn_gate_up", grid=(half, nm),
        in_specs=[pl.BlockSpec((tm, d), lambda j, i: (i, 0)), pl.BlockSpec((d, cs), lambda j, i: (j, 0)),
                  pl.BlockSpec((d, cs), lambda j, i: (j + half, 0))],
        out_specs=[hid, hid, hid], out_shape=[_sds((half * t, cs)), _sds((half * t, cs)), _sds((half * t, cs), BF16)],
        compiler_params=_cparams(dimension_semantics=("parallel", "arbitrary")),
    )(h, w2, w2)
    w_down = down_weight(act)

    def down(a_ref, w_ref, x_ref, y_ref):
        y = x_ref[...]
        for j in range(half):
            y = y + _dot(a_ref[j], w_ref[j])
        y_ref[...] = y

    y = pl.pallas_call(
        down, name="ffn_down", grid=(nm,),
        in_specs=[pl.BlockSpec((half, tm, cs), lambda i: (0, i, 0)), pl.BlockSpec((half, cs, d), lambda i: (0, 0, 0)),
                  pl.BlockSpec((tm, d), lambda i: (i, 0))],
        out_specs=pl.BlockSpec((tm, d), lambda i: (i, 0)), out_shape=_sds((t, d)),
        compiler_params=_cparams(dimension_semantics=("parallel",)),
    )(act.reshape(half, t, cs), w_down.reshape(half, cs, d), x)
    return y, (x, h, g, u, act)


def _ffn_bwd(saved, dy, norm_g, w3, w_down):
    x, h, g, u, act = saved
    t, d = x.shape
    ns, _, cs = w3.shape
    half = ns // 2
    w2 = w3.reshape(ns * d, cs)
    tm = _pick(t, (1024, 512, 256, 128))
    nm = t // tm
    tk = _pick(t, (4096, 2048, 1024, 512, 256, 128))
    nk = t // tk
    d_wdown = _mm_raw("ffn_down_wgrad", act, dy, ta=True, tb=False, out_dtype=GRAD_DTYPE, add=None, grid=(half, 1, nk),
                      out_shape=(half * cs, d), a_block=(tk, cs), a_map=lambda i, j, k: (i * nk + k, 0),
                      b_block=(tk, d), b_map=lambda i, j, k: (k, 0), o_block=(cs, d), o_map=lambda i, j, k: (i, 0))
    def down_dgrad(dy_ref, wd_ref, g_ref, u_ref, dg_ref, du_ref):
        da = _dot_nt(dy_ref[...].astype(BF16), wd_ref[...])
        gv, uv = g_ref[...], u_ref[...]
        s = _sigmoid(gv)
        dg_ref[...] = (da * uv * s * (1.0 + gv * (1.0 - s))).astype(BF16)
        du_ref[...] = (da * gv * s).astype(BF16)

    hid = pl.BlockSpec((tm, cs), lambda j, i: (j * nm + i, 0))
    dg, du = pl.pallas_call(
        down_dgrad, name="ffn_down_dgrad", grid=(half, nm),
        in_specs=[pl.BlockSpec((tm, d), lambda j, i: (i, 0)), pl.BlockSpec((cs, d), lambda j, i: (j, 0)), hid, hid],
        out_specs=[hid, hid], out_shape=[_sds((half * t, cs), BF16), _sds((half * t, cs), BF16)],
        compiler_params=_cparams(dimension_semantics=("parallel", "arbitrary")),
    )(dy, w_down, g, u)

    def wgrad(name, dd):
        return _mm_raw(name, h, dd, ta=True, tb=False, out_dtype=GRAD_DTYPE, add=None, grid=(1, half, nk), out_shape=(half * d, cs),
                       a_block=(tk, d), a_map=lambda i, j, k: (k, 0), b_block=(tk, cs), b_map=lambda i, j, k: (j * nk + k, 0),
                       o_block=(d, cs), o_map=lambda i, j, k: (j, 0))

    def gate_up_dgrad(dg_ref, du_ref, w_ref, dh_ref):
        dh = _dot_nt(dg_ref[0], w_ref[0]) + _dot_nt(du_ref[0], w_ref[half])
        for j in range(1, half):
            dh = dh + _dot_nt(dg_ref[j], w_ref[j]) + _dot_nt(du_ref[j], w_ref[half + j])
        dh_ref[...] = dh

    th = _pick(t, (512, 256, 128))
    hid3 = pl.BlockSpec((half, th, cs), lambda i: (0, i, 0))
    dh = pl.pallas_call(
        gate_up_dgrad, name="ffn_gate_up_dgrad", grid=(t // th,),
        in_specs=[hid3, hid3, pl.BlockSpec((ns, d, cs), lambda i: (0, 0, 0))],
        out_specs=pl.BlockSpec((th, d), lambda i: (i, 0)), out_shape=_sds((t, d)),
        compiler_params=_cparams(dimension_semantics=("parallel",)),
    )(dg.reshape(half, t, cs), du.reshape(half, t, cs), w3)
    d_w3 = jnp.concatenate([wgrad("ffn_gate_wgrad", dg), wgrad("ffn_up_wgrad", du)], axis=0).reshape(ns, d, cs)
    dx, dgain = _rmsnorm_bwd("ffn_norm_bwd", x, norm_g, dh, dy)
    return dx, dgain, d_w3, d_wdown


def _dot_nt(a, b):
    return lax.dot_general(a, b, (((1,), (1,)), ((), ())), preferred_element_type=F32)


def _dot_tn(a, b):
    return lax.dot_general(a, b, (((0,), (0,)), ((), ())), preferred_element_type=F32)


def _dot(a, b):
    return jnp.dot(a, b, preferred_element_type=F32)


CUM_BLOCK = 128


def _tri2(lower):
    r = lax.broadcasted_iota(jnp.int32, (CUM_BLOCK, CUM_BLOCK), 0)
    c = lax.broadcasted_iota(jnp.int32, (CUM_BLOCK, CUM_BLOCK), 1)
    tri = ((r > c) if lower else (r < c)).astype(BF16)
    return jnp.concatenate([tri, tri], axis=0)


def _run_sums(x, tri2, run, reverse):
    nb = x.shape[1] // CUM_BLOCK
    outs = [None] * nb
    for j in (reversed(range(nb)) if reverse else range(nb)):
        xj = x[:, j * CUM_BLOCK:(j + 1) * CUM_BLOCK]
        hi = xj.astype(BF16)
        lo = (xj - hi.astype(F32)).astype(BF16)
        outs[j] = _dot(jnp.concatenate([hi, lo], axis=1), tri2) + run
        run = run + jnp.sum(xj, axis=1, keepdims=True)
    return jnp.concatenate(outs, axis=1), run


def _log_sigmoid(z):
    return jnp.minimum(z, 0.0) - jnp.log(1.0 + jnp.exp(-jnp.abs(z)))


def _heads_in(ref, h, width=HEAD):
    return ref[:, h * width:(h + 1) * width]


def _sb_qk(q, k, gq, gk):
    return _rms(q, gq) * (HEAD ** -0.5), _rms(k, gk)


def _sb_prep_fwd(qkv, gq, gk):
    t = qkv.shape[0]

    def body(x_ref, gq_ref, gk_ref, q_ref, k_ref, v_ref):
        for h in range(N_HEADS):
            q, k = _sb_qk(_heads_in(x_ref, h), _heads_in(x_ref, N_HEADS + h), gq_ref[...], gk_ref[...])
            q_ref[h] = q.astype(BF16)
            k_ref[h] = k.astype(BF16)
            v_ref[h] = _heads_in(x_ref, 2 * N_HEADS + h).astype(BF16)

    hm = _sds((N_HEADS, t, HEAD), BF16)
    return _rows("sb_prep", body, [qkv], [hm, hm, hm], tt=256, consts=[gq, gk])


def _sb_prep_bwd(qkv, gq, gk, dq, dk, dv):
    def body(x_ref, dq_ref, dk_ref, dv_ref, gq_ref, gk_ref, dx_ref, dgq_ref, dgk_ref):
        dgq = jnp.zeros(gq_ref.shape, F32)
        dgk = jnp.zeros(gk_ref.shape, F32)
        for h in range(N_HEADS):
            _, vjp = jax.vjp(_sb_qk, _heads_in(x_ref, h), _heads_in(x_ref, N_HEADS + h), gq_ref[...], gk_ref[...])
            a, b, c, d = vjp((dq_ref[h], dk_ref[h]))
            dx_ref[:, h * HEAD:(h + 1) * HEAD] = a.astype(BF16)
            dx_ref[:, (N_HEADS + h) * HEAD:(N_HEADS + h + 1) * HEAD] = b.astype(BF16)
            dx_ref[:, (2 * N_HEADS + h) * HEAD:(2 * N_HEADS + h + 1) * HEAD] = dv_ref[h].astype(BF16)
            dgq, dgk = dgq + c, dgk + d
        _acc(dgq_ref, dgq)
        _acc(dgk_ref, dgk)

    return _rows("sb_prep_bwd", body, [qkv, dq, dk, dv], [_sds(qkv.shape, BF16)], tt=256, consts=[gq, gk],
                 accs=[_sds(gq.shape), _sds(gk.shape)])


def _q_block(t):
    return min(ATT_Q, t)


def _key_order(bq, qb):
    rows = lax.broadcasted_iota(jnp.int32, (bq, ATT_BLOCK), 0)
    cols = lax.broadcasted_iota(jnp.int32, (bq, ATT_BLOCK), 1)
    return rows - cols + qb * bq


def _sb_attn_fwd(q, k, v):
    nh, t, _ = q.shape
    bq = _q_block(t)
    per = bq // ATT_BLOCK

    def kern(q_ref, k_ref, v_ref, o_ref):
        qb = pl.program_id(1)
        qv = q_ref[0]
        after = _tri2(True)
        order = _key_order(bq, qb)
        nkb = (qb + 1) * per

        def body(i, carry, diagonal):
            o_acc, run = carry
            kb = nkb - 1 - i
            off = pl.multiple_of(kb * ATT_BLOCK, ATT_BLOCK)
            kv = k_ref[0, pl.ds(off, ATT_BLOCK), :]
            vv = v_ref[0, pl.ds(off, ATT_BLOCK), :]
            z = _dot_nt(qv, kv)
            lsz = _log_sigmoid(z)
            lsn = lsz - z
            if diagonal:
                past = order > kb * ATT_BLOCK
                lsn = jnp.where(past, lsn, 0.0)
            la, run = _run_sums(lsn, after, run, True)
            a = jnp.exp(lsz + la)
            if diagonal:
                a = jnp.where(past, a, 0.0)
            o_acc = o_acc + _dot(a.astype(BF16), vv)
            return o_acc, run

        carry = lax.fori_loop(0, per, lambda i, c: body(i, c, True), (jnp.zeros((bq, HEAD), F32), jnp.zeros((bq, 1), F32)))
        o, _ = lax.fori_loop(per, nkb, lambda i, c: body(i, c, False), carry)
        o_ref[...] = o

    return pl.pallas_call(
        kern, name="sb_attn_fwd", grid=(nh, t // bq),
        in_specs=[pl.BlockSpec((1, bq, HEAD), lambda h, i: (h, i, 0)),
                  pl.BlockSpec((1, t, HEAD), lambda h, i: (h, 0, 0)),
                  pl.BlockSpec((1, t, HEAD), lambda h, i: (h, 0, 0))],
        out_specs=pl.BlockSpec((bq, HEAD), lambda h, i: (i, h)),
        out_shape=_sds((t, nh * HEAD)),
        compiler_params=_cparams(dimension_semantics=("parallel", "arbitrary")),
    )(q, k, v)


def _sb_attn_bwd(q, k, v, do):
    nh, t, _ = q.shape
    bq = _q_block(t)
    per = bq // ATT_BLOCK

    def kern(q_ref, k_ref, v_ref, do_ref, dq_ref, dk_ref, dv_ref, g_s, ls_s):
        qb = pl.program_id(1)

        @pl.when(qb == 0)
        def _():
            dk_ref[...] = jnp.zeros(dk_ref.shape, F32)
            dv_ref[...] = jnp.zeros(dv_ref.shape, F32)

        qv = q_ref[0]
        dob = do_ref[...].astype(BF16)
        after, before = _tri2(True), _tri2(False)
        order = _key_order(bq, qb)
        nkb = (qb + 1) * per

        def sweep_left(i, run, diagonal):
            kb = nkb - 1 - i
            off = pl.multiple_of(kb * ATT_BLOCK, ATT_BLOCK)
            kv = k_ref[0, pl.ds(off, ATT_BLOCK), :]
            vv = v_ref[0, pl.ds(off, ATT_BLOCK), :]
            z = _dot_nt(qv, kv)
            lsz = _log_sigmoid(z)
            lsn = lsz - z
            if diagonal:
                past = order > kb * ATT_BLOCK
                lsn = jnp.where(past, lsn, 0.0)
            la, run = _run_sums(lsn, after, run, True)
            a = jnp.exp(lsz + la)
            if diagonal:
                a = jnp.where(past, a, 0.0)
            g_s[kb] = _dot_nt(dob, vv) * a
            ls_s[kb] = lsz
            dv_ref[0, pl.ds(off, ATT_BLOCK), :] += _dot_tn(a.astype(BF16), dob)
            return run

        zero = jnp.zeros((bq, 1), F32)
        run = lax.fori_loop(0, per, lambda i, c: sweep_left(i, c, True), zero)
        lax.fori_loop(per, nkb, lambda i, c: sweep_left(i, c, False), run)

        def sweep_right(kb, carry, diagonal):
            dq_acc, run_g = carry
            off = pl.multiple_of(kb * ATT_BLOCK, ATT_BLOCK)
            kv = k_ref[0, pl.ds(off, ATT_BLOCK), :]
            g = g_s[kb]
            sg = jnp.exp(ls_s[kb])
            dls, run_g = _run_sums(g, before, run_g, False)
            dz = g * (1.0 - sg) - dls * sg
            if diagonal:
                dz = jnp.where(order > kb * ATT_BLOCK, dz, 0.0)
            dzb = dz.astype(BF16)
            dk_ref[0, pl.ds(off, ATT_BLOCK), :] += _dot_tn(dzb, qv)
            return dq_acc + _dot(dzb, kv), run_g

        carry = lax.fori_loop(0, nkb - per, lambda i, c: sweep_right(i, c, False), (jnp.zeros((bq, HEAD), F32), zero))
        dq, _ = lax.fori_loop(nkb - per, nkb, lambda i, c: sweep_right(i, c, True), carry)
        dq_ref[0] = dq

    hm = _sds((nh, t, HEAD))
    full = pl.BlockSpec((1, t, HEAD), lambda h, i: (h, 0, 0))
    tok = pl.BlockSpec((bq, HEAD), lambda h, i: (i, h))
    nkb_max = t // ATT_BLOCK
    return pl.pallas_call(
        kern, name="sb_attn_bwd", grid=(nh, t // bq),
        in_specs=[pl.BlockSpec((1, bq, HEAD), lambda h, i: (h, i, 0)), full, full, tok],
        out_specs=[pl.BlockSpec((1, bq, HEAD), lambda h, i: (h, i, 0)), full, full],
        out_shape=[hm, hm, hm],
        scratch_shapes=[pltpu.VMEM((nkb_max, bq, ATT_BLOCK), F32), pltpu.VMEM((nkb_max, bq, ATT_BLOCK), F32)],
        compiler_params=_cparams(dimension_semantics=("parallel", "arbitrary")),
    )(q, k, v, do)


def _sb_fwd(x, h, w_qkv, gq, gk, w_out):
    qkv = _mm("sb_qkv", h, w_qkv)
    q, k, v = _sb_prep_fwd(qkv, gq, gk)
    o = _sb_attn_fwd(q, k, v)
    y = _mm("sb_out", o, w_out, add=x)
    return y, (h, qkv, q, k, v, o)


def _sb_bwd(saved, dy, w_qkv, gq, gk, w_out):
    h, qkv, q, k, v, o = saved
    d_wout = _mm("sb_out_wgrad", o, dy, ta=True, out_dtype=GRAD_DTYPE)
    do = _mm("sb_out_dgrad", dy, w_out, tb=True)
    dq, dk, dv = _sb_attn_bwd(q, k, v, do)
    dqkv, dgq, dgk = _sb_prep_bwd(qkv, gq, gk, dq, dk, dv)
    d_wqkv = _mm("sb_qkv_wgrad", h, dqkv, ta=True, out_dtype=GRAD_DTYPE)
    dh = _mm("sb_qkv_dgrad", dqkv, w_qkv, tb=True)
    return dh, d_wqkv, dgq, dgk, d_wout


DN_QKV = 3 * N_HEADS * HEAD
DN_PROJ = DN_QKV + N_HEADS * HEAD + LANES
DN_CONV = 4
HALO = 8
CONV_COLS = 512


def _dn_conv_fwd(proj, conv_w, tt=256):
    t = proj.shape[0]
    tt = min(tt, t)

    def body(u_ref, prev_ref, w_ref, c_ref):
        i = pl.program_id(0)
        for cc in range(DN_QKV // CONV_COLS):
            cs = slice(cc * CONV_COLS, (cc + 1) * CONV_COLS)
            cur = u_ref[:, cs]
            prev = jnp.where(i > 0, prev_ref[:, cs], 0.0)
            ext = jnp.concatenate([prev, cur], axis=0)
            y = cur * w_ref[DN_CONV - 1:DN_CONV, cs]
            for j in range(DN_CONV - 1):
                y = y + pltpu.roll(ext, DN_CONV - 1 - j, 0)[HALO:] * w_ref[j:j + 1, cs]
            c_ref[:, cs] = y

    return _rows("dn_conv", body,
                 [(proj, (tt, DN_QKV), lambda i: (i, 0)),
                  (proj, (HALO, DN_QKV), lambda i: (jnp.maximum(i * (tt // HALO) - 1, 0), 0))],
                 [_sds((t, DN_QKV))], tt=tt, consts=[conv_w])[0]


def _dn_conv_bwd(proj, conv_w, dc, dz, dab, tt=256):
    t = proj.shape[0]
    tt = min(tt, t)
    nblk = t // tt

    def body(u_ref, prev_ref, dc_ref, next_ref, dz_ref, dab_ref, w_ref, dp_ref, dw_ref):
        i = pl.program_id(0)
        dws = []
        for cc in range(DN_QKV // CONV_COLS):
            cs = slice(cc * CONV_COLS, (cc + 1) * CONV_COLS)
            cur = u_ref[:, cs]
            prev = jnp.where(i > 0, prev_ref[:, cs], 0.0)
            ext_u = jnp.concatenate([prev, cur], axis=0)
            d = dc_ref[:, cs]
            nxt = jnp.where(i < nblk - 1, next_ref[:, cs], 0.0)
            ext_d = jnp.concatenate([d, nxt], axis=0)
            du = d * w_ref[DN_CONV - 1:DN_CONV, cs]
            rows = [jnp.sum(d * cur, axis=0, keepdims=True)]
            for j in range(DN_CONV - 2, -1, -1):
                sh = DN_CONV - 1 - j
                du = du + pltpu.roll(ext_d, tt + HALO - sh, 0)[:tt] * w_ref[j:j + 1, cs]
                rows.insert(0, jnp.sum(d * pltpu.roll(ext_u, sh, 0)[HALO:], axis=0, keepdims=True))
            dp_ref[:, cs] = du.astype(BF16)
            dws.append(jnp.concatenate(rows, axis=0))
        dp_ref[:, DN_QKV:DN_QKV + N_HEADS * HEAD] = dz_ref[...].astype(BF16)
        dp_ref[:, DN_QKV + N_HEADS * HEAD:] = dab_ref[...].astype(BF16)
        _acc(dw_ref, jnp.concatenate(dws, axis=1))

    return _rows("dn_conv_bwd", body,
                 [(proj, (tt, DN_QKV), lambda i: (i, 0)),
                  (proj, (HALO, DN_QKV), lambda i: (jnp.maximum(i * (tt // HALO) - 1, 0), 0)),
                  dc,
                  (dc, (HALO, DN_QKV), lambda i: (jnp.minimum((i + 1) * (tt // HALO), t // HALO - 1), 0)),
                  dz, dab],
                 [_sds((t, DN_PROJ), BF16)], tt=tt, consts=[conv_w], accs=[_sds(conv_w.shape)])


def _l2n(x):
    return x * lax.rsqrt(jnp.sum(x * x, axis=-1, keepdims=True) + NORM_EPS)


def _dn_qkv(cq, ck, cv):
    return _l2n(_silu(cq)) * (HEAD ** -0.5), _l2n(_silu(ck)), _silu(cv)


def _dn_gates(ab, a_log, dt_bias):
    lane = lax.broadcasted_iota(jnp.int32, ab.shape, 1)
    g = -jnp.exp(a_log) * _softplus(ab + dt_bias)
    return jnp.where(lane < N_HEADS, g, jnp.where(lane < 2 * N_HEADS, _sigmoid(ab), 0.0))


def _ab_spec(tt):
    return (tt, LANES), lambda i: (i, DN_PROJ // LANES - 1)


def _dn_prep_fwd(c, proj, a_log, dt_bias, tt=256):
    t = c.shape[0]
    tt = min(tt, t)

    def body(c_ref, ab_ref, al_ref, dt_ref, q_ref, k_ref, v_ref, g_ref):
        for h in range(N_HEADS):
            q_ref[h], k_ref[h], v_ref[h] = _dn_qkv(_heads_in(c_ref, h), _heads_in(c_ref, N_HEADS + h), _heads_in(c_ref, 2 * N_HEADS + h))
        g_ref[...] = _dn_gates(ab_ref[...], al_ref[...], dt_ref[...])

    hm = _sds((N_HEADS, t, HEAD))
    return _rows("dn_prep", body, [c, (proj,) + _ab_spec(tt)], [hm, hm, hm, _sds((t, LANES))], tt=tt, consts=[a_log, dt_bias])


def _dn_prep_bwd(c, proj, a_log, dt_bias, dq, dk, dv, dgates, tt=256):
    t = c.shape[0]
    tt = min(tt, t)

    def body(c_ref, ab_ref, dq_ref, dk_ref, dv_ref, dg_ref, al_ref, dt_ref, dc_ref, dab_ref, dal_ref, ddt_ref):
        for h in range(N_HEADS):
            _, vjp = jax.vjp(_dn_qkv, _heads_in(c_ref, h), _heads_in(c_ref, N_HEADS + h), _heads_in(c_ref, 2 * N_HEADS + h))
            a, b, d = vjp((dq_ref[h], dk_ref[h], dv_ref[h]))
            dc_ref[:, h * HEAD:(h + 1) * HEAD] = a
            dc_ref[:, (N_HEADS + h) * HEAD:(N_HEADS + h + 1) * HEAD] = b
            dc_ref[:, (2 * N_HEADS + h) * HEAD:(2 * N_HEADS + h + 1) * HEAD] = d
        _, vjp = jax.vjp(_dn_gates, ab_ref[...], al_ref[...], dt_ref[...])
        dab, dal, ddt = vjp(dg_ref[...])
        dab_ref[...] = dab
        _acc(dal_ref, dal)
        _acc(ddt_ref, ddt)

    return _rows("dn_prep_bwd", body, [c, (proj,) + _ab_spec(tt), dq, dk, dv, dgates], [_sds(c.shape), _sds((t, LANES))],
                 tt=tt, consts=[a_log, dt_bias], accs=[_sds(a_log.shape), _sds(dt_bias.shape)])


def _bdot(a, b, prec=None):
    return lax.dot_general(a, b, (((2,), (1,)), ((0,), (0,))), precision=prec, preferred_element_type=F32)


def _bdot_nt(a, b, prec=None):
    return lax.dot_general(a, b, (((2,), (2,)), ((0,), (0,))), precision=prec, preferred_element_type=F32)


def _bdot_tn(a, b, prec=None):
    return lax.dot_general(a, b, (((1,), (1,)), ((0,), (0,))), precision=prec, preferred_element_type=F32)


def _inv_raw(low):
    c = low.shape[-1]
    r = lax.broadcasted_iota(jnp.int32, (c, c), 0)
    s = lax.broadcasted_iota(jnp.int32, (c, c), 1)
    m = jnp.where(r == s, 1.0, 0.0) - low
    p = _bdot(low, low, F32X3)
    n_fac = int(math.log2(c)) - 1
    for i in range(n_fac):
        m = m + _bdot(m, p, F32X3)
        if i < n_fac - 1:
            p = _bdot(p, p, F32X3)
    return m


@jax.custom_vjp
def _inv_unit_lower(low):
    return _inv_raw(low)


def _inv_fwd(low):
    m = _inv_raw(low)
    return m, m


def _inv_bwd(m, dm):
    return (-_bdot_nt(_bdot_tn(m, dm, F32X3), m, F32X3),)


_inv_unit_lower.defvjp(_inv_fwd, _inv_bwd)


def _dn_chunk(q, k, v, gates, s):
    nh, rows, _ = q.shape
    c = DN_CHUNK
    nc = rows // c
    nb = nh * nc
    lane = lax.broadcasted_iota(jnp.int32, gates.shape, 1)

    def column(j):
        return jnp.sum(jnp.where(lane == j, gates, 0.0), axis=1, keepdims=True)[None]

    def fold(x):
        return x.reshape((nb, c) + x.shape[2:])

    g_col = fold(jnp.concatenate([column(h) for h in range(nh)], axis=0))
    b_col = fold(jnp.concatenate([column(h + nh) for h in range(nh)], axis=0))
    q, k, v = fold(q), fold(k), fold(v)
    r = lax.broadcasted_iota(jnp.int32, (c, c), 0)
    cc = lax.broadcasted_iota(jnp.int32, (c, c), 1)
    causal, strict = r >= cc, r > cc
    incl = jnp.broadcast_to(jnp.where(causal, 1.0, 0.0), (nb, c, c))
    upper = jnp.broadcast_to(jnp.where(r <= cc, 1.0, 0.0), (nb, c, c))
    gb = jnp.broadcast_to(g_col, (nb, c, LANES))
    gbc = jnp.broadcast_to(g_col, (nb, c, c))
    gc = _bdot(incl, gb, F32X3)
    gc_r = _bdot(incl, gbc, F32X3)
    gc_c = _bdot_tn(gbc, upper, F32X3)
    decay = jnp.where(causal, jnp.exp(jnp.where(causal, gc_r - gc_c, 0.0)), 0.0)
    kb = k * b_col
    low = jnp.where(strict, _bdot_nt(kb, k) * decay, 0.0)
    m = _inv_unit_lower(low)
    egc = jnp.exp(gc)
    gl = jnp.sum(gb, axis=1, keepdims=True)
    local = (_bdot(m, v * b_col), _bdot(m, kb * egc), _bdot_nt(q, k) * decay, q * egc,
             k * jnp.exp(gl - gc), jnp.exp(gl))
    outs = []
    for i in range(nc):
        u, w, attn, q_dec, k_dec, cd = (x.reshape((nh, nc) + x.shape[1:])[:, i] for x in local)
        v_new = u - _bdot(w, s)
        outs.append(_bdot(q_dec, s) + _bdot(attn, v_new))
        s = s * cd + _bdot_tn(k_dec, v_new)
    return jnp.concatenate(outs, axis=1), s


def _dn_chunks_fwd(q, k, v, gates):
    nh, t, _ = q.shape
    n = t // DN_STEP

    def kern(q_ref, k_ref, v_ref, g_ref, o_ref, sin_ref, s_scr):
        @pl.when(pl.program_id(0) == 0)
        def _():
            s_scr[...] = jnp.zeros(s_scr.shape, F32)

        s = s_scr[...]
        sin_ref[0] = s
        o_ref[...], s_scr[...] = _dn_chunk(q_ref[...], k_ref[...], v_ref[...], g_ref[...], s)

    blk = pl.BlockSpec((nh, DN_STEP, HEAD), lambda i: (0, i, 0))
    return pl.pallas_call(
        kern, name="dn_chunks_fwd", grid=(n,),
        in_specs=[blk, blk, blk, pl.BlockSpec((DN_STEP, LANES), lambda i: (i, 0))],
        out_specs=[blk, pl.BlockSpec((1, nh, HEAD, HEAD), lambda i: (i, 0, 0, 0))],
        out_shape=[_sds((nh, t, HEAD)), _sds((n, nh, HEAD, HEAD))],
        scratch_shapes=[pltpu.VMEM((nh, HEAD, HEAD), F32)],
        compiler_params=_cparams(dimension_semantics=("arbitrary",)),
    )(q, k, v, gates)


def _dn_chunks_bwd(q, k, v, gates, s_in, do):
    nh, t, _ = q.shape
    n = t // DN_STEP

    def kern(q_ref, k_ref, v_ref, g_ref, sin_ref, do_ref, dq_ref, dk_ref, dv_ref, dg_ref, ds_scr):
        @pl.when(pl.program_id(0) == 0)
        def _():
            ds_scr[...] = jnp.zeros(ds_scr.shape, F32)

        _, vjp = jax.vjp(_dn_chunk, q_ref[...], k_ref[...], v_ref[...], g_ref[...], sin_ref[0])
        dq_ref[...], dk_ref[...], dv_ref[...], dg_ref[...], ds_scr[...] = vjp((do_ref[...], ds_scr[...]))

    blk = pl.BlockSpec((nh, DN_STEP, HEAD), lambda i: (0, n - 1 - i, 0))
    gblk = pl.BlockSpec((DN_STEP, LANES), lambda i: (n - 1 - i, 0))
    hm = _sds((nh, t, HEAD))
    return pl.pallas_call(
        kern, name="dn_chunks_bwd", grid=(n,),
        in_specs=[blk, blk, blk, gblk, pl.BlockSpec((1, nh, HEAD, HEAD), lambda i: (n - 1 - i, 0, 0, 0)), blk],
        out_specs=[blk, blk, blk, gblk],
        out_shape=[hm, hm, hm, _sds((t, LANES))],
        scratch_shapes=[pltpu.VMEM((nh, HEAD, HEAD), F32)],
        compiler_params=_cparams(dimension_semantics=("arbitrary",)),
    )(q, k, v, gates, s_in, do)


def _dn_gate_out(o, z, g):
    return _rms(o, g) * _silu(z)


def _z_spec(tt):
    return (tt, N_HEADS * HEAD), lambda i: (i, DN_QKV // (N_HEADS * HEAD))


def _dn_post_fwd(o, proj, out_norm, tt=256):
    t = o.shape[1]
    tt = min(tt, t)

    def body(o_ref, z_ref, g_ref, y_ref):
        for h in range(N_HEADS):
            y_ref[:, h * HEAD:(h + 1) * HEAD] = _dn_gate_out(o_ref[h], _heads_in(z_ref, h), g_ref[...]).astype(BF16)

    return _rows("dn_post", body, [o, (proj,) + _z_spec(tt)], [_sds((t, N_HEADS * HEAD), BF16)], tt=tt, consts=[out_norm])[0]


def _dn_post_bwd(o, proj, out_norm, dy, tt=256):
    t = o.shape[1]
    tt = min(tt, t)

    def body(o_ref, z_ref, dy_ref, g_ref, do_ref, dz_ref, dg_ref):
        dg = jnp.zeros(g_ref.shape, F32)
        for h in range(N_HEADS):
            _, vjp = jax.vjp(_dn_gate_out, o_ref[h], _heads_in(z_ref, h), g_ref[...])
            a, b, d = vjp(_heads_in(dy_ref, h))
            do_ref[h] = a
            dz_ref[:, h * HEAD:(h + 1) * HEAD] = b
            dg = dg + d
        _acc(dg_ref, dg)

    return _rows("dn_post_bwd", body, [o, (proj,) + _z_spec(tt), dy], [_sds(o.shape), _sds((t, N_HEADS * HEAD))], tt=tt,
                 consts=[out_norm], accs=[_sds(out_norm.shape)])


def _dn_fwd(x, h, w_in, conv_w, a_log, dt_bias, out_norm, w_out):
    proj = _mm("dn_in", h, w_in)
    c = _dn_conv_fwd(proj, conv_w)
    q, k, v, gates = _dn_prep_fwd(c, proj, a_log, dt_bias)
    o, s_in = _dn_chunks_fwd(q, k, v, gates)
    on = _dn_post_fwd(o, proj, out_norm)
    y = _mm("dn_out", on, w_out, add=x)
    return y, (h, proj, c, q, k, v, gates, o, s_in, on)


def _dn_bwd(saved, dy, w_in, conv_w, a_log, dt_bias, out_norm, w_out):
    h, proj, c, q, k, v, gates, o, s_in, on = saved
    d_wout = _mm("dn_out_wgrad", on, dy, ta=True, out_dtype=GRAD_DTYPE)
    don = _mm("dn_out_dgrad", dy, w_out, tb=True)
    do, dz, d_out_norm = _dn_post_bwd(o, proj, out_norm, don)
    dq, dk, dv, dgates = _dn_chunks_bwd(q, k, v, gates, s_in, do)
    dc, dab, d_a_log, d_dt_bias = _dn_prep_bwd(c, proj, a_log, dt_bias, dq, dk, dv, dgates)
    dproj, d_conv_w = _dn_conv_bwd(proj, conv_w, dc, dz, dab)
    d_win = _mm("dn_in_wgrad", h, dproj, ta=True, out_dtype=GRAD_DTYPE)
    dh = _mm("dn_in_dgrad", dproj, w_in, tb=True)
    return dh, d_win, d_conv_w, d_a_log, d_dt_bias, d_out_norm, d_wout


MLA_SCALE = MLA_QK ** -0.5
MLA_C = 512


def _swap_raw(x):
    lane = lax.broadcasted_iota(jnp.int32, x.shape, 1)
    half = MLA_ROPE // 2
    y = jnp.where(lane < half, pltpu.roll(x, LANES - half, 1), pltpu.roll(x, half, 1))
    return jnp.where(lane < MLA_ROPE, y, 0.0)


@jax.custom_vjp
def _swap_halves(x):
    return _swap_raw(x)


_swap_halves.defvjp(lambda x: (_swap_raw(x), None), lambda _, d: (_swap_raw(d),))


def _rms_rope(x, g, cos, sin):
    y = x * lax.rsqrt(jnp.sum(x * x, axis=-1, keepdims=True) * (1.0 / MLA_ROPE) + NORM_EPS) * g
    return y * cos + _swap_halves(y) * sin


def _mla_latent(cq, ckv, kr, gq, gkv, gkr, cos, sin):
    return _rms(cq, gq), _rms(ckv, gkv), _rms_rope(kr, gkr, cos, sin)


def _mla_prep1_fwd(c, gq, gkv, gkr, cos, sin):
    t = c.shape[0]

    def body(c_ref, cos_ref, sin_ref, gq_ref, gkv_ref, gkr_ref, cq_ref, ckv_ref, kr_ref):
        a, b, r = _mla_latent(c_ref[:, :256], c_ref[:, 256:384], c_ref[:, 384:], gq_ref[...], gkv_ref[...], gkr_ref[...],
                              cos_ref[...], sin_ref[...])
        cq_ref[...] = a.astype(BF16)
        ckv_ref[...] = b.astype(BF16)
        kr_ref[...] = r.astype(BF16)

    return _rows("mla_prep1", body, [c, cos, sin], [_sds((t, 256), BF16), _sds((t, HEAD), BF16), _sds((t, HEAD), BF16)],
                 tt=512, consts=[gq, gkv, gkr])


def _mla_prep1_bwd(c, gq, gkv, gkr, cos, sin, dcq, dckv, dkr_heads):
    def body(c_ref, cos_ref, sin_ref, dcq_ref, dckv_ref, dkr_ref, gq_ref, gkv_ref, gkr_ref, dc_ref, dgq_ref, dgkv_ref, dgkr_ref):
        dkr = dkr_ref[0]
        for h in range(1, N_HEADS):
            dkr = dkr + dkr_ref[h]
        _, vjp = jax.vjp(_mla_latent, c_ref[:, :256], c_ref[:, 256:384], c_ref[:, 384:], gq_ref[...], gkv_ref[...], gkr_ref[...],
                         cos_ref[...], sin_ref[...])
        a, b, r, d1, d2, d3, _, _ = vjp((dcq_ref[...], dckv_ref[...], dkr))
        dc_ref[:, :256] = a.astype(BF16)
        dc_ref[:, 256:384] = b.astype(BF16)
        dc_ref[:, 384:] = r.astype(BF16)
        _acc(dgq_ref, d1)
        _acc(dgkv_ref, d2)
        _acc(dgkr_ref, d3)

    return _rows("mla_prep1_bwd", body, [c, cos, sin, dcq, dckv, dkr_heads], [_sds(c.shape, BF16)], tt=512,
                 consts=[gq, gkv, gkr], accs=[_sds(gq.shape), _sds(gkv.shape), _sds(gkr.shape)])


def _mla_heads(qn, qr, kn, gqn, gqr, gkn, cos, sin):
    return _rms(qn, gqn) * MLA_SCALE, _rms_rope(qr, gqr, cos, sin) * MLA_SCALE, _rms(kn, gkn)


def _mla_prep2_fwd(qa, kv, gqn, gqr, gkn, cos, sin):
    t = qa.shape[0]

    def body(qa_ref, kv_ref, cos_ref, sin_ref, gqn_ref, gqr_ref, gkn_ref, qn_ref, qr_ref, kn_ref, v_ref):
        for h in range(N_HEADS):
            a, b, c = _mla_heads(_heads_in(qa_ref, h), _heads_in(qa_ref, N_HEADS + h), _heads_in(kv_ref, h),
                                 gqn_ref[...], gqr_ref[...], gkn_ref[...], cos_ref[...], sin_ref[...])
            qn_ref[h] = a.astype(BF16)
            qr_ref[h] = b.astype(BF16)
            kn_ref[h] = c.astype(BF16)
            v_ref[h] = _heads_in(kv_ref, N_HEADS + h).astype(BF16)

    hm = _sds((N_HEADS, t, HEAD), BF16)
    return _rows("mla_prep2", body, [qa, kv, cos, sin], [hm, hm, hm, hm], tt=256, consts=[gqn, gqr, gkn])


def _mla_prep2_bwd(qa, kv, gqn, gqr, gkn, cos, sin, dqn, dqr, dkn, dv):
    def body(qa_ref, kv_ref, cos_ref, sin_ref, dqn_ref, dqr_ref, dkn_ref, dv_ref, gqn_ref, gqr_ref, gkn_ref,
             dqa_ref, dkv_ref, d1_ref, d2_ref, d3_ref):
        d1 = jnp.zeros(gqn_ref.shape, F32)
        d2 = jnp.zeros(gqr_ref.shape, F32)
        d3 = jnp.zeros(gkn_ref.shape, F32)
        for h in range(N_HEADS):
            _, vjp = jax.vjp(_mla_heads, _heads_in(qa_ref, h), _heads_in(qa_ref, N_HEADS + h), _heads_in(kv_ref, h),
                             gqn_ref[...], gqr_ref[...], gkn_ref[...], cos_ref[...], sin_ref[...])
            a, b, c, e1, e2, e3, _, _ = vjp((dqn_ref[h], dqr_ref[h], dkn_ref[h]))
            dqa_ref[:, h * HEAD:(h + 1) * HEAD] = a.astype(BF16)
            dqa_ref[:, (N_HEADS + h) * HEAD:(N_HEADS + h + 1) * HEAD] = b.astype(BF16)
            dkv_ref[:, h * HEAD:(h + 1) * HEAD] = c.astype(BF16)
            dkv_ref[:, (N_HEADS + h) * HEAD:(N_HEADS + h + 1) * HEAD] = dv_ref[h].astype(BF16)
            d1, d2, d3 = d1 + e1, d2 + e2, d3 + e3
        _acc(d1_ref, d1)
        _acc(d2_ref, d2)
        _acc(d3_ref, d3)

    return _rows("mla_prep2_bwd", body, [qa, kv, cos, sin, dqn, dqr, dkn, dv], [_sds(qa.shape, BF16), _sds(kv.shape, BF16)],
                 tt=256, consts=[gqn, gqr, gkn], accs=[_sds(gqn.shape), _sds(gqr.shape), _sds(gkn.shape)])


def _mla_attn_fwd(qn, qr, kn, kr, v):
    nh, t, _ = qn.shape
    bq = _q_block(t)
    per = bq // ATT_BLOCK

    def kern(qn_ref, qr_ref, kn_ref, kr_ref, v_ref, o_ref, lse_ref):
        qb = pl.program_id(1)
        qv = jnp.concatenate([qn_ref[0], qr_ref[0]], axis=1)
        order = _key_order(bq, qb)

        def body(kb, carry, diagonal):
            acc, m, l = carry
            off = pl.multiple_of(kb * ATT_BLOCK, ATT_BLOCK)
            kv = jnp.concatenate([kn_ref[0, pl.ds(off, ATT_BLOCK), :], kr_ref[pl.ds(off, ATT_BLOCK), :]], axis=1)
            s = _dot_nt(qv, kv)
            if diagonal:
                s = jnp.where(order >= kb * ATT_BLOCK, s, -jnp.inf)
            m_new = jnp.maximum(m, jnp.max(s, axis=1, keepdims=True))
            alpha = jnp.exp(m - m_new)
            p = jnp.exp(s - m_new)
            acc = acc * alpha + _dot(p.astype(BF16), v_ref[0, pl.ds(off, ATT_BLOCK), :])
            return acc, m_new, l * alpha + jnp.sum(p, axis=1, keepdims=True)

        init = (jnp.zeros((bq, HEAD), F32), jnp.full((bq, 1), -jnp.inf, F32), jnp.zeros((bq, 1), F32))
        carry = lax.fori_loop(0, qb * per, lambda i, c: body(i, c, False), init)
        acc, m, l = lax.fori_loop(qb * per, (qb + 1) * per, lambda i, c: body(i, c, True), carry)
        o_ref[...] = acc / l
        lse_ref[...] = jnp.broadcast_to(m + jnp.log(l), (bq, HEAD))

    blk = pl.BlockSpec((1, bq, HEAD), lambda h, i: (h, i, 0))
    full = pl.BlockSpec((1, t, HEAD), lambda h, i: (h, 0, 0))
    tok = pl.BlockSpec((bq, HEAD), lambda h, i: (i, h))
    return pl.pallas_call(
        kern, name="mla_attn_fwd", grid=(nh, t // bq),
        in_specs=[blk, blk, full, pl.BlockSpec((t, HEAD), lambda h, i: (0, 0)), full],
        out_specs=[tok, tok], out_shape=[_sds((t, nh * HEAD)), _sds((t, nh * HEAD))],
        compiler_params=_cparams(dimension_semantics=("parallel", "arbitrary")),
    )(qn, qr, kn, kr, v)


def _mla_attn_bwd(qn, qr, kn, kr, v, o, lse, do):
    nh, t, _ = qn.shape
    bq = _q_block(t)
    per = bq // ATT_BLOCK

    def kern(qn_ref, qr_ref, kn_ref, kr_ref, v_ref, o_ref, lse_ref, do_ref, dqn_ref, dqr_ref, dkn_ref, dkr_ref, dv_ref):
        qb = pl.program_id(1)

        @pl.when(qb == 0)
        def _():
            dkn_ref[...] = jnp.zeros(dkn_ref.shape, F32)
            dkr_ref[...] = jnp.zeros(dkr_ref.shape, F32)
            dv_ref[...] = jnp.zeros(dv_ref.shape, F32)

        qv = jnp.concatenate([qn_ref[0], qr_ref[0]], axis=1)
        dov = do_ref[...]
        dob = dov.astype(BF16)
        delta = jnp.sum(dov * o_ref[...], axis=1, keepdims=True)
        lse_col = lse_ref[:, :1]
        order = _key_order(bq, qb)

        def body(kb, dq, diagonal):
            off = pl.multiple_of(kb * ATT_BLOCK, ATT_BLOCK)
            kv = jnp.concatenate([kn_ref[0, pl.ds(off, ATT_BLOCK), :], kr_ref[pl.ds(off, ATT_BLOCK), :]], axis=1)
            vv = v_ref[0, pl.ds(off, ATT_BLOCK), :]
            p = jnp.exp(_dot_nt(qv, kv) - lse_col)
            if diagonal:
                p = jnp.where(order >= kb * ATT_BLOCK, p, 0.0)
            ds = (p * (_dot_nt(dob, vv) - delta)).astype(BF16)
            dk = _dot_tn(ds, qv)
            dkn_ref[0, pl.ds(off, ATT_BLOCK), :] += dk[:, :HEAD]
            dkr_ref[0, pl.ds(off, ATT_BLOCK), :] += dk[:, HEAD:]
            dv_ref[0, pl.ds(off, ATT_BLOCK), :] += _dot_tn(p.astype(BF16), dob)
            return dq + _dot(ds, kv)

        dq = lax.fori_loop(0, qb * per, lambda i, c: body(i, c, False), jnp.zeros((bq, 2 * HEAD), F32))
        dq = lax.fori_loop(qb * per, (qb + 1) * per, lambda i, c: body(i, c, True), dq)
        dqn_ref[0] = dq[:, :HEAD]
        dqr_ref[0] = dq[:, HEAD:]

    hm = _sds((nh, t, HEAD))
    blk = pl.BlockSpec((1, bq, HEAD), lambda h, i: (h, i, 0))
    full = pl.BlockSpec((1, t, HEAD), lambda h, i: (h, 0, 0))
    tok = pl.BlockSpec((bq, HEAD), lambda h, i: (i, h))
    return pl.pallas_call(
        kern, name="mla_attn_bwd", grid=(nh, t // bq),
        in_specs=[blk, blk, full, pl.BlockSpec((t, HEAD), lambda h, i: (0, 0)), full, tok, tok, tok],
        out_specs=[blk, blk, full, full, full], out_shape=[hm, hm, hm, hm, hm],
        compiler_params=_cparams(dimension_semantics=("parallel", "arbitrary")),
    )(qn, qr, kn, kr, v, o, lse, do)


def _rope_tables(t):
    inv_freq = ROPE_THETA ** (-jnp.arange(0, MLA_ROPE, 2, dtype=F32) / MLA_ROPE)
    ang = jnp.arange(t, dtype=F32)[:, None] * inv_freq[None, :]
    c, s = jnp.cos(ang), jnp.sin(ang)
    pad = ((0, 0), (0, LANES - MLA_ROPE))
    return jnp.pad(jnp.concatenate([c, c], axis=1), pad), jnp.pad(jnp.concatenate([-s, s], axis=1), pad)


def _pad_lanes(v, n=LANES):
    return jnp.pad(v, (0, n - v.shape[0])).reshape(1, n)


def _mla_layout(w_down, w_uq, w_ukv):
    w_down_p = jnp.pad(w_down, ((0, 0), (0, MLA_C - w_down.shape[1])))
    uq = w_uq.reshape(w_uq.shape[0], N_HEADS, MLA_QK)
    rope = jnp.pad(uq[:, :, HEAD:], ((0, 0), (0, 0), (0, LANES - MLA_ROPE)))
    w_uq_p = jnp.concatenate([uq[:, :, :HEAD].reshape(-1, N_HEADS * HEAD), rope.reshape(-1, N_HEADS * LANES)], axis=1)
    ukv = w_ukv.reshape(w_ukv.shape[0], N_HEADS, 2 * HEAD)
    w_ukv_p = jnp.concatenate([ukv[:, :, :HEAD].reshape(-1, N_HEADS * HEAD), ukv[:, :, HEAD:].reshape(-1, N_HEADS * HEAD)], axis=1)
    return w_down_p, w_uq_p, w_ukv_p


def _mla_unlayout(d_down_p, d_uq_p, d_ukv_p):
    d_down = d_down_p[:, :256 + HEAD + MLA_ROPE]
    nope = d_uq_p[:, :N_HEADS * HEAD].reshape(-1, N_HEADS, HEAD)
    rope = d_uq_p[:, N_HEADS * HEAD:].reshape(-1, N_HEADS, LANES)[:, :, :MLA_ROPE]
    d_uq = jnp.concatenate([nope, rope], axis=2).reshape(-1, N_HEADS * MLA_QK)
    kn = d_ukv_p[:, :N_HEADS * HEAD].reshape(-1, N_HEADS, HEAD)
    vv = d_ukv_p[:, N_HEADS * HEAD:].reshape(-1, N_HEADS, HEAD)
    d_ukv = jnp.concatenate([kn, vv], axis=2).reshape(-1, N_HEADS * 2 * HEAD)
    return d_down, d_uq, d_ukv


def _mla_weight_shapes():
    return (_sds((1024, MLA_C), BF16), _sds((1, 256)), _sds((1, HEAD)), _sds((256, 2048), BF16), _sds((HEAD, 2048), BF16),
            _sds((1, HEAD)), _sds((1, HEAD)), _sds((1, HEAD)), _sds((1, HEAD)), _sds((1024, 1024), BF16),
            _sds((4096, HEAD)), _sds((4096, HEAD)))


def _mla_fwd(x, h, w_down, gq, gkv, w_uq, w_ukv, gqn, gqr, gkn, gkr, w_out, cos, sin):
    c = _mm("mla_down", h, w_down)
    cq, ckv, kr = _mla_prep1_fwd(c, gq, gkv, gkr, cos, sin)
    qa = _mm("mla_uq", cq, w_uq)
    kv = _mm("mla_ukv", ckv, w_ukv)
    qn, qr, kn, v = _mla_prep2_fwd(qa, kv, gqn, gqr, gkn, cos, sin)
    o, lse = _mla_attn_fwd(qn, qr, kn, kr, v)
    y = _mm("mla_out", o, w_out, add=x)
    return y, (h, c, cq, ckv, kr, qa, kv, qn, qr, kn, v, o, lse)


def _mla_bwd(saved, dy, w_down, gq, gkv, w_uq, w_ukv, gqn, gqr, gkn, gkr, w_out, cos, sin):
    h, c, cq, ckv, kr, qa, kv, qn, qr, kn, v, o, lse = saved
    d_wout = _mm("mla_out_wgrad", o, dy, ta=True, out_dtype=GRAD_DTYPE)
    do = _mm("mla_out_dgrad", dy, w_out, tb=True)
    dqn, dqr, dkn, dkr, dv = _mla_attn_bwd(qn, qr, kn, kr, v, o, lse, do)
    dqa, dkv, dgqn, dgqr, dgkn = _mla_prep2_bwd(qa, kv, gqn, gqr, gkn, cos, sin, dqn, dqr, dkn, dv)
    d_wuq = _mm("mla_uq_wgrad", cq, dqa, ta=True, out_dtype=GRAD_DTYPE)
    d_wukv = _mm("mla_ukv_wgrad", ckv, dkv, ta=True, out_dtype=GRAD_DTYPE)
    dcq = _mm("mla_uq_dgrad", dqa, w_uq, tb=True)
    dckv = _mm("mla_ukv_dgrad", dkv, w_ukv, tb=True)
    dc, dgq, dgkv, dgkr = _mla_prep1_bwd(c, gq, gkv, gkr, cos, sin, dcq, dckv, dkr)
    d_wdown = _mm("mla_down_wgrad", h, dc, ta=True, out_dtype=GRAD_DTYPE)
    dh = _mm("mla_down_dgrad", dc, w_down, tb=True)
    return dh, d_wdown, dgq, dgkv, d_wuq, d_wukv, dgqn, dgqr, dgkn, dgkr, d_wout


def _loss_head(y, target):
    d = y.shape[1]

    def body(y_ref, t_ref, dy_ref, l_ref):
        err = y_ref[...] - t_ref[...]
        dy_ref[...] = err * (1.0 / d)
        part = 0.5 * jnp.sum(jnp.sum(err * err, axis=1, keepdims=True) * (1.0 / d), axis=0, keepdims=True)
        _acc(l_ref, jnp.broadcast_to(part, (1, LANES)))

    return _rows("loss_head", body, [y, target], [_sds(y.shape)], tt=512, accs=[_sds((1, LANES))])


MESH_ID = pl.DeviceIdType.MESH
HBM_SPEC = pl.BlockSpec(memory_space=pltpu.HBM)


def _all_gather(name, x):
    m_per, n = x.shape

    def body(x_ref, out_ref, send_sems, recv_sems, local_sem):
        x, y, c = lax.axis_index("x"), lax.axis_index("y"), lax.axis_index("c")
        me, sibling = (x, y, c), (x, y, 1 - c)
        chips = [(1 - x, y), (x, 1 - y), (1 - x, 1 - y)]

        def rows(px, py, pc):
            return out_ref.at[pl.ds((4 * px + 2 * py + pc) * m_per, m_per), :]

        def copy(k, block, to, src=None):
            return pltpu.make_async_remote_copy(
                src_ref=rows(*block) if src is None else src, dst_ref=rows(*block),
                send_sem=send_sems.at[k], recv_sem=recv_sems.at[k], device_id=to, device_id_type=MESH_ID)

        mine = pltpu.make_async_copy(x_ref, rows(*me), local_sem)
        mine.start()
        first = [copy(0, me, sibling, src=x_ref)]
        first += [copy(1 + j, me, (*chip, c), src=x_ref) for j, chip in enumerate(chips)]
        for cp in first:
            cp.start()
        passed = [copy(4 + j, (*chip, c), sibling) for j, chip in enumerate(chips)]
        for j, chip in enumerate(chips):
            copy(1 + j, (*chip, c), me).wait_recv()
            passed[j].start()
        copy(0, sibling, me).wait_recv()
        for j, chip in enumerate(chips):
            copy(4 + j, (*chip, 1 - c), me).wait_recv()
        for cp in first + passed:
            cp.wait_send()
        mine.wait()

    return pl.pallas_call(
        body, name=name,
        out_shape=jax.ShapeDtypeStruct((N_DEV * m_per, n), x.dtype),
        in_specs=[HBM_SPEC], out_specs=HBM_SPEC,
        scratch_shapes=[pltpu.SemaphoreType.DMA((7,)), pltpu.SemaphoreType.DMA((7,)), pltpu.SemaphoreType.DMA],
    )(x)


def _all_gather_groups(name, xs):
    ng = len(xs)

    def body(*refs):
        x_refs, out_refs, token = refs[:ng], refs[ng:2 * ng], refs[2 * ng]
        send_sems, recv_sems, local_sems = refs[2 * ng + 1:]
        token[...] = jnp.zeros(token.shape, F32)
        x, y, c = lax.axis_index("x"), lax.axis_index("y"), lax.axis_index("c")
        me, sibling = (x, y, c), (x, y, 1 - c)
        chips = [(1 - x, y), (x, 1 - y), (1 - x, 1 - y)]

        def copy(g, k, block, to, src=None):
            px, py, pc = block
            dst = out_refs[g].at[4 * px + 2 * py + pc]
            return pltpu.make_async_remote_copy(
                src_ref=dst if src is None else src, dst_ref=dst,
                send_sem=send_sems.at[g, k], recv_sem=recv_sems.at[g, k], device_id=to, device_id_type=MESH_ID)

        mine = [pltpu.make_async_copy(x_refs[g], out_refs[g].at[4 * x + 2 * y + c], local_sems.at[g]) for g in range(ng)]
        for cp in mine:
            cp.start()
        first = []
        for g in range(ng):
            first.append(copy(g, 0, me, sibling, src=x_refs[g]))
            first += [copy(g, 1 + j, me, (*chip, c), src=x_refs[g]) for j, chip in enumerate(chips)]
        for cp in first:
            cp.start()
        passed = []
        for j, chip in enumerate(chips):
            for g in range(ng):
                copy(g, 1 + j, (*chip, c), me).wait_recv()
                passed.append(copy(g, 4 + j, (*chip, c), sibling))
                passed[-1].start()
        for g in range(ng):
            copy(g, 0, sibling, me).wait_recv()
            for j, chip in enumerate(chips):
                copy(g, 4 + j, (*chip, 1 - c), me).wait_recv()
        for cp in first + passed:
            cp.wait_send()
        for cp in mine:
            cp.wait()

    return pl.pallas_call(
        body, name=name,
        out_shape=[jax.ShapeDtypeStruct((N_DEV,) + x.shape, x.dtype) for x in xs] + [_sds((8, LANES))],
        in_specs=[HBM_SPEC] * ng, out_specs=[HBM_SPEC] * ng + [pl.BlockSpec(memory_space=pltpu.VMEM)],
        scratch_shapes=[pltpu.SemaphoreType.DMA((ng, 7)), pltpu.SemaphoreType.DMA((ng, 7)), pltpu.SemaphoreType.DMA((ng,))],
    )(*xs)


EFFECT = pltpu.SideEffectType.DATAFLOW_SIDE_EFFECTING
SEM_SPEC = pl.BlockSpec(memory_space=pltpu.SEMAPHORE)


def _push_copies(src_refs, land_refs, send_sems, recv_sems, chunked):
    x, y, c = lax.axis_index("x"), lax.axis_index("y"), lax.axis_index("c")
    me = 4 * x + 2 * y + c
    copies = []
    for g, (src, land) in enumerate(zip(src_refs, land_refs)):
        for k in range(1, N_DEV):
            px = 1 - x if k & 4 else x
            py = 1 - y if k & 2 else y
            pc = 1 - c if k & 1 else c
            copies.append(pltpu.make_async_remote_copy(
                src_ref=src.at[4 * px + 2 * py + pc] if chunked else src, dst_ref=land.at[me],
                send_sem=send_sems.at[g * (N_DEV - 1) + k - 1], recv_sem=recv_sems.at[g * (N_DEV - 1) + k - 1],
                device_id=(px, py, pc), device_id_type=MESH_ID))
    return copies


def _hbm(x):
    return pltpu.with_memory_space_constraint(x, pltpu.HBM)


def _push_start(name, srcs, lands, chunked):
    ng = len(srcs)

    def body(*refs):
        for cp in _push_copies(refs[:ng], refs[ng:2 * ng], refs[2 * ng], refs[2 * ng + 1], chunked):
            cp.start()
        refs[-1][...] = jnp.zeros(refs[-1].shape, F32)

    bufs = list(srcs) + list(lands)
    outs = pl.pallas_call(
        body, name=name,
        out_shape=(pltpu.SemaphoreType.DMA((ng * (N_DEV - 1),)), pltpu.SemaphoreType.DMA((ng * (N_DEV - 1),)),
                   *[pltpu.HBM(b.shape, b.dtype) for b in bufs], jax.ShapeDtypeStruct((8, LANES), F32)),
        in_specs=[HBM_SPEC] * (2 * ng),
        out_specs=(SEM_SPEC, SEM_SPEC, *[HBM_SPEC] * (2 * ng), pl.BlockSpec(memory_space=pltpu.VMEM)),
        input_output_aliases={i: 2 + i for i in range(2 * ng)},
        compiler_params=pltpu.CompilerParams(has_side_effects=EFFECT),
    )(*[_hbm(b) for b in bufs])
    return outs[0], outs[1], list(outs[2:2 + ng]), list(outs[2 + ng:2 + 2 * ng]), outs[-1]


def _push_wait(name, started, after, chunked):
    send_sems, recv_sems, srcs, lands, _ = started
    ng = len(srcs)

    def body(*refs):
        copies = _push_copies(refs[:ng], refs[ng:2 * ng], refs[2 * ng], refs[2 * ng + 1], chunked)
        for cp in copies:
            cp.wait_send()
        for cp in copies:
            cp.wait_recv()

    bufs = srcs + lands
    outs = pl.pallas_call(
        body, name=name,
        out_shape=tuple(pltpu.HBM(b.shape, b.dtype) for b in bufs),
        in_specs=[HBM_SPEC] * (2 * ng) + [SEM_SPEC, SEM_SPEC, pl.BlockSpec(memory_space=pl.ANY)],
        out_specs=tuple([HBM_SPEC] * (2 * ng)),
        input_output_aliases={i: i for i in range(2 * ng)},
        compiler_params=pltpu.CompilerParams(has_side_effects=EFFECT),
    )(*bufs, send_sems, recv_sems, after)
    return list(outs[:ng]), list(outs[ng:])


def _sum_adam_devices(sent, recv, dev, w, m, v):
    ndev, r, c_ = recv.shape
    tr = _pick(r, (256, 128, 96, 32))

    def body(dev_ref, own_ref, r_ref, w_ref, m_ref, v_ref, g_ref, d_ref, mo_ref, vo_ref):
        me = dev_ref[0]
        g = jnp.where(me == 0, own_ref[0], r_ref[0]).astype(F32)
        for j in range(1, ndev):
            g = g + jnp.where(me == j, own_ref[0], r_ref[j]).astype(F32)
        g_ref[...] = g
        d_ref[...], mo_ref[...], vo_ref[...] = _adam(w_ref[...], g, m_ref[...], v_ref[...])

    row = pl.BlockSpec((tr, c_), lambda i, dev_ref: (i, 0))
    return pl.pallas_call(
        body, name="sum_adam",
        grid_spec=pltpu.PrefetchScalarGridSpec(
            num_scalar_prefetch=1, grid=(r // tr,),
            in_specs=[pl.BlockSpec((1, tr, c_), lambda i, dev_ref: (dev_ref[0], i, 0)),
                      pl.BlockSpec((ndev, tr, c_), lambda i, dev_ref: (0, i, 0)), row, row, row],
            out_specs=[row, row, row, row]),
        out_shape=[_sds((r, c_))] * 4,
        compiler_params=_cparams(dimension_semantics=("arbitrary",)),
    )(dev, sent, recv, w, m, v)


def _cols_from_shards(w, width):
    ns, r, cs = w.shape
    tr = _pick(r, (256, 128))

    def body(w_ref, o_ref):
        parts = [w_ref[j] for j in range(ns)]
        if width > ns * cs:
            parts.append(jnp.zeros((tr, width - ns * cs), w.dtype))
        o_ref[...] = jnp.concatenate(parts, axis=1)

    return pl.pallas_call(
        body, name="cols_from_shards", grid=(r // tr,),
        in_specs=[pl.BlockSpec((ns, tr, cs), lambda i: (0, i, 0))], out_specs=pl.BlockSpec((tr, width), lambda i: (i, 0)),
        out_shape=jax.ShapeDtypeStruct((r, width), w.dtype), compiler_params=_cparams(dimension_semantics=("arbitrary",)),
    )(w)


def _shards_from_cols(g, cs):
    r, width = g.shape
    tr = _pick(r, (256, 128))

    def body(g_ref, o_ref):
        for j in range(N_DEV):
            o_ref[j] = g_ref[:, j * cs:(j + 1) * cs]

    return pl.pallas_call(
        body, name="shards_from_cols", grid=(r // tr,),
        in_specs=[pl.BlockSpec((tr, width), lambda i: (i, 0))], out_specs=pl.BlockSpec((N_DEV, tr, cs), lambda i: (0, i, 0)),
        out_shape=jax.ShapeDtypeStruct((N_DEV, r, cs), g.dtype), compiler_params=_cparams(dimension_semantics=("arbitrary",)),
    )(g)


def _adam(w, g, m, v):
    m = ADAM_B1 * m + (1.0 - ADAM_B1) * g
    v = ADAM_B2 * v + (1.0 - ADAM_B2) * (g * g)
    m_hat = m / (1.0 - ADAM_B1 ** ADAM_STEP)
    v_hat = v / (1.0 - ADAM_B2 ** ADAM_STEP)
    return -ADAM_LR * (m_hat / (jnp.sqrt(v_hat) + ADAM_EPS) + ADAM_WD * w), m, v


def _sum_devices(gathered):
    m_all, n = gathered.shape
    m_per = m_all // N_DEV

    def body(x_ref, o_ref):
        s = x_ref[0:m_per, :]
        for j in range(1, N_DEV):
            s = s + x_ref[j * m_per:(j + 1) * m_per, :]
        o_ref[...] = s

    return pl.pallas_call(body, name="sum_devices", out_shape=_sds((m_per, n)), compiler_params=_cparams())(gathered)


def _adam_small(w, g, m, v):
    def body(w_ref, g_ref, m_ref, v_ref, d_ref, mo_ref, vo_ref):
        d_ref[...], mo_ref[...], vo_ref[...] = _adam(w_ref[...], g_ref[...], m_ref[...], v_ref[...])

    return pl.pallas_call(body, name="adam_small", out_shape=[_sds(w.shape)] * 3, compiler_params=_cparams())(w, g, m, v)


N_LAYERS = 4
_MIXER = ("dn", "sb", "mla")
_MIXER_PARAMS = {
    "dn": ("dn_w_in", "dn_conv_w", "dn_a_log", "dn_dt_bias", "dn_out_norm", "dn_w_out"),
    "sb": ("sb_w_qkv", "sb_q_norm", "sb_k_norm", "sb_w_out"),
    "mla": ("mla_w_down", "mla_q_a_norm", "mla_kv_a_norm", "mla_w_uq", "mla_w_ukv", "mla_q_nope_norm", "mla_q_rope_norm",
            "mla_k_nope_norm", "mla_k_rope_norm", "mla_w_out"),
}
_BIG_AXIS = {"dn_w_in": 1, "dn_w_out": 0, "sb_w_qkv": 1, "sb_w_out": 0, "mla_w_down": 0, "mla_w_uq": 1, "mla_w_ukv": 1,
             "mla_w_out": 0, "ffn_w_gate_up": 1, "ffn_w_down": 0}


def _weight_names():
    names = []
    for i in range(N_LAYERS):
        p = "l%d_" % i
        names += [p + "mix_norm"] + [p + n for n in _MIXER_PARAMS[_MIXER[i % 3]]] + [p + "ffn_norm", p + "ffn_w_gate_up", p + "ffn_w_down"]
    return names


WEIGHTS = _weight_names()
BIG = [n for n in WEIGHTS if n[3:] in _BIG_AXIS]
SMALL = [n for n in WEIGHTS if n[3:] not in _BIG_AXIS]
CONV = [n for n in SMALL if n.endswith("conv_w")]


def _ceil_to(n, k):
    return -(-n // k) * k


def _pack(arrs, cols, row_mult):
    parts = []
    for a in arrs:
        f = a.reshape(-1)
        parts.append(jnp.pad(f, (0, _ceil_to(f.shape[0], cols) - f.shape[0])))
    flat = jnp.concatenate(parts)
    rows = _ceil_to(flat.shape[0] // cols, row_mult)
    return jnp.pad(flat, (0, rows * cols - flat.shape[0])).reshape(rows, cols)


def _unpack(buf, shapes):
    cols = buf.shape[-1]
    out, r0 = [], 0
    for s in shapes:
        n = math.prod(s)
        nr = _ceil_to(n, cols) // cols
        out.append(buf[r0:r0 + nr].reshape(-1)[:n].reshape(s))
        r0 += nr
    return out


def _layer_groups(i):
    by = {"gu": (704, []), "down": (1024, []), "out": (1024, []), "dn_in": (514, []), "sb_qkv": (384, []), "mla": (512, [])}
    key = {"ffn_w_gate_up": "gu", "ffn_w_down": "down", "dn_w_in": "dn_in", "sb_w_qkv": "sb_qkv", "mla_w_down": "mla",
           "mla_w_uq": "mla", "mla_w_ukv": "mla"}
    for n in BIG:
        if n.startswith("l%d_" % i):
            by[key.get(n[3:], "out")][1].append(n)
    return [g for g in by.values() if g[1]]


LAYER_GROUPS = [_layer_groups(i) for i in range(N_LAYERS)]
N_FFN_GROUPS = 2


def _stack_group(grp, get):
    width, names = grp
    if len(names) == 1 and get(names[0]).shape[1] == width:
        return get(names[0])
    return jnp.concatenate([jnp.pad(get(n), ((0, 0), (0, width - get(n).shape[1]))) for n in names], axis=0)


def _unstack_group(grp, buf, shape_of):
    if len(grp[1]) == 1 and tuple(shape_of(grp[1][0])) == tuple(buf.shape[-2:]):
        return [buf]
    out, r0 = [], 0
    for n in grp[1]:
        rs, cs = shape_of(n)
        out.append(buf[..., r0:r0 + rs, :cs])
        r0 += rs
    return out


def kernel(x, l0_mix_norm, l0_dn_w_in, l0_dn_conv_w, l0_dn_a_log, l0_dn_dt_bias, l0_dn_out_norm, l0_dn_w_out, l0_ffn_norm, l0_ffn_w_gate_up, l0_ffn_w_down, l1_mix_norm, l1_sb_w_qkv, l1_sb_q_norm, l1_sb_k_norm, l1_sb_w_out, l1_ffn_norm, l1_ffn_w_gate_up, l1_ffn_w_down, l2_mix_norm, l2_mla_w_down, l2_mla_q_a_norm, l2_mla_kv_a_norm, l2_mla_w_uq, l2_mla_w_ukv, l2_mla_q_nope_norm, l2_mla_q_rope_norm, l2_mla_k_nope_norm, l2_mla_k_rope_norm, l2_mla_w_out, l2_ffn_norm, l2_ffn_w_gate_up, l2_ffn_w_down, l3_mix_norm, l3_dn_w_in, l3_dn_conv_w, l3_dn_a_log, l3_dn_dt_bias, l3_dn_out_norm, l3_dn_w_out, l3_ffn_norm, l3_ffn_w_gate_up, l3_ffn_w_down, loss_target, m_l0_mix_norm, m_l0_dn_w_in, m_l0_dn_conv_w, m_l0_dn_a_log, m_l0_dn_dt_bias, m_l0_dn_out_norm, m_l0_dn_w_out, m_l0_ffn_norm, m_l0_ffn_w_gate_up, m_l0_ffn_w_down, m_l1_mix_norm, m_l1_sb_w_qkv, m_l1_sb_q_norm, m_l1_sb_k_norm, m_l1_sb_w_out, m_l1_ffn_norm, m_l1_ffn_w_gate_up, m_l1_ffn_w_down, m_l2_mix_norm, m_l2_mla_w_down, m_l2_mla_q_a_norm, m_l2_mla_kv_a_norm, m_l2_mla_w_uq, m_l2_mla_w_ukv, m_l2_mla_q_nope_norm, m_l2_mla_q_rope_norm, m_l2_mla_k_nope_norm, m_l2_mla_k_rope_norm, m_l2_mla_w_out, m_l2_ffn_norm, m_l2_ffn_w_gate_up, m_l2_ffn_w_down, m_l3_mix_norm, m_l3_dn_w_in, m_l3_dn_conv_w, m_l3_dn_a_log, m_l3_dn_dt_bias, m_l3_dn_out_norm, m_l3_dn_w_out, m_l3_ffn_norm, m_l3_ffn_w_gate_up, m_l3_ffn_w_down, v_l0_mix_norm, v_l0_dn_w_in, v_l0_dn_conv_w, v_l0_dn_a_log, v_l0_dn_dt_bias, v_l0_dn_out_norm, v_l0_dn_w_out, v_l0_ffn_norm, v_l0_ffn_w_gate_up, v_l0_ffn_w_down, v_l1_mix_norm, v_l1_sb_w_qkv, v_l1_sb_q_norm, v_l1_sb_k_norm, v_l1_sb_w_out, v_l1_ffn_norm, v_l1_ffn_w_gate_up, v_l1_ffn_w_down, v_l2_mix_norm, v_l2_mla_w_down, v_l2_mla_q_a_norm, v_l2_mla_kv_a_norm, v_l2_mla_w_uq, v_l2_mla_w_ukv, v_l2_mla_q_nope_norm, v_l2_mla_q_rope_norm, v_l2_mla_k_nope_norm, v_l2_mla_k_rope_norm, v_l2_mla_w_out, v_l2_ffn_norm, v_l2_ffn_w_gate_up, v_l2_ffn_w_down, v_l3_mix_norm, v_l3_dn_w_in, v_l3_dn_conv_w, v_l3_dn_a_log, v_l3_dn_dt_bias, v_l3_dn_out_norm, v_l3_dn_w_out, v_l3_ffn_norm, v_l3_ffn_w_gate_up, v_l3_ffn_w_down):
    a = dict(locals())
    return _train_step(a)


def _train_step(a):
    mx, my, mc = lax.axis_index("x"), lax.axis_index("y"), lax.axis_index("c")
    dev = 4 * mx + 2 * my + mc
    dev_arr = jnp.reshape(dev, (1,)).astype(jnp.int32)
    t, d = a["x"].shape[1], a["x"].shape[2]
    xs = a["x"].reshape(t, d)
    target = a["loss_target"].reshape(t, d)

    full = {}

    def unpack(groups, bufs):
        for grp, buf in zip(groups, bufs):
            for n, shards in zip(grp[1], _unstack_group(grp, buf, lambda n: a[n].shape)):
                kind = n[3:]
                if kind == "ffn_w_gate_up":
                    full[n] = shards
                elif _BIG_AXIS[kind] == 0:
                    full[n] = shards.reshape(N_DEV * shards.shape[1], shards.shape[2])
                else:
                    width = DN_PROJ if kind == "dn_w_in" else N_DEV * shards.shape[2]
                    full[n] = _cols_from_shards(shards, width)

    def local_shards(groups):
        return [_stack_group(grp, lambda n: a[n].astype(BF16)) for grp in groups]

    pushed_groups = {"l0_gate_up": LAYER_GROUPS[0][:1], "l0_down": LAYER_GROUPS[0][1:N_FFN_GROUPS]}
    pushed_groups.update({"l%d" % i: LAYER_GROUPS[i] for i in range(1, N_LAYERS)})
    conv_pack = _pack([a[n] for n in CONV], LANES, 8)
    first_groups = LAYER_GROUPS[0][N_FFN_GROUPS:]
    *first_bufs, conv_all, gathered = _all_gather_groups("gather_weights", local_shards(first_groups) + [conv_pack])
    unpack(first_groups, first_bufs)
    for n, parts in zip(CONV, zip(*[_unpack(conv_all[j], [a[n].shape for n in CONV]) for j in range(N_DEV)])):
        full[n] = jnp.concatenate(parts, axis=1)
    after_first = gathered[0, 0].astype(BF16)
    gathers, started = {}, jnp.zeros((), F32)
    for tag, groups in pushed_groups.items():
        srcs = local_shards(groups)
        srcs[-1] = srcs[-1] + after_first
        lands = [lax.dynamic_update_index_in_dim(lax.empty((N_DEV,) + s.shape, s.dtype), s, dev, 0) for s in srcs]
        gathers[tag] = _push_start("gather_start_" + tag, srcs, lands, False)
        started = started + gathers[tag][-1][0, 0]

    def arrived(tag, after):
        unpack(pushed_groups[tag], _push_wait("gather_wait_" + tag, gathers[tag], after, False)[1])

    def vec(n):
        return a[n].reshape(1, -1)

    cos, sin = _rope_tables(t)

    def mixer_args(i):
        p = "l%d_" % i
        kind = _MIXER[i % 3]
        if kind == "dn":
            args = (full[p + "dn_w_in"], full[p + "dn_conv_w"], _pad_lanes(a[p + "dn_a_log"]), _pad_lanes(a[p + "dn_dt_bias"]),
                    vec(p + "dn_out_norm"), full[p + "dn_w_out"])
        elif kind == "sb":
            args = (full[p + "sb_w_qkv"], vec(p + "sb_q_norm"), vec(p + "sb_k_norm"), full[p + "sb_w_out"])
        else:
            w_down, w_uq, w_ukv = _mla_layout(full[p + "mla_w_down"], full[p + "mla_w_uq"], full[p + "mla_w_ukv"])
            args = (w_down, vec(p + "mla_q_a_norm"), vec(p + "mla_kv_a_norm"), w_uq, w_ukv, vec(p + "mla_q_nope_norm"),
                    _pad_lanes(a[p + "mla_q_rope_norm"]), vec(p + "mla_k_nope_norm"), _pad_lanes(a[p + "mla_k_rope_norm"]),
                    full[p + "mla_w_out"], cos, sin)
        return kind, args

    fwd = {"dn": _dn_fwd, "sb": _sb_fwd, "mla": _mla_fwd}
    bwd = {"dn": _dn_bwd, "sb": _sb_bwd, "mla": _mla_bwd}
    saved, layer_args = [], []
    for i in range(N_LAYERS):
        p = "l%d_" % i
        if i > 0:
            arrived("l%d" % i, xs)
        kind, args = mixer_args(i)
        layer_args.append((kind, args))
        gain = vec(p + "mix_norm") + started if i == 0 else vec(p + "mix_norm")
        h = _rmsnorm_fwd("mix_norm", xs, gain)
        x_mid, sv_mix = fwd[kind](xs, h, *args)
        if i == 0:
            arrived("l0_gate_up", x_mid)

        def down_weight(act, p=p, first=(i == 0)):
            if first:
                arrived("l0_down", act)
            return full[p + "ffn_w_down"]

        x_out, sv_ffn = _ffn_fwd(x_mid, vec(p + "ffn_norm"), full[p + "ffn_w_gate_up"], down_weight)
        saved.append((xs, sv_mix, sv_ffn))
        xs = x_out
    dy, loss_part = _loss_head(xs, target)

    grads, big_out = {}, {}

    def grad_shards(n):
        g, (rs, cs) = grads[n], a[n].shape
        if g.ndim == 3:
            return g
        if _BIG_AXIS[n[3:]] == 0:
            return g.reshape(N_DEV, rs, cs)
        return _shards_from_cols(g, cs)

    def push_grads(tag, groups):
        sends = []
        for grp in groups:
            if len(grp[1]) == 1 and a[grp[1][0]].shape[1] == grp[0]:
                sends.append(grad_shards(grp[1][0]))
                continue
            parts = [jnp.pad(grad_shards(n), ((0, 0), (0, 0), (0, grp[0] - a[n].shape[1]))) for n in grp[1]]
            sends.append(jnp.concatenate(parts, axis=1))
        lands = [lax.empty(s.shape, s.dtype) for s in sends]
        return tag, groups, _push_start("grads_start_" + tag, sends, lands, True)

    def finish_grads(push, after):
        tag, groups, pushed = push
        sents, recvs = _push_wait("grads_wait_" + tag, pushed, after, True)
        for grp, sent, recv in zip(groups, sents, recvs):
            packs = [_stack_group(grp, lambda n, pre=pre: a[pre + n]) for pre in ("", "m_", "v_")]
            outs = [_unstack_group(grp, o, lambda n: a[n].shape) for o in _sum_adam_devices(sent, recv, dev_arr, *packs)]
            for j, n in enumerate(grp[1]):
                big_out[n] = [o[j] for o in outs]

    mixer_push = None
    for i in reversed(range(N_LAYERS)):
        p = "l%d_" % i
        kind, args = layer_args[i]
        x_in, sv_mix, sv_ffn = saved[i]
        gain = vec(p + "ffn_norm") if mixer_push is None else vec(p + "ffn_norm") + mixer_push[2][-1][0, 0]
        dx_mid, grads[p + "ffn_norm"], grads[p + "ffn_w_gate_up"], grads[p + "ffn_w_down"] = _ffn_bwd(
            sv_ffn, dy, gain, full[p + "ffn_w_gate_up"], full[p + "ffn_w_down"])
        ffn_push = push_grads("l%d_ffn" % i, LAYER_GROUPS[i][:N_FFN_GROUPS])
        if mixer_push is not None:
            finish_grads(mixer_push, dx_mid)
        res = bwd[kind](sv_mix, dx_mid, *args)
        dh = res[0]
        if kind == "mla":
            res = list(res)
            res[1], res[4], res[5] = _mla_unlayout(res[1], res[4], res[5])
        for n, g in zip(_MIXER_PARAMS[kind], res[1:]):
            grads[p + n] = g
        mixer_push = push_grads("l%d_mix" % i, LAYER_GROUPS[i][N_FFN_GROUPS:])
        pushed = ffn_push[2][-1][0, 0] + mixer_push[2][-1][0, 0]
        dy, grads[p + "mix_norm"] = _rmsnorm_bwd("mix_norm_bwd", x_in, vec(p + "mix_norm") + pushed, dh, dx_mid)
        finish_grads(ffn_push, dy)
    grad_x = dy.reshape(a["x"].shape)

    small_full_shapes = [full[n].shape if n in CONV else a[n].shape for n in SMALL]
    small_grads = []
    for n, s in zip(SMALL, small_full_shapes):
        g = grads[n].reshape(-1)
        small_grads.append(g[:math.prod(s)])
    small_pack = _pack(small_grads + [loss_part.reshape(-1)], LANES, 8)
    small_sum = _sum_devices(_all_gather("gather_small_grads", small_pack))
    small_red = _unpack(small_sum, small_full_shapes + [(LANES,)])
    loss = small_red[-1][0]
    g_small = {}
    for n, g in zip(SMALL, small_red[:-1]):
        if n in CONV:
            cs = a[n].shape[1]
            g = lax.dynamic_slice_in_dim(g, dev * cs, cs, axis=1)
        g_small[n] = g
    small_shapes = [a[n].shape for n in SMALL]
    packs = [_pack([src[n] for n in SMALL], LANES, 8) for src in
             ({n: a[n] for n in SMALL}, g_small, {n: a["m_" + n] for n in SMALL}, {n: a["v_" + n] for n in SMALL})]
    d_small, m_small, v_small = (_unpack(o, small_shapes) for o in _adam_small(*packs))
    finish_grads(mixer_push, small_sum)

    small_out = dict(zip(SMALL, zip([g_small[n] for n in SMALL], d_small, m_small, v_small)))

    def out(k):
        return [small_out[n][k] if n in small_out else big_out[n][k] for n in WEIGHTS]

    return (loss, grad_x, *out(0), *out(1), *out(2), *out(3))
```

```python
import math

import jax
import jax.numpy as jnp
from jax import lax
from jax.experimental import pallas as pl
from jax.experimental.pallas import tpu as pltpu

F32 = jnp.float32
BF16 = jnp.bfloat16
GRAD_DTYPE = BF16
F32X3 = lax.Precision.HIGH

LANES = 128
N_DEV = 8
N_HEADS = 8
HEAD = 128
NORM_EPS = 1e-6
DN_CHUNK = 64
DN_STEP = 2 * DN_CHUNK
ATT_BLOCK = 512
ATT_Q = 512
MLA_ROPE = 64
MLA_QK = 192
ROPE_THETA = 10000.0
VMEM_LIMIT = 56 * 1024 * 1024
MM_TILE_BYTES = 32 * 1024 * 1024

ADAM_LR = 0.001
ADAM_B1 = 0.9
ADAM_B2 = 0.999
ADAM_EPS = 1e-08
ADAM_WD = 0.01
ADAM_STEP = 10


def _cparams(**kw):
    return pltpu.CompilerParams(vmem_limit_bytes=VMEM_LIMIT, **kw)


def _pick(n, cands):
    for c in cands:
        if c <= n and n % c == 0:
            return c
    return n


def _mm(name, a, b, *, ta=False, tb=False, out_dtype=F32, add=None, tm=None, tn=None, tk=None):
    if ta:
        K, M = a.shape
    else:
        M, K = a.shape
    N = b.shape[0] if tb else b.shape[1]
    tm = tm or _pick(M, (1024, 512, 256, 128))
    tn = tn or _pick(N, (1024, 512, 384, 256, 128))
    if tk is None:
        fits = [c for c in (4096, 2048, 1408, 1024, 512, 384, 256, 128)
                if c <= K and K % c == 0 and 2 * c * (tm * a.dtype.itemsize + tn * b.dtype.itemsize) <= MM_TILE_BYTES]
        tk = fits[0] if fits else K
    return _mm_raw(
        name, a, b, ta=ta, tb=tb, out_dtype=out_dtype, add=add, grid=(M // tm, N // tn, K // tk), out_shape=(M, N),
        a_block=(tk, tm) if ta else (tm, tk), a_map=(lambda i, j, k: (k, i)) if ta else (lambda i, j, k: (i, k)),
        b_block=(tn, tk) if tb else (tk, tn), b_map=(lambda i, j, k: (j, k)) if tb else (lambda i, j, k: (k, j)),
        o_block=(tm, tn), o_map=lambda i, j, k: (i, j))


def _mm_raw(name, a, b, *, ta, tb, out_dtype, add, grid, out_shape, a_block, a_map, b_block, b_map, o_block, o_map):
    nk = grid[2]
    tm, tn = o_block
    dn = (((0 if ta else 1,), (1 if tb else 0,)), ((), ()))
    has_add = add is not None

    def kern(*refs):
        if has_add:
            a_ref, b_ref, add_ref, o_ref, acc_ref = refs
        else:
            a_ref, b_ref, o_ref, acc_ref = refs
        k = pl.program_id(2)
        part = lax.dot_general(a_ref[...].astype(BF16), b_ref[...].astype(BF16), dn, preferred_element_type=F32)

        @pl.when(k == 0)
        def _():
            acc_ref[...] = part

        @pl.when(k > 0)
        def _():
            acc_ref[...] += part

        @pl.when(k == nk - 1)
        def _():
            r = acc_ref[...]
            if has_add:
                r = r + add_ref[...]
            o_ref[...] = r.astype(out_dtype)

    in_specs = [pl.BlockSpec(a_block, a_map), pl.BlockSpec(b_block, b_map)]
    args = [a, b]
    if has_add:
        in_specs.append(pl.BlockSpec(o_block, o_map))
        args.append(add)
    return pl.pallas_call(
        kern, name=name,
        grid=grid,
        in_specs=in_specs,
        out_specs=pl.BlockSpec(o_block, o_map),
        out_shape=jax.ShapeDtypeStruct(out_shape, out_dtype),
        scratch_shapes=[pltpu.VMEM((tm, tn), F32)],
        compiler_params=_cparams(dimension_semantics=("parallel", "parallel", "arbitrary")),
    )(*args)


def _rows(name, body, ins, outs, *, tt, consts=(), accs=()):
    in_specs, args = [], []
    first = ins[0][0] if isinstance(ins[0], tuple) else ins[0]
    t = first.shape[-2]
    tt = min(tt, t)
    for x in ins:
        if isinstance(x, tuple):
            arr, bs, im = x
            in_specs.append(pl.BlockSpec(bs, im))
            args.append(arr)
        else:
            in_specs.append(_row_spec(x.shape, tt))
            args.append(x)
    for c in consts:
        in_specs.append(pl.BlockSpec(c.shape, lambda i, _n=c.ndim: (0,) * _n))
        args.append(c)
    out_specs = [_row_spec(o.shape, tt) for o in outs]
    out_specs += [pl.BlockSpec(a.shape, lambda i, _n=len(a.shape): (0,) * _n) for a in accs]
    res = pl.pallas_call(
        body, name=name, grid=(t // tt,),
        in_specs=in_specs, out_specs=out_specs, out_shape=list(outs) + list(accs),
        compiler_params=_cparams(dimension_semantics=("arbitrary",)),
    )(*args)
    return res


def _row_spec(shape, tt):
    if len(shape) == 2:
        return pl.BlockSpec((tt, shape[1]), lambda i: (i, 0))
    return pl.BlockSpec((shape[0], tt, shape[2]), lambda i: (0, i, 0))


def _sds(shape, dtype=F32):
    return jax.ShapeDtypeStruct(tuple(shape), dtype)


def _acc(ref, val):
    i = pl.program_id(0)

    @pl.when(i == 0)
    def _():
        ref[...] = val

    @pl.when(i > 0)
    def _():
        ref[...] += val


def _rms(x, g):
    return x * lax.rsqrt(jnp.mean(x * x, axis=-1, keepdims=True) + NORM_EPS) * g


def _silu(x):
    return x / (1.0 + jnp.exp(-x))


def _softplus(x):
    return jnp.maximum(x, 0.0) + jnp.log(1.0 + jnp.exp(-jnp.abs(x)))


def _sigmoid(x):
    return 1.0 / (1.0 + jnp.exp(-x))


def _rmsnorm_fwd(name, x, g, tt=512):
    def body(x_ref, g_ref, h_ref):
        h_ref[...] = _rms(x_ref[...], g_ref[...]).astype(BF16)

    return _rows(name, body, [x], [_sds(x.shape, BF16)], tt=tt, consts=[g])[0]


def _rmsnorm_bwd(name, x, g, dh, dres, tt=512):
    def body(x_ref, dh_ref, dres_ref, g_ref, dx_ref, dg_ref):
        _, vjp = jax.vjp(_rms, x_ref[...], g_ref[...])
        dx, dg = vjp(dh_ref[...])
        dx_ref[...] = dx + dres_ref[...]
        _acc(dg_ref, dg)

    return _rows(name, body, [x, dh, dres], [_sds(x.shape)], tt=tt, consts=[g], accs=[_sds(g.shape)])


def _ffn_fwd(x, norm_g, w3, down_weight):
    t, d = x.shape
    ns, _, cs = w3.shape
    half = ns // 2
    w2 = w3.reshape(ns * d, cs)
    h = _rmsnorm_fwd("ffn_norm", x, norm_g)
    tm = _pick(t, (1024, 512, 256, 128))
    nm = t // tm

    def gate_up(h_ref, wg_ref, wu_ref, g_ref, u_ref, a_ref):
        hv = h_ref[...]
        g = _dot(hv, wg_ref[...])
        u = _dot(hv, wu_ref[...])
        g_ref[...] = g
        u_ref[...] = u
        a_ref[...] = (_silu(g) * u).astype(BF16)

    hid = pl.BlockSpec((tm, cs), lambda j, i: (j * nm + i, 0))
    g, u, act = pl.pallas_call(
        gate_up, name="ffn_gate_up", grid=(half, nm),
        in_specs=[pl.BlockSpec((tm, d), lambda j, i: (i, 0)), pl.BlockSpec((d, cs), lambda j, i: (j, 0)),
                  pl.BlockSpec((d, cs), lambda j, i: (j + half, 0))],
        out_specs=[hid, hid, hid], out_shape=[_sds((half * t, cs)), _sds((half * t, cs)), _sds((half * t, cs), BF16)],
        compiler_params=_cparams(dimension_semantics=("parallel", "arbitrary")),
    )(h, w2, w2)
    w_down = down_weight(act)

    def down(a_ref, w_ref, x_ref, y_ref):
        y = x_ref[...]
        for j in range(half):
            y = y + _dot(a_ref[j], w_ref[j])
        y_ref[...] = y

    y = pl.pallas_call(
        down, name="ffn_down", grid=(nm,),
        in_specs=[pl.BlockSpec((half, tm, cs), lambda i: (0, i, 0)), pl.BlockSpec((half, cs, d), lambda i: (0, 0, 0)),
                  pl.BlockSpec((tm, d), lambda i: (i, 0))],
        out_specs=pl.BlockSpec((tm, d), lambda i: (i, 0)), out_shape=_sds((t, d)),
        compiler_params=_cparams(dimension_semantics=("parallel",)),
    )(act.reshape(half, t, cs), w_down.reshape(half, cs, d), x)
    return y, (x, h, g, u, act)


def _ffn_bwd(saved, dy, norm_g, w3, w_down):
    x, h, g, u, act = saved
    t, d = x.shape
    ns, _, cs = w3.shape
    half = ns // 2
    w2 = w3.reshape(ns * d, cs)
    tm = _pick(t, (1024, 512, 256, 128))
    nm = t // tm
    tk = _pick(t, (4096, 2048, 1024, 512, 256, 128))
    nk = t // tk
    d_wdown = _mm_raw("ffn_down_wgrad", act, dy, ta=True, tb=False, out_dtype=GRAD_DTYPE, add=None, grid=(half, 1, nk),
                      out_shape=(half * cs, d), a_block=(tk, cs), a_map=lambda i, j, k: (i * nk + k, 0),
                      b_block=(tk, d), b_map=lambda i, j, k: (k, 0), o_block=(cs, d), o_map=lambda i, j, k: (i, 0))
    def down_dgrad(dy_ref, wd_ref, g_ref, u_ref, dg_ref, du_ref):
        da = _dot_nt(dy_ref[...].astype(BF16), wd_ref[...])
        gv, uv = g_ref[...], u_ref[...]
        s = _sigmoid(gv)
        dg_ref[...] = (da * uv * s * (1.0 + gv * (1.0 - s))).astype(BF16)
        du_ref[...] = (da * gv * s).astype(BF16)

    hid = pl.BlockSpec((tm, cs), lambda j, i: (j * nm + i, 0))
    dg, du = pl.pallas_call(
        down_dgrad, name="ffn_down_dgrad", grid=(half, nm),
        in_specs=[pl.BlockSpec((tm, d), lambda j, i: (i, 0)), pl.BlockSpec((cs, d), lambda j, i: (j, 0)), hid, hid],
        out_specs=[hid, hid], out_shape=[_sds((half * t, cs), BF16), _sds((half * t, cs), BF16)],
        compiler_params=_cparams(dimension_semantics=("parallel", "arbitrary")),
    )(dy, w_down, g, u)

    def wgrad(name, dd):
        return _mm_raw(name, h, dd, ta=True, tb=False, out_dtype=GRAD_DTYPE, add=None, grid=(1, half, nk), out_shape=(half * d, cs),
                       a_block=(tk, d), a_map=lambda i, j, k: (k, 0), b_block=(tk, cs), b_map=lambda i, j, k: (j * nk + k, 0),
                       o_block=(d, cs), o_map=lambda i, j, k: (j, 0))

    def gate_up_dgrad(dg_ref, du_ref, w_ref, dh_ref):
        dh = _dot_nt(dg_ref[0], w_ref[0]) + _dot_nt(du_ref[0], w_ref[half])
        for j in range(1, half):
            dh = dh + _dot_nt(dg_ref[j], w_ref[j]) + _dot_nt(du_ref[j], w_ref[half + j])
        dh_ref[...] = dh

    th = _pick(t, (512, 256, 128))
    hid3 = pl.BlockSpec((half, th, cs), lambda i: (0, i, 0))
    dh = pl.pallas_call(
        gate_up_dgrad, name="ffn_gate_up_dgrad", grid=(t // th,),
        in_specs=[hid3, hid3, pl.BlockSpec((ns, d, cs), lambda i: (0, 0, 0))],
        out_specs=pl.BlockSpec((th, d), lambda i: (i, 0)), out_shape=_sds((t, d)),
        compiler_params=_cparams(dimension_semantics=("parallel",)),
    )(dg.reshape(half, t, cs), du.reshape(half, t, cs), w3)
    d_w3 = jnp.concatenate([wgrad("ffn_gate_wgrad", dg), wgrad("ffn_up_wgrad", du)], axis=0).reshape(ns, d, cs)
    dx, dgain = _rmsnorm_bwd("ffn_norm_bwd", x, norm_g, dh, dy)
    return dx, dgain, d_w3, d_wdown


def _dot_nt(a, b):
    return lax.dot_general(a, b, (((1,), (1,)), ((), ())), preferred_element_type=F32)


def _dot_tn(a, b):
    return lax.dot_general(a, b, (((0,), (0,)), ((), ())), preferred_element_type=F32)


def _dot(a, b):
    return jnp.dot(a, b, preferred_element_type=F32)


CUM_BLOCK = 128


def _tri2(lower):
    r = lax.broadcasted_iota(jnp.int32, (CUM_BLOCK, CUM_BLOCK), 0)
    c = lax.broadcasted_iota(jnp.int32, (CUM_BLOCK, CUM_BLOCK), 1)
    tri = ((r > c) if lower else (r < c)).astype(BF16)
    return jnp.concatenate([tri, tri], axis=0)


def _run_sums(x, tri2, run, reverse):
    nb = x.shape[1] // CUM_BLOCK
    outs = [None] * nb
    for j in (reversed(range(nb)) if reverse else range(nb)):
        xj = x[:, j * CUM_BLOCK:(j + 1) * CUM_BLOCK]
        hi = xj.astype(BF16)
        lo = (xj - hi.astype(F32)).astype(BF16)
        outs[j] = _dot(jnp.concatenate([hi, lo], axis=1), tri2) + run
        run = run + jnp.sum(xj, axis=1, keepdims=True)
    return jnp.concatenate(outs, axis=1), run


def _log_sigmoid(z):
    return jnp.minimum(z, 0.0) - jnp.log(1.0 + jnp.exp(-jnp.abs(z)))


def _heads_in(ref, h, width=HEAD):
    return ref[:, h * width:(h + 1) * width]


def _sb_qk(q, k, gq, gk):
    return _rms(q, gq) * (HEAD ** -0.5), _rms(k, gk)


def _sb_prep_fwd(qkv, gq, gk):
    t = qkv.shape[0]

    def body(x_ref, gq_ref, gk_ref, q_ref, k_ref, v_ref):
        for h in range(N_HEADS):
            q, k = _sb_qk(_heads_in(x_ref, h), _heads_in(x_ref, N_HEADS + h), gq_ref[...], gk_ref[...])
            q_ref[h] = q.astype(BF16)
            k_ref[h] = k.astype(BF16)
            v_ref[h] = _heads_in(x_ref, 2 * N_HEADS + h).astype(BF16)

    hm = _sds((N_HEADS, t, HEAD), BF16)
    return _rows("sb_prep", body, [qkv], [hm, hm, hm], tt=256, consts=[gq, gk])


def _sb_prep_bwd(qkv, gq, gk, dq, dk, dv):
    def body(x_ref, dq_ref, dk_ref, dv_ref, gq_ref, gk_ref, dx_ref, dgq_ref, dgk_ref):
        dgq = jnp.zeros(gq_ref.shape, F32)
        dgk = jnp.zeros(gk_ref.shape, F32)
        for h in range(N_HEADS):
            _, vjp = jax.vjp(_sb_qk, _heads_in(x_ref, h), _heads_in(x_ref, N_HEADS + h), gq_ref[...], gk_ref[...])
            a, b, c, d = vjp((dq_ref[h], dk_ref[h]))
            dx_ref[:, h * HEAD:(h + 1) * HEAD] = a.astype(BF16)
            dx_ref[:, (N_HEADS + h) * HEAD:(N_HEADS + h + 1) * HEAD] = b.astype(BF16)
            dx_ref[:, (2 * N_HEADS + h) * HEAD:(2 * N_HEADS + h + 1) * HEAD] = dv_ref[h].astype(BF16)
            dgq, dgk = dgq + c, dgk + d
        _acc(dgq_ref, dgq)
        _acc(dgk_ref, dgk)

    return _rows("sb_prep_bwd", body, [qkv, dq, dk, dv], [_sds(qkv.shape, BF16)], tt=256, consts=[gq, gk],
                 accs=[_sds(gq.shape), _sds(gk.shape)])


def _q_block(t):
    return min(ATT_Q, t)


def _key_order(bq, qb):
    rows = lax.broadcasted_iota(jnp.int32, (bq, ATT_BLOCK), 0)
    cols = lax.broadcasted_iota(jnp.int32, (bq, ATT_BLOCK), 1)
    return rows - cols + qb * bq


def _sb_attn_fwd(q, k, v):
    nh, t, _ = q.shape
    bq = _q_block(t)
    per = bq // ATT_BLOCK

    def kern(q_ref, k_ref, v_ref, o_ref):
        qb = pl.program_id(1)
        qv = q_ref[0]
        after = _tri2(True)
        order = _key_order(bq, qb)
        nkb = (qb + 1) * per

        def body(i, carry, diagonal):
            o_acc, run = carry
            kb = nkb - 1 - i
            off = pl.multiple_of(kb * ATT_BLOCK, ATT_BLOCK)
            kv = k_ref[0, pl.ds(off, ATT_BLOCK), :]
            vv = v_ref[0, pl.ds(off, ATT_BLOCK), :]
            z = _dot_nt(qv, kv)
            lsz = _log_sigmoid(z)
            lsn = lsz - z
            if diagonal:
                past = order > kb * ATT_BLOCK
                lsn = jnp.where(past, lsn, 0.0)
            la, run = _run_sums(lsn, after, run, True)
            a = jnp.exp(lsz + la)
            if diagonal:
                a = jnp.where(past, a, 0.0)
            o_acc = o_acc + _dot(a.astype(BF16), vv)
            return o_acc, run

        carry = lax.fori_loop(0, per, lambda i, c: body(i, c, True), (jnp.zeros((bq, HEAD), F32), jnp.zeros((bq, 1), F32)))
        o, _ = lax.fori_loop(per, nkb, lambda i, c: body(i, c, False), carry)
        o_ref[...] = o

    return pl.pallas_call(
        kern, name="sb_attn_fwd", grid=(nh, t // bq),
        in_specs=[pl.BlockSpec((1, bq, HEAD), lambda h, i: (h, i, 0)),
                  pl.BlockSpec((1, t, HEAD), lambda h, i: (h, 0, 0)),
                  pl.BlockSpec((1, t, HEAD), lambda h, i: (h, 0, 0))],
        out_specs=pl.BlockSpec((bq, HEAD), lambda h, i: (i, h)),
        out_shape=_sds((t, nh * HEAD)),
        compiler_params=_cparams(dimension_semantics=("parallel", "arbitrary")),
    )(q, k, v)


def _sb_attn_bwd(q, k, v, do):
    nh, t, _ = q.shape
    bq = _q_block(t)
    per = bq // ATT_BLOCK

    def kern(q_ref, k_ref, v_ref, do_ref, dq_ref, dk_ref, dv_ref, g_s, ls_s):
        qb = pl.program_id(1)

        @pl.when(qb == 0)
        def _():
            dk_ref[...] = jnp.zeros(dk_ref.shape, F32)
            dv_ref[...] = jnp.zeros(dv_ref.shape, F32)

        qv = q_ref[0]
        dob = do_ref[...].astype(BF16)
        after, before = _tri2(True), _tri2(False)
        order = _key_order(bq, qb)
        nkb = (qb + 1) * per

        def sweep_left(i, run, diagonal):
            kb = nkb - 1 - i
            off = pl.multiple_of(kb * ATT_BLOCK, ATT_BLOCK)
            kv = k_ref[0, pl.ds(off, ATT_BLOCK), :]
            vv = v_ref[0, pl.ds(off, ATT_BLOCK), :]
            z = _dot_nt(qv, kv)
            lsz = _log_sigmoid(z)
            lsn = lsz - z
            if diagonal:
                past = order > kb * ATT_BLOCK
                lsn = jnp.where(past, lsn, 0.0)
            la, run = _run_sums(lsn, after, run, True)
            a = jnp.exp(lsz + la)
            if diagonal:
                a = jnp.where(past, a, 0.0)
            g_s[kb] = _dot_nt(dob, vv) * a
            ls_s[kb] = lsz
            dv_ref[0, pl.ds(off, ATT_BLOCK), :] += _dot_tn(a.astype(BF16), dob)
            return run

        zero = jnp.zeros((bq, 1), F32)
        run = lax.fori_loop(0, per, lambda i, c: sweep_left(i, c, True), zero)
        lax.fori_loop(per, nkb, lambda i, c: sweep_left(i, c, False), run)

        def sweep_right(kb, carry, diagonal):
            dq_acc, run_g = carry
            off = pl.multiple_of(kb * ATT_BLOCK, ATT_BLOCK)
            kv = k_ref[0, pl.ds(off, ATT_BLOCK), :]
            g = g_s[kb]
            sg = jnp.exp(ls_s[kb])
            dls, run_g = _run_sums(g, before, run_g, False)
            dz = g * (1.0 - sg) - dls * sg
            if diagonal:
                dz = jnp.where(order > kb * ATT_BLOCK, dz, 0.0)
            dzb = dz.astype(BF16)
            dk_ref[0, pl.ds(off, ATT_BLOCK), :] += _dot_tn(dzb, qv)
            return dq_acc + _dot(dzb, kv), run_g

        carry = lax.fori_loop(0, nkb - per, lambda i, c: sweep_right(i, c, False), (jnp.zeros((bq, HEAD), F32), zero))
        dq, _ = lax.fori_loop(nkb - per, nkb, lambda i, c: sweep_right(i, c, True), carry)
        dq_ref[0] = dq

    hm = _sds((nh, t, HEAD))
    full = pl.BlockSpec((1, t, HEAD), lambda h, i: (h, 0, 0))
    tok = pl.BlockSpec((bq, HEAD), lambda h, i: (i, h))
    nkb_max = t // ATT_BLOCK
    return pl.pallas_call(
        kern, name="sb_attn_bwd", grid=(nh, t // bq),
        in_specs=[pl.BlockSpec((1, bq, HEAD), lambda h, i: (h, i, 0)), full, full, tok],
        out_specs=[pl.BlockSpec((1, bq, HEAD), lambda h, i: (h, i, 0)), full, full],
        out_shape=[hm, hm, hm],
        scratch_shapes=[pltpu.VMEM((nkb_max, bq, ATT_BLOCK), F32), pltpu.VMEM((nkb_max, bq, ATT_BLOCK), F32)],
        compiler_params=_cparams(dimension_semantics=("parallel", "arbitrary")),
    )(q, k, v, do)


def _sb_fwd(x, h, w_qkv, gq, gk, w_out):
    qkv = _mm("sb_qkv", h, w_qkv)
    q, k, v = _sb_prep_fwd(qkv, gq, gk)
    o = _sb_attn_fwd(q, k, v)
    y = _mm("sb_out", o, w_out, add=x)
    return y, (h, qkv, q, k, v, o)


def _sb_bwd(saved, dy, w_qkv, gq, gk, w_out):
    h, qkv, q, k, v, o = saved
    d_wout = _mm("sb_out_wgrad", o, dy, ta=True, out_dtype=GRAD_DTYPE)
    do = _mm("sb_out_dgrad", dy, w_out, tb=True)
    dq, dk, dv = _sb_attn_bwd(q, k, v, do)
    dqkv, dgq, dgk = _sb_prep_bwd(qkv, gq, gk, dq, dk, dv)
    d_wqkv = _mm("sb_qkv_wgrad", h, dqkv, ta=True, out_dtype=GRAD_DTYPE)
    dh = _mm("sb_qkv_dgrad", dqkv, w_qkv, tb=True)
    return dh, d_wqkv, dgq, dgk, d_wout


DN_QKV = 3 * N_HEADS * HEAD
DN_PROJ = DN_QKV + N_HEADS * HEAD + LANES
DN_CONV = 4
HALO = 8
CONV_COLS = 512


def _dn_conv_fwd(proj, conv_w, tt=256):
    t = proj.shape[0]
    tt = min(tt, t)

    def body(u_ref, prev_ref, w_ref, c_ref):
        i = pl.program_id(0)
        for cc in range(DN_QKV // CONV_COLS):
            cs = slice(cc * CONV_COLS, (cc + 1) * CONV_COLS)
            cur = u_ref[:, cs]
            prev = jnp.where(i > 0, prev_ref[:, cs], 0.0)
            ext = jnp.concatenate([prev, cur], axis=0)
            y = cur * w_ref[DN_CONV - 1:DN_CONV, cs]
            for j in range(DN_CONV - 1):
                y = y + pltpu.roll(ext, DN_CONV - 1 - j, 0)[HALO:] * w_ref[j:j + 1, cs]
            c_ref[:, cs] = y

    return _rows("dn_conv", body,
                 [(proj, (tt, DN_QKV), lambda i: (i, 0)),
                  (proj, (HALO, DN_QKV), lambda i: (jnp.maximum(i * (tt // HALO) - 1, 0), 0))],
                 [_sds((t, DN_QKV))], tt=tt, consts=[conv_w])[0]


def _dn_conv_bwd(proj, conv_w, dc, dz, dab, tt=256):
    t = proj.shape[0]
    tt = min(tt, t)
    nblk = t // tt

    def body(u_ref, prev_ref, dc_ref, next_ref, dz_ref, dab_ref, w_ref, dp_ref, dw_ref):
        i = pl.program_id(0)
        dws = []
        for cc in range(DN_QKV // CONV_COLS):
            cs = slice(cc * CONV_COLS, (cc + 1) * CONV_COLS)
            cur = u_ref[:, cs]
            prev = jnp.where(i > 0, prev_ref[:, cs], 0.0)
            ext_u = jnp.concatenate([prev, cur], axis=0)
            d = dc_ref[:, cs]
            nxt = jnp.where(i < nblk - 1, next_ref[:, cs], 0.0)
            ext_d = jnp.concatenate([d, nxt], axis=0)
            du = d * w_ref[DN_CONV - 1:DN_CONV, cs]
            rows = [jnp.sum(d * cur, axis=0, keepdims=True)]
            for j in range(DN_CONV - 2, -1, -1):
                sh = DN_CONV - 1 - j
                du = du + pltpu.roll(ext_d, tt + HALO - sh, 0)[:tt] * w_ref[j:j + 1, cs]
                rows.insert(0, jnp.sum(d * pltpu.roll(ext_u, sh, 0)[HALO:], axis=0, keepdims=True))
            dp_ref[:, cs] = du.astype(BF16)
            dws.append(jnp.concatenate(rows, axis=0))
        dp_ref[:, DN_QKV:DN_QKV + N_HEADS * HEAD] = dz_ref[...].astype(BF16)
        dp_ref[:, DN_QKV + N_HEADS * HEAD:] = dab_ref[...].astype(BF16)
        _acc(dw_ref, jnp.concatenate(dws, axis=1))

    return _rows("dn_conv_bwd", body,
                 [(proj, (tt, DN_QKV), lambda i: (i, 0)),
                  (proj, (HALO, DN_QKV), lambda i: (jnp.maximum(i * (tt // HALO) - 1, 0), 0)),
                  dc,
                  (dc, (HALO, DN_QKV), lambda i: (jnp.minimum((i + 1) * (tt // HALO), t // HALO - 1), 0)),
                  dz, dab],
                 [_sds((t, DN_PROJ), BF16)], tt=tt, consts=[conv_w], accs=[_sds(conv_w.shape)])


def _l2n(x):
    return x * lax.rsqrt(jnp.sum(x * x, axis=-1, keepdims=True) + NORM_EPS)


def _dn_qkv(cq, ck, cv):
    return _l2n(_silu(cq)) * (HEAD ** -0.5), _l2n(_silu(ck)), _silu(cv)


def _dn_gates(ab, a_log, dt_bias):
    lane = lax.broadcasted_iota(jnp.int32, ab.shape, 1)
    g = -jnp.exp(a_log) * _softplus(ab + dt_bias)
    return jnp.where(lane < N_HEADS, g, jnp.where(lane < 2 * N_HEADS, _sigmoid(ab), 0.0))


def _ab_spec(tt):
    return (tt, LANES), lambda i: (i, DN_PROJ // LANES - 1)


def _dn_prep_fwd(c, proj, a_log, dt_bias, tt=256):
    t = c.shape[0]
    tt = min(tt, t)

    def body(c_ref, ab_ref, al_ref, dt_ref, q_ref, k_ref, v_ref, g_ref):
        for h in range(N_HEADS):
            q_ref[h], k_ref[h], v_ref[h] = _dn_qkv(_heads_in(c_ref, h), _heads_in(c_ref, N_HEADS + h), _heads_in(c_ref, 2 * N_HEADS + h))
        g_ref[...] = _dn_gates(ab_ref[...], al_ref[...], dt_ref[...])

    hm = _sds((N_HEADS, t, HEAD))
    return _rows("dn_prep", body, [c, (proj,) + _ab_spec(tt)], [hm, hm, hm, _sds((t, LANES))], tt=tt, consts=[a_log, dt_bias])


def _dn_prep_bwd(c, proj, a_log, dt_bias, dq, dk, dv, dgates, tt=256):
    t = c.shape[0]
    tt = min(tt, t)

    def body(c_ref, ab_ref, dq_ref, dk_ref, dv_ref, dg_ref, al_ref, dt_ref, dc_ref, dab_ref, dal_ref, ddt_ref):
        for h in range(N_HEADS):
            _, vjp = jax.vjp(_dn_qkv, _heads_in(c_ref, h), _heads_in(c_ref, N_HEADS + h), _heads_in(c_ref, 2 * N_HEADS + h))
            a, b, d = vjp((dq_ref[h], dk_ref[h], dv_ref[h]))
            dc_ref[:, h * HEAD:(h + 1) * HEAD] = a
            dc_ref[:, (N_HEADS + h) * HEAD:(N_HEADS + h + 1) * HEAD] = b
            dc_ref[:, (2 * N_HEADS + h) * HEAD:(2 * N_HEADS + h + 1) * HEAD] = d
        _, vjp = jax.vjp(_dn_gates, ab_ref[...], al_ref[...], dt_ref[...])
        dab, dal, ddt = vjp(dg_ref[...])
        dab_ref[...] = dab
        _acc(dal_ref, dal)
        _acc(ddt_ref, ddt)

    return _rows("dn_prep_bwd", body, [c, (proj,) + _ab_spec(tt), dq, dk, dv, dgates], [_sds(c.shape), _sds((t, LANES))],
                 tt=tt, consts=[a_log, dt_bias], accs=[_sds(a_log.shape), _sds(dt_bias.shape)])


def _bdot(a, b, prec=None):
    return lax.dot_general(a, b, (((2,), (1,)), ((0,), (0,))), precision=prec, preferred_element_type=F32)


def _bdot_nt(a, b, prec=None):
    return lax.dot_general(a, b, (((2,), (2,)), ((0,), (0,))), precision=prec, preferred_element_type=F32)


def _bdot_tn(a, b, prec=None):
    return lax.dot_general(a, b, (((1,), (1,)), ((0,), (0,))), precision=prec, preferred_element_type=F32)


def _b16(x):
    return x.astype(BF16)


@jax.custom_vjp
def _ldot(a, b):
    return _bdot(_b16(a), _b16(b))


_ldot.defvjp(lambda a, b: (_bdot(_b16(a), _b16(b)), (a, b)),
             lambda r, ct: (_bdot_nt(_b16(ct), _b16(r[1])), _bdot_tn(_b16(r[0]), _b16(ct))))


@jax.custom_vjp
def _ldot_nt(a, b):
    return _bdot_nt(_b16(a), _b16(b))


_ldot_nt.defvjp(lambda a, b: (_bdot_nt(_b16(a), _b16(b)), (a, b)),
                lambda r, ct: (_bdot(_b16(ct), _b16(r[1])), _bdot_tn(_b16(ct), _b16(r[0]))))


@jax.custom_vjp
def _ldot_tn(a, b):
    return _bdot_tn(_b16(a), _b16(b))


_ldot_tn.defvjp(lambda a, b: (_bdot_tn(_b16(a), _b16(b)), (a, b)),
                lambda r, ct: (_bdot_nt(_b16(r[1]), _b16(ct)), _bdot(_b16(r[0]), _b16(ct))))


def _inv_raw(low):
    c = low.shape[-1]
    r = lax.broadcasted_iota(jnp.int32, (c, c), 0)
    s = lax.broadcasted_iota(jnp.int32, (c, c), 1)
    m = jnp.where(r == s, 1.0, 0.0) - low
    p = _bdot(low, low, F32X3)
    n_fac = int(math.log2(c)) - 1
    for i in range(n_fac):
        m = m + _bdot(m, p, F32X3)
        if i < n_fac - 1:
            p = _bdot(p, p, F32X3)
    return m


@jax.custom_vjp
def _inv_unit_lower(low):
    return _inv_raw(low)


def _inv_fwd(low):
    m = _inv_raw(low)
    return m, m


def _inv_bwd(m, dm):
    return (-_bdot_nt(_bdot_tn(m, dm, F32X3), m, F32X3),)


_inv_unit_lower.defvjp(_inv_fwd, _inv_bwd)


def _dn_chunk(q, k, v, gates, s):
    nh, rows, _ = q.shape
    c = DN_CHUNK
    nc = rows // c
    nb = nh * nc
    lane = lax.broadcasted_iota(jnp.int32, gates.shape, 1)

    def column(j):
        return jnp.sum(jnp.where(lane == j, gates, 0.0), axis=1, keepdims=True)[None]

    def fold(x):
        return x.reshape((nb, c) + x.shape[2:])

    g_col = fold(jnp.concatenate([column(h) for h in range(nh)], axis=0))
    b_col = fold(jnp.concatenate([column(h + nh) for h in range(nh)], axis=0))
    q, k, v = fold(q), fold(k), fold(v)
    r = lax.broadcasted_iota(jnp.int32, (c, c), 0)
    cc = lax.broadcasted_iota(jnp.int32, (c, c), 1)
    causal, strict = r >= cc, r > cc
    incl = jnp.broadcast_to(jnp.where(causal, 1.0, 0.0), (nb, c, c))
    upper = jnp.broadcast_to(jnp.where(r <= cc, 1.0, 0.0), (nb, c, c))
    gb = jnp.broadcast_to(g_col, (nb, c, LANES))
    gbc = jnp.broadcast_to(g_col, (nb, c, c))
    gc = _bdot(incl, gb, F32X3)
    gc_r = _bdot(incl, gbc, F32X3)
    gc_c = _bdot_tn(gbc, upper, F32X3)
    decay = jnp.where(causal, jnp.exp(jnp.where(causal, gc_r - gc_c, 0.0)), 0.0)
    kb = k * b_col
    low = jnp.where(strict, _ldot_nt(kb, k) * decay, 0.0)
    m = _inv_unit_lower(low)
    egc = jnp.exp(gc)
    gl = jnp.sum(gb, axis=1, keepdims=True)
    local = (_ldot(m, v * b_col), _ldot(m, kb * egc), _ldot_nt(q, k) * decay, q * egc,
             k * jnp.exp(gl - gc), jnp.exp(gl))
    outs = []
    for i in range(nc):
        u, w, attn, q_dec, k_dec, cd = (x.reshape((nh, nc) + x.shape[1:])[:, i] for x in local)
        v_new = u - _ldot(w, s)
        outs.append(_ldot(q_dec, s) + _ldot(attn, v_new))
        s = s * cd + _ldot_tn(k_dec, v_new)
    return jnp.concatenate(outs, axis=1), s


def _dn_chunks_fwd(q, k, v, gates):
    nh, t, _ = q.shape
    n = t // DN_STEP

    def kern(q_ref, k_ref, v_ref, g_ref, o_ref, sin_ref, s_scr):
        @pl.when(pl.program_id(0) == 0)
        def _():
            s_scr[...] = jnp.zeros(s_scr.shape, F32)

        s = s_scr[...]
        sin_ref[0] = s
        o_ref[...], s_scr[...] = _dn_chunk(q_ref[...], k_ref[...], v_ref[...], g_ref[...], s)

    blk = pl.BlockSpec((nh, DN_STEP, HEAD), lambda i: (0, i, 0))
    return pl.pallas_call(
        kern, name="dn_chunks_fwd", grid=(n,),
        in_specs=[blk, blk, blk, pl.BlockSpec((DN_STEP, LANES), lambda i: (i, 0))],
        out_specs=[blk, pl.BlockSpec((1, nh, HEAD, HEAD), lambda i: (i, 0, 0, 0))],
        out_shape=[_sds((nh, t, HEAD)), _sds((n, nh, HEAD, HEAD))],
        scratch_shapes=[pltpu.VMEM((nh, HEAD, HEAD), F32)],
        compiler_params=_cparams(dimension_semantics=("arbitrary",)),
    )(q, k, v, gates)


def _dn_chunks_bwd(q, k, v, gates, s_in, do):
    nh, t, _ = q.shape
    n = t // DN_STEP

    def kern(q_ref, k_ref, v_ref, g_ref, sin_ref, do_ref, dq_ref, dk_ref, dv_ref, dg_ref, ds_scr):
        @pl.when(pl.program_id(0) == 0)
        def _():
            ds_scr[...] = jnp.zeros(ds_scr.shape, F32)

        _, vjp = jax.vjp(_dn_chunk, q_ref[...], k_ref[...], v_ref[...], g_ref[...], sin_ref[0])
        dq_ref[...], dk_ref[...], dv_ref[...], dg_ref[...], ds_scr[...] = vjp((do_ref[...], ds_scr[...]))

    blk = pl.BlockSpec((nh, DN_STEP, HEAD), lambda i: (0, n - 1 - i, 0))
    gblk = pl.BlockSpec((DN_STEP, LANES), lambda i: (n - 1 - i, 0))
    hm = _sds((nh, t, HEAD))
    return pl.pallas_call(
        kern, name="dn_chunks_bwd", grid=(n,),
        in_specs=[blk, blk, blk, gblk, pl.BlockSpec((1, nh, HEAD, HEAD), lambda i: (n - 1 - i, 0, 0, 0)), blk],
        out_specs=[blk, blk, blk, gblk],
        out_shape=[hm, hm, hm, _sds((t, LANES))],
        scratch_shapes=[pltpu.VMEM((nh, HEAD, HEAD), F32)],
        compiler_params=_cparams(dimension_semantics=("arbitrary",)),
    )(q, k, v, gates, s_in, do)


def _dn_gate_out(o, z, g):
    return _rms(o, g) * _silu(z)


def _z_spec(tt):
    return (tt, N_HEADS * HEAD), lambda i: (i, DN_QKV // (N_HEADS * HEAD))


def _dn_post_fwd(o, proj, out_norm, tt=256):
    t = o.shape[1]
    tt = min(tt, t)

    def body(o_ref, z_ref, g_ref, y_ref):
        for h in range(N_HEADS):
            y_ref[:, h * HEAD:(h + 1) * HEAD] = _dn_gate_out(o_ref[h], _heads_in(z_ref, h), g_ref[...]).astype(BF16)

    return _rows("dn_post", body, [o, (proj,) + _z_spec(tt)], [_sds((t, N_HEADS * HEAD), BF16)], tt=tt, consts=[out_norm])[0]


def _dn_post_bwd(o, proj, out_norm, dy, tt=256):
    t = o.shape[1]
    tt = min(tt, t)

    def body(o_ref, z_ref, dy_ref, g_ref, do_ref, dz_ref, dg_ref):
        dg = jnp.zeros(g_ref.shape, F32)
        for h in range(N_HEADS):
            _, vjp = jax.vjp(_dn_gate_out, o_ref[h], _heads_in(z_ref, h), g_ref[...])
            a, b, d = vjp(_heads_in(dy_ref, h))
            do_ref[h] = a
            dz_ref[:, h * HEAD:(h + 1) * HEAD] = b
            dg = dg + d
        _acc(dg_ref, dg)

    return _rows("dn_post_bwd", body, [o, (proj,) + _z_spec(tt), dy], [_sds(o.shape), _sds((t, N_HEADS * HEAD))], tt=tt,
                 consts=[out_norm], accs=[_sds(out_norm.shape)])


def _dn_fwd(x, h, w_in, conv_w, a_log, dt_bias, out_norm, w_out):
    proj = _mm("dn_in", h, w_in)
    c = _dn_conv_fwd(proj, conv_w)
    q, k, v, gates = _dn_prep_fwd(c, proj, a_log, dt_bias)
    o, s_in = _dn_chunks_fwd(q, k, v, gates)
    on = _dn_post_fwd(o, proj, out_norm)
    y = _mm("dn_out", on, w_out, add=x)
    return y, (h, proj, c, q, k, v, gates, o, s_in, on)


def _dn_bwd(saved, dy, w_in, conv_w, a_log, dt_bias, out_norm, w_out):
    h, proj, c, q, k, v, gates, o, s_in, on = saved
    d_wout = _mm("dn_out_wgrad", on, dy, ta=True, out_dtype=GRAD_DTYPE)
    don = _mm("dn_out_dgrad", dy, w_out, tb=True)
    do, dz, d_out_norm = _dn_post_bwd(o, proj, out_norm, don)
    dq, dk, dv, dgates = _dn_chunks_bwd(q, k, v, gates, s_in, do)
    dc, dab, d_a_log, d_dt_bias = _dn_prep_bwd(c, proj, a_log, dt_bias, dq, dk, dv, dgates)
    dproj, d_conv_w = _dn_conv_bwd(proj, conv_w, dc, dz, dab)
    d_win = _mm("dn_in_wgrad", h, dproj, ta=True, out_dtype=GRAD_DTYPE)
    dh = _mm("dn_in_dgrad", dproj, w_in, tb=True)
    return dh, d_win, d_conv_w, d_a_log, d_dt_bias, d_out_norm, d_wout


MLA_SCALE = MLA_QK ** -0.5
MLA_C = 512


def _swap_raw(x):
    lane = lax.broadcasted_iota(jnp.int32, x.shape, 1)
    half = MLA_ROPE // 2
    y = jnp.where(lane < half, pltpu.roll(x, LANES - half, 1), pltpu.roll(x, half, 1))
    return jnp.where(lane < MLA_ROPE, y, 0.0)


@jax.custom_vjp
def _swap_halves(x):
    return _swap_raw(x)


_swap_halves.defvjp(lambda x: (_swap_raw(x), None), lambda _, d: (_swap_raw(d),))


def _rms_rope(x, g, cos, sin):
    y = x * lax.rsqrt(jnp.sum(x * x, axis=-1, keepdims=True) * (1.0 / MLA_ROPE) + NORM_EPS) * g
    return y * cos + _swap_halves(y) * sin


def _mla_latent(cq, ckv, kr, gq, gkv, gkr, cos, sin):
    return _rms(cq, gq), _rms(ckv, gkv), _rms_rope(kr, gkr, cos, sin)


def _mla_prep1_fwd(c, gq, gkv, gkr, cos, sin):
    t = c.shape[0]

    def body(c_ref, cos_ref, sin_ref, gq_ref, gkv_ref, gkr_ref, cq_ref, ckv_ref, kr_ref):
        a, b, r = _mla_latent(c_ref[:, :256], c_ref[:, 256:384], c_ref[:, 384:], gq_ref[...], gkv_ref[...], gkr_ref[...],
                              cos_ref[...], sin_ref[...])
        cq_ref[...] = a.astype(BF16)
        ckv_ref[...] = b.astype(BF16)
        kr_ref[...] = r.astype(BF16)

    return _rows("mla_prep1", body, [c, cos, sin], [_sds((t, 256), BF16), _sds((t, HEAD), BF16), _sds((t, HEAD), BF16)],
                 tt=512, consts=[gq, gkv, gkr])


def _mla_prep1_bwd(c, gq, gkv, gkr, cos, sin, dcq, dckv, dkr_heads):
    def body(c_ref, cos_ref, sin_ref, dcq_ref, dckv_ref, dkr_ref, gq_ref, gkv_ref, gkr_ref, dc_ref, dgq_ref, dgkv_ref, dgkr_ref):
        dkr = dkr_ref[0]
        for h in range(1, N_HEADS):
            dkr = dkr + dkr_ref[h]
        _, vjp = jax.vjp(_mla_latent, c_ref[:, :256], c_ref[:, 256:384], c_ref[:, 384:], gq_ref[...], gkv_ref[...], gkr_ref[...],
                         cos_ref[...], sin_ref[...])
        a, b, r, d1, d2, d3, _, _ = vjp((dcq_ref[...], dckv_ref[...], dkr))
        dc_ref[:, :256] = a.astype(BF16)
        dc_ref[:, 256:384] = b.astype(BF16)
        dc_ref[:, 384:] = r.astype(BF16)
        _acc(dgq_ref, d1)
        _acc(dgkv_ref, d2)
        _acc(dgkr_ref, d3)

    return _rows("mla_prep1_bwd", body, [c, cos, sin, dcq, dckv, dkr_heads], [_sds(c.shape, BF16)], tt=512,
                 consts=[gq, gkv, gkr], accs=[_sds(gq.shape), _sds(gkv.shape), _sds(gkr.shape)])


def _mla_heads(qn, qr, kn, gqn, gqr, gkn, cos, sin):
    return _rms(qn, gqn) * MLA_SCALE, _rms_rope(qr, gqr, cos, sin) * MLA_SCALE, _rms(kn, gkn)


def _mla_prep2_fwd(qa, kv, gqn, gqr, gkn, cos, sin):
    t = qa.shape[0]

    def body(qa_ref, kv_ref, cos_ref, sin_ref, gqn_ref, gqr_ref, gkn_ref, qn_ref, qr_ref, kn_ref, v_ref):
        for h in range(N_HEADS):
            a, b, c = _mla_heads(_heads_in(qa_ref, h), _heads_in(qa_ref, N_HEADS + h), _heads_in(kv_ref, h),
                                 gqn_ref[...], gqr_ref[...], gkn_ref[...], cos_ref[...], sin_ref[...])
            qn_ref[h] = a.astype(BF16)
            qr_ref[h] = b.astype(BF16)
            kn_ref[h] = c.astype(BF16)
            v_ref[h] = _heads_in(kv_ref, N_HEADS + h).astype(BF16)

    hm = _sds((N_HEADS, t, HEAD), BF16)
    return _rows("mla_prep2", body, [qa, kv, cos, sin], [hm, hm, hm, hm], tt=256, consts=[gqn, gqr, gkn])


def _mla_prep2_bwd(qa, kv, gqn, gqr, gkn, cos, sin, dqn, dqr, dkn, dv):
    def body(qa_ref, kv_ref, cos_ref, sin_ref, dqn_ref, dqr_ref, dkn_ref, dv_ref, gqn_ref, gqr_ref, gkn_ref,
             dqa_ref, dkv_ref, d1_ref, d2_ref, d3_ref):
        d1 = jnp.zeros(gqn_ref.shape, F32)
        d2 = jnp.zeros(gqr_ref.shape, F32)
        d3 = jnp.zeros(gkn_ref.shape, F32)
        for h in range(N_HEADS):
            _, vjp = jax.vjp(_mla_heads, _heads_in(qa_ref, h), _heads_in(qa_ref, N_HEADS + h), _heads_in(kv_ref, h),
                             gqn_ref[...], gqr_ref[...], gkn_ref[...], cos_ref[...], sin_ref[...])
            a, b, c, e1, e2, e3, _, _ = vjp((dqn_ref[h], dqr_ref[h], dkn_ref[h]))
            dqa_ref[:, h * HEAD:(h + 1) * HEAD] = a.astype(BF16)
            dqa_ref[:, (N_HEADS + h) * HEAD:(N_HEADS + h + 1) * HEAD] = b.astype(BF16)
            dkv_ref[:, h * HEAD:(h + 1) * HEAD] = c.astype(BF16)
            dkv_ref[:, (N_HEADS + h) * HEAD:(N_HEADS + h + 1) * HEAD] = dv_ref[h].astype(BF16)
            d1, d2, d3 = d1 + e1, d2 + e2, d3 + e3
        _acc(d1_ref, d1)
        _acc(d2_ref, d2)
        _acc(d3_ref, d3)

    return _rows("mla_prep2_bwd", body, [qa, kv, cos, sin, dqn, dqr, dkn, dv], [_sds(qa.shape, BF16), _sds(kv.shape, BF16)],
                 tt=256, consts=[gqn, gqr, gkn], accs=[_sds(gqn.shape), _sds(gqr.shape), _sds(gkn.shape)])


def _mla_attn_fwd(qn, qr, kn, kr, v):
    nh, t, _ = qn.shape
    bq = _q_block(t)
    per = bq // ATT_BLOCK

    def kern(qn_ref, qr_ref, kn_ref, kr_ref, v_ref, o_ref, lse_ref):
        qb = pl.program_id(1)
        qv = jnp.concatenate([qn_ref[0], qr_ref[0]], axis=1)
        order = _key_order(bq, qb)

        def body(kb, carry, diagonal):
            acc, m, l = carry
            off = pl.multiple_of(kb * ATT_BLOCK, ATT_BLOCK)
            kv = jnp.concatenate([kn_ref[0, pl.ds(off, ATT_BLOCK), :], kr_ref[pl.ds(off, ATT_BLOCK), :]], axis=1)
            s = _dot_nt(qv, kv)
            if diagonal:
                s = jnp.where(order >= kb * ATT_BLOCK, s, -jnp.inf)
            m_new = jnp.maximum(m, jnp.max(s, axis=1, keepdims=True))
            alpha = jnp.exp(m - m_new)
            p = jnp.exp(s - m_new)
            acc = acc * alpha + _dot(p.astype(BF16), v_ref[0, pl.ds(off, ATT_BLOCK), :])
            return acc, m_new, l * alpha + jnp.sum(p, axis=1, keepdims=True)

        init = (jnp.zeros((bq, HEAD), F32), jnp.full((bq, 1), -jnp.inf, F32), jnp.zeros((bq, 1), F32))
        carry = lax.fori_loop(0, qb * per, lambda i, c: body(i, c, False), init)
        acc, m, l = lax.fori_loop(qb * per, (qb + 1) * per, lambda i, c: body(i, c, True), carry)
        o_ref[...] = acc / l
        lse_ref[...] = jnp.broadcast_to(m + jnp.log(l), (bq, HEAD))

    blk = pl.BlockSpec((1, bq, HEAD), lambda h, i: (h, i, 0))
    full = pl.BlockSpec((1, t, HEAD), lambda h, i: (h, 0, 0))
    tok = pl.BlockSpec((bq, HEAD), lambda h, i: (i, h))
    return pl.pallas_call(
        kern, name="mla_attn_fwd", grid=(nh, t // bq),
        in_specs=[blk, blk, full, pl.BlockSpec((t, HEAD), lambda h, i: (0, 0)), full],
        out_specs=[tok, tok], out_shape=[_sds((t, nh * HEAD)), _sds((t, nh * HEAD))],
        compiler_params=_cparams(dimension_semantics=("parallel", "arbitrary")),
    )(qn, qr, kn, kr, v)


def _mla_attn_bwd(qn, qr, kn, kr, v, o, lse, do):
    nh, t, _ = qn.shape
    bq = _q_block(t)
    per = bq // ATT_BLOCK

    def kern(qn_ref, qr_ref, kn_ref, kr_ref, v_ref, o_ref, lse_ref, do_ref, dqn_ref, dqr_ref, dkn_ref, dkr_ref, dv_ref):
        qb = pl.program_id(1)

        @pl.when(qb == 0)
        def _():
            dkn_ref[...] = jnp.zeros(dkn_ref.shape, F32)
            dkr_ref[...] = jnp.zeros(dkr_ref.shape, F32)
            dv_ref[...] = jnp.zeros(dv_ref.shape, F32)

        qv = jnp.concatenate([qn_ref[0], qr_ref[0]], axis=1)
        dov = do_ref[...]
        dob = dov.astype(BF16)
        delta = jnp.sum(dov * o_ref[...], axis=1, keepdims=True)
        lse_col = lse_ref[:, :1]
        order = _key_order(bq, qb)

        def body(kb, dq, diagonal):
            off = pl.multiple_of(kb * ATT_BLOCK, ATT_BLOCK)
            kv = jnp.concatenate([kn_ref[0, pl.ds(off, ATT_BLOCK), :], kr_ref[pl.ds(off, ATT_BLOCK), :]], axis=1)
            vv = v_ref[0, pl.ds(off, ATT_BLOCK), :]
            p = jnp.exp(_dot_nt(qv, kv) - lse_col)
            if diagonal:
                p = jnp.where(order >= kb * ATT_BLOCK, p, 0.0)
            ds = (p * (_dot_nt(dob, vv) - delta)).astype(BF16)
            dk = _dot_tn(ds, qv)
            dkn_ref[0, pl.ds(off, ATT_BLOCK), :] += dk[:, :HEAD]
            dkr_ref[0, pl.ds(off, ATT_BLOCK), :] += dk[:, HEAD:]
            dv_ref[0, pl.ds(off, ATT_BLOCK), :] += _dot_tn(p.astype(BF16), dob)
            return dq + _dot(ds, kv)

        dq = lax.fori_loop(0, qb * per, lambda i, c: body(i, c, False), jnp.zeros((bq, 2 * HEAD), F32))
        dq = lax.fori_loop(qb * per, (qb + 1) * per, lambda i, c: body(i, c, True), dq)
        dqn_ref[0] = dq[:, :HEAD]
        dqr_ref[0] = dq[:, HEAD:]

    hm = _sds((nh, t, HEAD))
    blk = pl.BlockSpec((1, bq, HEAD), lambda h, i: (h, i, 0))
    full = pl.BlockSpec((1, t, HEAD), lambda h, i: (h, 0, 0))
    tok = pl.BlockSpec((bq, HEAD), lambda h, i: (i, h))
    return pl.pallas_call(
        kern, name="mla_attn_bwd", grid=(nh, t // bq),
        in_specs=[blk, blk, full, pl.BlockSpec((t, HEAD), lambda h, i: (0, 0)), full, tok, tok, tok],
        out_specs=[blk, blk, full, full, full], out_shape=[hm, hm, hm, hm, hm],
        compiler_params=_cparams(dimension_semantics=("parallel", "arbitrary")),
    )(qn, qr, kn, kr, v, o, lse, do)


def _rope_tables(t):
    inv_freq = ROPE_THETA ** (-jnp.arange(0, MLA_ROPE, 2, dtype=F32) / MLA_ROPE)
    ang = jnp.arange(t, dtype=F32)[:, None] * inv_freq[None, :]
    c, s = jnp.cos(ang), jnp.sin(ang)
    pad = ((0, 0), (0, LANES - MLA_ROPE))
    return jnp.pad(jnp.concatenate([c, c], axis=1), pad), jnp.pad(jnp.concatenate([-s, s], axis=1), pad)


def _pad_lanes(v, n=LANES):
    return jnp.pad(v, (0, n - v.shape[0])).reshape(1, n)


def _mla_layout(w_down, w_uq, w_ukv):
    w_down_p = jnp.pad(w_down, ((0, 0), (0, MLA_C - w_down.shape[1])))
    uq = w_uq.reshape(w_uq.shape[0], N_HEADS, MLA_QK)
    rope = jnp.pad(uq[:, :, HEAD:], ((0, 0), (0, 0), (0, LANES - MLA_ROPE)))
    w_uq_p = jnp.concatenate([uq[:, :, :HEAD].reshape(-1, N_HEADS * HEAD), rope.reshape(-1, N_HEADS * LANES)], axis=1)
    ukv = w_ukv.reshape(w_ukv.shape[0], N_HEADS, 2 * HEAD)
    w_ukv_p = jnp.concatenate([ukv[:, :, :HEAD].reshape(-1, N_HEADS * HEAD), ukv[:, :, HEAD:].reshape(-1, N_HEADS * HEAD)], axis=1)
    return w_down_p, w_uq_p, w_ukv_p


def _mla_unlayout(d_down_p, d_uq_p, d_ukv_p):
    d_down = d_down_p[:, :256 + HEAD + MLA_ROPE]
    nope = d_uq_p[:, :N_HEADS * HEAD].reshape(-1, N_HEADS, HEAD)
    rope = d_uq_p[:, N_HEADS * HEAD:].reshape(-1, N_HEADS, LANES)[:, :, :MLA_ROPE]
    d_uq = jnp.concatenate([nope, rope], axis=2).reshape(-1, N_HEADS * MLA_QK)
    kn = d_ukv_p[:, :N_HEADS * HEAD].reshape(-1, N_HEADS, HEAD)
    vv = d_ukv_p[:, N_HEADS * HEAD:].reshape(-1, N_HEADS, HEAD)
    d_ukv = jnp.concatenate([kn, vv], axis=2).reshape(-1, N_HEADS * 2 * HEAD)
    return d_down, d_uq, d_ukv


def _mla_weight_shapes():
    return (_sds((1024, MLA_C), BF16), _sds((1, 256)), _sds((1, HEAD)), _sds((256, 2048), BF16), _sds((HEAD, 2048), BF16),
            _sds((1, HEAD)), _sds((1, HEAD)), _sds((1, HEAD)), _sds((1, HEAD)), _sds((1024, 1024), BF16),
            _sds((4096, HEAD)), _sds((4096, HEAD)))


def _mla_fwd(x, h, w_down, gq, gkv, w_uq, w_ukv, gqn, gqr, gkn, gkr, w_out, cos, sin):
    c = _mm("mla_down", h, w_down)
    cq, ckv, kr = _mla_prep1_fwd(c, gq, gkv, gkr, cos, sin)
    qa = _mm("mla_uq", cq, w_uq)
    kv = _mm("mla_ukv", ckv, w_ukv)
    qn, qr, kn, v = _mla_prep2_fwd(qa, kv, gqn, gqr, gkn, cos, sin)
    o, lse = _mla_attn_fwd(qn, qr, kn, kr, v)
    y = _mm("mla_out", o, w_out, add=x)
    return y, (h, c, cq, ckv, kr, qa, kv, qn, qr, kn, v, o, lse)


def _mla_bwd(saved, dy, w_down, gq, gkv, w_uq, w_ukv, gqn, gqr, gkn, gkr, w_out, cos, sin):
    h, c, cq, ckv, kr, qa, kv, qn, qr, kn, v, o, lse = saved
    d_wout = _mm("mla_out_wgrad", o, dy, ta=True, out_dtype=GRAD_DTYPE)
    do = _mm("mla_out_dgrad", dy, w_out, tb=True)
    dqn, dqr, dkn, dkr, dv = _mla_attn_bwd(qn, qr, kn, kr, v, o, lse, do)
    dqa, dkv, dgqn, dgqr, dgkn = _mla_prep2_bwd(qa, kv, gqn, gqr, gkn, cos, sin, dqn, dqr, dkn, dv)
    d_wuq = _mm("mla_uq_wgrad", cq, dqa, ta=True, out_dtype=GRAD_DTYPE)
    d_wukv = _mm("mla_ukv_wgrad", ckv, dkv, ta=True, out_dtype=GRAD_DTYPE)
    dcq = _mm("mla_uq_dgrad", dqa, w_uq, tb=True)
    dckv = _mm("mla_ukv_dgrad", dkv, w_ukv, tb=True)
    dc, dgq, dgkv, dgkr = _mla_prep1_bwd(c, gq, gkv, gkr, cos, sin, dcq, dckv, dkr)
    d_wdown = _mm("mla_down_wgrad", h, dc, ta=True, out_dtype=GRAD_DTYPE)
    dh = _mm("mla_down_dgrad", dc, w_down, tb=True)
    return dh, d_wdown, dgq, dgkv, d_wuq, d_wukv, dgqn, dgqr, dgkn, dgkr, d_wout


def _loss_head(y, target):
    d = y.shape[1]

    def body(y_ref, t_ref, dy_ref, l_ref):
        err = y_ref[...] - t_ref[...]
        dy_ref[...] = err * (1.0 / d)
        part = 0.5 * jnp.sum(jnp.sum(err * err, axis=1, keepdims=True) * (1.0 / d), axis=0, keepdims=True)
        _acc(l_ref, jnp.broadcast_to(part, (1, LANES)))

    return _rows("loss_head", body, [y, target], [_sds(y.shape)], tt=512, accs=[_sds((1, LANES))])


MESH_ID = pl.DeviceIdType.MESH
HBM_SPEC = pl.BlockSpec(memory_space=pltpu.HBM)


def _all_gather(name, x):
    m_per, n = x.shape

    def body(x_ref, out_ref, send_sems, recv_sems, local_sem):
        x, y, c = lax.axis_index("x"), lax.axis_index("y"), lax.axis_index("c")
        me, sibling = (x, y, c), (x, y, 1 - c)
        chips = [(1 - x, y), (x, 1 - y), (1 - x, 1 - y)]

        def rows(px, py, pc):
            return out_ref.at[pl.ds((4 * px + 2 * py + pc) * m_per, m_per), :]

        def copy(k, block, to, src=None):
            return pltpu.make_async_remote_copy(
                src_ref=rows(*block) if src is None else src, dst_ref=rows(*block),
                send_sem=send_sems.at[k], recv_sem=recv_sems.at[k], device_id=to, device_id_type=MESH_ID)

        mine = pltpu.make_async_copy(x_ref, rows(*me), local_sem)
        mine.start()
        first = [copy(0, me, sibling, src=x_ref)]
        first += [copy(1 + j, me, (*chip, c), src=x_ref) for j, chip in enumerate(chips)]
        for cp in first:
            cp.start()
        passed = [copy(4 + j, (*chip, c), sibling) for j, chip in enumerate(chips)]
        for j, chip in enumerate(chips):
            copy(1 + j, (*chip, c), me).wait_recv()
            passed[j].start()
        copy(0, sibling, me).wait_recv()
        for j, chip in enumerate(chips):
            copy(4 + j, (*chip, 1 - c), me).wait_recv()
        for cp in first + passed:
            cp.wait_send()
        mine.wait()

    return pl.pallas_call(
        body, name=name,
        out_shape=jax.ShapeDtypeStruct((N_DEV * m_per, n), x.dtype),
        in_specs=[HBM_SPEC], out_specs=HBM_SPEC,
        scratch_shapes=[pltpu.SemaphoreType.DMA((7,)), pltpu.SemaphoreType.DMA((7,)), pltpu.SemaphoreType.DMA],
    )(x)


def _all_gather_groups(name, xs):
    ng = len(xs)

    def body(*refs):
        x_refs, out_refs, token = refs[:ng], refs[ng:2 * ng], refs[2 * ng]
        send_sems, recv_sems, local_sems = refs[2 * ng + 1:]
        token[...] = jnp.zeros(token.shape, F32)
        x, y, c = lax.axis_index("x"), lax.axis_index("y"), lax.axis_index("c")
        me, sibling = (x, y, c), (x, y, 1 - c)
        chips = [(1 - x, y), (x, 1 - y), (1 - x, 1 - y)]

        def copy(g, k, block, to, src=None):
            px, py, pc = block
            dst = out_refs[g].at[4 * px + 2 * py + pc]
            return pltpu.make_async_remote_copy(
                src_ref=dst if src is None else src, dst_ref=dst,
                send_sem=send_sems.at[g, k], recv_sem=recv_sems.at[g, k], device_id=to, device_id_type=MESH_ID)

        mine = [pltpu.make_async_copy(x_refs[g], out_refs[g].at[4 * x + 2 * y + c], local_sems.at[g]) for g in range(ng)]
        for cp in mine:
            cp.start()
        first = []
        for g in range(ng):
            first.append(copy(g, 0, me, sibling, src=x_refs[g]))
            first += [copy(g, 1 + j, me, (*chip, c), src=x_refs[g]) for j, chip in enumerate(chips)]
        for cp in first:
            cp.start()
        passed = []
        for j, chip in enumerate(chips):
            for g in range(ng):
                copy(g, 1 + j, (*chip, c), me).wait_recv()
                passed.append(copy(g, 4 + j, (*chip, c), sibling))
                passed[-1].start()
        for g in range(ng):
            copy(g, 0, sibling, me).wait_recv()
            for j, chip in enumerate(chips):
                copy(g, 4 + j, (*chip, 1 - c), me).wait_recv()
        for cp in first + passed:
            cp.wait_send()
        for cp in mine:
            cp.wait()

    return pl.pallas_call(
        body, name=name,
        out_shape=[jax.ShapeDtypeStruct((N_DEV,) + x.shape, x.dtype) for x in xs] + [_sds((8, LANES))],
        in_specs=[HBM_SPEC] * ng, out_specs=[HBM_SPEC] * ng + [pl.BlockSpec(memory_space=pltpu.VMEM)],
        scratch_shapes=[pltpu.SemaphoreType.DMA((ng, 7)), pltpu.SemaphoreType.DMA((ng, 7)), pltpu.SemaphoreType.DMA((ng,))],
    )(*xs)


EFFECT = pltpu.SideEffectType.DATAFLOW_SIDE_EFFECTING
SEM_SPEC = pl.BlockSpec(memory_space=pltpu.SEMAPHORE)


def _push_copies(src_refs, land_refs, send_sems, recv_sems, chunked):
    x, y, c = lax.axis_index("x"), lax.axis_index("y"), lax.axis_index("c")
    me = 4 * x + 2 * y + c
    copies = []
    for g, (src, land) in enumerate(zip(src_refs, land_refs)):
        for k in range(1, N_DEV):
            px = 1 - x if k & 4 else x
            py = 1 - y if k & 2 else y
            pc = 1 - c if k & 1 else c
            copies.append(pltpu.make_async_remote_copy(
                src_ref=src.at[4 * px + 2 * py + pc] if chunked else src, dst_ref=land.at[me],
                send_sem=send_sems.at[g * (N_DEV - 1) + k - 1], recv_sem=recv_sems.at[g * (N_DEV - 1) + k - 1],
                device_id=(px, py, pc), device_id_type=MESH_ID))
    return copies


def _hbm(x):
    return pltpu.with_memory_space_constraint(x, pltpu.HBM)


def _push_start(name, srcs, lands, chunked):
    ng = len(srcs)

    def body(*refs):
        for cp in _push_copies(refs[:ng], refs[ng:2 * ng], refs[2 * ng], refs[2 * ng + 1], chunked):
            cp.start()
        refs[-1][...] = jnp.zeros(refs[-1].shape, F32)

    bufs = list(srcs) + list(lands)
    outs = pl.pallas_call(
        body, name=name,
        out_shape=(pltpu.SemaphoreType.DMA((ng * (N_DEV - 1),)), pltpu.SemaphoreType.DMA((ng * (N_DEV - 1),)),
                   *[pltpu.HBM(b.shape, b.dtype) for b in bufs], jax.ShapeDtypeStruct((8, LANES), F32)),
        in_specs=[HBM_SPEC] * (2 * ng),
        out_specs=(SEM_SPEC, SEM_SPEC, *[HBM_SPEC] * (2 * ng), pl.BlockSpec(memory_space=pltpu.VMEM)),
        input_output_aliases={i: 2 + i for i in range(2 * ng)},
        compiler_params=pltpu.CompilerParams(has_side_effects=EFFECT),
    )(*[_hbm(b) for b in bufs])
    return outs[0], outs[1], list(outs[2:2 + ng]), list(outs[2 + ng:2 + 2 * ng]), outs[-1]


def _push_wait(name, started, after, chunked):
    send_sems, recv_sems, srcs, lands, _ = started
    ng = len(srcs)

    def body(*refs):
        copies = _push_copies(refs[:ng], refs[ng:2 * ng], refs[2 * ng], refs[2 * ng + 1], chunked)
        for cp in copies:
            cp.wait_send()
        for cp in copies:
            cp.wait_recv()

    bufs = srcs + lands
    outs = pl.pallas_call(
        body, name=name,
        out_shape=tuple(pltpu.HBM(b.shape, b.dtype) for b in bufs),
        in_specs=[HBM_SPEC] * (2 * ng) + [SEM_SPEC, SEM_SPEC, pl.BlockSpec(memory_space=pl.ANY)],
        out_specs=tuple([HBM_SPEC] * (2 * ng)),
        input_output_aliases={i: i for i in range(2 * ng)},
        compiler_params=pltpu.CompilerParams(has_side_effects=EFFECT),
    )(*bufs, send_sems, recv_sems, after)
    return list(outs[:ng]), list(outs[ng:])


def _sum_adam_devices(sent, recv, dev, w, m, v):
    ndev, r, c_ = recv.shape
    tr = _pick(r, (256, 128, 96, 32))

    def body(dev_ref, own_ref, r_ref, w_ref, m_ref, v_ref, g_ref, d_ref, mo_ref, vo_ref):
        me = dev_ref[0]
        g = jnp.where(me == 0, own_ref[0], r_ref[0]).astype(F32)
        for j in range(1, ndev):
            g = g + jnp.where(me == j, own_ref[0], r_ref[j]).astype(F32)
        g_ref[...] = g
        d_ref[...], mo_ref[...], vo_ref[...] = _adam(w_ref[...], g, m_ref[...], v_ref[...])

    row = pl.BlockSpec((tr, c_), lambda i, dev_ref: (i, 0))
    return pl.pallas_call(
        body, name="sum_adam",
        grid_spec=pltpu.PrefetchScalarGridSpec(
            num_scalar_prefetch=1, grid=(r // tr,),
            in_specs=[pl.BlockSpec((1, tr, c_), lambda i, dev_ref: (dev_ref[0], i, 0)),
                      pl.BlockSpec((ndev, tr, c_), lambda i, dev_ref: (0, i, 0)), row, row, row],
            out_specs=[row, row, row, row]),
        out_shape=[_sds((r, c_))] * 4,
        compiler_params=_cparams(dimension_semantics=("arbitrary",)),
    )(dev, sent, recv, w, m, v)


def _cols_from_shards(w, width):
    ns, r, cs = w.shape
    tr = _pick(r, (256, 128))

    def body(w_ref, o_ref):
        parts = [w_ref[j] for j in range(ns)]
        if width > ns * cs:
            parts.append(jnp.zeros((tr, width - ns * cs), w.dtype))
        o_ref[...] = jnp.concatenate(parts, axis=1)

    return pl.pallas_call(
        body, name="cols_from_shards", grid=(r // tr,),
        in_specs=[pl.BlockSpec((ns, tr, cs), lambda i: (0, i, 0))], out_specs=pl.BlockSpec((tr, width), lambda i: (i, 0)),
        out_shape=jax.ShapeDtypeStruct((r, width), w.dtype), compiler_params=_cparams(dimension_semantics=("arbitrary",)),
    )(w)


def _shards_from_cols(g, cs):
    r, width = g.shape
    tr = _pick(r, (256, 128))

    def body(g_ref, o_ref):
        for j in range(N_DEV):
            o_ref[j] = g_ref[:, j * cs:(j + 1) * cs]

    return pl.pallas_call(
        body, name="shards_from_cols", grid=(r // tr,),
        in_specs=[pl.BlockSpec((tr, width), lambda i: (i, 0))], out_specs=pl.BlockSpec((N_DEV, tr, cs), lambda i: (0, i, 0)),
        out_shape=jax.ShapeDtypeStruct((N_DEV, r, cs), g.dtype), compiler_params=_cparams(dimension_semantics=("arbitrary",)),
    )(g)


def _adam(w, g, m, v):
    m = ADAM_B1 * m + (1.0 - ADAM_B1) * g
    v = ADAM_B2 * v + (1.0 - ADAM_B2) * (g * g)
    m_hat = m / (1.0 - ADAM_B1 ** ADAM_STEP)
    v_hat = v / (1.0 - ADAM_B2 ** ADAM_STEP)
    return -ADAM_LR * (m_hat / (jnp.sqrt(v_hat) + ADAM_EPS) + ADAM_WD * w), m, v


def _sum_devices(gathered):
    m_all, n = gathered.shape
    m_per = m_all // N_DEV

    def body(x_ref, o_ref):
        s = x_ref[0:m_per, :]
        for j in range(1, N_DEV):
            s = s + x_ref[j * m_per:(j + 1) * m_per, :]
        o_ref[...] = s

    return pl.pallas_call(body, name="sum_devices", out_shape=_sds((m_per, n)), compiler_params=_cparams())(gathered)


def _adam_small(w, g, m, v):
    def body(w_ref, g_ref, m_ref, v_ref, d_ref, mo_ref, vo_ref):
        d_ref[...], mo_ref[...], vo_ref[...] = _adam(w_ref[...], g_ref[...], m_ref[...], v_ref[...])

    return pl.pallas_call(body, name="adam_small", out_shape=[_sds(w.shape)] * 3, compiler_params=_cparams())(w, g, m, v)


N_LAYERS = 4
_MIXER = ("dn", "sb", "mla")
_MIXER_PARAMS = {
    "dn": ("dn_w_in", "dn_conv_w", "dn_a_log", "dn_dt_bias", "dn_out_norm", "dn_w_out"),
    "sb": ("sb_w_qkv", "sb_q_norm", "sb_k_norm", "sb_w_out"),
    "mla": ("mla_w_down", "mla_q_a_norm", "mla_kv_a_norm", "mla_w_uq", "mla_w_ukv", "mla_q_nope_norm", "mla_q_rope_norm",
            "mla_k_nope_norm", "mla_k_rope_norm", "mla_w_out"),
}
_BIG_AXIS = {"dn_w_in": 1, "dn_w_out": 0, "sb_w_qkv": 1, "sb_w_out": 0, "mla_w_down": 0, "mla_w_uq": 1, "mla_w_ukv": 1,
             "mla_w_out": 0, "ffn_w_gate_up": 1, "ffn_w_down": 0}


def _weight_names():
    names = []
    for i in range(N_LAYERS):
        p = "l%d_" % i
        names += [p + "mix_norm"] + [p + n for n in _MIXER_PARAMS[_MIXER[i % 3]]] + [p + "ffn_norm", p + "ffn_w_gate_up", p + "ffn_w_down"]
    return names


WEIGHTS = _weight_names()
BIG = [n for n in WEIGHTS if n[3:] in _BIG_AXIS]
SMALL = [n for n in WEIGHTS if n[3:] not in _BIG_AXIS]
CONV = [n for n in SMALL if n.endswith("conv_w")]


def _ceil_to(n, k):
    return -(-n // k) * k


def _pack(arrs, cols, row_mult):
    parts = []
    for a in arrs:
        f = a.reshape(-1)
        parts.append(jnp.pad(f, (0, _ceil_to(f.shape[0], cols) - f.shape[0])))
    flat = jnp.concatenate(parts)
    rows = _ceil_to(flat.shape[0] // cols, row_mult)
    return jnp.pad(flat, (0, rows * cols - flat.shape[0])).reshape(rows, cols)


def _unpack(buf, shapes):
    cols = buf.shape[-1]
    out, r0 = [], 0
    for s in shapes:
        n = math.prod(s)
        nr = _ceil_to(n, cols) // cols
        out.append(buf[r0:r0 + nr].reshape(-1)[:n].reshape(s))
        r0 += nr
    return out


def _layer_groups(i):
    by = {"gu": (704, []), "down": (1024, []), "out": (1024, []), "dn_in": (514, []), "sb_qkv": (384, []), "mla": (512, [])}
    key = {"ffn_w_gate_up": "gu", "ffn_w_down": "down", "dn_w_in": "dn_in", "sb_w_qkv": "sb_qkv", "mla_w_down": "mla",
           "mla_w_uq": "mla", "mla_w_ukv": "mla"}
    for n in BIG:
        if n.startswith("l%d_" % i):
            by[key.get(n[3:], "out")][1].append(n)
    return [g for g in by.values() if g[1]]


LAYER_GROUPS = [_layer_groups(i) for i in range(N_LAYERS)]
N_FFN_GROUPS = 2


def _stack_group(grp, get):
    width, names = grp
    if len(names) == 1 and get(names[0]).shape[1] == width:
        return get(names[0])
    return jnp.concatenate([jnp.pad(get(n), ((0, 0), (0, width - get(n).shape[1]))) for n in names], axis=0)


def _unstack_group(grp, buf, shape_of):
    if len(grp[1]) == 1 and tuple(shape_of(grp[1][0])) == tuple(buf.shape[-2:]):
        return [buf]
    out, r0 = [], 0
    for n in grp[1]:
        rs, cs = shape_of(n)
        out.append(buf[..., r0:r0 + rs, :cs])
        r0 += rs
    return out


def kernel(x, l0_mix_norm, l0_dn_w_in, l0_dn_conv_w, l0_dn_a_log, l0_dn_dt_bias, l0_dn_out_norm, l0_dn_w_out, l0_ffn_norm, l0_ffn_w_gate_up, l0_ffn_w_down, l1_mix_norm, l1_sb_w_qkv, l1_sb_q_norm, l1_sb_k_norm, l1_sb_w_out, l1_ffn_norm, l1_ffn_w_gate_up, l1_ffn_w_down, l2_mix_norm, l2_mla_w_down, l2_mla_q_a_norm, l2_mla_kv_a_norm, l2_mla_w_uq, l2_mla_w_ukv, l2_mla_q_nope_norm, l2_mla_q_rope_norm, l2_mla_k_nope_norm, l2_mla_k_rope_norm, l2_mla_w_out, l2_ffn_norm, l2_ffn_w_gate_up, l2_ffn_w_down, l3_mix_norm, l3_dn_w_in, l3_dn_conv_w, l3_dn_a_log, l3_dn_dt_bias, l3_dn_out_norm, l3_dn_w_out, l3_ffn_norm, l3_ffn_w_gate_up, l3_ffn_w_down, loss_target, m_l0_mix_norm, m_l0_dn_w_in, m_l0_dn_conv_w, m_l0_dn_a_log, m_l0_dn_dt_bias, m_l0_dn_out_norm, m_l0_dn_w_out, m_l0_ffn_norm, m_l0_ffn_w_gate_up, m_l0_ffn_w_down, m_l1_mix_norm, m_l1_sb_w_qkv, m_l1_sb_q_norm, m_l1_sb_k_norm, m_l1_sb_w_out, m_l1_ffn_norm, m_l1_ffn_w_gate_up, m_l1_ffn_w_down, m_l2_mix_norm, m_l2_mla_w_down, m_l2_mla_q_a_norm, m_l2_mla_kv_a_norm, m_l2_mla_w_uq, m_l2_mla_w_ukv, m_l2_mla_q_nope_norm, m_l2_mla_q_rope_norm, m_l2_mla_k_nope_norm, m_l2_mla_k_rope_norm, m_l2_mla_w_out, m_l2_ffn_norm, m_l2_ffn_w_gate_up, m_l2_ffn_w_down, m_l3_mix_norm, m_l3_dn_w_in, m_l3_dn_conv_w, m_l3_dn_a_log, m_l3_dn_dt_bias, m_l3_dn_out_norm, m_l3_dn_w_out, m_l3_ffn_norm, m_l3_ffn_w_gate_up, m_l3_ffn_w_down, v_l0_mix_norm, v_l0_dn_w_in, v_l0_dn_conv_w, v_l0_dn_a_log, v_l0_dn_dt_bias, v_l0_dn_out_norm, v_l0_dn_w_out, v_l0_ffn_norm, v_l0_ffn_w_gate_up, v_l0_ffn_w_down, v_l1_mix_norm, v_l1_sb_w_qkv, v_l1_sb_q_norm, v_l1_sb_k_norm, v_l1_sb_w_out, v_l1_ffn_norm, v_l1_ffn_w_gate_up, v_l1_ffn_w_down, v_l2_mix_norm, v_l2_mla_w_down, v_l2_mla_q_a_norm, v_l2_mla_kv_a_norm, v_l2_mla_w_uq, v_l2_mla_w_ukv, v_l2_mla_q_nope_norm, v_l2_mla_q_rope_norm, v_l2_mla_k_nope_norm, v_l2_mla_k_rope_norm, v_l2_mla_w_out, v_l2_ffn_norm, v_l2_ffn_w_gate_up, v_l2_ffn_w_down, v_l3_mix_norm, v_l3_dn_w_in, v_l3_dn_conv_w, v_l3_dn_a_log, v_l3_dn_dt_bias, v_l3_dn_out_norm, v_l3_dn_w_out, v_l3_ffn_norm, v_l3_ffn_w_gate_up, v_l3_ffn_w_down):
    a = dict(locals())
    return _train_step(a)


def _train_step(a):
    mx, my, mc = lax.axis_index("x"), lax.axis_index("y"), lax.axis_index("c")
    dev = 4 * mx + 2 * my + mc
    dev_arr = jnp.reshape(dev, (1,)).astype(jnp.int32)
    t, d = a["x"].shape[1], a["x"].shape[2]
    xs = a["x"].reshape(t, d)
    target = a["loss_target"].reshape(t, d)

    full = {}

    def unpack(groups, bufs):
        for grp, buf in zip(groups, bufs):
            for n, shards in zip(grp[1], _unstack_group(grp, buf, lambda n: a[n].shape)):
                kind = n[3:]
                if kind == "ffn_w_gate_up":
                    full[n] = shards
                elif _BIG_AXIS[kind] == 0:
                    full[n] = shards.reshape(N_DEV * shards.shape[1], shards.shape[2])
                else:
                    width = DN_PROJ if kind == "dn_w_in" else N_DEV * shards.shape[2]
                    full[n] = _cols_from_shards(shards, width)

    def local_shards(groups):
        return [_stack_group(grp, lambda n: a[n].astype(BF16)) for grp in groups]

    pushed_groups = {"l0_gate_up": LAYER_GROUPS[0][:1], "l0_down": LAYER_GROUPS[0][1:N_FFN_GROUPS]}
    pushed_groups.update({"l%d" % i: LAYER_GROUPS[i] for i in range(1, N_LAYERS)})
    conv_pack = _pack([a[n] for n in CONV], LANES, 8)
    first_groups = LAYER_GROUPS[0][N_FFN_GROUPS:]
    *first_bufs, conv_all, gathered = _all_gather_groups("gather_weights", local_shards(first_groups) + [conv_pack])
    unpack(first_groups, first_bufs)
    for n, parts in zip(CONV, zip(*[_unpack(conv_all[j], [a[n].shape for n in CONV]) for j in range(N_DEV)])):
        full[n] = jnp.concatenate(parts, axis=1)
    after_first = gathered[0, 0].astype(BF16)
    gathers, started = {}, jnp.zeros((), F32)
    for tag, groups in pushed_groups.items():
        srcs = local_shards(groups)
        srcs[-1] = srcs[-1] + after_first
        lands = [lax.dynamic_update_index_in_dim(lax.empty((N_DEV,) + s.shape, s.dtype), s, dev, 0) for s in srcs]
        gathers[tag] = _push_start("gather_start_" + tag, srcs, lands, False)
        started = started + gathers[tag][-1][0, 0]

    def arrived(tag, after):
        unpack(pushed_groups[tag], _push_wait("gather_wait_" + tag, gathers[tag], after, False)[1])

    def vec(n):
        return a[n].reshape(1, -1)

    cos, sin = _rope_tables(t)

    def mixer_args(i):
        p = "l%d_" % i
        kind = _MIXER[i % 3]
        if kind == "dn":
            args = (full[p + "dn_w_in"], full[p + "dn_conv_w"], _pad_lanes(a[p + "dn_a_log"]), _pad_lanes(a[p + "dn_dt_bias"]),
                    vec(p + "dn_out_norm"), full[p + "dn_w_out"])
        elif kind == "sb":
            args = (full[p + "sb_w_qkv"], vec(p + "sb_q_norm"), vec(p + "sb_k_norm"), full[p + "sb_w_out"])
        else:
            w_down, w_uq, w_ukv = _mla_layout(full[p + "mla_w_down"], full[p + "mla_w_uq"], full[p + "mla_w_ukv"])
            args = (w_down, vec(p + "mla_q_a_norm"), vec(p + "mla_kv_a_norm"), w_uq, w_ukv, vec(p + "mla_q_nope_norm"),
                    _pad_lanes(a[p + "mla_q_rope_norm"]), vec(p + "mla_k_nope_norm"), _pad_lanes(a[p + "mla_k_rope_norm"]),
                    full[p + "mla_w_out"], cos, sin)
        return kind, args

    fwd = {"dn": _dn_fwd, "sb": _sb_fwd, "mla": _mla_fwd}
    bwd = {"dn": _dn_bwd, "sb": _sb_bwd, "mla": _mla_bwd}
    saved, layer_args = [], []
    for i in range(N_LAYERS):
        p = "l%d_" % i
        if i > 0:
            arrived("l%d" % i, xs)
        kind, args = mixer_args(i)
        layer_args.append((kind, args))
        gain = vec(p + "mix_norm") + started if i == 0 else vec(p + "mix_norm")
        h = _rmsnorm_fwd("mix_norm", xs, gain)
        x_mid, sv_mix = fwd[kind](xs, h, *args)
        if i == 0:
            arrived("l0_gate_up", x_mid)

        def down_weight(act, p=p, first=(i == 0)):
            if first:
                arrived("l0_down", act)
            return full[p + "ffn_w_down"]

        x_out, sv_ffn = _ffn_fwd(x_mid, vec(p + "ffn_norm"), full[p + "ffn_w_gate_up"], down_weight)
        saved.append((xs, sv_mix, sv_ffn))
        xs = x_out
    dy, loss_part = _loss_head(xs, target)

    grads, big_out = {}, {}

    def grad_shards(n):
        g, (rs, cs) = grads[n], a[n].shape
        if g.ndim == 3:
            return g
        if _BIG_AXIS[n[3:]] == 0:
            return g.reshape(N_DEV, rs, cs)
        return _shards_from_cols(g, cs)

    def push_grads(tag, groups):
        sends = []
        for grp in groups:
            if len(grp[1]) == 1 and a[grp[1][0]].shape[1] == grp[0]:
                sends.append(grad_shards(grp[1][0]))
                continue
            parts = [jnp.pad(grad_shards(n), ((0, 0), (0, 0), (0, grp[0] - a[n].shape[1]))) for n in grp[1]]
            sends.append(jnp.concatenate(parts, axis=1))
        lands = [lax.empty(s.shape, s.dtype) for s in sends]
        return tag, groups, _push_start("grads_start_" + tag, sends, lands, True)

    def finish_grads(push, after):
        tag, groups, pushed = push
        sents, recvs = _push_wait("grads_wait_" + tag, pushed, after, True)
        for grp, sent, recv in zip(groups, sents, recvs):
            packs = [_stack_group(grp, lambda n, pre=pre: a[pre + n]) for pre in ("", "m_", "v_")]
            outs = [_unstack_group(grp, o, lambda n: a[n].shape) for o in _sum_adam_devices(sent, recv, dev_arr, *packs)]
            for j, n in enumerate(grp[1]):
                big_out[n] = [o[j] for o in outs]

    mixer_push = None
    for i in reversed(range(N_LAYERS)):
        p = "l%d_" % i
        kind, args = layer_args[i]
        x_in, sv_mix, sv_ffn = saved[i]
        gain = vec(p + "ffn_norm") if mixer_push is None else vec(p + "ffn_norm") + mixer_push[2][-1][0, 0]
        dx_mid, grads[p + "ffn_norm"], grads[p + "ffn_w_gate_up"], grads[p + "ffn_w_down"] = _ffn_bwd(
            sv_ffn, dy, gain, full[p + "ffn_w_gate_up"], full[p + "ffn_w_down"])
        ffn_push = push_grads("l%d_ffn" % i, LAYER_GROUPS[i][:N_FFN_GROUPS])
        if mixer_push is not None:
            finish_grads(mixer_push, dx_mid)
        res = bwd[kind](sv_mix, dx_mid, *args)
        dh = res[0]
        if kind == "mla":
            res = list(res)
            res[1], res[4], res[5] = _mla_unlayout(res[1], res[4], res[5])
        for n, g in zip(_MIXER_PARAMS[kind], res[1:]):
            grads[p + n] = g
        mixer_push = push_grads("l%d_mix" % i, LAYER_GROUPS[i][N_FFN_GROUPS:])
        pushed = ffn_push[2][-1][0, 0] + mixer_push[2][-1][0, 0]
        dy, grads[p + "mix_norm"] = _rmsnorm_bwd("mix_norm_bwd", x_in, vec(p + "mix_norm") + pushed, dh, dx_mid)
        finish_grads(ffn_push, dy)
    grad_x = dy.reshape(a["x"].shape)

    small_full_shapes = [full[n].shape if n in CONV else a[n].shape for n in SMALL]
    small_grads = []
    for n, s in zip(SMALL, small_full_shapes):
        g = grads[n].reshape(-1)
        small_grads.append(g[:math.prod(s)])
    small_pack = _pack(small_grads + [loss_part.reshape(-1)], LANES, 8)
    small_sum = _sum_devices(_all_gather("gather_small_grads", small_pack))
    small_red = _unpack(small_sum, small_full_shapes + [(LANES,)])
    loss = small_red[-1][0]
    g_small = {}
    for n, g in zip(SMALL, small_red[:-1]):
        if n in CONV:
            cs = a[n].shape[1]
            g = lax.dynamic_slice_in_dim(g, dev * cs, cs, axis=1)
        g_small[n] = g
    small_shapes = [a[n].shape for n in SMALL]
    packs = [_pack([src[n] for n in SMALL], LANES, 8) for src in
             ({n: a[n] for n in SMALL}, g_small, {n: a["m_" + n] for n in SMALL}, {n: a["v_" + n] for n in SMALL})]
    d_small, m_small, v_small = (_unpack(o, small_shapes) for o in _adam_small(*packs))
    finish_grads(mixer_push, small_sum)

    small_out = dict(zip(SMALL, zip([g_small[n] for n in SMALL], d_small, m_small, v_small)))

    def out(k):
        return [small_out[n][k] if n in small_out else big_out[n][k] for n in WEIGHTS]

    return (loss, grad_x, *out(0), *out(1), *out(2), *out(3))
```
